```python
import math
import jax, jax.numpy as jnp
from jax import lax
import numpy as np

D_MODEL = 1024
BATCH = 2
SEQ = 16384
DEPTH = 2

N_EVEN = (DEPTH + 1) // 2
N_ODD = DEPTH // 2

ATTN_HEADS = 16
HEAD_DIM = 64
ATTN_DIM = ATTN_HEADS * HEAD_DIM
DILATED_PATTERNS = ((128, 1), (512, 4), (2048, 16))
ROPE_THETA = 500000.0
ROPE_DIM = HEAD_DIM // 4

SSD_HEADS = 16
SSD_HEAD_DIM = 64
SSD_DIM = SSD_HEADS * SSD_HEAD_DIM
SSD_GROUPS = 2
SSD_STATE = 128
SSD_CONV = 5
SSD_CHUNK = 128
SSD_XBC = SSD_DIM + 2 * SSD_GROUPS * SSD_STATE

MIX_DIM_EVEN = ATTN_DIM + SSD_DIM
EVEN_SPLIT = (ATTN_DIM, ATTN_DIM, ATTN_DIM, SSD_DIM, SSD_XBC, SSD_HEADS, SSD_HEADS)
IN_DIM_EVEN = sum(EVEN_SPLIT)

CONV_DIM = D_MODEL
SHORT_CONV = 3

N_EXPERTS = 16
CAPACITY_FACTOR = 2
D_FF_EXPERT = 2 * D_MODEL

NORM_EPS = 1e-6

kernel_name = "hybrid_dilated_attn_ssd_shortconv_ecmoe"


def rms_norm(x, g):
    xf = x.astype(jnp.float32)
    y = xf * lax.rsqrt(jnp.mean(xf * xf, axis=-1, keepdims=True) + NORM_EPS)
    return (y * g.astype(jnp.float32)).astype(x.dtype)


def partial_rope(t, positions):
    half = ROPE_DIM // 2
    inv_freq = ROPE_THETA ** (-jnp.arange(half, dtype=jnp.float32) * 2.0 / ROPE_DIM)
    ang = positions[:, None] * inv_freq[None, :]
    cos = jnp.cos(ang)[:, None, :]
    sin = jnp.sin(ang)[:, None, :]
    tf = t.astype(jnp.float32)
    x1, x2, rest = tf[..., :half], tf[..., half:ROPE_DIM], tf[..., ROPE_DIM:]
    return jnp.concatenate([x1 * cos - x2 * sin, x2 * cos + x1 * sin, rest], axis=-1).astype(t.dtype)


def window_branch(q, k, v, dilation, half):
    b, s, h, e = q.shape
    n = s // dilation
    nb = -(-n // half)
    n_pad = nb * half

    def to_blocks(t):
        t = t.reshape(b, n, dilation, h, e).transpose(0, 2, 3, 1, 4)
        t = jnp.pad(t, ((0, 0), (0, 0), (0, 0), (0, n_pad - n), (0, 0)))
        return t.reshape(b, dilation, h, nb, half, e)

    def neighbours(t):
        tp = jnp.pad(t, ((0, 0), (0, 0), (0, 0), (1, 1), (0, 0), (0, 0)))
        return jnp.concatenate([tp[:, :, :, :-2], tp[:, :, :, 1:-1], tp[:, :, :, 2:]], axis=4)

    qb = to_blocks(q)
    kc = neighbours(to_blocks(k))
    vc = neighbours(to_blocks(v))
    blk = jnp.arange(nb)[:, None] * half
    qpos = blk + jnp.arange(half)[None, :]
    kpos = blk + jnp.arange(-half, 2 * half)[None, :]
    mask = ((jnp.abs(qpos[:, :, None] - kpos[:, None, :]) <= half)
            & (kpos >= 0)[:, None, :] & (kpos < n)[:, None, :])
    scores = jnp.einsum("bdhnqe,bdhnke->bdhnqk", qb, kc).astype(jnp.float32) * (e ** -0.5)
    scores = jnp.where(mask, scores, -jnp.inf)
    m = jnp.max(scores, axis=-1, keepdims=True)
    p = jnp.exp(scores - m)
    l = jnp.sum(p, axis=-1, keepdims=True)
    o = jnp.einsum("bdhnqk,bdhnke->bdhnqe", p, vc.astype(jnp.float32)) / l
    lse = m + jnp.log(l)

    def back(t):
        t = t.reshape(b, dilation, h, n_pad, -1)[:, :, :, :n]
        return t.transpose(0, 3, 1, 2, 4).reshape(b, s, h, -1)

    return back(o), back(lse)[..., 0]


def dilated_attention(q, k, v):
    outs, lses = zip(*[window_branch(q, k, v, d, w // (2 * d)) for (w, d) in DILATED_PATTERNS])
    wts = jax.nn.softmax(jnp.stack(lses, axis=0), axis=0)
    return jnp.sum(jnp.stack(outs, axis=0) * wts[..., None], axis=0)


def dwconv_centred(x, w):
    ksz = w.shape[0]
    pad = ksz // 2
    s = x.shape[1]
    xp = jnp.pad(x, ((0, 0), (pad, pad), (0, 0)))
    y = xp[:, 0:s] * w[0]
    for j in range(1, ksz):
        y = y + xp[:, j:j + s] * w[j]
    return y


def segsum(a):
    cs = jnp.cumsum(a, axis=-1)
    t = a.shape[-1]
    diff = cs[..., :, None] - cs[..., None, :]
    return jnp.where(jnp.tril(jnp.ones((t, t), dtype=bool)), diff, -jnp.inf)


def ssd_scan(x, dt, a_neg, bm, cm):
    b, l, h, p = x.shape
    g, n = bm.shape[2], bm.shape[3]
    r = h // g
    c = l // SSD_CHUNK
    xdt = (x * dt[..., None]).reshape(b, c, SSD_CHUNK, g, r, p)
    a = (dt * a_neg).reshape(b, c, SSD_CHUNK, g, r).transpose(0, 3, 4, 1, 2)
    bm = bm.reshape(b, c, SSD_CHUNK, g, n)
    cm = cm.reshape(b, c, SSD_CHUNK, g, n)
    a_cs = jnp.cumsum(a, axis=-1)
    decay_in = jnp.exp(segsum(a))
    y_diag = jnp.einsum("bclgn,bcsgn,bgrcls,bcsgrp->bclgrp", cm, bm, decay_in, xdt)
    decay_to_end = jnp.exp(a_cs[..., -1:] - a_cs)
    chunk_states = jnp.einsum("bclgn,bgrcl,bclgrp->bcgrpn", bm, decay_to_end, xdt)
    chunk_decay = jnp.exp(a_cs[..., -1])

    def step(state, inp):
        st, dec = inp
        return state * dec[..., None, None] + st, state

    _, prev = lax.scan(step, jnp.zeros((b, g, r, p, n), x.dtype),
                       (chunk_states.transpose(1, 0, 2, 3, 4, 5), chunk_decay.transpose(3, 0, 1, 2)))
    prev = prev.transpose(1, 0, 2, 3, 4, 5)
    y_off = jnp.einsum("bclgn,bcgrpn,bgrcl->bclgrp", cm, prev, jnp.exp(a_cs))
    return (y_diag + y_off).reshape(b, l, h, p)


def bidirectional_ssd(xs, bm, cm, dt_f, dt_b, a_log_f, a_log_b, dt_bias_f, dt_bias_b, d_skip):
    f32 = jnp.float32
    step_f = jax.nn.softplus(dt_f + dt_bias_f.astype(f32))
    step_b = jax.nn.softplus(dt_b + dt_bias_b.astype(f32))
    y_f = ssd_scan(xs, step_f, -jnp.exp(a_log_f.astype(f32)), bm, cm)
    flip = lambda t: jnp.flip(t, axis=1)
    y_b = flip(ssd_scan(flip(xs), flip(step_b), -jnp.exp(a_log_b.astype(f32)), flip(bm), flip(cm)))
    return y_f + y_b + xs * d_skip.astype(f32)[:, None]


def even_mixer(h, w_in, q_norm, k_norm, conv_w, conv_b, a_log_f, a_log_b,
               dt_bias_f, dt_bias_b, d_skip, out_norm, w_out, positions):
    b, s, _ = h.shape
    f32 = jnp.float32
    idx = np.cumsum(EVEN_SPLIT)[:-1].tolist()
    q, k, v, z, xbc, dt_f, dt_b = jnp.split(h @ w_in, idx, axis=-1)
    heads = lambda t: t.reshape(b, s, ATTN_HEADS, HEAD_DIM)
    q = partial_rope(rms_norm(heads(q), q_norm), positions)
    k = partial_rope(rms_norm(heads(k), k_norm), positions)
    attn = dilated_attention(q, k, heads(v)).reshape(b, s, ATTN_DIM).astype(h.dtype)
    xbc = jax.nn.silu(dwconv_centred(xbc, conv_w) + conv_b)
    xs, bm, cm = jnp.split(xbc, [SSD_DIM, SSD_DIM + SSD_GROUPS * SSD_STATE], axis=-1)
    y = bidirectional_ssd(
        xs.astype(f32).reshape(b, s, SSD_HEADS, SSD_HEAD_DIM),
        bm.astype(f32).reshape(b, s, SSD_GROUPS, SSD_STATE),
        cm.astype(f32).reshape(b, s, SSD_GROUPS, SSD_STATE),
        dt_f.astype(f32), dt_b.astype(f32), a_log_f, a_log_b, dt_bias_f, dt_bias_b, d_skip)
    y = rms_norm(y.reshape(b, s, SSD_DIM) * jax.nn.silu(z.astype(f32)), out_norm).astype(h.dtype)
    return jnp.concatenate([attn, y], axis=-1) @ w_out


def odd_mixer(h, w_in, conv_w, w_out):
    gate_b, gate_c, u = jnp.split(h @ w_in, 3, axis=-1)
    return (gate_b * dwconv_centred(gate_c * u, conv_w)) @ w_out


def expert_choice_moe(h, router_w, w_gate, w_up, w_down):
    b, s, _ = h.shape
    cap = CAPACITY_FACTOR * s // N_EXPERTS
    affinity = jax.nn.softmax((h @ router_w).astype(jnp.float32), axis=-1)
    gate, idx = lax.top_k(jnp.swapaxes(affinity, 1, 2), cap)
    bidx = jnp.arange(b)[:, None, None]
    xe = h[bidx, idx]
    hid = jax.nn.silu(jnp.einsum("becd,edf->becf", xe, w_gate)) * jnp.einsum("becd,edf->becf", xe, w_up)
    ye = jnp.einsum("becf,efd->becd", hid, w_down) * gate[..., None].astype(h.dtype)
    return jnp.zeros_like(h).at[bidx, idx].add(ye)


def setup_inputs(seed: int = 0) -> dict:
    key = jax.random.key(seed)
    ks = jax.random.split(key, 24)
    f32 = jnp.float32
    nrm = lambda k, shape, fan_in: jax.random.normal(k, shape, f32) * (fan_in ** -0.5)
    gain = lambda k, shape: 1.0 + 0.05 * jax.random.normal(k, shape, f32)

    def dt_bias(k, shape):
        dt = jnp.exp(jax.random.uniform(k, shape, f32, minval=math.log(1e-3), maxval=math.log(1e-1)))
        return dt + jnp.log(-jnp.expm1(-dt))

    return {
        "x": jax.random.normal(ks[0], (BATCH, SEQ, D_MODEL), f32),
        "attn_norm": gain(ks[1], (N_EVEN, D_MODEL)),
        "w_in_even": nrm(ks[2], (N_EVEN, D_MODEL, IN_DIM_EVEN), D_MODEL),
        "q_norm": gain(ks[3], (N_EVEN, HEAD_DIM)),
        "k_norm": gain(ks[4], (N_EVEN, HEAD_DIM)),
        "ssd_conv_w": nrm(ks[5], (N_EVEN, SSD_CONV, SSD_XBC), SSD_CONV),
        "ssd_conv_b": 0.02 * jax.random.normal(ks[6], (N_EVEN, SSD_XBC), f32),
        "ssd_a_log_fwd": jnp.log(jax.random.uniform(ks[7], (N_EVEN, SSD_HEADS), f32, minval=1.0, maxval=16.0)),
        "ssd_a_log_bwd": jnp.log(jax.random.uniform(ks[8], (N_EVEN, SSD_HEADS), f32, minval=1.0, maxval=16.0)),
        "ssd_dt_bias_fwd": dt_bias(ks[9], (N_EVEN, SSD_HEADS)),
        "ssd_dt_bias_bwd": dt_bias(ks[10], (N_EVEN, SSD_HEADS)),
        "ssd_d": gain(ks[11], (N_EVEN, SSD_HEADS)),
        "ssd_out_norm": gain(ks[12], (N_EVEN, SSD_DIM)),
        "w_out_even": nrm(ks[13], (N_EVEN, MIX_DIM_EVEN, D_MODEL), MIX_DIM_EVEN),
        "conv_norm": gain(ks[14], (N_ODD, D_MODEL)),
        "conv_w_in": nrm(ks[15], (N_ODD, D_MODEL, 3 * CONV_DIM), D_MODEL),
        "conv_w": nrm(ks[16], (N_ODD, SHORT_CONV, CONV_DIM), SHORT_CONV),
        "conv_w_out": nrm(ks[17], (N_ODD, CONV_DIM, D_MODEL), CONV_DIM),
        "ffn_norm": gain(ks[18], (DEPTH, D_MODEL)),
        "router_w": nrm(ks[19], (DEPTH, D_MODEL, N_EXPERTS), D_MODEL),
        "expert_w_gate": nrm(ks[20], (DEPTH, N_EXPERTS, D_MODEL, D_FF_EXPERT), D_MODEL),
        "expert_w_up": nrm(ks[21], (DEPTH, N_EXPERTS, D_MODEL, D_FF_EXPERT), D_MODEL),
        "expert_w_down": nrm(ks[22], (DEPTH, N_EXPERTS, D_FF_EXPERT, D_MODEL), D_FF_EXPERT),
    }


def reference(x, attn_norm, w_in_even, q_norm, k_norm, ssd_conv_w, ssd_conv_b,
              ssd_a_log_fwd, ssd_a_log_bwd, ssd_dt_bias_fwd, ssd_dt_bias_bwd, ssd_d,
              ssd_out_norm, w_out_even, conv_norm, conv_w_in, conv_w, conv_w_out,
              ffn_norm, router_w, expert_w_gate, expert_w_up, expert_w_down):
    positions = jnp.arange(x.shape[1], dtype=jnp.float32)
    for layer in range(DEPTH):
        i = layer // 2
        if layer % 2 == 0:
            x = x + even_mixer(rms_norm(x, attn_norm[i]), w_in_even[i], q_norm[i], k_norm[i],
                               ssd_conv_w[i], ssd_conv_b[i], ssd_a_log_fwd[i], ssd_a_log_bwd[i],
                               ssd_dt_bias_fwd[i], ssd_dt_bias_bwd[i], ssd_d[i], ssd_out_norm[i],
                               w_out_even[i], positions)
        else:
            x = x + odd_mixer(rms_norm(x, conv_norm[i]), conv_w_in[i], conv_w[i], conv_w_out[i])
        x = x + expert_choice_moe(rms_norm(x, ffn_norm[layer]), router_w[layer],
                                  expert_w_gate[layer], expert_w_up[layer], expert_w_down[layer])
    return x
```

```python
import functools
import math

import jax
import jax.numpy as jnp
import numpy as np
from jax import lax
from jax.experimental import pallas as pl
from jax.experimental.pallas import tpu as pltpu

F32, BF16, I32 = jnp.float32, jnp.bfloat16, jnp.int32

D_MODEL = 1024
N_HEADS = 16
HEAD_DIM = 64
ROPE_HALF = 8
ROPE_THETA = 500000.0
PATTERNS = ((128, 1), (512, 4), (2048, 16))
HALF_STEPS = 64
SSD_GROUPS = 2
SSD_STATE = 128
SSD_XBC = 1536
SSD_CONV = 5
CHUNK = 128
N_EXPERTS = 16
CAPACITY_FACTOR = 2
D_FF = 2048
SHORT_CONV = 3
EPS = 1e-6

LANE = 128
VMEM_LIMIT = 56 * 1024 * 1024

ROW_TILE = 256
ATTN_SUPER = 2048
ATTN_HALO = 1024
ATTN_TQ = 128
ATTN_TK = ATTN_TQ + 2 * HALF_STEPS
TOK_TILE = 256
SLOT_BLK = 128
FFN_ROWS = 512


def _cparams(n_axes):
    return pltpu.CompilerParams(dimension_semantics=("arbitrary",) * n_axes, vmem_limit_bytes=VMEM_LIMIT)


def _const_spec(shape):
    nd = len(shape)
    return pl.BlockSpec(shape, lambda *_: (0,) * nd, pipeline_mode=pl.Buffered(1))


def _dot(a, b):
    return jnp.dot(a, b, preferred_element_type=F32)


def _dot_nt(a, b):
    return lax.dot_general(a, b, (((1,), (1,)), ((), ())), preferred_element_type=F32)


def _dot_tn(a, b):
    return lax.dot_general(a, b, (((0,), (0,)), ((), ())), preferred_element_type=F32)


def _split2(x):
    hi = x.astype(BF16)
    lo = (x - hi.astype(F32)).astype(BF16)
    return hi, lo


def _split3(x):
    hi = x.astype(BF16)
    r = x - hi.astype(F32)
    mid = r.astype(BF16)
    lo = (r - mid.astype(F32)).astype(BF16)
    return hi, mid, lo


def _dot3(x, m_bf16):
    hi, mid, lo = _split3(x)
    return _dot(hi, m_bf16) + _dot(mid, m_bf16) + _dot(lo, m_bf16)


def _rms(x, g):
    return x * lax.rsqrt(jnp.mean(x * x, axis=-1, keepdims=True) + EPS) * g


def _silu(x):
    return x * jax.nn.sigmoid(x)


def _in_even_kernel(x_ref, g_ref, wq_ref, wk_ref, wv_ref, wz_ref, wx_ref, wd_ref, qg_ref, kg_ref, bd_ref,
                    rc_ref, r1_ref, r2_ref, q_ref, k_ref, v_ref, z_ref, xbc_ref, dt_ref):
    hb = _rms(x_ref[...], g_ref[...]).astype(BF16)
    bd = bd_ref[...]
    rc, r1, r2 = rc_ref[...], r1_ref[...], r2_ref[...]

    def head_norm_rope(w_ref, gain_ref, out_ref):
        t = _dot(hb, w_ref[...])
        for c in range(D_MODEL // 256):
            tc = t[:, c * 256:(c + 1) * 256]
            sq_hi, sq_lo = _split2(tc * tc)
            ms = _dot(sq_hi, bd) + _dot(sq_lo, bd)
            tn = tc * lax.rsqrt(ms + EPS) * gain_ref[:, c * 256:(c + 1) * 256]
            for hh in range(2):
                u = tn[:, hh * LANE:(hh + 1) * LANE]
                r = u * rc + pltpu.roll(u, LANE - ROPE_HALF, 1) * r1 + pltpu.roll(u, ROPE_HALF, 1) * r2
                out_ref[:, c * 256 + hh * LANE:c * 256 + (hh + 1) * LANE] = r.astype(BF16)

    head_norm_rope(wq_ref, qg_ref, q_ref)
    head_norm_rope(wk_ref, kg_ref, k_ref)
    v_ref[...] = _dot(hb, wv_ref[...]).astype(BF16)
    z_ref[...] = _dot(hb, wz_ref[...])
    xbc_ref[...] = _dot(hb, wx_ref[...])
    dt_ref[...] = _dot(hb, wd_ref[...])


def _in_even(x, g, wq, wk, wv, wz, wx, wd, qg, kg, bd, rc, r1, r2, seq):
    n = x.shape[0]
    tm = ROW_TILE
    row = lambda w: pl.BlockSpec((tm, w), lambda i: (i, 0))
    tab = pl.BlockSpec((tm, LANE), lambda i: (i % (seq // tm), 0))
    outs = [jax.ShapeDtypeStruct((n, D_MODEL), BF16)] * 3 + [jax.ShapeDtypeStruct((n, D_MODEL), F32),
                                                            jax.ShapeDtypeStruct((n, SSD_XBC), F32),
                                                            jax.ShapeDtypeStruct((n, LANE), F32)]
    return pl.pallas_call(
        _in_even_kernel, grid=(n // tm,),
        in_specs=[row(D_MODEL), _const_spec((1, D_MODEL)), _const_spec(wq.shape), _const_spec(wk.shape),
                  _const_spec(wv.shape), _const_spec(wz.shape), _const_spec(wx.shape), _const_spec(wd.shape),
                  _const_spec((1, D_MODEL)), _const_spec((1, D_MODEL)), _const_spec((256, 256)), tab, tab, tab],
        out_specs=[row(D_MODEL)] * 4 + [row(SSD_XBC), row(LANE)], out_shape=outs,
        compiler_params=_cparams(1), name="in_even",
    )(x, g, wq, wk, wv, wz, wx, wd, qg, kg, bd, rc, r1, r2)


def _attn_kernel(q_ref, k0, k1, k2, k3, v0, v1, v2, v3, o_ref, qf, kf, vf, acc, mst, lst, *, seq):
    p0 = pl.program_id(2) * ATTN_SUPER
    qf[...] = q_ref[0].astype(F32) * (HEAD_DIM ** -0.5)
    for i, (kr, vr) in enumerate(((k0, v0), (k1, v1), (k2, v2), (k3, v3))):
        kf[i * ATTN_HALO:(i + 1) * ATTN_HALO, :] = kr[0].astype(F32)
        vf[i * ATTN_HALO:(i + 1) * ATTN_HALO, :] = vr[0].astype(F32)

    head_a = lax.broadcasted_iota(I32, (ATTN_TQ, LANE), 1) < HEAD_DIM
    off = lax.broadcasted_iota(I32, (ATTN_TQ, ATTN_TK), 1) - lax.broadcasted_iota(I32, (ATTN_TQ, ATTN_TK), 0)
    band = (off >= 0) & (off <= 2 * HALF_STEPS)
    colpos = lax.broadcasted_iota(I32, (1, ATTN_TK), 1)

    def visit(d, i, first, last):
        log_d = d.bit_length() - 1
        r = i & (d - 1)
        t = i >> log_d
        qs = r + d * ATTN_TQ * t
        ks = ATTN_HALO + r + d * (ATTN_TQ * t - HALF_STEPS)
        q = qf[pl.ds(qs, ATTN_TQ, stride=d), :]
        kt = kf[pl.ds(ks, ATTN_TK, stride=d), :].astype(BF16)
        vt = vf[pl.ds(ks, ATTN_TK, stride=d), :].astype(BF16)
        pos = p0 - ATTN_HALO + ks + d * colpos
        valid = band & (pos >= 0) & (pos < seq)

        def one_head(qh):
            s = jnp.where(valid, _dot_nt(qh.astype(BF16), kt), -jnp.inf)
            m = jnp.max(s, axis=1, keepdims=True)
            p = jnp.exp(s - m)
            return m, jnp.sum(p, axis=1, keepdims=True), _dot(p.astype(BF16), vt)

        ma, la, na = one_head(jnp.where(head_a, q, 0.0))
        mb, lb, nb = one_head(jnp.where(head_a, 0.0, q))
        m_loc = jnp.where(head_a, ma, mb)
        l_loc = jnp.where(head_a, la, lb)
        n_loc = jnp.where(head_a, na, nb)
        rows = pl.ds(qs, ATTN_TQ, stride=d)
        if first:
            m_new, l_new, a_new = m_loc, l_loc, n_loc
        else:
            m_old = mst[rows, :]
            m_new = jnp.maximum(m_old, m_loc)
            w_old = jnp.exp(m_old - m_new)
            w_loc = jnp.exp(m_loc - m_new)
            l_new = lst[rows, :] * w_old + l_loc * w_loc
            a_new = acc[rows, :] * w_old + n_loc * w_loc
        if last:
            acc[rows, :] = a_new / l_new
        else:
            mst[rows, :] = m_new
            lst[rows, :] = l_new
            acc[rows, :] = a_new

    n_visits = ATTN_SUPER // ATTN_TQ
    for idx, (_, d) in enumerate(PATTERNS):
        first, last = idx == 0, idx == len(PATTERNS) - 1

        def body(i, c, d=d, first=first, last=last):
            visit(d, i, first, last)
            return c

        lax.fori_loop(0, n_visits, body, 0)
    o_ref[0] = acc[...].astype(BF16)


def _attention(q, k, v):
    b, seq, _ = q.shape
    nblk = seq // ATTN_HALO
    ratio = ATTN_SUPER // ATTN_HALO

    def halo(i):
        return pl.BlockSpec((1, ATTN_HALO, LANE),
                            lambda bi, hp, j: (bi, jnp.clip(ratio * j - 1 + i, 0, nblk - 1), hp))

    main = pl.BlockSpec((1, ATTN_SUPER, LANE), lambda bi, hp, j: (bi, j, hp))
    return pl.pallas_call(
        functools.partial(_attn_kernel, seq=seq),
        grid=(b, D_MODEL // LANE, seq // ATTN_SUPER),
        in_specs=[main] + [halo(i) for i in range(4)] * 2,
        out_specs=main,
        out_shape=jax.ShapeDtypeStruct((b, seq, D_MODEL), BF16),
        scratch_shapes=[pltpu.VMEM((ATTN_SUPER, LANE), F32),
                        pltpu.VMEM((4 * ATTN_HALO, LANE), F32), pltpu.VMEM((4 * ATTN_HALO, LANE), F32),
                        pltpu.VMEM((ATTN_SUPER, LANE), F32), pltpu.VMEM((ATTN_SUPER, LANE), F32),
                        pltpu.VMEM((ATTN_SUPER, LANE), F32)],
        compiler_params=_cparams(3), name="dilated_attn",
    )(q, k, k, k, k, v, v, v, v)


def _shifted(cur, prev8, next8, s, first, last):
    n = cur.shape[0]
    if s == 0:
        return cur
    rows = lax.broadcasted_iota(I32, cur.shape, 0)
    out = pltpu.roll(cur, (-s) % n, 0)
    if s < 0:
        for j in range(-s):
            src = jnp.where(first, 0.0, prev8[8 + s + j:8 + s + j + 1, :])
            out = jnp.where(rows == j, src, out)
    else:
        for j in range(s):
            src = jnp.where(last, 0.0, next8[j:j + 1, :])
            out = jnp.where(rows == n - s + j, src, out)
    return out


def _ssd_conv_kernel(c_ref, p_ref, n_ref, w_ref, b_ref, o_ref):
    first = pl.program_id(1) == 0
    last = pl.program_id(1) == pl.num_programs(1) - 1
    for c in range(SSD_XBC // 256):
        sl = slice(c * 256, (c + 1) * 256)
        cur, prev8, next8 = c_ref[0, :, sl], p_ref[0, :, sl], n_ref[0, :, sl]
        y = b_ref[:, sl]
        for j in range(SSD_CONV):
            y = y + _shifted(cur, prev8, next8, j - SSD_CONV // 2, first, last) * w_ref[j:j + 1, sl]
        o_ref[0, :, sl] = _silu(y)


def _halo_specs(tm, width, seq):
    cur = pl.BlockSpec((1, tm, width), lambda b, i: (b, i, 0))
    prev = pl.BlockSpec((1, 8, width), lambda b, i: (b, jnp.maximum(i * (tm // 8) - 1, 0), 0))
    nxt = pl.BlockSpec((1, 8, width), lambda b, i: (b, jnp.minimum((i + 1) * (tm // 8), seq // 8 - 1), 0))
    return cur, prev, nxt


def _ssd_conv(xbc, w, bias):
    b, seq, width = xbc.shape
    tm = 512
    cur, prev, nxt = _halo_specs(tm, width, seq)
    return pl.pallas_call(
        _ssd_conv_kernel, grid=(b, seq // tm),
        in_specs=[cur, prev, nxt, pl.BlockSpec((8, width), lambda b, i: (0, 0)),
                  pl.BlockSpec((1, width), lambda b, i: (0, 0))],
        out_specs=cur, out_shape=jax.ShapeDtypeStruct(xbc.shape, F32),
        compiler_params=_cparams(2), name="ssd_conv",
    )(xbc, xbc, xbc, w, bias)


def _softplus(x):
    return jnp.maximum(x, 0.0) + jnp.log1p(jnp.exp(-jnp.abs(x)))


def _tri(kind):
    s = lax.broadcasted_iota(I32, (CHUNK, CHUNK), 0)
    l = lax.broadcasted_iota(I32, (CHUNK, CHUNK), 1)
    return {"le": s <= l, "ge": s >= l, "lt": s < l}[kind]


def _expand(cols, e2_ref):
    hi, lo = _split2(cols)
    return _dot(jnp.concatenate([hi, lo], axis=1), e2_ref[...])


def _ssd_bwd_kernel(xs_ref, b_ref, dt_ref, pc_ref, e2_ref, sb_ref, st):
    @pl.when(pl.program_id(1) == 0)
    def _():
        st[...] = jnp.zeros_like(st)

    sb_ref[0, 0] = st[...].astype(BF16)
    dt_t = dt_ref[0].T
    dtb = _softplus(dt_t[N_HEADS:2 * N_HEADS, :] + pc_ref[:, 1:2])
    a = dtb * pc_ref[:, 3:4]
    ex = _dot3(a, _tri("lt").astype(BF16))
    tot = ex[:, CHUNK - 1:CHUNK] + a[:, CHUNK - 1:CHUNK]
    rowform = jnp.concatenate([dtb * jnp.exp(ex), jnp.broadcast_to(jnp.exp(tot), (N_HEADS, CHUNK)),
                               jnp.zeros((CHUNK - 2 * N_HEADS, CHUNK), F32)], axis=0)
    ex2 = _expand(rowform.T, e2_ref)
    xw = (xs_ref[0] * ex2[:, :D_MODEL]).astype(BF16)
    half = D_MODEL // SSD_GROUPS
    upd = [_dot(b_ref[0, :, g * SSD_STATE:(g + 1) * SSD_STATE].T.astype(BF16), xw[:, g * half:(g + 1) * half])
           for g in range(SSD_GROUPS)]
    st[...] = st[...] * ex2[0:1, D_MODEL:] + jnp.concatenate(upd, axis=1)


def _ssd_fwd_kernel(xs_ref, b_ref, c_ref, dt_ref, z_ref, sb_ref, pc_ref, e3_ref, dexp_ref, on_ref, o_ref, st):
    @pl.when(pl.program_id(1) == 0)
    def _():
        st[...] = jnp.zeros_like(st)

    xs = xs_ref[0]
    dt_t = dt_ref[0].T
    dtf = _softplus(dt_t[0:N_HEADS, :] + pc_ref[:, 0:1])
    dtb = _softplus(dt_t[N_HEADS:2 * N_HEADS, :] + pc_ref[:, 1:2])
    af = dtf * pc_ref[:, 2:3]
    ab = dtb * pc_ref[:, 3:4]
    csf = _dot3(af, _tri("le").astype(BF16))
    rcs = _dot3(ab, _tri("ge").astype(BF16))
    totf = csf[:, CHUNK - 1:CHUNK]
    rowform = jnp.concatenate([dtf * jnp.exp(totf - csf), jnp.exp(csf), jnp.exp(rcs), csf, rcs,
                               jnp.zeros((CHUNK - 5 * N_HEADS, CHUNK), F32)], axis=0)
    cols = rowform.T
    ex3 = _expand(cols, e3_ref)
    w_state, e_f, e_b = ex3[:, :D_MODEL], ex3[:, D_MODEL:2 * D_MODEL], ex3[:, 2 * D_MODEL:]

    xb = xs.astype(BF16)
    lower, upper = _tri("ge"), _tri("le")
    head_a = lax.broadcasted_iota(I32, (CHUNK, LANE), 1) < HEAD_DIM
    half = D_MODEL // SSD_GROUPS
    hpg = N_HEADS // SSD_GROUPS
    st_all = st[...]
    sb_all = sb_ref[0, 0]
    ys = []
    b_t = []
    for g in range(SSD_GROUPS):
        bg = b_ref[0, :, g * SSD_STATE:(g + 1) * SSD_STATE]
        cg = c_ref[0, :, g * SSD_STATE:(g + 1) * SSD_STATE].astype(BF16)
        b_t.append(bg.T.astype(BF16))
        gm = _dot_nt(cg, bg.astype(BF16))
        states = jnp.concatenate([st_all[:, g * half:(g + 1) * half].astype(BF16),
                                  sb_all[:, g * half:(g + 1) * half]], axis=1)
        off = _dot(cg, states)
        y_off = (off[:, :half] * e_f[:, g * half:(g + 1) * half]
                 + off[:, half:] * e_b[:, g * half:(g + 1) * half])
        for pair in range(hpg // 2):
            ms = []
            for h in (g * hpg + 2 * pair, g * hpg + 2 * pair + 1):
                dec_f = jnp.where(lower, jnp.exp(cols[:, 3 * N_HEADS + h:3 * N_HEADS + h + 1] - csf[h:h + 1, :]), 0.0)
                dec_b = jnp.where(upper, jnp.exp(cols[:, 4 * N_HEADS + h:4 * N_HEADS + h + 1] - rcs[h:h + 1, :]), 0.0)
                ms.append((gm * (dec_f * dtf[h:h + 1, :] + dec_b * dtb[h:h + 1, :])).astype(BF16))
            lo = g * half + pair * LANE
            xp = xb[:, lo:lo + LANE]
            ys.append(jnp.where(head_a, _dot(ms[0], xp), _dot(ms[1], xp))
                      + y_off[:, pair * LANE:(pair + 1) * LANE])
    y = jnp.concatenate(ys, axis=1) + xs * dexp_ref[...]
    yz = y * _silu(z_ref[0])
    o_ref[0] = _rms(yz, on_ref[...]).astype(BF16)

    xw = (xs * w_state).astype(BF16)
    upd = [_dot(b_t[g], xw[:, g * half:(g + 1) * half]) for g in range(SSD_GROUPS)]
    st[...] = st_all * e_f[CHUNK - 1:CHUNK, :] + jnp.concatenate(upd, axis=1)


def _ssd(xbc_act, dt, z, pc, dexp, out_norm):
    b, seq, _ = xbc_act.shape
    nc = seq // CHUNK
    sel = np.zeros((2 * CHUNK, 3 * D_MODEL), np.float32)
    for part in range(3):
        for h in range(N_HEADS):
            for rep in range(2):
                sel[rep * CHUNK + part * N_HEADS + h, part * D_MODEL + h * HEAD_DIM:part * D_MODEL + (h + 1) * HEAD_DIM] = 1.0
    e3 = jnp.asarray(sel, BF16)
    e2 = jnp.asarray(sel[:, :2 * D_MODEL], BF16)

    rev = lambda bi, c: (bi, nc - 1 - c, 0)
    sb = pl.pallas_call(
        _ssd_bwd_kernel, grid=(b, nc),
        in_specs=[pl.BlockSpec((1, CHUNK, D_MODEL), rev),
                  pl.BlockSpec((1, CHUNK, 2 * SSD_STATE), lambda bi, c: (bi, nc - 1 - c, D_MODEL // (2 * SSD_STATE))),
                  pl.BlockSpec((1, CHUNK, LANE), rev), _const_spec(pc.shape), _const_spec(e2.shape)],
        out_specs=pl.BlockSpec((1, 1, SSD_STATE, D_MODEL), lambda bi, c: (bi, nc - 1 - c, 0, 0)),
        out_shape=jax.ShapeDtypeStruct((b, nc, SSD_STATE, D_MODEL), BF16),
        scratch_shapes=[pltpu.VMEM((SSD_STATE, D_MODEL), F32)],
        compiler_params=_cparams(2), name="ssd_bwd_state",
    )(xbc_act, xbc_act, dt, pc, e2)

    fwd = lambda bi, c: (bi, c, 0)
    return pl.pallas_call(
        _ssd_fwd_kernel, grid=(b, nc),
        in_specs=[pl.BlockSpec((1, CHUNK, D_MODEL), fwd),
                  pl.BlockSpec((1, CHUNK, 2 * SSD_STATE), lambda bi, c: (bi, c, D_MODEL // (2 * SSD_STATE))),
                  pl.BlockSpec((1, CHUNK, 2 * SSD_STATE), lambda bi, c: (bi, c, D_MODEL // (2 * SSD_STATE) + 1)),
                  pl.BlockSpec((1, CHUNK, LANE), fwd), pl.BlockSpec((1, CHUNK, D_MODEL), fwd),
                  pl.BlockSpec((1, 1, SSD_STATE, D_MODEL), lambda bi, c: (bi, c, 0, 0)),
                  _const_spec(pc.shape), _const_spec(e3.shape), _const_spec((1, D_MODEL)), _const_spec((1, D_MODEL))],
        out_specs=pl.BlockSpec((1, CHUNK, D_MODEL), fwd),
        out_shape=jax.ShapeDtypeStruct((b, seq, D_MODEL), BF16),
        scratch_shapes=[pltpu.VMEM((SSD_STATE, D_MODEL), F32)],
        compiler_params=_cparams(2), name="ssd_fwd",
    )(xbc_act, xbc_act, xbc_act, dt, z, sb, pc, e3, dexp, out_norm)


def _norm_and_route(x1, fg_ref, wr_hi_ref, wr_lo_ref, hn_ref, aff_ref):
    hn = _rms(x1, fg_ref[...])
    hi, lo = _split2(hn)
    hn_ref[...] = hi
    logits = _dot(hi, wr_hi_ref[...]) + _dot(lo, wr_hi_ref[...]) + _dot(hi, wr_lo_ref[...])
    lt = logits.T[0:N_EXPERTS, :]
    e = jnp.exp(lt - jnp.max(lt, axis=0, keepdims=True))
    aff_ref[0] = e / jnp.sum(e, axis=0, keepdims=True)


def _out_even_kernel(x_ref, a_ref, y_ref, wa_ref, wy_ref, fg_ref, wr_hi_ref, wr_lo_ref, x1_ref, hn_ref, aff_ref):
    x1 = x_ref[...] + _dot(a_ref[...], wa_ref[...]) + _dot(y_ref[...], wy_ref[...])
    x1_ref[...] = x1
    _norm_and_route(x1, fg_ref, wr_hi_ref, wr_lo_ref, hn_ref, aff_ref)


def _out_odd_kernel(x_ref, gb_ref, cu_ref, cp_ref, cn_ref, cw_ref, wo_ref, fg_ref, wr_hi_ref, wr_lo_ref,
                    x1_ref, hn_ref, aff_ref, *, tiles_per_seq):
    i = pl.program_id(0) % tiles_per_seq
    first, last = i == 0, i == tiles_per_seq - 1
    cur, prev8, next8 = cu_ref[...], cp_ref[...], cn_ref[...]
    conv = sum(_shifted(cur, prev8, next8, j - SHORT_CONV // 2, first, last) * cw_ref[j:j + 1, :]
               for j in range(SHORT_CONV))
    x1 = x_ref[...] + _dot((gb_ref[...] * conv).astype(BF16), wo_ref[...])
    x1_ref[...] = x1
    _norm_and_route(x1, fg_ref, wr_hi_ref, wr_lo_ref, hn_ref, aff_ref)


def _route_outs(n, b, seq, tm):
    row = pl.BlockSpec((tm, D_MODEL), lambda i: (i, 0))
    aff = pl.BlockSpec((1, N_EXPERTS, tm), lambda i: (i // (seq // tm), 0, i % (seq // tm)))
    shapes = [jax.ShapeDtypeStruct((n, D_MODEL), F32), jax.ShapeDtypeStruct((n, D_MODEL), BF16),
              jax.ShapeDtypeStruct((b, N_EXPERTS, seq), F32)]
    return [row, row, aff], shapes


def _out_even(x, attn, y, wa, wy, fg, wr_hi, wr_lo, b, seq):
    n = x.shape[0]
    tm = ROW_TILE
    row = pl.BlockSpec((tm, D_MODEL), lambda i: (i, 0))
    out_specs, shapes = _route_outs(n, b, seq, tm)
    return pl.pallas_call(
        _out_even_kernel, grid=(n // tm,),
        in_specs=[row, row, row, _const_spec(wa.shape), _const_spec(wy.shape), _const_spec((1, D_MODEL)),
                  _const_spec(wr_hi.shape), _const_spec(wr_lo.shape)],
        out_specs=out_specs, out_shape=shapes, compiler_params=_cparams(1), name="out_even",
    )(x, attn, y, wa, wy, fg, wr_hi, wr_lo)


def _out_odd(x, gb, cu, cw, wo, fg, wr_hi, wr_lo, b, seq):
    n = x.shape[0]
    tm = ROW_TILE
    row = pl.BlockSpec((tm, D_MODEL), lambda i: (i, 0))
    prev = pl.BlockSpec((8, D_MODEL), lambda i: (jnp.maximum(i * (tm // 8) - 1, 0), 0))
    nxt = pl.BlockSpec((8, D_MODEL), lambda i: (jnp.minimum((i + 1) * (tm // 8), n // 8 - 1), 0))
    out_specs, shapes = _route_outs(n, b, seq, tm)
    return pl.pallas_call(
        functools.partial(_out_odd_kernel, tiles_per_seq=seq // tm), grid=(n // tm,),
        in_specs=[row, row, row, prev, nxt, pl.BlockSpec((8, D_MODEL), lambda i: (0, 0)), _const_spec(wo.shape),
                  _const_spec((1, D_MODEL)), _const_spec(wr_hi.shape), _const_spec(wr_lo.shape)],
        out_specs=out_specs, out_shape=shapes, compiler_params=_cparams(1), name="out_odd",
    )(x, gb, cu, cu, cu, cw, wo, fg, wr_hi, wr_lo)


def _in_odd_kernel(x_ref, g_ref, wb_ref, wc_ref, wu_ref, gb_ref, cu_ref):
    hb = _rms(x_ref[...], g_ref[...]).astype(BF16)
    gb_ref[...] = _dot(hb, wb_ref[...])
    cu_ref[...] = _dot(hb, wc_ref[...]) * _dot(hb, wu_ref[...])


def _in_odd(x, g, wb, wc, wu):
    n = x.shape[0]
    tm = ROW_TILE
    row = pl.BlockSpec((tm, D_MODEL), lambda i: (i, 0))
    return pl.pallas_call(
        _in_odd_kernel, grid=(n // tm,),
        in_specs=[row, _const_spec((1, D_MODEL)), _const_spec(wb.shape), _const_spec(wc.shape), _const_spec(wu.shape)],
        out_specs=[row, row], out_shape=[jax.ShapeDtypeStruct((n, D_MODEL), F32)] * 2,
        compiler_params=_cparams(1), name="in_odd",
    )(x, g, wb, wc, wu)


def _prefix(mask_f32, incl_ref, strict_ref, ones_ref):
    mb = mask_f32.astype(BF16)
    within = _dot(mb, incl_ref[...])
    totals = _dot(mb, ones_ref[...])
    before = _dot(strict_ref[...], totals.astype(BF16))
    return within + before, before


def _route_kernel(aff_ref, incl_ref, strict_ref, ones_ref, slot_ref, off_ref, *, cap):
    for e in range(N_EXPERTS):
        bits = pltpu.bitcast(aff_ref[0, e], I32)

        def step(i, thr):
            cand = thr | (jnp.int32(1) << (30 - i))
            cnt = jnp.sum(jnp.sum((bits >= cand).astype(F32), axis=0, keepdims=True), axis=1, keepdims=True)
            return jnp.where(cnt >= cap, cand, thr)

        thr = lax.fori_loop(0, 31, step, jnp.zeros((1, 1), I32))
        gt = bits > thr
        eq = bits == thr
        n_gt = jnp.sum(jnp.sum(gt.astype(F32), axis=0, keepdims=True), axis=1, keepdims=True)
        eq_rank, _ = _prefix(eq.astype(F32), incl_ref, strict_ref, ones_ref)
        sel = gt | (eq & (eq_rank <= cap - n_gt))
        rank, before = _prefix(sel.astype(F32), incl_ref, strict_ref, ones_ref)
        slot_ref[0, e] = jnp.where(sel, rank.astype(I32) - 1, -1)
        off_ref[0, e:e + 1, :] = before.T[0:1, :].astype(I32)


def _route(aff, cap):
    b, _, seq = aff.shape
    nt = seq // LANE
    tri = np.arange(LANE)
    incl = jnp.asarray(tri[:, None] <= tri[None, :], BF16)
    tt = np.arange(nt)
    strict = jnp.asarray(tt[None, :] < tt[:, None], BF16)
    ones = jnp.ones((LANE, LANE), BF16)
    return pl.pallas_call(
        functools.partial(_route_kernel, cap=cap), grid=(b,),
        in_specs=[pl.BlockSpec((1, N_EXPERTS, nt, LANE), lambda i: (i, 0, 0, 0)), _const_spec((LANE, LANE)),
                  _const_spec((nt, nt)), _const_spec((LANE, LANE))],
        out_specs=[pl.BlockSpec((1, N_EXPERTS, nt, LANE), lambda i: (i, 0, 0, 0)),
                   pl.BlockSpec((1, N_EXPERTS, nt), lambda i: (i, 0, 0))],
        out_shape=[jax.ShapeDtypeStruct((b, N_EXPERTS, nt, LANE), I32), jax.ShapeDtypeStruct((b, N_EXPERTS, nt), I32)],
        compiler_params=_cparams(1), name="route",
    )(aff.reshape(b, N_EXPERTS, nt, LANE), incl, strict, ones)


def _tile_range(off_ref, bi, e, lo, hi, n_tiles, per):
    def count(j, c):
        nxt = jnp.where(j + 1 < n_tiles, off_ref[bi, e, jnp.minimum((j + 1) * per, n_tiles * per - 1)], jnp.int32(2 ** 30))
        return (c[0] + (nxt <= lo).astype(I32), c[1] + (off_ref[bi, e, j * per] < hi).astype(I32))
    return lax.fori_loop(0, n_tiles, count, (jnp.int32(0), jnp.int32(0)))


def _one_hot(slot_rows, base, n_slots):
    want = lax.broadcasted_iota(I32, (n_slots, LANE), 0) + base
    return jnp.concatenate([(slot_rows[r:r + 1, :] == want).astype(BF16) for r in range(slot_rows.shape[0])], axis=1)


def _gather_kernel(off_ref, slot_ref, hn_ref, xe_ref, acc):
    bi, e = pl.program_id(0), pl.program_id(2)
    per = TOK_TILE // LANE
    n_tiles = hn_ref.shape[1] // TOK_TILE
    for sb in range(xe_ref.shape[2] // SLOT_BLK):
        base = sb * SLOT_BLK
        j0, j1 = _tile_range(off_ref, bi, e, base, base + SLOT_BLK, n_tiles, per)
        acc[...] = jnp.zeros_like(acc)

        def body(j, c):
            p = _one_hot(slot_ref[0, 0, pl.ds(j * per, per), :], base, SLOT_BLK)
            acc[...] += _dot(p, hn_ref[0, pl.ds(pl.multiple_of(j * TOK_TILE, TOK_TILE), TOK_TILE), :])
            return c

        lax.fori_loop(j0, j1, body, 0)
        xe_ref[0, 0, base:base + SLOT_BLK, :] = acc[...].astype(BF16)


def _gather(off, slot, hn, cap):
    b, seq, _ = hn.shape
    nt = seq // LANE
    half = D_MODEL // 2
    return pl.pallas_call(
        _gather_kernel,
        grid_spec=pltpu.PrefetchScalarGridSpec(
            num_scalar_prefetch=1, grid=(b, 2, N_EXPERTS),
            in_specs=[pl.BlockSpec((1, 1, nt, LANE), lambda bi, h, e, off: (bi, e, 0, 0)),
                      pl.BlockSpec((1, seq, half), lambda bi, h, e, off: (bi, 0, h))],
            out_specs=pl.BlockSpec((1, 1, cap, half), lambda bi, h, e, off: (bi, e, 0, h)),
            scratch_shapes=[pltpu.VMEM((SLOT_BLK, half), F32)]),
        out_shape=jax.ShapeDtypeStruct((b, N_EXPERTS, cap, D_MODEL), BF16),
        compiler_params=_cparams(3), name="moe_gather",
    )(off, slot, hn)


def _ffn_kernel(xe_ref, wg_ref, wu_ref, wdt_ref, yt_ref):
    xe = xe_ref[0, 0]
    hid = (_silu(_dot(xe, wg_ref[0])) * _dot(xe, wu_ref[0])).astype(BF16)
    yt_ref[0, 0] = _dot_nt(wdt_ref[0], hid).astype(BF16)


def _ffn(xe, wg, wu, wdt):
    b, ne, cap, _ = xe.shape
    return pl.pallas_call(
        _ffn_kernel, grid=(ne, b, cap // FFN_ROWS),
        in_specs=[pl.BlockSpec((1, 1, FFN_ROWS, D_MODEL), lambda e, bi, r: (bi, e, r, 0)),
                  pl.BlockSpec((1, D_MODEL, D_FF), lambda e, bi, r: (e, 0, 0)),
                  pl.BlockSpec((1, D_MODEL, D_FF), lambda e, bi, r: (e, 0, 0)),
                  pl.BlockSpec((1, D_MODEL, D_FF), lambda e, bi, r: (e, 0, 0))],
        out_specs=pl.BlockSpec((1, 1, D_MODEL, FFN_ROWS), lambda e, bi, r: (bi, e, 0, r)),
        out_shape=jax.ShapeDtypeStruct((b, ne, D_MODEL, cap), BF16),
        compiler_params=_cparams(3), name="moe_ffn",
    )(xe, wg, wu, wdt)


def _scatter_kernel(off_ref, slot_ref, aff_ref, yt_ref, x1_ref, o_ref, acc):
    bi, e = pl.program_id(0), pl.program_id(2)
    per = TOK_TILE // LANE
    n_tiles = acc.shape[0]

    @pl.when(e == 0)
    def _():
        acc[...] = jnp.zeros_like(acc)

    for sb in range(yt_ref.shape[3] // TOK_TILE):
        base = sb * TOK_TILE
        j0, j1 = _tile_range(off_ref, bi, e, base, base + TOK_TILE, n_tiles, per)
        yt = yt_ref[0, 0, :, base:base + TOK_TILE]

        def body(j, c):
            p = _one_hot(slot_ref[0, 0, pl.ds(j * per, per), :], base, TOK_TILE)
            gate = jnp.concatenate([aff_ref[0, 0, pl.ds(j * per + r, 1), :] for r in range(per)], axis=1)
            acc[j] += _dot(yt, p) * gate
            return c

        lax.fori_loop(j0, j1, body, 0)

    @pl.when(e == pl.num_programs(2) - 1)
    def _():
        def fin(j, c):
            rows = pl.ds(pl.multiple_of(j * TOK_TILE, TOK_TILE), TOK_TILE)
            o_ref[0, rows, :] = x1_ref[0, rows, :] + acc[j].T
            return c
        lax.fori_loop(0, n_tiles, fin, 0)


def _scatter(off, slot, aff, yt, x1):
    b, seq, _ = x1.shape
    nt = seq // LANE
    cw = LANE
    cap = yt.shape[3]
    col = pl.BlockSpec((1, seq, cw), lambda bi, q, e, off: (bi, 0, q))
    return pl.pallas_call(
        _scatter_kernel,
        grid_spec=pltpu.PrefetchScalarGridSpec(
            num_scalar_prefetch=1, grid=(b, D_MODEL // cw, N_EXPERTS),
            in_specs=[pl.BlockSpec((1, 1, nt, LANE), lambda bi, q, e, off: (bi, e, 0, 0)),
                      pl.BlockSpec((1, 1, nt, LANE), lambda bi, q, e, off: (bi, e, 0, 0)),
                      pl.BlockSpec((1, 1, cw, cap), lambda bi, q, e, off: (bi, e, q, 0)), col],
            out_specs=col,
            scratch_shapes=[pltpu.VMEM((seq // TOK_TILE, cw, TOK_TILE), F32)]),
        out_shape=jax.ShapeDtypeStruct((b, seq, D_MODEL), F32),
        compiler_params=_cparams(3), name="moe_scatter",
    )(off, slot, aff.reshape(b, N_EXPERTS, nt, LANE), yt, x1)


def _moe(x1, hn, aff, wg, wu, wdt):
    b, seq, _ = x1.shape
    cap = CAPACITY_FACTOR * seq // N_EXPERTS
    slot, off = _route(aff, cap)
    xe = _gather(off, slot, hn, cap)
    yt = _ffn(xe, wg, wu, wdt)
    return _scatter(off, slot, aff, yt, x1)


def _rope_tables(seq):
    inv_freq = ROPE_THETA ** (-jnp.arange(ROPE_HALF, dtype=F32) * 2.0 / (2 * ROPE_HALF))
    ang = jnp.arange(seq, dtype=F32)[:, None] * inv_freq[None, :]
    cos, sin = jnp.cos(ang), jnp.sin(ang)
    z = lambda w: jnp.zeros((seq, w), F32)
    rest = HEAD_DIM - 2 * ROPE_HALF
    rc = jnp.concatenate([cos, cos, jnp.ones((seq, rest), F32)], axis=1)
    r1 = jnp.concatenate([-sin, z(ROPE_HALF + rest)], axis=1)
    r2 = jnp.concatenate([z(ROPE_HALF), sin, z(rest)], axis=1)
    return tuple(jnp.tile(t, (1, LANE // HEAD_DIM)) for t in (rc, r1, r2))


def _router_split(w):
    wp = jnp.pad(w, ((0, 0), (0, LANE - N_EXPERTS)))
    hi = wp.astype(BF16)
    return hi, (wp - hi.astype(F32)).astype(BF16)


def kernel(x, attn_norm, w_in_even, q_norm, k_norm, ssd_conv_w, ssd_conv_b, ssd_a_log_fwd, ssd_a_log_bwd,
           ssd_dt_bias_fwd, ssd_dt_bias_bwd, ssd_d, ssd_out_norm, w_out_even, conv_norm, conv_w_in, conv_w,
           conv_w_out, ffn_norm, router_w, expert_w_gate, expert_w_up, expert_w_down):
    b, seq, _ = x.shape
    n = b * seq
    depth = ffn_norm.shape[0]
    rc, r1, r2 = _rope_tables(seq)
    blk = np.arange(256) // HEAD_DIM
    bd = jnp.asarray((blk[:, None] == blk[None, :]) / HEAD_DIM, BF16)
    row = lambda v: v.reshape(1, -1).astype(F32)

    xf = x.reshape(n, D_MODEL)
    for layer in range(depth):
        i = layer // 2
        wr_hi, wr_lo = _router_split(router_w[layer])
        fg = row(ffn_norm[layer])
        if layer % 2 == 0:
            w = w_in_even[i].astype(BF16)
            o = np.cumsum([0, D_MODEL, D_MODEL, D_MODEL, D_MODEL, SSD_XBC, N_HEADS, N_HEADS])
            wq, wk, wv, wz, wx = (w[:, o[j]:o[j + 1]] for j in range(5))
            wd = jnp.pad(w[:, o[5]:o[7]], ((0, 0), (0, LANE - 2 * N_HEADS)))
            tile_heads = lambda g: row(jnp.tile(g, N_HEADS))
            q, k, v, z, xbc, dt = _in_even(xf, row(attn_norm[i]), wq, wk, wv, wz, wx, wd,
                                           tile_heads(q_norm[i]), tile_heads(k_norm[i]), bd, rc, r1, r2, seq)
            as3 = lambda t: t.reshape(b, seq, -1)
            attn = _attention(as3(q), as3(k), as3(v))
            cw = jnp.pad(ssd_conv_w[i], ((0, 8 - SSD_CONV), (0, 0)))
            act = _ssd_conv(as3(xbc), cw, row(ssd_conv_b[i]))
            pc = jnp.pad(jnp.stack([ssd_dt_bias_fwd[i], ssd_dt_bias_bwd[i], -jnp.exp(ssd_a_log_fwd[i]),
                                    -jnp.exp(ssd_a_log_bwd[i])], axis=1).astype(F32), ((0, 0), (0, LANE - 4)))
            y = _ssd(act, as3(dt), as3(z), pc, row(jnp.repeat(ssd_d[i], HEAD_DIM)), row(ssd_out_norm[i]))
            wo = w_out_even[i].astype(BF16)
            x1, hn, aff = _out_even(xf, attn.reshape(n, D_MODEL), y.reshape(n, D_MODEL), wo[:D_MODEL], wo[D_MODEL:],
                                    fg, wr_hi, wr_lo, b, seq)
        else:
            w = conv_w_in[i].astype(BF16)
            gb, cu = _in_odd(xf, row(conv_norm[i]), w[:, :D_MODEL], w[:, D_MODEL:2 * D_MODEL], w[:, 2 * D_MODEL:])
            cw = jnp.pad(conv_w[i], ((0, 8 - SHORT_CONV), (0, 0)))
            x1, hn, aff = _out_odd(xf, gb, cu, cw, conv_w_out[i].astype(BF16), fg, wr_hi, wr_lo, b, seq)
        wg = expert_w_gate[layer].astype(BF16)
        wu = expert_w_up[layer].astype(BF16)
        wdt = jnp.swapaxes(expert_w_down[layer], 1, 2).astype(BF16)
        xf = _moe(x1.reshape(b, seq, D_MODEL), hn.reshape(b, seq, D_MODEL), aff, wg, wu, wdt).reshape(n, D_MODEL)
    return xf.reshape(b, seq, D_MODEL)
```

```python
import functools
import math

import jax
import jax.numpy as jnp
import numpy as np
from jax import lax
from jax.experimental import pallas as pl
from jax.experimental.pallas import tpu as pltpu

F32, BF16, I32 = jnp.float32, jnp.bfloat16, jnp.int32

D_MODEL = 1024
N_HEADS = 16
HEAD_DIM = 64
ROPE_HALF = 8
ROPE_THETA = 500000.0
PATTERNS = ((128, 1), (512, 4), (2048, 16))
HALF_STEPS = 64
SSD_GROUPS = 2
SSD_STATE = 128
SSD_XBC = 1536
SSD_CONV = 5
CHUNK = 128
N_EXPERTS = 16
CAPACITY_FACTOR = 2
D_FF = 2048
SHORT_CONV = 3
EPS = 1e-6

LANE = 128
VMEM_LIMIT = 56 * 1024 * 1024

ROW_TILE = 256
ATTN_SUPER = 2048
ATTN_HALO = 1024
ATTN_TQ = 128
ATTN_TK = ATTN_TQ + 2 * HALF_STEPS
TOK_TILE = 512
SUB = TOK_TILE // LANE
SLOT_ALIGN = 16
WINDOW = 128
FFN_ROWS = 256


def _cparams(n_axes):
    return pltpu.CompilerParams(dimension_semantics=("arbitrary",) * n_axes, vmem_limit_bytes=VMEM_LIMIT)


def _const_spec(shape):
    nd = len(shape)
    return pl.BlockSpec(shape, lambda *_: (0,) * nd, pipeline_mode=pl.Buffered(1))


def _dot(a, b):
    return jnp.dot(a, b, preferred_element_type=F32)


def _dot_nt(a, b):
    return lax.dot_general(a, b, (((1,), (1,)), ((), ())), preferred_element_type=F32)


def _dot_tn(a, b):
    return lax.dot_general(a, b, (((0,), (0,)), ((), ())), preferred_element_type=F32)


def _split2(x):
    hi = x.astype(BF16)
    lo = (x - hi.astype(F32)).astype(BF16)
    return hi, lo


def _split3(x):
    hi = x.astype(BF16)
    r = x - hi.astype(F32)
    mid = r.astype(BF16)
    lo = (r - mid.astype(F32)).astype(BF16)
    return hi, mid, lo


def _dot3(x, m_bf16):
    hi, mid, lo = _split3(x)
    return _dot(hi, m_bf16) + _dot(mid, m_bf16) + _dot(lo, m_bf16)


def _rms(x, g):
    return x * lax.rsqrt(jnp.mean(x * x, axis=-1, keepdims=True) + EPS) * g


def _silu(x):
    return x * jax.nn.sigmoid(x)


def _in_even_kernel(x_ref, g_ref, wq_ref, wk_ref, wv_ref, wz_ref, wx_ref, wd_ref, qg_ref, kg_ref, bd_ref,
                    rc_ref, r1_ref, r2_ref, q_ref, k_ref, v_ref, z_ref, xbc_ref, dt_ref):
    hb = _rms(x_ref[...], g_ref[...]).astype(BF16)
    bd = bd_ref[...]
    rc, r1, r2 = rc_ref[...], r1_ref[...], r2_ref[...]

    def head_norm_rope(w_ref, gain_ref, out_ref):
        t = _dot(hb, w_ref[...])
        for c in range(D_MODEL // 256):
            tc = t[:, c * 256:(c + 1) * 256]
            sq_hi, sq_lo = _split2(tc * tc)
            ms = _dot(sq_hi, bd) + _dot(sq_lo, bd)
            tn = tc * lax.rsqrt(ms + EPS) * gain_ref[:, c * 256:(c + 1) * 256]
            for hh in range(2):
                u = tn[:, hh * LANE:(hh + 1) * LANE]
                r = u * rc + pltpu.roll(u, LANE - ROPE_HALF, 1) * r1 + pltpu.roll(u, ROPE_HALF, 1) * r2
                out_ref[:, c * 256 + hh * LANE:c * 256 + (hh + 1) * LANE] = r.astype(BF16)

    head_norm_rope(wq_ref, qg_ref, q_ref)
    head_norm_rope(wk_ref, kg_ref, k_ref)
    v_ref[...] = _dot(hb, wv_ref[...]).astype(BF16)
    z_ref[...] = _dot(hb, wz_ref[...])
    xbc_ref[...] = _dot(hb, wx_ref[...])
    dt_ref[...] = _dot(hb, wd_ref[...])


def _in_even(x, g, wq, wk, wv, wz, wx, wd, qg, kg, bd, rc, r1, r2, seq):
    n = x.shape[0]
    tm = ROW_TILE
    row = lambda w: pl.BlockSpec((tm, w), lambda i: (i, 0))
    tab = pl.BlockSpec((tm, LANE), lambda i: (i % (seq // tm), 0))
    outs = [jax.ShapeDtypeStruct((n, D_MODEL), BF16)] * 3 + [jax.ShapeDtypeStruct((n, D_MODEL), F32),
                                                            jax.ShapeDtypeStruct((n, SSD_XBC), F32),
                                                            jax.ShapeDtypeStruct((n, LANE), F32)]
    return pl.pallas_call(
        _in_even_kernel, grid=(n // tm,),
        in_specs=[row(D_MODEL), _const_spec((1, D_MODEL)), _const_spec(wq.shape), _const_spec(wk.shape),
                  _const_spec(wv.shape), _const_spec(wz.shape), _const_spec(wx.shape), _const_spec(wd.shape),
                  _const_spec((1, D_MODEL)), _const_spec((1, D_MODEL)), _const_spec((256, 256)), tab, tab, tab],
        out_specs=[row(D_MODEL)] * 4 + [row(SSD_XBC), row(LANE)], out_shape=outs,
        compiler_params=_cparams(1), name="in_even",
    )(x, g, wq, wk, wv, wz, wx, wd, qg, kg, bd, rc, r1, r2)


def _attn_kernel(q_ref, k0, k1, k2, k3, v0, v1, v2, v3, o_ref, qf, kf, vf, acc, mst, lst, *, seq):
    p0 = pl.program_id(2) * ATTN_SUPER
    qf[...] = q_ref[0].astype(F32) * (HEAD_DIM ** -0.5)
    for i, (kr, vr) in enumerate(((k0, v0), (k1, v1), (k2, v2), (k3, v3))):
        kf[i * ATTN_HALO:(i + 1) * ATTN_HALO, :] = kr[0].astype(F32)
        vf[i * ATTN_HALO:(i + 1) * ATTN_HALO, :] = vr[0].astype(F32)

    head_a = lax.broadcasted_iota(I32, (ATTN_TQ, LANE), 1) < HEAD_DIM
    off = lax.broadcasted_iota(I32, (ATTN_TQ, ATTN_TK), 1) - lax.broadcasted_iota(I32, (ATTN_TQ, ATTN_TK), 0)
    band = (off >= 0) & (off <= 2 * HALF_STEPS)
    colpos = lax.broadcasted_iota(I32, (1, ATTN_TK), 1)

    def visit(d, i, first, last):
        log_d = d.bit_length() - 1
        r = i & (d - 1)
        t = i >> log_d
        qs = r + d * ATTN_TQ * t
        ks = ATTN_HALO + r + d * (ATTN_TQ * t - HALF_STEPS)
        q = qf[pl.ds(qs, ATTN_TQ, stride=d), :]
        kt = kf[pl.ds(ks, ATTN_TK, stride=d), :].astype(BF16)
        vt = vf[pl.ds(ks, ATTN_TK, stride=d), :].astype(BF16)
        pos = p0 - ATTN_HALO + ks + d * colpos
        valid = band & (pos >= 0) & (pos < seq)

        def one_head(qh):
            s = jnp.where(valid, _dot_nt(qh.astype(BF16), kt), -jnp.inf)
            m = jnp.max(s, axis=1, keepdims=True)
            p = jnp.exp(s - m)
            return m, jnp.sum(p, axis=1, keepdims=True), _dot(p.astype(BF16), vt)

        ma, la, na = one_head(jnp.where(head_a, q, 0.0))
        mb, lb, nb = one_head(jnp.where(head_a, 0.0, q))
        m_loc = jnp.where(head_a, ma, mb)
        l_loc = jnp.where(head_a, la, lb)
        n_loc = jnp.where(head_a, na, nb)
        rows = pl.ds(qs, ATTN_TQ, stride=d)
        if first:
            m_new, l_new, a_new = m_loc, l_loc, n_loc
        else:
            m_old = mst[rows, :]
            m_new = jnp.maximum(m_old, m_loc)
            w_old = jnp.exp(m_old - m_new)
            w_loc = jnp.exp(m_loc - m_new)
            l_new = lst[rows, :] * w_old + l_loc * w_loc
            a_new = acc[rows, :] * w_old + n_loc * w_loc
        if last:
            acc[rows, :] = a_new / l_new
        else:
            mst[rows, :] = m_new
            lst[rows, :] = l_new
            acc[rows, :] = a_new

    n_visits = ATTN_SUPER // ATTN_TQ
    for idx, (_, d) in enumerate(PATTERNS):
        first, last = idx == 0, idx == len(PATTERNS) - 1

        def body(i, c, d=d, first=first, last=last):
            visit(d, i, first, last)
            return c

        lax.fori_loop(0, n_visits, body, 0)
    o_ref[0] = acc[...].astype(BF16)


def _attention(q, k, v):
    b, seq, _ = q.shape
    nblk = seq // ATTN_HALO
    ratio = ATTN_SUPER // ATTN_HALO

    def halo(i):
        return pl.BlockSpec((1, ATTN_HALO, LANE),
                            lambda bi, hp, j: (bi, jnp.clip(ratio * j - 1 + i, 0, nblk - 1), hp))

    main = pl.BlockSpec((1, ATTN_SUPER, LANE), lambda bi, hp, j: (bi, j, hp))
    return pl.pallas_call(
        functools.partial(_attn_kernel, seq=seq),
        grid=(b, D_MODEL // LANE, seq // ATTN_SUPER),
        in_specs=[main] + [halo(i) for i in range(4)] * 2,
        out_specs=main,
        out_shape=jax.ShapeDtypeStruct((b, seq, D_MODEL), BF16),
        scratch_shapes=[pltpu.VMEM((ATTN_SUPER, LANE), F32),
                        pltpu.VMEM((4 * ATTN_HALO, LANE), F32), pltpu.VMEM((4 * ATTN_HALO, LANE), F32),
                        pltpu.VMEM((ATTN_SUPER, LANE), F32), pltpu.VMEM((ATTN_SUPER, LANE), F32),
                        pltpu.VMEM((ATTN_SUPER, LANE), F32)],
        compiler_params=_cparams(3), name="dilated_attn",
    )(q, k, k, k, k, v, v, v, v)


def _shifted(cur, prev8, next8, s, first, last):
    n = cur.shape[0]
    if s == 0:
        return cur
    rows = lax.broadcasted_iota(I32, cur.shape, 0)
    out = pltpu.roll(cur, (-s) % n, 0)
    if s < 0:
        for j in range(-s):
            src = jnp.where(first, 0.0, prev8[8 + s + j:8 + s + j + 1, :])
            out = jnp.where(rows == j, src, out)
    else:
        for j in range(s):
            src = jnp.where(last, 0.0, next8[j:j + 1, :])
            out = jnp.where(rows == n - s + j, src, out)
    return out


def _ssd_conv_kernel(c_ref, p_ref, n_ref, w_ref, b_ref, o_ref):
    first = pl.program_id(1) == 0
    last = pl.program_id(1) == pl.num_programs(1) - 1
    for c in range(SSD_XBC // 256):
        sl = slice(c * 256, (c + 1) * 256)
        cur, prev8, next8 = c_ref[0, :, sl], p_ref[0, :, sl], n_ref[0, :, sl]
        y = b_ref[:, sl]
        for j in range(SSD_CONV):
            y = y + _shifted(cur, prev8, next8, j - SSD_CONV // 2, first, last) * w_ref[j:j + 1, sl]
        o_ref[0, :, sl] = _silu(y)


def _halo_specs(tm, width, seq):
    cur = pl.BlockSpec((1, tm, width), lambda b, i: (b, i, 0))
    prev = pl.BlockSpec((1, 8, width), lambda b, i: (b, jnp.maximum(i * (tm // 8) - 1, 0), 0))
    nxt = pl.BlockSpec((1, 8, width), lambda b, i: (b, jnp.minimum((i + 1) * (tm // 8), seq // 8 - 1), 0))
    return cur, prev, nxt


def _ssd_conv(xbc, w, bias):
    b, seq, width = xbc.shape
    tm = 512
    cur, prev, nxt = _halo_specs(tm, width, seq)
    return pl.pallas_call(
        _ssd_conv_kernel, grid=(b, seq // tm),
        in_specs=[cur, prev, nxt, pl.BlockSpec((8, width), lambda b, i: (0, 0)),
                  pl.BlockSpec((1, width), lambda b, i: (0, 0))],
        out_specs=cur, out_shape=jax.ShapeDtypeStruct(xbc.shape, F32),
        compiler_params=_cparams(2), name="ssd_conv",
    )(xbc, xbc, xbc, w, bias)


def _softplus(x):
    return jnp.maximum(x, 0.0) + jnp.log1p(jnp.exp(-jnp.abs(x)))


def _tri(kind):
    s = lax.broadcasted_iota(I32, (CHUNK, CHUNK), 0)
    l = lax.broadcasted_iota(I32, (CHUNK, CHUNK), 1)
    return {"le": s <= l, "ge": s >= l, "lt": s < l}[kind]


def _expand(cols, e2_ref):
    hi, lo = _split2(cols)
    return _dot(jnp.concatenate([hi, lo], axis=1), e2_ref[...])


def _ssd_bwd_kernel(xs_ref, b_ref, dt_ref, pc_ref, e2_ref, sb_ref, st):
    @pl.when(pl.program_id(1) == 0)
    def _():
        st[...] = jnp.zeros_like(st)

    sb_ref[0, 0] = st[...].astype(BF16)
    dt_t = dt_ref[0].T
    dtb = _softplus(dt_t[N_HEADS:2 * N_HEADS, :] + pc_ref[:, 1:2])
    a = dtb * pc_ref[:, 3:4]
    ex = _dot3(a, _tri("lt").astype(BF16))
    tot = ex[:, CHUNK - 1:CHUNK] + a[:, CHUNK - 1:CHUNK]
    rowform = jnp.concatenate([dtb * jnp.exp(ex), jnp.broadcast_to(jnp.exp(tot), (N_HEADS, CHUNK)),
                               jnp.zeros((CHUNK - 2 * N_HEADS, CHUNK), F32)], axis=0)
    ex2 = _expand(rowform.T, e2_ref)
    xw = (xs_ref[0] * ex2[:, :D_MODEL]).astype(BF16)
    half = D_MODEL // SSD_GROUPS
    upd = [_dot(b_ref[0, :, g * SSD_STATE:(g + 1) * SSD_STATE].T.astype(BF16), xw[:, g * half:(g + 1) * half])
           for g in range(SSD_GROUPS)]
    st[...] = st[...] * ex2[0:1, D_MODEL:] + jnp.concatenate(upd, axis=1)


def _ssd_fwd_kernel(xs_ref, b_ref, c_ref, dt_ref, z_ref, sb_ref, pc_ref, e3_ref, dexp_ref, on_ref, o_ref, st):
    @pl.when(pl.program_id(1) == 0)
    def _():
        st[...] = jnp.zeros_like(st)

    xs = xs_ref[0]
    dt_t = dt_ref[0].T
    dtf = _softplus(dt_t[0:N_HEADS, :] + pc_ref[:, 0:1])
    dtb = _softplus(dt_t[N_HEADS:2 * N_HEADS, :] + pc_ref[:, 1:2])
    af = dtf * pc_ref[:, 2:3]
    ab = dtb * pc_ref[:, 3:4]
    csf = _dot3(af, _tri("le").astype(BF16))
    rcs = _dot3(ab, _tri("ge").astype(BF16))
    totf = csf[:, CHUNK - 1:CHUNK]
    rowform = jnp.concatenate([dtf * jnp.exp(totf - csf), jnp.exp(csf), jnp.exp(rcs), csf, rcs,
                               jnp.zeros((CHUNK - 5 * N_HEADS, CHUNK), F32)], axis=0)
    cols = rowform.T
    ex3 = _expand(cols, e3_ref)
    w_state, e_f, e_b = ex3[:, :D_MODEL], ex3[:, D_MODEL:2 * D_MODEL], ex3[:, 2 * D_MODEL:]

    xb = xs.astype(BF16)
    lower, upper = _tri("ge"), _tri("le")
    head_a = lax.broadcasted_iota(I32, (CHUNK, LANE), 1) < HEAD_DIM
    half = D_MODEL // SSD_GROUPS
    hpg = N_HEADS // SSD_GROUPS
    st_all = st[...]
    sb_all = sb_ref[0, 0]
    ys = []
    b_t = []
    for g in range(SSD_GROUPS):
        bg = b_ref[0, :, g * SSD_STATE:(g + 1) * SSD_STATE]
        cg = c_ref[0, :, g * SSD_STATE:(g + 1) * SSD_STATE].astype(BF16)
        b_t.append(bg.T.astype(BF16))
        gm = _dot_nt(cg, bg.astype(BF16))
        states = jnp.concatenate([st_all[:, g * half:(g + 1) * half].astype(BF16),
                                  sb_all[:, g * half:(g + 1) * half]], axis=1)
        off = _dot(cg, states)
        y_off = (off[:, :half] * e_f[:, g * half:(g + 1) * half]
                 + off[:, half:] * e_b[:, g * half:(g + 1) * half])
        for pair in range(hpg // 2):
            ms = []
            for h in (g * hpg + 2 * pair, g * hpg + 2 * pair + 1):
                dec_f = jnp.where(lower, jnp.exp(cols[:, 3 * N_HEADS + h:3 * N_HEADS + h + 1] - csf[h:h + 1, :]), 0.0)
                dec_b = jnp.where(upper, jnp.exp(cols[:, 4 * N_HEADS + h:4 * N_HEADS + h + 1] - rcs[h:h + 1, :]), 0.0)
                ms.append((gm * (dec_f * dtf[h:h + 1, :] + dec_b * dtb[h:h + 1, :])).astype(BF16))
            lo = g * half + pair * LANE
            xp = xb[:, lo:lo + LANE]
            ys.append(jnp.where(head_a, _dot(ms[0], xp), _dot(ms[1], xp))
                      + y_off[:, pair * LANE:(pair + 1) * LANE])
    y = jnp.concatenate(ys, axis=1) + xs * dexp_ref[...]
    yz = y * _silu(z_ref[0])
    o_ref[0] = _rms(yz, on_ref[...]).astype(BF16)

    xw = (xs * w_state).astype(BF16)
    upd = [_dot(b_t[g], xw[:, g * half:(g + 1) * half]) for g in range(SSD_GROUPS)]
    st[...] = st_all * e_f[CHUNK - 1:CHUNK, :] + jnp.concatenate(upd, axis=1)


def _ssd(xbc_act, dt, z, pc, dexp, out_norm):
    b, seq, _ = xbc_act.shape
    nc = seq // CHUNK
    sel = np.zeros((2 * CHUNK, 3 * D_MODEL), np.float32)
    for part in range(3):
        for h in range(N_HEADS):
            for rep in range(2):
                sel[rep * CHUNK + part * N_HEADS + h, part * D_MODEL + h * HEAD_DIM:part * D_MODEL + (h + 1) * HEAD_DIM] = 1.0
    e3 = jnp.asarray(sel, BF16)
    e2 = jnp.asarray(sel[:, :2 * D_MODEL], BF16)

    rev = lambda bi, c: (bi, nc - 1 - c, 0)
    sb = pl.pallas_call(
        _ssd_bwd_kernel, grid=(b, nc),
        in_specs=[pl.BlockSpec((1, CHUNK, D_MODEL), rev),
                  pl.BlockSpec((1, CHUNK, 2 * SSD_STATE), lambda bi, c: (bi, nc - 1 - c, D_MODEL // (2 * SSD_STATE))),
                  pl.BlockSpec((1, CHUNK, LANE), rev), _const_spec(pc.shape), _const_spec(e2.shape)],
        out_specs=pl.BlockSpec((1, 1, SSD_STATE, D_MODEL), lambda bi, c: (bi, nc - 1 - c, 0, 0)),
        out_shape=jax.ShapeDtypeStruct((b, nc, SSD_STATE, D_MODEL), BF16),
        scratch_shapes=[pltpu.VMEM((SSD_STATE, D_MODEL), F32)],
        compiler_params=_cparams(2), name="ssd_bwd_state",
    )(xbc_act, xbc_act, dt, pc, e2)

    fwd = lambda bi, c: (bi, c, 0)
    return pl.pallas_call(
        _ssd_fwd_kernel, grid=(b, nc),
        in_specs=[pl.BlockSpec((1, CHUNK, D_MODEL), fwd),
                  pl.BlockSpec((1, CHUNK, 2 * SSD_STATE), lambda bi, c: (bi, c, D_MODEL // (2 * SSD_STATE))),
                  pl.BlockSpec((1, CHUNK, 2 * SSD_STATE), lambda bi, c: (bi, c, D_MODEL // (2 * SSD_STATE) + 1)),
                  pl.BlockSpec((1, CHUNK, LANE), fwd), pl.BlockSpec((1, CHUNK, D_MODEL), fwd),
                  pl.BlockSpec((1, 1, SSD_STATE, D_MODEL), lambda bi, c: (bi, c, 0, 0)),
                  _const_spec(pc.shape), _const_spec(e3.shape), _const_spec((1, D_MODEL)), _const_spec((1, D_MODEL))],
        out_specs=pl.BlockSpec((1, CHUNK, D_MODEL), fwd),
        out_shape=jax.ShapeDtypeStruct((b, seq, D_MODEL), BF16),
        scratch_shapes=[pltpu.VMEM((SSD_STATE, D_MODEL), F32)],
        compiler_params=_cparams(2), name="ssd_fwd",
    )(xbc_act, xbc_act, xbc_act, dt, z, sb, pc, e3, dexp, out_norm)


def _norm_and_route(x1, fg_ref, wr_hi_ref, wr_lo_ref, hn_ref, aff_ref):
    hn = _rms(x1, fg_ref[...])
    hi, lo = _split2(hn)
    hn_ref[...] = hi
    logits = _dot(hi, wr_hi_ref[...]) + _dot(lo, wr_hi_ref[...]) + _dot(hi, wr_lo_ref[...])
    lt = logits.T[0:N_EXPERTS, :]
    e = jnp.exp(lt - jnp.max(lt, axis=0, keepdims=True))
    aff_ref[0] = e / jnp.sum(e, axis=0, keepdims=True)


def _out_even_kernel(x_ref, a_ref, y_ref, wa_ref, wy_ref, fg_ref, wr_hi_ref, wr_lo_ref, x1_ref, hn_ref, aff_ref):
    x1 = x_ref[...] + _dot(a_ref[...], wa_ref[...]) + _dot(y_ref[...], wy_ref[...])
    x1_ref[...] = x1
    _norm_and_route(x1, fg_ref, wr_hi_ref, wr_lo_ref, hn_ref, aff_ref)


def _out_odd_kernel(x_ref, gb_ref, cu_ref, cp_ref, cn_ref, cw_ref, wo_ref, fg_ref, wr_hi_ref, wr_lo_ref,
                    x1_ref, hn_ref, aff_ref, *, tiles_per_seq):
    i = pl.program_id(0) % tiles_per_seq
    first, last = i == 0, i == tiles_per_seq - 1
    cur, prev8, next8 = cu_ref[...], cp_ref[...], cn_ref[...]
    conv = sum(_shifted(cur, prev8, next8, j - SHORT_CONV // 2, first, last) * cw_ref[j:j + 1, :]
               for j in range(SHORT_CONV))
    x1 = x_ref[...] + _dot((gb_ref[...] * conv).astype(BF16), wo_ref[...])
    x1_ref[...] = x1
    _norm_and_route(x1, fg_ref, wr_hi_ref, wr_lo_ref, hn_ref, aff_ref)


def _route_outs(n, b, seq, tm):
    row = pl.BlockSpec((tm, D_MODEL), lambda i: (i, 0))
    aff = pl.BlockSpec((1, N_EXPERTS, tm), lambda i: (i // (seq // tm), 0, i % (seq // tm)))
    shapes = [jax.ShapeDtypeStruct((n, D_MODEL), F32), jax.ShapeDtypeStruct((n, D_MODEL), BF16),
              jax.ShapeDtypeStruct((b, N_EXPERTS, seq), F32)]
    return [row, row, aff], shapes


def _out_even(x, attn, y, wa, wy, fg, wr_hi, wr_lo, b, seq):
    n = x.shape[0]
    tm = ROW_TILE
    row = pl.BlockSpec((tm, D_MODEL), lambda i: (i, 0))
    out_specs, shapes = _route_outs(n, b, seq, tm)
    return pl.pallas_call(
        _out_even_kernel, grid=(n // tm,),
        in_specs=[row, row, row, _const_spec(wa.shape), _const_spec(wy.shape), _const_spec((1, D_MODEL)),
                  _const_spec(wr_hi.shape), _const_spec(wr_lo.shape)],
        out_specs=out_specs, out_shape=shapes, compiler_params=_cparams(1), name="out_even",
    )(x, attn, y, wa, wy, fg, wr_hi, wr_lo)


def _out_odd(x, gb, cu, cw, wo, fg, wr_hi, wr_lo, b, seq):
    n = x.shape[0]
    tm = ROW_TILE
    row = pl.BlockSpec((tm, D_MODEL), lambda i: (i, 0))
    prev = pl.BlockSpec((8, D_MODEL), lambda i: (jnp.maximum(i * (tm // 8) - 1, 0), 0))
    nxt = pl.BlockSpec((8, D_MODEL), lambda i: (jnp.minimum((i + 1) * (tm // 8), n // 8 - 1), 0))
    out_specs, shapes = _route_outs(n, b, seq, tm)
    return pl.pallas_call(
        functools.partial(_out_odd_kernel, tiles_per_seq=seq // tm), grid=(n // tm,),
        in_specs=[row, row, row, prev, nxt, pl.BlockSpec((8, D_MODEL), lambda i: (0, 0)), _const_spec(wo.shape),
                  _const_spec((1, D_MODEL)), _const_spec(wr_hi.shape), _const_spec(wr_lo.shape)],
        out_specs=out_specs, out_shape=shapes, compiler_params=_cparams(1), name="out_odd",
    )(x, gb, cu, cu, cu, cw, wo, fg, wr_hi, wr_lo)


def _in_odd_kernel(x_ref, g_ref, wb_ref, wc_ref, wu_ref, gb_ref, cu_ref):
    hb = _rms(x_ref[...], g_ref[...]).astype(BF16)
    gb_ref[...] = _dot(hb, wb_ref[...])
    cu_ref[...] = _dot(hb, wc_ref[...]) * _dot(hb, wu_ref[...])


def _in_odd(x, g, wb, wc, wu):
    n = x.shape[0]
    tm = ROW_TILE
    row = pl.BlockSpec((tm, D_MODEL), lambda i: (i, 0))
    return pl.pallas_call(
        _in_odd_kernel, grid=(n // tm,),
        in_specs=[row, _const_spec((1, D_MODEL)), _const_spec(wb.shape), _const_spec(wc.shape), _const_spec(wu.shape)],
        out_specs=[row, row], out_shape=[jax.ShapeDtypeStruct((n, D_MODEL), F32)] * 2,
        compiler_params=_cparams(1), name="in_odd",
    )(x, g, wb, wc, wu)


def _count(mask):
    return jnp.sum(jnp.sum(mask.astype(F32), axis=0, keepdims=True), axis=1, keepdims=True)


def _route_kernel(aff_ref, incl_ref, ones_ref, strict_ref, local_ref, group_ref, first_ref, slot_ref, off_ref, end_ref,
                  *, cap):
    for e in range(N_EXPERTS):
        bits = pltpu.bitcast(aff_ref[0, e], I32)

        def step(i, thr):
            cand = thr | (jnp.int32(1) << (30 - i))
            return jnp.where(_count(bits >= cand) >= cap, cand, thr)

        thr = lax.fori_loop(0, 31, step, jnp.zeros((1, 1), I32))
        gt = bits > thr
        eq = (bits == thr).astype(BF16)
        eq_rank = _dot(eq, incl_ref[...]) + _dot(strict_ref[...], _dot(eq, ones_ref[...]).astype(BF16))
        sel = (gt | ((bits == thr) & (eq_rank <= cap - _count(gt)))).astype(BF16)
        within = _dot(sel, incl_ref[...])
        totals = _dot(sel, ones_ref[...]).astype(BF16)
        local = _dot(local_ref[...], totals)
        cnt = _dot(group_ref[...], totals)
        padded = jnp.floor((cnt + (SLOT_ALIGN - 1)) * (1.0 / SLOT_ALIGN)) * SLOT_ALIGN
        start = _dot(first_ref[...], padded.astype(BF16))
        slot_ref[0, e] = jnp.where(sel > 0, (start + local + within).astype(I32) - 1, -1)
        off_ref[0, e:e + 1, :] = start.T[0:1, :].astype(I32)
        end_ref[0, e:e + 1, :] = (start + padded).T[0:1, :].astype(I32)


def _route(aff, cap):
    b, _, seq = aff.shape
    nt = seq // LANE
    tri = np.arange(LANE)
    tt = np.arange(nt)
    grp = tt // SUB
    as_bf16 = lambda m: jnp.asarray(m, BF16)
    incl = as_bf16(tri[:, None] <= tri[None, :])
    strict = as_bf16(tt[None, :] < tt[:, None])
    local = as_bf16((tt[None, :] < tt[:, None]) & (grp[None, :] == grp[:, None]))
    group = as_bf16(grp[None, :] == grp[:, None])
    first = as_bf16((grp[None, :] < grp[:, None]) & (tt[None, :] % SUB == 0))
    ones = jnp.ones((LANE, LANE), BF16)
    tiles = pl.BlockSpec((1, N_EXPERTS, nt, LANE), lambda i: (i, 0, 0, 0))
    rows = pl.BlockSpec((1, N_EXPERTS, nt), lambda i: (i, 0, 0))
    return pl.pallas_call(
        functools.partial(_route_kernel, cap=cap), grid=(b,),
        in_specs=[tiles, _const_spec((LANE, LANE)), _const_spec((LANE, LANE))] + [_const_spec((nt, nt))] * 4,
        out_specs=[tiles, rows, rows],
        out_shape=[jax.ShapeDtypeStruct((b, N_EXPERTS, nt, LANE), I32)] + [jax.ShapeDtypeStruct((b, N_EXPERTS, nt), I32)] * 2,
        compiler_params=_cparams(1), name="route",
    )(aff.reshape(b, N_EXPERTS, nt, LANE), incl, ones, strict, local, group, first)


def _tile_row(ref, e, j):
    return jnp.concatenate([ref[0, e, j * SUB + r:j * SUB + r + 1, :] for r in range(SUB)], axis=1)


def _one_hot(slots, base):
    return (slots == lax.broadcasted_iota(I32, (WINDOW, slots.shape[1]), 0) + base).astype(BF16)


def _gather_kernel(off_ref, end_ref, slot_ref, aff_ref, hn_ref, xe_ref):
    bi, e = pl.program_id(0), pl.program_id(2)
    half = hn_ref.shape[2]
    xe_ref[...] = jnp.zeros_like(xe_ref)
    for j in range(hn_ref.shape[1] // TOK_TILE):
        first = off_ref[bi, e, j * SUB]
        n_pad = end_ref[bi, e, j * SUB] - first
        slots = _tile_row(slot_ref, 0, j)
        parts = [p.astype(F32) for p in _split3(_tile_row(aff_ref, 0, j))]
        gates = jnp.concatenate(parts + [jnp.zeros((LANE - 3, TOK_TILE), F32)], axis=0).astype(BF16)
        tokens = hn_ref[0, j * TOK_TILE:(j + 1) * TOK_TILE, :]

        def window(w):
            base = pl.multiple_of(first + w * WINDOW, SLOT_ALIGN)
            p = _one_hot(slots, base)
            xe_ref[0, 0, pl.ds(base, WINDOW), 0:half] = _dot(p, tokens).astype(BF16)
            xe_ref[0, 0, pl.ds(base, WINDOW), half:] = _dot_nt(p, gates).astype(BF16)

        window(0)
        for w in range(1, TOK_TILE // WINDOW):
            pl.when(n_pad > w * WINDOW)(functools.partial(window, w))


def _gather(off, end, slot, aff4, hn):
    b, seq, _ = hn.shape
    nt = seq // LANE
    half = D_MODEL // 2
    cap_pad = _cap_pad(seq)
    tiles = pl.BlockSpec((1, 1, nt, LANE), lambda bi, h, e, *_: (bi, e, 0, 0))
    return pl.pallas_call(
        _gather_kernel,
        grid_spec=pltpu.PrefetchScalarGridSpec(
            num_scalar_prefetch=2, grid=(b, 2, N_EXPERTS),
            in_specs=[tiles, tiles, pl.BlockSpec((1, seq, half), lambda bi, h, e, *_: (bi, 0, h))],
            out_specs=pl.BlockSpec((1, 1, cap_pad, half + LANE), lambda bi, h, e, *_: (bi, e, 0, h))),
        out_shape=jax.ShapeDtypeStruct((b, N_EXPERTS, cap_pad, 2 * (half + LANE)), BF16),
        compiler_params=_cparams(3), name="moe_gather",
    )(off, end, slot, aff4, hn)


def _ffn_kernel(end_ref, xa_ref, xb_ref, wg_ref, wu_ref, wd_ref, y_ref):
    e, bi, r = pl.program_id(0), pl.program_id(1), pl.program_id(2)
    used = end_ref[bi, e, end_ref.shape[2] - 1]
    half = D_MODEL // 2

    @pl.when(r * FFN_ROWS < used)
    def _():
        xe = jnp.concatenate([xa_ref[0, 0, :, 0:half], xb_ref[0, 0, :, 0:half]], axis=1)
        hid = (_silu(_dot(xe, wg_ref[0])) * _dot(xe, wu_ref[0])).astype(BF16)
        g = xa_ref[0, 0, :, half:].astype(F32)
        gate = g[:, 0:1] + g[:, 1:2] + g[:, 2:3]
        y_ref[0, 0] = (_dot(hid, wd_ref[0]) * gate).astype(BF16)

    @pl.when(r * FFN_ROWS >= used)
    def _():
        y_ref[...] = jnp.zeros_like(y_ref)


def _ffn(end, xe, wg, wu, wd):
    b, ne, cap_pad, _ = xe.shape
    xhalf = lambda h: pl.BlockSpec((1, 1, FFN_ROWS, D_MODEL // 2 + LANE), lambda e, bi, r, *_: (bi, e, r, h))
    wspec = lambda s: pl.BlockSpec((1,) + s, lambda e, bi, r, *_: (e, 0, 0))
    return pl.pallas_call(
        _ffn_kernel,
        grid_spec=pltpu.PrefetchScalarGridSpec(
            num_scalar_prefetch=1, grid=(ne, b, cap_pad // FFN_ROWS),
            in_specs=[xhalf(0), xhalf(1), wspec((D_MODEL, D_FF)), wspec((D_MODEL, D_FF)), wspec((D_FF, D_MODEL))],
            out_specs=pl.BlockSpec((1, 1, FFN_ROWS, D_MODEL), lambda e, bi, r, *_: (bi, e, r, 0))),
        out_shape=jax.ShapeDtypeStruct((b, ne, cap_pad, D_MODEL), BF16),
        compiler_params=_cparams(3), name="moe_ffn",
    )(end, xe, xe, wg, wu, wd)


def _combine_kernel(off_ref, end_ref, slot_ref, y_hbm, x1_ref, o_ref, win, extra, sem, xsem):
    bi, j = pl.program_id(0), pl.program_id(1)
    n_j = pl.num_programs(1)
    step = bi * n_j + j
    cur = step % 2

    def window_copy(bb, jj, e, buf):
        start = pl.multiple_of(off_ref[bb, e, jj * SUB], SLOT_ALIGN)
        return pltpu.make_async_copy(y_hbm.at[bb, e, pl.ds(start, WINDOW), :], win.at[buf, e], sem.at[buf, e])

    @pl.when(step == 0)
    def _():
        for e in range(N_EXPERTS):
            window_copy(bi, j, e, cur).start()

    @pl.when(step + 1 < pl.num_programs(0) * n_j)
    def _():
        nxt = step + 1
        for e in range(N_EXPERTS):
            window_copy(nxt // n_j, nxt % n_j, e, 1 - cur).start()

    ps = []
    for e in range(N_EXPERTS):
        window_copy(bi, j, e, cur).wait()
        ps.append(_one_hot(slot_ref[0, 0, e:e + 1, :], off_ref[bi, e, j * SUB]))
    p_all = jnp.concatenate(ps, axis=0)
    y_all = win[cur].reshape(N_EXPERTS * WINDOW, D_MODEL)
    o_ref[0] = x1_ref[0] + _dot_tn(p_all, y_all)

    def overflow(e, c):
        first = off_ref[bi, e, j * SUB]
        n_win = (end_ref[bi, e, j * SUB] - first + WINDOW - 1) // WINDOW
        slots = slot_ref[0, 0, pl.ds(e, 1), :]

        def one(w, c2):
            base = pl.multiple_of(first + w * WINDOW, SLOT_ALIGN)
            cp = pltpu.make_async_copy(y_hbm.at[bi, e, pl.ds(base, WINDOW), :], extra, xsem.at[0])
            cp.start()
            cp.wait()
            o_ref[0] += _dot_tn(_one_hot(slots, base), extra[...])
            return c2

        return lax.fori_loop(1, n_win, one, c)

    lax.fori_loop(0, N_EXPERTS, overflow, 0)


def _combine(off, end, slot, y, x1):
    b, seq, _ = x1.shape
    nt = seq // LANE
    tile = pl.BlockSpec((1, TOK_TILE, D_MODEL), lambda bi, j, *_: (bi, j, 0))
    return pl.pallas_call(
        _combine_kernel,
        grid_spec=pltpu.PrefetchScalarGridSpec(
            num_scalar_prefetch=2, grid=(b, seq // TOK_TILE),
            in_specs=[pl.BlockSpec((1, 1, N_EXPERTS, TOK_TILE), lambda bi, j, *_: (bi, j, 0, 0)),
                      pl.BlockSpec(memory_space=pl.ANY), tile],
            out_specs=tile,
            scratch_shapes=[pltpu.VMEM((2, N_EXPERTS, WINDOW, D_MODEL), BF16), pltpu.VMEM((WINDOW, D_MODEL), BF16),
                            pltpu.SemaphoreType.DMA((2, N_EXPERTS)), pltpu.SemaphoreType.DMA((1,))]),
        out_shape=jax.ShapeDtypeStruct((b, seq, D_MODEL), F32),
        compiler_params=_cparams(2), name="moe_combine",
    )(off, end, jnp.swapaxes(slot.reshape(b, N_EXPERTS, seq // TOK_TILE, TOK_TILE), 1, 2), y, x1)


def _cap_pad(seq):
    cap = CAPACITY_FACTOR * seq // N_EXPERTS
    worst = cap + (seq // TOK_TILE) * (SLOT_ALIGN - 1) + WINDOW
    return -(-worst // FFN_ROWS) * FFN_ROWS


def _moe(x1, hn, aff, wg, wu, wd):
    b, seq, _ = x1.shape
    cap = CAPACITY_FACTOR * seq // N_EXPERTS
    slot, off, end = _route(aff, cap)
    xe = _gather(off, end, slot, aff.reshape(b, N_EXPERTS, seq // LANE, LANE), hn)
    y = _ffn(end, xe, wg, wu, wd)
    return _combine(off, end, slot, y, x1)


def _rope_tables(seq):
    inv_freq = ROPE_THETA ** (-jnp.arange(ROPE_HALF, dtype=F32) * 2.0 / (2 * ROPE_HALF))
    ang = jnp.arange(seq, dtype=F32)[:, None] * inv_freq[None, :]
    cos, sin = jnp.cos(ang), jnp.sin(ang)
    z = lambda w: jnp.zeros((seq, w), F32)
    rest = HEAD_DIM - 2 * ROPE_HALF
    rc = jnp.concatenate([cos, cos, jnp.ones((seq, rest), F32)], axis=1)
    r1 = jnp.concatenate([-sin, z(ROPE_HALF + rest)], axis=1)
    r2 = jnp.concatenate([z(ROPE_HALF), sin, z(rest)], axis=1)
    return tuple(jnp.tile(t, (1, LANE // HEAD_DIM)) for t in (rc, r1, r2))


def _router_split(w):
    wp = jnp.pad(w, ((0, 0), (0, LANE - N_EXPERTS)))
    hi = wp.astype(BF16)
    return hi, (wp - hi.astype(F32)).astype(BF16)


def kernel(x, attn_norm, w_in_even, q_norm, k_norm, ssd_conv_w, ssd_conv_b, ssd_a_log_fwd, ssd_a_log_bwd,
           ssd_dt_bias_fwd, ssd_dt_bias_bwd, ssd_d, ssd_out_norm, w_out_even, conv_norm, conv_w_in, conv_w,
           conv_w_out, ffn_norm, router_w, expert_w_gate, expert_w_up, expert_w_down):
    b, seq, _ = x.shape
    n = b * seq
    depth = ffn_norm.shape[0]
    rc, r1, r2 = _rope_tables(seq)
    blk = np.arange(256) // HEAD_DIM
    bd = jnp.asarray((blk[:, None] == blk[None, :]) / HEAD_DIM, BF16)
    row = lambda v: v.reshape(1, -1).astype(F32)

    xf = x.reshape(n, D_MODEL)
    for layer in range(depth):
        i = layer // 2
        wr_hi, wr_lo = _router_split(router_w[layer])
        fg = row(ffn_norm[layer])
        if layer % 2 == 0:
            w = w_in_even[i].astype(BF16)
            o = np.cumsum([0, D_MODEL, D_MODEL, D_MODEL, D_MODEL, SSD_XBC, N_HEADS, N_HEADS])
            wq, wk, wv, wz, wx = (w[:, o[j]:o[j + 1]] for j in range(5))
            wd = jnp.pad(w[:, o[5]:o[7]], ((0, 0), (0, LANE - 2 * N_HEADS)))
            tile_heads = lambda g: row(jnp.tile(g, N_HEADS))
            q, k, v, z, xbc, dt = _in_even(xf, row(attn_norm[i]), wq, wk, wv, wz, wx, wd,
                                           tile_heads(q_norm[i]), tile_heads(k_norm[i]), bd, rc, r1, r2, seq)
            as3 = lambda t: t.reshape(b, seq, -1)
            attn = _attention(as3(q), as3(k), as3(v))
            cw = jnp.pad(ssd_conv_w[i], ((0, 8 - SSD_CONV), (0, 0)))
            act = _ssd_conv(as3(xbc), cw, row(ssd_conv_b[i]))
            pc = jnp.pad(jnp.stack([ssd_dt_bias_fwd[i], ssd_dt_bias_bwd[i], -jnp.exp(ssd_a_log_fwd[i]),
                                    -jnp.exp(ssd_a_log_bwd[i])], axis=1).astype(F32), ((0, 0), (0, LANE - 4)))
            y = _ssd(act, as3(dt), as3(z), pc, row(jnp.repeat(ssd_d[i], HEAD_DIM)), row(ssd_out_norm[i]))
            wo = w_out_even[i].astype(BF16)
            x1, hn, aff = _out_even(xf, attn.reshape(n, D_MODEL), y.reshape(n, D_MODEL), wo[:D_MODEL], wo[D_MODEL:],
                                    fg, wr_hi, wr_lo, b, seq)
        else:
            w = conv_w_in[i].astype(BF16)
            gb, cu = _in_odd(xf, row(conv_norm[i]), w[:, :D_MODEL], w[:, D_MODEL:2 * D_MODEL], w[:, 2 * D_MODEL:])
            cw = jnp.pad(conv_w[i], ((0, 8 - SHORT_CONV), (0, 0)))
            x1, hn, aff = _out_odd(xf, gb, cu, cw, conv_w_out[i].astype(BF16), fg, wr_hi, wr_lo, b, seq)
        wg = expert_w_gate[layer].astype(BF16)
        wu = expert_w_up[layer].astype(BF16)
        wd = expert_w_down[layer].astype(BF16)
        xf = _moe(x1.reshape(b, seq, D_MODEL), hn.reshape(b, seq, D_MODEL), aff, wg, wu, wd).reshape(n, D_MODEL)
    return xf.reshape(b, seq, D_MODEL)
```

```python
import functools
import math

import jax
import jax.numpy as jnp
import numpy as np
from jax import lax
from jax.experimental import pallas as pl
from jax.experimental.pallas import tpu as pltpu

F32, BF16, I32 = jnp.float32, jnp.bfloat16, jnp.int32

D_MODEL = 1024
N_HEADS = 16
HEAD_DIM = 64
ROPE_HALF = 8
ROPE_THETA = 500000.0
PATTERNS = ((128, 1), (512, 4), (2048, 16))
HALF_STEPS = 64
SSD_GROUPS = 2
SSD_STATE = 128
SSD_XBC = 1536
SSD_CONV = 5
CHUNK = 128
N_EXPERTS = 16
CAPACITY_FACTOR = 2
D_FF = 2048
SHORT_CONV = 3
EPS = 1e-6

LANE = 128
VMEM_LIMIT = 56 * 1024 * 1024

ROW_TILE = 256
ATTN_SUPER = 2048
ATTN_HALO = 1024
ATTN_TQ = 128
ATTN_TK = ATTN_TQ + 2 * HALF_STEPS
ATTN_UNROLL = 4
TOK_TILE = 512
SUB = TOK_TILE // LANE
SLOT_ALIGN = 16
WINDOW = 128
FFN_ROWS = 256


def _cparams(n_axes):
    return pltpu.CompilerParams(dimension_semantics=("arbitrary",) * n_axes, vmem_limit_bytes=VMEM_LIMIT)


def _const_spec(shape):
    nd = len(shape)
    return pl.BlockSpec(shape, lambda *_: (0,) * nd, pipeline_mode=pl.Buffered(1))


def _dot(a, b):
    return jnp.dot(a, b, preferred_element_type=F32)


def _dot_nt(a, b):
    return lax.dot_general(a, b, (((1,), (1,)), ((), ())), preferred_element_type=F32)


def _dot_tn(a, b):
    return lax.dot_general(a, b, (((0,), (0,)), ((), ())), preferred_element_type=F32)


def _split2(x):
    hi = x.astype(BF16)
    lo = (x - hi.astype(F32)).astype(BF16)
    return hi, lo


def _split3(x):
    hi = x.astype(BF16)
    r = x - hi.astype(F32)
    mid = r.astype(BF16)
    lo = (r - mid.astype(F32)).astype(BF16)
    return hi, mid, lo


def _dot3(x, m_bf16):
    hi, mid, lo = _split3(x)
    return _dot(hi, m_bf16) + _dot(mid, m_bf16) + _dot(lo, m_bf16)


def _rms(x, g):
    return x * lax.rsqrt(jnp.mean(x * x, axis=-1, keepdims=True) + EPS) * g


def _silu(x):
    return x * jax.nn.sigmoid(x)


def _in_even_kernel(x_ref, g_ref, wq_ref, wk_ref, wv_ref, wz_ref, wx_ref, wd_ref, qg_ref, kg_ref, bd_ref,
                    rc_ref, r1_ref, r2_ref, q_ref, k_ref, v_ref, z_ref, xbc_ref, dt_ref):
    hb = _rms(x_ref[...], g_ref[...]).astype(BF16)
    bd = bd_ref[...]
    rc, r1, r2 = rc_ref[...], r1_ref[...], r2_ref[...]

    def head_norm_rope(w_ref, gain_ref, out_ref):
        t = _dot(hb, w_ref[...])
        for c in range(D_MODEL // 256):
            tc = t[:, c * 256:(c + 1) * 256]
            sq_hi, sq_lo = _split2(tc * tc)
            ms = _dot(sq_hi, bd) + _dot(sq_lo, bd)
            tn = tc * lax.rsqrt(ms + EPS) * gain_ref[:, c * 256:(c + 1) * 256]
            for hh in range(2):
                u = tn[:, hh * LANE:(hh + 1) * LANE]
                r = u * rc + pltpu.roll(u, LANE - ROPE_HALF, 1) * r1 + pltpu.roll(u, ROPE_HALF, 1) * r2
                out_ref[:, c * 256 + hh * LANE:c * 256 + (hh + 1) * LANE] = r.astype(BF16)

    head_norm_rope(wq_ref, qg_ref, q_ref)
    head_norm_rope(wk_ref, kg_ref, k_ref)
    v_ref[...] = _dot(hb, wv_ref[...]).astype(BF16)
    z_ref[...] = _dot(hb, wz_ref[...])
    xbc_ref[...] = _dot(hb, wx_ref[...])
    dt_ref[...] = _dot(hb, wd_ref[...])


def _in_even(x, g, wq, wk, wv, wz, wx, wd, qg, kg, bd, rc, r1, r2, seq):
    n = x.shape[0]
    tm = ROW_TILE
    row = lambda w: pl.BlockSpec((tm, w), lambda i: (i, 0))
    tab = pl.BlockSpec((tm, LANE), lambda i: (i % (seq // tm), 0))
    outs = [jax.ShapeDtypeStruct((n, D_MODEL), BF16)] * 3 + [jax.ShapeDtypeStruct((n, D_MODEL), F32),
                                                            jax.ShapeDtypeStruct((n, SSD_XBC), F32),
                                                            jax.ShapeDtypeStruct((n, LANE), F32)]
    return pl.pallas_call(
        _in_even_kernel, grid=(n // tm,),
        in_specs=[row(D_MODEL), _const_spec((1, D_MODEL)), _const_spec(wq.shape), _const_spec(wk.shape),
                  _const_spec(wv.shape), _const_spec(wz.shape), _const_spec(wx.shape), _const_spec(wd.shape),
                  _const_spec((1, D_MODEL)), _const_spec((1, D_MODEL)), _const_spec((256, 256)), tab, tab, tab],
        out_specs=[row(D_MODEL)] * 4 + [row(SSD_XBC), row(LANE)], out_shape=outs,
        compiler_params=_cparams(1), name="in_even",
    )(x, g, wq, wk, wv, wz, wx, wd, qg, kg, bd, rc, r1, r2)


def _attn_kernel(q_ref, k0, k1, k2, k3, v0, v1, v2, v3, o_ref, qf, kf, vf, acc, mst, lst, *, seq):
    p0 = pl.program_id(2) * ATTN_SUPER
    qf[...] = q_ref[0].astype(F32) * (HEAD_DIM ** -0.5)
    for i, (kr, vr) in enumerate(((k0, v0), (k1, v1), (k2, v2), (k3, v3))):
        kf[i * ATTN_HALO:(i + 1) * ATTN_HALO, :] = kr[0].astype(F32)
        vf[i * ATTN_HALO:(i + 1) * ATTN_HALO, :] = vr[0].astype(F32)

    head_a = lax.broadcasted_iota(I32, (ATTN_TQ, LANE), 1) < HEAD_DIM
    off = lax.broadcasted_iota(I32, (ATTN_TQ, ATTN_TK), 1) - lax.broadcasted_iota(I32, (ATTN_TQ, ATTN_TK), 0)
    band = (off >= 0) & (off <= 2 * HALF_STEPS)
    band2 = jnp.concatenate([band, band], axis=0)
    colpos = lax.broadcasted_iota(I32, (1, ATTN_TK), 1)

    def visit(d, i, first, last):
        log_d = d.bit_length() - 1
        r = i & (d - 1)
        t = i >> log_d
        qs = r + d * ATTN_TQ * t
        ks = ATTN_HALO + r + d * (ATTN_TQ * t - HALF_STEPS)
        q = qf[pl.ds(qs, ATTN_TQ, stride=d), :]
        kt = kf[pl.ds(ks, ATTN_TK, stride=d), :].astype(BF16)
        vt = vf[pl.ds(ks, ATTN_TK, stride=d), :].astype(BF16)
        pos = p0 - ATTN_HALO + ks + d * colpos
        valid = band2 & (pos >= 0) & (pos < seq)
        q2 = jnp.concatenate([jnp.where(head_a, q, 0.0), jnp.where(head_a, 0.0, q)], axis=0).astype(BF16)
        s = jnp.where(valid, _dot_nt(q2, kt), -jnp.inf)
        m = jnp.max(s, axis=1, keepdims=True)
        p = jnp.exp(s - m)
        l = jnp.sum(p, axis=1, keepdims=True)
        n = _dot(p.astype(BF16), vt)
        m_loc = jnp.where(head_a, m[:ATTN_TQ], m[ATTN_TQ:])
        l_loc = jnp.where(head_a, l[:ATTN_TQ], l[ATTN_TQ:])
        n_loc = jnp.where(head_a, n[:ATTN_TQ], n[ATTN_TQ:])
        rows = pl.ds(qs, ATTN_TQ, stride=d)
        if first:
            m_new, l_new, a_new = m_loc, l_loc, n_loc
        else:
            m_old = mst[rows, :]
            m_new = jnp.maximum(m_old, m_loc)
            w_old = jnp.exp(m_old - m_new)
            w_loc = jnp.exp(m_loc - m_new)
            l_new = lst[rows, :] * w_old + l_loc * w_loc
            a_new = acc[rows, :] * w_old + n_loc * w_loc
        if last:
            acc[rows, :] = a_new / l_new
        else:
            mst[rows, :] = m_new
            lst[rows, :] = l_new
            acc[rows, :] = a_new

    n_visits = ATTN_SUPER // ATTN_TQ
    for idx, (_, d) in enumerate(PATTERNS):
        first, last = idx == 0, idx == len(PATTERNS) - 1

        def body(i, c, d=d, first=first, last=last):
            for u in range(ATTN_UNROLL):
                visit(d, i * ATTN_UNROLL + u, first, last)
            return c

        lax.fori_loop(0, n_visits // ATTN_UNROLL, body, 0)
    o_ref[0] = acc[...].astype(BF16)


def _attention(q, k, v):
    b, seq, _ = q.shape
    nblk = seq // ATTN_HALO
    ratio = ATTN_SUPER // ATTN_HALO

    def halo(i):
        return pl.BlockSpec((1, ATTN_HALO, LANE),
                            lambda bi, hp, j: (bi, jnp.clip(ratio * j - 1 + i, 0, nblk - 1), hp))

    main = pl.BlockSpec((1, ATTN_SUPER, LANE), lambda bi, hp, j: (bi, j, hp))
    return pl.pallas_call(
        functools.partial(_attn_kernel, seq=seq),
        grid=(b, D_MODEL // LANE, seq // ATTN_SUPER),
        in_specs=[main] + [halo(i) for i in range(4)] * 2,
        out_specs=main,
        out_shape=jax.ShapeDtypeStruct((b, seq, D_MODEL), BF16),
        scratch_shapes=[pltpu.VMEM((ATTN_SUPER, LANE), F32),
                        pltpu.VMEM((4 * ATTN_HALO, LANE), F32), pltpu.VMEM((4 * ATTN_HALO, LANE), F32),
                        pltpu.VMEM((ATTN_SUPER, LANE), F32), pltpu.VMEM((ATTN_SUPER, LANE), F32),
                        pltpu.VMEM((ATTN_SUPER, LANE), F32)],
        compiler_params=_cparams(3), name="dilated_attn",
    )(q, k, k, k, k, v, v, v, v)


def _shifted(cur, prev8, next8, s, first, last):
    n = cur.shape[0]
    if s == 0:
        return cur
    rows = lax.broadcasted_iota(I32, cur.shape, 0)
    out = pltpu.roll(cur, (-s) % n, 0)
    if s < 0:
        for j in range(-s):
            src = jnp.where(first, 0.0, prev8[8 + s + j:8 + s + j + 1, :])
            out = jnp.where(rows == j, src, out)
    else:
        for j in range(s):
            src = jnp.where(last, 0.0, next8[j:j + 1, :])
            out = jnp.where(rows == n - s + j, src, out)
    return out


def _ssd_conv_kernel(c_ref, p_ref, n_ref, w_ref, b_ref, o_ref):
    first = pl.program_id(1) == 0
    last = pl.program_id(1) == pl.num_programs(1) - 1
    for c in range(SSD_XBC // 256):
        sl = slice(c * 256, (c + 1) * 256)
        cur, prev8, next8 = c_ref[0, :, sl], p_ref[0, :, sl], n_ref[0, :, sl]
        y = b_ref[:, sl]
        for j in range(SSD_CONV):
            y = y + _shifted(cur, prev8, next8, j - SSD_CONV // 2, first, last) * w_ref[j:j + 1, sl]
        o_ref[0, :, sl] = _silu(y)


def _halo_specs(tm, width, seq):
    cur = pl.BlockSpec((1, tm, width), lambda b, i: (b, i, 0))
    prev = pl.BlockSpec((1, 8, width), lambda b, i: (b, jnp.maximum(i * (tm // 8) - 1, 0), 0))
    nxt = pl.BlockSpec((1, 8, width), lambda b, i: (b, jnp.minimum((i + 1) * (tm // 8), seq // 8 - 1), 0))
    return cur, prev, nxt


def _ssd_conv(xbc, w, bias):
    b, seq, width = xbc.shape
    tm = 512
    cur, prev, nxt = _halo_specs(tm, width, seq)
    return pl.pallas_call(
        _ssd_conv_kernel, grid=(b, seq // tm),
        in_specs=[cur, prev, nxt, pl.BlockSpec((8, width), lambda b, i: (0, 0)),
                  pl.BlockSpec((1, width), lambda b, i: (0, 0))],
        out_specs=cur, out_shape=jax.ShapeDtypeStruct(xbc.shape, F32),
        compiler_params=_cparams(2), name="ssd_conv",
    )(xbc, xbc, xbc, w, bias)


def _softplus(x):
    return jnp.maximum(x, 0.0) + jnp.log1p(jnp.exp(-jnp.abs(x)))


def _tri(kind):
    s = lax.broadcasted_iota(I32, (CHUNK, CHUNK), 0)
    l = lax.broadcasted_iota(I32, (CHUNK, CHUNK), 1)
    return {"le": s <= l, "ge": s >= l, "lt": s < l}[kind]


def _expand(cols, e2_ref):
    hi, lo = _split2(cols)
    return _dot(jnp.concatenate([hi, lo], axis=1), e2_ref[...])


def _ssd_bwd_kernel(xs_ref, b_ref, dt_ref, pc_ref, e2_ref, sb_ref, st):
    @pl.when(pl.program_id(1) == 0)
    def _():
        st[...] = jnp.zeros_like(st)

    sb_ref[0, 0] = st[...].astype(BF16)
    dt_t = dt_ref[0].T
    dtb = _softplus(dt_t[N_HEADS:2 * N_HEADS, :] + pc_ref[:, 1:2])
    a = dtb * pc_ref[:, 3:4]
    ex = _dot3(a, _tri("lt").astype(BF16))
    tot = ex[:, CHUNK - 1:CHUNK] + a[:, CHUNK - 1:CHUNK]
    rowform = jnp.concatenate([dtb * jnp.exp(ex), jnp.broadcast_to(jnp.exp(tot), (N_HEADS, CHUNK)),
                               jnp.zeros((CHUNK - 2 * N_HEADS, CHUNK), F32)], axis=0)
    ex2 = _expand(rowform.T, e2_ref)
    xw = (xs_ref[0] * ex2[:, :D_MODEL]).astype(BF16)
    half = D_MODEL // SSD_GROUPS
    upd = [_dot(b_ref[0, :, g * SSD_STATE:(g + 1) * SSD_STATE].T.astype(BF16), xw[:, g * half:(g + 1) * half])
           for g in range(SSD_GROUPS)]
    st[...] = st[...] * ex2[0:1, D_MODEL:] + jnp.concatenate(upd, axis=1)


def _ssd_fwd_kernel(xs_ref, b_ref, c_ref, dt_ref, z_ref, sb_ref, pc_ref, e3_ref, dexp_ref, on_ref, o_ref, st):
    @pl.when(pl.program_id(1) == 0)
    def _():
        st[...] = jnp.zeros_like(st)

    xs = xs_ref[0]
    dt_t = dt_ref[0].T
    dtf = _softplus(dt_t[0:N_HEADS, :] + pc_ref[:, 0:1])
    dtb = _softplus(dt_t[N_HEADS:2 * N_HEADS, :] + pc_ref[:, 1:2])
    af = dtf * pc_ref[:, 2:3]
    ab = dtb * pc_ref[:, 3:4]
    csf = _dot3(af, _tri("le").astype(BF16))
    rcs = _dot3(ab, _tri("ge").astype(BF16))
    totf = csf[:, CHUNK - 1:CHUNK]
    rowform = jnp.concatenate([dtf * jnp.exp(totf - csf), jnp.exp(csf), jnp.exp(rcs), csf, rcs,
                               jnp.zeros((CHUNK - 5 * N_HEADS, CHUNK), F32)], axis=0)
    cols = rowform.T
    ex3 = _expand(cols, e3_ref)
    w_state, e_f, e_b = ex3[:, :D_MODEL], ex3[:, D_MODEL:2 * D_MODEL], ex3[:, 2 * D_MODEL:]

    xb = xs.astype(BF16)
    lower, upper = _tri("ge"), _tri("le")
    head_a = lax.broadcasted_iota(I32, (CHUNK, LANE), 1) < HEAD_DIM
    half = D_MODEL // SSD_GROUPS
    hpg = N_HEADS // SSD_GROUPS
    st_all = st[...]
    sb_all = sb_ref[0, 0]
    ys = []
    b_t = []
    for g in range(SSD_GROUPS):
        bg = b_ref[0, :, g * SSD_STATE:(g + 1) * SSD_STATE]
        cg = c_ref[0, :, g * SSD_STATE:(g + 1) * SSD_STATE].astype(BF16)
        b_t.append(bg.T.astype(BF16))
        gm = _dot_nt(cg, bg.astype(BF16))
        states = jnp.concatenate([st_all[:, g * half:(g + 1) * half].astype(BF16),
                                  sb_all[:, g * half:(g + 1) * half]], axis=1)
        off = _dot(cg, states)
        y_off = (off[:, :half] * e_f[:, g * half:(g + 1) * half]
                 + off[:, half:] * e_b[:, g * half:(g + 1) * half])
        for pair in range(hpg // 2):
            ms = []
            for h in (g * hpg + 2 * pair, g * hpg + 2 * pair + 1):
                dec_f = jnp.where(lower, jnp.exp(cols[:, 3 * N_HEADS + h:3 * N_HEADS + h + 1] - csf[h:h + 1, :]), 0.0)
                dec_b = jnp.where(upper, jnp.exp(cols[:, 4 * N_HEADS + h:4 * N_HEADS + h + 1] - rcs[h:h + 1, :]), 0.0)
                ms.append((gm * (dec_f * dtf[h:h + 1, :] + dec_b * dtb[h:h + 1, :])).astype(BF16))
            lo = g * half + pair * LANE
            xp = xb[:, lo:lo + LANE]
            ys.append(jnp.where(head_a, _dot(ms[0], xp), _dot(ms[1], xp))
                      + y_off[:, pair * LANE:(pair + 1) * LANE])
    y = jnp.concatenate(ys, axis=1) + xs * dexp_ref[...]
    yz = y * _silu(z_ref[0])
    o_ref[0] = _rms(yz, on_ref[...]).astype(BF16)

    xw = (xs * w_state).astype(BF16)
    upd = [_dot(b_t[g], xw[:, g * half:(g + 1) * half]) for g in range(SSD_GROUPS)]
    st[...] = st_all * e_f[CHUNK - 1:CHUNK, :] + jnp.concatenate(upd, axis=1)


def _ssd(xbc_act, dt, z, pc, dexp, out_norm):
    b, seq, _ = xbc_act.shape
    nc = seq // CHUNK
    sel = np.zeros((2 * CHUNK, 3 * D_MODEL), np.float32)
    for part in range(3):
        for h in range(N_HEADS):
            for rep in range(2):
                sel[rep * CHUNK + part * N_HEADS + h, part * D_MODEL + h * HEAD_DIM:part * D_MODEL + (h + 1) * HEAD_DIM] = 1.0
    e3 = jnp.asarray(sel, BF16)
    e2 = jnp.asarray(sel[:, :2 * D_MODEL], BF16)

    rev = lambda bi, c: (bi, nc - 1 - c, 0)
    sb = pl.pallas_call(
        _ssd_bwd_kernel, grid=(b, nc),
        in_specs=[pl.BlockSpec((1, CHUNK, D_MODEL), rev),
                  pl.BlockSpec((1, CHUNK, 2 * SSD_STATE), lambda bi, c: (bi, nc - 1 - c, D_MODEL // (2 * SSD_STATE))),
                  pl.BlockSpec((1, CHUNK, LANE), rev), _const_spec(pc.shape), _const_spec(e2.shape)],
        out_specs=pl.BlockSpec((1, 1, SSD_STATE, D_MODEL), lambda bi, c: (bi, nc - 1 - c, 0, 0)),
        out_shape=jax.ShapeDtypeStruct((b, nc, SSD_STATE, D_MODEL), BF16),
        scratch_shapes=[pltpu.VMEM((SSD_STATE, D_MODEL), F32)],
        compiler_params=_cparams(2), name="ssd_bwd_state",
    )(xbc_act, xbc_act, dt, pc, e2)

    fwd = lambda bi, c: (bi, c, 0)
    return pl.pallas_call(
        _ssd_fwd_kernel, grid=(b, nc),
        in_specs=[pl.BlockSpec((1, CHUNK, D_MODEL), fwd),
                  pl.BlockSpec((1, CHUNK, 2 * SSD_STATE), lambda bi, c: (bi, c, D_MODEL // (2 * SSD_STATE))),
                  pl.BlockSpec((1, CHUNK, 2 * SSD_STATE), lambda bi, c: (bi, c, D_MODEL // (2 * SSD_STATE) + 1)),
                  pl.BlockSpec((1, CHUNK, LANE), fwd), pl.BlockSpec((1, CHUNK, D_MODEL), fwd),
                  pl.BlockSpec((1, 1, SSD_STATE, D_MODEL), lambda bi, c: (bi, c, 0, 0)),
                  _const_spec(pc.shape), _const_spec(e3.shape), _const_spec((1, D_MODEL)), _const_spec((1, D_MODEL))],
        out_specs=pl.BlockSpec((1, CHUNK, D_MODEL), fwd),
        out_shape=jax.ShapeDtypeStruct((b, seq, D_MODEL), BF16),
        scratch_shapes=[pltpu.VMEM((SSD_STATE, D_MODEL), F32)],
        compiler_params=_cparams(2), name="ssd_fwd",
    )(xbc_act, xbc_act, xbc_act, dt, z, sb, pc, e3, dexp, out_norm)


def _norm_and_route(x1, fg_ref, wr_hi_ref, wr_lo_ref, hn_ref, aff_ref):
    hn = _rms(x1, fg_ref[...])
    hi, lo = _split2(hn)
    hn_ref[...] = hi
    logits = _dot(hi, wr_hi_ref[...]) + _dot(lo, wr_hi_ref[...]) + _dot(hi, wr_lo_ref[...])
    lt = logits.T[0:N_EXPERTS, :]
    e = jnp.exp(lt - jnp.max(lt, axis=0, keepdims=True))
    aff_ref[0] = e / jnp.sum(e, axis=0, keepdims=True)


def _out_even_kernel(x_ref, a_ref, y_ref, wa_ref, wy_ref, fg_ref, wr_hi_ref, wr_lo_ref, x1_ref, hn_ref, aff_ref):
    x1 = x_ref[...] + _dot(a_ref[...], wa_ref[...]) + _dot(y_ref[...], wy_ref[...])
    x1_ref[...] = x1
    _norm_and_route(x1, fg_ref, wr_hi_ref, wr_lo_ref, hn_ref, aff_ref)


def _out_odd_kernel(x_ref, gb_ref, cu_ref, cp_ref, cn_ref, cw_ref, wo_ref, fg_ref, wr_hi_ref, wr_lo_ref,
                    x1_ref, hn_ref, aff_ref, *, tiles_per_seq):
    i = pl.program_id(0) % tiles_per_seq
    first, last = i == 0, i == tiles_per_seq - 1
    cur, prev8, next8 = cu_ref[...], cp_ref[...], cn_ref[...]
    conv = sum(_shifted(cur, prev8, next8, j - SHORT_CONV // 2, first, last) * cw_ref[j:j + 1, :]
               for j in range(SHORT_CONV))
    x1 = x_ref[...] + _dot((gb_ref[...] * conv).astype(BF16), wo_ref[...])
    x1_ref[...] = x1
    _norm_and_route(x1, fg_ref, wr_hi_ref, wr_lo_ref, hn_ref, aff_ref)


def _route_outs(n, b, seq, tm):
    row = pl.BlockSpec((tm, D_MODEL), lambda i: (i, 0))
    aff = pl.BlockSpec((1, N_EXPERTS, tm), lambda i: (i // (seq // tm), 0, i % (seq // tm)))
    shapes = [jax.ShapeDtypeStruct((n, D_MODEL), F32), jax.ShapeDtypeStruct((n, D_MODEL), BF16),
              jax.ShapeDtypeStruct((b, N_EXPERTS, seq), F32)]
    return [row, row, aff], shapes


def _out_even(x, attn, y, wa, wy, fg, wr_hi, wr_lo, b, seq):
    n = x.shape[0]
    tm = ROW_TILE
    row = pl.BlockSpec((tm, D_MODEL), lambda i: (i, 0))
    out_specs, shapes = _route_outs(n, b, seq, tm)
    return pl.pallas_call(
        _out_even_kernel, grid=(n // tm,),
        in_specs=[row, row, row, _const_spec(wa.shape), _const_spec(wy.shape), _const_spec((1, D_MODEL)),
                  _const_spec(wr_hi.shape), _const_spec(wr_lo.shape)],
        out_specs=out_specs, out_shape=shapes, compiler_params=_cparams(1), name="out_even",
    )(x, attn, y, wa, wy, fg, wr_hi, wr_lo)


def _out_odd(x, gb, cu, cw, wo, fg, wr_hi, wr_lo, b, seq):
    n = x.shape[0]
    tm = ROW_TILE
    row = pl.BlockSpec((tm, D_MODEL), lambda i: (i, 0))
    prev = pl.BlockSpec((8, D_MODEL), lambda i: (jnp.maximum(i * (tm // 8) - 1, 0), 0))
    nxt = pl.BlockSpec((8, D_MODEL), lambda i: (jnp.minimum((i + 1) * (tm // 8), n // 8 - 1), 0))
    out_specs, shapes = _route_outs(n, b, seq, tm)
    return pl.pallas_call(
        functools.partial(_out_odd_kernel, tiles_per_seq=seq // tm), grid=(n // tm,),
        in_specs=[row, row, row, prev, nxt, pl.BlockSpec((8, D_MODEL), lambda i: (0, 0)), _const_spec(wo.shape),
                  _const_spec((1, D_MODEL)), _const_spec(wr_hi.shape), _const_spec(wr_lo.shape)],
        out_specs=out_specs, out_shape=shapes, compiler_params=_cparams(1), name="out_odd",
    )(x, gb, cu, cu, cu, cw, wo, fg, wr_hi, wr_lo)


def _in_odd_kernel(x_ref, g_ref, wb_ref, wc_ref, wu_ref, gb_ref, cu_ref):
    hb = _rms(x_ref[...], g_ref[...]).astype(BF16)
    gb_ref[...] = _dot(hb, wb_ref[...])
    cu_ref[...] = _dot(hb, wc_ref[...]) * _dot(hb, wu_ref[...])


def _in_odd(x, g, wb, wc, wu):
    n = x.shape[0]
    tm = ROW_TILE
    row = pl.BlockSpec((tm, D_MODEL), lambda i: (i, 0))
    return pl.pallas_call(
        _in_odd_kernel, grid=(n // tm,),
        in_specs=[row, _const_spec((1, D_MODEL)), _const_spec(wb.shape), _const_spec(wc.shape), _const_spec(wu.shape)],
        out_specs=[row, row], out_shape=[jax.ShapeDtypeStruct((n, D_MODEL), F32)] * 2,
        compiler_params=_cparams(1), name="in_odd",
    )(x, g, wb, wc, wu)


def _count(mask):
    return jnp.sum(jnp.sum(mask.astype(F32), axis=0, keepdims=True), axis=1, keepdims=True)


def _route_kernel(aff_ref, incl_ref, ones_ref, strict_ref, local_ref, group_ref, first_ref, slot_ref, off_ref, end_ref,
                  *, cap):
    for e in range(N_EXPERTS):
        bits = pltpu.bitcast(aff_ref[0, e], I32)

        def step(i, thr):
            cand = thr | (jnp.int32(1) << (30 - i))
            return jnp.where(_count(bits >= cand) >= cap, cand, thr)

        thr = lax.fori_loop(0, 31, step, jnp.zeros((1, 1), I32))
        gt = bits > thr
        eq = (bits == thr).astype(BF16)
        eq_rank = _dot(eq, incl_ref[...]) + _dot(strict_ref[...], _dot(eq, ones_ref[...]).astype(BF16))
        sel = (gt | ((bits == thr) & (eq_rank <= cap - _count(gt)))).astype(BF16)
        within = _dot(sel, incl_ref[...])
        totals = _dot(sel, ones_ref[...]).astype(BF16)
        local = _dot(local_ref[...], totals)
        cnt = _dot(group_ref[...], totals)
        padded = jnp.floor((cnt + (SLOT_ALIGN - 1)) * (1.0 / SLOT_ALIGN)) * SLOT_ALIGN
        start = _dot(first_ref[...], padded.astype(BF16))
        slot_ref[0, e] = jnp.where(sel > 0, (start + local + within).astype(I32) - 1, -1)
        off_ref[0, e:e + 1, :] = start.T[0:1, :].astype(I32)
        end_ref[0, e:e + 1, :] = (start + padded).T[0:1, :].astype(I32)


def _route(aff, cap):
    b, _, seq = aff.shape
    nt = seq // LANE
    tri = np.arange(LANE)
    tt = np.arange(nt)
    grp = tt // SUB
    as_bf16 = lambda m: jnp.asarray(m, BF16)
    incl = as_bf16(tri[:, None] <= tri[None, :])
    strict = as_bf16(tt[None, :] < tt[:, None])
    local = as_bf16((tt[None, :] < tt[:, None]) & (grp[None, :] == grp[:, None]))
    group = as_bf16(grp[None, :] == grp[:, None])
    first = as_bf16((grp[None, :] < grp[:, None]) & (tt[None, :] % SUB == 0))
    ones = jnp.ones((LANE, LANE), BF16)
    tiles = pl.BlockSpec((1, N_EXPERTS, nt, LANE), lambda i: (i, 0, 0, 0))
    rows = pl.BlockSpec((1, N_EXPERTS, nt), lambda i: (i, 0, 0))
    return pl.pallas_call(
        functools.partial(_route_kernel, cap=cap), grid=(b,),
        in_specs=[tiles, _const_spec((LANE, LANE)), _const_spec((LANE, LANE))] + [_const_spec((nt, nt))] * 4,
        out_specs=[tiles, rows, rows],
        out_shape=[jax.ShapeDtypeStruct((b, N_EXPERTS, nt, LANE), I32)] + [jax.ShapeDtypeStruct((b, N_EXPERTS, nt), I32)] * 2,
        compiler_params=_cparams(1), name="route",
    )(aff.reshape(b, N_EXPERTS, nt, LANE), incl, ones, strict, local, group, first)


def _tile_row(ref, e, j):
    return jnp.concatenate([ref[0, e, j * SUB + r:j * SUB + r + 1, :] for r in range(SUB)], axis=1)


def _one_hot(slots, base):
    return (slots == lax.broadcasted_iota(I32, (WINDOW, slots.shape[1]), 0) + base).astype(BF16)


def _gather_kernel(off_ref, end_ref, slot_ref, aff_ref, hn_ref, xe_ref):
    bi, e = pl.program_id(0), pl.program_id(2)
    half = hn_ref.shape[2]
    xe_ref[...] = jnp.zeros_like(xe_ref)
    for j in range(hn_ref.shape[1] // TOK_TILE):
        first = off_ref[bi, e, j * SUB]
        n_pad = end_ref[bi, e, j * SUB] - first
        slots = _tile_row(slot_ref, 0, j)
        parts = [p.astype(F32) for p in _split3(_tile_row(aff_ref, 0, j))]
        gates = jnp.concatenate(parts + [jnp.zeros((LANE - 3, TOK_TILE), F32)], axis=0).astype(BF16)
        tokens = hn_ref[0, j * TOK_TILE:(j + 1) * TOK_TILE, :]

        def window(w):
            base = pl.multiple_of(first + w * WINDOW, SLOT_ALIGN)
            p = _one_hot(slots, base)
            xe_ref[0, 0, pl.ds(base, WINDOW), 0:half] = _dot(p, tokens).astype(BF16)
            xe_ref[0, 0, pl.ds(base, WINDOW), half:] = _dot_nt(p, gates).astype(BF16)

        window(0)
        for w in range(1, TOK_TILE // WINDOW):
            pl.when(n_pad > w * WINDOW)(functools.partial(window, w))


def _gather(off, end, slot, aff4, hn):
    b, seq, _ = hn.shape
    nt = seq // LANE
    half = D_MODEL // 2
    cap_pad = _cap_pad(seq)
    tiles = pl.BlockSpec((1, 1, nt, LANE), lambda bi, h, e, *_: (bi, e, 0, 0))
    return pl.pallas_call(
        _gather_kernel,
        grid_spec=pltpu.PrefetchScalarGridSpec(
            num_scalar_prefetch=2, grid=(b, 2, N_EXPERTS),
            in_specs=[tiles, tiles, pl.BlockSpec((1, seq, half), lambda bi, h, e, *_: (bi, 0, h))],
            out_specs=pl.BlockSpec((1, 1, cap_pad, half + LANE), lambda bi, h, e, *_: (bi, e, 0, h))),
        out_shape=jax.ShapeDtypeStruct((b, N_EXPERTS, cap_pad, 2 * (half + LANE)), BF16),
        compiler_params=_cparams(3), name="moe_gather",
    )(off, end, slot, aff4, hn)


def _ffn_kernel(end_ref, xa_ref, xb_ref, wg_ref, wu_ref, wd_ref, y_ref):
    e, bi, r = pl.program_id(0), pl.program_id(1), pl.program_id(2)
    used = end_ref[bi, e, end_ref.shape[2] - 1]
    half = D_MODEL // 2

    @pl.when(r * FFN_ROWS < used)
    def _():
        xe = jnp.concatenate([xa_ref[0, 0, :, 0:half], xb_ref[0, 0, :, 0:half]], axis=1)
        hid = (_silu(_dot(xe, wg_ref[0])) * _dot(xe, wu_ref[0])).astype(BF16)
        g = xa_ref[0, 0, :, half:].astype(F32)
        gate = g[:, 0:1] + g[:, 1:2] + g[:, 2:3]
        y_ref[0, 0] = (_dot(hid, wd_ref[0]) * gate).astype(BF16)

    @pl.when(r * FFN_ROWS >= used)
    def _():
        y_ref[...] = jnp.zeros_like(y_ref)


def _ffn(end, xe, wg, wu, wd):
    b, ne, cap_pad, _ = xe.shape
    xhalf = lambda h: pl.BlockSpec((1, 1, FFN_ROWS, D_MODEL // 2 + LANE), lambda e, bi, r, *_: (bi, e, r, h))
    wspec = lambda s: pl.BlockSpec((1,) + s, lambda e, bi, r, *_: (e, 0, 0))
    return pl.pallas_call(
        _ffn_kernel,
        grid_spec=pltpu.PrefetchScalarGridSpec(
            num_scalar_prefetch=1, grid=(ne, b, cap_pad // FFN_ROWS),
            in_specs=[xhalf(0), xhalf(1), wspec((D_MODEL, D_FF)), wspec((D_MODEL, D_FF)), wspec((D_FF, D_MODEL))],
            out_specs=pl.BlockSpec((1, 1, FFN_ROWS, D_MODEL), lambda e, bi, r, *_: (bi, e, r, 0))),
        out_shape=jax.ShapeDtypeStruct((b, ne, cap_pad, D_MODEL), BF16),
        compiler_params=_cparams(3), name="moe_ffn",
    )(end, xe, xe, wg, wu, wd)


def _combine_kernel(off_ref, end_ref, slot_ref, y_hbm, x1_ref, o_ref, win, extra, sem, xsem):
    bi, j = pl.program_id(0), pl.program_id(1)
    n_j = pl.num_programs(1)
    step = bi * n_j + j
    cur = step % 2

    def window_copy(bb, jj, e, buf):
        start = pl.multiple_of(off_ref[bb, e, jj * SUB], SLOT_ALIGN)
        return pltpu.make_async_copy(y_hbm.at[bb, e, pl.ds(start, WINDOW), :], win.at[buf, e], sem.at[buf, e])

    @pl.when(step == 0)
    def _():
        for e in range(N_EXPERTS):
            window_copy(bi, j, e, cur).start()

    @pl.when(step + 1 < pl.num_programs(0) * n_j)
    def _():
        nxt = step + 1
        for e in range(N_EXPERTS):
            window_copy(nxt // n_j, nxt % n_j, e, 1 - cur).start()

    ps = []
    for e in range(N_EXPERTS):
        window_copy(bi, j, e, cur).wait()
        ps.append(_one_hot(slot_ref[0, 0, e:e + 1, :], off_ref[bi, e, j * SUB]))
    p_all = jnp.concatenate(ps, axis=0)
    y_all = win[cur].reshape(N_EXPERTS * WINDOW, D_MODEL)
    o_ref[0] = x1_ref[0] + _dot_tn(p_all, y_all)

    def overflow(e, c):
        first = off_ref[bi, e, j * SUB]
        n_win = (end_ref[bi, e, j * SUB] - first + WINDOW - 1) // WINDOW
        slots = slot_ref[0, 0, pl.ds(e, 1), :]

        def one(w, c2):
            base = pl.multiple_of(first + w * WINDOW, SLOT_ALIGN)
            cp = pltpu.make_async_copy(y_hbm.at[bi, e, pl.ds(base, WINDOW), :], extra, xsem.at[0])
            cp.start()
            cp.wait()
            o_ref[0] += _dot_tn(_one_hot(slots, base), extra[...])
            return c2

        return lax.fori_loop(1, n_win, one, c)

    lax.fori_loop(0, N_EXPERTS, overflow, 0)


def _combine(off, end, slot, y, x1):
    b, seq, _ = x1.shape
    nt = seq // LANE
    tile = pl.BlockSpec((1, TOK_TILE, D_MODEL), lambda bi, j, *_: (bi, j, 0))
    return pl.pallas_call(
        _combine_kernel,
        grid_spec=pltpu.PrefetchScalarGridSpec(
            num_scalar_prefetch=2, grid=(b, seq // TOK_TILE),
            in_specs=[pl.BlockSpec((1, 1, N_EXPERTS, TOK_TILE), lambda bi, j, *_: (bi, j, 0, 0)),
                      pl.BlockSpec(memory_space=pl.ANY), tile],
            out_specs=tile,
            scratch_shapes=[pltpu.VMEM((2, N_EXPERTS, WINDOW, D_MODEL), BF16), pltpu.VMEM((WINDOW, D_MODEL), BF16),
                            pltpu.SemaphoreType.DMA((2, N_EXPERTS)), pltpu.SemaphoreType.DMA((1,))]),
        out_shape=jax.ShapeDtypeStruct((b, seq, D_MODEL), F32),
        compiler_params=_cparams(2), name="moe_combine",
    )(off, end, jnp.swapaxes(slot.reshape(b, N_EXPERTS, seq // TOK_TILE, TOK_TILE), 1, 2), y, x1)


def _cap_pad(seq):
    cap = CAPACITY_FACTOR * seq // N_EXPERTS
    worst = cap + (seq // TOK_TILE) * (SLOT_ALIGN - 1) + WINDOW
    return -(-worst // FFN_ROWS) * FFN_ROWS


def _moe(x1, hn, aff, wg, wu, wd):
    b, seq, _ = x1.shape
    cap = CAPACITY_FACTOR * seq // N_EXPERTS
    slot, off, end = _route(aff, cap)
    xe = _gather(off, end, slot, aff.reshape(b, N_EXPERTS, seq // LANE, LANE), hn)
    y = _ffn(end, xe, wg, wu, wd)
    return _combine(off, end, slot, y, x1)


def _rope_tables(seq):
    inv_freq = ROPE_THETA ** (-jnp.arange(ROPE_HALF, dtype=F32) * 2.0 / (2 * ROPE_HALF))
    ang = jnp.arange(seq, dtype=F32)[:, None] * inv_freq[None, :]
    cos, sin = jnp.cos(ang), jnp.sin(ang)
    z = lambda w: jnp.zeros((seq, w), F32)
    rest = HEAD_DIM - 2 * ROPE_HALF
    rc = jnp.concatenate([cos, cos, jnp.ones((seq, rest), F32)], axis=1)
    r1 = jnp.concatenate([-sin, z(ROPE_HALF + rest)], axis=1)
    r2 = jnp.concatenate([z(ROPE_HALF), sin, z(rest)], axis=1)
    return tuple(jnp.tile(t, (1, LANE // HEAD_DIM)) for t in (rc, r1, r2))


def _router_split(w):
    wp = jnp.pad(w, ((0, 0), (0, LANE - N_EXPERTS)))
    hi = wp.astype(BF16)
    return hi, (wp - hi.astype(F32)).astype(BF16)


def kernel(x, attn_norm, w_in_even, q_norm, k_norm, ssd_conv_w, ssd_conv_b, ssd_a_log_fwd, ssd_a_log_bwd,
           ssd_dt_bias_fwd, ssd_dt_bias_bwd, ssd_d, ssd_out_norm, w_out_even, conv_norm, conv_w_in, conv_w,
           conv_w_out, ffn_norm, router_w, expert_w_gate, expert_w_up, expert_w_down):
    b, seq, _ = x.shape
    n = b * seq
    depth = ffn_norm.shape[0]
    rc, r1, r2 = _rope_tables(seq)
    blk = np.arange(256) // HEAD_DIM
    bd = jnp.asarray((blk[:, None] == blk[None, :]) / HEAD_DIM, BF16)
    row = lambda v: v.reshape(1, -1).astype(F32)

    xf = x.reshape(n, D_MODEL)
    for layer in range(depth):
        i = layer // 2
        wr_hi, wr_lo = _router_split(router_w[layer])
        fg = row(ffn_norm[layer])
        if layer % 2 == 0:
            w = w_in_even[i].astype(BF16)
            o = np.cumsum([0, D_MODEL, D_MODEL, D_MODEL, D_MODEL, SSD_XBC, N_HEADS, N_HEADS])
            wq, wk, wv, wz, wx = (w[:, o[j]:o[j + 1]] for j in range(5))
            wd = jnp.pad(w[:, o[5]:o[7]], ((0, 0), (0, LANE - 2 * N_HEADS)))
            tile_heads = lambda g: row(jnp.tile(g, N_HEADS))
            q, k, v, z, xbc, dt = _in_even(xf, row(attn_norm[i]), wq, wk, wv, wz, wx, wd,
                                           tile_heads(q_norm[i]), tile_heads(k_norm[i]), bd, rc, r1, r2, seq)
            as3 = lambda t: t.reshape(b, seq, -1)
            attn = _attention(as3(q), as3(k), as3(v))
            cw = jnp.pad(ssd_conv_w[i], ((0, 8 - SSD_CONV), (0, 0)))
            act = _ssd_conv(as3(xbc), cw, row(ssd_conv_b[i]))
            pc = jnp.pad(jnp.stack([ssd_dt_bias_fwd[i], ssd_dt_bias_bwd[i], -jnp.exp(ssd_a_log_fwd[i]),
                                    -jnp.exp(ssd_a_log_bwd[i])], axis=1).astype(F32), ((0, 0), (0, LANE - 4)))
            y = _ssd(act, as3(dt), as3(z), pc, row(jnp.repeat(ssd_d[i], HEAD_DIM)), row(ssd_out_norm[i]))
            wo = w_out_even[i].astype(BF16)
            x1, hn, aff = _out_even(xf, attn.reshape(n, D_MODEL), y.reshape(n, D_MODEL), wo[:D_MODEL], wo[D_MODEL:],
                                    fg, wr_hi, wr_lo, b, seq)
        else:
            w = conv_w_in[i].astype(BF16)
            gb, cu = _in_odd(xf, row(conv_norm[i]), w[:, :D_MODEL], w[:, D_MODEL:2 * D_MODEL], w[:, 2 * D_MODEL:])
            cw = jnp.pad(conv_w[i], ((0, 8 - SHORT_CONV), (0, 0)))
            x1, hn, aff = _out_odd(xf, gb, cu, cw, conv_w_out[i].astype(BF16), fg, wr_hi, wr_lo, b, seq)
        wg = expert_w_gate[layer].astype(BF16)
        wu = expert_w_up[layer].astype(BF16)
        wd = expert_w_down[layer].astype(BF16)
        xf = _moe(x1.reshape(b, seq, D_MODEL), hn.reshape(b, seq, D_MODEL), aff, wg, wu, wd).reshape(n, D_MODEL)
    return xf.reshape(b, seq, D_MODEL)
```

```python
import functools
import math

import jax
import jax.numpy as jnp
import numpy as np
from jax import lax
from jax.experimental import pallas as pl
from jax.experimental.pallas import tpu as pltpu

F32, BF16, I32 = jnp.float32, jnp.bfloat16, jnp.int32

D_MODEL = 1024
N_HEADS = 16
HEAD_DIM = 64
ROPE_HALF = 8
ROPE_THETA = 500000.0
PATTERNS = ((128, 1), (512, 4), (2048, 16))
HALF_STEPS = 64
SSD_GROUPS = 2
SSD_STATE = 128
SSD_XBC = 1536
SSD_CONV = 5
CHUNK = 128
N_EXPERTS = 16
CAPACITY_FACTOR = 2
D_FF = 2048
SHORT_CONV = 3
EPS = 1e-6

LANE = 128
VMEM_LIMIT = 56 * 1024 * 1024

ROW_TILE = 256
ATTN_SUPER = 2048
ATTN_HALO = 1024
ATTN_TQ = 128
ATTN_TK = ATTN_TQ + 2 * HALF_STEPS
ATTN_UNROLL = 4
TOK_TILE = 512
SUB = TOK_TILE // LANE
SLOT_ALIGN = 16
WINDOW = 128
FFN_ROWS = 256


def _cparams(n_axes):
    return pltpu.CompilerParams(dimension_semantics=("arbitrary",) * n_axes, vmem_limit_bytes=VMEM_LIMIT)


def _const_spec(shape):
    nd = len(shape)
    return pl.BlockSpec(shape, lambda *_: (0,) * nd, pipeline_mode=pl.Buffered(1))


def _dot(a, b):
    return jnp.dot(a, b, preferred_element_type=F32)


def _dot_nt(a, b):
    return lax.dot_general(a, b, (((1,), (1,)), ((), ())), preferred_element_type=F32)


def _dot_tn(a, b):
    return lax.dot_general(a, b, (((0,), (0,)), ((), ())), preferred_element_type=F32)


def _split2(x):
    hi = x.astype(BF16)
    lo = (x - hi.astype(F32)).astype(BF16)
    return hi, lo


def _split3(x):
    hi = x.astype(BF16)
    r = x - hi.astype(F32)
    mid = r.astype(BF16)
    lo = (r - mid.astype(F32)).astype(BF16)
    return hi, mid, lo


def _dot3(x, m_bf16):
    hi, mid, lo = _split3(x)
    return _dot(hi, m_bf16) + _dot(mid, m_bf16) + _dot(lo, m_bf16)


def _rms(x, g):
    return x * lax.rsqrt(jnp.mean(x * x, axis=-1, keepdims=True) + EPS) * g


def _silu(x):
    return x * jax.nn.sigmoid(x)


def _in_even_kernel(x_ref, g_ref, wq_ref, wk_ref, wv_ref, wz_ref, wx_ref, wd_ref, qg_ref, kg_ref, bd_ref,
                    rc_ref, r1_ref, r2_ref, q_ref, k_ref, v_ref, z_ref, xbc_ref, dt_ref):
    hb = _rms(x_ref[...], g_ref[...]).astype(BF16)
    bd = bd_ref[...]
    rc, r1, r2 = rc_ref[...], r1_ref[...], r2_ref[...]

    def head_norm_rope(w_ref, gain_ref, out_ref):
        t = _dot(hb, w_ref[...])
        for c in range(D_MODEL // 256):
            tc = t[:, c * 256:(c + 1) * 256]
            sq_hi, sq_lo = _split2(tc * tc)
            ms = _dot(sq_hi, bd) + _dot(sq_lo, bd)
            tn = tc * lax.rsqrt(ms + EPS) * gain_ref[:, c * 256:(c + 1) * 256]
            for hh in range(2):
                u = tn[:, hh * LANE:(hh + 1) * LANE]
                r = u * rc + pltpu.roll(u, LANE - ROPE_HALF, 1) * r1 + pltpu.roll(u, ROPE_HALF, 1) * r2
                out_ref[:, c * 256 + hh * LANE:c * 256 + (hh + 1) * LANE] = r.astype(BF16)

    head_norm_rope(wq_ref, qg_ref, q_ref)
    head_norm_rope(wk_ref, kg_ref, k_ref)
    v_ref[...] = _dot(hb, wv_ref[...]).astype(BF16)
    z_ref[...] = _dot(hb, wz_ref[...])
    xbc_ref[...] = _dot(hb, wx_ref[...])
    dt_ref[...] = _dot(hb, wd_ref[...])


def _in_even(x, g, wq, wk, wv, wz, wx, wd, qg, kg, bd, rc, r1, r2, seq):
    n = x.shape[0]
    tm = ROW_TILE
    row = lambda w: pl.BlockSpec((tm, w), lambda i: (i, 0))
    tab = pl.BlockSpec((tm, LANE), lambda i: (i % (seq // tm), 0))
    outs = [jax.ShapeDtypeStruct((n, D_MODEL), BF16)] * 3 + [jax.ShapeDtypeStruct((n, D_MODEL), F32),
                                                            jax.ShapeDtypeStruct((n, SSD_XBC), F32),
                                                            jax.ShapeDtypeStruct((n, LANE), F32)]
    return pl.pallas_call(
        _in_even_kernel, grid=(n // tm,),
        in_specs=[row(D_MODEL), _const_spec((1, D_MODEL)), _const_spec(wq.shape), _const_spec(wk.shape),
                  _const_spec(wv.shape), _const_spec(wz.shape), _const_spec(wx.shape), _const_spec(wd.shape),
                  _const_spec((1, D_MODEL)), _const_spec((1, D_MODEL)), _const_spec((256, 256)), tab, tab, tab],
        out_specs=[row(D_MODEL)] * 4 + [row(SSD_XBC), row(LANE)], out_shape=outs,
        compiler_params=_cparams(1), name="in_even",
    )(x, g, wq, wk, wv, wz, wx, wd, qg, kg, bd, rc, r1, r2)


def _attn_kernel(q_ref, k0, k1, k2, k3, v0, v1, v2, v3, o_ref, qf, kf, vf, acc, mst, lst, *, seq):
    p0 = pl.program_id(2) * ATTN_SUPER
    qf[...] = q_ref[0].astype(F32) * (HEAD_DIM ** -0.5)
    for i, (kr, vr) in enumerate(((k0, v0), (k1, v1), (k2, v2), (k3, v3))):
        kf[i * ATTN_HALO:(i + 1) * ATTN_HALO, :] = kr[0].astype(F32)
        vf[i * ATTN_HALO:(i + 1) * ATTN_HALO, :] = vr[0].astype(F32)

    head_a = lax.broadcasted_iota(I32, (ATTN_TQ, LANE), 1) < HEAD_DIM
    off = lax.broadcasted_iota(I32, (ATTN_TQ, ATTN_TK), 1) - lax.broadcasted_iota(I32, (ATTN_TQ, ATTN_TK), 0)
    band = (off >= 0) & (off <= 2 * HALF_STEPS)
    band2 = jnp.concatenate([band, band], axis=0)
    colpos = lax.broadcasted_iota(I32, (1, ATTN_TK), 1)

    def visit(d, i, first, last):
        log_d = d.bit_length() - 1
        r = i & (d - 1)
        t = i >> log_d
        qs = r + d * ATTN_TQ * t
        ks = ATTN_HALO + r + d * (ATTN_TQ * t - HALF_STEPS)
        q = qf[pl.ds(qs, ATTN_TQ, stride=d), :]
        kt = kf[pl.ds(ks, ATTN_TK, stride=d), :].astype(BF16)
        vt = vf[pl.ds(ks, ATTN_TK, stride=d), :].astype(BF16)
        pos = p0 - ATTN_HALO + ks + d * colpos
        valid = band2 & (pos >= 0) & (pos < seq)
        q2 = jnp.concatenate([jnp.where(head_a, q, 0.0), jnp.where(head_a, 0.0, q)], axis=0).astype(BF16)
        s = jnp.where(valid, _dot_nt(q2, kt), -jnp.inf)
        m = jnp.max(s, axis=1, keepdims=True)
        p = jnp.exp(s - m)
        l = jnp.sum(p, axis=1, keepdims=True)
        n = _dot(p.astype(BF16), vt)
        m_loc = jnp.where(head_a, m[:ATTN_TQ], m[ATTN_TQ:])
        l_loc = jnp.where(head_a, l[:ATTN_TQ], l[ATTN_TQ:])
        n_loc = jnp.where(head_a, n[:ATTN_TQ], n[ATTN_TQ:])
        rows = pl.ds(qs, ATTN_TQ, stride=d)
        if first:
            m_new, l_new, a_new = m_loc, l_loc, n_loc
        else:
            m_old = mst[rows, :]
            m_new = jnp.maximum(m_old, m_loc)
            w_old = jnp.exp(m_old - m_new)
            w_loc = jnp.exp(m_loc - m_new)
            l_new = lst[rows, :] * w_old + l_loc * w_loc
            a_new = acc[rows, :] * w_old + n_loc * w_loc
        if last:
            acc[rows, :] = a_new / l_new
        else:
            mst[rows, :] = m_new
            lst[rows, :] = l_new
            acc[rows, :] = a_new

    n_visits = ATTN_SUPER // ATTN_TQ
    for idx, (_, d) in enumerate(PATTERNS):
        first, last = idx == 0, idx == len(PATTERNS) - 1

        def body(i, c, d=d, first=first, last=last):
            for u in range(ATTN_UNROLL):
                visit(d, i * ATTN_UNROLL + u, first, last)
            return c

        lax.fori_loop(0, n_visits // ATTN_UNROLL, body, 0)
    o_ref[0] = acc[...].astype(BF16)


def _attention(q, k, v):
    b, seq, _ = q.shape
    nblk = seq // ATTN_HALO
    ratio = ATTN_SUPER // ATTN_HALO

    def halo(i):
        return pl.BlockSpec((1, ATTN_HALO, LANE),
                            lambda bi, hp, j: (bi, jnp.clip(ratio * j - 1 + i, 0, nblk - 1), hp))

    main = pl.BlockSpec((1, ATTN_SUPER, LANE), lambda bi, hp, j: (bi, j, hp))
    return pl.pallas_call(
        functools.partial(_attn_kernel, seq=seq),
        grid=(b, D_MODEL // LANE, seq // ATTN_SUPER),
        in_specs=[main] + [halo(i) for i in range(4)] * 2,
        out_specs=main,
        out_shape=jax.ShapeDtypeStruct((b, seq, D_MODEL), BF16),
        scratch_shapes=[pltpu.VMEM((ATTN_SUPER, LANE), F32),
                        pltpu.VMEM((4 * ATTN_HALO, LANE), F32), pltpu.VMEM((4 * ATTN_HALO, LANE), F32),
                        pltpu.VMEM((ATTN_SUPER, LANE), F32), pltpu.VMEM((ATTN_SUPER, LANE), F32),
                        pltpu.VMEM((ATTN_SUPER, LANE), F32)],
        compiler_params=_cparams(3), name="dilated_attn",
    )(q, k, k, k, k, v, v, v, v)


def _shifted(cur, prev8, next8, s, first, last):
    n = cur.shape[0]
    if s == 0:
        return cur
    rows = lax.broadcasted_iota(I32, cur.shape, 0)
    out = pltpu.roll(cur, (-s) % n, 0)
    if s < 0:
        for j in range(-s):
            src = jnp.where(first, 0.0, prev8[8 + s + j:8 + s + j + 1, :])
            out = jnp.where(rows == j, src, out)
    else:
        for j in range(s):
            src = jnp.where(last, 0.0, next8[j:j + 1, :])
            out = jnp.where(rows == n - s + j, src, out)
    return out


def _ssd_conv_kernel(c_ref, p_ref, n_ref, w_ref, b_ref, o_ref):
    first = pl.program_id(1) == 0
    last = pl.program_id(1) == pl.num_programs(1) - 1
    for c in range(SSD_XBC // 256):
        sl = slice(c * 256, (c + 1) * 256)
        cur, prev8, next8 = c_ref[0, :, sl], p_ref[0, :, sl], n_ref[0, :, sl]
        y = b_ref[:, sl]
        for j in range(SSD_CONV):
            y = y + _shifted(cur, prev8, next8, j - SSD_CONV // 2, first, last) * w_ref[j:j + 1, sl]
        o_ref[0, :, sl] = _silu(y)


def _halo_specs(tm, width, seq):
    cur = pl.BlockSpec((1, tm, width), lambda b, i: (b, i, 0))
    prev = pl.BlockSpec((1, 8, width), lambda b, i: (b, jnp.maximum(i * (tm // 8) - 1, 0), 0))
    nxt = pl.BlockSpec((1, 8, width), lambda b, i: (b, jnp.minimum((i + 1) * (tm // 8), seq // 8 - 1), 0))
    return cur, prev, nxt


def _ssd_conv(xbc, w, bias):
    b, seq, width = xbc.shape
    tm = 512
    cur, prev, nxt = _halo_specs(tm, width, seq)
    return pl.pallas_call(
        _ssd_conv_kernel, grid=(b, seq // tm),
        in_specs=[cur, prev, nxt, pl.BlockSpec((8, width), lambda b, i: (0, 0)),
                  pl.BlockSpec((1, width), lambda b, i: (0, 0))],
        out_specs=cur, out_shape=jax.ShapeDtypeStruct(xbc.shape, F32),
        compiler_params=_cparams(2), name="ssd_conv",
    )(xbc, xbc, xbc, w, bias)


def _softplus(x):
    return jnp.maximum(x, 0.0) + jnp.log1p(jnp.exp(-jnp.abs(x)))


def _tri(kind):
    s = lax.broadcasted_iota(I32, (CHUNK, CHUNK), 0)
    l = lax.broadcasted_iota(I32, (CHUNK, CHUNK), 1)
    return {"le": s <= l, "ge": s >= l, "lt": s < l}[kind]


def _expand(cols, e2_ref):
    hi, lo = _split2(cols)
    return _dot(jnp.concatenate([hi, lo], axis=1), e2_ref[...])


def _ssd_bwd_kernel(xs_ref, b_ref, dt_ref, pc_ref, e2_ref, sb_ref, st):
    @pl.when(pl.program_id(1) == 0)
    def _():
        st[...] = jnp.zeros_like(st)

    sb_ref[0, 0] = st[...].astype(BF16)
    dt_t = dt_ref[0].T
    dtb = _softplus(dt_t[N_HEADS:2 * N_HEADS, :] + pc_ref[:, 1:2])
    a = dtb * pc_ref[:, 3:4]
    ex = _dot3(a, _tri("lt").astype(BF16))
    tot = ex[:, CHUNK - 1:CHUNK] + a[:, CHUNK - 1:CHUNK]
    rowform = jnp.concatenate([dtb * jnp.exp(ex), jnp.broadcast_to(jnp.exp(tot), (N_HEADS, CHUNK)),
                               jnp.zeros((CHUNK - 2 * N_HEADS, CHUNK), F32)], axis=0)
    ex2 = _expand(rowform.T, e2_ref)
    xw = (xs_ref[0] * ex2[:, :D_MODEL]).astype(BF16)
    half = D_MODEL // SSD_GROUPS
    upd = [_dot(b_ref[0, :, g * SSD_STATE:(g + 1) * SSD_STATE].T.astype(BF16), xw[:, g * half:(g + 1) * half])
           for g in range(SSD_GROUPS)]
    st[...] = st[...] * ex2[0:1, D_MODEL:] + jnp.concatenate(upd, axis=1)


def _ssd_fwd_kernel(xs_ref, b_ref, c_ref, dt_ref, z_ref, sb_ref, pc_ref, e3_ref, dexp_ref, on_ref, o_ref, st):
    @pl.when(pl.program_id(1) == 0)
    def _():
        st[...] = jnp.zeros_like(st)

    xs = xs_ref[0]
    dt_t = dt_ref[0].T
    dtf = _softplus(dt_t[0:N_HEADS, :] + pc_ref[:, 0:1])
    dtb = _softplus(dt_t[N_HEADS:2 * N_HEADS, :] + pc_ref[:, 1:2])
    af = dtf * pc_ref[:, 2:3]
    ab = dtb * pc_ref[:, 3:4]
    csf = _dot3(af, _tri("le").astype(BF16))
    rcs = _dot3(ab, _tri("ge").astype(BF16))
    totf = csf[:, CHUNK - 1:CHUNK]
    rowform = jnp.concatenate([dtf * jnp.exp(totf - csf), jnp.exp(csf), jnp.exp(rcs), csf, rcs,
                               jnp.zeros((CHUNK - 5 * N_HEADS, CHUNK), F32)], axis=0)
    cols = rowform.T
    ex3 = _expand(cols, e3_ref)
    w_state, e_f, e_b = ex3[:, :D_MODEL], ex3[:, D_MODEL:2 * D_MODEL], ex3[:, 2 * D_MODEL:]

    xb = xs.astype(BF16)
    lower, upper = _tri("ge"), _tri("le")
    head_a = lax.broadcasted_iota(I32, (CHUNK, LANE), 1) < HEAD_DIM
    half = D_MODEL // SSD_GROUPS
    hpg = N_HEADS // SSD_GROUPS
    st_all = st[...]
    sb_all = sb_ref[0, 0]
    ys = []
    b_t = []
    for g in range(SSD_GROUPS):
        bg = b_ref[0, :, g * SSD_STATE:(g + 1) * SSD_STATE]
        cg = c_ref[0, :, g * SSD_STATE:(g + 1) * SSD_STATE].astype(BF16)
        b_t.append(bg.T.astype(BF16))
        gm = _dot_nt(cg, bg.astype(BF16))
        states = jnp.concatenate([st_all[:, g * half:(g + 1) * half].astype(BF16),
                                  sb_all[:, g * half:(g + 1) * half]], axis=1)
        off = _dot(cg, states)
        y_off = (off[:, :half] * e_f[:, g * half:(g + 1) * half]
                 + off[:, half:] * e_b[:, g * half:(g + 1) * half])
        for pair in range(hpg // 2):
            ms = []
            for h in (g * hpg + 2 * pair, g * hpg + 2 * pair + 1):
                dec_f = jnp.where(lower, jnp.exp(cols[:, 3 * N_HEADS + h:3 * N_HEADS + h + 1] - csf[h:h + 1, :]), 0.0)
                dec_b = jnp.where(upper, jnp.exp(cols[:, 4 * N_HEADS + h:4 * N_HEADS + h + 1] - rcs[h:h + 1, :]), 0.0)
                ms.append((gm * (dec_f * dtf[h:h + 1, :] + dec_b * dtb[h:h + 1, :])).astype(BF16))
            lo = g * half + pair * LANE
            xp = xb[:, lo:lo + LANE]
            ys.append(jnp.where(head_a, _dot(ms[0], xp), _dot(ms[1], xp))
                      + y_off[:, pair * LANE:(pair + 1) * LANE])
    y = jnp.concatenate(ys, axis=1) + xs * dexp_ref[...]
    yz = y * _silu(z_ref[0])
    o_ref[0] = _rms(yz, on_ref[...]).astype(BF16)

    xw = (xs * w_state).astype(BF16)
    upd = [_dot(b_t[g], xw[:, g * half:(g + 1) * half]) for g in range(SSD_GROUPS)]
    st[...] = st_all * e_f[CHUNK - 1:CHUNK, :] + jnp.concatenate(upd, axis=1)


def _ssd(xbc_act, dt, z, pc, dexp, out_norm):
    b, seq, _ = xbc_act.shape
    nc = seq // CHUNK
    sel = np.zeros((2 * CHUNK, 3 * D_MODEL), np.float32)
    for part in range(3):
        for h in range(N_HEADS):
            for rep in range(2):
                sel[rep * CHUNK + part * N_HEADS + h, part * D_MODEL + h * HEAD_DIM:part * D_MODEL + (h + 1) * HEAD_DIM] = 1.0
    e3 = jnp.asarray(sel, BF16)
    e2 = jnp.asarray(sel[:, :2 * D_MODEL], BF16)

    rev = lambda bi, c: (bi, nc - 1 - c, 0)
    sb = pl.pallas_call(
        _ssd_bwd_kernel, grid=(b, nc),
        in_specs=[pl.BlockSpec((1, CHUNK, D_MODEL), rev),
                  pl.BlockSpec((1, CHUNK, 2 * SSD_STATE), lambda bi, c: (bi, nc - 1 - c, D_MODEL // (2 * SSD_STATE))),
                  pl.BlockSpec((1, CHUNK, LANE), rev), _const_spec(pc.shape), _const_spec(e2.shape)],
        out_specs=pl.BlockSpec((1, 1, SSD_STATE, D_MODEL), lambda bi, c: (bi, nc - 1 - c, 0, 0)),
        out_shape=jax.ShapeDtypeStruct((b, nc, SSD_STATE, D_MODEL), BF16),
        scratch_shapes=[pltpu.VMEM((SSD_STATE, D_MODEL), F32)],
        compiler_params=_cparams(2), name="ssd_bwd_state",
    )(xbc_act, xbc_act, dt, pc, e2)

    fwd = lambda bi, c: (bi, c, 0)
    return pl.pallas_call(
        _ssd_fwd_kernel, grid=(b, nc),
        in_specs=[pl.BlockSpec((1, CHUNK, D_MODEL), fwd),
                  pl.BlockSpec((1, CHUNK, 2 * SSD_STATE), lambda bi, c: (bi, c, D_MODEL // (2 * SSD_STATE))),
                  pl.BlockSpec((1, CHUNK, 2 * SSD_STATE), lambda bi, c: (bi, c, D_MODEL // (2 * SSD_STATE) + 1)),
                  pl.BlockSpec((1, CHUNK, LANE), fwd), pl.BlockSpec((1, CHUNK, D_MODEL), fwd),
                  pl.BlockSpec((1, 1, SSD_STATE, D_MODEL), lambda bi, c: (bi, c, 0, 0)),
                  _const_spec(pc.shape), _const_spec(e3.shape), _const_spec((1, D_MODEL)), _const_spec((1, D_MODEL))],
        out_specs=pl.BlockSpec((1, CHUNK, D_MODEL), fwd),
        out_shape=jax.ShapeDtypeStruct((b, seq, D_MODEL), BF16),
        scratch_shapes=[pltpu.VMEM((SSD_STATE, D_MODEL), F32)],
        compiler_params=_cparams(2), name="ssd_fwd",
    )(xbc_act, xbc_act, xbc_act, dt, z, sb, pc, e3, dexp, out_norm)


def _norm_and_route(x1, fg_ref, wr_hi_ref, wr_lo_ref, hn_ref, aff_ref):
    hn = _rms(x1, fg_ref[...])
    hi, lo = _split2(hn)
    hn_ref[...] = hi
    logits = _dot(hi, wr_hi_ref[...]) + _dot(lo, wr_hi_ref[...]) + _dot(hi, wr_lo_ref[...])
    lt = logits.T[0:N_EXPERTS, :]
    e = jnp.exp(lt - jnp.max(lt, axis=0, keepdims=True))
    aff_ref[0] = e / jnp.sum(e, axis=0, keepdims=True)


def _out_even_kernel(x_ref, a_ref, y_ref, wa_ref, wy_ref, fg_ref, wr_hi_ref, wr_lo_ref, x1_ref, hn_ref, aff_ref):
    x1 = x_ref[...] + _dot(a_ref[...], wa_ref[...]) + _dot(y_ref[...], wy_ref[...])
    x1_ref[...] = x1
    _norm_and_route(x1, fg_ref, wr_hi_ref, wr_lo_ref, hn_ref, aff_ref)


def _out_odd_kernel(x_ref, gb_ref, cu_ref, cp_ref, cn_ref, cw_ref, wo_ref, fg_ref, wr_hi_ref, wr_lo_ref,
                    x1_ref, hn_ref, aff_ref, *, tiles_per_seq):
    i = pl.program_id(0) % tiles_per_seq
    first, last = i == 0, i == tiles_per_seq - 1
    cur, prev8, next8 = cu_ref[...], cp_ref[...], cn_ref[...]
    conv = sum(_shifted(cur, prev8, next8, j - SHORT_CONV // 2, first, last) * cw_ref[j:j + 1, :]
               for j in range(SHORT_CONV))
    x1 = x_ref[...] + _dot((gb_ref[...] * conv).astype(BF16), wo_ref[...])
    x1_ref[...] = x1
    _norm_and_route(x1, fg_ref, wr_hi_ref, wr_lo_ref, hn_ref, aff_ref)


def _route_outs(n, b, seq, tm):
    row = pl.BlockSpec((tm, D_MODEL), lambda i: (i, 0))
    aff = pl.BlockSpec((1, N_EXPERTS, tm), lambda i: (i // (seq // tm), 0, i % (seq // tm)))
    shapes = [jax.ShapeDtypeStruct((n, D_MODEL), F32), jax.ShapeDtypeStruct((n, D_MODEL), BF16),
              jax.ShapeDtypeStruct((b, N_EXPERTS, seq), F32)]
    return [row, row, aff], shapes


def _out_even(x, attn, y, wa, wy, fg, wr_hi, wr_lo, b, seq):
    n = x.shape[0]
    tm = ROW_TILE
    row = pl.BlockSpec((tm, D_MODEL), lambda i: (i, 0))
    out_specs, shapes = _route_outs(n, b, seq, tm)
    return pl.pallas_call(
        _out_even_kernel, grid=(n // tm,),
        in_specs=[row, row, row, _const_spec(wa.shape), _const_spec(wy.shape), _const_spec((1, D_MODEL)),
                  _const_spec(wr_hi.shape), _const_spec(wr_lo.shape)],
        out_specs=out_specs, out_shape=shapes, compiler_params=_cparams(1), name="out_even",
    )(x, attn, y, wa, wy, fg, wr_hi, wr_lo)


def _out_odd(x, gb, cu, cw, wo, fg, wr_hi, wr_lo, b, seq):
    n = x.shape[0]
    tm = ROW_TILE
    row = pl.BlockSpec((tm, D_MODEL), lambda i: (i, 0))
    prev = pl.BlockSpec((8, D_MODEL), lambda i: (jnp.maximum(i * (tm // 8) - 1, 0), 0))
    nxt = pl.BlockSpec((8, D_MODEL), lambda i: (jnp.minimum((i + 1) * (tm // 8), n // 8 - 1), 0))
    out_specs, shapes = _route_outs(n, b, seq, tm)
    return pl.pallas_call(
        functools.partial(_out_odd_kernel, tiles_per_seq=seq // tm), grid=(n // tm,),
        in_specs=[row, row, row, prev, nxt, pl.BlockSpec((8, D_MODEL), lambda i: (0, 0)), _const_spec(wo.shape),
                  _const_spec((1, D_MODEL)), _const_spec(wr_hi.shape), _const_spec(wr_lo.shape)],
        out_specs=out_specs, out_shape=shapes, compiler_params=_cparams(1), name="out_odd",
    )(x, gb, cu, cu, cu, cw, wo, fg, wr_hi, wr_lo)


def _in_odd_kernel(x_ref, g_ref, wb_ref, wc_ref, wu_ref, gb_ref, cu_ref):
    hb = _rms(x_ref[...], g_ref[...]).astype(BF16)
    gb_ref[...] = _dot(hb, wb_ref[...])
    cu_ref[...] = _dot(hb, wc_ref[...]) * _dot(hb, wu_ref[...])


def _in_odd(x, g, wb, wc, wu):
    n = x.shape[0]
    tm = ROW_TILE
    row = pl.BlockSpec((tm, D_MODEL), lambda i: (i, 0))
    return pl.pallas_call(
        _in_odd_kernel, grid=(n // tm,),
        in_specs=[row, _const_spec((1, D_MODEL)), _const_spec(wb.shape), _const_spec(wc.shape), _const_spec(wu.shape)],
        out_specs=[row, row], out_shape=[jax.ShapeDtypeStruct((n, D_MODEL), F32)] * 2,
        compiler_params=_cparams(1), name="in_odd",
    )(x, g, wb, wc, wu)


def _count(mask):
    return jnp.sum(jnp.sum(mask.astype(F32), axis=0, keepdims=True), axis=1, keepdims=True)


def _route_kernel(aff_ref, incl_ref, ones_ref, strict_ref, local_ref, group_ref, first_ref, slot_ref, off_ref, end_ref,
                  *, cap):
    for e in range(N_EXPERTS):
        bits = pltpu.bitcast(aff_ref[0, e], I32)

        def step(i, thr):
            cand = thr | (jnp.int32(1) << (30 - i))
            return jnp.where(_count(bits >= cand) >= cap, cand, thr)

        thr = lax.fori_loop(0, 31, step, jnp.zeros((1, 1), I32))
        gt = bits > thr
        eq = (bits == thr).astype(BF16)
        eq_rank = _dot(eq, incl_ref[...]) + _dot(strict_ref[...], _dot(eq, ones_ref[...]).astype(BF16))
        sel = (gt | ((bits == thr) & (eq_rank <= cap - _count(gt)))).astype(BF16)
        within = _dot(sel, incl_ref[...])
        totals = _dot(sel, ones_ref[...]).astype(BF16)
        local = _dot(local_ref[...], totals)
        cnt = _dot(group_ref[...], totals)
        padded = jnp.floor((cnt + (SLOT_ALIGN - 1)) * (1.0 / SLOT_ALIGN)) * SLOT_ALIGN
        start = _dot(first_ref[...], padded.astype(BF16))
        slot_ref[0, e] = jnp.where(sel > 0, (start + local + within).astype(I32) - 1, -1)
        off_ref[0, e:e + 1, :] = start.T[0:1, :].astype(I32)
        end_ref[0, e:e + 1, :] = (start + padded).T[0:1, :].astype(I32)


def _route(aff, cap):
    b, _, seq = aff.shape
    nt = seq // LANE
    tri = np.arange(LANE)
    tt = np.arange(nt)
    grp = tt // SUB
    as_bf16 = lambda m: jnp.asarray(m, BF16)
    incl = as_bf16(tri[:, None] <= tri[None, :])
    strict = as_bf16(tt[None, :] < tt[:, None])
    local = as_bf16((tt[None, :] < tt[:, None]) & (grp[None, :] == grp[:, None]))
    group = as_bf16(grp[None, :] == grp[:, None])
    first = as_bf16((grp[None, :] < grp[:, None]) & (tt[None, :] % SUB == 0))
    ones = jnp.ones((LANE, LANE), BF16)
    tiles = pl.BlockSpec((1, N_EXPERTS, nt, LANE), lambda i: (i, 0, 0, 0))
    rows = pl.BlockSpec((1, N_EXPERTS, nt), lambda i: (i, 0, 0))
    return pl.pallas_call(
        functools.partial(_route_kernel, cap=cap), grid=(b,),
        in_specs=[tiles, _const_spec((LANE, LANE)), _const_spec((LANE, LANE))] + [_const_spec((nt, nt))] * 4,
        out_specs=[tiles, rows, rows],
        out_shape=[jax.ShapeDtypeStruct((b, N_EXPERTS, nt, LANE), I32)] + [jax.ShapeDtypeStruct((b, N_EXPERTS, nt), I32)] * 2,
        compiler_params=_cparams(1), name="route",
    )(aff.reshape(b, N_EXPERTS, nt, LANE), incl, ones, strict, local, group, first)


def _one_hot(slots, base):
    return (slots == lax.broadcasted_iota(I32, (WINDOW, slots.shape[1]), 0) + base).astype(BF16)


def _gather_kernel(off_ref, end_ref, slot_ref, aff_ref, hn_ref, zero_hbm, xe_hbm, stage, extra, sem, xsem):
    del zero_hbm
    bi, j = pl.program_id(0), pl.program_id(1)
    n_j = pl.num_programs(1)
    step = bi * n_j + j
    cur = step % 2

    def window_copy(bb, jj, e, buf):
        start = pl.multiple_of(off_ref[bb, e, jj * SUB], SLOT_ALIGN)
        return pltpu.make_async_copy(stage.at[buf, e], xe_hbm.at[bb, e, pl.ds(start, WINDOW), :], sem.at[buf, e])

    tokens = hn_ref[0]
    parts = [p.astype(F32) for p in _split3(aff_ref[0, 0])]
    gates = jnp.concatenate(parts + [jnp.zeros((LANE - 3 * N_EXPERTS, TOK_TILE), F32)], axis=0).astype(BF16)

    def rows_of(p):
        return jnp.concatenate([_dot(p, tokens), _dot_nt(p, gates)], axis=1).astype(BF16)

    p_all = jnp.concatenate([_one_hot(slot_ref[0, 0, e:e + 1, :], off_ref[bi, e, j * SUB])
                             for e in range(N_EXPERTS)], axis=0)
    stage[cur] = rows_of(p_all).reshape(N_EXPERTS, WINDOW, D_MODEL + LANE)

    @pl.when(step > 0)
    def _():
        prev = step - 1
        for e in range(N_EXPERTS):
            window_copy(prev // n_j, prev % n_j, e, 1 - cur).wait()

    for e in range(N_EXPERTS):
        window_copy(bi, j, e, cur).start()

    def overflow(e, c):
        first = off_ref[bi, e, j * SUB]
        n_win = (end_ref[bi, e, j * SUB] - first + WINDOW - 1) // WINDOW
        slots = slot_ref[0, 0, pl.ds(e, 1), :]

        def one(w, c2):
            base = pl.multiple_of(first + w * WINDOW, SLOT_ALIGN)
            extra[...] = rows_of(_one_hot(slots, base))
            cp = pltpu.make_async_copy(extra, xe_hbm.at[bi, e, pl.ds(base, WINDOW), :], xsem.at[0])
            cp.start()
            cp.wait()
            return c2

        return lax.fori_loop(1, n_win, one, c)

    lax.fori_loop(0, N_EXPERTS, overflow, 0)

    @pl.when(step == pl.num_programs(0) * n_j - 1)
    def _():
        for e in range(N_EXPERTS):
            window_copy(bi, j, e, cur).wait()


def _gather(off, end, slot_t, aff_t, hn):
    b, seq, _ = hn.shape
    cap_pad = _cap_pad(seq)
    width = D_MODEL + LANE
    per_tile = pl.BlockSpec((1, 1, N_EXPERTS, TOK_TILE), lambda bi, j, *_: (bi, j, 0, 0))
    return pl.pallas_call(
        _gather_kernel,
        grid_spec=pltpu.PrefetchScalarGridSpec(
            num_scalar_prefetch=2, grid=(b, seq // TOK_TILE),
            in_specs=[per_tile, per_tile, pl.BlockSpec((1, TOK_TILE, D_MODEL), lambda bi, j, *_: (bi, j, 0)),
                      pl.BlockSpec(memory_space=pl.ANY)],
            out_specs=pl.BlockSpec(memory_space=pl.ANY),
            scratch_shapes=[pltpu.VMEM((2, N_EXPERTS, WINDOW, width), BF16), pltpu.VMEM((WINDOW, width), BF16),
                            pltpu.SemaphoreType.DMA((2, N_EXPERTS)), pltpu.SemaphoreType.DMA((1,))]),
        out_shape=jax.ShapeDtypeStruct((b, N_EXPERTS, cap_pad, width), BF16),
        input_output_aliases={5: 0},
        compiler_params=_cparams(2), name="moe_gather",
    )(off, end, slot_t, aff_t, hn, jnp.zeros((b, N_EXPERTS, cap_pad, width), BF16))


def _ffn_kernel(end_ref, xe_ref, wg_ref, wu_ref, wd_ref, y_ref):
    e, bi, r = pl.program_id(0), pl.program_id(1), pl.program_id(2)
    used = end_ref[bi, e, end_ref.shape[2] - 1]

    @pl.when(r * FFN_ROWS < used)
    def _():
        xe = xe_ref[0, 0, :, 0:D_MODEL]
        hid = (_silu(_dot(xe, wg_ref[0])) * _dot(xe, wu_ref[0])).astype(BF16)
        g = xe_ref[0, 0, :, D_MODEL:].astype(F32)
        lane = lax.broadcasted_iota(I32, g.shape, 1)
        mine = (lane % N_EXPERTS == e) & (lane < 3 * N_EXPERTS)
        gate = jnp.sum(jnp.where(mine, g, 0.0), axis=1, keepdims=True)
        y_ref[0, 0] = (_dot(hid, wd_ref[0]) * gate).astype(BF16)

    @pl.when(r * FFN_ROWS >= used)
    def _():
        y_ref[...] = jnp.zeros_like(y_ref)


def _ffn(end, xe, wg, wu, wd):
    b, ne, cap_pad, width = xe.shape
    rows = lambda w: pl.BlockSpec((1, 1, FFN_ROWS, w), lambda e, bi, r, *_: (bi, e, r, 0))
    wspec = lambda s: pl.BlockSpec((1,) + s, lambda e, bi, r, *_: (e, 0, 0))
    return pl.pallas_call(
        _ffn_kernel,
        grid_spec=pltpu.PrefetchScalarGridSpec(
            num_scalar_prefetch=1, grid=(ne, b, cap_pad // FFN_ROWS),
            in_specs=[rows(width), wspec((D_MODEL, D_FF)), wspec((D_MODEL, D_FF)), wspec((D_FF, D_MODEL))],
            out_specs=rows(D_MODEL)),
        out_shape=jax.ShapeDtypeStruct((b, ne, cap_pad, D_MODEL), BF16),
        compiler_params=_cparams(3), name="moe_ffn",
    )(end, xe, wg, wu, wd)


def _combine_kernel(off_ref, end_ref, slot_ref, y_hbm, x1_ref, o_ref, win, extra, sem, xsem):
    bi, j = pl.program_id(0), pl.program_id(1)
    n_j = pl.num_programs(1)
    step = bi * n_j + j
    cur = step % 2

    def window_copy(bb, jj, e, buf):
        start = pl.multiple_of(off_ref[bb, e, jj * SUB], SLOT_ALIGN)
        return pltpu.make_async_copy(y_hbm.at[bb, e, pl.ds(start, WINDOW), :], win.at[buf, e], sem.at[buf, e])

    @pl.when(step == 0)
    def _():
        for e in range(N_EXPERTS):
            window_copy(bi, j, e, cur).start()

    @pl.when(step + 1 < pl.num_programs(0) * n_j)
    def _():
        nxt = step + 1
        for e in range(N_EXPERTS):
            window_copy(nxt // n_j, nxt % n_j, e, 1 - cur).start()

    ps = []
    for e in range(N_EXPERTS):
        window_copy(bi, j, e, cur).wait()
        ps.append(_one_hot(slot_ref[0, 0, e:e + 1, :], off_ref[bi, e, j * SUB]))
    p_all = jnp.concatenate(ps, axis=0)
    y_all = win[cur].reshape(N_EXPERTS * WINDOW, D_MODEL)
    o_ref[0] = x1_ref[0] + _dot_tn(p_all, y_all)

    def overflow(e, c):
        first = off_ref[bi, e, j * SUB]
        n_win = (end_ref[bi, e, j * SUB] - first + WINDOW - 1) // WINDOW
        slots = slot_ref[0, 0, pl.ds(e, 1), :]

        def one(w, c2):
            base = pl.multiple_of(first + w * WINDOW, SLOT_ALIGN)
            cp = pltpu.make_async_copy(y_hbm.at[bi, e, pl.ds(base, WINDOW), :], extra, xsem.at[0])
            cp.start()
            cp.wait()
            o_ref[0] += _dot_tn(_one_hot(slots, base), extra[...])
            return c2

        return lax.fori_loop(1, n_win, one, c)

    lax.fori_loop(0, N_EXPERTS, overflow, 0)


def _combine(off, end, slot_t, y, x1):
    b, seq, _ = x1.shape
    tile = pl.BlockSpec((1, TOK_TILE, D_MODEL), lambda bi, j, *_: (bi, j, 0))
    return pl.pallas_call(
        _combine_kernel,
        grid_spec=pltpu.PrefetchScalarGridSpec(
            num_scalar_prefetch=2, grid=(b, seq // TOK_TILE),
            in_specs=[pl.BlockSpec((1, 1, N_EXPERTS, TOK_TILE), lambda bi, j, *_: (bi, j, 0, 0)),
                      pl.BlockSpec(memory_space=pl.ANY), tile],
            out_specs=tile,
            scratch_shapes=[pltpu.VMEM((2, N_EXPERTS, WINDOW, D_MODEL), BF16), pltpu.VMEM((WINDOW, D_MODEL), BF16),
                            pltpu.SemaphoreType.DMA((2, N_EXPERTS)), pltpu.SemaphoreType.DMA((1,))]),
        out_shape=jax.ShapeDtypeStruct((b, seq, D_MODEL), F32),
        compiler_params=_cparams(2), name="moe_combine",
    )(off, end, slot_t, y, x1)


def _cap_pad(seq):
    cap = CAPACITY_FACTOR * seq // N_EXPERTS
    worst = cap + (seq // TOK_TILE) * (SLOT_ALIGN - 1) + WINDOW
    return -(-worst // FFN_ROWS) * FFN_ROWS


def _moe(x1, hn, aff, wg, wu, wd):
    b, seq, _ = x1.shape
    cap = CAPACITY_FACTOR * seq // N_EXPERTS
    slot, off, end = _route(aff, cap)
    per_tile = lambda a: jnp.swapaxes(a.reshape(b, N_EXPERTS, seq // TOK_TILE, TOK_TILE), 1, 2)
    slot_t = per_tile(slot)
    xe = _gather(off, end, slot_t, per_tile(aff), hn)
    y = _ffn(end, xe, wg, wu, wd)
    return _combine(off, end, slot_t, y, x1)


def _rope_tables(seq):
    inv_freq = ROPE_THETA ** (-jnp.arange(ROPE_HALF, dtype=F32) * 2.0 / (2 * ROPE_HALF))
    ang = jnp.arange(seq, dtype=F32)[:, None] * inv_freq[None, :]
    cos, sin = jnp.cos(ang), jnp.sin(ang)
    z = lambda w: jnp.zeros((seq, w), F32)
    rest = HEAD_DIM - 2 * ROPE_HALF
    rc = jnp.concatenate([cos, cos, jnp.ones((seq, rest), F32)], axis=1)
    r1 = jnp.concatenate([-sin, z(ROPE_HALF + rest)], axis=1)
    r2 = jnp.concatenate([z(ROPE_HALF), sin, z(rest)], axis=1)
    return tuple(jnp.tile(t, (1, LANE // HEAD_DIM)) for t in (rc, r1, r2))


def _router_split(w):
    wp = jnp.pad(w, ((0, 0), (0, LANE - N_EXPERTS)))
    hi = wp.astype(BF16)
    return hi, (wp - hi.astype(F32)).astype(BF16)


def kernel(x, attn_norm, w_in_even, q_norm, k_norm, ssd_conv_w, ssd_conv_b, ssd_a_log_fwd, ssd_a_log_bwd,
           ssd_dt_bias_fwd, ssd_dt_bias_bwd, ssd_d, ssd_out_norm, w_out_even, conv_norm, conv_w_in, conv_w,
           conv_w_out, ffn_norm, router_w, expert_w_gate, expert_w_up, expert_w_down):
    b, seq, _ = x.shape
    n = b * seq
    depth = ffn_norm.shape[0]
    rc, r1, r2 = _rope_tables(seq)
    blk = np.arange(256) // HEAD_DIM
    bd = jnp.asarray((blk[:, None] == blk[None, :]) / HEAD_DIM, BF16)
    row = lambda v: v.reshape(1, -1).astype(F32)

    xf = x.reshape(n, D_MODEL)
    for layer in range(depth):
        i = layer // 2
        wr_hi, wr_lo = _router_split(router_w[layer])
        fg = row(ffn_norm[layer])
        if layer % 2 == 0:
            w = w_in_even[i].astype(BF16)
            o = np.cumsum([0, D_MODEL, D_MODEL, D_MODEL, D_MODEL, SSD_XBC, N_HEADS, N_HEADS])
            wq, wk, wv, wz, wx = (w[:, o[j]:o[j + 1]] for j in range(5))
            wd = jnp.pad(w[:, o[5]:o[7]], ((0, 0), (0, LANE - 2 * N_HEADS)))
            tile_heads = lambda g: row(jnp.tile(g, N_HEADS))
            q, k, v, z, xbc, dt = _in_even(xf, row(attn_norm[i]), wq, wk, wv, wz, wx, wd,
                                           tile_heads(q_norm[i]), tile_heads(k_norm[i]), bd, rc, r1, r2, seq)
            as3 = lambda t: t.reshape(b, seq, -1)
            attn = _attention(as3(q), as3(k), as3(v))
            cw = jnp.pad(ssd_conv_w[i], ((0, 8 - SSD_CONV), (0, 0)))
            act = _ssd_conv(as3(xbc), cw, row(ssd_conv_b[i]))
            pc = jnp.pad(jnp.stack([ssd_dt_bias_fwd[i], ssd_dt_bias_bwd[i], -jnp.exp(ssd_a_log_fwd[i]),
                                    -jnp.exp(ssd_a_log_bwd[i])], axis=1).astype(F32), ((0, 0), (0, LANE - 4)))
            y = _ssd(act, as3(dt), as3(z), pc, row(jnp.repeat(ssd_d[i], HEAD_DIM)), row(ssd_out_norm[i]))
            wo = w_out_even[i].astype(BF16)
            x1, hn, aff = _out_even(xf, attn.reshape(n, D_MODEL), y.reshape(n, D_MODEL), wo[:D_MODEL], wo[D_MODEL:],
                                    fg, wr_hi, wr_lo, b, seq)
        else:
            w = conv_w_in[i].astype(BF16)
            gb, cu = _in_odd(xf, row(conv_norm[i]), w[:, :D_MODEL], w[:, D_MODEL:2 * D_MODEL], w[:, 2 * D_MODEL:])
            cw = jnp.pad(conv_w[i], ((0, 8 - SHORT_CONV), (0, 0)))
            x1, hn, aff = _out_odd(xf, gb, cu, cw, conv_w_out[i].astype(BF16), fg, wr_hi, wr_lo, b, seq)
        wg = expert_w_gate[layer].astype(BF16)
        wu = expert_w_up[layer].astype(BF16)
        wd = expert_w_down[layer].astype(BF16)
        xf = _moe(x1.reshape(b, seq, D_MODEL), hn.reshape(b, seq, D_MODEL), aff, wg, wu, wd).reshape(n, D_MODEL)
    return xf.reshape(b, seq, D_MODEL)
```

```python
import functools
import math

import jax
import jax.numpy as jnp
import numpy as np
from jax import lax
from jax.experimental import pallas as pl
from jax.experimental.pallas import tpu as pltpu

F32, BF16, I32 = jnp.float32, jnp.bfloat16, jnp.int32

D_MODEL = 1024
N_HEADS = 16
HEAD_DIM = 64
ROPE_HALF = 8
ROPE_THETA = 500000.0
PATTERNS = ((128, 1), (512, 4), (2048, 16))
HALF_STEPS = 64
SSD_GROUPS = 2
SSD_STATE = 128
SSD_XBC = 1536
SSD_CONV = 5
CHUNK = 128
N_EXPERTS = 16
CAPACITY_FACTOR = 2
D_FF = 2048
SHORT_CONV = 3
EPS = 1e-6

LANE = 128
VMEM_LIMIT = 56 * 1024 * 1024

ROW_TILE = 256
ATTN_SUPER = 2048
ATTN_HALO = 1024
ATTN_TQ = 128
ATTN_TK = ATTN_TQ + 2 * HALF_STEPS
ATTN_UNROLL = 4
TOK_TILE = 512
SUB = TOK_TILE // LANE
SLOT_ALIGN = 16
WINDOW = 128
FFN_ROWS = 256
CAST_ROWS = 64


def _cparams(n_axes):
    return pltpu.CompilerParams(dimension_semantics=("arbitrary",) * n_axes, vmem_limit_bytes=VMEM_LIMIT)


def _const_spec(shape):
    nd = len(shape)
    return pl.BlockSpec(shape, lambda *_: (0,) * nd, pipeline_mode=pl.Buffered(1))


def _dot(a, b):
    return jnp.dot(a, b, preferred_element_type=F32)


def _dot_nt(a, b):
    return lax.dot_general(a, b, (((1,), (1,)), ((), ())), preferred_element_type=F32)


def _dot_tn(a, b):
    return lax.dot_general(a, b, (((0,), (0,)), ((), ())), preferred_element_type=F32)


def _split2(x):
    hi = x.astype(BF16)
    lo = (x - hi.astype(F32)).astype(BF16)
    return hi, lo


def _split3(x):
    hi = x.astype(BF16)
    r = x - hi.astype(F32)
    mid = r.astype(BF16)
    lo = (r - mid.astype(F32)).astype(BF16)
    return hi, mid, lo


def _dot3(x, m_bf16):
    hi, mid, lo = _split3(x)
    return _dot(hi, m_bf16) + _dot(mid, m_bf16) + _dot(lo, m_bf16)


def _rms(x, g):
    return x * lax.rsqrt(jnp.mean(x * x, axis=-1, keepdims=True) + EPS) * g


def _silu(x):
    return x * jax.nn.sigmoid(x)


def _in_even_kernel(x_ref, g_ref, wq_ref, wk_ref, wv_ref, wz_ref, wx_ref, wd_ref, qg_ref, kg_ref, bd_ref,
                    rc_ref, r1_ref, r2_ref, q_ref, k_ref, v_ref, z_ref, xbc_ref, dt_ref):
    hb = _rms(x_ref[...], g_ref[...]).astype(BF16)
    bd = bd_ref[...]
    rc, r1, r2 = rc_ref[...], r1_ref[...], r2_ref[...]

    def head_norm_rope(w_ref, gain_ref, out_ref):
        t = _dot(hb, w_ref[...])
        for c in range(D_MODEL // 256):
            tc = t[:, c * 256:(c + 1) * 256]
            sq_hi, sq_lo = _split2(tc * tc)
            ms = _dot(sq_hi, bd) + _dot(sq_lo, bd)
            tn = tc * lax.rsqrt(ms + EPS) * gain_ref[:, c * 256:(c + 1) * 256]
            for hh in range(2):
                u = tn[:, hh * LANE:(hh + 1) * LANE]
                r = u * rc + pltpu.roll(u, LANE - ROPE_HALF, 1) * r1 + pltpu.roll(u, ROPE_HALF, 1) * r2
                out_ref[:, c * 256 + hh * LANE:c * 256 + (hh + 1) * LANE] = r.astype(BF16)

    head_norm_rope(wq_ref, qg_ref, q_ref)
    head_norm_rope(wk_ref, kg_ref, k_ref)
    v_ref[...] = _dot(hb, wv_ref[...]).astype(BF16)
    z_ref[...] = _dot(hb, wz_ref[...])
    xbc_ref[...] = _dot(hb, wx_ref[...])
    dt_ref[...] = _dot(hb, wd_ref[...])


def _in_even(x, g, wq, wk, wv, wz, wx, wd, qg, kg, bd, rc, r1, r2, seq):
    n = x.shape[0]
    tm = ROW_TILE
    row = lambda w: pl.BlockSpec((tm, w), lambda i: (i, 0))
    tab = pl.BlockSpec((tm, LANE), lambda i: (i % (seq // tm), 0))
    outs = [jax.ShapeDtypeStruct((n, D_MODEL), BF16)] * 3 + [jax.ShapeDtypeStruct((n, D_MODEL), F32),
                                                            jax.ShapeDtypeStruct((n, SSD_XBC), F32),
                                                            jax.ShapeDtypeStruct((n, LANE), F32)]
    return pl.pallas_call(
        _in_even_kernel, grid=(n // tm,),
        in_specs=[row(D_MODEL), _const_spec((1, D_MODEL)), _const_spec(wq.shape), _const_spec(wk.shape),
                  _const_spec(wv.shape), _const_spec(wz.shape), _const_spec(wx.shape), _const_spec(wd.shape),
                  _const_spec((1, D_MODEL)), _const_spec((1, D_MODEL)), _const_spec((256, 256)), tab, tab, tab],
        out_specs=[row(D_MODEL)] * 4 + [row(SSD_XBC), row(LANE)], out_shape=outs,
        compiler_params=_cparams(1), name="in_even",
    )(x, g, wq, wk, wv, wz, wx, wd, qg, kg, bd, rc, r1, r2)


def _attn_kernel(q_ref, k0, k1, k2, k3, v0, v1, v2, v3, o_ref, qf, kf, vf, acc, mst, lst, *, seq):
    p0 = pl.program_id(2) * ATTN_SUPER
    qf[...] = q_ref[0].astype(F32) * (HEAD_DIM ** -0.5)
    for i, (kr, vr) in enumerate(((k0, v0), (k1, v1), (k2, v2), (k3, v3))):
        kf[i * ATTN_HALO:(i + 1) * ATTN_HALO, :] = kr[0].astype(F32)
        vf[i * ATTN_HALO:(i + 1) * ATTN_HALO, :] = vr[0].astype(F32)

    head_a = lax.broadcasted_iota(I32, (ATTN_TQ, LANE), 1) < HEAD_DIM
    off = lax.broadcasted_iota(I32, (ATTN_TQ, ATTN_TK), 1) - lax.broadcasted_iota(I32, (ATTN_TQ, ATTN_TK), 0)
    band = (off >= 0) & (off <= 2 * HALF_STEPS)
    band2 = jnp.concatenate([band, band], axis=0)
    colpos = lax.broadcasted_iota(I32, (1, ATTN_TK), 1)

    def visit(d, i, first, last):
        log_d = d.bit_length() - 1
        r = i & (d - 1)
        t = i >> log_d
        qs = r + d * ATTN_TQ * t
        ks = ATTN_HALO + r + d * (ATTN_TQ * t - HALF_STEPS)
        q = qf[pl.ds(qs, ATTN_TQ, stride=d), :]
        kt = kf[pl.ds(ks, ATTN_TK, stride=d), :].astype(BF16)
        vt = vf[pl.ds(ks, ATTN_TK, stride=d), :].astype(BF16)
        pos = p0 - ATTN_HALO + ks + d * colpos
        valid = band2 & (pos >= 0) & (pos < seq)
        q2 = jnp.concatenate([jnp.where(head_a, q, 0.0), jnp.where(head_a, 0.0, q)], axis=0).astype(BF16)
        s = jnp.where(valid, _dot_nt(q2, kt), -jnp.inf)
        m = jnp.max(s, axis=1, keepdims=True)
        p = jnp.exp(s - m)
        l = jnp.sum(p, axis=1, keepdims=True)
        n = _dot(p.astype(BF16), vt)
        m_loc = jnp.where(head_a, m[:ATTN_TQ], m[ATTN_TQ:])
        l_loc = jnp.where(head_a, l[:ATTN_TQ], l[ATTN_TQ:])
        n_loc = jnp.where(head_a, n[:ATTN_TQ], n[ATTN_TQ:])
        rows = pl.ds(qs, ATTN_TQ, stride=d)
        if first:
            m_new, l_new, a_new = m_loc, l_loc, n_loc
        else:
            m_old = mst[rows, :]
            m_new = jnp.maximum(m_old, m_loc)
            w_old = jnp.exp(m_old - m_new)
            w_loc = jnp.exp(m_loc - m_new)
            l_new = lst[rows, :] * w_old + l_loc * w_loc
            a_new = acc[rows, :] * w_old + n_loc * w_loc
        if last:
            acc[rows, :] = a_new / l_new
        else:
            mst[rows, :] = m_new
            lst[rows, :] = l_new
            acc[rows, :] = a_new

    n_visits = ATTN_SUPER // ATTN_TQ
    for idx, (_, d) in enumerate(PATTERNS):
        first, last = idx == 0, idx == len(PATTERNS) - 1

        def body(i, c, d=d, first=first, last=last):
            for u in range(ATTN_UNROLL):
                visit(d, i * ATTN_UNROLL + u, first, last)
            return c

        lax.fori_loop(0, n_visits // ATTN_UNROLL, body, 0)
    o_ref[0] = acc[...].astype(BF16)


def _attention(q, k, v):
    b, seq, _ = q.shape
    nblk = seq // ATTN_HALO
    ratio = ATTN_SUPER // ATTN_HALO

    def halo(i):
        return pl.BlockSpec((1, ATTN_HALO, LANE),
                            lambda bi, hp, j: (bi, jnp.clip(ratio * j - 1 + i, 0, nblk - 1), hp))

    main = pl.BlockSpec((1, ATTN_SUPER, LANE), lambda bi, hp, j: (bi, j, hp))
    return pl.pallas_call(
        functools.partial(_attn_kernel, seq=seq),
        grid=(b, D_MODEL // LANE, seq // ATTN_SUPER),
        in_specs=[main] + [halo(i) for i in range(4)] * 2,
        out_specs=main,
        out_shape=jax.ShapeDtypeStruct((b, seq, D_MODEL), BF16),
        scratch_shapes=[pltpu.VMEM((ATTN_SUPER, LANE), F32),
                        pltpu.VMEM((4 * ATTN_HALO, LANE), F32), pltpu.VMEM((4 * ATTN_HALO, LANE), F32),
                        pltpu.VMEM((ATTN_SUPER, LANE), F32), pltpu.VMEM((ATTN_SUPER, LANE), F32),
                        pltpu.VMEM((ATTN_SUPER, LANE), F32)],
        compiler_params=_cparams(3), name="dilated_attn",
    )(q, k, k, k, k, v, v, v, v)


def _shifted(cur, prev8, next8, s, first, last):
    n = cur.shape[0]
    if s == 0:
        return cur
    rows = lax.broadcasted_iota(I32, cur.shape, 0)
    out = pltpu.roll(cur, (-s) % n, 0)
    if s < 0:
        for j in range(-s):
            src = jnp.where(first, 0.0, prev8[8 + s + j:8 + s + j + 1, :])
            out = jnp.where(rows == j, src, out)
    else:
        for j in range(s):
            src = jnp.where(last, 0.0, next8[j:j + 1, :])
            out = jnp.where(rows == n - s + j, src, out)
    return out


def _ssd_conv_kernel(c_ref, p_ref, n_ref, w_ref, b_ref, o_ref):
    first = pl.program_id(1) == 0
    last = pl.program_id(1) == pl.num_programs(1) - 1
    for c in range(SSD_XBC // 256):
        sl = slice(c * 256, (c + 1) * 256)
        cur, prev8, next8 = c_ref[0, :, sl], p_ref[0, :, sl], n_ref[0, :, sl]
        y = b_ref[:, sl]
        for j in range(SSD_CONV):
            y = y + _shifted(cur, prev8, next8, j - SSD_CONV // 2, first, last) * w_ref[j:j + 1, sl]
        o_ref[0, :, sl] = _silu(y)


def _halo_specs(tm, width, seq):
    cur = pl.BlockSpec((1, tm, width), lambda b, i: (b, i, 0))
    prev = pl.BlockSpec((1, 8, width), lambda b, i: (b, jnp.maximum(i * (tm // 8) - 1, 0), 0))
    nxt = pl.BlockSpec((1, 8, width), lambda b, i: (b, jnp.minimum((i + 1) * (tm // 8), seq // 8 - 1), 0))
    return cur, prev, nxt


def _ssd_conv(xbc, w, bias):
    b, seq, width = xbc.shape
    tm = 512
    cur, prev, nxt = _halo_specs(tm, width, seq)
    return pl.pallas_call(
        _ssd_conv_kernel, grid=(b, seq // tm),
        in_specs=[cur, prev, nxt, pl.BlockSpec((8, width), lambda b, i: (0, 0)),
                  pl.BlockSpec((1, width), lambda b, i: (0, 0))],
        out_specs=cur, out_shape=jax.ShapeDtypeStruct(xbc.shape, F32),
        compiler_params=_cparams(2), name="ssd_conv",
    )(xbc, xbc, xbc, w, bias)


def _softplus(x):
    return jnp.maximum(x, 0.0) + jnp.log1p(jnp.exp(-jnp.abs(x)))


def _tri(kind):
    s = lax.broadcasted_iota(I32, (CHUNK, CHUNK), 0)
    l = lax.broadcasted_iota(I32, (CHUNK, CHUNK), 1)
    return {"le": s <= l, "ge": s >= l, "lt": s < l}[kind]


def _expand(cols, e2_ref):
    hi, lo = _split2(cols)
    return _dot(jnp.concatenate([hi, lo], axis=1), e2_ref[...])


def _ssd_bwd_kernel(xs_ref, b_ref, dt_ref, pc_ref, e2_ref, sb_ref, st):
    @pl.when(pl.program_id(1) == 0)
    def _():
        st[...] = jnp.zeros_like(st)

    sb_ref[0, 0] = st[...].astype(BF16)
    dt_t = dt_ref[0].T
    dtb = _softplus(dt_t[N_HEADS:2 * N_HEADS, :] + pc_ref[:, 1:2])
    a = dtb * pc_ref[:, 3:4]
    ex = _dot3(a, _tri("lt").astype(BF16))
    tot = ex[:, CHUNK - 1:CHUNK] + a[:, CHUNK - 1:CHUNK]
    rowform = jnp.concatenate([dtb * jnp.exp(ex), jnp.broadcast_to(jnp.exp(tot), (N_HEADS, CHUNK)),
                               jnp.zeros((CHUNK - 2 * N_HEADS, CHUNK), F32)], axis=0)
    ex2 = _expand(rowform.T, e2_ref)
    xw = (xs_ref[0] * ex2[:, :D_MODEL]).astype(BF16)
    half = D_MODEL // SSD_GROUPS
    upd = [_dot(b_ref[0, :, g * SSD_STATE:(g + 1) * SSD_STATE].T.astype(BF16), xw[:, g * half:(g + 1) * half])
           for g in range(SSD_GROUPS)]
    st[...] = st[...] * ex2[0:1, D_MODEL:] + jnp.concatenate(upd, axis=1)


def _ssd_fwd_kernel(xs_ref, b_ref, c_ref, dt_ref, z_ref, sb_ref, pc_ref, e3_ref, dexp_ref, on_ref, o_ref, st):
    @pl.when(pl.program_id(1) == 0)
    def _():
        st[...] = jnp.zeros_like(st)

    xs = xs_ref[0]
    dt_t = dt_ref[0].T
    dtf = _softplus(dt_t[0:N_HEADS, :] + pc_ref[:, 0:1])
    dtb = _softplus(dt_t[N_HEADS:2 * N_HEADS, :] + pc_ref[:, 1:2])
    af = dtf * pc_ref[:, 2:3]
    ab = dtb * pc_ref[:, 3:4]
    csf = _dot3(af, _tri("le").astype(BF16))
    rcs = _dot3(ab, _tri("ge").astype(BF16))
    totf = csf[:, CHUNK - 1:CHUNK]
    rowform = jnp.concatenate([dtf * jnp.exp(totf - csf), jnp.exp(csf), jnp.exp(rcs), csf, rcs,
                               jnp.zeros((CHUNK - 5 * N_HEADS, CHUNK), F32)], axis=0)
    cols = rowform.T
    ex3 = _expand(cols, e3_ref)
    w_state, e_f, e_b = ex3[:, :D_MODEL], ex3[:, D_MODEL:2 * D_MODEL], ex3[:, 2 * D_MODEL:]

    xb = xs.astype(BF16)
    lower, upper = _tri("ge"), _tri("le")
    head_a = lax.broadcasted_iota(I32, (CHUNK, LANE), 1) < HEAD_DIM
    half = D_MODEL // SSD_GROUPS
    hpg = N_HEADS // SSD_GROUPS
    st_all = st[...]
    sb_all = sb_ref[0, 0]
    ys = []
    b_t = []
    for g in range(SSD_GROUPS):
        bg = b_ref[0, :, g * SSD_STATE:(g + 1) * SSD_STATE]
        cg = c_ref[0, :, g * SSD_STATE:(g + 1) * SSD_STATE].astype(BF16)
        b_t.append(bg.T.astype(BF16))
        gm = _dot_nt(cg, bg.astype(BF16))
        states = jnp.concatenate([st_all[:, g * half:(g + 1) * half].astype(BF16),
                                  sb_all[:, g * half:(g + 1) * half]], axis=1)
        off = _dot(cg, states)
        y_off = (off[:, :half] * e_f[:, g * half:(g + 1) * half]
                 + off[:, half:] * e_b[:, g * half:(g + 1) * half])
        for pair in range(hpg // 2):
            ms = []
            for h in (g * hpg + 2 * pair, g * hpg + 2 * pair + 1):
                dec_f = jnp.where(lower, jnp.exp(cols[:, 3 * N_HEADS + h:3 * N_HEADS + h + 1] - csf[h:h + 1, :]), 0.0)
                dec_b = jnp.where(upper, jnp.exp(cols[:, 4 * N_HEADS + h:4 * N_HEADS + h + 1] - rcs[h:h + 1, :]), 0.0)
                ms.append((gm * (dec_f * dtf[h:h + 1, :] + dec_b * dtb[h:h + 1, :])).astype(BF16))
            lo = g * half + pair * LANE
            xp = xb[:, lo:lo + LANE]
            ys.append(jnp.where(head_a, _dot(ms[0], xp), _dot(ms[1], xp))
                      + y_off[:, pair * LANE:(pair + 1) * LANE])
    y = jnp.concatenate(ys, axis=1) + xs * dexp_ref[...]
    yz = y * _silu(z_ref[0])
    o_ref[0] = _rms(yz, on_ref[...]).astype(BF16)

    xw = (xs * w_state).astype(BF16)
    upd = [_dot(b_t[g], xw[:, g * half:(g + 1) * half]) for g in range(SSD_GROUPS)]
    st[...] = st_all * e_f[CHUNK - 1:CHUNK, :] + jnp.concatenate(upd, axis=1)


def _ssd(xbc_act, dt, z, pc, dexp, out_norm):
    b, seq, _ = xbc_act.shape
    nc = seq // CHUNK
    sel = np.zeros((2 * CHUNK, 3 * D_MODEL), np.float32)
    for part in range(3):
        for h in range(N_HEADS):
            for rep in range(2):
                sel[rep * CHUNK + part * N_HEADS + h, part * D_MODEL + h * HEAD_DIM:part * D_MODEL + (h + 1) * HEAD_DIM] = 1.0
    e3 = jnp.asarray(sel, BF16)
    e2 = jnp.asarray(sel[:, :2 * D_MODEL], BF16)

    rev = lambda bi, c: (bi, nc - 1 - c, 0)
    sb = pl.pallas_call(
        _ssd_bwd_kernel, grid=(b, nc),
        in_specs=[pl.BlockSpec((1, CHUNK, D_MODEL), rev),
                  pl.BlockSpec((1, CHUNK, 2 * SSD_STATE), lambda bi, c: (bi, nc - 1 - c, D_MODEL // (2 * SSD_STATE))),
                  pl.BlockSpec((1, CHUNK, LANE), rev), _const_spec(pc.shape), _const_spec(e2.shape)],
        out_specs=pl.BlockSpec((1, 1, SSD_STATE, D_MODEL), lambda bi, c: (bi, nc - 1 - c, 0, 0)),
        out_shape=jax.ShapeDtypeStruct((b, nc, SSD_STATE, D_MODEL), BF16),
        scratch_shapes=[pltpu.VMEM((SSD_STATE, D_MODEL), F32)],
        compiler_params=_cparams(2), name="ssd_bwd_state",
    )(xbc_act, xbc_act, dt, pc, e2)

    fwd = lambda bi, c: (bi, c, 0)
    return pl.pallas_call(
        _ssd_fwd_kernel, grid=(b, nc),
        in_specs=[pl.BlockSpec((1, CHUNK, D_MODEL), fwd),
                  pl.BlockSpec((1, CHUNK, 2 * SSD_STATE), lambda bi, c: (bi, c, D_MODEL // (2 * SSD_STATE))),
                  pl.BlockSpec((1, CHUNK, 2 * SSD_STATE), lambda bi, c: (bi, c, D_MODEL // (2 * SSD_STATE) + 1)),
                  pl.BlockSpec((1, CHUNK, LANE), fwd), pl.BlockSpec((1, CHUNK, D_MODEL), fwd),
                  pl.BlockSpec((1, 1, SSD_STATE, D_MODEL), lambda bi, c: (bi, c, 0, 0)),
                  _const_spec(pc.shape), _const_spec(e3.shape), _const_spec((1, D_MODEL)), _const_spec((1, D_MODEL))],
        out_specs=pl.BlockSpec((1, CHUNK, D_MODEL), fwd),
        out_shape=jax.ShapeDtypeStruct((b, seq, D_MODEL), BF16),
        scratch_shapes=[pltpu.VMEM((SSD_STATE, D_MODEL), F32)],
        compiler_params=_cparams(2), name="ssd_fwd",
    )(xbc_act, xbc_act, xbc_act, dt, z, sb, pc, e3, dexp, out_norm)


def _norm_and_route(x1, fg_ref, wr_hi_ref, wr_lo_ref, hn_ref, aff_ref):
    hn = _rms(x1, fg_ref[...])
    hi, lo = _split2(hn)
    hn_ref[...] = hi
    logits = _dot(hi, wr_hi_ref[...]) + _dot(lo, wr_hi_ref[...]) + _dot(hi, wr_lo_ref[...])
    lt = logits.T[0:N_EXPERTS, :]
    e = jnp.exp(lt - jnp.max(lt, axis=0, keepdims=True))
    aff_ref[0] = e / jnp.sum(e, axis=0, keepdims=True)


def _out_even_kernel(x_ref, a_ref, y_ref, wa_ref, wy_ref, fg_ref, wr_hi_ref, wr_lo_ref, x1_ref, hn_ref, aff_ref):
    x1 = x_ref[...] + _dot(a_ref[...], wa_ref[...]) + _dot(y_ref[...], wy_ref[...])
    x1_ref[...] = x1
    _norm_and_route(x1, fg_ref, wr_hi_ref, wr_lo_ref, hn_ref, aff_ref)


def _out_odd_kernel(x_ref, gb_ref, cu_ref, cp_ref, cn_ref, cw_ref, wo_ref, fg_ref, wr_hi_ref, wr_lo_ref,
                    x1_ref, hn_ref, aff_ref, *, tiles_per_seq):
    i = pl.program_id(0) % tiles_per_seq
    first, last = i == 0, i == tiles_per_seq - 1
    cur, prev8, next8 = cu_ref[...], cp_ref[...], cn_ref[...]
    conv = sum(_shifted(cur, prev8, next8, j - SHORT_CONV // 2, first, last) * cw_ref[j:j + 1, :]
               for j in range(SHORT_CONV))
    x1 = x_ref[...] + _dot((gb_ref[...] * conv).astype(BF16), wo_ref[...])
    x1_ref[...] = x1
    _norm_and_route(x1, fg_ref, wr_hi_ref, wr_lo_ref, hn_ref, aff_ref)


def _route_outs(n, b, seq, tm):
    row = pl.BlockSpec((tm, D_MODEL), lambda i: (i, 0))
    aff = pl.BlockSpec((1, N_EXPERTS, tm), lambda i: (i // (seq // tm), 0, i % (seq // tm)))
    shapes = [jax.ShapeDtypeStruct((n, D_MODEL), F32), jax.ShapeDtypeStruct((n, D_MODEL), BF16),
              jax.ShapeDtypeStruct((b, N_EXPERTS, seq), F32)]
    return [row, row, aff], shapes


def _out_even(x, attn, y, wa, wy, fg, wr_hi, wr_lo, b, seq):
    n = x.shape[0]
    tm = ROW_TILE
    row = pl.BlockSpec((tm, D_MODEL), lambda i: (i, 0))
    out_specs, shapes = _route_outs(n, b, seq, tm)
    return pl.pallas_call(
        _out_even_kernel, grid=(n // tm,),
        in_specs=[row, row, row, _const_spec(wa.shape), _const_spec(wy.shape), _const_spec((1, D_MODEL)),
                  _const_spec(wr_hi.shape), _const_spec(wr_lo.shape)],
        out_specs=out_specs, out_shape=shapes, compiler_params=_cparams(1), name="out_even",
    )(x, attn, y, wa, wy, fg, wr_hi, wr_lo)


def _out_odd(x, gb, cu, cw, wo, fg, wr_hi, wr_lo, b, seq):
    n = x.shape[0]
    tm = ROW_TILE
    row = pl.BlockSpec((tm, D_MODEL), lambda i: (i, 0))
    prev = pl.BlockSpec((8, D_MODEL), lambda i: (jnp.maximum(i * (tm // 8) - 1, 0), 0))
    nxt = pl.BlockSpec((8, D_MODEL), lambda i: (jnp.minimum((i + 1) * (tm // 8), n // 8 - 1), 0))
    out_specs, shapes = _route_outs(n, b, seq, tm)
    return pl.pallas_call(
        functools.partial(_out_odd_kernel, tiles_per_seq=seq // tm), grid=(n // tm,),
        in_specs=[row, row, row, prev, nxt, pl.BlockSpec((8, D_MODEL), lambda i: (0, 0)), _const_spec(wo.shape),
                  _const_spec((1, D_MODEL)), _const_spec(wr_hi.shape), _const_spec(wr_lo.shape)],
        out_specs=out_specs, out_shape=shapes, compiler_params=_cparams(1), name="out_odd",
    )(x, gb, cu, cu, cu, cw, wo, fg, wr_hi, wr_lo)


def _in_odd_kernel(x_ref, g_ref, wb_ref, wc_ref, wu_ref, gb_ref, cu_ref):
    hb = _rms(x_ref[...], g_ref[...]).astype(BF16)
    gb_ref[...] = _dot(hb, wb_ref[...])
    cu_ref[...] = _dot(hb, wc_ref[...]) * _dot(hb, wu_ref[...])


def _in_odd(x, g, wb, wc, wu):
    n = x.shape[0]
    tm = ROW_TILE
    row = pl.BlockSpec((tm, D_MODEL), lambda i: (i, 0))
    return pl.pallas_call(
        _in_odd_kernel, grid=(n // tm,),
        in_specs=[row, _const_spec((1, D_MODEL)), _const_spec(wb.shape), _const_spec(wc.shape), _const_spec(wu.shape)],
        out_specs=[row, row], out_shape=[jax.ShapeDtypeStruct((n, D_MODEL), F32)] * 2,
        compiler_params=_cparams(1), name="in_odd",
    )(x, g, wb, wc, wu)


def _count(mask):
    return jnp.sum(jnp.sum(mask.astype(F32), axis=0, keepdims=True), axis=1, keepdims=True)


def _route_kernel(aff_ref, incl_ref, ones_ref, strict_ref, local_ref, group_ref, first_ref, slot_ref, off_ref, end_ref,
                  *, cap):
    for e in range(N_EXPERTS):
        bits = pltpu.bitcast(aff_ref[0, e], I32)

        def step(i, thr):
            cand = thr | (jnp.int32(1) << (30 - i))
            return jnp.where(_count(bits >= cand) >= cap, cand, thr)

        thr = lax.fori_loop(0, 31, step, jnp.zeros((1, 1), I32))
        gt = bits > thr
        eq = (bits == thr).astype(BF16)
        eq_rank = _dot(eq, incl_ref[...]) + _dot(strict_ref[...], _dot(eq, ones_ref[...]).astype(BF16))
        sel = (gt | ((bits == thr) & (eq_rank <= cap - _count(gt)))).astype(BF16)
        within = _dot(sel, incl_ref[...])
        totals = _dot(sel, ones_ref[...]).astype(BF16)
        local = _dot(local_ref[...], totals)
        cnt = _dot(group_ref[...], totals)
        padded = jnp.floor((cnt + (SLOT_ALIGN - 1)) * (1.0 / SLOT_ALIGN)) * SLOT_ALIGN
        start = _dot(first_ref[...], padded.astype(BF16))
        slot_ref[0, e] = jnp.where(sel > 0, (start + local + within).astype(I32) - 1, -1)
        off_ref[0, e:e + 1, :] = start.T[0:1, :].astype(I32)
        end_ref[0, e:e + 1, :] = (start + padded).T[0:1, :].astype(I32)


def _route(aff, cap):
    b, _, seq = aff.shape
    nt = seq // LANE
    tri = np.arange(LANE)
    tt = np.arange(nt)
    grp = tt // SUB
    as_bf16 = lambda m: jnp.asarray(m, BF16)
    incl = as_bf16(tri[:, None] <= tri[None, :])
    strict = as_bf16(tt[None, :] < tt[:, None])
    local = as_bf16((tt[None, :] < tt[:, None]) & (grp[None, :] == grp[:, None]))
    group = as_bf16(grp[None, :] == grp[:, None])
    first = as_bf16((grp[None, :] < grp[:, None]) & (tt[None, :] % SUB == 0))
    ones = jnp.ones((LANE, LANE), BF16)
    tiles = pl.BlockSpec((1, N_EXPERTS, nt, LANE), lambda i: (i, 0, 0, 0))
    rows = pl.BlockSpec((1, N_EXPERTS, nt), lambda i: (i, 0, 0))
    return pl.pallas_call(
        functools.partial(_route_kernel, cap=cap), grid=(b,),
        in_specs=[tiles, _const_spec((LANE, LANE)), _const_spec((LANE, LANE))] + [_const_spec((nt, nt))] * 4,
        out_specs=[tiles, rows, rows],
        out_shape=[jax.ShapeDtypeStruct((b, N_EXPERTS, nt, LANE), I32)] + [jax.ShapeDtypeStruct((b, N_EXPERTS, nt), I32)] * 2,
        compiler_params=_cparams(1), name="route",
    )(aff.reshape(b, N_EXPERTS, nt, LANE), incl, ones, strict, local, group, first)


def _one_hot(slots, base):
    return (slots == lax.broadcasted_iota(I32, (WINDOW, slots.shape[1]), 0) + base).astype(BF16)


def _gather_kernel(off_ref, end_ref, slot_ref, aff_ref, hn_ref, zero_hbm, xe_hbm, stage, extra, sem, xsem):
    del zero_hbm
    bi, j = pl.program_id(0), pl.program_id(1)
    n_j = pl.num_programs(1)
    step = bi * n_j + j
    cur = step % 2

    def window_copy(bb, jj, e, buf):
        start = pl.multiple_of(off_ref[bb, e, jj * SUB], SLOT_ALIGN)
        return pltpu.make_async_copy(stage.at[buf, e], xe_hbm.at[bb, e, pl.ds(start, WINDOW), :], sem.at[buf, e])

    tokens = hn_ref[0]
    parts = [p.astype(F32) for p in _split3(aff_ref[0, 0])]
    gates = jnp.concatenate(parts + [jnp.zeros((LANE - 3 * N_EXPERTS, TOK_TILE), F32)], axis=0).astype(BF16)

    def rows_of(p):
        return jnp.concatenate([_dot(p, tokens), _dot_nt(p, gates)], axis=1).astype(BF16)

    p_all = jnp.concatenate([_one_hot(slot_ref[0, 0, e:e + 1, :], off_ref[bi, e, j * SUB])
                             for e in range(N_EXPERTS)], axis=0)
    stage[cur] = rows_of(p_all).reshape(N_EXPERTS, WINDOW, D_MODEL + LANE)

    @pl.when(step > 0)
    def _():
        prev = step - 1
        for e in range(N_EXPERTS):
            window_copy(prev // n_j, prev % n_j, e, 1 - cur).wait()

    for e in range(N_EXPERTS):
        window_copy(bi, j, e, cur).start()

    def overflow(e, c):
        first = off_ref[bi, e, j * SUB]
        n_win = (end_ref[bi, e, j * SUB] - first + WINDOW - 1) // WINDOW
        slots = slot_ref[0, 0, pl.ds(e, 1), :]

        def one(w, c2):
            base = pl.multiple_of(first + w * WINDOW, SLOT_ALIGN)
            extra[...] = rows_of(_one_hot(slots, base))
            cp = pltpu.make_async_copy(extra, xe_hbm.at[bi, e, pl.ds(base, WINDOW), :], xsem.at[0])
            cp.start()
            cp.wait()
            return c2

        return lax.fori_loop(1, n_win, one, c)

    lax.fori_loop(0, N_EXPERTS, overflow, 0)

    @pl.when(step == pl.num_programs(0) * n_j - 1)
    def _():
        for e in range(N_EXPERTS):
            window_copy(bi, j, e, cur).wait()


def _gather(off, end, slot_t, aff_t, hn):
    b, seq, _ = hn.shape
    cap_pad = _cap_pad(seq)
    width = D_MODEL + LANE
    per_tile = pl.BlockSpec((1, 1, N_EXPERTS, TOK_TILE), lambda bi, j, *_: (bi, j, 0, 0))
    return pl.pallas_call(
        _gather_kernel,
        grid_spec=pltpu.PrefetchScalarGridSpec(
            num_scalar_prefetch=2, grid=(b, seq // TOK_TILE),
            in_specs=[per_tile, per_tile, pl.BlockSpec((1, TOK_TILE, D_MODEL), lambda bi, j, *_: (bi, j, 0)),
                      pl.BlockSpec(memory_space=pl.ANY)],
            out_specs=pl.BlockSpec(memory_space=pl.ANY),
            scratch_shapes=[pltpu.VMEM((2, N_EXPERTS, WINDOW, width), BF16), pltpu.VMEM((WINDOW, width), BF16),
                            pltpu.SemaphoreType.DMA((2, N_EXPERTS)), pltpu.SemaphoreType.DMA((1,))]),
        out_shape=jax.ShapeDtypeStruct((b, N_EXPERTS, cap_pad, width), BF16),
        input_output_aliases={5: 0},
        compiler_params=_cparams(2), name="moe_gather",
    )(off, end, slot_t, aff_t, hn, jnp.zeros((b, N_EXPERTS, cap_pad, width), BF16))


def _ffn_kernel(end_ref, xe_ref, wg_hbm, wu_hbm, wd_hbm, y_ref, stage_g, stage_u, stage_d, wg, wu, wd, sem, *, layer):
    e, bi, r = pl.program_id(0), pl.program_id(1), pl.program_id(2)
    used = end_ref[bi, e, end_ref.shape[2] - 1]
    pairs = ((wg_hbm, stage_g, wg), (wu_hbm, stage_u, wu), (wd_hbm, stage_d, wd))

    def weight_copies(ee):
        return [pltpu.make_async_copy(src.at[layer, ee], stg, sem.at[k]) for k, (src, stg, _) in enumerate(pairs)]

    @pl.when((bi == 0) & (r == 0))
    def _():
        @pl.when(e == 0)
        def _():
            for cp in weight_copies(e):
                cp.start()

        for cp, (_, stg, dst) in zip(weight_copies(e), pairs):
            cp.wait()
            n_rows = stg.shape[0]

            def cast(i, c, stg=stg, dst=dst):
                rows = pl.ds(pl.multiple_of(i * CAST_ROWS, CAST_ROWS), CAST_ROWS)
                dst[rows, :] = stg[rows, :].astype(BF16)
                return c

            lax.fori_loop(0, n_rows // CAST_ROWS, cast, 0)

        @pl.when(e + 1 < pl.num_programs(0))
        def _():
            for cp in weight_copies(e + 1):
                cp.start()

    @pl.when(r * FFN_ROWS < used)
    def _():
        xe = xe_ref[0, 0, :, 0:D_MODEL]
        hid = (_silu(_dot(xe, wg[...])) * _dot(xe, wu[...])).astype(BF16)
        g = xe_ref[0, 0, :, D_MODEL:].astype(F32)
        lane = lax.broadcasted_iota(I32, g.shape, 1)
        mine = (lane % N_EXPERTS == e) & (lane < 3 * N_EXPERTS)
        gate = jnp.sum(jnp.where(mine, g, 0.0), axis=1, keepdims=True)
        y_ref[0, 0] = (_dot(hid, wd[...]) * gate).astype(BF16)

    @pl.when(r * FFN_ROWS >= used)
    def _():
        y_ref[...] = jnp.zeros_like(y_ref)


def _ffn(end, xe, wg, wu, wd, layer):
    b, ne, cap_pad, width = xe.shape
    rows = lambda w: pl.BlockSpec((1, 1, FFN_ROWS, w), lambda e, bi, r, *_: (bi, e, r, 0))
    hbm = pl.BlockSpec(memory_space=pl.ANY)
    return pl.pallas_call(
        functools.partial(_ffn_kernel, layer=layer),
        grid_spec=pltpu.PrefetchScalarGridSpec(
            num_scalar_prefetch=1, grid=(ne, b, cap_pad // FFN_ROWS),
            in_specs=[rows(width), hbm, hbm, hbm],
            out_specs=rows(D_MODEL),
            scratch_shapes=[pltpu.VMEM((D_MODEL, D_FF), F32), pltpu.VMEM((D_MODEL, D_FF), F32),
                            pltpu.VMEM((D_FF, D_MODEL), F32), pltpu.VMEM((D_MODEL, D_FF), BF16),
                            pltpu.VMEM((D_MODEL, D_FF), BF16), pltpu.VMEM((D_FF, D_MODEL), BF16),
                            pltpu.SemaphoreType.DMA((3,))]),
        out_shape=jax.ShapeDtypeStruct((b, ne, cap_pad, D_MODEL), BF16),
        compiler_params=_cparams(3), name="moe_ffn",
    )(end, xe, wg, wu, wd)


def _combine_kernel(off_ref, end_ref, slot_ref, y_hbm, x1_ref, o_ref, win, extra, sem, xsem):
    bi, j = pl.program_id(0), pl.program_id(1)
    n_j = pl.num_programs(1)
    step = bi * n_j + j
    cur = step % 2

    def window_copy(bb, jj, e, buf):
        start = pl.multiple_of(off_ref[bb, e, jj * SUB], SLOT_ALIGN)
        return pltpu.make_async_copy(y_hbm.at[bb, e, pl.ds(start, WINDOW), :], win.at[buf, e], sem.at[buf, e])

    @pl.when(step == 0)
    def _():
        for e in range(N_EXPERTS):
            window_copy(bi, j, e, cur).start()

    @pl.when(step + 1 < pl.num_programs(0) * n_j)
    def _():
        nxt = step + 1
        for e in range(N_EXPERTS):
            window_copy(nxt // n_j, nxt % n_j, e, 1 - cur).start()

    ps = []
    for e in range(N_EXPERTS):
        window_copy(bi, j, e, cur).wait()
        ps.append(_one_hot(slot_ref[0, 0, e:e + 1, :], off_ref[bi, e, j * SUB]))
    p_all = jnp.concatenate(ps, axis=0)
    y_all = win[cur].reshape(N_EXPERTS * WINDOW, D_MODEL)
    o_ref[0] = x1_ref[0] + _dot_tn(p_all, y_all)

    def overflow(e, c):
        first = off_ref[bi, e, j * SUB]
        n_win = (end_ref[bi, e, j * SUB] - first + WINDOW - 1) // WINDOW
        slots = slot_ref[0, 0, pl.ds(e, 1), :]

        def one(w, c2):
            base = pl.multiple_of(first + w * WINDOW, SLOT_ALIGN)
            cp = pltpu.make_async_copy(y_hbm.at[bi, e, pl.ds(base, WINDOW), :], extra, xsem.at[0])
            cp.start()
            cp.wait()
            o_ref[0] += _dot_tn(_one_hot(slots, base), extra[...])
            return c2

        return lax.fori_loop(1, n_win, one, c)

    lax.fori_loop(0, N_EXPERTS, overflow, 0)


def _combine(off, end, slot_t, y, x1):
    b, seq, _ = x1.shape
    tile = pl.BlockSpec((1, TOK_TILE, D_MODEL), lambda bi, j, *_: (bi, j, 0))
    return pl.pallas_call(
        _combine_kernel,
        grid_spec=pltpu.PrefetchScalarGridSpec(
            num_scalar_prefetch=2, grid=(b, seq // TOK_TILE),
            in_specs=[pl.BlockSpec((1, 1, N_EXPERTS, TOK_TILE), lambda bi, j, *_: (bi, j, 0, 0)),
                      pl.BlockSpec(memory_space=pl.ANY), tile],
            out_specs=tile,
            scratch_shapes=[pltpu.VMEM((2, N_EXPERTS, WINDOW, D_MODEL), BF16), pltpu.VMEM((WINDOW, D_MODEL), BF16),
                            pltpu.SemaphoreType.DMA((2, N_EXPERTS)), pltpu.SemaphoreType.DMA((1,))]),
        out_shape=jax.ShapeDtypeStruct((b, seq, D_MODEL), F32),
        compiler_params=_cparams(2), name="moe_combine",
    )(off, end, slot_t, y, x1)


def _cap_pad(seq):
    cap = CAPACITY_FACTOR * seq // N_EXPERTS
    worst = cap + (seq // TOK_TILE) * (SLOT_ALIGN - 1) + WINDOW
    return -(-worst // FFN_ROWS) * FFN_ROWS


def _moe(x1, hn, aff, wg, wu, wd, layer):
    b, seq, _ = x1.shape
    cap = CAPACITY_FACTOR * seq // N_EXPERTS
    slot, off, end = _route(aff, cap)
    per_tile = lambda a: jnp.swapaxes(a.reshape(b, N_EXPERTS, seq // TOK_TILE, TOK_TILE), 1, 2)
    slot_t = per_tile(slot)
    xe = _gather(off, end, slot_t, per_tile(aff), hn)
    y = _ffn(end, xe, wg, wu, wd, layer)
    return _combine(off, end, slot_t, y, x1)


def _rope_tables(seq):
    inv_freq = ROPE_THETA ** (-jnp.arange(ROPE_HALF, dtype=F32) * 2.0 / (2 * ROPE_HALF))
    ang = jnp.arange(seq, dtype=F32)[:, None] * inv_freq[None, :]
    cos, sin = jnp.cos(ang), jnp.sin(ang)
    z = lambda w: jnp.zeros((seq, w), F32)
    rest = HEAD_DIM - 2 * ROPE_HALF
    rc = jnp.concatenate([cos, cos, jnp.ones((seq, rest), F32)], axis=1)
    r1 = jnp.concatenate([-sin, z(ROPE_HALF + rest)], axis=1)
    r2 = jnp.concatenate([z(ROPE_HALF), sin, z(rest)], axis=1)
    return tuple(jnp.tile(t, (1, LANE // HEAD_DIM)) for t in (rc, r1, r2))


def _router_split(w):
    wp = jnp.pad(w, ((0, 0), (0, LANE - N_EXPERTS)))
    hi = wp.astype(BF16)
    return hi, (wp - hi.astype(F32)).astype(BF16)


def kernel(x, attn_norm, w_in_even, q_norm, k_norm, ssd_conv_w, ssd_conv_b, ssd_a_log_fwd, ssd_a_log_bwd,
           ssd_dt_bias_fwd, ssd_dt_bias_bwd, ssd_d, ssd_out_norm, w_out_even, conv_norm, conv_w_in, conv_w,
           conv_w_out, ffn_norm, router_w, expert_w_gate, expert_w_up, expert_w_down):
    b, seq, _ = x.shape
    n = b * seq
    depth = ffn_norm.shape[0]
    rc, r1, r2 = _rope_tables(seq)
    blk = np.arange(256) // HEAD_DIM
    bd = jnp.asarray((blk[:, None] == blk[None, :]) / HEAD_DIM, BF16)
    row = lambda v: v.reshape(1, -1).astype(F32)

    xf = x.reshape(n, D_MODEL)
    for layer in range(depth):
        i = layer // 2
        wr_hi, wr_lo = _router_split(router_w[layer])
        fg = row(ffn_norm[layer])
        if layer % 2 == 0:
            w = w_in_even[i].astype(BF16)
            o = np.cumsum([0, D_MODEL, D_MODEL, D_MODEL, D_MODEL, SSD_XBC, N_HEADS, N_HEADS])
            wq, wk, wv, wz, wx = (w[:, o[j]:o[j + 1]] for j in range(5))
            wd = jnp.pad(w[:, o[5]:o[7]], ((0, 0), (0, LANE - 2 * N_HEADS)))
            tile_heads = lambda g: row(jnp.tile(g, N_HEADS))
            q, k, v, z, xbc, dt = _in_even(xf, row(attn_norm[i]), wq, wk, wv, wz, wx, wd,
                                           tile_heads(q_norm[i]), tile_heads(k_norm[i]), bd, rc, r1, r2, seq)
            as3 = lambda t: t.reshape(b, seq, -1)
            attn = _attention(as3(q), as3(k), as3(v))
            cw = jnp.pad(ssd_conv_w[i], ((0, 8 - SSD_CONV), (0, 0)))
            act = _ssd_conv(as3(xbc), cw, row(ssd_conv_b[i]))
            pc = jnp.pad(jnp.stack([ssd_dt_bias_fwd[i], ssd_dt_bias_bwd[i], -jnp.exp(ssd_a_log_fwd[i]),
                                    -jnp.exp(ssd_a_log_bwd[i])], axis=1).astype(F32), ((0, 0), (0, LANE - 4)))
            y = _ssd(act, as3(dt), as3(z), pc, row(jnp.repeat(ssd_d[i], HEAD_DIM)), row(ssd_out_norm[i]))
            wo = w_out_even[i].astype(BF16)
            x1, hn, aff = _out_even(xf, attn.reshape(n, D_MODEL), y.reshape(n, D_MODEL), wo[:D_MODEL], wo[D_MODEL:],
                                    fg, wr_hi, wr_lo, b, seq)
        else:
            w = conv_w_in[i].astype(BF16)
            gb, cu = _in_odd(xf, row(conv_norm[i]), w[:, :D_MODEL], w[:, D_MODEL:2 * D_MODEL], w[:, 2 * D_MODEL:])
            cw = jnp.pad(conv_w[i], ((0, 8 - SHORT_CONV), (0, 0)))
            x1, hn, aff = _out_odd(xf, gb, cu, cw, conv_w_out[i].astype(BF16), fg, wr_hi, wr_lo, b, seq)
        xf = _moe(x1.reshape(b, seq, D_MODEL), hn.reshape(b, seq, D_MODEL), aff, expert_w_gate, expert_w_up,
                  expert_w_down, layer).reshape(n, D_MODEL)
    return xf.reshape(b, seq, D_MODEL)
```

```python
import functools
import math

import jax
import jax.numpy as jnp
import numpy as np
from jax import lax
from jax.experimental import pallas as pl
from jax.experimental.pallas import tpu as pltpu

F32, BF16, I32 = jnp.float32, jnp.bfloat16, jnp.int32

D_MODEL = 1024
N_HEADS = 16
HEAD_DIM = 64
ROPE_HALF = 8
ROPE_THETA = 500000.0
PATTERNS = ((128, 1), (512, 4), (2048, 16))
HALF_STEPS = 64
SSD_GROUPS = 2
SSD_STATE = 128
SSD_XBC = 1536
SSD_CONV = 5
CHUNK = 128
N_EXPERTS = 16
CAPACITY_FACTOR = 2
D_FF = 2048
SHORT_CONV = 3
EPS = 1e-6

LANE = 128
VMEM_LIMIT = 56 * 1024 * 1024

ROW_TILE = 256
ATTN_SUPER = 2048
ATTN_HALO = 1024
ATTN_TQ = 128
ATTN_TK = ATTN_TQ + 2 * HALF_STEPS
ATTN_UNROLL = 8
TOK_TILE = 512
SUB = TOK_TILE // LANE
SLOT_ALIGN = 16
WINDOW = 64
FFN_ROWS = 256
CAST_ROWS = 64


def _cparams(n_axes):
    return pltpu.CompilerParams(dimension_semantics=("arbitrary",) * n_axes, vmem_limit_bytes=VMEM_LIMIT)


def _const_spec(shape):
    nd = len(shape)
    return pl.BlockSpec(shape, lambda *_: (0,) * nd, pipeline_mode=pl.Buffered(1))


def _dot(a, b):
    return jnp.dot(a, b, preferred_element_type=F32)


def _dot_nt(a, b):
    return lax.dot_general(a, b, (((1,), (1,)), ((), ())), preferred_element_type=F32)


def _dot_tn(a, b):
    return lax.dot_general(a, b, (((0,), (0,)), ((), ())), preferred_element_type=F32)


def _split2(x):
    hi = x.astype(BF16)
    lo = (x - hi.astype(F32)).astype(BF16)
    return hi, lo


def _split3(x):
    hi = x.astype(BF16)
    r = x - hi.astype(F32)
    mid = r.astype(BF16)
    lo = (r - mid.astype(F32)).astype(BF16)
    return hi, mid, lo


def _dot3(x, m_bf16):
    hi, mid, lo = _split3(x)
    return _dot(hi, m_bf16) + _dot(mid, m_bf16) + _dot(lo, m_bf16)


def _rms(x, g):
    return x * lax.rsqrt(jnp.mean(x * x, axis=-1, keepdims=True) + EPS) * g


def _silu(x):
    return x * jax.nn.sigmoid(x)


def _in_even_kernel(x_ref, g_ref, wq_ref, wk_ref, wv_ref, wz_ref, wx_ref, wd_ref, qg_ref, kg_ref, bd_ref,
                    rc_ref, r1_ref, r2_ref, q_ref, k_ref, v_ref, z_ref, xbc_ref, dt_ref):
    hb = _rms(x_ref[...], g_ref[...]).astype(BF16)
    bd = bd_ref[...]
    rc, r1, r2 = rc_ref[...], r1_ref[...], r2_ref[...]

    def head_norm_rope(w_ref, gain_ref, out_ref):
        t = _dot(hb, w_ref[...])
        for c in range(D_MODEL // 256):
            tc = t[:, c * 256:(c + 1) * 256]
            sq_hi, sq_lo = _split2(tc * tc)
            ms = _dot(sq_hi, bd) + _dot(sq_lo, bd)
            tn = tc * lax.rsqrt(ms + EPS) * gain_ref[:, c * 256:(c + 1) * 256]
            for hh in range(2):
                u = tn[:, hh * LANE:(hh + 1) * LANE]
                r = u * rc + pltpu.roll(u, LANE - ROPE_HALF, 1) * r1 + pltpu.roll(u, ROPE_HALF, 1) * r2
                out_ref[:, c * 256 + hh * LANE:c * 256 + (hh + 1) * LANE] = r.astype(BF16)

    head_norm_rope(wq_ref, qg_ref, q_ref)
    head_norm_rope(wk_ref, kg_ref, k_ref)
    v_ref[...] = _dot(hb, wv_ref[...]).astype(BF16)
    z_ref[...] = _dot(hb, wz_ref[...])
    xbc_ref[...] = _dot(hb, wx_ref[...])
    dt_ref[...] = _dot(hb, wd_ref[...])


def _in_even(x, g, wq, wk, wv, wz, wx, wd, qg, kg, bd, rc, r1, r2, seq):
    n = x.shape[0]
    tm = ROW_TILE
    row = lambda w: pl.BlockSpec((tm, w), lambda i: (i, 0))
    tab = pl.BlockSpec((tm, LANE), lambda i: (i % (seq // tm), 0))
    outs = [jax.ShapeDtypeStruct((n, D_MODEL), BF16)] * 3 + [jax.ShapeDtypeStruct((n, D_MODEL), F32),
                                                            jax.ShapeDtypeStruct((n, SSD_XBC), F32),
                                                            jax.ShapeDtypeStruct((n, LANE), F32)]
    return pl.pallas_call(
        _in_even_kernel, grid=(n // tm,),
        in_specs=[row(D_MODEL), _const_spec((1, D_MODEL)), _const_spec(wq.shape), _const_spec(wk.shape),
                  _const_spec(wv.shape), _const_spec(wz.shape), _const_spec(wx.shape), _const_spec(wd.shape),
                  _const_spec((1, D_MODEL)), _const_spec((1, D_MODEL)), _const_spec((256, 256)), tab, tab, tab],
        out_specs=[row(D_MODEL)] * 4 + [row(SSD_XBC), row(LANE)], out_shape=outs,
        compiler_params=_cparams(1), name="in_even",
    )(x, g, wq, wk, wv, wz, wx, wd, qg, kg, bd, rc, r1, r2)


def _attn_kernel(q_ref, k0, k1, k2, k3, v0, v1, v2, v3, qw_hbm, kw_hbm, vw_hbm, o_ref, qf, kf, vf, q16, k16, v16,
                 acc, mst, lst, a16, m16, l16, an3, mn3, ln3, sem, *, seq):
    dmax = PATTERNS[-1][1]
    bi, hp, j = pl.program_id(0), pl.program_id(1), pl.program_id(2)
    p0 = j * ATTN_SUPER
    halo = ATTN_HALO // dmax

    def residue_copies(r, where):
        lanes = pl.ds(pl.multiple_of((r * (D_MODEL // LANE) + hp) * LANE, LANE), LANE)
        q0 = pl.multiple_of(j * ATTN_TQ, ATTN_TQ)
        n_kv = ATTN_TK - halo if where else ATTN_TK
        src0 = 0 if where < 0 else q0 - halo
        dst0 = halo if where < 0 else 0
        cps = [pltpu.make_async_copy(qw_hbm.at[bi, pl.ds(q0, ATTN_TQ), lanes], q16.at[r], sem.at[0, r])]
        for n, (src, dst) in enumerate(((kw_hbm, k16), (vw_hbm, v16))):
            cps.append(pltpu.make_async_copy(src.at[bi, pl.ds(src0, n_kv), lanes], dst.at[r, pl.ds(dst0, n_kv), :],
                                             sem.at[1 + n, r]))
        return cps

    def per_position(fn):
        last = pl.num_programs(2) - 1
        pl.when(j == 0)(functools.partial(fn, -1))
        pl.when((j > 0) & (j < last))(functools.partial(fn, 0))
        pl.when(j == last)(functools.partial(fn, 1))

    def state_copies(r):
        return [pltpu.make_async_copy(src.at[r], dst.at[:, r, :], sem.at[3 + n, r])
                for n, (src, dst) in enumerate(((a16, an3), (m16, mn3), (l16, ln3)))]

    @pl.when((bi == 0) & (hp == 0) & (j == 0))
    def _():
        k16[...] = jnp.zeros_like(k16)
        v16[...] = jnp.zeros_like(v16)

    def start_residues(where):
        for r in range(dmax):
            for cp in residue_copies(r, where):
                cp.start()

    def wait_residues(where):
        for r in range(dmax):
            for cp in residue_copies(r, where):
                cp.wait()

    per_position(start_residues)

    qf[...] = q_ref[0].astype(F32) * (HEAD_DIM ** -0.5)
    for i, (kr, vr) in enumerate(((k0, v0), (k1, v1), (k2, v2), (k3, v3))):
        kf[i * ATTN_HALO:(i + 1) * ATTN_HALO, :] = kr[0].astype(F32)
        vf[i * ATTN_HALO:(i + 1) * ATTN_HALO, :] = vr[0].astype(F32)

    head_a = lax.broadcasted_iota(I32, (ATTN_TQ, LANE), 1) < HEAD_DIM
    off = lax.broadcasted_iota(I32, (ATTN_TQ, ATTN_TK), 1) - lax.broadcasted_iota(I32, (ATTN_TQ, ATTN_TK), 0)
    band = (off >= 0) & (off <= 2 * HALF_STEPS)
    band2 = jnp.concatenate([band, band], axis=0)
    colpos = lax.broadcasted_iota(I32, (1, ATTN_TK), 1)

    def local_softmax(q, kt, vt, first_pos, d):
        pos = first_pos + d * colpos
        valid = band2 & (pos >= 0) & (pos < seq)
        q2 = jnp.concatenate([jnp.where(head_a, q, 0.0), jnp.where(head_a, 0.0, q)], axis=0).astype(BF16)
        s = jnp.where(valid, _dot_nt(q2, kt.astype(BF16)), -jnp.inf)
        m = jnp.max(s, axis=1, keepdims=True)
        p = jnp.exp(s - m)
        l = jnp.sum(p, axis=1, keepdims=True)
        n = _dot(p.astype(BF16), vt.astype(BF16))
        return (jnp.where(head_a, m[:ATTN_TQ], m[ATTN_TQ:]), jnp.where(head_a, l[:ATTN_TQ], l[ATTN_TQ:]),
                jnp.where(head_a, n[:ATTN_TQ], n[ATTN_TQ:]))

    def visit(d, i, first):
        log_d = d.bit_length() - 1
        r = i & (d - 1)
        t = i >> log_d
        qs = r + d * ATTN_TQ * t
        ks = ATTN_HALO + r + d * (ATTN_TQ * t - HALF_STEPS)
        m_loc, l_loc, n_loc = local_softmax(qf[pl.ds(qs, ATTN_TQ, stride=d), :], kf[pl.ds(ks, ATTN_TK, stride=d), :],
                                            vf[pl.ds(ks, ATTN_TK, stride=d), :], p0 - ATTN_HALO + ks, d)
        rows = pl.ds(qs, ATTN_TQ, stride=d)
        if first:
            mst[rows, :], lst[rows, :], acc[rows, :] = m_loc, l_loc, n_loc
        else:
            m_old = mst[rows, :]
            m_new = jnp.maximum(m_old, m_loc)
            w_old = jnp.exp(m_old - m_new)
            w_loc = jnp.exp(m_loc - m_new)
            mst[rows, :] = m_new
            lst[rows, :] = lst[rows, :] * w_old + l_loc * w_loc
            acc[rows, :] = acc[rows, :] * w_old + n_loc * w_loc

    def visit_dmax(r):
        q = q16[r].astype(F32) * (HEAD_DIM ** -0.5)
        m16[r], l16[r], a16[r] = local_softmax(q, k16[r], v16[r], p0 - ATTN_HALO + r, dmax)

    def loop(n, fn):
        def body(i, c):
            for u in range(ATTN_UNROLL):
                fn(i * ATTN_UNROLL + u)
            return c
        lax.fori_loop(0, n // ATTN_UNROLL, body, 0)

    n_visits = ATTN_SUPER // ATTN_TQ
    loop(n_visits, lambda i: visit(PATTERNS[0][1], i, True))
    per_position(wait_residues)
    loop(dmax, visit_dmax)
    for r in range(dmax):
        for cp in state_copies(r):
            cp.start()
    for _, d in PATTERNS[1:-1]:
        loop(n_visits, lambda i, d=d: visit(d, i, False))
    for r in range(dmax):
        for cp in state_copies(r):
            cp.wait()

    def finish(i, c):
        rows = pl.ds(pl.multiple_of(i * ATTN_TQ, ATTN_TQ), ATTN_TQ)
        slabs = pl.ds(pl.multiple_of(i * (ATTN_TQ // dmax), ATTN_TQ // dmax), ATTN_TQ // dmax)
        a_a, m_a, l_a = (t[slabs].reshape(ATTN_TQ, LANE) for t in (an3, mn3, ln3))
        m_b = mst[rows, :]
        m = jnp.maximum(m_a, m_b)
        w_a, w_b = jnp.exp(m_a - m), jnp.exp(m_b - m)
        o = (a_a * w_a + acc[rows, :] * w_b) / (l_a * w_a + lst[rows, :] * w_b)
        o_ref[0, rows, :] = o.astype(BF16)
        return c

    lax.fori_loop(0, n_visits, finish, 0)


def _attention(q, k, v):
    b, seq, _ = q.shape
    nblk = seq // ATTN_HALO
    ratio = ATTN_SUPER // ATTN_HALO
    dmax = PATTERNS[-1][1]
    assert ATTN_SUPER // dmax == ATTN_TQ and 4 * ATTN_HALO // dmax == ATTN_TK
    assert seq // ATTN_SUPER >= 2

    def halo(i):
        return pl.BlockSpec((1, ATTN_HALO, LANE),
                            lambda bi, hp, j: (bi, jnp.clip(ratio * j - 1 + i, 0, nblk - 1), hp))

    main = pl.BlockSpec((1, ATTN_SUPER, LANE), lambda bi, hp, j: (bi, j, hp))
    hbm = pl.BlockSpec(memory_space=pl.ANY)
    by_residue = lambda t: t.reshape(b, seq // dmax, dmax * D_MODEL)
    return pl.pallas_call(
        functools.partial(_attn_kernel, seq=seq),
        grid=(b, D_MODEL // LANE, seq // ATTN_SUPER),
        in_specs=[main] + [halo(i) for i in range(4)] * 2 + [hbm] * 3,
        out_specs=main,
        out_shape=jax.ShapeDtypeStruct((b, seq, D_MODEL), BF16),
        scratch_shapes=[pltpu.VMEM((ATTN_SUPER, LANE), F32), pltpu.VMEM((4 * ATTN_HALO, LANE), F32),
                        pltpu.VMEM((4 * ATTN_HALO, LANE), F32), pltpu.VMEM((dmax, ATTN_TQ, LANE), BF16),
                        pltpu.VMEM((dmax, ATTN_TK, LANE), BF16), pltpu.VMEM((dmax, ATTN_TK, LANE), BF16)]
                       + [pltpu.VMEM((ATTN_SUPER, LANE), F32)] * 3
                       + [pltpu.VMEM((dmax, ATTN_TQ, LANE), F32)] * 3
                       + [pltpu.VMEM((ATTN_TQ, dmax, LANE), F32)] * 3
                       + [pltpu.SemaphoreType.DMA((6, dmax))],
        compiler_params=_cparams(3), name="dilated_attn",
    )(q, k, k, k, k, v, v, v, v, by_residue(q), by_residue(k), by_residue(v))


def _shifted(cur, prev8, next8, s, first, last):
    n = cur.shape[0]
    if s == 0:
        return cur
    rows = lax.broadcasted_iota(I32, cur.shape, 0)
    out = pltpu.roll(cur, (-s) % n, 0)
    if s < 0:
        for j in range(-s):
            src = jnp.where(first, 0.0, prev8[8 + s + j:8 + s + j + 1, :])
            out = jnp.where(rows == j, src, out)
    else:
        for j in range(s):
            src = jnp.where(last, 0.0, next8[j:j + 1, :])
            out = jnp.where(rows == n - s + j, src, out)
    return out


def _ssd_conv_kernel(c_ref, p_ref, n_ref, w_ref, b_ref, o_ref):
    first = pl.program_id(1) == 0
    last = pl.program_id(1) == pl.num_programs(1) - 1
    for c in range(SSD_XBC // 256):
        sl = slice(c * 256, (c + 1) * 256)
        cur, prev8, next8 = c_ref[0, :, sl], p_ref[0, :, sl], n_ref[0, :, sl]
        y = b_ref[:, sl]
        for j in range(SSD_CONV):
            y = y + _shifted(cur, prev8, next8, j - SSD_CONV // 2, first, last) * w_ref[j:j + 1, sl]
        o_ref[0, :, sl] = _silu(y)


def _halo_specs(tm, width, seq):
    cur = pl.BlockSpec((1, tm, width), lambda b, i: (b, i, 0))
    prev = pl.BlockSpec((1, 8, width), lambda b, i: (b, jnp.maximum(i * (tm // 8) - 1, 0), 0))
    nxt = pl.BlockSpec((1, 8, width), lambda b, i: (b, jnp.minimum((i + 1) * (tm // 8), seq // 8 - 1), 0))
    return cur, prev, nxt


def _ssd_conv(xbc, w, bias):
    b, seq, width = xbc.shape
    tm = 512
    cur, prev, nxt = _halo_specs(tm, width, seq)
    return pl.pallas_call(
        _ssd_conv_kernel, grid=(b, seq // tm),
        in_specs=[cur, prev, nxt, pl.BlockSpec((8, width), lambda b, i: (0, 0)),
                  pl.BlockSpec((1, width), lambda b, i: (0, 0))],
        out_specs=cur, out_shape=jax.ShapeDtypeStruct(xbc.shape, F32),
        compiler_params=_cparams(2), name="ssd_conv",
    )(xbc, xbc, xbc, w, bias)


def _softplus(x):
    return jnp.maximum(x, 0.0) + jnp.log1p(jnp.exp(-jnp.abs(x)))


def _tri(kind):
    s = lax.broadcasted_iota(I32, (CHUNK, CHUNK), 0)
    l = lax.broadcasted_iota(I32, (CHUNK, CHUNK), 1)
    return {"le": s <= l, "ge": s >= l, "lt": s < l}[kind]


def _expand(cols, e2_ref):
    hi, lo = _split2(cols)
    return _dot(jnp.concatenate([hi, lo], axis=1), e2_ref[...])


def _ssd_bwd_kernel(xs_ref, b_ref, dt_ref, pc_ref, e2_ref, sb_ref, st):
    @pl.when(pl.program_id(1) == 0)
    def _():
        st[...] = jnp.zeros_like(st)

    sb_ref[0, 0] = st[...].astype(BF16)
    dt_t = dt_ref[0].T
    dtb = _softplus(dt_t[N_HEADS:2 * N_HEADS, :] + pc_ref[:, 1:2])
    a = dtb * pc_ref[:, 3:4]
    ex = _dot3(a, _tri("lt").astype(BF16))
    tot = ex[:, CHUNK - 1:CHUNK] + a[:, CHUNK - 1:CHUNK]
    rowform = jnp.concatenate([dtb * jnp.exp(ex), jnp.broadcast_to(jnp.exp(tot), (N_HEADS, CHUNK)),
                               jnp.zeros((CHUNK - 2 * N_HEADS, CHUNK), F32)], axis=0)
    ex2 = _expand(rowform.T, e2_ref)
    xw = (xs_ref[0] * ex2[:, :D_MODEL]).astype(BF16)
    half = D_MODEL // SSD_GROUPS
    upd = [_dot(b_ref[0, :, g * SSD_STATE:(g + 1) * SSD_STATE].T.astype(BF16), xw[:, g * half:(g + 1) * half])
           for g in range(SSD_GROUPS)]
    st[...] = st[...] * ex2[0:1, D_MODEL:] + jnp.concatenate(upd, axis=1)


def _ssd_fwd_kernel(xs_ref, b_ref, c_ref, dt_ref, z_ref, sb_ref, pc_ref, e3_ref, dexp_ref, on_ref, o_ref, st):
    @pl.when(pl.program_id(1) == 0)
    def _():
        st[...] = jnp.zeros_like(st)

    xs = xs_ref[0]
    dt_t = dt_ref[0].T
    dtf = _softplus(dt_t[0:N_HEADS, :] + pc_ref[:, 0:1])
    dtb = _softplus(dt_t[N_HEADS:2 * N_HEADS, :] + pc_ref[:, 1:2])
    af = dtf * pc_ref[:, 2:3]
    ab = dtb * pc_ref[:, 3:4]
    csf = _dot3(af, _tri("le").astype(BF16))
    rcs = _dot3(ab, _tri("ge").astype(BF16))
    totf = csf[:, CHUNK - 1:CHUNK]
    rowform = jnp.concatenate([dtf * jnp.exp(totf - csf), jnp.exp(csf), jnp.exp(rcs), csf, rcs,
                               jnp.zeros((CHUNK - 5 * N_HEADS, CHUNK), F32)], axis=0)
    cols = rowform.T
    ex3 = _expand(cols, e3_ref)
    w_state, e_f, e_b = ex3[:, :D_MODEL], ex3[:, D_MODEL:2 * D_MODEL], ex3[:, 2 * D_MODEL:]

    xb = xs.astype(BF16)
    lower, upper = _tri("ge"), _tri("le")
    head_a = lax.broadcasted_iota(I32, (CHUNK, LANE), 1) < HEAD_DIM
    half = D_MODEL // SSD_GROUPS
    hpg = N_HEADS // SSD_GROUPS
    st_all = st[...]
    sb_all = sb_ref[0, 0]
    ys = []
    b_t = []
    for g in range(SSD_GROUPS):
        bg = b_ref[0, :, g * SSD_STATE:(g + 1) * SSD_STATE]
        cg = c_ref[0, :, g * SSD_STATE:(g + 1) * SSD_STATE].astype(BF16)
        b_t.append(bg.T.astype(BF16))
        gm = _dot_nt(cg, bg.astype(BF16))
        states = jnp.concatenate([st_all[:, g * half:(g + 1) * half].astype(BF16),
                                  sb_all[:, g * half:(g + 1) * half]], axis=1)
        off = _dot(cg, states)
        y_off = (off[:, :half] * e_f[:, g * half:(g + 1) * half]
                 + off[:, half:] * e_b[:, g * half:(g + 1) * half])
        for pair in range(hpg // 2):
            ms = []
            for h in (g * hpg + 2 * pair, g * hpg + 2 * pair + 1):
                dec_f = jnp.where(lower, jnp.exp(cols[:, 3 * N_HEADS + h:3 * N_HEADS + h + 1] - csf[h:h + 1, :]), 0.0)
                dec_b = jnp.where(upper, jnp.exp(cols[:, 4 * N_HEADS + h:4 * N_HEADS + h + 1] - rcs[h:h + 1, :]), 0.0)
                ms.append((gm * (dec_f * dtf[h:h + 1, :] + dec_b * dtb[h:h + 1, :])).astype(BF16))
            lo = g * half + pair * LANE
            xp = xb[:, lo:lo + LANE]
            ys.append(jnp.where(head_a, _dot(ms[0], xp), _dot(ms[1], xp))
                      + y_off[:, pair * LANE:(pair + 1) * LANE])
    y = jnp.concatenate(ys, axis=1) + xs * dexp_ref[...]
    yz = y * _silu(z_ref[0])
    o_ref[0] = _rms(yz, on_ref[...]).astype(BF16)

    xw = (xs * w_state).astype(BF16)
    upd = [_dot(b_t[g], xw[:, g * half:(g + 1) * half]) for g in range(SSD_GROUPS)]
    st[...] = st_all * e_f[CHUNK - 1:CHUNK, :] + jnp.concatenate(upd, axis=1)


def _ssd(xbc_act, dt, z, pc, dexp, out_norm):
    b, seq, _ = xbc_act.shape
    nc = seq // CHUNK
    sel = np.zeros((2 * CHUNK, 3 * D_MODEL), np.float32)
    for part in range(3):
        for h in range(N_HEADS):
            for rep in range(2):
                sel[rep * CHUNK + part * N_HEADS + h, part * D_MODEL + h * HEAD_DIM:part * D_MODEL + (h + 1) * HEAD_DIM] = 1.0
    e3 = jnp.asarray(sel, BF16)
    e2 = jnp.asarray(sel[:, :2 * D_MODEL], BF16)

    rev = lambda bi, c: (bi, nc - 1 - c, 0)
    sb = pl.pallas_call(
        _ssd_bwd_kernel, grid=(b, nc),
        in_specs=[pl.BlockSpec((1, CHUNK, D_MODEL), rev),
                  pl.BlockSpec((1, CHUNK, 2 * SSD_STATE), lambda bi, c: (bi, nc - 1 - c, D_MODEL // (2 * SSD_STATE))),
                  pl.BlockSpec((1, CHUNK, LANE), rev), _const_spec(pc.shape), _const_spec(e2.shape)],
        out_specs=pl.BlockSpec((1, 1, SSD_STATE, D_MODEL), lambda bi, c: (bi, nc - 1 - c, 0, 0)),
        out_shape=jax.ShapeDtypeStruct((b, nc, SSD_STATE, D_MODEL), BF16),
        scratch_shapes=[pltpu.VMEM((SSD_STATE, D_MODEL), F32)],
        compiler_params=_cparams(2), name="ssd_bwd_state",
    )(xbc_act, xbc_act, dt, pc, e2)

    fwd = lambda bi, c: (bi, c, 0)
    return pl.pallas_call(
        _ssd_fwd_kernel, grid=(b, nc),
        in_specs=[pl.BlockSpec((1, CHUNK, D_MODEL), fwd),
                  pl.BlockSpec((1, CHUNK, 2 * SSD_STATE), lambda bi, c: (bi, c, D_MODEL // (2 * SSD_STATE))),
                  pl.BlockSpec((1, CHUNK, 2 * SSD_STATE), lambda bi, c: (bi, c, D_MODEL // (2 * SSD_STATE) + 1)),
                  pl.BlockSpec((1, CHUNK, LANE), fwd), pl.BlockSpec((1, CHUNK, D_MODEL), fwd),
                  pl.BlockSpec((1, 1, SSD_STATE, D_MODEL), lambda bi, c: (bi, c, 0, 0)),
                  _const_spec(pc.shape), _const_spec(e3.shape), _const_spec((1, D_MODEL)), _const_spec((1, D_MODEL))],
        out_specs=pl.BlockSpec((1, CHUNK, D_MODEL), fwd),
        out_shape=jax.ShapeDtypeStruct((b, seq, D_MODEL), BF16),
        scratch_shapes=[pltpu.VMEM((SSD_STATE, D_MODEL), F32)],
        compiler_params=_cparams(2), name="ssd_fwd",
    )(xbc_act, xbc_act, xbc_act, dt, z, sb, pc, e3, dexp, out_norm)


def _norm_and_route(x1, fg_ref, wr_hi_ref, wr_lo_ref, hn_ref, aff_ref):
    hn = _rms(x1, fg_ref[...])
    hi, lo = _split2(hn)
    hn_ref[...] = hi
    logits = _dot(hi, wr_hi_ref[...]) + _dot(lo, wr_hi_ref[...]) + _dot(hi, wr_lo_ref[...])
    lt = logits.T[0:N_EXPERTS, :]
    e = jnp.exp(lt - jnp.max(lt, axis=0, keepdims=True))
    aff_ref[0] = e / jnp.sum(e, axis=0, keepdims=True)


def _out_even_kernel(x_ref, a_ref, y_ref, wa_ref, wy_ref, fg_ref, wr_hi_ref, wr_lo_ref, x1_ref, hn_ref, aff_ref):
    x1 = x_ref[...] + _dot(a_ref[...], wa_ref[...]) + _dot(y_ref[...], wy_ref[...])
    x1_ref[...] = x1
    _norm_and_route(x1, fg_ref, wr_hi_ref, wr_lo_ref, hn_ref, aff_ref)


def _out_odd_kernel(x_ref, gb_ref, cu_ref, cp_ref, cn_ref, cw_ref, wo_ref, fg_ref, wr_hi_ref, wr_lo_ref,
                    x1_ref, hn_ref, aff_ref, *, tiles_per_seq):
    i = pl.program_id(0) % tiles_per_seq
    first, last = i == 0, i == tiles_per_seq - 1
    cur, prev8, next8 = cu_ref[...], cp_ref[...], cn_ref[...]
    conv = sum(_shifted(cur, prev8, next8, j - SHORT_CONV // 2, first, last) * cw_ref[j:j + 1, :]
               for j in range(SHORT_CONV))
    x1 = x_ref[...] + _dot((gb_ref[...] * conv).astype(BF16), wo_ref[...])
    x1_ref[...] = x1
    _norm_and_route(x1, fg_ref, wr_hi_ref, wr_lo_ref, hn_ref, aff_ref)


def _route_outs(n, b, seq, tm):
    row = pl.BlockSpec((tm, D_MODEL), lambda i: (i, 0))
    aff = pl.BlockSpec((1, N_EXPERTS, tm), lambda i: (i // (seq // tm), 0, i % (seq // tm)))
    shapes = [jax.ShapeDtypeStruct((n, D_MODEL), F32), jax.ShapeDtypeStruct((n, D_MODEL), BF16),
              jax.ShapeDtypeStruct((b, N_EXPERTS, seq), F32)]
    return [row, row, aff], shapes


def _out_even(x, attn, y, wa, wy, fg, wr_hi, wr_lo, b, seq):
    n = x.shape[0]
    tm = ROW_TILE
    row = pl.BlockSpec((tm, D_MODEL), lambda i: (i, 0))
    out_specs, shapes = _route_outs(n, b, seq, tm)
    return pl.pallas_call(
        _out_even_kernel, grid=(n // tm,),
        in_specs=[row, row, row, _const_spec(wa.shape), _const_spec(wy.shape), _const_spec((1, D_MODEL)),
                  _const_spec(wr_hi.shape), _const_spec(wr_lo.shape)],
        out_specs=out_specs, out_shape=shapes, compiler_params=_cparams(1), name="out_even",
    )(x, attn, y, wa, wy, fg, wr_hi, wr_lo)


def _out_odd(x, gb, cu, cw, wo, fg, wr_hi, wr_lo, b, seq):
    n = x.shape[0]
    tm = ROW_TILE
    row = pl.BlockSpec((tm, D_MODEL), lambda i: (i, 0))
    prev = pl.BlockSpec((8, D_MODEL), lambda i: (jnp.maximum(i * (tm // 8) - 1, 0), 0))
    nxt = pl.BlockSpec((8, D_MODEL), lambda i: (jnp.minimum((i + 1) * (tm // 8), n // 8 - 1), 0))
    out_specs, shapes = _route_outs(n, b, seq, tm)
    return pl.pallas_call(
        functools.partial(_out_odd_kernel, tiles_per_seq=seq // tm), grid=(n // tm,),
        in_specs=[row, row, row, prev, nxt, pl.BlockSpec((8, D_MODEL), lambda i: (0, 0)), _const_spec(wo.shape),
                  _const_spec((1, D_MODEL)), _const_spec(wr_hi.shape), _const_spec(wr_lo.shape)],
        out_specs=out_specs, out_shape=shapes, compiler_params=_cparams(1), name="out_odd",
    )(x, gb, cu, cu, cu, cw, wo, fg, wr_hi, wr_lo)


def _in_odd_kernel(x_ref, g_ref, wb_ref, wc_ref, wu_ref, gb_ref, cu_ref):
    hb = _rms(x_ref[...], g_ref[...]).astype(BF16)
    gb_ref[...] = _dot(hb, wb_ref[...])
    cu_ref[...] = _dot(hb, wc_ref[...]) * _dot(hb, wu_ref[...])


def _in_odd(x, g, wb, wc, wu):
    n = x.shape[0]
    tm = ROW_TILE
    row = pl.BlockSpec((tm, D_MODEL), lambda i: (i, 0))
    return pl.pallas_call(
        _in_odd_kernel, grid=(n // tm,),
        in_specs=[row, _const_spec((1, D_MODEL)), _const_spec(wb.shape), _const_spec(wc.shape), _const_spec(wu.shape)],
        out_specs=[row, row], out_shape=[jax.ShapeDtypeStruct((n, D_MODEL), F32)] * 2,
        compiler_params=_cparams(1), name="in_odd",
    )(x, g, wb, wc, wu)


def _count(mask):
    return jnp.sum(jnp.sum(mask.astype(F32), axis=0, keepdims=True), axis=1, keepdims=True)


def _route_kernel(aff_ref, incl_ref, ones_ref, strict_ref, local_ref, group_ref, first_ref, slot_ref, off_ref, end_ref,
                  *, cap):
    for e in range(N_EXPERTS):
        bits = pltpu.bitcast(aff_ref[0, e], I32)

        def step(i, thr):
            cand = thr | (jnp.int32(1) << (30 - i))
            return jnp.where(_count(bits >= cand) >= cap, cand, thr)

        thr = lax.fori_loop(0, 31, step, jnp.zeros((1, 1), I32))
        gt = bits > thr
        eq = (bits == thr).astype(BF16)
        eq_rank = _dot(eq, incl_ref[...]) + _dot(strict_ref[...], _dot(eq, ones_ref[...]).astype(BF16))
        sel = (gt | ((bits == thr) & (eq_rank <= cap - _count(gt)))).astype(BF16)
        within = _dot(sel, incl_ref[...])
        totals = _dot(sel, ones_ref[...]).astype(BF16)
        local = _dot(local_ref[...], totals)
        cnt = _dot(group_ref[...], totals)
        padded = jnp.floor((cnt + (SLOT_ALIGN - 1)) * (1.0 / SLOT_ALIGN)) * SLOT_ALIGN
        start = _dot(first_ref[...], padded.astype(BF16))
        slot_ref[0, e] = jnp.where(sel > 0, (start + local + within).astype(I32) - 1, -1)
        off_ref[0, e:e + 1, :] = start.T[0:1, :].astype(I32)
        end_ref[0, e:e + 1, :] = (start + padded).T[0:1, :].astype(I32)


def _route(aff, cap):
    b, _, seq = aff.shape
    nt = seq // LANE
    tri = np.arange(LANE)
    tt = np.arange(nt)
    grp = tt // SUB
    as_bf16 = lambda m: jnp.asarray(m, BF16)
    incl = as_bf16(tri[:, None] <= tri[None, :])
    strict = as_bf16(tt[None, :] < tt[:, None])
    local = as_bf16((tt[None, :] < tt[:, None]) & (grp[None, :] == grp[:, None]))
    group = as_bf16(grp[None, :] == grp[:, None])
    first = as_bf16((grp[None, :] < grp[:, None]) & (tt[None, :] % SUB == 0))
    ones = jnp.ones((LANE, LANE), BF16)
    tiles = pl.BlockSpec((1, N_EXPERTS, nt, LANE), lambda i: (i, 0, 0, 0))
    rows = pl.BlockSpec((1, N_EXPERTS, nt), lambda i: (i, 0, 0))
    return pl.pallas_call(
        functools.partial(_route_kernel, cap=cap), grid=(b,),
        in_specs=[tiles, _const_spec((LANE, LANE)), _const_spec((LANE, LANE))] + [_const_spec((nt, nt))] * 4,
        out_specs=[tiles, rows, rows],
        out_shape=[jax.ShapeDtypeStruct((b, N_EXPERTS, nt, LANE), I32)] + [jax.ShapeDtypeStruct((b, N_EXPERTS, nt), I32)] * 2,
        compiler_params=_cparams(1), name="route",
    )(aff.reshape(b, N_EXPERTS, nt, LANE), incl, ones, strict, local, group, first)


def _one_hot(slots, base):
    return (slots == lax.broadcasted_iota(I32, (WINDOW, slots.shape[1]), 0) + base).astype(BF16)


def _gather_kernel(off_ref, end_ref, slot_ref, aff_ref, hn_ref, zero_hbm, xe_hbm, stage, extra, sem, xsem):
    del zero_hbm
    bi, j = pl.program_id(0), pl.program_id(1)
    n_j = pl.num_programs(1)
    step = bi * n_j + j
    cur = step % 2

    def window_copy(bb, jj, e, buf):
        start = pl.multiple_of(off_ref[bb, e, jj * SUB], SLOT_ALIGN)
        return pltpu.make_async_copy(stage.at[buf, e], xe_hbm.at[bb, e, pl.ds(start, WINDOW), :], sem.at[buf, e])

    tokens = hn_ref[0]
    parts = [p.astype(F32) for p in _split3(aff_ref[0, 0])]
    gates = jnp.concatenate(parts + [jnp.zeros((LANE - 3 * N_EXPERTS, TOK_TILE), F32)], axis=0).astype(BF16)

    def rows_of(p):
        return jnp.concatenate([_dot(p, tokens), _dot_nt(p, gates)], axis=1).astype(BF16)

    p_all = jnp.concatenate([_one_hot(slot_ref[0, 0, e:e + 1, :], off_ref[bi, e, j * SUB])
                             for e in range(N_EXPERTS)], axis=0)
    stage[cur] = rows_of(p_all).reshape(N_EXPERTS, WINDOW, D_MODEL + LANE)

    @pl.when(step > 0)
    def _():
        prev = step - 1
        for e in range(N_EXPERTS):
            window_copy(prev // n_j, prev % n_j, e, 1 - cur).wait()

    for e in range(N_EXPERTS):
        window_copy(bi, j, e, cur).start()

    def overflow(e, c):
        first = off_ref[bi, e, j * SUB]
        n_win = (end_ref[bi, e, j * SUB] - first + WINDOW - 1) // WINDOW
        slots = slot_ref[0, 0, pl.ds(e, 1), :]

        def one(w, c2):
            base = pl.multiple_of(first + w * WINDOW, SLOT_ALIGN)
            extra[...] = rows_of(_one_hot(slots, base))
            cp = pltpu.make_async_copy(extra, xe_hbm.at[bi, e, pl.ds(base, WINDOW), :], xsem.at[0])
            cp.start()
            cp.wait()
            return c2

        return lax.fori_loop(1, n_win, one, c)

    lax.fori_loop(0, N_EXPERTS, overflow, 0)

    @pl.when(step == pl.num_programs(0) * n_j - 1)
    def _():
        for e in range(N_EXPERTS):
            window_copy(bi, j, e, cur).wait()


def _gather(off, end, slot_t, aff_t, hn):
    b, seq, _ = hn.shape
    cap_pad = _cap_pad(seq)
    width = D_MODEL + LANE
    per_tile = pl.BlockSpec((1, 1, N_EXPERTS, TOK_TILE), lambda bi, j, *_: (bi, j, 0, 0))
    return pl.pallas_call(
        _gather_kernel,
        grid_spec=pltpu.PrefetchScalarGridSpec(
            num_scalar_prefetch=2, grid=(b, seq // TOK_TILE),
            in_specs=[per_tile, per_tile, pl.BlockSpec((1, TOK_TILE, D_MODEL), lambda bi, j, *_: (bi, j, 0)),
                      pl.BlockSpec(memory_space=pl.ANY)],
            out_specs=pl.BlockSpec(memory_space=pl.ANY),
            scratch_shapes=[pltpu.VMEM((2, N_EXPERTS, WINDOW, width), BF16), pltpu.VMEM((WINDOW, width), BF16),
                            pltpu.SemaphoreType.DMA((2, N_EXPERTS)), pltpu.SemaphoreType.DMA((1,))]),
        out_shape=jax.ShapeDtypeStruct((b, N_EXPERTS, cap_pad, width), BF16),
        input_output_aliases={5: 0},
        compiler_params=_cparams(2), name="moe_gather",
    )(off, end, slot_t, aff_t, hn, jnp.zeros((b, N_EXPERTS, cap_pad, width), BF16))


def _ffn_kernel(end_ref, xe_ref, wg_hbm, wu_hbm, wd_hbm, y_ref, stage_g, stage_u, stage_d, wg, wu, wd, sem, *, layer):
    e, bi, r = pl.program_id(0), pl.program_id(1), pl.program_id(2)
    used = end_ref[bi, e, end_ref.shape[2] - 1]
    pairs = ((wg_hbm, stage_g, wg), (wu_hbm, stage_u, wu), (wd_hbm, stage_d, wd))

    def weight_copies(ee):
        return [pltpu.make_async_copy(src.at[layer, ee], stg, sem.at[k]) for k, (src, stg, _) in enumerate(pairs)]

    @pl.when((bi == 0) & (r == 0))
    def _():
        @pl.when(e == 0)
        def _():
            for cp in weight_copies(e):
                cp.start()

        for cp, (_, stg, dst) in zip(weight_copies(e), pairs):
            cp.wait()
            n_rows = stg.shape[0]

            def cast(i, c, stg=stg, dst=dst):
                rows = pl.ds(pl.multiple_of(i * CAST_ROWS, CAST_ROWS), CAST_ROWS)
                dst[rows, :] = stg[rows, :].astype(BF16)
                return c

            lax.fori_loop(0, n_rows // CAST_ROWS, cast, 0)

        @pl.when(e + 1 < pl.num_programs(0))
        def _():
            for cp in weight_copies(e + 1):
                cp.start()

    @pl.when(r * FFN_ROWS < used)
    def _():
        xe = xe_ref[0, 0, :, 0:D_MODEL]
        hid = (_silu(_dot(xe, wg[...])) * _dot(xe, wu[...])).astype(BF16)
        g = xe_ref[0, 0, :, D_MODEL:].astype(F32)
        lane = lax.broadcasted_iota(I32, g.shape, 1)
        mine = (lane % N_EXPERTS == e) & (lane < 3 * N_EXPERTS)
        gate = jnp.sum(jnp.where(mine, g, 0.0), axis=1, keepdims=True)
        y_ref[0, 0] = (_dot(hid, wd[...]) * gate).astype(BF16)

    @pl.when(r * FFN_ROWS >= used)
    def _():
        y_ref[...] = jnp.zeros_like(y_ref)


def _ffn(end, xe, wg, wu, wd, layer):
    b, ne, cap_pad, width = xe.shape
    rows = lambda w: pl.BlockSpec((1, 1, FFN_ROWS, w), lambda e, bi, r, *_: (bi, e, r, 0))
    hbm = pl.BlockSpec(memory_space=pl.ANY)
    return pl.pallas_call(
        functools.partial(_ffn_kernel, layer=layer),
        grid_spec=pltpu.PrefetchScalarGridSpec(
            num_scalar_prefetch=1, grid=(ne, b, cap_pad // FFN_ROWS),
            in_specs=[rows(width), hbm, hbm, hbm],
            out_specs=rows(D_MODEL),
            scratch_shapes=[pltpu.VMEM((D_MODEL, D_FF), F32), pltpu.VMEM((D_MODEL, D_FF), F32),
                            pltpu.VMEM((D_FF, D_MODEL), F32), pltpu.VMEM((D_MODEL, D_FF), BF16),
                            pltpu.VMEM((D_MODEL, D_FF), BF16), pltpu.VMEM((D_FF, D_MODEL), BF16),
                            pltpu.SemaphoreType.DMA((3,))]),
        out_shape=jax.ShapeDtypeStruct((b, ne, cap_pad, D_MODEL), BF16),
        compiler_params=_cparams(3), name="moe_ffn",
    )(end, xe, wg, wu, wd)


def _combine_kernel(off_ref, end_ref, slot_ref, y_hbm, x1_ref, o_ref, win, extra, sem, xsem):
    bi, j = pl.program_id(0), pl.program_id(1)
    n_j = pl.num_programs(1)
    step = bi * n_j + j
    cur = step % 2

    def window_copy(bb, jj, e, buf):
        start = pl.multiple_of(off_ref[bb, e, jj * SUB], SLOT_ALIGN)
        return pltpu.make_async_copy(y_hbm.at[bb, e, pl.ds(start, WINDOW), :], win.at[buf, e], sem.at[buf, e])

    @pl.when(step == 0)
    def _():
        for e in range(N_EXPERTS):
            window_copy(bi, j, e, cur).start()

    @pl.when(step + 1 < pl.num_programs(0) * n_j)
    def _():
        nxt = step + 1
        for e in range(N_EXPERTS):
            window_copy(nxt // n_j, nxt % n_j, e, 1 - cur).start()

    ps = []
    for e in range(N_EXPERTS):
        window_copy(bi, j, e, cur).wait()
        ps.append(_one_hot(slot_ref[0, 0, e:e + 1, :], off_ref[bi, e, j * SUB]))
    p_all = jnp.concatenate(ps, axis=0)
    y_all = win[cur].reshape(N_EXPERTS * WINDOW, D_MODEL)
    o_ref[0] = x1_ref[0] + _dot_tn(p_all, y_all)

    def overflow(e, c):
        first = off_ref[bi, e, j * SUB]
        n_win = (end_ref[bi, e, j * SUB] - first + WINDOW - 1) // WINDOW
        slots = slot_ref[0, 0, pl.ds(e, 1), :]

        def one(w, c2):
            base = pl.multiple_of(first + w * WINDOW, SLOT_ALIGN)
            cp = pltpu.make_async_copy(y_hbm.at[bi, e, pl.ds(base, WINDOW), :], extra, xsem.at[0])
            cp.start()
            cp.wait()
            o_ref[0] += _dot_tn(_one_hot(slots, base), extra[...])
            return c2

        return lax.fori_loop(1, n_win, one, c)

    lax.fori_loop(0, N_EXPERTS, overflow, 0)


def _combine(off, end, slot_t, y, x1):
    b, seq, _ = x1.shape
    tile = pl.BlockSpec((1, TOK_TILE, D_MODEL), lambda bi, j, *_: (bi, j, 0))
    return pl.pallas_call(
        _combine_kernel,
        grid_spec=pltpu.PrefetchScalarGridSpec(
            num_scalar_prefetch=2, grid=(b, seq // TOK_TILE),
            in_specs=[pl.BlockSpec((1, 1, N_EXPERTS, TOK_TILE), lambda bi, j, *_: (bi, j, 0, 0)),
                      pl.BlockSpec(memory_space=pl.ANY), tile],
            out_specs=tile,
            scratch_shapes=[pltpu.VMEM((2, N_EXPERTS, WINDOW, D_MODEL), BF16), pltpu.VMEM((WINDOW, D_MODEL), BF16),
                            pltpu.SemaphoreType.DMA((2, N_EXPERTS)), pltpu.SemaphoreType.DMA((1,))]),
        out_shape=jax.ShapeDtypeStruct((b, seq, D_MODEL), F32),
        compiler_params=_cparams(2), name="moe_combine",
    )(off, end, slot_t, y, x1)


def _cap_pad(seq):
    cap = CAPACITY_FACTOR * seq // N_EXPERTS
    worst = cap + (seq // TOK_TILE) * (SLOT_ALIGN - 1) + WINDOW
    return -(-worst // FFN_ROWS) * FFN_ROWS


def _moe(x1, hn, aff, wg, wu, wd, layer):
    b, seq, _ = x1.shape
    cap = CAPACITY_FACTOR * seq // N_EXPERTS
    slot, off, end = _route(aff, cap)
    per_tile = lambda a: jnp.swapaxes(a.reshape(b, N_EXPERTS, seq // TOK_TILE, TOK_TILE), 1, 2)
    slot_t = per_tile(slot)
    xe = _gather(off, end, slot_t, per_tile(aff), hn)
    y = _ffn(end, xe, wg, wu, wd, layer)
    return _combine(off, end, slot_t, y, x1)


def _rope_tables(seq):
    inv_freq = ROPE_THETA ** (-jnp.arange(ROPE_HALF, dtype=F32) * 2.0 / (2 * ROPE_HALF))
    ang = jnp.arange(seq, dtype=F32)[:, None] * inv_freq[None, :]
    cos, sin = jnp.cos(ang), jnp.sin(ang)
    z = lambda w: jnp.zeros((seq, w), F32)
    rest = HEAD_DIM - 2 * ROPE_HALF
    rc = jnp.concatenate([cos, cos, jnp.ones((seq, rest), F32)], axis=1)
    r1 = jnp.concatenate([-sin, z(ROPE_HALF + rest)], axis=1)
    r2 = jnp.concatenate([z(ROPE_HALF), sin, z(rest)], axis=1)
    return tuple(jnp.tile(t, (1, LANE // HEAD_DIM)) for t in (rc, r1, r2))


def _router_split(w):
    wp = jnp.pad(w, ((0, 0), (0, LANE - N_EXPERTS)))
    hi = wp.astype(BF16)
    return hi, (wp - hi.astype(F32)).astype(BF16)


def kernel(x, attn_norm, w_in_even, q_norm, k_norm, ssd_conv_w, ssd_conv_b, ssd_a_log_fwd, ssd_a_log_bwd,
           ssd_dt_bias_fwd, ssd_dt_bias_bwd, ssd_d, ssd_out_norm, w_out_even, conv_norm, conv_w_in, conv_w,
           conv_w_out, ffn_norm, router_w, expert_w_gate, expert_w_up, expert_w_down):
    b, seq, _ = x.shape
    n = b * seq
    depth = ffn_norm.shape[0]
    rc, r1, r2 = _rope_tables(seq)
    blk = np.arange(256) // HEAD_DIM
    bd = jnp.asarray((blk[:, None] == blk[None, :]) / HEAD_DIM, BF16)
    row = lambda v: v.reshape(1, -1).astype(F32)

    xf = x.reshape(n, D_MODEL)
    for layer in range(depth):
        i = layer // 2
        wr_hi, wr_lo = _router_split(router_w[layer])
        fg = row(ffn_norm[layer])
        if layer % 2 == 0:
            w = w_in_even[i].astype(BF16)
            o = np.cumsum([0, D_MODEL, D_MODEL, D_MODEL, D_MODEL, SSD_XBC, N_HEADS, N_HEADS])
            wq, wk, wv, wz, wx = (w[:, o[j]:o[j + 1]] for j in range(5))
            wd = jnp.pad(w[:, o[5]:o[7]], ((0, 0), (0, LANE - 2 * N_HEADS)))
            tile_heads = lambda g: row(jnp.tile(g, N_HEADS))
            q, k, v, z, xbc, dt = _in_even(xf, row(attn_norm[i]), wq, wk, wv, wz, wx, wd,
                                           tile_heads(q_norm[i]), tile_heads(k_norm[i]), bd, rc, r1, r2, seq)
            as3 = lambda t: t.reshape(b, seq, -1)
            attn = _attention(as3(q), as3(k), as3(v))
            cw = jnp.pad(ssd_conv_w[i], ((0, 8 - SSD_CONV), (0, 0)))
            act = _ssd_conv(as3(xbc), cw, row(ssd_conv_b[i]))
            pc = jnp.pad(jnp.stack([ssd_dt_bias_fwd[i], ssd_dt_bias_bwd[i], -jnp.exp(ssd_a_log_fwd[i]),
                                    -jnp.exp(ssd_a_log_bwd[i])], axis=1).astype(F32), ((0, 0), (0, LANE - 4)))
            y = _ssd(act, as3(dt), as3(z), pc, row(jnp.repeat(ssd_d[i], HEAD_DIM)), row(ssd_out_norm[i]))
            wo = w_out_even[i].astype(BF16)
            x1, hn, aff = _out_even(xf, attn.reshape(n, D_MODEL), y.reshape(n, D_MODEL), wo[:D_MODEL], wo[D_MODEL:],
                                    fg, wr_hi, wr_lo, b, seq)
        else:
            w = conv_w_in[i].astype(BF16)
            gb, cu = _in_odd(xf, row(conv_norm[i]), w[:, :D_MODEL], w[:, D_MODEL:2 * D_MODEL], w[:, 2 * D_MODEL:])
            cw = jnp.pad(conv_w[i], ((0, 8 - SHORT_CONV), (0, 0)))
            x1, hn, aff = _out_odd(xf, gb, cu, cw, conv_w_out[i].astype(BF16), fg, wr_hi, wr_lo, b, seq)
        xf = _moe(x1.reshape(b, seq, D_MODEL), hn.reshape(b, seq, D_MODEL), aff, expert_w_gate, expert_w_up,
                  expert_w_down, layer).reshape(n, D_MODEL)
    return xf.reshape(b, seq, D_MODEL)
```

```python
import functools
import math

import jax
import jax.numpy as jnp
import numpy as np
from jax import lax
from jax.experimental import pallas as pl
from jax.experimental.pallas import tpu as pltpu

F32, BF16, I32 = jnp.float32, jnp.bfloat16, jnp.int32

D_MODEL = 1024
N_HEADS = 16
HEAD_DIM = 64
ROPE_HALF = 8
ROPE_THETA = 500000.0
PATTERNS = ((128, 1), (512, 4), (2048, 16))
HALF_STEPS = 64
SSD_GROUPS = 2
SSD_STATE = 128
SSD_XBC = 1536
SSD_CONV = 5
CHUNK = 128
N_EXPERTS = 16
CAPACITY_FACTOR = 2
D_FF = 2048
SHORT_CONV = 3
EPS = 1e-6

LANE = 128
VMEM_LIMIT = 56 * 1024 * 1024

ROW_TILE = 256
ATTN_SUPER = 2048
ATTN_HALO = 1024
ATTN_TQ = 128
ATTN_TK = ATTN_TQ + 2 * HALF_STEPS
ATTN_UNROLL = 8
TOK_TILE = 512
SUB = TOK_TILE // LANE
SLOT_ALIGN = 16
WINDOW = 96
FFN_ROWS = 256
CAST_ROWS = 64


def _cparams(n_axes):
    return pltpu.CompilerParams(dimension_semantics=("arbitrary",) * n_axes, vmem_limit_bytes=VMEM_LIMIT)


def _const_spec(shape):
    nd = len(shape)
    return pl.BlockSpec(shape, lambda *_: (0,) * nd, pipeline_mode=pl.Buffered(1))


def _dot(a, b):
    return jnp.dot(a, b, preferred_element_type=F32)


def _dot_nt(a, b):
    return lax.dot_general(a, b, (((1,), (1,)), ((), ())), preferred_element_type=F32)


def _dot_tn(a, b):
    return lax.dot_general(a, b, (((0,), (0,)), ((), ())), preferred_element_type=F32)


def _split2(x):
    hi = x.astype(BF16)
    lo = (x - hi.astype(F32)).astype(BF16)
    return hi, lo


def _split3(x):
    hi = x.astype(BF16)
    r = x - hi.astype(F32)
    mid = r.astype(BF16)
    lo = (r - mid.astype(F32)).astype(BF16)
    return hi, mid, lo


def _dot3(x, m_bf16):
    hi, mid, lo = _split3(x)
    return _dot(hi, m_bf16) + _dot(mid, m_bf16) + _dot(lo, m_bf16)


def _rms(x, g):
    return x * lax.rsqrt(jnp.mean(x * x, axis=-1, keepdims=True) + EPS) * g


def _silu(x):
    return x * jax.nn.sigmoid(x)


def _in_even_kernel(x_ref, g_ref, wq_ref, wk_ref, wv_ref, wz_ref, wx_ref, wd_ref, qg_ref, kg_ref, bd_ref,
                    rc_ref, r1_ref, r2_ref, q_ref, k_ref, v_ref, z_ref, xbc_ref, dt_ref):
    hb = _rms(x_ref[...], g_ref[...]).astype(BF16)
    bd = bd_ref[...]
    rc, r1, r2 = rc_ref[...], r1_ref[...], r2_ref[...]

    def head_norm_rope(w_ref, gain_ref, out_ref):
        t = _dot(hb, w_ref[...])
        for c in range(D_MODEL // 256):
            tc = t[:, c * 256:(c + 1) * 256]
            sq_hi, sq_lo = _split2(tc * tc)
            ms = _dot(sq_hi, bd) + _dot(sq_lo, bd)
            tn = tc * lax.rsqrt(ms + EPS) * gain_ref[:, c * 256:(c + 1) * 256]
            for hh in range(2):
                u = tn[:, hh * LANE:(hh + 1) * LANE]
                r = u * rc + pltpu.roll(u, LANE - ROPE_HALF, 1) * r1 + pltpu.roll(u, ROPE_HALF, 1) * r2
                out_ref[:, c * 256 + hh * LANE:c * 256 + (hh + 1) * LANE] = r

    head_norm_rope(wq_ref, qg_ref, q_ref)
    head_norm_rope(wk_ref, kg_ref, k_ref)
    v_ref[...] = _dot(hb, wv_ref[...])
    z_ref[...] = _dot(hb, wz_ref[...])
    xbc_ref[...] = _dot(hb, wx_ref[...])
    dt_ref[...] = _dot(hb, wd_ref[...])


def _in_even(x, g, wq, wk, wv, wz, wx, wd, qg, kg, bd, rc, r1, r2, seq):
    n = x.shape[0]
    tm = ROW_TILE
    row = lambda w: pl.BlockSpec((tm, w), lambda i: (i, 0))
    tab = pl.BlockSpec((tm, LANE), lambda i: (i % (seq // tm), 0))
    outs = [jax.ShapeDtypeStruct((n, D_MODEL), F32)] * 4 + [jax.ShapeDtypeStruct((n, SSD_XBC), F32),
                                                           jax.ShapeDtypeStruct((n, LANE), F32)]
    return pl.pallas_call(
        _in_even_kernel, grid=(n // tm,),
        in_specs=[row(D_MODEL), _const_spec((1, D_MODEL)), _const_spec(wq.shape), _const_spec(wk.shape),
                  _const_spec(wv.shape), _const_spec(wz.shape), _const_spec(wx.shape), _const_spec(wd.shape),
                  _const_spec((1, D_MODEL)), _const_spec((1, D_MODEL)), _const_spec((256, 256)), tab, tab, tab],
        out_specs=[row(D_MODEL)] * 4 + [row(SSD_XBC), row(LANE)], out_shape=outs,
        compiler_params=_cparams(1), name="in_even",
    )(x, g, wq, wk, wv, wz, wx, wd, qg, kg, bd, rc, r1, r2)


def _attn_kernel(q_ref, k0, k1, k2, k3, v0, v1, v2, v3, qw_hbm, kw_hbm, vw_hbm, o_ref, qf, kf, vf, q16, k16, v16,
                 acc, mst, lst, a16, m16, l16, an3, mn3, ln3, sem, *, seq):
    dmax = PATTERNS[-1][1]
    bi, hp, j = pl.program_id(0), pl.program_id(1), pl.program_id(2)
    p0 = j * ATTN_SUPER
    halo = ATTN_HALO // dmax

    def residue_copies(r, where):
        lanes = pl.ds(pl.multiple_of(hp * LANE, LANE), LANE)
        q0 = pl.multiple_of(j * ATTN_TQ, ATTN_TQ)
        n_kv = ATTN_TK - halo if where else ATTN_TK
        src0 = 0 if where < 0 else q0 - halo
        dst0 = halo if where < 0 else 0
        cps = [pltpu.make_async_copy(qw_hbm.at[bi, pl.ds(q0, ATTN_TQ), r, lanes], q16.at[r], sem.at[0, r])]
        for n, (src, dst) in enumerate(((kw_hbm, k16), (vw_hbm, v16))):
            cps.append(pltpu.make_async_copy(src.at[bi, pl.ds(src0, n_kv), r, lanes],
                                             dst.at[r, pl.ds(dst0, n_kv), :], sem.at[1 + n, r]))
        return cps

    def per_position(fn):
        last = pl.num_programs(2) - 1
        pl.when(j == 0)(functools.partial(fn, -1))
        pl.when((j > 0) & (j < last))(functools.partial(fn, 0))
        pl.when(j == last)(functools.partial(fn, 1))

    def state_copies(r):
        return [pltpu.make_async_copy(src.at[r], dst.at[:, r, :], sem.at[3 + n, r])
                for n, (src, dst) in enumerate(((a16, an3), (m16, mn3), (l16, ln3)))]

    @pl.when((bi == 0) & (hp == 0) & (j == 0))
    def _():
        k16[...] = jnp.zeros_like(k16)
        v16[...] = jnp.zeros_like(v16)

    def start_residues(where):
        for r in range(dmax):
            for cp in residue_copies(r, where):
                cp.start()

    def wait_residues(where):
        for r in range(dmax):
            for cp in residue_copies(r, where):
                cp.wait()

    per_position(start_residues)

    qf[...] = q_ref[0] * (HEAD_DIM ** -0.5)
    for i, (kr, vr) in enumerate(((k0, v0), (k1, v1), (k2, v2), (k3, v3))):
        kf[i * ATTN_HALO:(i + 1) * ATTN_HALO, :] = kr[0]
        vf[i * ATTN_HALO:(i + 1) * ATTN_HALO, :] = vr[0]

    head_a = lax.broadcasted_iota(I32, (ATTN_TQ, LANE), 1) < HEAD_DIM
    off = lax.broadcasted_iota(I32, (ATTN_TQ, ATTN_TK), 1) - lax.broadcasted_iota(I32, (ATTN_TQ, ATTN_TK), 0)
    band = (off >= 0) & (off <= 2 * HALF_STEPS)
    band2 = jnp.concatenate([band, band], axis=0)
    colpos = lax.broadcasted_iota(I32, (1, ATTN_TK), 1)

    def local_softmax(q, kt, vt, first_pos, d):
        pos = first_pos + d * colpos
        valid = band2 & (pos >= 0) & (pos < seq)
        q2 = jnp.concatenate([jnp.where(head_a, q, 0.0), jnp.where(head_a, 0.0, q)], axis=0).astype(BF16)
        s = jnp.where(valid, _dot_nt(q2, kt.astype(BF16)), -jnp.inf)
        m = jnp.max(s, axis=1, keepdims=True)
        p = jnp.exp(s - m)
        l = jnp.sum(p, axis=1, keepdims=True)
        n = _dot(p.astype(BF16), vt.astype(BF16))
        return (jnp.where(head_a, m[:ATTN_TQ], m[ATTN_TQ:]), jnp.where(head_a, l[:ATTN_TQ], l[ATTN_TQ:]),
                jnp.where(head_a, n[:ATTN_TQ], n[ATTN_TQ:]))

    def visit(d, i, first):
        log_d = d.bit_length() - 1
        r = i & (d - 1)
        t = i >> log_d
        qs = r + d * ATTN_TQ * t
        ks = ATTN_HALO + r + d * (ATTN_TQ * t - HALF_STEPS)
        m_loc, l_loc, n_loc = local_softmax(qf[pl.ds(qs, ATTN_TQ, stride=d), :], kf[pl.ds(ks, ATTN_TK, stride=d), :],
                                            vf[pl.ds(ks, ATTN_TK, stride=d), :], p0 - ATTN_HALO + ks, d)
        rows = pl.ds(qs, ATTN_TQ, stride=d)
        if first:
            mst[rows, :], lst[rows, :], acc[rows, :] = m_loc, l_loc, n_loc
        else:
            m_old = mst[rows, :]
            m_new = jnp.maximum(m_old, m_loc)
            w_old = jnp.exp(m_old - m_new)
            w_loc = jnp.exp(m_loc - m_new)
            mst[rows, :] = m_new
            lst[rows, :] = lst[rows, :] * w_old + l_loc * w_loc
            acc[rows, :] = acc[rows, :] * w_old + n_loc * w_loc

    def visit_dmax(r):
        q = q16[r] * (HEAD_DIM ** -0.5)
        m16[r], l16[r], a16[r] = local_softmax(q, k16[r], v16[r], p0 - ATTN_HALO + r, dmax)

    def loop(n, fn):
        def body(i, c):
            for u in range(ATTN_UNROLL):
                fn(i * ATTN_UNROLL + u)
            return c
        lax.fori_loop(0, n // ATTN_UNROLL, body, 0)

    n_visits = ATTN_SUPER // ATTN_TQ
    loop(n_visits, lambda i: visit(PATTERNS[0][1], i, True))
    per_position(wait_residues)
    loop(dmax, visit_dmax)
    for r in range(dmax):
        for cp in state_copies(r):
            cp.start()
    for _, d in PATTERNS[1:-1]:
        loop(n_visits, lambda i, d=d: visit(d, i, False))
    for r in range(dmax):
        for cp in state_copies(r):
            cp.wait()

    def finish(i, c):
        rows = pl.ds(pl.multiple_of(i * ATTN_TQ, ATTN_TQ), ATTN_TQ)
        slabs = pl.ds(pl.multiple_of(i * (ATTN_TQ // dmax), ATTN_TQ // dmax), ATTN_TQ // dmax)
        a_a, m_a, l_a = (t[slabs].reshape(ATTN_TQ, LANE) for t in (an3, mn3, ln3))
        m_b = mst[rows, :]
        m = jnp.maximum(m_a, m_b)
        w_a, w_b = jnp.exp(m_a - m), jnp.exp(m_b - m)
        o = (a_a * w_a + acc[rows, :] * w_b) / (l_a * w_a + lst[rows, :] * w_b)
        o_ref[0, rows, :] = o.astype(BF16)
        return c

    lax.fori_loop(0, n_visits, finish, 0)


def _attention(q, k, v):
    b, seq, _ = q.shape
    nblk = seq // ATTN_HALO
    ratio = ATTN_SUPER // ATTN_HALO
    dmax = PATTERNS[-1][1]
    assert ATTN_SUPER // dmax == ATTN_TQ and 4 * ATTN_HALO // dmax == ATTN_TK
    assert seq // ATTN_SUPER >= 2

    def halo(i):
        return pl.BlockSpec((1, ATTN_HALO, LANE),
                            lambda bi, hp, j: (bi, jnp.clip(ratio * j - 1 + i, 0, nblk - 1), hp))

    main = pl.BlockSpec((1, ATTN_SUPER, LANE), lambda bi, hp, j: (bi, j, hp))
    hbm = pl.BlockSpec(memory_space=pl.ANY)
    by_residue = lambda t: t.reshape(b, seq // dmax, dmax, D_MODEL)
    return pl.pallas_call(
        functools.partial(_attn_kernel, seq=seq),
        grid=(b, D_MODEL // LANE, seq // ATTN_SUPER),
        in_specs=[main] + [halo(i) for i in range(4)] * 2 + [hbm] * 3,
        out_specs=main,
        out_shape=jax.ShapeDtypeStruct((b, seq, D_MODEL), BF16),
        scratch_shapes=[pltpu.VMEM((ATTN_SUPER, LANE), F32), pltpu.VMEM((4 * ATTN_HALO, LANE), F32),
                        pltpu.VMEM((4 * ATTN_HALO, LANE), F32), pltpu.VMEM((dmax, ATTN_TQ, LANE), F32),
                        pltpu.VMEM((dmax, ATTN_TK, LANE), F32), pltpu.VMEM((dmax, ATTN_TK, LANE), F32)]
                       + [pltpu.VMEM((ATTN_SUPER, LANE), F32)] * 3
                       + [pltpu.VMEM((dmax, ATTN_TQ, LANE), F32)] * 3
                       + [pltpu.VMEM((ATTN_TQ, dmax, LANE), F32)] * 3
                       + [pltpu.SemaphoreType.DMA((6, dmax))],
        compiler_params=_cparams(3), name="dilated_attn",
    )(q, k, k, k, k, v, v, v, v, by_residue(q), by_residue(k), by_residue(v))


def _shifted(cur, prev8, next8, s, first, last):
    n = cur.shape[0]
    if s == 0:
        return cur
    rows = lax.broadcasted_iota(I32, cur.shape, 0)
    out = pltpu.roll(cur, (-s) % n, 0)
    if s < 0:
        for j in range(-s):
            src = jnp.where(first, 0.0, prev8[8 + s + j:8 + s + j + 1, :])
            out = jnp.where(rows == j, src, out)
    else:
        for j in range(s):
            src = jnp.where(last, 0.0, next8[j:j + 1, :])
            out = jnp.where(rows == n - s + j, src, out)
    return out


def _ssd_conv_kernel(c_ref, p_ref, n_ref, w_ref, b_ref, o_ref):
    first = pl.program_id(1) == 0
    last = pl.program_id(1) == pl.num_programs(1) - 1
    for c in range(SSD_XBC // 256):
        sl = slice(c * 256, (c + 1) * 256)
        cur, prev8, next8 = c_ref[0, :, sl], p_ref[0, :, sl], n_ref[0, :, sl]
        y = b_ref[:, sl]
        for j in range(SSD_CONV):
            y = y + _shifted(cur, prev8, next8, j - SSD_CONV // 2, first, last) * w_ref[j:j + 1, sl]
        o_ref[0, :, sl] = _silu(y)


def _halo_specs(tm, width, seq):
    cur = pl.BlockSpec((1, tm, width), lambda b, i: (b, i, 0))
    prev = pl.BlockSpec((1, 8, width), lambda b, i: (b, jnp.maximum(i * (tm // 8) - 1, 0), 0))
    nxt = pl.BlockSpec((1, 8, width), lambda b, i: (b, jnp.minimum((i + 1) * (tm // 8), seq // 8 - 1), 0))
    return cur, prev, nxt


def _ssd_conv(xbc, w, bias):
    b, seq, width = xbc.shape
    tm = 512
    cur, prev, nxt = _halo_specs(tm, width, seq)
    return pl.pallas_call(
        _ssd_conv_kernel, grid=(b, seq // tm),
        in_specs=[cur, prev, nxt, pl.BlockSpec((8, width), lambda b, i: (0, 0)),
                  pl.BlockSpec((1, width), lambda b, i: (0, 0))],
        out_specs=cur, out_shape=jax.ShapeDtypeStruct(xbc.shape, F32),
        compiler_params=_cparams(2), name="ssd_conv",
    )(xbc, xbc, xbc, w, bias)


def _softplus(x):
    return jnp.maximum(x, 0.0) + jnp.log1p(jnp.exp(-jnp.abs(x)))


def _tri(kind):
    s = lax.broadcasted_iota(I32, (CHUNK, CHUNK), 0)
    l = lax.broadcasted_iota(I32, (CHUNK, CHUNK), 1)
    return {"le": s <= l, "ge": s >= l, "lt": s < l}[kind]


def _expand(cols, e2_ref):
    hi, lo = _split2(cols)
    return _dot(jnp.concatenate([hi, lo], axis=1), e2_ref[...])


def _ssd_bwd_kernel(xs_ref, b_ref, dt_ref, pc_ref, e2_ref, sb_ref, st):
    @pl.when(pl.program_id(1) == 0)
    def _():
        st[...] = jnp.zeros_like(st)

    sb_ref[0, 0] = st[...].astype(BF16)
    dt_t = dt_ref[0].T
    dtb = _softplus(dt_t[N_HEADS:2 * N_HEADS, :] + pc_ref[:, 1:2])
    a = dtb * pc_ref[:, 3:4]
    ex = _dot3(a, _tri("lt").astype(BF16))
    tot = ex[:, CHUNK - 1:CHUNK] + a[:, CHUNK - 1:CHUNK]
    rowform = jnp.concatenate([dtb * jnp.exp(ex), jnp.broadcast_to(jnp.exp(tot), (N_HEADS, CHUNK)),
                               jnp.zeros((CHUNK - 2 * N_HEADS, CHUNK), F32)], axis=0)
    ex2 = _expand(rowform.T, e2_ref)
    xw = (xs_ref[0] * ex2[:, :D_MODEL]).astype(BF16)
    half = D_MODEL // SSD_GROUPS
    upd = [_dot(b_ref[0, :, g * SSD_STATE:(g + 1) * SSD_STATE].T.astype(BF16), xw[:, g * half:(g + 1) * half])
           for g in range(SSD_GROUPS)]
    st[...] = st[...] * ex2[0:1, D_MODEL:] + jnp.concatenate(upd, axis=1)


def _ssd_fwd_kernel(xs_ref, b_ref, c_ref, dt_ref, z_ref, sb_ref, pc_ref, e3_ref, dexp_ref, on_ref, o_ref, st):
    @pl.when(pl.program_id(1) == 0)
    def _():
        st[...] = jnp.zeros_like(st)

    xs = xs_ref[0]
    dt_t = dt_ref[0].T
    dtf = _softplus(dt_t[0:N_HEADS, :] + pc_ref[:, 0:1])
    dtb = _softplus(dt_t[N_HEADS:2 * N_HEADS, :] + pc_ref[:, 1:2])
    af = dtf * pc_ref[:, 2:3]
    ab = dtb * pc_ref[:, 3:4]
    csf = _dot3(af, _tri("le").astype(BF16))
    rcs = _dot3(ab, _tri("ge").astype(BF16))
    totf = csf[:, CHUNK - 1:CHUNK]
    rowform = jnp.concatenate([dtf * jnp.exp(totf - csf), jnp.exp(csf), jnp.exp(rcs), csf, rcs,
                               jnp.zeros((CHUNK - 5 * N_HEADS, CHUNK), F32)], axis=0)
    cols = rowform.T
    ex3 = _expand(cols, e3_ref)
    w_state, e_f, e_b = ex3[:, :D_MODEL], ex3[:, D_MODEL:2 * D_MODEL], ex3[:, 2 * D_MODEL:]

    xb = xs.astype(BF16)
    lower, upper = _tri("ge"), _tri("le")
    head_a = lax.broadcasted_iota(I32, (CHUNK, LANE), 1) < HEAD_DIM
    half = D_MODEL // SSD_GROUPS
    hpg = N_HEADS // SSD_GROUPS
    st_all = st[...]
    sb_all = sb_ref[0, 0]
    ys = []
    b_t = []
    for g in range(SSD_GROUPS):
        bg = b_ref[0, :, g * SSD_STATE:(g + 1) * SSD_STATE]
        cg = c_ref[0, :, g * SSD_STATE:(g + 1) * SSD_STATE].astype(BF16)
        b_t.append(bg.T.astype(BF16))
        gm = _dot_nt(cg, bg.astype(BF16))
        states = jnp.concatenate([st_all[:, g * half:(g + 1) * half].astype(BF16),
                                  sb_all[:, g * half:(g + 1) * half]], axis=1)
        off = _dot(cg, states)
        y_off = (off[:, :half] * e_f[:, g * half:(g + 1) * half]
                 + off[:, half:] * e_b[:, g * half:(g + 1) * half])
        for pair in range(hpg // 2):
            ms = []
            for h in (g * hpg + 2 * pair, g * hpg + 2 * pair + 1):
                dec_f = jnp.where(lower, jnp.exp(cols[:, 3 * N_HEADS + h:3 * N_HEADS + h + 1] - csf[h:h + 1, :]), 0.0)
                dec_b = jnp.where(upper, jnp.exp(cols[:, 4 * N_HEADS + h:4 * N_HEADS + h + 1] - rcs[h:h + 1, :]), 0.0)
                ms.append((gm * (dec_f * dtf[h:h + 1, :] + dec_b * dtb[h:h + 1, :])).astype(BF16))
            lo = g * half + pair * LANE
            xp = xb[:, lo:lo + LANE]
            ys.append(jnp.where(head_a, _dot(ms[0], xp), _dot(ms[1], xp))
                      + y_off[:, pair * LANE:(pair + 1) * LANE])
    y = jnp.concatenate(ys, axis=1) + xs * dexp_ref[...]
    yz = y * _silu(z_ref[0])
    o_ref[0] = _rms(yz, on_ref[...]).astype(BF16)

    xw = (xs * w_state).astype(BF16)
    upd = [_dot(b_t[g], xw[:, g * half:(g + 1) * half]) for g in range(SSD_GROUPS)]
    st[...] = st_all * e_f[CHUNK - 1:CHUNK, :] + jnp.concatenate(upd, axis=1)


def _ssd(xbc_act, dt, z, pc, dexp, out_norm):
    b, seq, _ = xbc_act.shape
    nc = seq // CHUNK
    sel = np.zeros((2 * CHUNK, 3 * D_MODEL), np.float32)
    for part in range(3):
        for h in range(N_HEADS):
            for rep in range(2):
                sel[rep * CHUNK + part * N_HEADS + h, part * D_MODEL + h * HEAD_DIM:part * D_MODEL + (h + 1) * HEAD_DIM] = 1.0
    e3 = jnp.asarray(sel, BF16)
    e2 = jnp.asarray(sel[:, :2 * D_MODEL], BF16)

    rev = lambda bi, c: (bi, nc - 1 - c, 0)
    sb = pl.pallas_call(
        _ssd_bwd_kernel, grid=(b, nc),
        in_specs=[pl.BlockSpec((1, CHUNK, D_MODEL), rev),
                  pl.BlockSpec((1, CHUNK, 2 * SSD_STATE), lambda bi, c: (bi, nc - 1 - c, D_MODEL // (2 * SSD_STATE))),
                  pl.BlockSpec((1, CHUNK, LANE), rev), _const_spec(pc.shape), _const_spec(e2.shape)],
        out_specs=pl.BlockSpec((1, 1, SSD_STATE, D_MODEL), lambda bi, c: (bi, nc - 1 - c, 0, 0)),
        out_shape=jax.ShapeDtypeStruct((b, nc, SSD_STATE, D_MODEL), BF16),
        scratch_shapes=[pltpu.VMEM((SSD_STATE, D_MODEL), F32)],
        compiler_params=_cparams(2), name="ssd_bwd_state",
    )(xbc_act, xbc_act, dt, pc, e2)

    fwd = lambda bi, c: (bi, c, 0)
    return pl.pallas_call(
        _ssd_fwd_kernel, grid=(b, nc),
        in_specs=[pl.BlockSpec((1, CHUNK, D_MODEL), fwd),
                  pl.BlockSpec((1, CHUNK, 2 * SSD_STATE), lambda bi, c: (bi, c, D_MODEL // (2 * SSD_STATE))),
                  pl.BlockSpec((1, CHUNK, 2 * SSD_STATE), lambda bi, c: (bi, c, D_MODEL // (2 * SSD_STATE) + 1)),
                  pl.BlockSpec((1, CHUNK, LANE), fwd), pl.BlockSpec((1, CHUNK, D_MODEL), fwd),
                  pl.BlockSpec((1, 1, SSD_STATE, D_MODEL), lambda bi, c: (bi, c, 0, 0)),
                  _const_spec(pc.shape), _const_spec(e3.shape), _const_spec((1, D_MODEL)), _const_spec((1, D_MODEL))],
        out_specs=pl.BlockSpec((1, CHUNK, D_MODEL), fwd),
        out_shape=jax.ShapeDtypeStruct((b, seq, D_MODEL), BF16),
        scratch_shapes=[pltpu.VMEM((SSD_STATE, D_MODEL), F32)],
        compiler_params=_cparams(2), name="ssd_fwd",
    )(xbc_act, xbc_act, xbc_act, dt, z, sb, pc, e3, dexp, out_norm)


def _norm_and_route(x1, fg_ref, wr_hi_ref, wr_lo_ref, hn_ref, aff_ref):
    hn = _rms(x1, fg_ref[...])
    hi, lo = _split2(hn)
    hn_ref[...] = hi
    logits = _dot(hi, wr_hi_ref[...]) + _dot(lo, wr_hi_ref[...]) + _dot(hi, wr_lo_ref[...])
    lt = logits.T[0:N_EXPERTS, :]
    e = jnp.exp(lt - jnp.max(lt, axis=0, keepdims=True))
    aff_ref[0] = e / jnp.sum(e, axis=0, keepdims=True)


def _out_even_kernel(x_ref, a_ref, y_ref, wa_ref, wy_ref, fg_ref, wr_hi_ref, wr_lo_ref, x1_ref, hn_ref, aff_ref):
    x1 = x_ref[...] + _dot(a_ref[...], wa_ref[...]) + _dot(y_ref[...], wy_ref[...])
    x1_ref[...] = x1
    _norm_and_route(x1, fg_ref, wr_hi_ref, wr_lo_ref, hn_ref, aff_ref)


def _out_odd_kernel(x_ref, gb_ref, cu_ref, cp_ref, cn_ref, cw_ref, wo_ref, fg_ref, wr_hi_ref, wr_lo_ref,
                    x1_ref, hn_ref, aff_ref, *, tiles_per_seq):
    i = pl.program_id(0) % tiles_per_seq
    first, last = i == 0, i == tiles_per_seq - 1
    cur, prev8, next8 = cu_ref[...], cp_ref[...], cn_ref[...]
    conv = sum(_shifted(cur, prev8, next8, j - SHORT_CONV // 2, first, last) * cw_ref[j:j + 1, :]
               for j in range(SHORT_CONV))
    x1 = x_ref[...] + _dot((gb_ref[...] * conv).astype(BF16), wo_ref[...])
    x1_ref[...] = x1
    _norm_and_route(x1, fg_ref, wr_hi_ref, wr_lo_ref, hn_ref, aff_ref)


def _route_outs(n, b, seq, tm):
    row = pl.BlockSpec((tm, D_MODEL), lambda i: (i, 0))
    aff = pl.BlockSpec((1, N_EXPERTS, tm), lambda i: (i // (seq // tm), 0, i % (seq // tm)))
    shapes = [jax.ShapeDtypeStruct((n, D_MODEL), F32), jax.ShapeDtypeStruct((n, D_MODEL), BF16),
              jax.ShapeDtypeStruct((b, N_EXPERTS, seq), F32)]
    return [row, row, aff], shapes


def _out_even(x, attn, y, wa, wy, fg, wr_hi, wr_lo, b, seq):
    n = x.shape[0]
    tm = ROW_TILE
    row = pl.BlockSpec((tm, D_MODEL), lambda i: (i, 0))
    out_specs, shapes = _route_outs(n, b, seq, tm)
    return pl.pallas_call(
        _out_even_kernel, grid=(n // tm,),
        in_specs=[row, row, row, _const_spec(wa.shape), _const_spec(wy.shape), _const_spec((1, D_MODEL)),
                  _const_spec(wr_hi.shape), _const_spec(wr_lo.shape)],
        out_specs=out_specs, out_shape=shapes, compiler_params=_cparams(1), name="out_even",
    )(x, attn, y, wa, wy, fg, wr_hi, wr_lo)


def _out_odd(x, gb, cu, cw, wo, fg, wr_hi, wr_lo, b, seq):
    n = x.shape[0]
    tm = ROW_TILE
    row = pl.BlockSpec((tm, D_MODEL), lambda i: (i, 0))
    prev = pl.BlockSpec((8, D_MODEL), lambda i: (jnp.maximum(i * (tm // 8) - 1, 0), 0))
    nxt = pl.BlockSpec((8, D_MODEL), lambda i: (jnp.minimum((i + 1) * (tm // 8), n // 8 - 1), 0))
    out_specs, shapes = _route_outs(n, b, seq, tm)
    return pl.pallas_call(
        functools.partial(_out_odd_kernel, tiles_per_seq=seq // tm), grid=(n // tm,),
        in_specs=[row, row, row, prev, nxt, pl.BlockSpec((8, D_MODEL), lambda i: (0, 0)), _const_spec(wo.shape),
                  _const_spec((1, D_MODEL)), _const_spec(wr_hi.shape), _const_spec(wr_lo.shape)],
        out_specs=out_specs, out_shape=shapes, compiler_params=_cparams(1), name="out_odd",
    )(x, gb, cu, cu, cu, cw, wo, fg, wr_hi, wr_lo)


def _in_odd_kernel(x_ref, g_ref, wb_ref, wc_ref, wu_ref, gb_ref, cu_ref):
    hb = _rms(x_ref[...], g_ref[...]).astype(BF16)
    gb_ref[...] = _dot(hb, wb_ref[...])
    cu_ref[...] = _dot(hb, wc_ref[...]) * _dot(hb, wu_ref[...])


def _in_odd(x, g, wb, wc, wu):
    n = x.shape[0]
    tm = ROW_TILE
    row = pl.BlockSpec((tm, D_MODEL), lambda i: (i, 0))
    return pl.pallas_call(
        _in_odd_kernel, grid=(n // tm,),
        in_specs=[row, _const_spec((1, D_MODEL)), _const_spec(wb.shape), _const_spec(wc.shape), _const_spec(wu.shape)],
        out_specs=[row, row], out_shape=[jax.ShapeDtypeStruct((n, D_MODEL), F32)] * 2,
        compiler_params=_cparams(1), name="in_odd",
    )(x, g, wb, wc, wu)


def _count(mask):
    return jnp.sum(jnp.sum(mask.astype(F32), axis=0, keepdims=True), axis=1, keepdims=True)


def _route_kernel(aff_ref, incl_ref, ones_ref, strict_ref, local_ref, group_ref, first_ref, slot_ref, off_ref, end_ref,
                  *, cap):
    for e in range(N_EXPERTS):
        bits = pltpu.bitcast(aff_ref[0, e], I32)

        def step(i, thr):
            cand = thr | (jnp.int32(1) << (30 - i))
            return jnp.where(_count(bits >= cand) >= cap, cand, thr)

        thr = lax.fori_loop(0, 31, step, jnp.zeros((1, 1), I32))
        gt = bits > thr
        eq = (bits == thr).astype(BF16)
        eq_rank = _dot(eq, incl_ref[...]) + _dot(strict_ref[...], _dot(eq, ones_ref[...]).astype(BF16))
        sel = (gt | ((bits == thr) & (eq_rank <= cap - _count(gt)))).astype(BF16)
        within = _dot(sel, incl_ref[...])
        totals = _dot(sel, ones_ref[...]).astype(BF16)
        local = _dot(local_ref[...], totals)
        cnt = _dot(group_ref[...], totals)
        padded = jnp.floor((cnt + (SLOT_ALIGN - 1)) * (1.0 / SLOT_ALIGN)) * SLOT_ALIGN
        start = _dot(first_ref[...], padded.astype(BF16))
        slot_ref[0, e] = jnp.where(sel > 0, (start + local + within).astype(I32) - 1, -1)
        off_ref[0, e:e + 1, :] = start.T[0:1, :].astype(I32)
        end_ref[0, e:e + 1, :] = (start + padded).T[0:1, :].astype(I32)


def _route(aff, cap):
    b, _, seq = aff.shape
    nt = seq // LANE
    tri = np.arange(LANE)
    tt = np.arange(nt)
    grp = tt // SUB
    as_bf16 = lambda m: jnp.asarray(m, BF16)
    incl = as_bf16(tri[:, None] <= tri[None, :])
    strict = as_bf16(tt[None, :] < tt[:, None])
    local = as_bf16((tt[None, :] < tt[:, None]) & (grp[None, :] == grp[:, None]))
    group = as_bf16(grp[None, :] == grp[:, None])
    first = as_bf16((grp[None, :] < grp[:, None]) & (tt[None, :] % SUB == 0))
    ones = jnp.ones((LANE, LANE), BF16)
    tiles = pl.BlockSpec((1, N_EXPERTS, nt, LANE), lambda i: (i, 0, 0, 0))
    rows = pl.BlockSpec((1, N_EXPERTS, nt), lambda i: (i, 0, 0))
    return pl.pallas_call(
        functools.partial(_route_kernel, cap=cap), grid=(b,),
        in_specs=[tiles, _const_spec((LANE, LANE)), _const_spec((LANE, LANE))] + [_const_spec((nt, nt))] * 4,
        out_specs=[tiles, rows, rows],
        out_shape=[jax.ShapeDtypeStruct((b, N_EXPERTS, nt, LANE), I32)] + [jax.ShapeDtypeStruct((b, N_EXPERTS, nt), I32)] * 2,
        compiler_params=_cparams(1), name="route",
    )(aff.reshape(b, N_EXPERTS, nt, LANE), incl, ones, strict, local, group, first)


def _one_hot(slots, base):
    return (slots == lax.broadcasted_iota(I32, (WINDOW, slots.shape[1]), 0) + base).astype(BF16)


def _gather_kernel(off_ref, end_ref, slot_ref, aff_ref, hn_ref, zero_hbm, xe_hbm, stage, extra, sem, xsem):
    del zero_hbm
    bi, j = pl.program_id(0), pl.program_id(1)
    n_j = pl.num_programs(1)
    step = bi * n_j + j
    cur = step % 2

    def window_copy(bb, jj, e, buf):
        start = pl.multiple_of(off_ref[bb, e, jj * SUB], SLOT_ALIGN)
        return pltpu.make_async_copy(stage.at[buf, e], xe_hbm.at[bb, e, pl.ds(start, WINDOW), :], sem.at[buf, e])

    tokens = hn_ref[0]
    parts = [p.astype(F32) for p in _split3(aff_ref[0, 0])]
    gates = jnp.concatenate(parts + [jnp.zeros((LANE - 3 * N_EXPERTS, TOK_TILE), F32)], axis=0).astype(BF16)

    def rows_of(p):
        return jnp.concatenate([_dot(p, tokens), _dot_nt(p, gates)], axis=1).astype(BF16)

    p_all = jnp.concatenate([_one_hot(slot_ref[0, 0, e:e + 1, :], off_ref[bi, e, j * SUB])
                             for e in range(N_EXPERTS)], axis=0)
    stage[cur] = rows_of(p_all).reshape(N_EXPERTS, WINDOW, D_MODEL + LANE)

    @pl.when(step > 0)
    def _():
        prev = step - 1
        for e in range(N_EXPERTS):
            window_copy(prev // n_j, prev % n_j, e, 1 - cur).wait()

    for e in range(N_EXPERTS):
        window_copy(bi, j, e, cur).start()

    def overflow(e, c):
        first = off_ref[bi, e, j * SUB]
        n_win = (end_ref[bi, e, j * SUB] - first + WINDOW - 1) // WINDOW
        slots = slot_ref[0, 0, pl.ds(e, 1), :]

        def one(w, c2):
            base = pl.multiple_of(first + w * WINDOW, SLOT_ALIGN)
            extra[...] = rows_of(_one_hot(slots, base))
            cp = pltpu.make_async_copy(extra, xe_hbm.at[bi, e, pl.ds(base, WINDOW), :], xsem.at[0])
            cp.start()
            cp.wait()
            return c2

        return lax.fori_loop(1, n_win, one, c)

    lax.fori_loop(0, N_EXPERTS, overflow, 0)

    @pl.when(step == pl.num_programs(0) * n_j - 1)
    def _():
        for e in range(N_EXPERTS):
            window_copy(bi, j, e, cur).wait()


def _gather(off, end, slot_t, aff_t, hn):
    b, seq, _ = hn.shape
    cap_pad = _cap_pad(seq)
    width = D_MODEL + LANE
    per_tile = pl.BlockSpec((1, 1, N_EXPERTS, TOK_TILE), lambda bi, j, *_: (bi, j, 0, 0))
    return pl.pallas_call(
        _gather_kernel,
        grid_spec=pltpu.PrefetchScalarGridSpec(
            num_scalar_prefetch=2, grid=(b, seq // TOK_TILE),
            in_specs=[per_tile, per_tile, pl.BlockSpec((1, TOK_TILE, D_MODEL), lambda bi, j, *_: (bi, j, 0)),
                      pl.BlockSpec(memory_space=pl.ANY)],
            out_specs=pl.BlockSpec(memory_space=pl.ANY),
            scratch_shapes=[pltpu.VMEM((2, N_EXPERTS, WINDOW, width), BF16), pltpu.VMEM((WINDOW, width), BF16),
                            pltpu.SemaphoreType.DMA((2, N_EXPERTS)), pltpu.SemaphoreType.DMA((1,))]),
        out_shape=jax.ShapeDtypeStruct((b, N_EXPERTS, cap_pad, width), BF16),
        input_output_aliases={5: 0},
        compiler_params=_cparams(2), name="moe_gather",
    )(off, end, slot_t, aff_t, hn, jnp.zeros((b, N_EXPERTS, cap_pad, width), BF16))


def _ffn_kernel(end_ref, xe_ref, wg_hbm, wu_hbm, wd_hbm, y_ref, stage_g, stage_u, stage_d, wg, wu, wd, sem, *, layer):
    e, bi, r = pl.program_id(0), pl.program_id(1), pl.program_id(2)
    used = end_ref[bi, e, end_ref.shape[2] - 1]
    pairs = ((wg_hbm, stage_g, wg), (wu_hbm, stage_u, wu), (wd_hbm, stage_d, wd))

    def weight_copies(ee):
        return [pltpu.make_async_copy(src.at[layer, ee], stg, sem.at[k]) for k, (src, stg, _) in enumerate(pairs)]

    @pl.when((bi == 0) & (r == 0))
    def _():
        @pl.when(e == 0)
        def _():
            for cp in weight_copies(e):
                cp.start()

        for cp, (_, stg, dst) in zip(weight_copies(e), pairs):
            cp.wait()
            n_rows = stg.shape[0]

            def cast(i, c, stg=stg, dst=dst):
                rows = pl.ds(pl.multiple_of(i * CAST_ROWS, CAST_ROWS), CAST_ROWS)
                dst[rows, :] = stg[rows, :].astype(BF16)
                return c

            lax.fori_loop(0, n_rows // CAST_ROWS, cast, 0)

        @pl.when(e + 1 < pl.num_programs(0))
        def _():
            for cp in weight_copies(e + 1):
                cp.start()

    @pl.when(r * FFN_ROWS < used)
    def _():
        xe = xe_ref[0, 0, :, 0:D_MODEL]
        hid = (_silu(_dot(xe, wg[...])) * _dot(xe, wu[...])).astype(BF16)
        g = xe_ref[0, 0, :, D_MODEL:].astype(F32)
        lane = lax.broadcasted_iota(I32, g.shape, 1)
        mine = (lane % N_EXPERTS == e) & (lane < 3 * N_EXPERTS)
        gate = jnp.sum(jnp.where(mine, g, 0.0), axis=1, keepdims=True)
        y_ref[0, 0] = (_dot(hid, wd[...]) * gate).astype(BF16)

    @pl.when(r * FFN_ROWS >= used)
    def _():
        y_ref[...] = jnp.zeros_like(y_ref)


def _ffn(end, xe, wg, wu, wd, layer):
    b, ne, cap_pad, width = xe.shape
    rows = lambda w: pl.BlockSpec((1, 1, FFN_ROWS, w), lambda e, bi, r, *_: (bi, e, r, 0))
    hbm = pl.BlockSpec(memory_space=pl.ANY)
    return pl.pallas_call(
        functools.partial(_ffn_kernel, layer=layer),
        grid_spec=pltpu.PrefetchScalarGridSpec(
            num_scalar_prefetch=1, grid=(ne, b, cap_pad // FFN_ROWS),
            in_specs=[rows(width), hbm, hbm, hbm],
            out_specs=rows(D_MODEL),
            scratch_shapes=[pltpu.VMEM((D_MODEL, D_FF), F32), pltpu.VMEM((D_MODEL, D_FF), F32),
                            pltpu.VMEM((D_FF, D_MODEL), F32), pltpu.VMEM((D_MODEL, D_FF), BF16),
                            pltpu.VMEM((D_MODEL, D_FF), BF16), pltpu.VMEM((D_FF, D_MODEL), BF16),
                            pltpu.SemaphoreType.DMA((3,))]),
        out_shape=jax.ShapeDtypeStruct((b, ne, cap_pad, D_MODEL), BF16),
        compiler_params=_cparams(3), name="moe_ffn",
    )(end, xe, wg, wu, wd)


def _combine_kernel(off_ref, end_ref, slot_ref, y_hbm, x1_ref, o_ref, win, extra, sem, xsem):
    bi, j = pl.program_id(0), pl.program_id(1)
    n_j = pl.num_programs(1)
    step = bi * n_j + j
    cur = step % 2

    def window_copy(bb, jj, e, buf):
        start = pl.multiple_of(off_ref[bb, e, jj * SUB], SLOT_ALIGN)
        return pltpu.make_async_copy(y_hbm.at[bb, e, pl.ds(start, WINDOW), :], win.at[buf, e], sem.at[buf, e])

    @pl.when(step == 0)
    def _():
        for e in range(N_EXPERTS):
            window_copy(bi, j, e, cur).start()

    @pl.when(step + 1 < pl.num_programs(0) * n_j)
    def _():
        nxt = step + 1
        for e in range(N_EXPERTS):
            window_copy(nxt // n_j, nxt % n_j, e, 1 - cur).start()

    ps = []
    for e in range(N_EXPERTS):
        window_copy(bi, j, e, cur).wait()
        ps.append(_one_hot(slot_ref[0, 0, e:e + 1, :], off_ref[bi, e, j * SUB]))
    p_all = jnp.concatenate(ps, axis=0)
    y_all = win[cur].reshape(N_EXPERTS * WINDOW, D_MODEL)
    o_ref[0] = x1_ref[0] + _dot_tn(p_all, y_all)

    def overflow(e, c):
        first = off_ref[bi, e, j * SUB]
        n_win = (end_ref[bi, e, j * SUB] - first + WINDOW - 1) // WINDOW
        slots = slot_ref[0, 0, pl.ds(e, 1), :]

        def one(w, c2):
            base = pl.multiple_of(first + w * WINDOW, SLOT_ALIGN)
            cp = pltpu.make_async_copy(y_hbm.at[bi, e, pl.ds(base, WINDOW), :], extra, xsem.at[0])
            cp.start()
            cp.wait()
            o_ref[0] += _dot_tn(_one_hot(slots, base), extra[...])
            return c2

        return lax.fori_loop(1, n_win, one, c)

    lax.fori_loop(0, N_EXPERTS, overflow, 0)


def _combine(off, end, slot_t, y, x1):
    b, seq, _ = x1.shape
    tile = pl.BlockSpec((1, TOK_TILE, D_MODEL), lambda bi, j, *_: (bi, j, 0))
    return pl.pallas_call(
        _combine_kernel,
        grid_spec=pltpu.PrefetchScalarGridSpec(
            num_scalar_prefetch=2, grid=(b, seq // TOK_TILE),
            in_specs=[pl.BlockSpec((1, 1, N_EXPERTS, TOK_TILE), lambda bi, j, *_: (bi, j, 0, 0)),
                      pl.BlockSpec(memory_space=pl.ANY), tile],
            out_specs=tile,
            scratch_shapes=[pltpu.VMEM((2, N_EXPERTS, WINDOW, D_MODEL), BF16), pltpu.VMEM((WINDOW, D_MODEL), BF16),
                            pltpu.SemaphoreType.DMA((2, N_EXPERTS)), pltpu.SemaphoreType.DMA((1,))]),
        out_shape=jax.ShapeDtypeStruct((b, seq, D_MODEL), F32),
        compiler_params=_cparams(2), name="moe_combine",
    )(off, end, slot_t, y, x1)


def _cap_pad(seq):
    cap = CAPACITY_FACTOR * seq // N_EXPERTS
    worst = cap + (seq // TOK_TILE) * (SLOT_ALIGN - 1) + WINDOW
    return -(-worst // FFN_ROWS) * FFN_ROWS


def _moe(x1, hn, aff, wg, wu, wd, layer):
    b, seq, _ = x1.shape
    cap = CAPACITY_FACTOR * seq // N_EXPERTS
    slot, off, end = _route(aff, cap)
    per_tile = lambda a: jnp.swapaxes(a.reshape(b, N_EXPERTS, seq // TOK_TILE, TOK_TILE), 1, 2)
    slot_t = per_tile(slot)
    xe = _gather(off, end, slot_t, per_tile(aff), hn)
    y = _ffn(end, xe, wg, wu, wd, layer)
    return _combine(off, end, slot_t, y, x1)


def _rope_tables(seq):
    inv_freq = ROPE_THETA ** (-jnp.arange(ROPE_HALF, dtype=F32) * 2.0 / (2 * ROPE_HALF))
    ang = jnp.arange(seq, dtype=F32)[:, None] * inv_freq[None, :]
    cos, sin = jnp.cos(ang), jnp.sin(ang)
    z = lambda w: jnp.zeros((seq, w), F32)
    rest = HEAD_DIM - 2 * ROPE_HALF
    rc = jnp.concatenate([cos, cos, jnp.ones((seq, rest), F32)], axis=1)
    r1 = jnp.concatenate([-sin, z(ROPE_HALF + rest)], axis=1)
    r2 = jnp.concatenate([z(ROPE_HALF), sin, z(rest)], axis=1)
    return tuple(jnp.tile(t, (1, LANE // HEAD_DIM)) for t in (rc, r1, r2))


def _router_split(w):
    wp = jnp.pad(w, ((0, 0), (0, LANE - N_EXPERTS)))
    hi = wp.astype(BF16)
    return hi, (wp - hi.astype(F32)).astype(BF16)


def kernel(x, attn_norm, w_in_even, q_norm, k_norm, ssd_conv_w, ssd_conv_b, ssd_a_log_fwd, ssd_a_log_bwd,
           ssd_dt_bias_fwd, ssd_dt_bias_bwd, ssd_d, ssd_out_norm, w_out_even, conv_norm, conv_w_in, conv_w,
           conv_w_out, ffn_norm, router_w, expert_w_gate, expert_w_up, expert_w_down):
    b, seq, _ = x.shape
    n = b * seq
    depth = ffn_norm.shape[0]
    rc, r1, r2 = _rope_tables(seq)
    blk = np.arange(256) // HEAD_DIM
    bd = jnp.asarray((blk[:, None] == blk[None, :]) / HEAD_DIM, BF16)
    row = lambda v: v.reshape(1, -1).astype(F32)

    xf = x.reshape(n, D_MODEL)
    for layer in range(depth):
        i = layer // 2
        wr_hi, wr_lo = _router_split(router_w[layer])
        fg = row(ffn_norm[layer])
        if layer % 2 == 0:
            w = w_in_even[i].astype(BF16)
            o = np.cumsum([0, D_MODEL, D_MODEL, D_MODEL, D_MODEL, SSD_XBC, N_HEADS, N_HEADS])
            wq, wk, wv, wz, wx = (w[:, o[j]:o[j + 1]] for j in range(5))
            wd = jnp.pad(w[:, o[5]:o[7]], ((0, 0), (0, LANE - 2 * N_HEADS)))
            tile_heads = lambda g: row(jnp.tile(g, N_HEADS))
            q, k, v, z, xbc, dt = _in_even(xf, row(attn_norm[i]), wq, wk, wv, wz, wx, wd,
                                           tile_heads(q_norm[i]), tile_heads(k_norm[i]), bd, rc, r1, r2, seq)
            as3 = lambda t: t.reshape(b, seq, -1)
            attn = _attention(as3(q), as3(k), as3(v))
            cw = jnp.pad(ssd_conv_w[i], ((0, 8 - SSD_CONV), (0, 0)))
            act = _ssd_conv(as3(xbc), cw, row(ssd_conv_b[i]))
            pc = jnp.pad(jnp.stack([ssd_dt_bias_fwd[i], ssd_dt_bias_bwd[i], -jnp.exp(ssd_a_log_fwd[i]),
                                    -jnp.exp(ssd_a_log_bwd[i])], axis=1).astype(F32), ((0, 0), (0, LANE - 4)))
            y = _ssd(act, as3(dt), as3(z), pc, row(jnp.repeat(ssd_d[i], HEAD_DIM)), row(ssd_out_norm[i]))
            wo = w_out_even[i].astype(BF16)
            x1, hn, aff = _out_even(xf, attn.reshape(n, D_MODEL), y.reshape(n, D_MODEL), wo[:D_MODEL], wo[D_MODEL:],
                                    fg, wr_hi, wr_lo, b, seq)
        else:
            w = conv_w_in[i].astype(BF16)
            gb, cu = _in_odd(xf, row(conv_norm[i]), w[:, :D_MODEL], w[:, D_MODEL:2 * D_MODEL], w[:, 2 * D_MODEL:])
            cw = jnp.pad(conv_w[i], ((0, 8 - SHORT_CONV), (0, 0)))
            x1, hn, aff = _out_odd(xf, gb, cu, cw, conv_w_out[i].astype(BF16), fg, wr_hi, wr_lo, b, seq)
        xf = _moe(x1.reshape(b, seq, D_MODEL), hn.reshape(b, seq, D_MODEL), aff, expert_w_gate, expert_w_up,
                  expert_w_down, layer).reshape(n, D_MODEL)
    return xf.reshape(b, seq, D_MODEL)
```

```python
import functools
import math

import jax
import jax.numpy as jnp
import numpy as np
from jax import lax
from jax.experimental import pallas as pl
from jax.experimental.pallas import tpu as pltpu

F32, BF16, I32 = jnp.float32, jnp.bfloat16, jnp.int32

D_MODEL = 1024
N_HEADS = 16
HEAD_DIM = 64
ROPE_HALF = 8
ROPE_THETA = 500000.0
PATTERNS = ((128, 1), (512, 4), (2048, 16))
HALF_STEPS = 64
SSD_GROUPS = 2
SSD_STATE = 128
SSD_XBC = 1536
SSD_CONV = 5
CHUNK = 128
N_EXPERTS = 16
CAPACITY_FACTOR = 2
D_FF = 2048
SHORT_CONV = 3
EPS = 1e-6

LANE = 128
VMEM_LIMIT = 56 * 1024 * 1024

ROW_TILE = 256
ATTN_SUPER = 2048
ATTN_HALO = 1024
ATTN_TQ = 128
ATTN_TK = ATTN_TQ + 2 * HALF_STEPS
ATTN_UNROLL = 8
TOK_TILE = 512
SUB = TOK_TILE // LANE
SLOT_ALIGN = 16
WINDOW = 96
FFN_ROWS = 256
CAST_ROWS = 64


def _cparams(n_axes):
    return pltpu.CompilerParams(dimension_semantics=("arbitrary",) * n_axes, vmem_limit_bytes=VMEM_LIMIT)


def _const_spec(shape):
    nd = len(shape)
    return pl.BlockSpec(shape, lambda *_: (0,) * nd, pipeline_mode=pl.Buffered(1))


def _dot(a, b):
    return jnp.dot(a, b, preferred_element_type=F32)


def _dot_nt(a, b):
    return lax.dot_general(a, b, (((1,), (1,)), ((), ())), preferred_element_type=F32)


def _dot_tn(a, b):
    return lax.dot_general(a, b, (((0,), (0,)), ((), ())), preferred_element_type=F32)


def _split2(x):
    hi = x.astype(BF16)
    lo = (x - hi.astype(F32)).astype(BF16)
    return hi, lo


def _split3(x):
    hi = x.astype(BF16)
    r = x - hi.astype(F32)
    mid = r.astype(BF16)
    lo = (r - mid.astype(F32)).astype(BF16)
    return hi, mid, lo


def _dot3(x, m_bf16):
    hi, mid, lo = _split3(x)
    return _dot(hi, m_bf16) + _dot(mid, m_bf16) + _dot(lo, m_bf16)


def _rms(x, g):
    return x * lax.rsqrt(jnp.mean(x * x, axis=-1, keepdims=True) + EPS) * g


def _silu(x):
    return x * jax.nn.sigmoid(x)


def _in_even_kernel(x_ref, g_ref, wq_ref, wk_ref, wv_ref, wz_ref, wx_ref, wd_ref, qg_ref, kg_ref, bd_ref,
                    rc_ref, r1_ref, r2_ref, q_ref, k_ref, v_ref, z_ref, xbc_ref, dt_ref):
    hb = _rms(x_ref[...], g_ref[...]).astype(BF16)
    bd = bd_ref[...]
    rc, r1, r2 = rc_ref[...], r1_ref[...], r2_ref[...]

    def head_norm_rope(w_ref, gain_ref, out_ref):
        t = _dot(hb, w_ref[...])
        for c in range(D_MODEL // 256):
            tc = t[:, c * 256:(c + 1) * 256]
            sq_hi, sq_lo = _split2(tc * tc)
            ms = _dot(sq_hi, bd) + _dot(sq_lo, bd)
            tn = tc * lax.rsqrt(ms + EPS) * gain_ref[:, c * 256:(c + 1) * 256]
            for hh in range(2):
                u = tn[:, hh * LANE:(hh + 1) * LANE]
                r = u * rc + pltpu.roll(u, LANE - ROPE_HALF, 1) * r1 + pltpu.roll(u, ROPE_HALF, 1) * r2
                out_ref[:, c * 256 + hh * LANE:c * 256 + (hh + 1) * LANE] = r

    head_norm_rope(wq_ref, qg_ref, q_ref)
    head_norm_rope(wk_ref, kg_ref, k_ref)
    v_ref[...] = _dot(hb, wv_ref[...])
    z_ref[...] = _dot(hb, wz_ref[...])
    xbc_ref[...] = _dot(hb, wx_ref[...])
    dt_ref[...] = _dot(hb, wd_ref[...])


def _in_even(x, g, wq, wk, wv, wz, wx, wd, qg, kg, bd, rc, r1, r2, seq):
    n = x.shape[0]
    tm = ROW_TILE
    row = lambda w: pl.BlockSpec((tm, w), lambda i: (i, 0))
    tab = pl.BlockSpec((tm, LANE), lambda i: (i % (seq // tm), 0))
    outs = [jax.ShapeDtypeStruct((n, D_MODEL), F32)] * 4 + [jax.ShapeDtypeStruct((n, SSD_XBC), F32),
                                                           jax.ShapeDtypeStruct((n, LANE), F32)]
    return pl.pallas_call(
        _in_even_kernel, grid=(n // tm,),
        in_specs=[row(D_MODEL), _const_spec((1, D_MODEL)), _const_spec(wq.shape), _const_spec(wk.shape),
                  _const_spec(wv.shape), _const_spec(wz.shape), _const_spec(wx.shape), _const_spec(wd.shape),
                  _const_spec((1, D_MODEL)), _const_spec((1, D_MODEL)), _const_spec((256, 256)), tab, tab, tab],
        out_specs=[row(D_MODEL)] * 4 + [row(SSD_XBC), row(LANE)], out_shape=outs,
        compiler_params=_cparams(1), name="in_even",
    )(x, g, wq, wk, wv, wz, wx, wd, qg, kg, bd, rc, r1, r2)


def _attn_kernel(q_ref, k0, k1, k2, k3, v0, v1, v2, v3, qw_hbm, kw_hbm, vw_hbm, o_ref, qf, kf, vf, q16, k16, v16,
                 acc, mst, lst, a16, m16, l16, an3, mn3, ln3, sem, rsem, *, seq):
    dmax = PATTERNS[-1][1]
    bi, hp, j = pl.program_id(0), pl.program_id(1), pl.program_id(2)
    n_hp, n_j = pl.num_programs(1), pl.num_programs(2)
    p0 = j * ATTN_SUPER
    halo = ATTN_HALO // dmax
    step = (bi * n_hp + hp) * n_j + j
    cur = step % 2

    def residue_copies(at, buf, r, where):
        b_, hp_, j_ = at
        lanes = pl.ds(pl.multiple_of(hp_ * LANE, LANE), LANE)
        q0 = pl.multiple_of(j_ * ATTN_TQ, ATTN_TQ)
        n_kv = ATTN_TK - halo if where else ATTN_TK
        src0 = 0 if where < 0 else q0 - halo
        dst0 = halo if where < 0 else 0
        cps = [pltpu.make_async_copy(qw_hbm.at[b_, pl.ds(q0, ATTN_TQ), r, lanes], q16.at[buf, r], rsem.at[buf, 0, r])]
        for n, (src, dst) in enumerate(((kw_hbm, k16), (vw_hbm, v16))):
            cps.append(pltpu.make_async_copy(src.at[b_, pl.ds(src0, n_kv), r, lanes],
                                             dst.at[buf, r, pl.ds(dst0, n_kv), :], rsem.at[buf, 1 + n, r]))
        return cps

    def all_residues(at, buf, action):
        def run(where):
            for r in range(dmax):
                for cp in residue_copies(at, buf, r, where):
                    getattr(cp, action)()
        j_ = at[2]
        pl.when(j_ == 0)(functools.partial(run, -1))
        pl.when((j_ > 0) & (j_ < n_j - 1))(functools.partial(run, 0))
        pl.when(j_ == n_j - 1)(functools.partial(run, 1))

    def state_copies(r):
        return [pltpu.make_async_copy(src.at[r], dst.at[:, r, :], sem.at[n, r])
                for n, (src, dst) in enumerate(((a16, an3), (m16, mn3), (l16, ln3)))]

    @pl.when(step == 0)
    def _():
        k16[...] = jnp.zeros_like(k16)
        v16[...] = jnp.zeros_like(v16)
        all_residues((bi, hp, j), cur, "start")

    @pl.when(step + 1 < pl.num_programs(0) * n_hp * n_j)
    def _():
        nxt = step + 1
        all_residues((nxt // (n_hp * n_j), (nxt // n_j) % n_hp, nxt % n_j), 1 - cur, "start")

    qf[...] = q_ref[0] * (HEAD_DIM ** -0.5)
    for i, (kr, vr) in enumerate(((k0, v0), (k1, v1), (k2, v2), (k3, v3))):
        kf[i * ATTN_HALO:(i + 1) * ATTN_HALO, :] = kr[0]
        vf[i * ATTN_HALO:(i + 1) * ATTN_HALO, :] = vr[0]

    head_a = lax.broadcasted_iota(I32, (ATTN_TQ, LANE), 1) < HEAD_DIM
    off = lax.broadcasted_iota(I32, (ATTN_TQ, ATTN_TK), 1) - lax.broadcasted_iota(I32, (ATTN_TQ, ATTN_TK), 0)
    band = (off >= 0) & (off <= 2 * HALF_STEPS)
    band2 = jnp.concatenate([band, band], axis=0)
    colpos = lax.broadcasted_iota(I32, (1, ATTN_TK), 1)

    def local_softmax(q, kt, vt, first_pos, d):
        pos = first_pos + d * colpos
        valid = band2 & (pos >= 0) & (pos < seq)
        q2 = jnp.concatenate([jnp.where(head_a, q, 0.0), jnp.where(head_a, 0.0, q)], axis=0).astype(BF16)
        s = jnp.where(valid, _dot_nt(q2, kt.astype(BF16)), -jnp.inf)
        m = jnp.max(s, axis=1, keepdims=True)
        p = jnp.exp(s - m)
        l = jnp.sum(p, axis=1, keepdims=True)
        n = _dot(p.astype(BF16), vt.astype(BF16))
        return (jnp.where(head_a, m[:ATTN_TQ], m[ATTN_TQ:]), jnp.where(head_a, l[:ATTN_TQ], l[ATTN_TQ:]),
                jnp.where(head_a, n[:ATTN_TQ], n[ATTN_TQ:]))

    def visit(d, i, first):
        log_d = d.bit_length() - 1
        r = i & (d - 1)
        t = i >> log_d
        qs = r + d * ATTN_TQ * t
        ks = ATTN_HALO + r + d * (ATTN_TQ * t - HALF_STEPS)
        m_loc, l_loc, n_loc = local_softmax(qf[pl.ds(qs, ATTN_TQ, stride=d), :], kf[pl.ds(ks, ATTN_TK, stride=d), :],
                                            vf[pl.ds(ks, ATTN_TK, stride=d), :], p0 - ATTN_HALO + ks, d)
        rows = pl.ds(qs, ATTN_TQ, stride=d)
        if first:
            mst[rows, :], lst[rows, :], acc[rows, :] = m_loc, l_loc, n_loc
        else:
            m_old = mst[rows, :]
            m_new = jnp.maximum(m_old, m_loc)
            w_old = jnp.exp(m_old - m_new)
            w_loc = jnp.exp(m_loc - m_new)
            mst[rows, :] = m_new
            lst[rows, :] = lst[rows, :] * w_old + l_loc * w_loc
            acc[rows, :] = acc[rows, :] * w_old + n_loc * w_loc

    def visit_dmax(r):
        q = q16[cur, r] * (HEAD_DIM ** -0.5)
        m16[r], l16[r], a16[r] = local_softmax(q, k16[cur, r], v16[cur, r], p0 - ATTN_HALO + r, dmax)

    def loop(n, fn):
        def body(i, c):
            for u in range(ATTN_UNROLL):
                fn(i * ATTN_UNROLL + u)
            return c
        lax.fori_loop(0, n // ATTN_UNROLL, body, 0)

    n_visits = ATTN_SUPER // ATTN_TQ
    loop(n_visits, lambda i: visit(PATTERNS[0][1], i, True))
    all_residues((bi, hp, j), cur, "wait")
    loop(dmax, visit_dmax)
    for r in range(dmax):
        for cp in state_copies(r):
            cp.start()
    for _, d in PATTERNS[1:-1]:
        loop(n_visits, lambda i, d=d: visit(d, i, False))
    for r in range(dmax):
        for cp in state_copies(r):
            cp.wait()

    def finish(i, c):
        rows = pl.ds(pl.multiple_of(i * ATTN_TQ, ATTN_TQ), ATTN_TQ)
        slabs = pl.ds(pl.multiple_of(i * (ATTN_TQ // dmax), ATTN_TQ // dmax), ATTN_TQ // dmax)
        a_a, m_a, l_a = (t[slabs].reshape(ATTN_TQ, LANE) for t in (an3, mn3, ln3))
        m_b = mst[rows, :]
        m = jnp.maximum(m_a, m_b)
        w_a, w_b = jnp.exp(m_a - m), jnp.exp(m_b - m)
        o = (a_a * w_a + acc[rows, :] * w_b) / (l_a * w_a + lst[rows, :] * w_b)
        o_ref[0, rows, :] = o.astype(BF16)
        return c

    lax.fori_loop(0, n_visits, finish, 0)


def _attention(q, k, v):
    b, seq, _ = q.shape
    nblk = seq // ATTN_HALO
    ratio = ATTN_SUPER // ATTN_HALO
    dmax = PATTERNS[-1][1]
    assert ATTN_SUPER // dmax == ATTN_TQ and 4 * ATTN_HALO // dmax == ATTN_TK
    assert seq // ATTN_SUPER >= 2

    def halo(i):
        return pl.BlockSpec((1, ATTN_HALO, LANE),
                            lambda bi, hp, j: (bi, jnp.clip(ratio * j - 1 + i, 0, nblk - 1), hp))

    main = pl.BlockSpec((1, ATTN_SUPER, LANE), lambda bi, hp, j: (bi, j, hp))
    hbm = pl.BlockSpec(memory_space=pl.ANY)
    by_residue = lambda t: t.reshape(b, seq // dmax, dmax, D_MODEL)
    return pl.pallas_call(
        functools.partial(_attn_kernel, seq=seq),
        grid=(b, D_MODEL // LANE, seq // ATTN_SUPER),
        in_specs=[main] + [halo(i) for i in range(4)] * 2 + [hbm] * 3,
        out_specs=main,
        out_shape=jax.ShapeDtypeStruct((b, seq, D_MODEL), BF16),
        scratch_shapes=[pltpu.VMEM((ATTN_SUPER, LANE), F32), pltpu.VMEM((4 * ATTN_HALO, LANE), F32),
                        pltpu.VMEM((4 * ATTN_HALO, LANE), F32), pltpu.VMEM((2, dmax, ATTN_TQ, LANE), F32),
                        pltpu.VMEM((2, dmax, ATTN_TK, LANE), F32), pltpu.VMEM((2, dmax, ATTN_TK, LANE), F32)]
                       + [pltpu.VMEM((ATTN_SUPER, LANE), F32)] * 3
                       + [pltpu.VMEM((dmax, ATTN_TQ, LANE), F32)] * 3
                       + [pltpu.VMEM((ATTN_TQ, dmax, LANE), F32)] * 3
                       + [pltpu.SemaphoreType.DMA((3, dmax)), pltpu.SemaphoreType.DMA((2, 3, dmax))],
        compiler_params=_cparams(3), name="dilated_attn",
    )(q, k, k, k, k, v, v, v, v, by_residue(q), by_residue(k), by_residue(v))


def _shifted(cur, prev8, next8, s, first, last):
    n = cur.shape[0]
    if s == 0:
        return cur
    rows = lax.broadcasted_iota(I32, cur.shape, 0)
    out = pltpu.roll(cur, (-s) % n, 0)
    if s < 0:
        for j in range(-s):
            src = jnp.where(first, 0.0, prev8[8 + s + j:8 + s + j + 1, :])
            out = jnp.where(rows == j, src, out)
    else:
        for j in range(s):
            src = jnp.where(last, 0.0, next8[j:j + 1, :])
            out = jnp.where(rows == n - s + j, src, out)
    return out


def _ssd_conv_kernel(c_ref, p_ref, n_ref, w_ref, b_ref, o_ref):
    first = pl.program_id(1) == 0
    last = pl.program_id(1) == pl.num_programs(1) - 1
    for c in range(SSD_XBC // 256):
        sl = slice(c * 256, (c + 1) * 256)
        cur, prev8, next8 = c_ref[0, :, sl], p_ref[0, :, sl], n_ref[0, :, sl]
        y = b_ref[:, sl]
        for j in range(SSD_CONV):
            y = y + _shifted(cur, prev8, next8, j - SSD_CONV // 2, first, last) * w_ref[j:j + 1, sl]
        o_ref[0, :, sl] = _silu(y)


def _halo_specs(tm, width, seq):
    cur = pl.BlockSpec((1, tm, width), lambda b, i: (b, i, 0))
    prev = pl.BlockSpec((1, 8, width), lambda b, i: (b, jnp.maximum(i * (tm // 8) - 1, 0), 0))
    nxt = pl.BlockSpec((1, 8, width), lambda b, i: (b, jnp.minimum((i + 1) * (tm // 8), seq // 8 - 1), 0))
    return cur, prev, nxt


def _ssd_conv(xbc, w, bias):
    b, seq, width = xbc.shape
    tm = 512
    cur, prev, nxt = _halo_specs(tm, width, seq)
    return pl.pallas_call(
        _ssd_conv_kernel, grid=(b, seq // tm),
        in_specs=[cur, prev, nxt, pl.BlockSpec((8, width), lambda b, i: (0, 0)),
                  pl.BlockSpec((1, width), lambda b, i: (0, 0))],
        out_specs=cur, out_shape=jax.ShapeDtypeStruct(xbc.shape, F32),
        compiler_params=_cparams(2), name="ssd_conv",
    )(xbc, xbc, xbc, w, bias)


def _softplus(x):
    return jnp.maximum(x, 0.0) + jnp.log1p(jnp.exp(-jnp.abs(x)))


def _tri(kind):
    s = lax.broadcasted_iota(I32, (CHUNK, CHUNK), 0)
    l = lax.broadcasted_iota(I32, (CHUNK, CHUNK), 1)
    return {"le": s <= l, "ge": s >= l, "lt": s < l}[kind]


def _expand(cols, e2_ref):
    hi, lo = _split2(cols)
    return _dot(jnp.concatenate([hi, lo], axis=1), e2_ref[...])


def _ssd_bwd_kernel(xs_ref, b_ref, dt_ref, pc_ref, e2_ref, sb_ref, st):
    @pl.when(pl.program_id(1) == 0)
    def _():
        st[...] = jnp.zeros_like(st)

    sb_ref[0, 0] = st[...].astype(BF16)
    dt_t = dt_ref[0].T
    dtb = _softplus(dt_t[N_HEADS:2 * N_HEADS, :] + pc_ref[:, 1:2])
    a = dtb * pc_ref[:, 3:4]
    ex = _dot3(a, _tri("lt").astype(BF16))
    tot = ex[:, CHUNK - 1:CHUNK] + a[:, CHUNK - 1:CHUNK]
    rowform = jnp.concatenate([dtb * jnp.exp(ex), jnp.broadcast_to(jnp.exp(tot), (N_HEADS, CHUNK)),
                               jnp.zeros((CHUNK - 2 * N_HEADS, CHUNK), F32)], axis=0)
    ex2 = _expand(rowform.T, e2_ref)
    xw = (xs_ref[0] * ex2[:, :D_MODEL]).astype(BF16)
    half = D_MODEL // SSD_GROUPS
    upd = [_dot(b_ref[0, :, g * SSD_STATE:(g + 1) * SSD_STATE].T.astype(BF16), xw[:, g * half:(g + 1) * half])
           for g in range(SSD_GROUPS)]
    st[...] = st[...] * ex2[0:1, D_MODEL:] + jnp.concatenate(upd, axis=1)


def _ssd_fwd_kernel(xs_ref, b_ref, c_ref, dt_ref, z_ref, sb_ref, pc_ref, e3_ref, dexp_ref, on_ref, o_ref, st):
    @pl.when(pl.program_id(1) == 0)
    def _():
        st[...] = jnp.zeros_like(st)

    xs = xs_ref[0]
    dt_t = dt_ref[0].T
    dtf = _softplus(dt_t[0:N_HEADS, :] + pc_ref[:, 0:1])
    dtb = _softplus(dt_t[N_HEADS:2 * N_HEADS, :] + pc_ref[:, 1:2])
    af = dtf * pc_ref[:, 2:3]
    ab = dtb * pc_ref[:, 3:4]
    csf = _dot3(af, _tri("le").astype(BF16))
    rcs = _dot3(ab, _tri("ge").astype(BF16))
    totf = csf[:, CHUNK - 1:CHUNK]
    rowform = jnp.concatenate([dtf * jnp.exp(totf - csf), jnp.exp(csf), jnp.exp(rcs), csf, rcs,
                               jnp.zeros((CHUNK - 5 * N_HEADS, CHUNK), F32)], axis=0)
    cols = rowform.T
    ex3 = _expand(cols, e3_ref)
    w_state, e_f, e_b = ex3[:, :D_MODEL], ex3[:, D_MODEL:2 * D_MODEL], ex3[:, 2 * D_MODEL:]

    xb = xs.astype(BF16)
    lower, upper = _tri("ge"), _tri("le")
    head_a = lax.broadcasted_iota(I32, (CHUNK, LANE), 1) < HEAD_DIM
    half = D_MODEL // SSD_GROUPS
    hpg = N_HEADS // SSD_GROUPS
    st_all = st[...]
    sb_all = sb_ref[0, 0]
    ys = []
    b_t = []
    for g in range(SSD_GROUPS):
        bg = b_ref[0, :, g * SSD_STATE:(g + 1) * SSD_STATE]
        cg = c_ref[0, :, g * SSD_STATE:(g + 1) * SSD_STATE].astype(BF16)
        b_t.append(bg.T.astype(BF16))
        gm = _dot_nt(cg, bg.astype(BF16))
        states = jnp.concatenate([st_all[:, g * half:(g + 1) * half].astype(BF16),
                                  sb_all[:, g * half:(g + 1) * half]], axis=1)
        off = _dot(cg, states)
        y_off = (off[:, :half] * e_f[:, g * half:(g + 1) * half]
                 + off[:, half:] * e_b[:, g * half:(g + 1) * half])
        for pair in range(hpg // 2):
            ms = []
            for h in (g * hpg + 2 * pair, g * hpg + 2 * pair + 1):
                dec_f = jnp.where(lower, jnp.exp(cols[:, 3 * N_HEADS + h:3 * N_HEADS + h + 1] - csf[h:h + 1, :]), 0.0)
                dec_b = jnp.where(upper, jnp.exp(cols[:, 4 * N_HEADS + h:4 * N_HEADS + h + 1] - rcs[h:h + 1, :]), 0.0)
                ms.append((gm * (dec_f * dtf[h:h + 1, :] + dec_b * dtb[h:h + 1, :])).astype(BF16))
            lo = g * half + pair * LANE
            xp = xb[:, lo:lo + LANE]
            ys.append(jnp.where(head_a, _dot(ms[0], xp), _dot(ms[1], xp))
                      + y_off[:, pair * LANE:(pair + 1) * LANE])
    y = jnp.concatenate(ys, axis=1) + xs * dexp_ref[...]
    yz = y * _silu(z_ref[0])
    o_ref[0] = _rms(yz, on_ref[...]).astype(BF16)

    xw = (xs * w_state).astype(BF16)
    upd = [_dot(b_t[g], xw[:, g * half:(g + 1) * half]) for g in range(SSD_GROUPS)]
    st[...] = st_all * e_f[CHUNK - 1:CHUNK, :] + jnp.concatenate(upd, axis=1)


def _ssd(xbc_act, dt, z, pc, dexp, out_norm):
    b, seq, _ = xbc_act.shape
    nc = seq // CHUNK
    sel = np.zeros((2 * CHUNK, 3 * D_MODEL), np.float32)
    for part in range(3):
        for h in range(N_HEADS):
            for rep in range(2):
                sel[rep * CHUNK + part * N_HEADS + h, part * D_MODEL + h * HEAD_DIM:part * D_MODEL + (h + 1) * HEAD_DIM] = 1.0
    e3 = jnp.asarray(sel, BF16)
    e2 = jnp.asarray(sel[:, :2 * D_MODEL], BF16)

    rev = lambda bi, c: (bi, nc - 1 - c, 0)
    sb = pl.pallas_call(
        _ssd_bwd_kernel, grid=(b, nc),
        in_specs=[pl.BlockSpec((1, CHUNK, D_MODEL), rev),
                  pl.BlockSpec((1, CHUNK, 2 * SSD_STATE), lambda bi, c: (bi, nc - 1 - c, D_MODEL // (2 * SSD_STATE))),
                  pl.BlockSpec((1, CHUNK, LANE), rev), _const_spec(pc.shape), _const_spec(e2.shape)],
        out_specs=pl.BlockSpec((1, 1, SSD_STATE, D_MODEL), lambda bi, c: (bi, nc - 1 - c, 0, 0)),
        out_shape=jax.ShapeDtypeStruct((b, nc, SSD_STATE, D_MODEL), BF16),
        scratch_shapes=[pltpu.VMEM((SSD_STATE, D_MODEL), F32)],
        compiler_params=_cparams(2), name="ssd_bwd_state",
    )(xbc_act, xbc_act, dt, pc, e2)

    fwd = lambda bi, c: (bi, c, 0)
    return pl.pallas_call(
        _ssd_fwd_kernel, grid=(b, nc),
        in_specs=[pl.BlockSpec((1, CHUNK, D_MODEL), fwd),
                  pl.BlockSpec((1, CHUNK, 2 * SSD_STATE), lambda bi, c: (bi, c, D_MODEL // (2 * SSD_STATE))),
                  pl.BlockSpec((1, CHUNK, 2 * SSD_STATE), lambda bi, c: (bi, c, D_MODEL // (2 * SSD_STATE) + 1)),
                  pl.BlockSpec((1, CHUNK, LANE), fwd), pl.BlockSpec((1, CHUNK, D_MODEL), fwd),
                  pl.BlockSpec((1, 1, SSD_STATE, D_MODEL), lambda bi, c: (bi, c, 0, 0)),
                  _const_spec(pc.shape), _const_spec(e3.shape), _const_spec((1, D_MODEL)), _const_spec((1, D_MODEL))],
        out_specs=pl.BlockSpec((1, CHUNK, D_MODEL), fwd),
        out_shape=jax.ShapeDtypeStruct((b, seq, D_MODEL), BF16),
        scratch_shapes=[pltpu.VMEM((SSD_STATE, D_MODEL), F32)],
        compiler_params=_cparams(2), name="ssd_fwd",
    )(xbc_act, xbc_act, xbc_act, dt, z, sb, pc, e3, dexp, out_norm)


def _norm_and_route(x1, fg_ref, wr_hi_ref, wr_lo_ref, hn_ref, aff_ref):
    hn = _rms(x1, fg_ref[...])
    hi, lo = _split2(hn)
    hn_ref[...] = hi
    logits = _dot(hi, wr_hi_ref[...]) + _dot(lo, wr_hi_ref[...]) + _dot(hi, wr_lo_ref[...])
    lt = logits.T[0:N_EXPERTS, :]
    e = jnp.exp(lt - jnp.max(lt, axis=0, keepdims=True))
    aff_ref[0] = e / jnp.sum(e, axis=0, keepdims=True)


def _out_even_kernel(x_ref, a_ref, y_ref, wa_ref, wy_ref, fg_ref, wr_hi_ref, wr_lo_ref, x1_ref, hn_ref, aff_ref):
    x1 = x_ref[...] + _dot(a_ref[...], wa_ref[...]) + _dot(y_ref[...], wy_ref[...])
    x1_ref[...] = x1
    _norm_and_route(x1, fg_ref, wr_hi_ref, wr_lo_ref, hn_ref, aff_ref)


def _out_odd_kernel(x_ref, gb_ref, cu_ref, cp_ref, cn_ref, cw_ref, wo_ref, fg_ref, wr_hi_ref, wr_lo_ref,
                    x1_ref, hn_ref, aff_ref, *, tiles_per_seq):
    i = pl.program_id(0) % tiles_per_seq
    first, last = i == 0, i == tiles_per_seq - 1
    cur, prev8, next8 = cu_ref[...], cp_ref[...], cn_ref[...]
    conv = sum(_shifted(cur, prev8, next8, j - SHORT_CONV // 2, first, last) * cw_ref[j:j + 1, :]
               for j in range(SHORT_CONV))
    x1 = x_ref[...] + _dot((gb_ref[...] * conv).astype(BF16), wo_ref[...])
    x1_ref[...] = x1
    _norm_and_route(x1, fg_ref, wr_hi_ref, wr_lo_ref, hn_ref, aff_ref)


def _route_outs(n, b, seq, tm):
    row = pl.BlockSpec((tm, D_MODEL), lambda i: (i, 0))
    aff = pl.BlockSpec((1, N_EXPERTS, tm), lambda i: (i // (seq // tm), 0, i % (seq // tm)))
    shapes = [jax.ShapeDtypeStruct((n, D_MODEL), F32), jax.ShapeDtypeStruct((n, D_MODEL), BF16),
              jax.ShapeDtypeStruct((b, N_EXPERTS, seq), F32)]
    return [row, row, aff], shapes


def _out_even(x, attn, y, wa, wy, fg, wr_hi, wr_lo, b, seq):
    n = x.shape[0]
    tm = ROW_TILE
    row = pl.BlockSpec((tm, D_MODEL), lambda i: (i, 0))
    out_specs, shapes = _route_outs(n, b, seq, tm)
    return pl.pallas_call(
        _out_even_kernel, grid=(n // tm,),
        in_specs=[row, row, row, _const_spec(wa.shape), _const_spec(wy.shape), _const_spec((1, D_MODEL)),
                  _const_spec(wr_hi.shape), _const_spec(wr_lo.shape)],
        out_specs=out_specs, out_shape=shapes, compiler_params=_cparams(1), name="out_even",
    )(x, attn, y, wa, wy, fg, wr_hi, wr_lo)


def _out_odd(x, gb, cu, cw, wo, fg, wr_hi, wr_lo, b, seq):
    n = x.shape[0]
    tm = ROW_TILE
    row = pl.BlockSpec((tm, D_MODEL), lambda i: (i, 0))
    prev = pl.BlockSpec((8, D_MODEL), lambda i: (jnp.maximum(i * (tm // 8) - 1, 0), 0))
    nxt = pl.BlockSpec((8, D_MODEL), lambda i: (jnp.minimum((i + 1) * (tm // 8), n // 8 - 1), 0))
    out_specs, shapes = _route_outs(n, b, seq, tm)
    return pl.pallas_call(
        functools.partial(_out_odd_kernel, tiles_per_seq=seq // tm), grid=(n // tm,),
        in_specs=[row, row, row, prev, nxt, pl.BlockSpec((8, D_MODEL), lambda i: (0, 0)), _const_spec(wo.shape),
                  _const_spec((1, D_MODEL)), _const_spec(wr_hi.shape), _const_spec(wr_lo.shape)],
        out_specs=out_specs, out_shape=shapes, compiler_params=_cparams(1), name="out_odd",
    )(x, gb, cu, cu, cu, cw, wo, fg, wr_hi, wr_lo)


def _in_odd_kernel(x_ref, g_ref, wb_ref, wc_ref, wu_ref, gb_ref, cu_ref):
    hb = _rms(x_ref[...], g_ref[...]).astype(BF16)
    gb_ref[...] = _dot(hb, wb_ref[...])
    cu_ref[...] = _dot(hb, wc_ref[...]) * _dot(hb, wu_ref[...])


def _in_odd(x, g, wb, wc, wu):
    n = x.shape[0]
    tm = ROW_TILE
    row = pl.BlockSpec((tm, D_MODEL), lambda i: (i, 0))
    return pl.pallas_call(
        _in_odd_kernel, grid=(n // tm,),
        in_specs=[row, _const_spec((1, D_MODEL)), _const_spec(wb.shape), _const_spec(wc.shape), _const_spec(wu.shape)],
        out_specs=[row, row], out_shape=[jax.ShapeDtypeStruct((n, D_MODEL), F32)] * 2,
        compiler_params=_cparams(1), name="in_odd",
    )(x, g, wb, wc, wu)


def _count(mask):
    return jnp.sum(jnp.sum(mask.astype(F32), axis=0, keepdims=True), axis=1, keepdims=True)


def _route_kernel(aff_ref, incl_ref, ones_ref, strict_ref, local_ref, group_ref, first_ref, slot_ref, off_ref, end_ref,
                  *, cap):
    def step(i, thrs):
        bit = jnp.int32(1) << (30 - i)
        out = []
        for e in range(N_EXPERTS):
            cand = thrs[e] | bit
            out.append(jnp.where(_count(pltpu.bitcast(aff_ref[0, e], I32) >= cand) >= cap, cand, thrs[e]))
        return tuple(out)

    thrs = lax.fori_loop(0, 31, step, tuple(jnp.zeros((1, 1), I32) for _ in range(N_EXPERTS)))
    for e in range(N_EXPERTS):
        bits = pltpu.bitcast(aff_ref[0, e], I32)
        thr = thrs[e]
        gt = bits > thr
        eq = (bits == thr).astype(BF16)
        eq_rank = _dot(eq, incl_ref[...]) + _dot(strict_ref[...], _dot(eq, ones_ref[...]).astype(BF16))
        sel = (gt | ((bits == thr) & (eq_rank <= cap - _count(gt)))).astype(BF16)
        within = _dot(sel, incl_ref[...])
        totals = _dot(sel, ones_ref[...]).astype(BF16)
        local = _dot(local_ref[...], totals)
        cnt = _dot(group_ref[...], totals)
        padded = jnp.floor((cnt + (SLOT_ALIGN - 1)) * (1.0 / SLOT_ALIGN)) * SLOT_ALIGN
        start = _dot(first_ref[...], padded.astype(BF16))
        slot_ref[0, e] = jnp.where(sel > 0, (start + local + within).astype(I32) - 1, -1)
        off_ref[0, e:e + 1, :] = start.T[0:1, :].astype(I32)
        end_ref[0, e:e + 1, :] = (start + padded).T[0:1, :].astype(I32)


def _route(aff, cap):
    b, _, seq = aff.shape
    nt = seq // LANE
    tri = np.arange(LANE)
    tt = np.arange(nt)
    grp = tt // SUB
    as_bf16 = lambda m: jnp.asarray(m, BF16)
    incl = as_bf16(tri[:, None] <= tri[None, :])
    strict = as_bf16(tt[None, :] < tt[:, None])
    local = as_bf16((tt[None, :] < tt[:, None]) & (grp[None, :] == grp[:, None]))
    group = as_bf16(grp[None, :] == grp[:, None])
    first = as_bf16((grp[None, :] < grp[:, None]) & (tt[None, :] % SUB == 0))
    ones = jnp.ones((LANE, LANE), BF16)
    tiles = pl.BlockSpec((1, N_EXPERTS, nt, LANE), lambda i: (i, 0, 0, 0))
    rows = pl.BlockSpec((1, N_EXPERTS, nt), lambda i: (i, 0, 0))
    return pl.pallas_call(
        functools.partial(_route_kernel, cap=cap), grid=(b,),
        in_specs=[tiles, _const_spec((LANE, LANE)), _const_spec((LANE, LANE))] + [_const_spec((nt, nt))] * 4,
        out_specs=[tiles, rows, rows],
        out_shape=[jax.ShapeDtypeStruct((b, N_EXPERTS, nt, LANE), I32)] + [jax.ShapeDtypeStruct((b, N_EXPERTS, nt), I32)] * 2,
        compiler_params=_cparams(1), name="route",
    )(aff.reshape(b, N_EXPERTS, nt, LANE), incl, ones, strict, local, group, first)


def _one_hot(slots, base):
    return (slots == lax.broadcasted_iota(I32, (WINDOW, slots.shape[1]), 0) + base).astype(BF16)


def _gather_kernel(off_ref, end_ref, slot_ref, aff_ref, hn_ref, zero_hbm, xe_hbm, stage, extra, sem, xsem):
    del zero_hbm
    bi, j = pl.program_id(0), pl.program_id(1)
    n_j = pl.num_programs(1)
    step = bi * n_j + j
    cur = step % 2

    def window_copy(bb, jj, e, buf):
        start = pl.multiple_of(off_ref[bb, e, jj * SUB], SLOT_ALIGN)
        return pltpu.make_async_copy(stage.at[buf, e], xe_hbm.at[bb, e, pl.ds(start, WINDOW), :], sem.at[buf, e])

    tokens = hn_ref[0]
    parts = [p.astype(F32) for p in _split3(aff_ref[0, 0])]
    gates = jnp.concatenate(parts + [jnp.zeros((LANE - 3 * N_EXPERTS, TOK_TILE), F32)], axis=0).astype(BF16)

    def rows_of(p):
        return jnp.concatenate([_dot(p, tokens), _dot_nt(p, gates)], axis=1).astype(BF16)

    p_all = jnp.concatenate([_one_hot(slot_ref[0, 0, e:e + 1, :], off_ref[bi, e, j * SUB])
                             for e in range(N_EXPERTS)], axis=0)
    stage[cur] = rows_of(p_all).reshape(N_EXPERTS, WINDOW, D_MODEL + LANE)

    @pl.when(step > 0)
    def _():
        prev = step - 1
        for e in range(N_EXPERTS):
            window_copy(prev // n_j, prev % n_j, e, 1 - cur).wait()

    for e in range(N_EXPERTS):
        window_copy(bi, j, e, cur).start()

    def overflow(e, c):
        first = off_ref[bi, e, j * SUB]
        n_win = (end_ref[bi, e, j * SUB] - first + WINDOW - 1) // WINDOW
        slots = slot_ref[0, 0, pl.ds(e, 1), :]

        def one(w, c2):
            base = pl.multiple_of(first + w * WINDOW, SLOT_ALIGN)
            extra[...] = rows_of(_one_hot(slots, base))
            cp = pltpu.make_async_copy(extra, xe_hbm.at[bi, e, pl.ds(base, WINDOW), :], xsem.at[0])
            cp.start()
            cp.wait()
            return c2

        return lax.fori_loop(1, n_win, one, c)

    lax.fori_loop(0, N_EXPERTS, overflow, 0)

    @pl.when(step == pl.num_programs(0) * n_j - 1)
    def _():
        for e in range(N_EXPERTS):
            window_copy(bi, j, e, cur).wait()


def _gather(off, end, slot_t, aff_t, hn):
    b, seq, _ = hn.shape
    cap_pad = _cap_pad(seq)
    width = D_MODEL + LANE
    per_tile = pl.BlockSpec((1, 1, N_EXPERTS, TOK_TILE), lambda bi, j, *_: (bi, j, 0, 0))
    return pl.pallas_call(
        _gather_kernel,
        grid_spec=pltpu.PrefetchScalarGridSpec(
            num_scalar_prefetch=2, grid=(b, seq // TOK_TILE),
            in_specs=[per_tile, per_tile, pl.BlockSpec((1, TOK_TILE, D_MODEL), lambda bi, j, *_: (bi, j, 0)),
                      pl.BlockSpec(memory_space=pl.ANY)],
            out_specs=pl.BlockSpec(memory_space=pl.ANY),
            scratch_shapes=[pltpu.VMEM((2, N_EXPERTS, WINDOW, width), BF16), pltpu.VMEM((WINDOW, width), BF16),
                            pltpu.SemaphoreType.DMA((2, N_EXPERTS)), pltpu.SemaphoreType.DMA((1,))]),
        out_shape=jax.ShapeDtypeStruct((b, N_EXPERTS, cap_pad, width), BF16),
        input_output_aliases={5: 0},
        compiler_params=_cparams(2), name="moe_gather",
    )(off, end, slot_t, aff_t, hn, jnp.zeros((b, N_EXPERTS, cap_pad, width), BF16))


def _ffn_kernel(end_ref, xe_ref, wg_hbm, wu_hbm, wd_hbm, y_ref, stage_g, stage_u, stage_d, wg, wu, wd, sem, *, layer):
    e, bi, r = pl.program_id(0), pl.program_id(1), pl.program_id(2)
    used = end_ref[bi, e, end_ref.shape[2] - 1]
    pairs = ((wg_hbm, stage_g, wg), (wu_hbm, stage_u, wu), (wd_hbm, stage_d, wd))

    def weight_copies(ee):
        return [pltpu.make_async_copy(src.at[layer, ee], stg, sem.at[k]) for k, (src, stg, _) in enumerate(pairs)]

    @pl.when((bi == 0) & (r == 0))
    def _():
        @pl.when(e == 0)
        def _():
            for cp in weight_copies(e):
                cp.start()

        for cp, (_, stg, dst) in zip(weight_copies(e), pairs):
            cp.wait()
            n_rows = stg.shape[0]

            def cast(i, c, stg=stg, dst=dst):
                rows = pl.ds(pl.multiple_of(i * CAST_ROWS, CAST_ROWS), CAST_ROWS)
                dst[rows, :] = stg[rows, :].astype(BF16)
                return c

            lax.fori_loop(0, n_rows // CAST_ROWS, cast, 0)

        @pl.when(e + 1 < pl.num_programs(0))
        def _():
            for cp in weight_copies(e + 1):
                cp.start()

    @pl.when(r * FFN_ROWS < used)
    def _():
        xe = xe_ref[0, 0, :, 0:D_MODEL]
        hid = (_silu(_dot(xe, wg[...])) * _dot(xe, wu[...])).astype(BF16)
        g = xe_ref[0, 0, :, D_MODEL:].astype(F32)
        lane = lax.broadcasted_iota(I32, g.shape, 1)
        mine = (lane % N_EXPERTS == e) & (lane < 3 * N_EXPERTS)
        gate = jnp.sum(jnp.where(mine, g, 0.0), axis=1, keepdims=True)
        y_ref[0, 0] = (_dot(hid, wd[...]) * gate).astype(BF16)

    @pl.when(r * FFN_ROWS >= used)
    def _():
        y_ref[...] = jnp.zeros_like(y_ref)


def _ffn(end, xe, wg, wu, wd, layer):
    b, ne, cap_pad, width = xe.shape
    rows = lambda w: pl.BlockSpec((1, 1, FFN_ROWS, w), lambda e, bi, r, *_: (bi, e, r, 0))
    hbm = pl.BlockSpec(memory_space=pl.ANY)
    return pl.pallas_call(
        functools.partial(_ffn_kernel, layer=layer),
        grid_spec=pltpu.PrefetchScalarGridSpec(
            num_scalar_prefetch=1, grid=(ne, b, cap_pad // FFN_ROWS),
            in_specs=[rows(width), hbm, hbm, hbm],
            out_specs=rows(D_MODEL),
            scratch_shapes=[pltpu.VMEM((D_MODEL, D_FF), F32), pltpu.VMEM((D_MODEL, D_FF), F32),
                            pltpu.VMEM((D_FF, D_MODEL), F32), pltpu.VMEM((D_MODEL, D_FF), BF16),
                            pltpu.VMEM((D_MODEL, D_FF), BF16), pltpu.VMEM((D_FF, D_MODEL), BF16),
                            pltpu.SemaphoreType.DMA((3,))]),
        out_shape=jax.ShapeDtypeStruct((b, ne, cap_pad, D_MODEL), BF16),
        compiler_params=_cparams(3), name="moe_ffn",
    )(end, xe, wg, wu, wd)


def _combine_kernel(off_ref, end_ref, slot_ref, y_hbm, x1_ref, o_ref, win, extra, sem, xsem):
    bi, j = pl.program_id(0), pl.program_id(1)
    n_j = pl.num_programs(1)
    step = bi * n_j + j
    cur = step % 2

    def window_copy(bb, jj, e, buf):
        start = pl.multiple_of(off_ref[bb, e, jj * SUB], SLOT_ALIGN)
        return pltpu.make_async_copy(y_hbm.at[bb, e, pl.ds(start, WINDOW), :], win.at[buf, e], sem.at[buf, e])

    @pl.when(step == 0)
    def _():
        for e in range(N_EXPERTS):
            window_copy(bi, j, e, cur).start()

    @pl.when(step + 1 < pl.num_programs(0) * n_j)
    def _():
        nxt = step + 1
        for e in range(N_EXPERTS):
            window_copy(nxt // n_j, nxt % n_j, e, 1 - cur).start()

    ps = []
    for e in range(N_EXPERTS):
        window_copy(bi, j, e, cur).wait()
        ps.append(_one_hot(slot_ref[0, 0, e:e + 1, :], off_ref[bi, e, j * SUB]))
    p_all = jnp.concatenate(ps, axis=0)
    y_all = win[cur].reshape(N_EXPERTS * WINDOW, D_MODEL)
    o_ref[0] = x1_ref[0] + _dot_tn(p_all, y_all)

    def overflow(e, c):
        first = off_ref[bi, e, j * SUB]
        n_win = (end_ref[bi, e, j * SUB] - first + WINDOW - 1) // WINDOW
        slots = slot_ref[0, 0, pl.ds(e, 1), :]

        def one(w, c2):
            base = pl.multiple_of(first + w * WINDOW, SLOT_ALIGN)
            cp = pltpu.make_async_copy(y_hbm.at[bi, e, pl.ds(base, WINDOW), :], extra, xsem.at[0])
            cp.start()
            cp.wait()
            o_ref[0] += _dot_tn(_one_hot(slots, base), extra[...])
            return c2

        return lax.fori_loop(1, n_win, one, c)

    lax.fori_loop(0, N_EXPERTS, overflow, 0)


def _combine(off, end, slot_t, y, x1):
    b, seq, _ = x1.shape
    tile = pl.BlockSpec((1, TOK_TILE, D_MODEL), lambda bi, j, *_: (bi, j, 0))
    return pl.pallas_call(
        _combine_kernel,
        grid_spec=pltpu.PrefetchScalarGridSpec(
            num_scalar_prefetch=2, grid=(b, seq // TOK_TILE),
            in_specs=[pl.BlockSpec((1, 1, N_EXPERTS, TOK_TILE), lambda bi, j, *_: (bi, j, 0, 0)),
                      pl.BlockSpec(memory_space=pl.ANY), tile],
            out_specs=tile,
            scratch_shapes=[pltpu.VMEM((2, N_EXPERTS, WINDOW, D_MODEL), BF16), pltpu.VMEM((WINDOW, D_MODEL), BF16),
                            pltpu.SemaphoreType.DMA((2, N_EXPERTS)), pltpu.SemaphoreType.DMA((1,))]),
        out_shape=jax.ShapeDtypeStruct((b, seq, D_MODEL), F32),
        compiler_params=_cparams(2), name="moe_combine",
    )(off, end, slot_t, y, x1)


def _cap_pad(seq):
    cap = CAPACITY_FACTOR * seq // N_EXPERTS
    worst = cap + (seq // TOK_TILE) * (SLOT_ALIGN - 1) + WINDOW
    return -(-worst // FFN_ROWS) * FFN_ROWS


def _moe(x1, hn, aff, wg, wu, wd, layer):
    b, seq, _ = x1.shape
    cap = CAPACITY_FACTOR * seq // N_EXPERTS
    slot, off, end = _route(aff, cap)
    per_tile = lambda a: jnp.swapaxes(a.reshape(b, N_EXPERTS, seq // TOK_TILE, TOK_TILE), 1, 2)
    slot_t = per_tile(slot)
    xe = _gather(off, end, slot_t, per_tile(aff), hn)
    y = _ffn(end, xe, wg, wu, wd, layer)
    return _combine(off, end, slot_t, y, x1)


def _rope_tables(seq):
    inv_freq = ROPE_THETA ** (-jnp.arange(ROPE_HALF, dtype=F32) * 2.0 / (2 * ROPE_HALF))
    ang = jnp.arange(seq, dtype=F32)[:, None] * inv_freq[None, :]
    cos, sin = jnp.cos(ang), jnp.sin(ang)
    z = lambda w: jnp.zeros((seq, w), F32)
    rest = HEAD_DIM - 2 * ROPE_HALF
    rc = jnp.concatenate([cos, cos, jnp.ones((seq, rest), F32)], axis=1)
    r1 = jnp.concatenate([-sin, z(ROPE_HALF + rest)], axis=1)
    r2 = jnp.concatenate([z(ROPE_HALF), sin, z(rest)], axis=1)
    return tuple(jnp.tile(t, (1, LANE // HEAD_DIM)) for t in (rc, r1, r2))


def _router_split(w):
    wp = jnp.pad(w, ((0, 0), (0, LANE - N_EXPERTS)))
    hi = wp.astype(BF16)
    return hi, (wp - hi.astype(F32)).astype(BF16)


def kernel(x, attn_norm, w_in_even, q_norm, k_norm, ssd_conv_w, ssd_conv_b, ssd_a_log_fwd, ssd_a_log_bwd,
           ssd_dt_bias_fwd, ssd_dt_bias_bwd, ssd_d, ssd_out_norm, w_out_even, conv_norm, conv_w_in, conv_w,
           conv_w_out, ffn_norm, router_w, expert_w_gate, expert_w_up, expert_w_down):
    b, seq, _ = x.shape
    n = b * seq
    depth = ffn_norm.shape[0]
    rc, r1, r2 = _rope_tables(seq)
    blk = np.arange(256) // HEAD_DIM
    bd = jnp.asarray((blk[:, None] == blk[None, :]) / HEAD_DIM, BF16)
    row = lambda v: v.reshape(1, -1).astype(F32)

    xf = x.reshape(n, D_MODEL)
    for layer in range(depth):
        i = layer // 2
        wr_hi, wr_lo = _router_split(router_w[layer])
        fg = row(ffn_norm[layer])
        if layer % 2 == 0:
            w = w_in_even[i].astype(BF16)
            o = np.cumsum([0, D_MODEL, D_MODEL, D_MODEL, D_MODEL, SSD_XBC, N_HEADS, N_HEADS])
            wq, wk, wv, wz, wx = (w[:, o[j]:o[j + 1]] for j in range(5))
            wd = jnp.pad(w[:, o[5]:o[7]], ((0, 0), (0, LANE - 2 * N_HEADS)))
            tile_heads = lambda g: row(jnp.tile(g, N_HEADS))
            q, k, v, z, xbc, dt = _in_even(xf, row(attn_norm[i]), wq, wk, wv, wz, wx, wd,
                                           tile_heads(q_norm[i]), tile_heads(k_norm[i]), bd, rc, r1, r2, seq)
            as3 = lambda t: t.reshape(b, seq, -1)
            attn = _attention(as3(q), as3(k), as3(v))
            cw = jnp.pad(ssd_conv_w[i], ((0, 8 - SSD_CONV), (0, 0)))
            act = _ssd_conv(as3(xbc), cw, row(ssd_conv_b[i]))
            pc = jnp.pad(jnp.stack([ssd_dt_bias_fwd[i], ssd_dt_bias_bwd[i], -jnp.exp(ssd_a_log_fwd[i]),
                                    -jnp.exp(ssd_a_log_bwd[i])], axis=1).astype(F32), ((0, 0), (0, LANE - 4)))
            y = _ssd(act, as3(dt), as3(z), pc, row(jnp.repeat(ssd_d[i], HEAD_DIM)), row(ssd_out_norm[i]))
            wo = w_out_even[i].astype(BF16)
            x1, hn, aff = _out_even(xf, attn.reshape(n, D_MODEL), y.reshape(n, D_MODEL), wo[:D_MODEL], wo[D_MODEL:],
                                    fg, wr_hi, wr_lo, b, seq)
        else:
            w = conv_w_in[i].astype(BF16)
            gb, cu = _in_odd(xf, row(conv_norm[i]), w[:, :D_MODEL], w[:, D_MODEL:2 * D_MODEL], w[:, 2 * D_MODEL:])
            cw = jnp.pad(conv_w[i], ((0, 8 - SHORT_CONV), (0, 0)))
            x1, hn, aff = _out_odd(xf, gb, cu, cw, conv_w_out[i].astype(BF16), fg, wr_hi, wr_lo, b, seq)
        xf = _moe(x1.reshape(b, seq, D_MODEL), hn.reshape(b, seq, D_MODEL), aff, expert_w_gate, expert_w_up,
                  expert_w_down, layer).reshape(n, D_MODEL)
    return xf.reshape(b, seq, D_MODEL)
```

```python
import functools
import math

import jax
import jax.numpy as jnp
import numpy as np
from jax import lax
from jax.experimental import pallas as pl
from jax.experimental.pallas import tpu as pltpu

F32, BF16, I32 = jnp.float32, jnp.bfloat16, jnp.int32

D_MODEL = 1024
N_HEADS = 16
HEAD_DIM = 64
ROPE_HALF = 8
ROPE_THETA = 500000.0
PATTERNS = ((128, 1), (512, 4), (2048, 16))
HALF_STEPS = 64
SSD_GROUPS = 2
SSD_STATE = 128
SSD_XBC = 1536
SSD_CONV = 5
CHUNK = 128
SSD_CPS = 2
N_EXPERTS = 16
CAPACITY_FACTOR = 2
D_FF = 2048
SHORT_CONV = 3
EPS = 1e-6

LANE = 128
VMEM_LIMIT = 56 * 1024 * 1024

ROW_TILE = 512
ATTN_SUPER = 2048
ATTN_HALO = 1024
ATTN_TQ = 128
ATTN_TK = ATTN_TQ + 2 * HALF_STEPS
ATTN_UNROLL = 8
TOK_TILE = 512
SUB = TOK_TILE // LANE
SLOT_ALIGN = 16
WINDOW = 96
FFN_ROWS = 256
CAST_ROWS = 64


def _cparams(n_axes):
    return pltpu.CompilerParams(dimension_semantics=("arbitrary",) * n_axes, vmem_limit_bytes=VMEM_LIMIT)


def _const_spec(shape):
    nd = len(shape)
    return pl.BlockSpec(shape, lambda *_: (0,) * nd, pipeline_mode=pl.Buffered(1))


def _dot(a, b):
    return jnp.dot(a, b, preferred_element_type=F32)


def _dot_nt(a, b):
    return lax.dot_general(a, b, (((1,), (1,)), ((), ())), preferred_element_type=F32)


def _dot_tn(a, b):
    return lax.dot_general(a, b, (((0,), (0,)), ((), ())), preferred_element_type=F32)


def _split2(x):
    hi = x.astype(BF16)
    lo = (x - hi.astype(F32)).astype(BF16)
    return hi, lo


def _split3(x):
    hi = x.astype(BF16)
    r = x - hi.astype(F32)
    mid = r.astype(BF16)
    lo = (r - mid.astype(F32)).astype(BF16)
    return hi, mid, lo


def _dot3(x, m_bf16):
    hi, mid, lo = _split3(x)
    return _dot(hi, m_bf16) + _dot(mid, m_bf16) + _dot(lo, m_bf16)


def _rms(x, g):
    return x * lax.rsqrt(jnp.mean(x * x, axis=-1, keepdims=True) + EPS) * g


def _silu(x):
    return x * jax.nn.sigmoid(x)


def _in_even_kernel(x_ref, g_ref, wq_ref, wk_ref, wv_ref, wz_ref, wx_ref, wd_ref, qg_ref, kg_ref, bd_ref,
                    rc_ref, r1_ref, r2_ref, q_ref, k_ref, v_ref, z_ref, xbc_ref, dt_ref):
    hb = _rms(x_ref[...], g_ref[...]).astype(BF16)
    bd = bd_ref[...]
    rc, r1, r2 = rc_ref[...], r1_ref[...], r2_ref[...]

    def head_norm_rope(w_ref, gain_ref, out_ref):
        t = _dot(hb, w_ref[...])
        for c in range(D_MODEL // 256):
            tc = t[:, c * 256:(c + 1) * 256]
            sq_hi, sq_lo = _split2(tc * tc)
            ms = _dot(sq_hi, bd) + _dot(sq_lo, bd)
            tn = tc * lax.rsqrt(ms + EPS) * gain_ref[:, c * 256:(c + 1) * 256]
            for hh in range(2):
                u = tn[:, hh * LANE:(hh + 1) * LANE]
                r = u * rc + pltpu.roll(u, LANE - ROPE_HALF, 1) * r1 + pltpu.roll(u, ROPE_HALF, 1) * r2
                out_ref[:, c * 256 + hh * LANE:c * 256 + (hh + 1) * LANE] = r

    head_norm_rope(wq_ref, qg_ref, q_ref)
    head_norm_rope(wk_ref, kg_ref, k_ref)
    v_ref[...] = _dot(hb, wv_ref[...])
    z_ref[...] = _dot(hb, wz_ref[...])
    xbc_ref[...] = _dot(hb, wx_ref[...])
    dt_ref[...] = _dot(hb, wd_ref[...])


def _in_even(x, g, wq, wk, wv, wz, wx, wd, qg, kg, bd, rc, r1, r2, seq):
    n = x.shape[0]
    tm = ROW_TILE
    row = lambda w: pl.BlockSpec((tm, w), lambda i: (i, 0))
    tab = pl.BlockSpec((tm, LANE), lambda i: (i % (seq // tm), 0))
    outs = [jax.ShapeDtypeStruct((n, D_MODEL), F32)] * 4 + [jax.ShapeDtypeStruct((n, SSD_XBC), F32),
                                                           jax.ShapeDtypeStruct((n, LANE), F32)]
    return pl.pallas_call(
        _in_even_kernel, grid=(n // tm,),
        in_specs=[row(D_MODEL), _const_spec((1, D_MODEL)), _const_spec(wq.shape), _const_spec(wk.shape),
                  _const_spec(wv.shape), _const_spec(wz.shape), _const_spec(wx.shape), _const_spec(wd.shape),
                  _const_spec((1, D_MODEL)), _const_spec((1, D_MODEL)), _const_spec((256, 256)), tab, tab, tab],
        out_specs=[row(D_MODEL)] * 4 + [row(SSD_XBC), row(LANE)], out_shape=outs,
        compiler_params=_cparams(1), name="in_even",
    )(x, g, wq, wk, wv, wz, wx, wd, qg, kg, bd, rc, r1, r2)


def _attn_kernel(q_ref, k0, k1, k2, k3, v0, v1, v2, v3, qw_hbm, kw_hbm, vw_hbm, o_ref, qf, kf, vf, q16, k16, v16,
                 acc, mst, lst, a16, m16, l16, an3, mn3, ln3, sem, rsem, *, seq):
    dmax = PATTERNS[-1][1]
    bi, hp, j = pl.program_id(0), pl.program_id(1), pl.program_id(2)
    n_hp, n_j = pl.num_programs(1), pl.num_programs(2)
    p0 = j * ATTN_SUPER
    halo = ATTN_HALO // dmax
    step = (bi * n_hp + hp) * n_j + j
    cur = step % 2

    def residue_copies(at, buf, r, where):
        b_, hp_, j_ = at
        lanes = pl.ds(pl.multiple_of(hp_ * LANE, LANE), LANE)
        q0 = pl.multiple_of(j_ * ATTN_TQ, ATTN_TQ)
        n_kv = ATTN_TK - halo if where else ATTN_TK
        src0 = 0 if where < 0 else q0 - halo
        dst0 = halo if where < 0 else 0
        cps = [pltpu.make_async_copy(qw_hbm.at[b_, pl.ds(q0, ATTN_TQ), r, lanes], q16.at[buf, r], rsem.at[buf, 0, r])]
        for n, (src, dst) in enumerate(((kw_hbm, k16), (vw_hbm, v16))):
            cps.append(pltpu.make_async_copy(src.at[b_, pl.ds(src0, n_kv), r, lanes],
                                             dst.at[buf, r, pl.ds(dst0, n_kv), :], rsem.at[buf, 1 + n, r]))
        return cps

    def all_residues(at, buf, action):
        def run(where):
            for r in range(dmax):
                for cp in residue_copies(at, buf, r, where):
                    getattr(cp, action)()
        j_ = at[2]
        pl.when(j_ == 0)(functools.partial(run, -1))
        pl.when((j_ > 0) & (j_ < n_j - 1))(functools.partial(run, 0))
        pl.when(j_ == n_j - 1)(functools.partial(run, 1))

    def state_copies(r):
        return [pltpu.make_async_copy(src.at[r], dst.at[:, r, :], sem.at[n, r])
                for n, (src, dst) in enumerate(((a16, an3), (m16, mn3), (l16, ln3)))]

    @pl.when(step == 0)
    def _():
        k16[...] = jnp.zeros_like(k16)
        v16[...] = jnp.zeros_like(v16)
        all_residues((bi, hp, j), cur, "start")

    @pl.when(step + 1 < pl.num_programs(0) * n_hp * n_j)
    def _():
        nxt = step + 1
        all_residues((nxt // (n_hp * n_j), (nxt // n_j) % n_hp, nxt % n_j), 1 - cur, "start")

    qf[...] = q_ref[0] * (HEAD_DIM ** -0.5)
    for i, (kr, vr) in enumerate(((k0, v0), (k1, v1), (k2, v2), (k3, v3))):
        kf[i * ATTN_HALO:(i + 1) * ATTN_HALO, :] = kr[0]
        vf[i * ATTN_HALO:(i + 1) * ATTN_HALO, :] = vr[0]

    head_a = lax.broadcasted_iota(I32, (ATTN_TQ, LANE), 1) < HEAD_DIM
    off = lax.broadcasted_iota(I32, (ATTN_TQ, ATTN_TK), 1) - lax.broadcasted_iota(I32, (ATTN_TQ, ATTN_TK), 0)
    band = (off >= 0) & (off <= 2 * HALF_STEPS)
    band2 = jnp.concatenate([band, band], axis=0)
    colpos = lax.broadcasted_iota(I32, (1, ATTN_TK), 1)

    def local_softmax(q, kt, vt, first_pos, d):
        pos = first_pos + d * colpos
        valid = band2 & (pos >= 0) & (pos < seq)
        q2 = jnp.concatenate([jnp.where(head_a, q, 0.0), jnp.where(head_a, 0.0, q)], axis=0).astype(BF16)
        s = jnp.where(valid, _dot_nt(q2, kt.astype(BF16)), -jnp.inf)
        m = jnp.max(s, axis=1, keepdims=True)
        p = jnp.exp(s - m)
        l = jnp.sum(p, axis=1, keepdims=True)
        n = _dot(p.astype(BF16), vt.astype(BF16))
        return (jnp.where(head_a, m[:ATTN_TQ], m[ATTN_TQ:]), jnp.where(head_a, l[:ATTN_TQ], l[ATTN_TQ:]),
                jnp.where(head_a, n[:ATTN_TQ], n[ATTN_TQ:]))

    def visit(d, i, first):
        log_d = d.bit_length() - 1
        r = i & (d - 1)
        t = i >> log_d
        qs = r + d * ATTN_TQ * t
        ks = ATTN_HALO + r + d * (ATTN_TQ * t - HALF_STEPS)
        m_loc, l_loc, n_loc = local_softmax(qf[pl.ds(qs, ATTN_TQ, stride=d), :], kf[pl.ds(ks, ATTN_TK, stride=d), :],
                                            vf[pl.ds(ks, ATTN_TK, stride=d), :], p0 - ATTN_HALO + ks, d)
        rows = pl.ds(qs, ATTN_TQ, stride=d)
        if first:
            mst[rows, :], lst[rows, :], acc[rows, :] = m_loc, l_loc, n_loc
        else:
            m_old = mst[rows, :]
            m_new = jnp.maximum(m_old, m_loc)
            w_old = jnp.exp(m_old - m_new)
            w_loc = jnp.exp(m_loc - m_new)
            mst[rows, :] = m_new
            lst[rows, :] = lst[rows, :] * w_old + l_loc * w_loc
            acc[rows, :] = acc[rows, :] * w_old + n_loc * w_loc

    def visit_dmax(r):
        q = q16[cur, r] * (HEAD_DIM ** -0.5)
        m16[r], l16[r], a16[r] = local_softmax(q, k16[cur, r], v16[cur, r], p0 - ATTN_HALO + r, dmax)

    def loop(n, fn):
        def body(i, c):
            for u in range(ATTN_UNROLL):
                fn(i * ATTN_UNROLL + u)
            return c
        lax.fori_loop(0, n // ATTN_UNROLL, body, 0)

    n_visits = ATTN_SUPER // ATTN_TQ
    loop(n_visits, lambda i: visit(PATTERNS[0][1], i, True))
    all_residues((bi, hp, j), cur, "wait")
    loop(dmax, visit_dmax)
    for r in range(dmax):
        for cp in state_copies(r):
            cp.start()
    for _, d in PATTERNS[1:-1]:
        loop(n_visits, lambda i, d=d: visit(d, i, False))
    for r in range(dmax):
        for cp in state_copies(r):
            cp.wait()

    def finish(i, c):
        rows = pl.ds(pl.multiple_of(i * ATTN_TQ, ATTN_TQ), ATTN_TQ)
        slabs = pl.ds(pl.multiple_of(i * (ATTN_TQ // dmax), ATTN_TQ // dmax), ATTN_TQ // dmax)
        a_a, m_a, l_a = (t[slabs].reshape(ATTN_TQ, LANE) for t in (an3, mn3, ln3))
        m_b = mst[rows, :]
        m = jnp.maximum(m_a, m_b)
        w_a, w_b = jnp.exp(m_a - m), jnp.exp(m_b - m)
        o = (a_a * w_a + acc[rows, :] * w_b) / (l_a * w_a + lst[rows, :] * w_b)
        o_ref[0, rows, :] = o.astype(BF16)
        return c

    lax.fori_loop(0, n_visits, finish, 0)


def _attention(q, k, v):
    b, seq, _ = q.shape
    nblk = seq // ATTN_HALO
    ratio = ATTN_SUPER // ATTN_HALO
    dmax = PATTERNS[-1][1]
    assert ATTN_SUPER // dmax == ATTN_TQ and 4 * ATTN_HALO // dmax == ATTN_TK
    assert seq // ATTN_SUPER >= 2

    def halo(i):
        return pl.BlockSpec((1, ATTN_HALO, LANE),
                            lambda bi, hp, j: (bi, jnp.clip(ratio * j - 1 + i, 0, nblk - 1), hp))

    main = pl.BlockSpec((1, ATTN_SUPER, LANE), lambda bi, hp, j: (bi, j, hp))
    hbm = pl.BlockSpec(memory_space=pl.ANY)
    by_residue = lambda t: t.reshape(b, seq // dmax, dmax, D_MODEL)
    return pl.pallas_call(
        functools.partial(_attn_kernel, seq=seq),
        grid=(b, D_MODEL // LANE, seq // ATTN_SUPER),
        in_specs=[main] + [halo(i) for i in range(4)] * 2 + [hbm] * 3,
        out_specs=main,
        out_shape=jax.ShapeDtypeStruct((b, seq, D_MODEL), BF16),
        scratch_shapes=[pltpu.VMEM((ATTN_SUPER, LANE), F32), pltpu.VMEM((4 * ATTN_HALO, LANE), F32),
                        pltpu.VMEM((4 * ATTN_HALO, LANE), F32), pltpu.VMEM((2, dmax, ATTN_TQ, LANE), F32),
                        pltpu.VMEM((2, dmax, ATTN_TK, LANE), F32), pltpu.VMEM((2, dmax, ATTN_TK, LANE), F32)]
                       + [pltpu.VMEM((ATTN_SUPER, LANE), F32)] * 3
                       + [pltpu.VMEM((dmax, ATTN_TQ, LANE), F32)] * 3
                       + [pltpu.VMEM((ATTN_TQ, dmax, LANE), F32)] * 3
                       + [pltpu.SemaphoreType.DMA((3, dmax)), pltpu.SemaphoreType.DMA((2, 3, dmax))],
        compiler_params=_cparams(3), name="dilated_attn",
    )(q, k, k, k, k, v, v, v, v, by_residue(q), by_residue(k), by_residue(v))


def _shifted(cur, prev8, next8, s, first, last):
    n = cur.shape[0]
    if s == 0:
        return cur
    rows = lax.broadcasted_iota(I32, cur.shape, 0)
    out = pltpu.roll(cur, (-s) % n, 0)
    if s < 0:
        for j in range(-s):
            src = jnp.where(first, 0.0, prev8[8 + s + j:8 + s + j + 1, :])
            out = jnp.where(rows == j, src, out)
    else:
        for j in range(s):
            src = jnp.where(last, 0.0, next8[j:j + 1, :])
            out = jnp.where(rows == n - s + j, src, out)
    return out


def _ssd_conv_kernel(c_ref, p_ref, n_ref, w_ref, b_ref, o_ref):
    first = pl.program_id(1) == 0
    last = pl.program_id(1) == pl.num_programs(1) - 1
    for c in range(SSD_XBC // 256):
        sl = slice(c * 256, (c + 1) * 256)
        cur, prev8, next8 = c_ref[0, :, sl], p_ref[0, :, sl], n_ref[0, :, sl]
        y = b_ref[:, sl]
        for j in range(SSD_CONV):
            y = y + _shifted(cur, prev8, next8, j - SSD_CONV // 2, first, last) * w_ref[j:j + 1, sl]
        o_ref[0, :, sl] = _silu(y)


def _halo_specs(tm, width, seq):
    cur = pl.BlockSpec((1, tm, width), lambda b, i: (b, i, 0))
    prev = pl.BlockSpec((1, 8, width), lambda b, i: (b, jnp.maximum(i * (tm // 8) - 1, 0), 0))
    nxt = pl.BlockSpec((1, 8, width), lambda b, i: (b, jnp.minimum((i + 1) * (tm // 8), seq // 8 - 1), 0))
    return cur, prev, nxt


def _ssd_conv(xbc, w, bias):
    b, seq, width = xbc.shape
    tm = 512
    cur, prev, nxt = _halo_specs(tm, width, seq)
    return pl.pallas_call(
        _ssd_conv_kernel, grid=(b, seq // tm),
        in_specs=[cur, prev, nxt, pl.BlockSpec((8, width), lambda b, i: (0, 0)),
                  pl.BlockSpec((1, width), lambda b, i: (0, 0))],
        out_specs=cur, out_shape=jax.ShapeDtypeStruct(xbc.shape, F32),
        compiler_params=_cparams(2), name="ssd_conv",
    )(xbc, xbc, xbc, w, bias)


def _softplus(x):
    return jnp.maximum(x, 0.0) + jnp.log1p(jnp.exp(-jnp.abs(x)))


def _tri(kind):
    s = lax.broadcasted_iota(I32, (CHUNK, CHUNK), 0)
    l = lax.broadcasted_iota(I32, (CHUNK, CHUNK), 1)
    return {"le": s <= l, "ge": s >= l, "lt": s < l}[kind]


def _expand(cols, e2_ref):
    hi, lo = _split2(cols)
    return _dot(jnp.concatenate([hi, lo], axis=1), e2_ref[...])


def _ssd_bwd_kernel(xs_ref, b_ref, dt_ref, pc_ref, e2_ref, sb_ref, st):
    @pl.when(pl.program_id(1) == 0)
    def _():
        st[...] = jnp.zeros_like(st)

    for c in reversed(range(SSD_CPS)):
        _ssd_bwd_chunk(slice(c * CHUNK, (c + 1) * CHUNK), c, xs_ref, b_ref, dt_ref, pc_ref, e2_ref, sb_ref, st)


def _ssd_bwd_chunk(rs, c, xs_ref, b_ref, dt_ref, pc_ref, e2_ref, sb_ref, st):
    sb_ref[0, c] = st[...].astype(BF16)
    dt_t = dt_ref[0, rs, :].T
    dtb = _softplus(dt_t[N_HEADS:2 * N_HEADS, :] + pc_ref[:, 1:2])
    a = dtb * pc_ref[:, 3:4]
    ex = _dot3(a, _tri("lt").astype(BF16))
    tot = ex[:, CHUNK - 1:CHUNK] + a[:, CHUNK - 1:CHUNK]
    rowform = jnp.concatenate([dtb * jnp.exp(ex), jnp.broadcast_to(jnp.exp(tot), (N_HEADS, CHUNK)),
                               jnp.zeros((CHUNK - 2 * N_HEADS, CHUNK), F32)], axis=0)
    ex2 = _expand(rowform.T, e2_ref)
    xw = (xs_ref[0, rs, :] * ex2[:, :D_MODEL]).astype(BF16)
    half = D_MODEL // SSD_GROUPS
    upd = [_dot(b_ref[0, rs, g * SSD_STATE:(g + 1) * SSD_STATE].T.astype(BF16), xw[:, g * half:(g + 1) * half])
           for g in range(SSD_GROUPS)]
    st[...] = st[...] * ex2[0:1, D_MODEL:] + jnp.concatenate(upd, axis=1)


def _ssd_fwd_kernel(xs_ref, b_ref, c_ref, dt_ref, z_ref, sb_ref, pc_ref, e3_ref, dexp_ref, on_ref, o_ref, st):
    @pl.when(pl.program_id(1) == 0)
    def _():
        st[...] = jnp.zeros_like(st)

    for c in range(SSD_CPS):
        _ssd_fwd_chunk(slice(c * CHUNK, (c + 1) * CHUNK), c, xs_ref, b_ref, c_ref, dt_ref, z_ref, sb_ref, pc_ref,
                       e3_ref, dexp_ref, on_ref, o_ref, st)


def _ssd_fwd_chunk(rs, c, xs_ref, b_ref, c_ref, dt_ref, z_ref, sb_ref, pc_ref, e3_ref, dexp_ref, on_ref, o_ref, st):
    xs = xs_ref[0, rs, :]
    dt_t = dt_ref[0, rs, :].T
    dtf = _softplus(dt_t[0:N_HEADS, :] + pc_ref[:, 0:1])
    dtb = _softplus(dt_t[N_HEADS:2 * N_HEADS, :] + pc_ref[:, 1:2])
    af = dtf * pc_ref[:, 2:3]
    ab = dtb * pc_ref[:, 3:4]
    csf = _dot3(af, _tri("le").astype(BF16))
    rcs = _dot3(ab, _tri("ge").astype(BF16))
    totf = csf[:, CHUNK - 1:CHUNK]
    rowform = jnp.concatenate([dtf * jnp.exp(totf - csf), jnp.exp(csf), jnp.exp(rcs), csf, rcs,
                               jnp.zeros((CHUNK - 5 * N_HEADS, CHUNK), F32)], axis=0)
    cols = rowform.T
    ex3 = _expand(cols, e3_ref)
    w_state, e_f, e_b = ex3[:, :D_MODEL], ex3[:, D_MODEL:2 * D_MODEL], ex3[:, 2 * D_MODEL:]

    xb = xs.astype(BF16)
    lower, upper = _tri("ge"), _tri("le")
    head_a = lax.broadcasted_iota(I32, (CHUNK, LANE), 1) < HEAD_DIM
    half = D_MODEL // SSD_GROUPS
    hpg = N_HEADS // SSD_GROUPS
    st_all = st[...]
    sb_all = sb_ref[0, c]
    ys = []
    b_t = []
    for g in range(SSD_GROUPS):
        bg = b_ref[0, rs, g * SSD_STATE:(g + 1) * SSD_STATE]
        cg = c_ref[0, rs, g * SSD_STATE:(g + 1) * SSD_STATE].astype(BF16)
        b_t.append(bg.T.astype(BF16))
        gm = _dot_nt(cg, bg.astype(BF16))
        states = jnp.concatenate([st_all[:, g * half:(g + 1) * half].astype(BF16),
                                  sb_all[:, g * half:(g + 1) * half]], axis=1)
        off = _dot(cg, states)
        y_off = (off[:, :half] * e_f[:, g * half:(g + 1) * half]
                 + off[:, half:] * e_b[:, g * half:(g + 1) * half])
        for pair in range(hpg // 2):
            ms = []
            for h in (g * hpg + 2 * pair, g * hpg + 2 * pair + 1):
                dec_f = jnp.where(lower, jnp.exp(cols[:, 3 * N_HEADS + h:3 * N_HEADS + h + 1] - csf[h:h + 1, :]), 0.0)
                dec_b = jnp.where(upper, jnp.exp(cols[:, 4 * N_HEADS + h:4 * N_HEADS + h + 1] - rcs[h:h + 1, :]), 0.0)
                ms.append((gm * (dec_f * dtf[h:h + 1, :] + dec_b * dtb[h:h + 1, :])).astype(BF16))
            lo = g * half + pair * LANE
            xp = xb[:, lo:lo + LANE]
            ys.append(jnp.where(head_a, _dot(ms[0], xp), _dot(ms[1], xp))
                      + y_off[:, pair * LANE:(pair + 1) * LANE])
    y = jnp.concatenate(ys, axis=1) + xs * dexp_ref[...]
    yz = y * _silu(z_ref[0, rs, :])
    o_ref[0, rs, :] = _rms(yz, on_ref[...]).astype(BF16)

    xw = (xs * w_state).astype(BF16)
    upd = [_dot(b_t[g], xw[:, g * half:(g + 1) * half]) for g in range(SSD_GROUPS)]
    st[...] = st_all * e_f[CHUNK - 1:CHUNK, :] + jnp.concatenate(upd, axis=1)


def _ssd(xbc_act, dt, z, pc, dexp, out_norm):
    b, seq, _ = xbc_act.shape
    nc = seq // CHUNK
    sel = np.zeros((2 * CHUNK, 3 * D_MODEL), np.float32)
    for part in range(3):
        for h in range(N_HEADS):
            for rep in range(2):
                sel[rep * CHUNK + part * N_HEADS + h, part * D_MODEL + h * HEAD_DIM:part * D_MODEL + (h + 1) * HEAD_DIM] = 1.0
    e3 = jnp.asarray(sel, BF16)
    e2 = jnp.asarray(sel[:, :2 * D_MODEL], BF16)

    rows = SSD_CPS * CHUNK
    nblk = seq // rows
    bcol = D_MODEL // (2 * SSD_STATE)
    rev = lambda bi, c: (bi, nblk - 1 - c, 0)
    sb = pl.pallas_call(
        _ssd_bwd_kernel, grid=(b, nblk),
        in_specs=[pl.BlockSpec((1, rows, D_MODEL), rev),
                  pl.BlockSpec((1, rows, 2 * SSD_STATE), lambda bi, c: (bi, nblk - 1 - c, bcol)),
                  pl.BlockSpec((1, rows, LANE), rev), _const_spec(pc.shape), _const_spec(e2.shape)],
        out_specs=pl.BlockSpec((1, SSD_CPS, SSD_STATE, D_MODEL), lambda bi, c: (bi, nblk - 1 - c, 0, 0)),
        out_shape=jax.ShapeDtypeStruct((b, nc, SSD_STATE, D_MODEL), BF16),
        scratch_shapes=[pltpu.VMEM((SSD_STATE, D_MODEL), F32)],
        compiler_params=_cparams(2), name="ssd_bwd_state",
    )(xbc_act, xbc_act, dt, pc, e2)

    fwd = lambda bi, c: (bi, c, 0)
    return pl.pallas_call(
        _ssd_fwd_kernel, grid=(b, nblk),
        in_specs=[pl.BlockSpec((1, rows, D_MODEL), fwd),
                  pl.BlockSpec((1, rows, 2 * SSD_STATE), lambda bi, c: (bi, c, bcol)),
                  pl.BlockSpec((1, rows, 2 * SSD_STATE), lambda bi, c: (bi, c, bcol + 1)),
                  pl.BlockSpec((1, rows, LANE), fwd), pl.BlockSpec((1, rows, D_MODEL), fwd),
                  pl.BlockSpec((1, SSD_CPS, SSD_STATE, D_MODEL), lambda bi, c: (bi, c, 0, 0)),
                  _const_spec(pc.shape), _const_spec(e3.shape), _const_spec((1, D_MODEL)), _const_spec((1, D_MODEL))],
        out_specs=pl.BlockSpec((1, rows, D_MODEL), fwd),
        out_shape=jax.ShapeDtypeStruct((b, seq, D_MODEL), BF16),
        scratch_shapes=[pltpu.VMEM((SSD_STATE, D_MODEL), F32)],
        compiler_params=_cparams(2), name="ssd_fwd",
    )(xbc_act, xbc_act, xbc_act, dt, z, sb, pc, e3, dexp, out_norm)


def _norm_and_route(x1, fg_ref, wr_hi_ref, wr_lo_ref, hn_ref, aff_ref):
    hn = _rms(x1, fg_ref[...])
    hi, lo = _split2(hn)
    hn_ref[...] = hi
    logits = _dot(hi, wr_hi_ref[...]) + _dot(lo, wr_hi_ref[...]) + _dot(hi, wr_lo_ref[...])
    lt = logits.T[0:N_EXPERTS, :]
    e = jnp.exp(lt - jnp.max(lt, axis=0, keepdims=True))
    aff_ref[0] = e / jnp.sum(e, axis=0, keepdims=True)


def _out_even_kernel(x_ref, a_ref, y_ref, wa_ref, wy_ref, fg_ref, wr_hi_ref, wr_lo_ref, x1_ref, hn_ref, aff_ref):
    x1 = x_ref[...] + _dot(a_ref[...], wa_ref[...]) + _dot(y_ref[...], wy_ref[...])
    x1_ref[...] = x1
    _norm_and_route(x1, fg_ref, wr_hi_ref, wr_lo_ref, hn_ref, aff_ref)


def _out_odd_kernel(x_ref, gb_ref, cu_ref, cp_ref, cn_ref, cw_ref, wo_ref, fg_ref, wr_hi_ref, wr_lo_ref,
                    x1_ref, hn_ref, aff_ref, *, tiles_per_seq):
    i = pl.program_id(0) % tiles_per_seq
    first, last = i == 0, i == tiles_per_seq - 1
    cur, prev8, next8 = cu_ref[...], cp_ref[...], cn_ref[...]
    conv = sum(_shifted(cur, prev8, next8, j - SHORT_CONV // 2, first, last) * cw_ref[j:j + 1, :]
               for j in range(SHORT_CONV))
    x1 = x_ref[...] + _dot((gb_ref[...] * conv).astype(BF16), wo_ref[...])
    x1_ref[...] = x1
    _norm_and_route(x1, fg_ref, wr_hi_ref, wr_lo_ref, hn_ref, aff_ref)


def _route_outs(n, b, seq, tm):
    row = pl.BlockSpec((tm, D_MODEL), lambda i: (i, 0))
    aff = pl.BlockSpec((1, N_EXPERTS, tm), lambda i: (i // (seq // tm), 0, i % (seq // tm)))
    shapes = [jax.ShapeDtypeStruct((n, D_MODEL), F32), jax.ShapeDtypeStruct((n, D_MODEL), BF16),
              jax.ShapeDtypeStruct((b, N_EXPERTS, seq), F32)]
    return [row, row, aff], shapes


def _out_even(x, attn, y, wa, wy, fg, wr_hi, wr_lo, b, seq):
    n = x.shape[0]
    tm = ROW_TILE
    row = pl.BlockSpec((tm, D_MODEL), lambda i: (i, 0))
    out_specs, shapes = _route_outs(n, b, seq, tm)
    return pl.pallas_call(
        _out_even_kernel, grid=(n // tm,),
        in_specs=[row, row, row, _const_spec(wa.shape), _const_spec(wy.shape), _const_spec((1, D_MODEL)),
                  _const_spec(wr_hi.shape), _const_spec(wr_lo.shape)],
        out_specs=out_specs, out_shape=shapes, compiler_params=_cparams(1), name="out_even",
    )(x, attn, y, wa, wy, fg, wr_hi, wr_lo)


def _out_odd(x, gb, cu, cw, wo, fg, wr_hi, wr_lo, b, seq):
    n = x.shape[0]
    tm = ROW_TILE
    row = pl.BlockSpec((tm, D_MODEL), lambda i: (i, 0))
    prev = pl.BlockSpec((8, D_MODEL), lambda i: (jnp.maximum(i * (tm // 8) - 1, 0), 0))
    nxt = pl.BlockSpec((8, D_MODEL), lambda i: (jnp.minimum((i + 1) * (tm // 8), n // 8 - 1), 0))
    out_specs, shapes = _route_outs(n, b, seq, tm)
    return pl.pallas_call(
        functools.partial(_out_odd_kernel, tiles_per_seq=seq // tm), grid=(n // tm,),
        in_specs=[row, row, row, prev, nxt, pl.BlockSpec((8, D_MODEL), lambda i: (0, 0)), _const_spec(wo.shape),
                  _const_spec((1, D_MODEL)), _const_spec(wr_hi.shape), _const_spec(wr_lo.shape)],
        out_specs=out_specs, out_shape=shapes, compiler_params=_cparams(1), name="out_odd",
    )(x, gb, cu, cu, cu, cw, wo, fg, wr_hi, wr_lo)


def _in_odd_kernel(x_ref, g_ref, wb_ref, wc_ref, wu_ref, gb_ref, cu_ref):
    hb = _rms(x_ref[...], g_ref[...]).astype(BF16)
    gb_ref[...] = _dot(hb, wb_ref[...])
    cu_ref[...] = _dot(hb, wc_ref[...]) * _dot(hb, wu_ref[...])


def _in_odd(x, g, wb, wc, wu):
    n = x.shape[0]
    tm = ROW_TILE
    row = pl.BlockSpec((tm, D_MODEL), lambda i: (i, 0))
    return pl.pallas_call(
        _in_odd_kernel, grid=(n // tm,),
        in_specs=[row, _const_spec((1, D_MODEL)), _const_spec(wb.shape), _const_spec(wc.shape), _const_spec(wu.shape)],
        out_specs=[row, row], out_shape=[jax.ShapeDtypeStruct((n, D_MODEL), F32)] * 2,
        compiler_params=_cparams(1), name="in_odd",
    )(x, g, wb, wc, wu)


def _count(mask):
    return jnp.sum(jnp.sum(mask.astype(F32), axis=0, keepdims=True), axis=1, keepdims=True)


def _route_kernel(aff_ref, incl_ref, ones_ref, strict_ref, local_ref, group_ref, first_ref, slot_ref, off_ref, end_ref,
                  *, cap):
    def step(i, thrs):
        bit = jnp.int32(1) << (30 - i)
        out = []
        for e in range(N_EXPERTS):
            cand = thrs[e] | bit
            out.append(jnp.where(_count(pltpu.bitcast(aff_ref[0, e], I32) >= cand) >= cap, cand, thrs[e]))
        return tuple(out)

    thrs = lax.fori_loop(0, 31, step, tuple(jnp.zeros((1, 1), I32) for _ in range(N_EXPERTS)))
    for e in range(N_EXPERTS):
        bits = pltpu.bitcast(aff_ref[0, e], I32)
        thr = thrs[e]
        gt = bits > thr
        eq = (bits == thr).astype(BF16)
        eq_rank = _dot(eq, incl_ref[...]) + _dot(strict_ref[...], _dot(eq, ones_ref[...]).astype(BF16))
        sel = (gt | ((bits == thr) & (eq_rank <= cap - _count(gt)))).astype(BF16)
        within = _dot(sel, incl_ref[...])
        totals = _dot(sel, ones_ref[...]).astype(BF16)
        local = _dot(local_ref[...], totals)
        cnt = _dot(group_ref[...], totals)
        padded = jnp.floor((cnt + (SLOT_ALIGN - 1)) * (1.0 / SLOT_ALIGN)) * SLOT_ALIGN
        start = _dot(first_ref[...], padded.astype(BF16))
        slot_ref[0, e] = jnp.where(sel > 0, (start + local + within).astype(I32) - 1, -1)
        off_ref[0, e:e + 1, :] = start.T[0:1, :].astype(I32)
        end_ref[0, e:e + 1, :] = (start + padded).T[0:1, :].astype(I32)


def _route(aff, cap):
    b, _, seq = aff.shape
    nt = seq // LANE
    tri = np.arange(LANE)
    tt = np.arange(nt)
    grp = tt // SUB
    as_bf16 = lambda m: jnp.asarray(m, BF16)
    incl = as_bf16(tri[:, None] <= tri[None, :])
    strict = as_bf16(tt[None, :] < tt[:, None])
    local = as_bf16((tt[None, :] < tt[:, None]) & (grp[None, :] == grp[:, None]))
    group = as_bf16(grp[None, :] == grp[:, None])
    first = as_bf16((grp[None, :] < grp[:, None]) & (tt[None, :] % SUB == 0))
    ones = jnp.ones((LANE, LANE), BF16)
    tiles = pl.BlockSpec((1, N_EXPERTS, nt, LANE), lambda i: (i, 0, 0, 0))
    rows = pl.BlockSpec((1, N_EXPERTS, nt), lambda i: (i, 0, 0))
    return pl.pallas_call(
        functools.partial(_route_kernel, cap=cap), grid=(b,),
        in_specs=[tiles, _const_spec((LANE, LANE)), _const_spec((LANE, LANE))] + [_const_spec((nt, nt))] * 4,
        out_specs=[tiles, rows, rows],
        out_shape=[jax.ShapeDtypeStruct((b, N_EXPERTS, nt, LANE), I32)] + [jax.ShapeDtypeStruct((b, N_EXPERTS, nt), I32)] * 2,
        compiler_params=_cparams(1), name="route",
    )(aff.reshape(b, N_EXPERTS, nt, LANE), incl, ones, strict, local, group, first)


def _one_hot(slots, base):
    return (slots == lax.broadcasted_iota(I32, (WINDOW, slots.shape[1]), 0) + base).astype(BF16)


def _gather_kernel(off_ref, end_ref, slot_ref, aff_ref, hn_ref, xe_hbm, stage, extra, sem, xsem):
    bi, j = pl.program_id(0), pl.program_id(1)
    n_j = pl.num_programs(1)
    step = bi * n_j + j
    cur = step % 2

    def window_copy(bb, jj, e, buf):
        start = pl.multiple_of(off_ref[bb, e, jj * SUB], SLOT_ALIGN)
        return pltpu.make_async_copy(stage.at[buf, e], xe_hbm.at[bb, e, pl.ds(start, WINDOW), :], sem.at[buf, e])

    tokens = hn_ref[0]
    parts = [p.astype(F32) for p in _split3(aff_ref[0, 0])]
    gates = jnp.concatenate(parts + [jnp.zeros((LANE - 3 * N_EXPERTS, TOK_TILE), F32)], axis=0).astype(BF16)

    def rows_of(p):
        return jnp.concatenate([_dot(p, tokens), _dot_nt(p, gates)], axis=1).astype(BF16)

    p_all = jnp.concatenate([_one_hot(slot_ref[0, 0, e:e + 1, :], off_ref[bi, e, j * SUB])
                             for e in range(N_EXPERTS)], axis=0)
    stage[cur] = rows_of(p_all).reshape(N_EXPERTS, WINDOW, D_MODEL + LANE)

    @pl.when(j > 0)
    def _():
        for e in range(N_EXPERTS):
            window_copy(bi, j - 1, e, 1 - cur).wait()

    for e in range(N_EXPERTS):
        window_copy(bi, j, e, cur).start()

    def overflow(e, c):
        first = off_ref[bi, e, j * SUB]
        n_win = (end_ref[bi, e, j * SUB] - first + WINDOW - 1) // WINDOW
        slots = slot_ref[0, 0, pl.ds(e, 1), :]

        def one(w, c2):
            base = pl.multiple_of(first + w * WINDOW, SLOT_ALIGN)
            extra[...] = rows_of(_one_hot(slots, base))
            cp = pltpu.make_async_copy(extra, xe_hbm.at[bi, e, pl.ds(base, WINDOW), :], xsem.at[0])
            cp.start()
            cp.wait()
            return c2

        return lax.fori_loop(1, n_win, one, c)

    lax.fori_loop(0, N_EXPERTS, overflow, 0)

    @pl.when(j == n_j - 1)
    def _():
        for e in range(N_EXPERTS):
            window_copy(bi, j, e, cur).wait()
        extra[...] = jnp.zeros_like(extra)
        cap_pad = xe_hbm.shape[2]

        def fill(e, c):
            used = end_ref[bi, e, end_ref.shape[2] - 1]
            n_big = (cap_pad - used) // WINDOW
            small0 = used + n_big * WINDOW
            n_small = (cap_pad - small0) // SLOT_ALIGN
            big = lambda i: pltpu.make_async_copy(
                extra, xe_hbm.at[bi, e, pl.ds(pl.multiple_of(used + i * WINDOW, SLOT_ALIGN), WINDOW), :], xsem.at[0])
            small = lambda i: pltpu.make_async_copy(
                extra.at[0:SLOT_ALIGN],
                xe_hbm.at[bi, e, pl.ds(pl.multiple_of(small0 + i * SLOT_ALIGN, SLOT_ALIGN), SLOT_ALIGN), :], xsem.at[0])
            for n, mk in ((n_big, big), (n_small, small)):
                lax.fori_loop(0, n, lambda i, c2, mk=mk: (mk(i).start(), c2)[1], 0)
            for n, mk in ((n_big, big), (n_small, small)):
                lax.fori_loop(0, n, lambda i, c2, mk=mk: (mk(i).wait(), c2)[1], 0)
            return c

        lax.fori_loop(0, N_EXPERTS, fill, 0)


def _gather(off, end, slot_t, aff_t, hn):
    b, seq, _ = hn.shape
    cap_pad = _cap_pad(seq)
    width = D_MODEL + LANE
    per_tile = pl.BlockSpec((1, 1, N_EXPERTS, TOK_TILE), lambda bi, j, *_: (bi, j, 0, 0))
    return pl.pallas_call(
        _gather_kernel,
        grid_spec=pltpu.PrefetchScalarGridSpec(
            num_scalar_prefetch=2, grid=(b, seq // TOK_TILE),
            in_specs=[per_tile, per_tile, pl.BlockSpec((1, TOK_TILE, D_MODEL), lambda bi, j, *_: (bi, j, 0))],
            out_specs=pl.BlockSpec(memory_space=pl.ANY),
            scratch_shapes=[pltpu.VMEM((2, N_EXPERTS, WINDOW, width), BF16), pltpu.VMEM((WINDOW, width), BF16),
                            pltpu.SemaphoreType.DMA((2, N_EXPERTS)), pltpu.SemaphoreType.DMA((1,))]),
        out_shape=jax.ShapeDtypeStruct((b, N_EXPERTS, cap_pad, width), BF16),
        compiler_params=_cparams(2), name="moe_gather",
    )(off, end, slot_t, aff_t, hn)


def _ffn_kernel(end_ref, xe_ref, wg_hbm, wu_hbm, wd_hbm, y_ref, stage_g, stage_u, stage_d, wg, wu, wd, sem, *, layer):
    e, bi, r = pl.program_id(0), pl.program_id(1), pl.program_id(2)
    used = end_ref[bi, e, end_ref.shape[2] - 1]
    pairs = ((wg_hbm, stage_g, wg), (wu_hbm, stage_u, wu), (wd_hbm, stage_d, wd))

    def weight_copies(ee):
        return [pltpu.make_async_copy(src.at[layer, ee], stg, sem.at[k]) for k, (src, stg, _) in enumerate(pairs)]

    @pl.when((bi == 0) & (r == 0))
    def _():
        @pl.when(e == 0)
        def _():
            for cp in weight_copies(e):
                cp.start()

        for cp, (_, stg, dst) in zip(weight_copies(e), pairs):
            cp.wait()
            n_rows = stg.shape[0]

            def cast(i, c, stg=stg, dst=dst):
                rows = pl.ds(pl.multiple_of(i * CAST_ROWS, CAST_ROWS), CAST_ROWS)
                dst[rows, :] = stg[rows, :].astype(BF16)
                return c

            lax.fori_loop(0, n_rows // CAST_ROWS, cast, 0)

        @pl.when(e + 1 < pl.num_programs(0))
        def _():
            for cp in weight_copies(e + 1):
                cp.start()

    @pl.when(r * FFN_ROWS < used)
    def _():
        xe = xe_ref[0, 0, :, 0:D_MODEL]
        hid = (_silu(_dot(xe, wg[...])) * _dot(xe, wu[...])).astype(BF16)
        g = xe_ref[0, 0, :, D_MODEL:].astype(F32)
        lane = lax.broadcasted_iota(I32, g.shape, 1)
        mine = (lane % N_EXPERTS == e) & (lane < 3 * N_EXPERTS)
        gate = jnp.sum(jnp.where(mine, g, 0.0), axis=1, keepdims=True)
        y_ref[0, 0] = (_dot(hid, wd[...]) * gate).astype(BF16)

    @pl.when(r * FFN_ROWS >= used)
    def _():
        y_ref[...] = jnp.zeros_like(y_ref)


def _ffn(end, xe, wg, wu, wd, layer):
    b, ne, cap_pad, width = xe.shape
    rows = lambda w: pl.BlockSpec((1, 1, FFN_ROWS, w), lambda e, bi, r, *_: (bi, e, r, 0))
    hbm = pl.BlockSpec(memory_space=pl.ANY)
    return pl.pallas_call(
        functools.partial(_ffn_kernel, layer=layer),
        grid_spec=pltpu.PrefetchScalarGridSpec(
            num_scalar_prefetch=1, grid=(ne, b, cap_pad // FFN_ROWS),
            in_specs=[rows(width), hbm, hbm, hbm],
            out_specs=rows(D_MODEL),
            scratch_shapes=[pltpu.VMEM((D_MODEL, D_FF), F32), pltpu.VMEM((D_MODEL, D_FF), F32),
                            pltpu.VMEM((D_FF, D_MODEL), F32), pltpu.VMEM((D_MODEL, D_FF), BF16),
                            pltpu.VMEM((D_MODEL, D_FF), BF16), pltpu.VMEM((D_FF, D_MODEL), BF16),
                            pltpu.SemaphoreType.DMA((3,))]),
        out_shape=jax.ShapeDtypeStruct((b, ne, cap_pad, D_MODEL), BF16),
        compiler_params=_cparams(3), name="moe_ffn",
    )(end, xe, wg, wu, wd)


def _combine_kernel(off_ref, end_ref, slot_ref, y_hbm, x1_ref, o_ref, win, extra, sem, xsem):
    bi, j = pl.program_id(0), pl.program_id(1)
    n_j = pl.num_programs(1)
    step = bi * n_j + j
    cur = step % 2

    def window_copy(bb, jj, e, buf):
        start = pl.multiple_of(off_ref[bb, e, jj * SUB], SLOT_ALIGN)
        return pltpu.make_async_copy(y_hbm.at[bb, e, pl.ds(start, WINDOW), :], win.at[buf, e], sem.at[buf, e])

    @pl.when(step == 0)
    def _():
        for e in range(N_EXPERTS):
            window_copy(bi, j, e, cur).start()

    @pl.when(step + 1 < pl.num_programs(0) * n_j)
    def _():
        nxt = step + 1
        for e in range(N_EXPERTS):
            window_copy(nxt // n_j, nxt % n_j, e, 1 - cur).start()

    ps = []
    for e in range(N_EXPERTS):
        window_copy(bi, j, e, cur).wait()
        ps.append(_one_hot(slot_ref[0, 0, e:e + 1, :], off_ref[bi, e, j * SUB]))
    p_all = jnp.concatenate(ps, axis=0)
    y_all = win[cur].reshape(N_EXPERTS * WINDOW, D_MODEL)
    o_ref[0] = x1_ref[0] + _dot_tn(p_all, y_all)

    def overflow(e, c):
        first = off_ref[bi, e, j * SUB]
        n_win = (end_ref[bi, e, j * SUB] - first + WINDOW - 1) // WINDOW
        slots = slot_ref[0, 0, pl.ds(e, 1), :]

        def one(w, c2):
            base = pl.multiple_of(first + w * WINDOW, SLOT_ALIGN)
            cp = pltpu.make_async_copy(y_hbm.at[bi, e, pl.ds(base, WINDOW), :], extra, xsem.at[0])
            cp.start()
            cp.wait()
            o_ref[0] += _dot_tn(_one_hot(slots, base), extra[...])
            return c2

        return lax.fori_loop(1, n_win, one, c)

    lax.fori_loop(0, N_EXPERTS, overflow, 0)


def _combine(off, end, slot_t, y, x1):
    b, seq, _ = x1.shape
    tile = pl.BlockSpec((1, TOK_TILE, D_MODEL), lambda bi, j, *_: (bi, j, 0))
    return pl.pallas_call(
        _combine_kernel,
        grid_spec=pltpu.PrefetchScalarGridSpec(
            num_scalar_prefetch=2, grid=(b, seq // TOK_TILE),
            in_specs=[pl.BlockSpec((1, 1, N_EXPERTS, TOK_TILE), lambda bi, j, *_: (bi, j, 0, 0)),
                      pl.BlockSpec(memory_space=pl.ANY), tile],
            out_specs=tile,
            scratch_shapes=[pltpu.VMEM((2, N_EXPERTS, WINDOW, D_MODEL), BF16), pltpu.VMEM((WINDOW, D_MODEL), BF16),
                            pltpu.SemaphoreType.DMA((2, N_EXPERTS)), pltpu.SemaphoreType.DMA((1,))]),
        out_shape=jax.ShapeDtypeStruct((b, seq, D_MODEL), F32),
        compiler_params=_cparams(2), name="moe_combine",
    )(off, end, slot_t, y, x1)


def _cap_pad(seq):
    cap = CAPACITY_FACTOR * seq // N_EXPERTS
    worst = cap + (seq // TOK_TILE) * (SLOT_ALIGN - 1) + WINDOW
    return -(-worst // FFN_ROWS) * FFN_ROWS


def _moe(x1, hn, aff, wg, wu, wd, layer):
    b, seq, _ = x1.shape
    cap = CAPACITY_FACTOR * seq // N_EXPERTS
    slot, off, end = _route(aff, cap)
    per_tile = lambda a: jnp.swapaxes(a.reshape(b, N_EXPERTS, seq // TOK_TILE, TOK_TILE), 1, 2)
    slot_t = per_tile(slot)
    xe = _gather(off, end, slot_t, per_tile(aff), hn)
    y = _ffn(end, xe, wg, wu, wd, layer)
    return _combine(off, end, slot_t, y, x1)


def _rope_tables(seq):
    inv_freq = ROPE_THETA ** (-jnp.arange(ROPE_HALF, dtype=F32) * 2.0 / (2 * ROPE_HALF))
    ang = jnp.arange(seq, dtype=F32)[:, None] * inv_freq[None, :]
    cos, sin = jnp.cos(ang), jnp.sin(ang)
    z = lambda w: jnp.zeros((seq, w), F32)
    rest = HEAD_DIM - 2 * ROPE_HALF
    rc = jnp.concatenate([cos, cos, jnp.ones((seq, rest), F32)], axis=1)
    r1 = jnp.concatenate([-sin, z(ROPE_HALF + rest)], axis=1)
    r2 = jnp.concatenate([z(ROPE_HALF), sin, z(rest)], axis=1)
    return tuple(jnp.tile(t, (1, LANE // HEAD_DIM)) for t in (rc, r1, r2))


def _router_split(w):
    wp = jnp.pad(w, ((0, 0), (0, LANE - N_EXPERTS)))
    hi = wp.astype(BF16)
    return hi, (wp - hi.astype(F32)).astype(BF16)


def kernel(x, attn_norm, w_in_even, q_norm, k_norm, ssd_conv_w, ssd_conv_b, ssd_a_log_fwd, ssd_a_log_bwd,
           ssd_dt_bias_fwd, ssd_dt_bias_bwd, ssd_d, ssd_out_norm, w_out_even, conv_norm, conv_w_in, conv_w,
           conv_w_out, ffn_norm, router_w, expert_w_gate, expert_w_up, expert_w_down):
    b, seq, _ = x.shape
    n = b * seq
    depth = ffn_norm.shape[0]
    rc, r1, r2 = _rope_tables(seq)
    blk = np.arange(256) // HEAD_DIM
    bd = jnp.asarray((blk[:, None] == blk[None, :]) / HEAD_DIM, BF16)
    row = lambda v: v.reshape(1, -1).astype(F32)

    xf = x.reshape(n, D_MODEL)
    for layer in range(depth):
        i = layer // 2
        wr_hi, wr_lo = _router_split(router_w[layer])
        fg = row(ffn_norm[layer])
        if layer % 2 == 0:
            w = w_in_even[i].astype(BF16)
            o = np.cumsum([0, D_MODEL, D_MODEL, D_MODEL, D_MODEL, SSD_XBC, N_HEADS, N_HEADS])
            wq, wk, wv, wz, wx = (w[:, o[j]:o[j + 1]] for j in range(5))
            wd = jnp.pad(w[:, o[5]:o[7]], ((0, 0), (0, LANE - 2 * N_HEADS)))
            tile_heads = lambda g: row(jnp.tile(g, N_HEADS))
            q, k, v, z, xbc, dt = _in_even(xf, row(attn_norm[i]), wq, wk, wv, wz, wx, wd,
                                           tile_heads(q_norm[i]), tile_heads(k_norm[i]), bd, rc, r1, r2, seq)
            as3 = lambda t: t.reshape(b, seq, -1)
            attn = _attention(as3(q), as3(k), as3(v))
            cw = jnp.pad(ssd_conv_w[i], ((0, 8 - SSD_CONV), (0, 0)))
            act = _ssd_conv(as3(xbc), cw, row(ssd_conv_b[i]))
            pc = jnp.pad(jnp.stack([ssd_dt_bias_fwd[i], ssd_dt_bias_bwd[i], -jnp.exp(ssd_a_log_fwd[i]),
                                    -jnp.exp(ssd_a_log_bwd[i])], axis=1).astype(F32), ((0, 0), (0, LANE - 4)))
            y = _ssd(act, as3(dt), as3(z), pc, row(jnp.repeat(ssd_d[i], HEAD_DIM)), row(ssd_out_norm[i]))
            wo = w_out_even[i].astype(BF16)
            x1, hn, aff = _out_even(xf, attn.reshape(n, D_MODEL), y.reshape(n, D_MODEL), wo[:D_MODEL], wo[D_MODEL:],
                                    fg, wr_hi, wr_lo, b, seq)
        else:
            w = conv_w_in[i].astype(BF16)
            gb, cu = _in_odd(xf, row(conv_norm[i]), w[:, :D_MODEL], w[:, D_MODEL:2 * D_MODEL], w[:, 2 * D_MODEL:])
            cw = jnp.pad(conv_w[i], ((0, 8 - SHORT_CONV), (0, 0)))
            x1, hn, aff = _out_odd(xf, gb, cu, cw, conv_w_out[i].astype(BF16), fg, wr_hi, wr_lo, b, seq)
        xf = _moe(x1.reshape(b, seq, D_MODEL), hn.reshape(b, seq, D_MODEL), aff, expert_w_gate, expert_w_up,
                  expert_w_down, layer).reshape(n, D_MODEL)
    return xf.reshape(b, seq, D_MODEL)
```

```python
import functools
import math

import jax
import jax.numpy as jnp
import numpy as np
from jax import lax
from jax.experimental import pallas as pl
from jax.experimental.pallas import tpu as pltpu

F32, BF16, I32 = jnp.float32, jnp.bfloat16, jnp.int32

D_MODEL = 1024
N_HEADS = 16
HEAD_DIM = 64
ROPE_HALF = 8
ROPE_THETA = 500000.0
PATTERNS = ((128, 1), (512, 4), (2048, 16))
HALF_STEPS = 64
SSD_GROUPS = 2
SSD_STATE = 128
SSD_XBC = 1536
SSD_CONV = 5
CHUNK = 128
SSD_CPS = 4
N_EXPERTS = 16
CAPACITY_FACTOR = 2
D_FF = 2048
SHORT_CONV = 3
EPS = 1e-6

LANE = 128
VMEM_LIMIT = 56 * 1024 * 1024

ROW_TILE = 512
OUT_ROW_TILE = 256
ATTN_SUPER = 2048
ATTN_HALO = 1024
ATTN_TQ = 128
ATTN_TK = ATTN_TQ + 2 * HALF_STEPS
ATTN_UNROLL = 8
TOK_TILE = 512
SUB = TOK_TILE // LANE
SLOT_ALIGN = 16
WINDOW = 96
FFN_ROWS = 256
CAST_ROWS = 64


def _cparams(n_axes):
    return pltpu.CompilerParams(dimension_semantics=("arbitrary",) * n_axes, vmem_limit_bytes=VMEM_LIMIT)


def _const_spec(shape):
    nd = len(shape)
    return pl.BlockSpec(shape, lambda *_: (0,) * nd, pipeline_mode=pl.Buffered(1))


def _dot(a, b):
    return jnp.dot(a, b, preferred_element_type=F32)


def _dot_nt(a, b):
    return lax.dot_general(a, b, (((1,), (1,)), ((), ())), preferred_element_type=F32)


def _dot_tn(a, b):
    return lax.dot_general(a, b, (((0,), (0,)), ((), ())), preferred_element_type=F32)


def _split2(x):
    hi = x.astype(BF16)
    lo = (x - hi.astype(F32)).astype(BF16)
    return hi, lo


def _split3(x):
    hi = x.astype(BF16)
    r = x - hi.astype(F32)
    mid = r.astype(BF16)
    lo = (r - mid.astype(F32)).astype(BF16)
    return hi, mid, lo


def _dot3(x, m_bf16):
    hi, mid, lo = _split3(x)
    return _dot(hi, m_bf16) + _dot(mid, m_bf16) + _dot(lo, m_bf16)


def _rms(x, g):
    return x * lax.rsqrt(jnp.mean(x * x, axis=-1, keepdims=True) + EPS) * g


def _silu(x):
    return x * jax.nn.sigmoid(x)


def _in_even_kernel(x_ref, g_ref, wq_ref, wk_ref, wv_ref, wz_ref, wx_ref, wd_ref, qg_ref, kg_ref, bd_ref,
                    rc_ref, r1_ref, r2_ref, q_ref, k_ref, v_ref, z_ref, xbc_ref, dt_ref):
    hb = _rms(x_ref[...], g_ref[...]).astype(BF16)
    bd = bd_ref[...]
    rc, r1, r2 = rc_ref[...], r1_ref[...], r2_ref[...]

    def head_norm_rope(w_ref, gain_ref, out_ref):
        t = _dot(hb, w_ref[...])
        for c in range(D_MODEL // 256):
            tc = t[:, c * 256:(c + 1) * 256]
            sq_hi, sq_lo = _split2(tc * tc)
            ms = _dot(sq_hi, bd) + _dot(sq_lo, bd)
            tn = tc * lax.rsqrt(ms + EPS) * gain_ref[:, c * 256:(c + 1) * 256]
            for hh in range(2):
                u = tn[:, hh * LANE:(hh + 1) * LANE]
                r = u * rc + pltpu.roll(u, LANE - ROPE_HALF, 1) * r1 + pltpu.roll(u, ROPE_HALF, 1) * r2
                out_ref[:, c * 256 + hh * LANE:c * 256 + (hh + 1) * LANE] = r

    head_norm_rope(wq_ref, qg_ref, q_ref)
    head_norm_rope(wk_ref, kg_ref, k_ref)
    v_ref[...] = _dot(hb, wv_ref[...])
    z_ref[...] = _dot(hb, wz_ref[...])
    xbc_ref[...] = _dot(hb, wx_ref[...])
    dt_ref[...] = _dot(hb, wd_ref[...])


def _in_even(x, g, wq, wk, wv, wz, wx, wd, qg, kg, bd, rc, r1, r2, seq):
    n = x.shape[0]
    tm = ROW_TILE
    row = lambda w: pl.BlockSpec((tm, w), lambda i: (i, 0))
    tab = pl.BlockSpec((tm, LANE), lambda i: (i % (seq // tm), 0))
    outs = [jax.ShapeDtypeStruct((n, D_MODEL), F32)] * 4 + [jax.ShapeDtypeStruct((n, SSD_XBC), F32),
                                                           jax.ShapeDtypeStruct((n, LANE), F32)]
    return pl.pallas_call(
        _in_even_kernel, grid=(n // tm,),
        in_specs=[row(D_MODEL), _const_spec((1, D_MODEL)), _const_spec(wq.shape), _const_spec(wk.shape),
                  _const_spec(wv.shape), _const_spec(wz.shape), _const_spec(wx.shape), _const_spec(wd.shape),
                  _const_spec((1, D_MODEL)), _const_spec((1, D_MODEL)), _const_spec((256, 256)), tab, tab, tab],
        out_specs=[row(D_MODEL)] * 4 + [row(SSD_XBC), row(LANE)], out_shape=outs,
        compiler_params=_cparams(1), name="in_even",
    )(x, g, wq, wk, wv, wz, wx, wd, qg, kg, bd, rc, r1, r2)


def _attn_kernel(q_ref, k0, k1, k2, k3, v0, v1, v2, v3, qw_hbm, kw_hbm, vw_hbm, o_ref, qf, kf, vf, q16, k16, v16,
                 acc, mst, lst, a16, m16, l16, an3, mn3, ln3, sem, rsem, *, seq):
    dmax = PATTERNS[-1][1]
    bi, hp, j = pl.program_id(0), pl.program_id(1), pl.program_id(2)
    n_hp, n_j = pl.num_programs(1), pl.num_programs(2)
    p0 = j * ATTN_SUPER
    halo = ATTN_HALO // dmax
    step = (bi * n_hp + hp) * n_j + j
    cur = step % 2

    def residue_copies(at, buf, r, where):
        b_, hp_, j_ = at
        lanes = pl.ds(pl.multiple_of(hp_ * LANE, LANE), LANE)
        q0 = pl.multiple_of(j_ * ATTN_TQ, ATTN_TQ)
        n_kv = ATTN_TK - halo if where else ATTN_TK
        src0 = 0 if where < 0 else q0 - halo
        dst0 = halo if where < 0 else 0
        cps = [pltpu.make_async_copy(qw_hbm.at[b_, pl.ds(q0, ATTN_TQ), r, lanes], q16.at[buf, r], rsem.at[buf, 0, r])]
        for n, (src, dst) in enumerate(((kw_hbm, k16), (vw_hbm, v16))):
            cps.append(pltpu.make_async_copy(src.at[b_, pl.ds(src0, n_kv), r, lanes],
                                             dst.at[buf, r, pl.ds(dst0, n_kv), :], rsem.at[buf, 1 + n, r]))
        return cps

    def all_residues(at, buf, action):
        def run(where):
            for r in range(dmax):
                for cp in residue_copies(at, buf, r, where):
                    getattr(cp, action)()
        j_ = at[2]
        pl.when(j_ == 0)(functools.partial(run, -1))
        pl.when((j_ > 0) & (j_ < n_j - 1))(functools.partial(run, 0))
        pl.when(j_ == n_j - 1)(functools.partial(run, 1))

    def state_copies(r):
        return [pltpu.make_async_copy(src.at[r], dst.at[:, r, :], sem.at[n, r])
                for n, (src, dst) in enumerate(((a16, an3), (m16, mn3), (l16, ln3)))]

    @pl.when(step == 0)
    def _():
        k16[...] = jnp.zeros_like(k16)
        v16[...] = jnp.zeros_like(v16)
        all_residues((bi, hp, j), cur, "start")

    @pl.when(step + 1 < pl.num_programs(0) * n_hp * n_j)
    def _():
        nxt = step + 1
        all_residues((nxt // (n_hp * n_j), (nxt // n_j) % n_hp, nxt % n_j), 1 - cur, "start")

    qf[...] = q_ref[0] * (HEAD_DIM ** -0.5)
    for i, (kr, vr) in enumerate(((k0, v0), (k1, v1), (k2, v2), (k3, v3))):
        kf[i * ATTN_HALO:(i + 1) * ATTN_HALO, :] = kr[0]
        vf[i * ATTN_HALO:(i + 1) * ATTN_HALO, :] = vr[0]

    head_a = lax.broadcasted_iota(I32, (ATTN_TQ, LANE), 1) < HEAD_DIM
    off = lax.broadcasted_iota(I32, (ATTN_TQ, ATTN_TK), 1) - lax.broadcasted_iota(I32, (ATTN_TQ, ATTN_TK), 0)
    band = (off >= 0) & (off <= 2 * HALF_STEPS)
    band2 = jnp.concatenate([band, band], axis=0)
    colpos = lax.broadcasted_iota(I32, (1, ATTN_TK), 1)

    def local_softmax(q, kt, vt, first_pos, d):
        pos = first_pos + d * colpos
        valid = band2 & (pos >= 0) & (pos < seq)
        q2 = jnp.concatenate([jnp.where(head_a, q, 0.0), jnp.where(head_a, 0.0, q)], axis=0).astype(BF16)
        s = jnp.where(valid, _dot_nt(q2, kt.astype(BF16)), -jnp.inf)
        m = jnp.max(s, axis=1, keepdims=True)
        p = jnp.exp(s - m)
        l = jnp.sum(p, axis=1, keepdims=True)
        n = _dot(p.astype(BF16), vt.astype(BF16))
        return (jnp.where(head_a, m[:ATTN_TQ], m[ATTN_TQ:]), jnp.where(head_a, l[:ATTN_TQ], l[ATTN_TQ:]),
                jnp.where(head_a, n[:ATTN_TQ], n[ATTN_TQ:]))

    def visit(d, i, first):
        log_d = d.bit_length() - 1
        r = i & (d - 1)
        t = i >> log_d
        qs = r + d * ATTN_TQ * t
        ks = ATTN_HALO + r + d * (ATTN_TQ * t - HALF_STEPS)
        m_loc, l_loc, n_loc = local_softmax(qf[pl.ds(qs, ATTN_TQ, stride=d), :], kf[pl.ds(ks, ATTN_TK, stride=d), :],
                                            vf[pl.ds(ks, ATTN_TK, stride=d), :], p0 - ATTN_HALO + ks, d)
        rows = pl.ds(qs, ATTN_TQ, stride=d)
        if first:
            mst[rows, :], lst[rows, :], acc[rows, :] = m_loc, l_loc, n_loc
        else:
            m_old = mst[rows, :]
            m_new = jnp.maximum(m_old, m_loc)
            w_old = jnp.exp(m_old - m_new)
            w_loc = jnp.exp(m_loc - m_new)
            mst[rows, :] = m_new
            lst[rows, :] = lst[rows, :] * w_old + l_loc * w_loc
            acc[rows, :] = acc[rows, :] * w_old + n_loc * w_loc

    def visit_dmax(r):
        q = q16[cur, r] * (HEAD_DIM ** -0.5)
        m16[r], l16[r], a16[r] = local_softmax(q, k16[cur, r], v16[cur, r], p0 - ATTN_HALO + r, dmax)

    def loop(n, fn):
        def body(i, c):
            for u in range(ATTN_UNROLL):
                fn(i * ATTN_UNROLL + u)
            return c
        lax.fori_loop(0, n // ATTN_UNROLL, body, 0)

    n_visits = ATTN_SUPER // ATTN_TQ
    loop(n_visits, lambda i: visit(PATTERNS[0][1], i, True))
    all_residues((bi, hp, j), cur, "wait")
    loop(dmax, visit_dmax)
    for r in range(dmax):
        for cp in state_copies(r):
            cp.start()
    for _, d in PATTERNS[1:-1]:
        loop(n_visits, lambda i, d=d: visit(d, i, False))
    for r in range(dmax):
        for cp in state_copies(r):
            cp.wait()

    def finish(i, c):
        rows = pl.ds(pl.multiple_of(i * ATTN_TQ, ATTN_TQ), ATTN_TQ)
        slabs = pl.ds(pl.multiple_of(i * (ATTN_TQ // dmax), ATTN_TQ // dmax), ATTN_TQ // dmax)
        a_a, m_a, l_a = (t[slabs].reshape(ATTN_TQ, LANE) for t in (an3, mn3, ln3))
        m_b = mst[rows, :]
        m = jnp.maximum(m_a, m_b)
        w_a, w_b = jnp.exp(m_a - m), jnp.exp(m_b - m)
        o = (a_a * w_a + acc[rows, :] * w_b) / (l_a * w_a + lst[rows, :] * w_b)
        o_ref[0, rows, :] = o.astype(BF16)
        return c

    lax.fori_loop(0, n_visits, finish, 0)


def _attention(q, k, v):
    b, seq, _ = q.shape
    nblk = seq // ATTN_HALO
    ratio = ATTN_SUPER // ATTN_HALO
    dmax = PATTERNS[-1][1]
    assert ATTN_SUPER // dmax == ATTN_TQ and 4 * ATTN_HALO // dmax == ATTN_TK
    assert seq // ATTN_SUPER >= 2

    def halo(i):
        return pl.BlockSpec((1, ATTN_HALO, LANE),
                            lambda bi, hp, j: (bi, jnp.clip(ratio * j - 1 + i, 0, nblk - 1), hp))

    main = pl.BlockSpec((1, ATTN_SUPER, LANE), lambda bi, hp, j: (bi, j, hp))
    hbm = pl.BlockSpec(memory_space=pl.ANY)
    by_residue = lambda t: t.reshape(b, seq // dmax, dmax, D_MODEL)
    return pl.pallas_call(
        functools.partial(_attn_kernel, seq=seq),
        grid=(b, D_MODEL // LANE, seq // ATTN_SUPER),
        in_specs=[main] + [halo(i) for i in range(4)] * 2 + [hbm] * 3,
        out_specs=main,
        out_shape=jax.ShapeDtypeStruct((b, seq, D_MODEL), BF16),
        scratch_shapes=[pltpu.VMEM((ATTN_SUPER, LANE), F32), pltpu.VMEM((4 * ATTN_HALO, LANE), F32),
                        pltpu.VMEM((4 * ATTN_HALO, LANE), F32), pltpu.VMEM((2, dmax, ATTN_TQ, LANE), F32),
                        pltpu.VMEM((2, dmax, ATTN_TK, LANE), F32), pltpu.VMEM((2, dmax, ATTN_TK, LANE), F32)]
                       + [pltpu.VMEM((ATTN_SUPER, LANE), F32)] * 3
                       + [pltpu.VMEM((dmax, ATTN_TQ, LANE), F32)] * 3
                       + [pltpu.VMEM((ATTN_TQ, dmax, LANE), F32)] * 3
                       + [pltpu.SemaphoreType.DMA((3, dmax)), pltpu.SemaphoreType.DMA((2, 3, dmax))],
        compiler_params=_cparams(3), name="dilated_attn",
    )(q, k, k, k, k, v, v, v, v, by_residue(q), by_residue(k), by_residue(v))


def _shifted(cur, prev8, next8, s, first, last):
    n = cur.shape[0]
    if s == 0:
        return cur
    rows = lax.broadcasted_iota(I32, cur.shape, 0)
    out = pltpu.roll(cur, (-s) % n, 0)
    if s < 0:
        for j in range(-s):
            src = jnp.where(first, 0.0, prev8[8 + s + j:8 + s + j + 1, :])
            out = jnp.where(rows == j, src, out)
    else:
        for j in range(s):
            src = jnp.where(last, 0.0, next8[j:j + 1, :])
            out = jnp.where(rows == n - s + j, src, out)
    return out


def _ssd_conv_kernel(c_ref, p_ref, n_ref, w_ref, b_ref, o_ref):
    first = pl.program_id(1) == 0
    last = pl.program_id(1) == pl.num_programs(1) - 1
    for c in range(SSD_XBC // 256):
        sl = slice(c * 256, (c + 1) * 256)
        cur, prev8, next8 = c_ref[0, :, sl], p_ref[0, :, sl], n_ref[0, :, sl]
        y = b_ref[:, sl]
        for j in range(SSD_CONV):
            y = y + _shifted(cur, prev8, next8, j - SSD_CONV // 2, first, last) * w_ref[j:j + 1, sl]
        o_ref[0, :, sl] = _silu(y)


def _halo_specs(tm, width, seq):
    cur = pl.BlockSpec((1, tm, width), lambda b, i: (b, i, 0))
    prev = pl.BlockSpec((1, 8, width), lambda b, i: (b, jnp.maximum(i * (tm // 8) - 1, 0), 0))
    nxt = pl.BlockSpec((1, 8, width), lambda b, i: (b, jnp.minimum((i + 1) * (tm // 8), seq // 8 - 1), 0))
    return cur, prev, nxt


def _ssd_conv(xbc, w, bias):
    b, seq, width = xbc.shape
    tm = 512
    cur, prev, nxt = _halo_specs(tm, width, seq)
    return pl.pallas_call(
        _ssd_conv_kernel, grid=(b, seq // tm),
        in_specs=[cur, prev, nxt, pl.BlockSpec((8, width), lambda b, i: (0, 0)),
                  pl.BlockSpec((1, width), lambda b, i: (0, 0))],
        out_specs=cur, out_shape=jax.ShapeDtypeStruct(xbc.shape, F32),
        compiler_params=_cparams(2), name="ssd_conv",
    )(xbc, xbc, xbc, w, bias)


def _softplus(x):
    return jnp.maximum(x, 0.0) + jnp.log1p(jnp.exp(-jnp.abs(x)))


def _tri(kind):
    s = lax.broadcasted_iota(I32, (CHUNK, CHUNK), 0)
    l = lax.broadcasted_iota(I32, (CHUNK, CHUNK), 1)
    return {"le": s <= l, "ge": s >= l, "lt": s < l}[kind]


def _expand(cols, e2_ref):
    hi, lo = _split2(cols)
    return _dot(jnp.concatenate([hi, lo], axis=1), e2_ref[...])


def _ssd_bwd_kernel(xs_ref, b_ref, dt_ref, pc_ref, e2_ref, sb_ref, st):
    @pl.when(pl.program_id(1) == 0)
    def _():
        st[...] = jnp.zeros_like(st)

    for c in reversed(range(SSD_CPS)):
        _ssd_bwd_chunk(slice(c * CHUNK, (c + 1) * CHUNK), c, xs_ref, b_ref, dt_ref, pc_ref, e2_ref, sb_ref, st)


def _ssd_bwd_chunk(rs, c, xs_ref, b_ref, dt_ref, pc_ref, e2_ref, sb_ref, st):
    sb_ref[0, c] = st[...].astype(BF16)
    dt_t = dt_ref[0, rs, :].T
    dtb = _softplus(dt_t[N_HEADS:2 * N_HEADS, :] + pc_ref[:, 1:2])
    a = dtb * pc_ref[:, 3:4]
    ex = _dot3(a, _tri("lt").astype(BF16))
    tot = ex[:, CHUNK - 1:CHUNK] + a[:, CHUNK - 1:CHUNK]
    rowform = jnp.concatenate([dtb * jnp.exp(ex), jnp.broadcast_to(jnp.exp(tot), (N_HEADS, CHUNK)),
                               jnp.zeros((CHUNK - 2 * N_HEADS, CHUNK), F32)], axis=0)
    ex2 = _expand(rowform.T, e2_ref)
    xw = (xs_ref[0, rs, :] * ex2[:, :D_MODEL]).astype(BF16)
    half = D_MODEL // SSD_GROUPS
    upd = [_dot(b_ref[0, rs, g * SSD_STATE:(g + 1) * SSD_STATE].T.astype(BF16), xw[:, g * half:(g + 1) * half])
           for g in range(SSD_GROUPS)]
    st[...] = st[...] * ex2[0:1, D_MODEL:] + jnp.concatenate(upd, axis=1)


def _ssd_fwd_kernel(xs_ref, b_ref, c_ref, dt_ref, z_ref, sb_ref, pc_ref, e3_ref, dexp_ref, on_ref, o_ref, st):
    @pl.when(pl.program_id(1) == 0)
    def _():
        st[...] = jnp.zeros_like(st)

    for c in range(SSD_CPS):
        _ssd_fwd_chunk(slice(c * CHUNK, (c + 1) * CHUNK), c, xs_ref, b_ref, c_ref, dt_ref, z_ref, sb_ref, pc_ref,
                       e3_ref, dexp_ref, on_ref, o_ref, st)


def _ssd_fwd_chunk(rs, c, xs_ref, b_ref, c_ref, dt_ref, z_ref, sb_ref, pc_ref, e3_ref, dexp_ref, on_ref, o_ref, st):
    xs = xs_ref[0, rs, :]
    dt_t = dt_ref[0, rs, :].T
    dtf = _softplus(dt_t[0:N_HEADS, :] + pc_ref[:, 0:1])
    dtb = _softplus(dt_t[N_HEADS:2 * N_HEADS, :] + pc_ref[:, 1:2])
    af = dtf * pc_ref[:, 2:3]
    ab = dtb * pc_ref[:, 3:4]
    csf = _dot3(af, _tri("le").astype(BF16))
    rcs = _dot3(ab, _tri("ge").astype(BF16))
    totf = csf[:, CHUNK - 1:CHUNK]
    rowform = jnp.concatenate([dtf * jnp.exp(totf - csf), jnp.exp(csf), jnp.exp(rcs), csf, rcs,
                               jnp.zeros((CHUNK - 5 * N_HEADS, CHUNK), F32)], axis=0)
    cols = rowform.T
    ex3 = _expand(cols, e3_ref)
    w_state, e_f, e_b = ex3[:, :D_MODEL], ex3[:, D_MODEL:2 * D_MODEL], ex3[:, 2 * D_MODEL:]

    xb = xs.astype(BF16)
    lower, upper = _tri("ge"), _tri("le")
    head_a = lax.broadcasted_iota(I32, (CHUNK, LANE), 1) < HEAD_DIM
    half = D_MODEL // SSD_GROUPS
    hpg = N_HEADS // SSD_GROUPS
    st_all = st[...]
    sb_all = sb_ref[0, c]
    ys = []
    b_t = []
    for g in range(SSD_GROUPS):
        bg = b_ref[0, rs, g * SSD_STATE:(g + 1) * SSD_STATE]
        cg = c_ref[0, rs, g * SSD_STATE:(g + 1) * SSD_STATE].astype(BF16)
        b_t.append(bg.T.astype(BF16))
        gm = _dot_nt(cg, bg.astype(BF16))
        states = jnp.concatenate([st_all[:, g * half:(g + 1) * half].astype(BF16),
                                  sb_all[:, g * half:(g + 1) * half]], axis=1)
        off = _dot(cg, states)
        y_off = (off[:, :half] * e_f[:, g * half:(g + 1) * half]
                 + off[:, half:] * e_b[:, g * half:(g + 1) * half])
        for pair in range(hpg // 2):
            ms = []
            for h in (g * hpg + 2 * pair, g * hpg + 2 * pair + 1):
                dec_f = jnp.where(lower, jnp.exp(cols[:, 3 * N_HEADS + h:3 * N_HEADS + h + 1] - csf[h:h + 1, :]), 0.0)
                dec_b = jnp.where(upper, jnp.exp(cols[:, 4 * N_HEADS + h:4 * N_HEADS + h + 1] - rcs[h:h + 1, :]), 0.0)
                ms.append((gm * (dec_f * dtf[h:h + 1, :] + dec_b * dtb[h:h + 1, :])).astype(BF16))
            lo = g * half + pair * LANE
            xp = xb[:, lo:lo + LANE]
            ys.append(jnp.where(head_a, _dot(ms[0], xp), _dot(ms[1], xp))
                      + y_off[:, pair * LANE:(pair + 1) * LANE])
    y = jnp.concatenate(ys, axis=1) + xs * dexp_ref[...]
    yz = y * _silu(z_ref[0, rs, :])
    o_ref[0, rs, :] = _rms(yz, on_ref[...]).astype(BF16)

    xw = (xs * w_state).astype(BF16)
    upd = [_dot(b_t[g], xw[:, g * half:(g + 1) * half]) for g in range(SSD_GROUPS)]
    st[...] = st_all * e_f[CHUNK - 1:CHUNK, :] + jnp.concatenate(upd, axis=1)


def _ssd(xbc_act, dt, z, pc, dexp, out_norm):
    b, seq, _ = xbc_act.shape
    nc = seq // CHUNK
    sel = np.zeros((2 * CHUNK, 3 * D_MODEL), np.float32)
    for part in range(3):
        for h in range(N_HEADS):
            for rep in range(2):
                sel[rep * CHUNK + part * N_HEADS + h, part * D_MODEL + h * HEAD_DIM:part * D_MODEL + (h + 1) * HEAD_DIM] = 1.0
    e3 = jnp.asarray(sel, BF16)
    e2 = jnp.asarray(sel[:, :2 * D_MODEL], BF16)

    rows = SSD_CPS * CHUNK
    nblk = seq // rows
    bcol = D_MODEL // (2 * SSD_STATE)
    rev = lambda bi, c: (bi, nblk - 1 - c, 0)
    sb = pl.pallas_call(
        _ssd_bwd_kernel, grid=(b, nblk),
        in_specs=[pl.BlockSpec((1, rows, D_MODEL), rev),
                  pl.BlockSpec((1, rows, 2 * SSD_STATE), lambda bi, c: (bi, nblk - 1 - c, bcol)),
                  pl.BlockSpec((1, rows, LANE), rev), _const_spec(pc.shape), _const_spec(e2.shape)],
        out_specs=pl.BlockSpec((1, SSD_CPS, SSD_STATE, D_MODEL), lambda bi, c: (bi, nblk - 1 - c, 0, 0)),
        out_shape=jax.ShapeDtypeStruct((b, nc, SSD_STATE, D_MODEL), BF16),
        scratch_shapes=[pltpu.VMEM((SSD_STATE, D_MODEL), F32)],
        compiler_params=_cparams(2), name="ssd_bwd_state",
    )(xbc_act, xbc_act, dt, pc, e2)

    fwd = lambda bi, c: (bi, c, 0)
    return pl.pallas_call(
        _ssd_fwd_kernel, grid=(b, nblk),
        in_specs=[pl.BlockSpec((1, rows, D_MODEL), fwd),
                  pl.BlockSpec((1, rows, 2 * SSD_STATE), lambda bi, c: (bi, c, bcol)),
                  pl.BlockSpec((1, rows, 2 * SSD_STATE), lambda bi, c: (bi, c, bcol + 1)),
                  pl.BlockSpec((1, rows, LANE), fwd), pl.BlockSpec((1, rows, D_MODEL), fwd),
                  pl.BlockSpec((1, SSD_CPS, SSD_STATE, D_MODEL), lambda bi, c: (bi, c, 0, 0)),
                  _const_spec(pc.shape), _const_spec(e3.shape), _const_spec((1, D_MODEL)), _const_spec((1, D_MODEL))],
        out_specs=pl.BlockSpec((1, rows, D_MODEL), fwd),
        out_shape=jax.ShapeDtypeStruct((b, seq, D_MODEL), BF16),
        scratch_shapes=[pltpu.VMEM((SSD_STATE, D_MODEL), F32)],
        compiler_params=_cparams(2), name="ssd_fwd",
    )(xbc_act, xbc_act, xbc_act, dt, z, sb, pc, e3, dexp, out_norm)


def _norm_and_route(x1, fg_ref, wr_hi_ref, wr_lo_ref, hn_ref, aff_ref):
    hn = _rms(x1, fg_ref[...])
    hi, lo = _split2(hn)
    hn_ref[...] = hi
    logits = _dot(hi, wr_hi_ref[...]) + _dot(lo, wr_hi_ref[...]) + _dot(hi, wr_lo_ref[...])
    lt = logits.T[0:N_EXPERTS, :]
    e = jnp.exp(lt - jnp.max(lt, axis=0, keepdims=True))
    aff_ref[0] = e / jnp.sum(e, axis=0, keepdims=True)


def _out_even_kernel(x_ref, a_ref, y_ref, wa_ref, wy_ref, fg_ref, wr_hi_ref, wr_lo_ref, x1_ref, hn_ref, aff_ref):
    x1 = x_ref[...] + _dot(a_ref[...], wa_ref[...]) + _dot(y_ref[...], wy_ref[...])
    x1_ref[...] = x1
    _norm_and_route(x1, fg_ref, wr_hi_ref, wr_lo_ref, hn_ref, aff_ref)


def _out_odd_kernel(x_ref, gb_ref, cu_ref, cp_ref, cn_ref, cw_ref, wo_ref, fg_ref, wr_hi_ref, wr_lo_ref,
                    x1_ref, hn_ref, aff_ref, *, tiles_per_seq):
    i = pl.program_id(0) % tiles_per_seq
    first, last = i == 0, i == tiles_per_seq - 1
    cur, prev8, next8 = cu_ref[...], cp_ref[...], cn_ref[...]
    conv = sum(_shifted(cur, prev8, next8, j - SHORT_CONV // 2, first, last) * cw_ref[j:j + 1, :]
               for j in range(SHORT_CONV))
    x1 = x_ref[...] + _dot((gb_ref[...] * conv).astype(BF16), wo_ref[...])
    x1_ref[...] = x1
    _norm_and_route(x1, fg_ref, wr_hi_ref, wr_lo_ref, hn_ref, aff_ref)


def _route_outs(n, b, seq, tm):
    row = pl.BlockSpec((tm, D_MODEL), lambda i: (i, 0))
    aff = pl.BlockSpec((1, N_EXPERTS, tm), lambda i: (i // (seq // tm), 0, i % (seq // tm)))
    shapes = [jax.ShapeDtypeStruct((n, D_MODEL), F32), jax.ShapeDtypeStruct((n, D_MODEL), BF16),
              jax.ShapeDtypeStruct((b, N_EXPERTS, seq), F32)]
    return [row, row, aff], shapes


def _out_even(x, attn, y, wa, wy, fg, wr_hi, wr_lo, b, seq):
    n = x.shape[0]
    tm = OUT_ROW_TILE
    row = pl.BlockSpec((tm, D_MODEL), lambda i: (i, 0))
    out_specs, shapes = _route_outs(n, b, seq, tm)
    return pl.pallas_call(
        _out_even_kernel, grid=(n // tm,),
        in_specs=[row, row, row, _const_spec(wa.shape), _const_spec(wy.shape), _const_spec((1, D_MODEL)),
                  _const_spec(wr_hi.shape), _const_spec(wr_lo.shape)],
        out_specs=out_specs, out_shape=shapes, compiler_params=_cparams(1), name="out_even",
    )(x, attn, y, wa, wy, fg, wr_hi, wr_lo)


def _out_odd(x, gb, cu, cw, wo, fg, wr_hi, wr_lo, b, seq):
    n = x.shape[0]
    tm = OUT_ROW_TILE
    row = pl.BlockSpec((tm, D_MODEL), lambda i: (i, 0))
    prev = pl.BlockSpec((8, D_MODEL), lambda i: (jnp.maximum(i * (tm // 8) - 1, 0), 0))
    nxt = pl.BlockSpec((8, D_MODEL), lambda i: (jnp.minimum((i + 1) * (tm // 8), n // 8 - 1), 0))
    out_specs, shapes = _route_outs(n, b, seq, tm)
    return pl.pallas_call(
        functools.partial(_out_odd_kernel, tiles_per_seq=seq // tm), grid=(n // tm,),
        in_specs=[row, row, row, prev, nxt, pl.BlockSpec((8, D_MODEL), lambda i: (0, 0)), _const_spec(wo.shape),
                  _const_spec((1, D_MODEL)), _const_spec(wr_hi.shape), _const_spec(wr_lo.shape)],
        out_specs=out_specs, out_shape=shapes, compiler_params=_cparams(1), name="out_odd",
    )(x, gb, cu, cu, cu, cw, wo, fg, wr_hi, wr_lo)


def _in_odd_kernel(x_ref, g_ref, wb_ref, wc_ref, wu_ref, gb_ref, cu_ref):
    hb = _rms(x_ref[...], g_ref[...]).astype(BF16)
    gb_ref[...] = _dot(hb, wb_ref[...])
    cu_ref[...] = _dot(hb, wc_ref[...]) * _dot(hb, wu_ref[...])


def _in_odd(x, g, wb, wc, wu):
    n = x.shape[0]
    tm = ROW_TILE
    row = pl.BlockSpec((tm, D_MODEL), lambda i: (i, 0))
    return pl.pallas_call(
        _in_odd_kernel, grid=(n // tm,),
        in_specs=[row, _const_spec((1, D_MODEL)), _const_spec(wb.shape), _const_spec(wc.shape), _const_spec(wu.shape)],
        out_specs=[row, row], out_shape=[jax.ShapeDtypeStruct((n, D_MODEL), F32)] * 2,
        compiler_params=_cparams(1), name="in_odd",
    )(x, g, wb, wc, wu)


def _count(mask):
    return jnp.sum(jnp.sum(mask.astype(F32), axis=0, keepdims=True), axis=1, keepdims=True)


def _route_kernel(aff_ref, incl_ref, ones_ref, strict_ref, local_ref, group_ref, first_ref, slot_ref, off_ref, end_ref,
                  *, cap):
    def step(i, thrs):
        bit = jnp.int32(1) << (30 - i)
        out = []
        for e in range(N_EXPERTS):
            cand = thrs[e] | bit
            out.append(jnp.where(_count(pltpu.bitcast(aff_ref[0, e], I32) >= cand) >= cap, cand, thrs[e]))
        return tuple(out)

    thrs = lax.fori_loop(0, 31, step, tuple(jnp.zeros((1, 1), I32) for _ in range(N_EXPERTS)))
    for e in range(N_EXPERTS):
        bits = pltpu.bitcast(aff_ref[0, e], I32)
        thr = thrs[e]
        gt = bits > thr
        eq = (bits == thr).astype(BF16)
        eq_rank = _dot(eq, incl_ref[...]) + _dot(strict_ref[...], _dot(eq, ones_ref[...]).astype(BF16))
        sel = (gt | ((bits == thr) & (eq_rank <= cap - _count(gt)))).astype(BF16)
        within = _dot(sel, incl_ref[...])
        totals = _dot(sel, ones_ref[...]).astype(BF16)
        local = _dot(local_ref[...], totals)
        cnt = _dot(group_ref[...], totals)
        padded = jnp.floor((cnt + (SLOT_ALIGN - 1)) * (1.0 / SLOT_ALIGN)) * SLOT_ALIGN
        start = _dot(first_ref[...], padded.astype(BF16))
        slot_ref[0, e] = jnp.where(sel > 0, (start + local + within).astype(I32) - 1, -1)
        off_ref[0, e:e + 1, :] = start.T[0:1, :].astype(I32)
        end_ref[0, e:e + 1, :] = (start + padded).T[0:1, :].astype(I32)


def _route(aff, cap):
    b, _, seq = aff.shape
    nt = seq // LANE
    tri = np.arange(LANE)
    tt = np.arange(nt)
    grp = tt // SUB
    as_bf16 = lambda m: jnp.asarray(m, BF16)
    incl = as_bf16(tri[:, None] <= tri[None, :])
    strict = as_bf16(tt[None, :] < tt[:, None])
    local = as_bf16((tt[None, :] < tt[:, None]) & (grp[None, :] == grp[:, None]))
    group = as_bf16(grp[None, :] == grp[:, None])
    first = as_bf16((grp[None, :] < grp[:, None]) & (tt[None, :] % SUB == 0))
    ones = jnp.ones((LANE, LANE), BF16)
    tiles = pl.BlockSpec((1, N_EXPERTS, nt, LANE), lambda i: (i, 0, 0, 0))
    rows = pl.BlockSpec((1, N_EXPERTS, nt), lambda i: (i, 0, 0))
    return pl.pallas_call(
        functools.partial(_route_kernel, cap=cap), grid=(b,),
        in_specs=[tiles, _const_spec((LANE, LANE)), _const_spec((LANE, LANE))] + [_const_spec((nt, nt))] * 4,
        out_specs=[tiles, rows, rows],
        out_shape=[jax.ShapeDtypeStruct((b, N_EXPERTS, nt, LANE), I32)] + [jax.ShapeDtypeStruct((b, N_EXPERTS, nt), I32)] * 2,
        compiler_params=_cparams(1), name="route",
    )(aff.reshape(b, N_EXPERTS, nt, LANE), incl, ones, strict, local, group, first)


def _one_hot(slots, base):
    return (slots == lax.broadcasted_iota(I32, (WINDOW, slots.shape[1]), 0) + base).astype(BF16)


def _gather_kernel(off_ref, end_ref, slot_ref, aff_ref, hn_ref, xe_hbm, stage, extra, sem, xsem):
    bi, j = pl.program_id(0), pl.program_id(1)
    n_j = pl.num_programs(1)
    step = bi * n_j + j
    cur = step % 2

    def window_copy(bb, jj, e, buf):
        start = pl.multiple_of(off_ref[bb, e, jj * SUB], SLOT_ALIGN)
        return pltpu.make_async_copy(stage.at[buf, e], xe_hbm.at[bb, e, pl.ds(start, WINDOW), :], sem.at[buf, e])

    tokens = hn_ref[0]
    parts = [p.astype(F32) for p in _split3(aff_ref[0, 0])]
    gates = jnp.concatenate(parts + [jnp.zeros((LANE - 3 * N_EXPERTS, TOK_TILE), F32)], axis=0).astype(BF16)

    def rows_of(p):
        return jnp.concatenate([_dot(p, tokens), _dot_nt(p, gates)], axis=1).astype(BF16)

    p_all = jnp.concatenate([_one_hot(slot_ref[0, 0, e:e + 1, :], off_ref[bi, e, j * SUB])
                             for e in range(N_EXPERTS)], axis=0)
    stage[cur] = rows_of(p_all).reshape(N_EXPERTS, WINDOW, D_MODEL + LANE)

    @pl.when(j > 0)
    def _():
        for e in range(N_EXPERTS):
            window_copy(bi, j - 1, e, 1 - cur).wait()

    for e in range(N_EXPERTS):
        window_copy(bi, j, e, cur).start()

    def overflow(e, c):
        first = off_ref[bi, e, j * SUB]
        n_win = (end_ref[bi, e, j * SUB] - first + WINDOW - 1) // WINDOW
        slots = slot_ref[0, 0, pl.ds(e, 1), :]

        def one(w, c2):
            base = pl.multiple_of(first + w * WINDOW, SLOT_ALIGN)
            extra[...] = rows_of(_one_hot(slots, base))
            cp = pltpu.make_async_copy(extra, xe_hbm.at[bi, e, pl.ds(base, WINDOW), :], xsem.at[0])
            cp.start()
            cp.wait()
            return c2

        return lax.fori_loop(1, n_win, one, c)

    lax.fori_loop(0, N_EXPERTS, overflow, 0)

    @pl.when(j == n_j - 1)
    def _():
        for e in range(N_EXPERTS):
            window_copy(bi, j, e, cur).wait()
        extra[...] = jnp.zeros_like(extra)
        cap_pad = xe_hbm.shape[2]

        def fill(e, c):
            used = end_ref[bi, e, end_ref.shape[2] - 1]
            n_big = (cap_pad - used) // WINDOW
            small0 = used + n_big * WINDOW
            n_small = (cap_pad - small0) // SLOT_ALIGN
            big = lambda i: pltpu.make_async_copy(
                extra, xe_hbm.at[bi, e, pl.ds(pl.multiple_of(used + i * WINDOW, SLOT_ALIGN), WINDOW), :], xsem.at[0])
            small = lambda i: pltpu.make_async_copy(
                extra.at[0:SLOT_ALIGN],
                xe_hbm.at[bi, e, pl.ds(pl.multiple_of(small0 + i * SLOT_ALIGN, SLOT_ALIGN), SLOT_ALIGN), :], xsem.at[0])
            for n, mk in ((n_big, big), (n_small, small)):
                lax.fori_loop(0, n, lambda i, c2, mk=mk: (mk(i).start(), c2)[1], 0)
            for n, mk in ((n_big, big), (n_small, small)):
                lax.fori_loop(0, n, lambda i, c2, mk=mk: (mk(i).wait(), c2)[1], 0)
            return c

        lax.fori_loop(0, N_EXPERTS, fill, 0)


def _gather(off, end, slot_t, aff_t, hn):
    b, seq, _ = hn.shape
    cap_pad = _cap_pad(seq)
    width = D_MODEL + LANE
    per_tile = pl.BlockSpec((1, 1, N_EXPERTS, TOK_TILE), lambda bi, j, *_: (bi, j, 0, 0))
    return pl.pallas_call(
        _gather_kernel,
        grid_spec=pltpu.PrefetchScalarGridSpec(
            num_scalar_prefetch=2, grid=(b, seq // TOK_TILE),
            in_specs=[per_tile, per_tile, pl.BlockSpec((1, TOK_TILE, D_MODEL), lambda bi, j, *_: (bi, j, 0))],
            out_specs=pl.BlockSpec(memory_space=pl.ANY),
            scratch_shapes=[pltpu.VMEM((2, N_EXPERTS, WINDOW, width), BF16), pltpu.VMEM((WINDOW, width), BF16),
                            pltpu.SemaphoreType.DMA((2, N_EXPERTS)), pltpu.SemaphoreType.DMA((1,))]),
        out_shape=jax.ShapeDtypeStruct((b, N_EXPERTS, cap_pad, width), BF16),
        compiler_params=_cparams(2), name="moe_gather",
    )(off, end, slot_t, aff_t, hn)


def _ffn_kernel(end_ref, xe_ref, wg_hbm, wu_hbm, wd_hbm, y_ref, stage_g, stage_u, stage_d, wg, wu, wd, sem, *, layer):
    e, bi, r = pl.program_id(0), pl.program_id(1), pl.program_id(2)
    used = end_ref[bi, e, end_ref.shape[2] - 1]
    pairs = ((wg_hbm, stage_g, wg), (wu_hbm, stage_u, wu), (wd_hbm, stage_d, wd))

    def weight_copies(ee):
        return [pltpu.make_async_copy(src.at[layer, ee], stg, sem.at[k]) for k, (src, stg, _) in enumerate(pairs)]

    @pl.when((bi == 0) & (r == 0))
    def _():
        @pl.when(e == 0)
        def _():
            for cp in weight_copies(e):
                cp.start()

        for cp, (_, stg, dst) in zip(weight_copies(e), pairs):
            cp.wait()
            n_rows = stg.shape[0]

            def cast(i, c, stg=stg, dst=dst):
                rows = pl.ds(pl.multiple_of(i * CAST_ROWS, CAST_ROWS), CAST_ROWS)
                dst[rows, :] = stg[rows, :].astype(BF16)
                return c

            lax.fori_loop(0, n_rows // CAST_ROWS, cast, 0)

        @pl.when(e + 1 < pl.num_programs(0))
        def _():
            for cp in weight_copies(e + 1):
                cp.start()

    @pl.when(r * FFN_ROWS < used)
    def _():
        xe = xe_ref[0, 0, :, 0:D_MODEL]
        hid = (_silu(_dot(xe, wg[...])) * _dot(xe, wu[...])).astype(BF16)
        g = xe_ref[0, 0, :, D_MODEL:].astype(F32)
        lane = lax.broadcasted_iota(I32, g.shape, 1)
        mine = (lane % N_EXPERTS == e) & (lane < 3 * N_EXPERTS)
        gate = jnp.sum(jnp.where(mine, g, 0.0), axis=1, keepdims=True)
        y_ref[0, 0] = (_dot(hid, wd[...]) * gate).astype(BF16)

    @pl.when(r * FFN_ROWS >= used)
    def _():
        y_ref[...] = jnp.zeros_like(y_ref)


def _ffn(end, xe, wg, wu, wd, layer):
    b, ne, cap_pad, width = xe.shape
    rows = lambda w: pl.BlockSpec((1, 1, FFN_ROWS, w), lambda e, bi, r, *_: (bi, e, r, 0))
    hbm = pl.BlockSpec(memory_space=pl.ANY)
    return pl.pallas_call(
        functools.partial(_ffn_kernel, layer=layer),
        grid_spec=pltpu.PrefetchScalarGridSpec(
            num_scalar_prefetch=1, grid=(ne, b, cap_pad // FFN_ROWS),
            in_specs=[rows(width), hbm, hbm, hbm],
            out_specs=rows(D_MODEL),
            scratch_shapes=[pltpu.VMEM((D_MODEL, D_FF), F32), pltpu.VMEM((D_MODEL, D_FF), F32),
                            pltpu.VMEM((D_FF, D_MODEL), F32), pltpu.VMEM((D_MODEL, D_FF), BF16),
                            pltpu.VMEM((D_MODEL, D_FF), BF16), pltpu.VMEM((D_FF, D_MODEL), BF16),
                            pltpu.SemaphoreType.DMA((3,))]),
        out_shape=jax.ShapeDtypeStruct((b, ne, cap_pad, D_MODEL), BF16),
        compiler_params=_cparams(3), name="moe_ffn",
    )(end, xe, wg, wu, wd)


def _combine_kernel(off_ref, end_ref, slot_ref, y_hbm, x1_ref, o_ref, win, extra, sem, xsem):
    bi, j = pl.program_id(0), pl.program_id(1)
    n_j = pl.num_programs(1)
    step = bi * n_j + j
    cur = step % 2

    def window_copy(bb, jj, e, buf):
        start = pl.multiple_of(off_ref[bb, e, jj * SUB], SLOT_ALIGN)
        return pltpu.make_async_copy(y_hbm.at[bb, e, pl.ds(start, WINDOW), :], win.at[buf, e], sem.at[buf, e])

    @pl.when(step == 0)
    def _():
        for e in range(N_EXPERTS):
            window_copy(bi, j, e, cur).start()

    @pl.when(step + 1 < pl.num_programs(0) * n_j)
    def _():
        nxt = step + 1
        for e in range(N_EXPERTS):
            window_copy(nxt // n_j, nxt % n_j, e, 1 - cur).start()

    ps = []
    for e in range(N_EXPERTS):
        window_copy(bi, j, e, cur).wait()
        ps.append(_one_hot(slot_ref[0, 0, e:e + 1, :], off_ref[bi, e, j * SUB]))
    p_all = jnp.concatenate(ps, axis=0)
    y_all = win[cur].reshape(N_EXPERTS * WINDOW, D_MODEL)
    o_ref[0] = x1_ref[0] + _dot_tn(p_all, y_all)

    def overflow(e, c):
        first = off_ref[bi, e, j * SUB]
        n_win = (end_ref[bi, e, j * SUB] - first + WINDOW - 1) // WINDOW
        slots = slot_ref[0, 0, pl.ds(e, 1), :]

        def one(w, c2):
            base = pl.multiple_of(first + w * WINDOW, SLOT_ALIGN)
            cp = pltpu.make_async_copy(y_hbm.at[bi, e, pl.ds(base, WINDOW), :], extra, xsem.at[0])
            cp.start()
            cp.wait()
            o_ref[0] += _dot_tn(_one_hot(slots, base), extra[...])
            return c2

        return lax.fori_loop(1, n_win, one, c)

    lax.fori_loop(0, N_EXPERTS, overflow, 0)


def _combine(off, end, slot_t, y, x1):
    b, seq, _ = x1.shape
    tile = pl.BlockSpec((1, TOK_TILE, D_MODEL), lambda bi, j, *_: (bi, j, 0))
    return pl.pallas_call(
        _combine_kernel,
        grid_spec=pltpu.PrefetchScalarGridSpec(
            num_scalar_prefetch=2, grid=(b, seq // TOK_TILE),
            in_specs=[pl.BlockSpec((1, 1, N_EXPERTS, TOK_TILE), lambda bi, j, *_: (bi, j, 0, 0)),
                      pl.BlockSpec(memory_space=pl.ANY), tile],
            out_specs=tile,
            scratch_shapes=[pltpu.VMEM((2, N_EXPERTS, WINDOW, D_MODEL), BF16), pltpu.VMEM((WINDOW, D_MODEL), BF16),
                            pltpu.SemaphoreType.DMA((2, N_EXPERTS)), pltpu.SemaphoreType.DMA((1,))]),
        out_shape=jax.ShapeDtypeStruct((b, seq, D_MODEL), F32),
        compiler_params=_cparams(2), name="moe_combine",
    )(off, end, slot_t, y, x1)


def _cap_pad(seq):
    cap = CAPACITY_FACTOR * seq // N_EXPERTS
    worst = cap + (seq // TOK_TILE) * (SLOT_ALIGN - 1) + WINDOW
    return -(-worst // FFN_ROWS) * FFN_ROWS


def _moe(x1, hn, aff, wg, wu, wd, layer):
    b, seq, _ = x1.shape
    cap = CAPACITY_FACTOR * seq // N_EXPERTS
    slot, off, end = _route(aff, cap)
    per_tile = lambda a: jnp.swapaxes(a.reshape(b, N_EXPERTS, seq // TOK_TILE, TOK_TILE), 1, 2)
    slot_t = per_tile(slot)
    xe = _gather(off, end, slot_t, per_tile(aff), hn)
    y = _ffn(end, xe, wg, wu, wd, layer)
    return _combine(off, end, slot_t, y, x1)


def _rope_tables(seq):
    inv_freq = ROPE_THETA ** (-jnp.arange(ROPE_HALF, dtype=F32) * 2.0 / (2 * ROPE_HALF))
    ang = jnp.arange(seq, dtype=F32)[:, None] * inv_freq[None, :]
    cos, sin = jnp.cos(ang), jnp.sin(ang)
    z = lambda w: jnp.zeros((seq, w), F32)
    rest = HEAD_DIM - 2 * ROPE_HALF
    rc = jnp.concatenate([cos, cos, jnp.ones((seq, rest), F32)], axis=1)
    r1 = jnp.concatenate([-sin, z(ROPE_HALF + rest)], axis=1)
    r2 = jnp.concatenate([z(ROPE_HALF), sin, z(rest)], axis=1)
    return tuple(jnp.tile(t, (1, LANE // HEAD_DIM)) for t in (rc, r1, r2))


def _router_split(w):
    wp = jnp.pad(w, ((0, 0), (0, LANE - N_EXPERTS)))
    hi = wp.astype(BF16)
    return hi, (wp - hi.astype(F32)).astype(BF16)


def kernel(x, attn_norm, w_in_even, q_norm, k_norm, ssd_conv_w, ssd_conv_b, ssd_a_log_fwd, ssd_a_log_bwd,
           ssd_dt_bias_fwd, ssd_dt_bias_bwd, ssd_d, ssd_out_norm, w_out_even, conv_norm, conv_w_in, conv_w,
           conv_w_out, ffn_norm, router_w, expert_w_gate, expert_w_up, expert_w_down):
    b, seq, _ = x.shape
    n = b * seq
    depth = ffn_norm.shape[0]
    rc, r1, r2 = _rope_tables(seq)
    blk = np.arange(256) // HEAD_DIM
    bd = jnp.asarray((blk[:, None] == blk[None, :]) / HEAD_DIM, BF16)
    row = lambda v: v.reshape(1, -1).astype(F32)

    xf = x.reshape(n, D_MODEL)
    for layer in range(depth):
        i = layer // 2
        wr_hi, wr_lo = _router_split(router_w[layer])
        fg = row(ffn_norm[layer])
        if layer % 2 == 0:
            w = w_in_even[i].astype(BF16)
            o = np.cumsum([0, D_MODEL, D_MODEL, D_MODEL, D_MODEL, SSD_XBC, N_HEADS, N_HEADS])
            wq, wk, wv, wz, wx = (w[:, o[j]:o[j + 1]] for j in range(5))
            wd = jnp.pad(w[:, o[5]:o[7]], ((0, 0), (0, LANE - 2 * N_HEADS)))
            tile_heads = lambda g: row(jnp.tile(g, N_HEADS))
            q, k, v, z, xbc, dt = _in_even(xf, row(attn_norm[i]), wq, wk, wv, wz, wx, wd,
                                           tile_heads(q_norm[i]), tile_heads(k_norm[i]), bd, rc, r1, r2, seq)
            as3 = lambda t: t.reshape(b, seq, -1)
            attn = _attention(as3(q), as3(k), as3(v))
            cw = jnp.pad(ssd_conv_w[i], ((0, 8 - SSD_CONV), (0, 0)))
            act = _ssd_conv(as3(xbc), cw, row(ssd_conv_b[i]))
            pc = jnp.pad(jnp.stack([ssd_dt_bias_fwd[i], ssd_dt_bias_bwd[i], -jnp.exp(ssd_a_log_fwd[i]),
                                    -jnp.exp(ssd_a_log_bwd[i])], axis=1).astype(F32), ((0, 0), (0, LANE - 4)))
            y = _ssd(act, as3(dt), as3(z), pc, row(jnp.repeat(ssd_d[i], HEAD_DIM)), row(ssd_out_norm[i]))
            wo = w_out_even[i].astype(BF16)
            x1, hn, aff = _out_even(xf, attn.reshape(n, D_MODEL), y.reshape(n, D_MODEL), wo[:D_MODEL], wo[D_MODEL:],
                                    fg, wr_hi, wr_lo, b, seq)
        else:
            w = conv_w_in[i].astype(BF16)
            gb, cu = _in_odd(xf, row(conv_norm[i]), w[:, :D_MODEL], w[:, D_MODEL:2 * D_MODEL], w[:, 2 * D_MODEL:])
            cw = jnp.pad(conv_w[i], ((0, 8 - SHORT_CONV), (0, 0)))
            x1, hn, aff = _out_odd(xf, gb, cu, cw, conv_w_out[i].astype(BF16), fg, wr_hi, wr_lo, b, seq)
        xf = _moe(x1.reshape(b, seq, D_MODEL), hn.reshape(b, seq, D_MODEL), aff, expert_w_gate, expert_w_up,
                  expert_w_down, layer).reshape(n, D_MODEL)
    return xf.reshape(b, seq, D_MODEL)
```

```python
import functools
import math

import jax
import jax.numpy as jnp
import numpy as np
from jax import lax
from jax.experimental import pallas as pl
from jax.experimental.pallas import tpu as pltpu

F32, BF16, I32 = jnp.float32, jnp.bfloat16, jnp.int32

D_MODEL = 1024
N_HEADS = 16
HEAD_DIM = 64
ROPE_HALF = 8
ROPE_THETA = 500000.0
PATTERNS = ((128, 1), (512, 4), (2048, 16))
HALF_STEPS = 64
SSD_GROUPS = 2
SSD_STATE = 128
SSD_XBC = 1536
SSD_CONV = 5
CHUNK = 128
SSD_CPS = 4
N_EXPERTS = 16
CAPACITY_FACTOR = 2
D_FF = 2048
SHORT_CONV = 3
EPS = 1e-6

LANE = 128
VMEM_LIMIT = 56 * 1024 * 1024

ROW_TILE = 512
OUT_ROW_TILE = 256
ATTN_SUPER = 2048
ATTN_HALO = 256
ATTN_TQ = 128
ATTN_TK = ATTN_TQ + 2 * HALF_STEPS
ATTN_UNROLL = 8
TOK_TILE = 512
SUB = TOK_TILE // LANE
SLOT_ALIGN = 16
WINDOW = 96
FFN_ROWS = 256
CAST_ROWS = 64


def _cparams(n_axes):
    return pltpu.CompilerParams(dimension_semantics=("arbitrary",) * n_axes, vmem_limit_bytes=VMEM_LIMIT)


def _const_spec(shape):
    nd = len(shape)
    return pl.BlockSpec(shape, lambda *_: (0,) * nd, pipeline_mode=pl.Buffered(1))


def _dot(a, b):
    return jnp.dot(a, b, preferred_element_type=F32)


def _dot_nt(a, b):
    return lax.dot_general(a, b, (((1,), (1,)), ((), ())), preferred_element_type=F32)


def _dot_tn(a, b):
    return lax.dot_general(a, b, (((0,), (0,)), ((), ())), preferred_element_type=F32)


def _split2(x):
    hi = x.astype(BF16)
    lo = (x - hi.astype(F32)).astype(BF16)
    return hi, lo


def _split3(x):
    hi = x.astype(BF16)
    r = x - hi.astype(F32)
    mid = r.astype(BF16)
    lo = (r - mid.astype(F32)).astype(BF16)
    return hi, mid, lo


def _dot3(x, m_bf16):
    hi, mid, lo = _split3(x)
    return _dot(hi, m_bf16) + _dot(mid, m_bf16) + _dot(lo, m_bf16)


def _rms(x, g):
    return x * lax.rsqrt(jnp.mean(x * x, axis=-1, keepdims=True) + EPS) * g


def _silu(x):
    return x * jax.nn.sigmoid(x)


def _in_even_kernel(x_ref, g_ref, wq_ref, wk_ref, wv_ref, wz_ref, wx_ref, wd_ref, qg_ref, kg_ref, bd_ref,
                    rc_ref, r1_ref, r2_ref, q_ref, k_ref, v_ref, z_ref, xbc_ref, dt_ref):
    hb = _rms(x_ref[...], g_ref[...]).astype(BF16)
    bd = bd_ref[...]
    rc, r1, r2 = rc_ref[...], r1_ref[...], r2_ref[...]

    def head_norm_rope(w_ref, gain_ref, out_ref):
        t = _dot(hb, w_ref[...])
        for c in range(D_MODEL // 256):
            tc = t[:, c * 256:(c + 1) * 256]
            sq_hi, sq_lo = _split2(tc * tc)
            ms = _dot(sq_hi, bd) + _dot(sq_lo, bd)
            tn = tc * lax.rsqrt(ms + EPS) * gain_ref[:, c * 256:(c + 1) * 256]
            for hh in range(2):
                u = tn[:, hh * LANE:(hh + 1) * LANE]
                r = u * rc + pltpu.roll(u, LANE - ROPE_HALF, 1) * r1 + pltpu.roll(u, ROPE_HALF, 1) * r2
                out_ref[:, c * 256 + hh * LANE:c * 256 + (hh + 1) * LANE] = r

    head_norm_rope(wq_ref, qg_ref, q_ref)
    head_norm_rope(wk_ref, kg_ref, k_ref)
    v_ref[...] = _dot(hb, wv_ref[...])
    z_ref[...] = _dot(hb, wz_ref[...])
    xbc_ref[...] = _dot(hb, wx_ref[...])
    dt_ref[...] = _dot(hb, wd_ref[...])


def _in_even(x, g, wq, wk, wv, wz, wx, wd, qg, kg, bd, rc, r1, r2, seq):
    n = x.shape[0]
    tm = ROW_TILE
    row = lambda w: pl.BlockSpec((tm, w), lambda i: (i, 0))
    tab = pl.BlockSpec((tm, LANE), lambda i: (i % (seq // tm), 0))
    outs = [jax.ShapeDtypeStruct((n, D_MODEL), F32)] * 4 + [jax.ShapeDtypeStruct((n, SSD_XBC), F32),
                                                           jax.ShapeDtypeStruct((n, LANE), F32)]
    return pl.pallas_call(
        _in_even_kernel, grid=(n // tm,),
        in_specs=[row(D_MODEL), _const_spec((1, D_MODEL)), _const_spec(wq.shape), _const_spec(wk.shape),
                  _const_spec(wv.shape), _const_spec(wz.shape), _const_spec(wx.shape), _const_spec(wd.shape),
                  _const_spec((1, D_MODEL)), _const_spec((1, D_MODEL)), _const_spec((256, 256)), tab, tab, tab],
        out_specs=[row(D_MODEL)] * 4 + [row(SSD_XBC), row(LANE)], out_shape=outs,
        compiler_params=_cparams(1), name="in_even",
    )(x, g, wq, wk, wv, wz, wx, wd, qg, kg, bd, rc, r1, r2)


def _attn_kernel(q_ref, k0, k1, k2, v0, v1, v2, qw_hbm, kw_hbm, vw_hbm, o_ref, qf, kf, vf, q16, k16, v16,
                 acc, mst, lst, a16, m16, l16, an3, mn3, ln3, sem, rsem, *, seq):
    dmax = PATTERNS[-1][1]
    bi, hp, j = pl.program_id(0), pl.program_id(1), pl.program_id(2)
    n_hp, n_j = pl.num_programs(1), pl.num_programs(2)
    p0 = j * ATTN_SUPER
    halo = HALF_STEPS
    step = (bi * n_hp + hp) * n_j + j
    cur = step % 2

    def residue_copies(at, buf, r, where):
        b_, hp_, j_ = at
        lanes = pl.ds(pl.multiple_of(hp_ * LANE, LANE), LANE)
        q0 = pl.multiple_of(j_ * ATTN_TQ, ATTN_TQ)
        n_kv = ATTN_TK - halo if where else ATTN_TK
        src0 = 0 if where < 0 else q0 - halo
        dst0 = halo if where < 0 else 0
        cps = [pltpu.make_async_copy(qw_hbm.at[b_, pl.ds(q0, ATTN_TQ), r, lanes], q16.at[buf, r], rsem.at[buf, 0, r])]
        for n, (src, dst) in enumerate(((kw_hbm, k16), (vw_hbm, v16))):
            cps.append(pltpu.make_async_copy(src.at[b_, pl.ds(src0, n_kv), r, lanes],
                                             dst.at[buf, r, pl.ds(dst0, n_kv), :], rsem.at[buf, 1 + n, r]))
        return cps

    def all_residues(at, buf, action):
        def run(where):
            for r in range(dmax):
                for cp in residue_copies(at, buf, r, where):
                    getattr(cp, action)()
        j_ = at[2]
        pl.when(j_ == 0)(functools.partial(run, -1))
        pl.when((j_ > 0) & (j_ < n_j - 1))(functools.partial(run, 0))
        pl.when(j_ == n_j - 1)(functools.partial(run, 1))

    def state_copies(r):
        return [pltpu.make_async_copy(src.at[r], dst.at[:, r, :], sem.at[n, r])
                for n, (src, dst) in enumerate(((a16, an3), (m16, mn3), (l16, ln3)))]

    @pl.when(step == 0)
    def _():
        k16[...] = jnp.zeros_like(k16)
        v16[...] = jnp.zeros_like(v16)
        all_residues((bi, hp, j), cur, "start")

    @pl.when(step + 1 < pl.num_programs(0) * n_hp * n_j)
    def _():
        nxt = step + 1
        all_residues((nxt // (n_hp * n_j), (nxt // n_j) % n_hp, nxt % n_j), 1 - cur, "start")

    qf[...] = q_ref[0] * (HEAD_DIM ** -0.5)
    row0 = 0
    for kr, vr in ((k0, v0), (k1, v1), (k2, v2)):
        n_rows = kr.shape[1]
        kf[row0:row0 + n_rows, :] = kr[0]
        vf[row0:row0 + n_rows, :] = vr[0]
        row0 += n_rows

    head_a = lax.broadcasted_iota(I32, (ATTN_TQ, LANE), 1) < HEAD_DIM
    off = lax.broadcasted_iota(I32, (ATTN_TQ, ATTN_TK), 1) - lax.broadcasted_iota(I32, (ATTN_TQ, ATTN_TK), 0)
    band = (off >= 0) & (off <= 2 * HALF_STEPS)
    band2 = jnp.concatenate([band, band], axis=0)
    colpos = lax.broadcasted_iota(I32, (1, ATTN_TK), 1)

    def local_softmax(q, kt, vt, first_pos, d):
        pos = first_pos + d * colpos
        valid = band2 & (pos >= 0) & (pos < seq)
        q2 = jnp.concatenate([jnp.where(head_a, q, 0.0), jnp.where(head_a, 0.0, q)], axis=0).astype(BF16)
        s = jnp.where(valid, _dot_nt(q2, kt.astype(BF16)), -jnp.inf)
        m = jnp.max(s, axis=1, keepdims=True)
        p = jnp.exp(s - m)
        l = jnp.sum(p, axis=1, keepdims=True)
        n = _dot(p.astype(BF16), vt.astype(BF16))
        return (jnp.where(head_a, m[:ATTN_TQ], m[ATTN_TQ:]), jnp.where(head_a, l[:ATTN_TQ], l[ATTN_TQ:]),
                jnp.where(head_a, n[:ATTN_TQ], n[ATTN_TQ:]))

    def visit(d, i, first):
        log_d = d.bit_length() - 1
        r = i & (d - 1)
        t = i >> log_d
        qs = r + d * ATTN_TQ * t
        ks = ATTN_HALO + r + d * (ATTN_TQ * t - HALF_STEPS)
        m_loc, l_loc, n_loc = local_softmax(qf[pl.ds(qs, ATTN_TQ, stride=d), :], kf[pl.ds(ks, ATTN_TK, stride=d), :],
                                            vf[pl.ds(ks, ATTN_TK, stride=d), :], p0 - ATTN_HALO + ks, d)
        rows = pl.ds(qs, ATTN_TQ, stride=d)
        if first:
            mst[rows, :], lst[rows, :], acc[rows, :] = m_loc, l_loc, n_loc
        else:
            m_old = mst[rows, :]
            m_new = jnp.maximum(m_old, m_loc)
            w_old = jnp.exp(m_old - m_new)
            w_loc = jnp.exp(m_loc - m_new)
            mst[rows, :] = m_new
            lst[rows, :] = lst[rows, :] * w_old + l_loc * w_loc
            acc[rows, :] = acc[rows, :] * w_old + n_loc * w_loc

    def visit_dmax(r):
        q = q16[cur, r] * (HEAD_DIM ** -0.5)
        m16[r], l16[r], a16[r] = local_softmax(q, k16[cur, r], v16[cur, r], p0 - dmax * HALF_STEPS + r, dmax)

    def loop(n, fn):
        def body(i, c):
            for u in range(ATTN_UNROLL):
                fn(i * ATTN_UNROLL + u)
            return c
        lax.fori_loop(0, n // ATTN_UNROLL, body, 0)

    n_visits = ATTN_SUPER // ATTN_TQ
    loop(n_visits, lambda i: visit(PATTERNS[0][1], i, True))
    all_residues((bi, hp, j), cur, "wait")
    loop(dmax, visit_dmax)
    for r in range(dmax):
        for cp in state_copies(r):
            cp.start()
    for _, d in PATTERNS[1:-1]:
        loop(n_visits, lambda i, d=d: visit(d, i, False))
    for r in range(dmax):
        for cp in state_copies(r):
            cp.wait()

    def finish(i, c):
        rows = pl.ds(pl.multiple_of(i * ATTN_TQ, ATTN_TQ), ATTN_TQ)
        slabs = pl.ds(pl.multiple_of(i * (ATTN_TQ // dmax), ATTN_TQ // dmax), ATTN_TQ // dmax)
        a_a, m_a, l_a = (t[slabs].reshape(ATTN_TQ, LANE) for t in (an3, mn3, ln3))
        m_b = mst[rows, :]
        m = jnp.maximum(m_a, m_b)
        w_a, w_b = jnp.exp(m_a - m), jnp.exp(m_b - m)
        o = (a_a * w_a + acc[rows, :] * w_b) / (l_a * w_a + lst[rows, :] * w_b)
        o_ref[0, rows, :] = o.astype(BF16)
        return c

    lax.fori_loop(0, n_visits, finish, 0)


def _attention(q, k, v):
    b, seq, _ = q.shape
    nblk = seq // ATTN_HALO
    ratio = ATTN_SUPER // ATTN_HALO
    dmax = PATTERNS[-1][1]
    assert ATTN_SUPER // dmax == ATTN_TQ
    assert ATTN_HALO == PATTERNS[-2][1] * HALF_STEPS and seq // ATTN_SUPER >= 2

    def halo(after):
        return pl.BlockSpec((1, ATTN_HALO, LANE),
                            lambda bi, hp, j: (bi, jnp.clip(ratio * (j + after) - 1 + after, 0, nblk - 1), hp))

    main = pl.BlockSpec((1, ATTN_SUPER, LANE), lambda bi, hp, j: (bi, j, hp))
    kv_specs = [halo(0), main, halo(1)]
    hbm = pl.BlockSpec(memory_space=pl.ANY)
    by_residue = lambda t: t.reshape(b, seq // dmax, dmax, D_MODEL)
    return pl.pallas_call(
        functools.partial(_attn_kernel, seq=seq),
        grid=(b, D_MODEL // LANE, seq // ATTN_SUPER),
        in_specs=[main] + kv_specs * 2 + [hbm] * 3,
        out_specs=main,
        out_shape=jax.ShapeDtypeStruct((b, seq, D_MODEL), BF16),
        scratch_shapes=[pltpu.VMEM((ATTN_SUPER, LANE), F32), pltpu.VMEM((ATTN_SUPER + 2 * ATTN_HALO, LANE), F32),
                        pltpu.VMEM((ATTN_SUPER + 2 * ATTN_HALO, LANE), F32), pltpu.VMEM((2, dmax, ATTN_TQ, LANE), F32),
                        pltpu.VMEM((2, dmax, ATTN_TK, LANE), F32), pltpu.VMEM((2, dmax, ATTN_TK, LANE), F32)]
                       + [pltpu.VMEM((ATTN_SUPER, LANE), F32)] * 3
                       + [pltpu.VMEM((dmax, ATTN_TQ, LANE), F32)] * 3
                       + [pltpu.VMEM((ATTN_TQ, dmax, LANE), F32)] * 3
                       + [pltpu.SemaphoreType.DMA((3, dmax)), pltpu.SemaphoreType.DMA((2, 3, dmax))],
        compiler_params=_cparams(3), name="dilated_attn",
    )(q, k, k, k, v, v, v, by_residue(q), by_residue(k), by_residue(v))


def _shifted(cur, prev8, next8, s, first, last):
    n = cur.shape[0]
    if s == 0:
        return cur
    rows = lax.broadcasted_iota(I32, cur.shape, 0)
    out = pltpu.roll(cur, (-s) % n, 0)
    if s < 0:
        for j in range(-s):
            src = jnp.where(first, 0.0, prev8[8 + s + j:8 + s + j + 1, :])
            out = jnp.where(rows == j, src, out)
    else:
        for j in range(s):
            src = jnp.where(last, 0.0, next8[j:j + 1, :])
            out = jnp.where(rows == n - s + j, src, out)
    return out


def _ssd_conv_kernel(c_ref, p_ref, n_ref, w_ref, b_ref, o_ref):
    first = pl.program_id(1) == 0
    last = pl.program_id(1) == pl.num_programs(1) - 1
    for c in range(SSD_XBC // 256):
        sl = slice(c * 256, (c + 1) * 256)
        cur, prev8, next8 = c_ref[0, :, sl], p_ref[0, :, sl], n_ref[0, :, sl]
        y = b_ref[:, sl]
        for j in range(SSD_CONV):
            y = y + _shifted(cur, prev8, next8, j - SSD_CONV // 2, first, last) * w_ref[j:j + 1, sl]
        o_ref[0, :, sl] = _silu(y)


def _halo_specs(tm, width, seq):
    cur = pl.BlockSpec((1, tm, width), lambda b, i: (b, i, 0))
    prev = pl.BlockSpec((1, 8, width), lambda b, i: (b, jnp.maximum(i * (tm // 8) - 1, 0), 0))
    nxt = pl.BlockSpec((1, 8, width), lambda b, i: (b, jnp.minimum((i + 1) * (tm // 8), seq // 8 - 1), 0))
    return cur, prev, nxt


def _ssd_conv(xbc, w, bias):
    b, seq, width = xbc.shape
    tm = 512
    cur, prev, nxt = _halo_specs(tm, width, seq)
    return pl.pallas_call(
        _ssd_conv_kernel, grid=(b, seq // tm),
        in_specs=[cur, prev, nxt, pl.BlockSpec((8, width), lambda b, i: (0, 0)),
                  pl.BlockSpec((1, width), lambda b, i: (0, 0))],
        out_specs=cur, out_shape=jax.ShapeDtypeStruct(xbc.shape, F32),
        compiler_params=_cparams(2), name="ssd_conv",
    )(xbc, xbc, xbc, w, bias)


def _softplus(x):
    return jnp.maximum(x, 0.0) + jnp.log1p(jnp.exp(-jnp.abs(x)))


def _tri(kind):
    s = lax.broadcasted_iota(I32, (CHUNK, CHUNK), 0)
    l = lax.broadcasted_iota(I32, (CHUNK, CHUNK), 1)
    return {"le": s <= l, "ge": s >= l, "lt": s < l}[kind]


def _expand(cols, e2_ref):
    hi, lo = _split2(cols)
    return _dot(jnp.concatenate([hi, lo], axis=1), e2_ref[...])


def _ssd_bwd_kernel(xs_ref, b_ref, dt_ref, pc_ref, e2_ref, sb_ref, st):
    @pl.when(pl.program_id(1) == 0)
    def _():
        st[...] = jnp.zeros_like(st)

    for c in reversed(range(SSD_CPS)):
        _ssd_bwd_chunk(slice(c * CHUNK, (c + 1) * CHUNK), c, xs_ref, b_ref, dt_ref, pc_ref, e2_ref, sb_ref, st)


def _ssd_bwd_chunk(rs, c, xs_ref, b_ref, dt_ref, pc_ref, e2_ref, sb_ref, st):
    sb_ref[0, c] = st[...].astype(BF16)
    dt_t = dt_ref[0, rs, :].T
    dtb = _softplus(dt_t[N_HEADS:2 * N_HEADS, :] + pc_ref[:, 1:2])
    a = dtb * pc_ref[:, 3:4]
    ex = _dot3(a, _tri("lt").astype(BF16))
    tot = ex[:, CHUNK - 1:CHUNK] + a[:, CHUNK - 1:CHUNK]
    rowform = jnp.concatenate([dtb * jnp.exp(ex), jnp.broadcast_to(jnp.exp(tot), (N_HEADS, CHUNK)),
                               jnp.zeros((CHUNK - 2 * N_HEADS, CHUNK), F32)], axis=0)
    ex2 = _expand(rowform.T, e2_ref)
    xw = (xs_ref[0, rs, :] * ex2[:, :D_MODEL]).astype(BF16)
    half = D_MODEL // SSD_GROUPS
    upd = [_dot(b_ref[0, rs, g * SSD_STATE:(g + 1) * SSD_STATE].T.astype(BF16), xw[:, g * half:(g + 1) * half])
           for g in range(SSD_GROUPS)]
    st[...] = st[...] * ex2[0:1, D_MODEL:] + jnp.concatenate(upd, axis=1)


def _ssd_fwd_kernel(xs_ref, b_ref, c_ref, dt_ref, z_ref, sb_ref, pc_ref, e3_ref, dexp_ref, on_ref, o_ref, st):
    @pl.when(pl.program_id(1) == 0)
    def _():
        st[...] = jnp.zeros_like(st)

    for c in range(SSD_CPS):
        _ssd_fwd_chunk(slice(c * CHUNK, (c + 1) * CHUNK), c, xs_ref, b_ref, c_ref, dt_ref, z_ref, sb_ref, pc_ref,
                       e3_ref, dexp_ref, on_ref, o_ref, st)


def _ssd_fwd_chunk(rs, c, xs_ref, b_ref, c_ref, dt_ref, z_ref, sb_ref, pc_ref, e3_ref, dexp_ref, on_ref, o_ref, st):
    xs = xs_ref[0, rs, :]
    dt_t = dt_ref[0, rs, :].T
    dtf = _softplus(dt_t[0:N_HEADS, :] + pc_ref[:, 0:1])
    dtb = _softplus(dt_t[N_HEADS:2 * N_HEADS, :] + pc_ref[:, 1:2])
    af = dtf * pc_ref[:, 2:3]
    ab = dtb * pc_ref[:, 3:4]
    csf = _dot3(af, _tri("le").astype(BF16))
    rcs = _dot3(ab, _tri("ge").astype(BF16))
    totf = csf[:, CHUNK - 1:CHUNK]
    rowform = jnp.concatenate([dtf * jnp.exp(totf - csf), jnp.exp(csf), jnp.exp(rcs), csf, rcs,
                               jnp.zeros((CHUNK - 5 * N_HEADS, CHUNK), F32)], axis=0)
    cols = rowform.T
    ex3 = _expand(cols, e3_ref)
    w_state, e_f, e_b = ex3[:, :D_MODEL], ex3[:, D_MODEL:2 * D_MODEL], ex3[:, 2 * D_MODEL:]

    xb = xs.astype(BF16)
    lower, upper = _tri("ge"), _tri("le")
    head_a = lax.broadcasted_iota(I32, (CHUNK, LANE), 1) < HEAD_DIM
    half = D_MODEL // SSD_GROUPS
    hpg = N_HEADS // SSD_GROUPS
    st_all = st[...]
    sb_all = sb_ref[0, c]
    ys = []
    b_t = []
    for g in range(SSD_GROUPS):
        bg = b_ref[0, rs, g * SSD_STATE:(g + 1) * SSD_STATE]
        cg = c_ref[0, rs, g * SSD_STATE:(g + 1) * SSD_STATE].astype(BF16)
        b_t.append(bg.T.astype(BF16))
        gm = _dot_nt(cg, bg.astype(BF16))
        states = jnp.concatenate([st_all[:, g * half:(g + 1) * half].astype(BF16),
                                  sb_all[:, g * half:(g + 1) * half]], axis=1)
        off = _dot(cg, states)
        y_off = (off[:, :half] * e_f[:, g * half:(g + 1) * half]
                 + off[:, half:] * e_b[:, g * half:(g + 1) * half])
        for pair in range(hpg // 2):
            ms = []
            for h in (g * hpg + 2 * pair, g * hpg + 2 * pair + 1):
                dec_f = jnp.where(lower, jnp.exp(cols[:, 3 * N_HEADS + h:3 * N_HEADS + h + 1] - csf[h:h + 1, :]), 0.0)
                dec_b = jnp.where(upper, jnp.exp(cols[:, 4 * N_HEADS + h:4 * N_HEADS + h + 1] - rcs[h:h + 1, :]), 0.0)
                ms.append((gm * (dec_f * dtf[h:h + 1, :] + dec_b * dtb[h:h + 1, :])).astype(BF16))
            lo = g * half + pair * LANE
            xp = xb[:, lo:lo + LANE]
            ys.append(jnp.where(head_a, _dot(ms[0], xp), _dot(ms[1], xp))
                      + y_off[:, pair * LANE:(pair + 1) * LANE])
    y = jnp.concatenate(ys, axis=1) + xs * dexp_ref[...]
    yz = y * _silu(z_ref[0, rs, :])
    o_ref[0, rs, :] = _rms(yz, on_ref[...]).astype(BF16)

    xw = (xs * w_state).astype(BF16)
    upd = [_dot(b_t[g], xw[:, g * half:(g + 1) * half]) for g in range(SSD_GROUPS)]
    st[...] = st_all * e_f[CHUNK - 1:CHUNK, :] + jnp.concatenate(upd, axis=1)


def _ssd(xbc_act, dt, z, pc, dexp, out_norm):
    b, seq, _ = xbc_act.shape
    nc = seq // CHUNK
    sel = np.zeros((2 * CHUNK, 3 * D_MODEL), np.float32)
    for part in range(3):
        for h in range(N_HEADS):
            for rep in range(2):
                sel[rep * CHUNK + part * N_HEADS + h, part * D_MODEL + h * HEAD_DIM:part * D_MODEL + (h + 1) * HEAD_DIM] = 1.0
    e3 = jnp.asarray(sel, BF16)
    e2 = jnp.asarray(sel[:, :2 * D_MODEL], BF16)

    rows = SSD_CPS * CHUNK
    nblk = seq // rows
    bcol = D_MODEL // (2 * SSD_STATE)
    rev = lambda bi, c: (bi, nblk - 1 - c, 0)
    sb = pl.pallas_call(
        _ssd_bwd_kernel, grid=(b, nblk),
        in_specs=[pl.BlockSpec((1, rows, D_MODEL), rev),
                  pl.BlockSpec((1, rows, 2 * SSD_STATE), lambda bi, c: (bi, nblk - 1 - c, bcol)),
                  pl.BlockSpec((1, rows, LANE), rev), _const_spec(pc.shape), _const_spec(e2.shape)],
        out_specs=pl.BlockSpec((1, SSD_CPS, SSD_STATE, D_MODEL), lambda bi, c: (bi, nblk - 1 - c, 0, 0)),
        out_shape=jax.ShapeDtypeStruct((b, nc, SSD_STATE, D_MODEL), BF16),
        scratch_shapes=[pltpu.VMEM((SSD_STATE, D_MODEL), F32)],
        compiler_params=_cparams(2), name="ssd_bwd_state",
    )(xbc_act, xbc_act, dt, pc, e2)

    fwd = lambda bi, c: (bi, c, 0)
    return pl.pallas_call(
        _ssd_fwd_kernel, grid=(b, nblk),
        in_specs=[pl.BlockSpec((1, rows, D_MODEL), fwd),
                  pl.BlockSpec((1, rows, 2 * SSD_STATE), lambda bi, c: (bi, c, bcol)),
                  pl.BlockSpec((1, rows, 2 * SSD_STATE), lambda bi, c: (bi, c, bcol + 1)),
                  pl.BlockSpec((1, rows, LANE), fwd), pl.BlockSpec((1, rows, D_MODEL), fwd),
                  pl.BlockSpec((1, SSD_CPS, SSD_STATE, D_MODEL), lambda bi, c: (bi, c, 0, 0)),
                  _const_spec(pc.shape), _const_spec(e3.shape), _const_spec((1, D_MODEL)), _const_spec((1, D_MODEL))],
        out_specs=pl.BlockSpec((1, rows, D_MODEL), fwd),
        out_shape=jax.ShapeDtypeStruct((b, seq, D_MODEL), BF16),
        scratch_shapes=[pltpu.VMEM((SSD_STATE, D_MODEL), F32)],
        compiler_params=_cparams(2), name="ssd_fwd",
    )(xbc_act, xbc_act, xbc_act, dt, z, sb, pc, e3, dexp, out_norm)


def _norm_and_route(x1, fg_ref, wr_hi_ref, wr_lo_ref, hn_ref, aff_ref):
    hn = _rms(x1, fg_ref[...])
    hi, lo = _split2(hn)
    hn_ref[...] = hi
    logits = _dot(hi, wr_hi_ref[...]) + _dot(lo, wr_hi_ref[...]) + _dot(hi, wr_lo_ref[...])
    lt = logits.T[0:N_EXPERTS, :]
    e = jnp.exp(lt - jnp.max(lt, axis=0, keepdims=True))
    aff_ref[0] = e / jnp.sum(e, axis=0, keepdims=True)


def _out_even_kernel(x_ref, a_ref, y_ref, wa_ref, wy_ref, fg_ref, wr_hi_ref, wr_lo_ref, x1_ref, hn_ref, aff_ref):
    x1 = x_ref[...] + _dot(a_ref[...], wa_ref[...]) + _dot(y_ref[...], wy_ref[...])
    x1_ref[...] = x1
    _norm_and_route(x1, fg_ref, wr_hi_ref, wr_lo_ref, hn_ref, aff_ref)


def _out_odd_kernel(x_ref, gb_ref, cu_ref, cp_ref, cn_ref, cw_ref, wo_ref, fg_ref, wr_hi_ref, wr_lo_ref,
                    x1_ref, hn_ref, aff_ref, *, tiles_per_seq):
    i = pl.program_id(0) % tiles_per_seq
    first, last = i == 0, i == tiles_per_seq - 1
    cur, prev8, next8 = cu_ref[...], cp_ref[...], cn_ref[...]
    conv = sum(_shifted(cur, prev8, next8, j - SHORT_CONV // 2, first, last) * cw_ref[j:j + 1, :]
               for j in range(SHORT_CONV))
    x1 = x_ref[...] + _dot((gb_ref[...] * conv).astype(BF16), wo_ref[...])
    x1_ref[...] = x1
    _norm_and_route(x1, fg_ref, wr_hi_ref, wr_lo_ref, hn_ref, aff_ref)


def _route_outs(n, b, seq, tm):
    row = pl.BlockSpec((tm, D_MODEL), lambda i: (i, 0))
    aff = pl.BlockSpec((1, N_EXPERTS, tm), lambda i: (i // (seq // tm), 0, i % (seq // tm)))
    shapes = [jax.ShapeDtypeStruct((n, D_MODEL), F32), jax.ShapeDtypeStruct((n, D_MODEL), BF16),
              jax.ShapeDtypeStruct((b, N_EXPERTS, seq), F32)]
    return [row, row, aff], shapes


def _out_even(x, attn, y, wa, wy, fg, wr_hi, wr_lo, b, seq):
    n = x.shape[0]
    tm = OUT_ROW_TILE
    row = pl.BlockSpec((tm, D_MODEL), lambda i: (i, 0))
    out_specs, shapes = _route_outs(n, b, seq, tm)
    return pl.pallas_call(
        _out_even_kernel, grid=(n // tm,),
        in_specs=[row, row, row, _const_spec(wa.shape), _const_spec(wy.shape), _const_spec((1, D_MODEL)),
                  _const_spec(wr_hi.shape), _const_spec(wr_lo.shape)],
        out_specs=out_specs, out_shape=shapes, compiler_params=_cparams(1), name="out_even",
    )(x, attn, y, wa, wy, fg, wr_hi, wr_lo)


def _out_odd(x, gb, cu, cw, wo, fg, wr_hi, wr_lo, b, seq):
    n = x.shape[0]
    tm = OUT_ROW_TILE
    row = pl.BlockSpec((tm, D_MODEL), lambda i: (i, 0))
    prev = pl.BlockSpec((8, D_MODEL), lambda i: (jnp.maximum(i * (tm // 8) - 1, 0), 0))
    nxt = pl.BlockSpec((8, D_MODEL), lambda i: (jnp.minimum((i + 1) * (tm // 8), n // 8 - 1), 0))
    out_specs, shapes = _route_outs(n, b, seq, tm)
    return pl.pallas_call(
        functools.partial(_out_odd_kernel, tiles_per_seq=seq // tm), grid=(n // tm,),
        in_specs=[row, row, row, prev, nxt, pl.BlockSpec((8, D_MODEL), lambda i: (0, 0)), _const_spec(wo.shape),
                  _const_spec((1, D_MODEL)), _const_spec(wr_hi.shape), _const_spec(wr_lo.shape)],
        out_specs=out_specs, out_shape=shapes, compiler_params=_cparams(1), name="out_odd",
    )(x, gb, cu, cu, cu, cw, wo, fg, wr_hi, wr_lo)


def _in_odd_kernel(x_ref, g_ref, wb_ref, wc_ref, wu_ref, gb_ref, cu_ref):
    hb = _rms(x_ref[...], g_ref[...]).astype(BF16)
    gb_ref[...] = _dot(hb, wb_ref[...])
    cu_ref[...] = _dot(hb, wc_ref[...]) * _dot(hb, wu_ref[...])


def _in_odd(x, g, wb, wc, wu):
    n = x.shape[0]
    tm = ROW_TILE
    row = pl.BlockSpec((tm, D_MODEL), lambda i: (i, 0))
    return pl.pallas_call(
        _in_odd_kernel, grid=(n // tm,),
        in_specs=[row, _const_spec((1, D_MODEL)), _const_spec(wb.shape), _const_spec(wc.shape), _const_spec(wu.shape)],
        out_specs=[row, row], out_shape=[jax.ShapeDtypeStruct((n, D_MODEL), F32)] * 2,
        compiler_params=_cparams(1), name="in_odd",
    )(x, g, wb, wc, wu)


def _count(mask):
    return jnp.sum(jnp.sum(mask.astype(F32), axis=0, keepdims=True), axis=1, keepdims=True)


def _route_kernel(aff_ref, incl_ref, ones_ref, strict_ref, local_ref, group_ref, first_ref, slot_ref, off_ref, end_ref,
                  *, cap):
    def step(i, thrs):
        bit = jnp.int32(1) << (30 - i)
        out = []
        for e in range(N_EXPERTS):
            cand = thrs[e] | bit
            out.append(jnp.where(_count(pltpu.bitcast(aff_ref[0, e], I32) >= cand) >= cap, cand, thrs[e]))
        return tuple(out)

    thrs = lax.fori_loop(0, 31, step, tuple(jnp.zeros((1, 1), I32) for _ in range(N_EXPERTS)))
    for e in range(N_EXPERTS):
        bits = pltpu.bitcast(aff_ref[0, e], I32)
        thr = thrs[e]
        gt = bits > thr
        eq = (bits == thr).astype(BF16)
        eq_rank = _dot(eq, incl_ref[...]) + _dot(strict_ref[...], _dot(eq, ones_ref[...]).astype(BF16))
        sel = (gt | ((bits == thr) & (eq_rank <= cap - _count(gt)))).astype(BF16)
        within = _dot(sel, incl_ref[...])
        totals = _dot(sel, ones_ref[...]).astype(BF16)
        local = _dot(local_ref[...], totals)
        cnt = _dot(group_ref[...], totals)
        padded = jnp.floor((cnt + (SLOT_ALIGN - 1)) * (1.0 / SLOT_ALIGN)) * SLOT_ALIGN
        start = _dot(first_ref[...], padded.astype(BF16))
        slot_ref[0, e] = jnp.where(sel > 0, (start + local + within).astype(I32) - 1, -1)
        off_ref[0, e:e + 1, :] = start.T[0:1, :].astype(I32)
        end_ref[0, e:e + 1, :] = (start + padded).T[0:1, :].astype(I32)


def _route(aff, cap):
    b, _, seq = aff.shape
    nt = seq // LANE
    tri = np.arange(LANE)
    tt = np.arange(nt)
    grp = tt // SUB
    as_bf16 = lambda m: jnp.asarray(m, BF16)
    incl = as_bf16(tri[:, None] <= tri[None, :])
    strict = as_bf16(tt[None, :] < tt[:, None])
    local = as_bf16((tt[None, :] < tt[:, None]) & (grp[None, :] == grp[:, None]))
    group = as_bf16(grp[None, :] == grp[:, None])
    first = as_bf16((grp[None, :] < grp[:, None]) & (tt[None, :] % SUB == 0))
    ones = jnp.ones((LANE, LANE), BF16)
    tiles = pl.BlockSpec((1, N_EXPERTS, nt, LANE), lambda i: (i, 0, 0, 0))
    rows = pl.BlockSpec((1, N_EXPERTS, nt), lambda i: (i, 0, 0))
    return pl.pallas_call(
        functools.partial(_route_kernel, cap=cap), grid=(b,),
        in_specs=[tiles, _const_spec((LANE, LANE)), _const_spec((LANE, LANE))] + [_const_spec((nt, nt))] * 4,
        out_specs=[tiles, rows, rows],
        out_shape=[jax.ShapeDtypeStruct((b, N_EXPERTS, nt, LANE), I32)] + [jax.ShapeDtypeStruct((b, N_EXPERTS, nt), I32)] * 2,
        compiler_params=_cparams(1), name="route",
    )(aff.reshape(b, N_EXPERTS, nt, LANE), incl, ones, strict, local, group, first)


def _one_hot(slots, base):
    return (slots == lax.broadcasted_iota(I32, (WINDOW, slots.shape[1]), 0) + base).astype(BF16)


def _gather_kernel(off_ref, end_ref, slot_ref, aff_ref, hn_ref, xe_hbm, stage, extra, sem, xsem):
    bi, j = pl.program_id(0), pl.program_id(1)
    n_j = pl.num_programs(1)
    step = bi * n_j + j
    cur = step % 2

    def window_copy(bb, jj, e, buf):
        start = pl.multiple_of(off_ref[bb, e, jj * SUB], SLOT_ALIGN)
        return pltpu.make_async_copy(stage.at[buf, e], xe_hbm.at[bb, e, pl.ds(start, WINDOW), :], sem.at[buf, e])

    tokens = hn_ref[0]
    parts = [p.astype(F32) for p in _split3(aff_ref[0, 0])]
    gates = jnp.concatenate(parts + [jnp.zeros((LANE - 3 * N_EXPERTS, TOK_TILE), F32)], axis=0).astype(BF16)

    def rows_of(p):
        return jnp.concatenate([_dot(p, tokens), _dot_nt(p, gates)], axis=1).astype(BF16)

    p_all = jnp.concatenate([_one_hot(slot_ref[0, 0, e:e + 1, :], off_ref[bi, e, j * SUB])
                             for e in range(N_EXPERTS)], axis=0)
    stage[cur] = rows_of(p_all).reshape(N_EXPERTS, WINDOW, D_MODEL + LANE)

    @pl.when(j > 0)
    def _():
        for e in range(N_EXPERTS):
            window_copy(bi, j - 1, e, 1 - cur).wait()

    for e in range(N_EXPERTS):
        window_copy(bi, j, e, cur).start()

    def overflow(e, c):
        first = off_ref[bi, e, j * SUB]
        n_win = (end_ref[bi, e, j * SUB] - first + WINDOW - 1) // WINDOW
        slots = slot_ref[0, 0, pl.ds(e, 1), :]

        def one(w, c2):
            base = pl.multiple_of(first + w * WINDOW, SLOT_ALIGN)
            extra[...] = rows_of(_one_hot(slots, base))
            cp = pltpu.make_async_copy(extra, xe_hbm.at[bi, e, pl.ds(base, WINDOW), :], xsem.at[0])
            cp.start()
            cp.wait()
            return c2

        return lax.fori_loop(1, n_win, one, c)

    lax.fori_loop(0, N_EXPERTS, overflow, 0)

    @pl.when(j == n_j - 1)
    def _():
        for e in range(N_EXPERTS):
            window_copy(bi, j, e, cur).wait()
        extra[...] = jnp.zeros_like(extra)
        cap_pad = xe_hbm.shape[2]

        def fill(e, c):
            used = end_ref[bi, e, end_ref.shape[2] - 1]
            n_big = (cap_pad - used) // WINDOW
            small0 = used + n_big * WINDOW
            n_small = (cap_pad - small0) // SLOT_ALIGN
            big = lambda i: pltpu.make_async_copy(
                extra, xe_hbm.at[bi, e, pl.ds(pl.multiple_of(used + i * WINDOW, SLOT_ALIGN), WINDOW), :], xsem.at[0])
            small = lambda i: pltpu.make_async_copy(
                extra.at[0:SLOT_ALIGN],
                xe_hbm.at[bi, e, pl.ds(pl.multiple_of(small0 + i * SLOT_ALIGN, SLOT_ALIGN), SLOT_ALIGN), :], xsem.at[0])
            for n, mk in ((n_big, big), (n_small, small)):
                lax.fori_loop(0, n, lambda i, c2, mk=mk: (mk(i).start(), c2)[1], 0)
            for n, mk in ((n_big, big), (n_small, small)):
                lax.fori_loop(0, n, lambda i, c2, mk=mk: (mk(i).wait(), c2)[1], 0)
            return c

        lax.fori_loop(0, N_EXPERTS, fill, 0)


def _gather(off, end, slot_t, aff_t, hn):
    b, seq, _ = hn.shape
    cap_pad = _cap_pad(seq)
    width = D_MODEL + LANE
    per_tile = pl.BlockSpec((1, 1, N_EXPERTS, TOK_TILE), lambda bi, j, *_: (bi, j, 0, 0))
    return pl.pallas_call(
        _gather_kernel,
        grid_spec=pltpu.PrefetchScalarGridSpec(
            num_scalar_prefetch=2, grid=(b, seq // TOK_TILE),
            in_specs=[per_tile, per_tile, pl.BlockSpec((1, TOK_TILE, D_MODEL), lambda bi, j, *_: (bi, j, 0))],
            out_specs=pl.BlockSpec(memory_space=pl.ANY),
            scratch_shapes=[pltpu.VMEM((2, N_EXPERTS, WINDOW, width), BF16), pltpu.VMEM((WINDOW, width), BF16),
                            pltpu.SemaphoreType.DMA((2, N_EXPERTS)), pltpu.SemaphoreType.DMA((1,))]),
        out_shape=jax.ShapeDtypeStruct((b, N_EXPERTS, cap_pad, width), BF16),
        compiler_params=_cparams(2), name="moe_gather",
    )(off, end, slot_t, aff_t, hn)


def _ffn_kernel(end_ref, xe_ref, wg_hbm, wu_hbm, wd_hbm, y_ref, stage_g, stage_u, stage_d, wg, wu, wd, sem, *, layer):
    e, bi, r = pl.program_id(0), pl.program_id(1), pl.program_id(2)
    used = end_ref[bi, e, end_ref.shape[2] - 1]
    pairs = ((wg_hbm, stage_g, wg), (wu_hbm, stage_u, wu), (wd_hbm, stage_d, wd))

    def weight_copies(ee):
        return [pltpu.make_async_copy(src.at[layer, ee], stg, sem.at[k]) for k, (src, stg, _) in enumerate(pairs)]

    @pl.when((bi == 0) & (r == 0))
    def _():
        @pl.when(e == 0)
        def _():
            for cp in weight_copies(e):
                cp.start()

        for cp, (_, stg, dst) in zip(weight_copies(e), pairs):
            cp.wait()
            n_rows = stg.shape[0]

            def cast(i, c, stg=stg, dst=dst):
                rows = pl.ds(pl.multiple_of(i * CAST_ROWS, CAST_ROWS), CAST_ROWS)
                dst[rows, :] = stg[rows, :].astype(BF16)
                return c

            lax.fori_loop(0, n_rows // CAST_ROWS, cast, 0)

        @pl.when(e + 1 < pl.num_programs(0))
        def _():
            for cp in weight_copies(e + 1):
                cp.start()

    @pl.when(r * FFN_ROWS < used)
    def _():
        xe = xe_ref[0, 0, :, 0:D_MODEL]
        hid = (_silu(_dot(xe, wg[...])) * _dot(xe, wu[...])).astype(BF16)
        g = xe_ref[0, 0, :, D_MODEL:].astype(F32)
        lane = lax.broadcasted_iota(I32, g.shape, 1)
        mine = (lane % N_EXPERTS == e) & (lane < 3 * N_EXPERTS)
        gate = jnp.sum(jnp.where(mine, g, 0.0), axis=1, keepdims=True)
        y_ref[0, 0] = (_dot(hid, wd[...]) * gate).astype(BF16)

    @pl.when(r * FFN_ROWS >= used)
    def _():
        y_ref[...] = jnp.zeros_like(y_ref)


def _ffn(end, xe, wg, wu, wd, layer):
    b, ne, cap_pad, width = xe.shape
    rows = lambda w: pl.BlockSpec((1, 1, FFN_ROWS, w), lambda e, bi, r, *_: (bi, e, r, 0))
    hbm = pl.BlockSpec(memory_space=pl.ANY)
    return pl.pallas_call(
        functools.partial(_ffn_kernel, layer=layer),
        grid_spec=pltpu.PrefetchScalarGridSpec(
            num_scalar_prefetch=1, grid=(ne, b, cap_pad // FFN_ROWS),
            in_specs=[rows(width), hbm, hbm, hbm],
            out_specs=rows(D_MODEL),
            scratch_shapes=[pltpu.VMEM((D_MODEL, D_FF), F32), pltpu.VMEM((D_MODEL, D_FF), F32),
                            pltpu.VMEM((D_FF, D_MODEL), F32), pltpu.VMEM((D_MODEL, D_FF), BF16),
                            pltpu.VMEM((D_MODEL, D_FF), BF16), pltpu.VMEM((D_FF, D_MODEL), BF16),
                            pltpu.SemaphoreType.DMA((3,))]),
        out_shape=jax.ShapeDtypeStruct((b, ne, cap_pad, D_MODEL), BF16),
        compiler_params=_cparams(3), name="moe_ffn",
    )(end, xe, wg, wu, wd)


def _combine_kernel(off_ref, end_ref, slot_ref, y_hbm, x1_ref, o_ref, win, extra, sem, xsem):
    bi, j = pl.program_id(0), pl.program_id(1)
    n_j = pl.num_programs(1)
    step = bi * n_j + j
    cur = step % 2

    def window_copy(bb, jj, e, buf):
        start = pl.multiple_of(off_ref[bb, e, jj * SUB], SLOT_ALIGN)
        return pltpu.make_async_copy(y_hbm.at[bb, e, pl.ds(start, WINDOW), :], win.at[buf, e], sem.at[buf, e])

    @pl.when(step == 0)
    def _():
        for e in range(N_EXPERTS):
            window_copy(bi, j, e, cur).start()

    @pl.when(step + 1 < pl.num_programs(0) * n_j)
    def _():
        nxt = step + 1
        for e in range(N_EXPERTS):
            window_copy(nxt // n_j, nxt % n_j, e, 1 - cur).start()

    ps = []
    for e in range(N_EXPERTS):
        window_copy(bi, j, e, cur).wait()
        ps.append(_one_hot(slot_ref[0, 0, e:e + 1, :], off_ref[bi, e, j * SUB]))
    p_all = jnp.concatenate(ps, axis=0)
    y_all = win[cur].reshape(N_EXPERTS * WINDOW, D_MODEL)
    o_ref[0] = x1_ref[0] + _dot_tn(p_all, y_all)

    def overflow(e, c):
        first = off_ref[bi, e, j * SUB]
        n_win = (end_ref[bi, e, j * SUB] - first + WINDOW - 1) // WINDOW
        slots = slot_ref[0, 0, pl.ds(e, 1), :]

        def one(w, c2):
            base = pl.multiple_of(first + w * WINDOW, SLOT_ALIGN)
            cp = pltpu.make_async_copy(y_hbm.at[bi, e, pl.ds(base, WINDOW), :], extra, xsem.at[0])
            cp.start()
            cp.wait()
            o_ref[0] += _dot_tn(_one_hot(slots, base), extra[...])
            return c2

        return lax.fori_loop(1, n_win, one, c)

    lax.fori_loop(0, N_EXPERTS, overflow, 0)


def _combine(off, end, slot_t, y, x1):
    b, seq, _ = x1.shape
    tile = pl.BlockSpec((1, TOK_TILE, D_MODEL), lambda bi, j, *_: (bi, j, 0))
    return pl.pallas_call(
        _combine_kernel,
        grid_spec=pltpu.PrefetchScalarGridSpec(
            num_scalar_prefetch=2, grid=(b, seq // TOK_TILE),
            in_specs=[pl.BlockSpec((1, 1, N_EXPERTS, TOK_TILE), lambda bi, j, *_: (bi, j, 0, 0)),
                      pl.BlockSpec(memory_space=pl.ANY), tile],
            out_specs=tile,
            scratch_shapes=[pltpu.VMEM((2, N_EXPERTS, WINDOW, D_MODEL), BF16), pltpu.VMEM((WINDOW, D_MODEL), BF16),
                            pltpu.SemaphoreType.DMA((2, N_EXPERTS)), pltpu.SemaphoreType.DMA((1,))]),
        out_shape=jax.ShapeDtypeStruct((b, seq, D_MODEL), F32),
        compiler_params=_cparams(2), name="moe_combine",
    )(off, end, slot_t, y, x1)


def _cap_pad(seq):
    cap = CAPACITY_FACTOR * seq // N_EXPERTS
    worst = cap + (seq // TOK_TILE) * (SLOT_ALIGN - 1) + WINDOW
    return -(-worst // FFN_ROWS) * FFN_ROWS


def _moe(x1, hn, aff, wg, wu, wd, layer):
    b, seq, _ = x1.shape
    cap = CAPACITY_FACTOR * seq // N_EXPERTS
    slot, off, end = _route(aff, cap)
    per_tile = lambda a: jnp.swapaxes(a.reshape(b, N_EXPERTS, seq // TOK_TILE, TOK_TILE), 1, 2)
    slot_t = per_tile(slot)
    xe = _gather(off, end, slot_t, per_tile(aff), hn)
    y = _ffn(end, xe, wg, wu, wd, layer)
    return _combine(off, end, slot_t, y, x1)


def _rope_tables(seq):
    inv_freq = ROPE_THETA ** (-jnp.arange(ROPE_HALF, dtype=F32) * 2.0 / (2 * ROPE_HALF))
    ang = jnp.arange(seq, dtype=F32)[:, None] * inv_freq[None, :]
    cos, sin = jnp.cos(ang), jnp.sin(ang)
    z = lambda w: jnp.zeros((seq, w), F32)
    rest = HEAD_DIM - 2 * ROPE_HALF
    rc = jnp.concatenate([cos, cos, jnp.ones((seq, rest), F32)], axis=1)
    r1 = jnp.concatenate([-sin, z(ROPE_HALF + rest)], axis=1)
    r2 = jnp.concatenate([z(ROPE_HALF), sin, z(rest)], axis=1)
    return tuple(jnp.tile(t, (1, LANE // HEAD_DIM)) for t in (rc, r1, r2))


def _router_split(w):
    wp = jnp.pad(w, ((0, 0), (0, LANE - N_EXPERTS)))
    hi = wp.astype(BF16)
    return hi, (wp - hi.astype(F32)).astype(BF16)


def kernel(x, attn_norm, w_in_even, q_norm, k_norm, ssd_conv_w, ssd_conv_b, ssd_a_log_fwd, ssd_a_log_bwd,
           ssd_dt_bias_fwd, ssd_dt_bias_bwd, ssd_d, ssd_out_norm, w_out_even, conv_norm, conv_w_in, conv_w,
           conv_w_out, ffn_norm, router_w, expert_w_gate, expert_w_up, expert_w_down):
    b, seq, _ = x.shape
    n = b * seq
    depth = ffn_norm.shape[0]
    rc, r1, r2 = _rope_tables(seq)
    blk = np.arange(256) // HEAD_DIM
    bd = jnp.asarray((blk[:, None] == blk[None, :]) / HEAD_DIM, BF16)
    row = lambda v: v.reshape(1, -1).astype(F32)

    xf = x.reshape(n, D_MODEL)
    for layer in range(depth):
        i = layer // 2
        wr_hi, wr_lo = _router_split(router_w[layer])
        fg = row(ffn_norm[layer])
        if layer % 2 == 0:
            w = w_in_even[i].astype(BF16)
            o = np.cumsum([0, D_MODEL, D_MODEL, D_MODEL, D_MODEL, SSD_XBC, N_HEADS, N_HEADS])
            wq, wk, wv, wz, wx = (w[:, o[j]:o[j + 1]] for j in range(5))
            wd = jnp.pad(w[:, o[5]:o[7]], ((0, 0), (0, LANE - 2 * N_HEADS)))
            tile_heads = lambda g: row(jnp.tile(g, N_HEADS))
            q, k, v, z, xbc, dt = _in_even(xf, row(attn_norm[i]), wq, wk, wv, wz, wx, wd,
                                           tile_heads(q_norm[i]), tile_heads(k_norm[i]), bd, rc, r1, r2, seq)
            as3 = lambda t: t.reshape(b, seq, -1)
            attn = _attention(as3(q), as3(k), as3(v))
            cw = jnp.pad(ssd_conv_w[i], ((0, 8 - SSD_CONV), (0, 0)))
            act = _ssd_conv(as3(xbc), cw, row(ssd_conv_b[i]))
            pc = jnp.pad(jnp.stack([ssd_dt_bias_fwd[i], ssd_dt_bias_bwd[i], -jnp.exp(ssd_a_log_fwd[i]),
                                    -jnp.exp(ssd_a_log_bwd[i])], axis=1).astype(F32), ((0, 0), (0, LANE - 4)))
            y = _ssd(act, as3(dt), as3(z), pc, row(jnp.repeat(ssd_d[i], HEAD_DIM)), row(ssd_out_norm[i]))
            wo = w_out_even[i].astype(BF16)
            x1, hn, aff = _out_even(xf, attn.reshape(n, D_MODEL), y.reshape(n, D_MODEL), wo[:D_MODEL], wo[D_MODEL:],
                                    fg, wr_hi, wr_lo, b, seq)
        else:
            w = conv_w_in[i].astype(BF16)
            gb, cu = _in_odd(xf, row(conv_norm[i]), w[:, :D_MODEL], w[:, D_MODEL:2 * D_MODEL], w[:, 2 * D_MODEL:])
            cw = jnp.pad(conv_w[i], ((0, 8 - SHORT_CONV), (0, 0)))
            x1, hn, aff = _out_odd(xf, gb, cu, cw, conv_w_out[i].astype(BF16), fg, wr_hi, wr_lo, b, seq)
        xf = _moe(x1.reshape(b, seq, D_MODEL), hn.reshape(b, seq, D_MODEL), aff, expert_w_gate, expert_w_up,
                  expert_w_down, layer).reshape(n, D_MODEL)
    return xf.reshape(b, seq, D_MODEL)
```

```python
import functools
import math

import jax
import jax.numpy as jnp
import numpy as np
from jax import lax
from jax.experimental import pallas as pl
from jax.experimental.pallas import tpu as pltpu

F32, BF16, I32 = jnp.float32, jnp.bfloat16, jnp.int32

D_MODEL = 1024
N_HEADS = 16
HEAD_DIM = 64
ROPE_HALF = 8
ROPE_THETA = 500000.0
PATTERNS = ((128, 1), (512, 4), (2048, 16))
HALF_STEPS = 64
SSD_GROUPS = 2
SSD_STATE = 128
SSD_XBC = 1536
SSD_CONV = 5
CHUNK = 128
SSD_CPS = 4
N_EXPERTS = 16
CAPACITY_FACTOR = 2
D_FF = 2048
SHORT_CONV = 3
EPS = 1e-6

LANE = 128
VMEM_LIMIT = 56 * 1024 * 1024

ROW_TILE = 512
OUT_ROW_TILE = 256
ATTN_SUPER = 2048
ATTN_HALO = 64
ATTN_TQ = 128
ATTN_TK = ATTN_TQ + 2 * HALF_STEPS
ATTN_UNROLL = 8
TOK_TILE = 512
SUB = TOK_TILE // LANE
SLOT_ALIGN = 16
WINDOW = 96
FFN_ROWS = 256
CAST_ROWS = 64


def _cparams(n_axes):
    return pltpu.CompilerParams(dimension_semantics=("arbitrary",) * n_axes, vmem_limit_bytes=VMEM_LIMIT)


def _const_spec(shape):
    nd = len(shape)
    return pl.BlockSpec(shape, lambda *_: (0,) * nd, pipeline_mode=pl.Buffered(1))


def _dot(a, b):
    return jnp.dot(a, b, preferred_element_type=F32)


def _dot_nt(a, b):
    return lax.dot_general(a, b, (((1,), (1,)), ((), ())), preferred_element_type=F32)


def _dot_tn(a, b):
    return lax.dot_general(a, b, (((0,), (0,)), ((), ())), preferred_element_type=F32)


def _split2(x):
    hi = x.astype(BF16)
    lo = (x - hi.astype(F32)).astype(BF16)
    return hi, lo


def _split3(x):
    hi = x.astype(BF16)
    r = x - hi.astype(F32)
    mid = r.astype(BF16)
    lo = (r - mid.astype(F32)).astype(BF16)
    return hi, mid, lo


def _dot3(x, m_bf16):
    hi, mid, lo = _split3(x)
    return _dot(hi, m_bf16) + _dot(mid, m_bf16) + _dot(lo, m_bf16)


def _rms(x, g):
    return x * lax.rsqrt(jnp.mean(x * x, axis=-1, keepdims=True) + EPS) * g


def _silu(x):
    return x * jax.nn.sigmoid(x)


def _in_even_kernel(x_ref, g_ref, wq_ref, wk_ref, wv_ref, wz_ref, wx_ref, wd_ref, qg_ref, kg_ref, bd_ref,
                    rc_ref, r1_ref, r2_ref, q_ref, k_ref, v_ref, z_ref, xbc_ref, dt_ref):
    hb = _rms(x_ref[...], g_ref[...]).astype(BF16)
    bd = bd_ref[...]
    rc, r1, r2 = rc_ref[...], r1_ref[...], r2_ref[...]

    def head_norm_rope(w_ref, gain_ref, out_ref):
        t = _dot(hb, w_ref[...])
        for c in range(D_MODEL // 256):
            tc = t[:, c * 256:(c + 1) * 256]
            sq_hi, sq_lo = _split2(tc * tc)
            ms = _dot(sq_hi, bd) + _dot(sq_lo, bd)
            tn = tc * lax.rsqrt(ms + EPS) * gain_ref[:, c * 256:(c + 1) * 256]
            for hh in range(2):
                u = tn[:, hh * LANE:(hh + 1) * LANE]
                r = u * rc + pltpu.roll(u, LANE - ROPE_HALF, 1) * r1 + pltpu.roll(u, ROPE_HALF, 1) * r2
                out_ref[:, c * 256 + hh * LANE:c * 256 + (hh + 1) * LANE] = r

    head_norm_rope(wq_ref, qg_ref, q_ref)
    head_norm_rope(wk_ref, kg_ref, k_ref)
    v_ref[...] = _dot(hb, wv_ref[...])
    z_ref[...] = _dot(hb, wz_ref[...])
    xbc_ref[...] = _dot(hb, wx_ref[...])
    dt_ref[...] = _dot(hb, wd_ref[...])


def _in_even(x, g, wq, wk, wv, wz, wx, wd, qg, kg, bd, rc, r1, r2, seq):
    n = x.shape[0]
    tm = ROW_TILE
    row = lambda w: pl.BlockSpec((tm, w), lambda i: (i, 0))
    tab = pl.BlockSpec((tm, LANE), lambda i: (i % (seq // tm), 0))
    outs = [jax.ShapeDtypeStruct((n, D_MODEL), F32)] * 4 + [jax.ShapeDtypeStruct((n, SSD_XBC), F32),
                                                           jax.ShapeDtypeStruct((n, LANE), F32)]
    return pl.pallas_call(
        _in_even_kernel, grid=(n // tm,),
        in_specs=[row(D_MODEL), _const_spec((1, D_MODEL)), _const_spec(wq.shape), _const_spec(wk.shape),
                  _const_spec(wv.shape), _const_spec(wz.shape), _const_spec(wx.shape), _const_spec(wd.shape),
                  _const_spec((1, D_MODEL)), _const_spec((1, D_MODEL)), _const_spec((256, 256)), tab, tab, tab],
        out_specs=[row(D_MODEL)] * 4 + [row(SSD_XBC), row(LANE)], out_shape=outs,
        compiler_params=_cparams(1), name="in_even",
    )(x, g, wq, wk, wv, wz, wx, wd, qg, kg, bd, rc, r1, r2)


def _attn_kernel(q_ref, k0, k1, k2, v0, v1, v2, qw_hbm, kw_hbm, vw_hbm, o_ref, qf, kf, vf, q16, k16, v16,
                 acc, mst, lst, a16, m16, l16, an3, mn3, ln3, sem, rsem, *, seq):
    dmax = PATTERNS[-1][1]
    bi, hp, j = pl.program_id(0), pl.program_id(1), pl.program_id(2)
    n_hp, n_j = pl.num_programs(1), pl.num_programs(2)
    p0 = j * ATTN_SUPER
    halo = HALF_STEPS
    step = (bi * n_hp + hp) * n_j + j
    cur = step % 2

    def residue_copies(at, buf, r, where):
        b_, hp_, j_ = at
        lanes = pl.ds(pl.multiple_of(hp_ * LANE, LANE), LANE)
        q0 = pl.multiple_of(j_ * ATTN_TQ, ATTN_TQ)
        n_kv = ATTN_TK - halo if where else ATTN_TK
        src0 = 0 if where < 0 else q0 - halo
        dst0 = halo if where < 0 else 0
        cps = [pltpu.make_async_copy(qw_hbm.at[b_, pl.ds(q0, ATTN_TQ), r, lanes], q16.at[buf, r], rsem.at[buf, 0, r])]
        for n, (src, dst) in enumerate(((kw_hbm, k16), (vw_hbm, v16))):
            cps.append(pltpu.make_async_copy(src.at[b_, pl.ds(src0, n_kv), r, lanes],
                                             dst.at[buf, r, pl.ds(dst0, n_kv), :], rsem.at[buf, 1 + n, r]))
        return cps

    def all_residues(at, buf, action):
        def run(where):
            for r in range(dmax):
                for cp in residue_copies(at, buf, r, where):
                    getattr(cp, action)()
        j_ = at[2]
        pl.when(j_ == 0)(functools.partial(run, -1))
        pl.when((j_ > 0) & (j_ < n_j - 1))(functools.partial(run, 0))
        pl.when(j_ == n_j - 1)(functools.partial(run, 1))

    def state_copies(r):
        return [pltpu.make_async_copy(src.at[r], dst.at[:, r, :], sem.at[n, r])
                for n, (src, dst) in enumerate(((a16, an3), (m16, mn3), (l16, ln3)))]

    @pl.when(step == 0)
    def _():
        k16[...] = jnp.zeros_like(k16)
        v16[...] = jnp.zeros_like(v16)
        all_residues((bi, hp, j), cur, "start")

    @pl.when(step + 1 < pl.num_programs(0) * n_hp * n_j)
    def _():
        nxt = step + 1
        all_residues((nxt // (n_hp * n_j), (nxt // n_j) % n_hp, nxt % n_j), 1 - cur, "start")

    head_a = lax.broadcasted_iota(I32, (ATTN_TQ, LANE), 1) < HEAD_DIM
    row_i = lax.broadcasted_iota(I32, (ATTN_TQ, ATTN_TK), 0)
    col_i = lax.broadcasted_iota(I32, (ATTN_TQ, ATTN_TK), 1)
    band = (col_i - row_i >= 0) & (col_i - row_i <= 2 * HALF_STEPS)
    colpos = lax.broadcasted_iota(I32, (1, ATTN_TK), 1)

    def local_softmax(q, kt, vt, mask, pos):
        ok = mask & (pos >= 0) & (pos < seq)
        valid = jnp.concatenate([ok, ok], axis=0)
        q2 = jnp.concatenate([jnp.where(head_a, q, 0.0), jnp.where(head_a, 0.0, q)], axis=0).astype(BF16)
        s = jnp.where(valid, _dot_nt(q2, kt.astype(BF16)), -jnp.inf)
        m = jnp.max(s, axis=1, keepdims=True)
        p = jnp.exp(s - m)
        l = jnp.sum(p, axis=1, keepdims=True)
        n = _dot(p.astype(BF16), vt.astype(BF16))
        return (jnp.where(head_a, m[:ATTN_TQ], m[ATTN_TQ:]), jnp.where(head_a, l[:ATTN_TQ], l[ATTN_TQ:]),
                jnp.where(head_a, n[:ATTN_TQ], n[ATTN_TQ:]))

    def visit_dense(i):
        qs = pl.multiple_of(i * ATTN_TQ, ATTN_TQ)
        rows = pl.ds(qs, ATTN_TQ)
        keys = pl.ds(qs, ATTN_TK)
        mst[rows, :], lst[rows, :], acc[rows, :] = local_softmax(
            qf[rows, :], kf[keys, :], vf[keys, :], band, p0 + qs - HALF_STEPS + colpos)

    def visit_dmax(r):
        q = q16[cur, r] * (HEAD_DIM ** -0.5)
        m16[r], l16[r], a16[r] = local_softmax(q, k16[cur, r], v16[cur, r], band,
                                               p0 - dmax * HALF_STEPS + r + dmax * colpos)

    dmid = PATTERNS[1][1]
    per = dmax // dmid
    seg_q, seg_k = ATTN_TQ // per, ATTN_TK // per
    q_step = per * (row_i % seg_q) + row_i // seg_q
    k_step = per * (col_i % seg_k - HALF_STEPS // per) + col_i // seg_k
    mask_mid = jnp.abs(k_step - q_step) <= HALF_STEPS
    k_off_mid = dmax * (colpos % seg_k - HALF_STEPS // per) + dmid * (colpos // seg_k)

    def visit_mid(i):
        r = i & (dmid - 1)
        t = i >> (dmid.bit_length() - 1)
        n0 = pl.multiple_of(t * seg_q, seg_q)
        q_rows = pl.ds(n0, seg_q)
        k_rows = pl.ds(pl.multiple_of(n0 + HALF_STEPS - HALF_STEPS // per, 8), seg_k)
        classes = [r + dmid * c for c in range(per)]
        cat = lambda ref, rows, buf=None: jnp.concatenate(
            [ref[c, rows, :] if buf is None else ref[buf, c, rows, :] for c in classes], axis=0)
        m_loc, l_loc, n_loc = local_softmax(cat(q16, q_rows, cur) * (HEAD_DIM ** -0.5), cat(k16, k_rows, cur),
                                            cat(v16, k_rows, cur), mask_mid, p0 + dmax * n0 + r + k_off_mid)
        m_old = cat(m16, q_rows)
        m_new = jnp.maximum(m_old, m_loc)
        w_old = jnp.exp(m_old - m_new)
        w_loc = jnp.exp(m_loc - m_new)
        l_new = cat(l16, q_rows) * w_old + l_loc * w_loc
        a_new = cat(a16, q_rows) * w_old + n_loc * w_loc
        for n, c in enumerate(classes):
            seg = slice(n * seg_q, (n + 1) * seg_q)
            m16[c, q_rows, :], l16[c, q_rows, :], a16[c, q_rows, :] = m_new[seg], l_new[seg], a_new[seg]

    def loop(n, fn):
        def body(i, c):
            for u in range(ATTN_UNROLL):
                fn(i * ATTN_UNROLL + u)
            return c
        lax.fori_loop(0, n // ATTN_UNROLL, body, 0)

    n_visits = ATTN_SUPER // ATTN_TQ
    all_residues((bi, hp, j), cur, "wait")
    loop(dmax, visit_dmax)
    loop(n_visits, visit_mid)
    for r in range(dmax):
        for cp in state_copies(r):
            cp.start()

    qf[...] = q_ref[0] * (HEAD_DIM ** -0.5)
    row0 = 0
    for kr, vr in ((k0, v0), (k1, v1), (k2, v2)):
        n_rows = kr.shape[1]
        kf[row0:row0 + n_rows, :] = kr[0]
        vf[row0:row0 + n_rows, :] = vr[0]
        row0 += n_rows
    loop(n_visits, visit_dense)
    for r in range(dmax):
        for cp in state_copies(r):
            cp.wait()

    def finish(i, c):
        rows = pl.ds(pl.multiple_of(i * ATTN_TQ, ATTN_TQ), ATTN_TQ)
        slabs = pl.ds(pl.multiple_of(i * (ATTN_TQ // dmax), ATTN_TQ // dmax), ATTN_TQ // dmax)
        a_a, m_a, l_a = (t[slabs].reshape(ATTN_TQ, LANE) for t in (an3, mn3, ln3))
        m_b = mst[rows, :]
        m = jnp.maximum(m_a, m_b)
        w_a, w_b = jnp.exp(m_a - m), jnp.exp(m_b - m)
        o = (a_a * w_a + acc[rows, :] * w_b) / (l_a * w_a + lst[rows, :] * w_b)
        o_ref[0, rows, :] = o.astype(BF16)
        return c

    lax.fori_loop(0, n_visits, finish, 0)


def _attention(q, k, v):
    b, seq, _ = q.shape
    nblk = seq // ATTN_HALO
    ratio = ATTN_SUPER // ATTN_HALO
    dmax = PATTERNS[-1][1]
    assert ATTN_SUPER // dmax == ATTN_TQ
    assert len(PATTERNS) == 3 and ATTN_HALO == PATTERNS[0][1] * HALF_STEPS and seq // ATTN_SUPER >= 2

    def halo(after):
        return pl.BlockSpec((1, ATTN_HALO, LANE),
                            lambda bi, hp, j: (bi, jnp.clip(ratio * (j + after) - 1 + after, 0, nblk - 1), hp))

    main = pl.BlockSpec((1, ATTN_SUPER, LANE), lambda bi, hp, j: (bi, j, hp))
    kv_specs = [halo(0), main, halo(1)]
    hbm = pl.BlockSpec(memory_space=pl.ANY)
    by_residue = lambda t: t.reshape(b, seq // dmax, dmax, D_MODEL)
    return pl.pallas_call(
        functools.partial(_attn_kernel, seq=seq),
        grid=(b, D_MODEL // LANE, seq // ATTN_SUPER),
        in_specs=[main] + kv_specs * 2 + [hbm] * 3,
        out_specs=main,
        out_shape=jax.ShapeDtypeStruct((b, seq, D_MODEL), BF16),
        scratch_shapes=[pltpu.VMEM((ATTN_SUPER, LANE), F32), pltpu.VMEM((ATTN_SUPER + 2 * ATTN_HALO, LANE), F32),
                        pltpu.VMEM((ATTN_SUPER + 2 * ATTN_HALO, LANE), F32), pltpu.VMEM((2, dmax, ATTN_TQ, LANE), F32),
                        pltpu.VMEM((2, dmax, ATTN_TK, LANE), F32), pltpu.VMEM((2, dmax, ATTN_TK, LANE), F32)]
                       + [pltpu.VMEM((ATTN_SUPER, LANE), F32)] * 3
                       + [pltpu.VMEM((dmax, ATTN_TQ, LANE), F32)] * 3
                       + [pltpu.VMEM((ATTN_TQ, dmax, LANE), F32)] * 3
                       + [pltpu.SemaphoreType.DMA((3, dmax)), pltpu.SemaphoreType.DMA((2, 3, dmax))],
        compiler_params=_cparams(3), name="dilated_attn",
    )(q, k, k, k, v, v, v, by_residue(q), by_residue(k), by_residue(v))


def _shifted(cur, prev8, next8, s, first, last):
    n = cur.shape[0]
    if s == 0:
        return cur
    rows = lax.broadcasted_iota(I32, cur.shape, 0)
    out = pltpu.roll(cur, (-s) % n, 0)
    if s < 0:
        for j in range(-s):
            src = jnp.where(first, 0.0, prev8[8 + s + j:8 + s + j + 1, :])
            out = jnp.where(rows == j, src, out)
    else:
        for j in range(s):
            src = jnp.where(last, 0.0, next8[j:j + 1, :])
            out = jnp.where(rows == n - s + j, src, out)
    return out


def _ssd_conv_kernel(c_ref, p_ref, n_ref, w_ref, b_ref, o_ref):
    first = pl.program_id(1) == 0
    last = pl.program_id(1) == pl.num_programs(1) - 1
    for c in range(SSD_XBC // 256):
        sl = slice(c * 256, (c + 1) * 256)
        cur, prev8, next8 = c_ref[0, :, sl], p_ref[0, :, sl], n_ref[0, :, sl]
        y = b_ref[:, sl]
        for j in range(SSD_CONV):
            y = y + _shifted(cur, prev8, next8, j - SSD_CONV // 2, first, last) * w_ref[j:j + 1, sl]
        o_ref[0, :, sl] = _silu(y)


def _halo_specs(tm, width, seq):
    cur = pl.BlockSpec((1, tm, width), lambda b, i: (b, i, 0))
    prev = pl.BlockSpec((1, 8, width), lambda b, i: (b, jnp.maximum(i * (tm // 8) - 1, 0), 0))
    nxt = pl.BlockSpec((1, 8, width), lambda b, i: (b, jnp.minimum((i + 1) * (tm // 8), seq // 8 - 1), 0))
    return cur, prev, nxt


def _ssd_conv(xbc, w, bias):
    b, seq, width = xbc.shape
    tm = 512
    cur, prev, nxt = _halo_specs(tm, width, seq)
    return pl.pallas_call(
        _ssd_conv_kernel, grid=(b, seq // tm),
        in_specs=[cur, prev, nxt, pl.BlockSpec((8, width), lambda b, i: (0, 0)),
                  pl.BlockSpec((1, width), lambda b, i: (0, 0))],
        out_specs=cur, out_shape=jax.ShapeDtypeStruct(xbc.shape, F32),
        compiler_params=_cparams(2), name="ssd_conv",
    )(xbc, xbc, xbc, w, bias)


def _softplus(x):
    return jnp.maximum(x, 0.0) + jnp.log1p(jnp.exp(-jnp.abs(x)))


def _tri(kind):
    s = lax.broadcasted_iota(I32, (CHUNK, CHUNK), 0)
    l = lax.broadcasted_iota(I32, (CHUNK, CHUNK), 1)
    return {"le": s <= l, "ge": s >= l, "lt": s < l}[kind]


def _expand(cols, e2_ref):
    hi, lo = _split2(cols)
    return _dot(jnp.concatenate([hi, lo], axis=1), e2_ref[...])


def _ssd_bwd_kernel(xs_ref, b_ref, dt_ref, pc_ref, e2_ref, sb_ref, st):
    @pl.when(pl.program_id(1) == 0)
    def _():
        st[...] = jnp.zeros_like(st)

    for c in reversed(range(SSD_CPS)):
        _ssd_bwd_chunk(slice(c * CHUNK, (c + 1) * CHUNK), c, xs_ref, b_ref, dt_ref, pc_ref, e2_ref, sb_ref, st)


def _ssd_bwd_chunk(rs, c, xs_ref, b_ref, dt_ref, pc_ref, e2_ref, sb_ref, st):
    sb_ref[0, c] = st[...].astype(BF16)
    dt_t = dt_ref[0, rs, :].T
    dtb = _softplus(dt_t[N_HEADS:2 * N_HEADS, :] + pc_ref[:, 1:2])
    a = dtb * pc_ref[:, 3:4]
    ex = _dot3(a, _tri("lt").astype(BF16))
    tot = ex[:, CHUNK - 1:CHUNK] + a[:, CHUNK - 1:CHUNK]
    rowform = jnp.concatenate([dtb * jnp.exp(ex), jnp.broadcast_to(jnp.exp(tot), (N_HEADS, CHUNK)),
                               jnp.zeros((CHUNK - 2 * N_HEADS, CHUNK), F32)], axis=0)
    ex2 = _expand(rowform.T, e2_ref)
    xw = (xs_ref[0, rs, :] * ex2[:, :D_MODEL]).astype(BF16)
    half = D_MODEL // SSD_GROUPS
    upd = [_dot(b_ref[0, rs, g * SSD_STATE:(g + 1) * SSD_STATE].T.astype(BF16), xw[:, g * half:(g + 1) * half])
           for g in range(SSD_GROUPS)]
    st[...] = st[...] * ex2[0:1, D_MODEL:] + jnp.concatenate(upd, axis=1)


def _ssd_fwd_kernel(xs_ref, b_ref, c_ref, dt_ref, z_ref, sb_ref, pc_ref, e3_ref, dexp_ref, on_ref, o_ref, st):
    @pl.when(pl.program_id(1) == 0)
    def _():
        st[...] = jnp.zeros_like(st)

    for c in range(SSD_CPS):
        _ssd_fwd_chunk(slice(c * CHUNK, (c + 1) * CHUNK), c, xs_ref, b_ref, c_ref, dt_ref, z_ref, sb_ref, pc_ref,
                       e3_ref, dexp_ref, on_ref, o_ref, st)


def _ssd_fwd_chunk(rs, c, xs_ref, b_ref, c_ref, dt_ref, z_ref, sb_ref, pc_ref, e3_ref, dexp_ref, on_ref, o_ref, st):
    xs = xs_ref[0, rs, :]
    dt_t = dt_ref[0, rs, :].T
    dtf = _softplus(dt_t[0:N_HEADS, :] + pc_ref[:, 0:1])
    dtb = _softplus(dt_t[N_HEADS:2 * N_HEADS, :] + pc_ref[:, 1:2])
    af = dtf * pc_ref[:, 2:3]
    ab = dtb * pc_ref[:, 3:4]
    csf = _dot3(af, _tri("le").astype(BF16))
    rcs = _dot3(ab, _tri("ge").astype(BF16))
    totf = csf[:, CHUNK - 1:CHUNK]
    rowform = jnp.concatenate([dtf * jnp.exp(totf - csf), jnp.exp(csf), jnp.exp(rcs), csf, rcs,
                               jnp.zeros((CHUNK - 5 * N_HEADS, CHUNK), F32)], axis=0)
    cols = rowform.T
    ex3 = _expand(cols, e3_ref)
    w_state, e_f, e_b = ex3[:, :D_MODEL], ex3[:, D_MODEL:2 * D_MODEL], ex3[:, 2 * D_MODEL:]

    xb = xs.astype(BF16)
    lower, upper = _tri("ge"), _tri("le")
    head_a = lax.broadcasted_iota(I32, (CHUNK, LANE), 1) < HEAD_DIM
    half = D_MODEL // SSD_GROUPS
    hpg = N_HEADS // SSD_GROUPS
    st_all = st[...]
    sb_all = sb_ref[0, c]
    ys = []
    b_t = []
    for g in range(SSD_GROUPS):
        bg = b_ref[0, rs, g * SSD_STATE:(g + 1) * SSD_STATE]
        cg = c_ref[0, rs, g * SSD_STATE:(g + 1) * SSD_STATE].astype(BF16)
        b_t.append(bg.T.astype(BF16))
        gm = _dot_nt(cg, bg.astype(BF16))
        states = jnp.concatenate([st_all[:, g * half:(g + 1) * half].astype(BF16),
                                  sb_all[:, g * half:(g + 1) * half]], axis=1)
        off = _dot(cg, states)
        y_off = (off[:, :half] * e_f[:, g * half:(g + 1) * half]
                 + off[:, half:] * e_b[:, g * half:(g + 1) * half])
        for pair in range(hpg // 2):
            ms = []
            for h in (g * hpg + 2 * pair, g * hpg + 2 * pair + 1):
                dec_f = jnp.where(lower, jnp.exp(cols[:, 3 * N_HEADS + h:3 * N_HEADS + h + 1] - csf[h:h + 1, :]), 0.0)
                dec_b = jnp.where(upper, jnp.exp(cols[:, 4 * N_HEADS + h:4 * N_HEADS + h + 1] - rcs[h:h + 1, :]), 0.0)
                ms.append((gm * (dec_f * dtf[h:h + 1, :] + dec_b * dtb[h:h + 1, :])).astype(BF16))
            lo = g * half + pair * LANE
            xp = xb[:, lo:lo + LANE]
            ys.append(jnp.where(head_a, _dot(ms[0], xp), _dot(ms[1], xp))
                      + y_off[:, pair * LANE:(pair + 1) * LANE])
    y = jnp.concatenate(ys, axis=1) + xs * dexp_ref[...]
    yz = y * _silu(z_ref[0, rs, :])
    o_ref[0, rs, :] = _rms(yz, on_ref[...]).astype(BF16)

    xw = (xs * w_state).astype(BF16)
    upd = [_dot(b_t[g], xw[:, g * half:(g + 1) * half]) for g in range(SSD_GROUPS)]
    st[...] = st_all * e_f[CHUNK - 1:CHUNK, :] + jnp.concatenate(upd, axis=1)


def _ssd(xbc_act, dt, z, pc, dexp, out_norm):
    b, seq, _ = xbc_act.shape
    nc = seq // CHUNK
    sel = np.zeros((2 * CHUNK, 3 * D_MODEL), np.float32)
    for part in range(3):
        for h in range(N_HEADS):
            for rep in range(2):
                sel[rep * CHUNK + part * N_HEADS + h, part * D_MODEL + h * HEAD_DIM:part * D_MODEL + (h + 1) * HEAD_DIM] = 1.0
    e3 = jnp.asarray(sel, BF16)
    e2 = jnp.asarray(sel[:, :2 * D_MODEL], BF16)

    rows = SSD_CPS * CHUNK
    nblk = seq // rows
    bcol = D_MODEL // (2 * SSD_STATE)
    rev = lambda bi, c: (bi, nblk - 1 - c, 0)
    sb = pl.pallas_call(
        _ssd_bwd_kernel, grid=(b, nblk),
        in_specs=[pl.BlockSpec((1, rows, D_MODEL), rev),
                  pl.BlockSpec((1, rows, 2 * SSD_STATE), lambda bi, c: (bi, nblk - 1 - c, bcol)),
                  pl.BlockSpec((1, rows, LANE), rev), _const_spec(pc.shape), _const_spec(e2.shape)],
        out_specs=pl.BlockSpec((1, SSD_CPS, SSD_STATE, D_MODEL), lambda bi, c: (bi, nblk - 1 - c, 0, 0)),
        out_shape=jax.ShapeDtypeStruct((b, nc, SSD_STATE, D_MODEL), BF16),
        scratch_shapes=[pltpu.VMEM((SSD_STATE, D_MODEL), F32)],
        compiler_params=_cparams(2), name="ssd_bwd_state",
    )(xbc_act, xbc_act, dt, pc, e2)

    fwd = lambda bi, c: (bi, c, 0)
    return pl.pallas_call(
        _ssd_fwd_kernel, grid=(b, nblk),
        in_specs=[pl.BlockSpec((1, rows, D_MODEL), fwd),
                  pl.BlockSpec((1, rows, 2 * SSD_STATE), lambda bi, c: (bi, c, bcol)),
                  pl.BlockSpec((1, rows, 2 * SSD_STATE), lambda bi, c: (bi, c, bcol + 1)),
                  pl.BlockSpec((1, rows, LANE), fwd), pl.BlockSpec((1, rows, D_MODEL), fwd),
                  pl.BlockSpec((1, SSD_CPS, SSD_STATE, D_MODEL), lambda bi, c: (bi, c, 0, 0)),
                  _const_spec(pc.shape), _const_spec(e3.shape), _const_spec((1, D_MODEL)), _const_spec((1, D_MODEL))],
        out_specs=pl.BlockSpec((1, rows, D_MODEL), fwd),
        out_shape=jax.ShapeDtypeStruct((b, seq, D_MODEL), BF16),
        scratch_shapes=[pltpu.VMEM((SSD_STATE, D_MODEL), F32)],
        compiler_params=_cparams(2), name="ssd_fwd",
    )(xbc_act, xbc_act, xbc_act, dt, z, sb, pc, e3, dexp, out_norm)


def _norm_and_route(x1, fg_ref, wr_hi_ref, wr_lo_ref, hn_ref, aff_ref):
    hn = _rms(x1, fg_ref[...])
    hi, lo = _split2(hn)
    hn_ref[...] = hi
    logits = _dot(hi, wr_hi_ref[...]) + _dot(lo, wr_hi_ref[...]) + _dot(hi, wr_lo_ref[...])
    lt = logits.T[0:N_EXPERTS, :]
    e = jnp.exp(lt - jnp.max(lt, axis=0, keepdims=True))
    aff_ref[0] = e / jnp.sum(e, axis=0, keepdims=True)


def _out_even_kernel(x_ref, a_ref, y_ref, wa_ref, wy_ref, fg_ref, wr_hi_ref, wr_lo_ref, x1_ref, hn_ref, aff_ref):
    x1 = x_ref[...] + _dot(a_ref[...], wa_ref[...]) + _dot(y_ref[...], wy_ref[...])
    x1_ref[...] = x1
    _norm_and_route(x1, fg_ref, wr_hi_ref, wr_lo_ref, hn_ref, aff_ref)


def _out_odd_kernel(x_ref, gb_ref, cu_ref, cp_ref, cn_ref, cw_ref, wo_ref, fg_ref, wr_hi_ref, wr_lo_ref,
                    x1_ref, hn_ref, aff_ref, *, tiles_per_seq):
    i = pl.program_id(0) % tiles_per_seq
    first, last = i == 0, i == tiles_per_seq - 1
    cur, prev8, next8 = cu_ref[...], cp_ref[...], cn_ref[...]
    conv = sum(_shifted(cur, prev8, next8, j - SHORT_CONV // 2, first, last) * cw_ref[j:j + 1, :]
               for j in range(SHORT_CONV))
    x1 = x_ref[...] + _dot((gb_ref[...] * conv).astype(BF16), wo_ref[...])
    x1_ref[...] = x1
    _norm_and_route(x1, fg_ref, wr_hi_ref, wr_lo_ref, hn_ref, aff_ref)


def _route_outs(n, b, seq, tm):
    row = pl.BlockSpec((tm, D_MODEL), lambda i: (i, 0))
    aff = pl.BlockSpec((1, N_EXPERTS, tm), lambda i: (i // (seq // tm), 0, i % (seq // tm)))
    shapes = [jax.ShapeDtypeStruct((n, D_MODEL), F32), jax.ShapeDtypeStruct((n, D_MODEL), BF16),
              jax.ShapeDtypeStruct((b, N_EXPERTS, seq), F32)]
    return [row, row, aff], shapes


def _out_even(x, attn, y, wa, wy, fg, wr_hi, wr_lo, b, seq):
    n = x.shape[0]
    tm = OUT_ROW_TILE
    row = pl.BlockSpec((tm, D_MODEL), lambda i: (i, 0))
    out_specs, shapes = _route_outs(n, b, seq, tm)
    return pl.pallas_call(
        _out_even_kernel, grid=(n // tm,),
        in_specs=[row, row, row, _const_spec(wa.shape), _const_spec(wy.shape), _const_spec((1, D_MODEL)),
                  _const_spec(wr_hi.shape), _const_spec(wr_lo.shape)],
        out_specs=out_specs, out_shape=shapes, compiler_params=_cparams(1), name="out_even",
    )(x, attn, y, wa, wy, fg, wr_hi, wr_lo)


def _out_odd(x, gb, cu, cw, wo, fg, wr_hi, wr_lo, b, seq):
    n = x.shape[0]
    tm = OUT_ROW_TILE
    row = pl.BlockSpec((tm, D_MODEL), lambda i: (i, 0))
    prev = pl.BlockSpec((8, D_MODEL), lambda i: (jnp.maximum(i * (tm // 8) - 1, 0), 0))
    nxt = pl.BlockSpec((8, D_MODEL), lambda i: (jnp.minimum((i + 1) * (tm // 8), n // 8 - 1), 0))
    out_specs, shapes = _route_outs(n, b, seq, tm)
    return pl.pallas_call(
        functools.partial(_out_odd_kernel, tiles_per_seq=seq // tm), grid=(n // tm,),
        in_specs=[row, row, row, prev, nxt, pl.BlockSpec((8, D_MODEL), lambda i: (0, 0)), _const_spec(wo.shape),
                  _const_spec((1, D_MODEL)), _const_spec(wr_hi.shape), _const_spec(wr_lo.shape)],
        out_specs=out_specs, out_shape=shapes, compiler_params=_cparams(1), name="out_odd",
    )(x, gb, cu, cu, cu, cw, wo, fg, wr_hi, wr_lo)


def _in_odd_kernel(x_ref, g_ref, wb_ref, wc_ref, wu_ref, gb_ref, cu_ref):
    hb = _rms(x_ref[...], g_ref[...]).astype(BF16)
    gb_ref[...] = _dot(hb, wb_ref[...])
    cu_ref[...] = _dot(hb, wc_ref[...]) * _dot(hb, wu_ref[...])


def _in_odd(x, g, wb, wc, wu):
    n = x.shape[0]
    tm = ROW_TILE
    row = pl.BlockSpec((tm, D_MODEL), lambda i: (i, 0))
    return pl.pallas_call(
        _in_odd_kernel, grid=(n // tm,),
        in_specs=[row, _const_spec((1, D_MODEL)), _const_spec(wb.shape), _const_spec(wc.shape), _const_spec(wu.shape)],
        out_specs=[row, row], out_shape=[jax.ShapeDtypeStruct((n, D_MODEL), F32)] * 2,
        compiler_params=_cparams(1), name="in_odd",
    )(x, g, wb, wc, wu)


def _count(mask):
    return jnp.sum(jnp.sum(mask.astype(F32), axis=0, keepdims=True), axis=1, keepdims=True)


def _route_kernel(aff_ref, incl_ref, ones_ref, strict_ref, local_ref, group_ref, first_ref, slot_ref, off_ref, end_ref,
                  *, cap):
    def step(i, thrs):
        bit = jnp.int32(1) << (30 - i)
        out = []
        for e in range(N_EXPERTS):
            cand = thrs[e] | bit
            out.append(jnp.where(_count(pltpu.bitcast(aff_ref[0, e], I32) >= cand) >= cap, cand, thrs[e]))
        return tuple(out)

    thrs = lax.fori_loop(0, 31, step, tuple(jnp.zeros((1, 1), I32) for _ in range(N_EXPERTS)))
    for e in range(N_EXPERTS):
        bits = pltpu.bitcast(aff_ref[0, e], I32)
        thr = thrs[e]
        gt = bits > thr
        eq = (bits == thr).astype(BF16)
        eq_rank = _dot(eq, incl_ref[...]) + _dot(strict_ref[...], _dot(eq, ones_ref[...]).astype(BF16))
        sel = (gt | ((bits == thr) & (eq_rank <= cap - _count(gt)))).astype(BF16)
        within = _dot(sel, incl_ref[...])
        totals = _dot(sel, ones_ref[...]).astype(BF16)
        local = _dot(local_ref[...], totals)
        cnt = _dot(group_ref[...], totals)
        padded = jnp.floor((cnt + (SLOT_ALIGN - 1)) * (1.0 / SLOT_ALIGN)) * SLOT_ALIGN
        start = _dot(first_ref[...], padded.astype(BF16))
        slot_ref[0, e] = jnp.where(sel > 0, (start + local + within).astype(I32) - 1, -1)
        off_ref[0, e:e + 1, :] = start.T[0:1, :].astype(I32)
        end_ref[0, e:e + 1, :] = (start + padded).T[0:1, :].astype(I32)


def _route(aff, cap):
    b, _, seq = aff.shape
    nt = seq // LANE
    tri = np.arange(LANE)
    tt = np.arange(nt)
    grp = tt // SUB
    as_bf16 = lambda m: jnp.asarray(m, BF16)
    incl = as_bf16(tri[:, None] <= tri[None, :])
    strict = as_bf16(tt[None, :] < tt[:, None])
    local = as_bf16((tt[None, :] < tt[:, None]) & (grp[None, :] == grp[:, None]))
    group = as_bf16(grp[None, :] == grp[:, None])
    first = as_bf16((grp[None, :] < grp[:, None]) & (tt[None, :] % SUB == 0))
    ones = jnp.ones((LANE, LANE), BF16)
    tiles = pl.BlockSpec((1, N_EXPERTS, nt, LANE), lambda i: (i, 0, 0, 0))
    rows = pl.BlockSpec((1, N_EXPERTS, nt), lambda i: (i, 0, 0))
    return pl.pallas_call(
        functools.partial(_route_kernel, cap=cap), grid=(b,),
        in_specs=[tiles, _const_spec((LANE, LANE)), _const_spec((LANE, LANE))] + [_const_spec((nt, nt))] * 4,
        out_specs=[tiles, rows, rows],
        out_shape=[jax.ShapeDtypeStruct((b, N_EXPERTS, nt, LANE), I32)] + [jax.ShapeDtypeStruct((b, N_EXPERTS, nt), I32)] * 2,
        compiler_params=_cparams(1), name="route",
    )(aff.reshape(b, N_EXPERTS, nt, LANE), incl, ones, strict, local, group, first)


def _one_hot(slots, base):
    return (slots == lax.broadcasted_iota(I32, (WINDOW, slots.shape[1]), 0) + base).astype(BF16)


def _gather_kernel(off_ref, end_ref, slot_ref, aff_ref, hn_ref, xe_hbm, stage, extra, sem, xsem):
    bi, j = pl.program_id(0), pl.program_id(1)
    n_j = pl.num_programs(1)
    step = bi * n_j + j
    cur = step % 2

    def window_copy(bb, jj, e, buf):
        start = pl.multiple_of(off_ref[bb, e, jj * SUB], SLOT_ALIGN)
        return pltpu.make_async_copy(stage.at[buf, e], xe_hbm.at[bb, e, pl.ds(start, WINDOW), :], sem.at[buf, e])

    tokens = hn_ref[0]
    parts = [p.astype(F32) for p in _split3(aff_ref[0, 0])]
    gates = jnp.concatenate(parts + [jnp.zeros((LANE - 3 * N_EXPERTS, TOK_TILE), F32)], axis=0).astype(BF16)

    def rows_of(p):
        return jnp.concatenate([_dot(p, tokens), _dot_nt(p, gates)], axis=1).astype(BF16)

    p_all = jnp.concatenate([_one_hot(slot_ref[0, 0, e:e + 1, :], off_ref[bi, e, j * SUB])
                             for e in range(N_EXPERTS)], axis=0)
    stage[cur] = rows_of(p_all).reshape(N_EXPERTS, WINDOW, D_MODEL + LANE)

    @pl.when(j > 0)
    def _():
        for e in range(N_EXPERTS):
            window_copy(bi, j - 1, e, 1 - cur).wait()

    for e in range(N_EXPERTS):
        window_copy(bi, j, e, cur).start()

    def overflow(e, c):
        first = off_ref[bi, e, j * SUB]
        n_win = (end_ref[bi, e, j * SUB] - first + WINDOW - 1) // WINDOW
        slots = slot_ref[0, 0, pl.ds(e, 1), :]

        def one(w, c2):
            base = pl.multiple_of(first + w * WINDOW, SLOT_ALIGN)
            extra[...] = rows_of(_one_hot(slots, base))
            cp = pltpu.make_async_copy(extra, xe_hbm.at[bi, e, pl.ds(base, WINDOW), :], xsem.at[0])
            cp.start()
            cp.wait()
            return c2

        return lax.fori_loop(1, n_win, one, c)

    lax.fori_loop(0, N_EXPERTS, overflow, 0)

    @pl.when(j == n_j - 1)
    def _():
        for e in range(N_EXPERTS):
            window_copy(bi, j, e, cur).wait()
        extra[...] = jnp.zeros_like(extra)
        cap_pad = xe_hbm.shape[2]

        def fill(e, c):
            used = end_ref[bi, e, end_ref.shape[2] - 1]
            n_big = (cap_pad - used) // WINDOW
            small0 = used + n_big * WINDOW
            n_small = (cap_pad - small0) // SLOT_ALIGN
            big = lambda i: pltpu.make_async_copy(
                extra, xe_hbm.at[bi, e, pl.ds(pl.multiple_of(used + i * WINDOW, SLOT_ALIGN), WINDOW), :], xsem.at[0])
            small = lambda i: pltpu.make_async_copy(
                extra.at[0:SLOT_ALIGN],
                xe_hbm.at[bi, e, pl.ds(pl.multiple_of(small0 + i * SLOT_ALIGN, SLOT_ALIGN), SLOT_ALIGN), :], xsem.at[0])
            for n, mk in ((n_big, big), (n_small, small)):
                lax.fori_loop(0, n, lambda i, c2, mk=mk: (mk(i).start(), c2)[1], 0)
            for n, mk in ((n_big, big), (n_small, small)):
                lax.fori_loop(0, n, lambda i, c2, mk=mk: (mk(i).wait(), c2)[1], 0)
            return c

        lax.fori_loop(0, N_EXPERTS, fill, 0)


def _gather(off, end, slot_t, aff_t, hn):
    b, seq, _ = hn.shape
    cap_pad = _cap_pad(seq)
    width = D_MODEL + LANE
    per_tile = pl.BlockSpec((1, 1, N_EXPERTS, TOK_TILE), lambda bi, j, *_: (bi, j, 0, 0))
    return pl.pallas_call(
        _gather_kernel,
        grid_spec=pltpu.PrefetchScalarGridSpec(
            num_scalar_prefetch=2, grid=(b, seq // TOK_TILE),
            in_specs=[per_tile, per_tile, pl.BlockSpec((1, TOK_TILE, D_MODEL), lambda bi, j, *_: (bi, j, 0))],
            out_specs=pl.BlockSpec(memory_space=pl.ANY),
            scratch_shapes=[pltpu.VMEM((2, N_EXPERTS, WINDOW, width), BF16), pltpu.VMEM((WINDOW, width), BF16),
                            pltpu.SemaphoreType.DMA((2, N_EXPERTS)), pltpu.SemaphoreType.DMA((1,))]),
        out_shape=jax.ShapeDtypeStruct((b, N_EXPERTS, cap_pad, width), BF16),
        compiler_params=_cparams(2), name="moe_gather",
    )(off, end, slot_t, aff_t, hn)


def _ffn_kernel(end_ref, xe_ref, wg_hbm, wu_hbm, wd_hbm, y_ref, stage_g, stage_u, stage_d, wg, wu, wd, sem, *, layer):
    e, bi, r = pl.program_id(0), pl.program_id(1), pl.program_id(2)
    used = end_ref[bi, e, end_ref.shape[2] - 1]
    pairs = ((wg_hbm, stage_g, wg), (wu_hbm, stage_u, wu), (wd_hbm, stage_d, wd))

    def weight_copies(ee):
        return [pltpu.make_async_copy(src.at[layer, ee], stg, sem.at[k]) for k, (src, stg, _) in enumerate(pairs)]

    @pl.when((bi == 0) & (r == 0))
    def _():
        @pl.when(e == 0)
        def _():
            for cp in weight_copies(e):
                cp.start()

        for cp, (_, stg, dst) in zip(weight_copies(e), pairs):
            cp.wait()
            n_rows = stg.shape[0]

            def cast(i, c, stg=stg, dst=dst):
                rows = pl.ds(pl.multiple_of(i * CAST_ROWS, CAST_ROWS), CAST_ROWS)
                dst[rows, :] = stg[rows, :].astype(BF16)
                return c

            lax.fori_loop(0, n_rows // CAST_ROWS, cast, 0)

        @pl.when(e + 1 < pl.num_programs(0))
        def _():
            for cp in weight_copies(e + 1):
                cp.start()

    @pl.when(r * FFN_ROWS < used)
    def _():
        xe = xe_ref[0, 0, :, 0:D_MODEL]
        hid = (_silu(_dot(xe, wg[...])) * _dot(xe, wu[...])).astype(BF16)
        g = xe_ref[0, 0, :, D_MODEL:].astype(F32)
        lane = lax.broadcasted_iota(I32, g.shape, 1)
        mine = (lane % N_EXPERTS == e) & (lane < 3 * N_EXPERTS)
        gate = jnp.sum(jnp.where(mine, g, 0.0), axis=1, keepdims=True)
        y_ref[0, 0] = (_dot(hid, wd[...]) * gate).astype(BF16)

    @pl.when(r * FFN_ROWS >= used)
    def _():
        y_ref[...] = jnp.zeros_like(y_ref)


def _ffn(end, xe, wg, wu, wd, layer):
    b, ne, cap_pad, width = xe.shape
    rows = lambda w: pl.BlockSpec((1, 1, FFN_ROWS, w), lambda e, bi, r, *_: (bi, e, r, 0))
    hbm = pl.BlockSpec(memory_space=pl.ANY)
    return pl.pallas_call(
        functools.partial(_ffn_kernel, layer=layer),
        grid_spec=pltpu.PrefetchScalarGridSpec(
            num_scalar_prefetch=1, grid=(ne, b, cap_pad // FFN_ROWS),
            in_specs=[rows(width), hbm, hbm, hbm],
            out_specs=rows(D_MODEL),
            scratch_shapes=[pltpu.VMEM((D_MODEL, D_FF), F32), pltpu.VMEM((D_MODEL, D_FF), F32),
                            pltpu.VMEM((D_FF, D_MODEL), F32), pltpu.VMEM((D_MODEL, D_FF), BF16),
                            pltpu.VMEM((D_MODEL, D_FF), BF16), pltpu.VMEM((D_FF, D_MODEL), BF16),
                            pltpu.SemaphoreType.DMA((3,))]),
        out_shape=jax.ShapeDtypeStruct((b, ne, cap_pad, D_MODEL), BF16),
        compiler_params=_cparams(3), name="moe_ffn",
    )(end, xe, wg, wu, wd)


def _combine_kernel(off_ref, end_ref, slot_ref, y_hbm, x1_ref, o_ref, win, extra, sem, xsem):
    bi, j = pl.program_id(0), pl.program_id(1)
    n_j = pl.num_programs(1)
    step = bi * n_j + j
    cur = step % 2

    def window_copy(bb, jj, e, buf):
        start = pl.multiple_of(off_ref[bb, e, jj * SUB], SLOT_ALIGN)
        return pltpu.make_async_copy(y_hbm.at[bb, e, pl.ds(start, WINDOW), :], win.at[buf, e], sem.at[buf, e])

    @pl.when(step == 0)
    def _():
        for e in range(N_EXPERTS):
            window_copy(bi, j, e, cur).start()

    @pl.when(step + 1 < pl.num_programs(0) * n_j)
    def _():
        nxt = step + 1
        for e in range(N_EXPERTS):
            window_copy(nxt // n_j, nxt % n_j, e, 1 - cur).start()

    ps = []
    for e in range(N_EXPERTS):
        window_copy(bi, j, e, cur).wait()
        ps.append(_one_hot(slot_ref[0, 0, e:e + 1, :], off_ref[bi, e, j * SUB]))
    p_all = jnp.concatenate(ps, axis=0)
    y_all = win[cur].reshape(N_EXPERTS * WINDOW, D_MODEL)
    o_ref[0] = x1_ref[0] + _dot_tn(p_all, y_all)

    def overflow(e, c):
        first = off_ref[bi, e, j * SUB]
        n_win = (end_ref[bi, e, j * SUB] - first + WINDOW - 1) // WINDOW
        slots = slot_ref[0, 0, pl.ds(e, 1), :]

        def one(w, c2):
            base = pl.multiple_of(first + w * WINDOW, SLOT_ALIGN)
            cp = pltpu.make_async_copy(y_hbm.at[bi, e, pl.ds(base, WINDOW), :], extra, xsem.at[0])
            cp.start()
            cp.wait()
            o_ref[0] += _dot_tn(_one_hot(slots, base), extra[...])
            return c2

        return lax.fori_loop(1, n_win, one, c)

    lax.fori_loop(0, N_EXPERTS, overflow, 0)


def _combine(off, end, slot_t, y, x1):
    b, seq, _ = x1.shape
    tile = pl.BlockSpec((1, TOK_TILE, D_MODEL), lambda bi, j, *_: (bi, j, 0))
    return pl.pallas_call(
        _combine_kernel,
        grid_spec=pltpu.PrefetchScalarGridSpec(
            num_scalar_prefetch=2, grid=(b, seq // TOK_TILE),
            in_specs=[pl.BlockSpec((1, 1, N_EXPERTS, TOK_TILE), lambda bi, j, *_: (bi, j, 0, 0)),
                      pl.BlockSpec(memory_space=pl.ANY), tile],
            out_specs=tile,
            scratch_shapes=[pltpu.VMEM((2, N_EXPERTS, WINDOW, D_MODEL), BF16), pltpu.VMEM((WINDOW, D_MODEL), BF16),
                            pltpu.SemaphoreType.DMA((2, N_EXPERTS)), pltpu.SemaphoreType.DMA((1,))]),
        out_shape=jax.ShapeDtypeStruct((b, seq, D_MODEL), F32),
        compiler_params=_cparams(2), name="moe_combine",
    )(off, end, slot_t, y, x1)


def _cap_pad(seq):
    cap = CAPACITY_FACTOR * seq // N_EXPERTS
    worst = cap + (seq // TOK_TILE) * (SLOT_ALIGN - 1) + WINDOW
    return -(-worst // FFN_ROWS) * FFN_ROWS


def _moe(x1, hn, aff, wg, wu, wd, layer):
    b, seq, _ = x1.shape
    cap = CAPACITY_FACTOR * seq // N_EXPERTS
    slot, off, end = _route(aff, cap)
    per_tile = lambda a: jnp.swapaxes(a.reshape(b, N_EXPERTS, seq // TOK_TILE, TOK_TILE), 1, 2)
    slot_t = per_tile(slot)
    xe = _gather(off, end, slot_t, per_tile(aff), hn)
    y = _ffn(end, xe, wg, wu, wd, layer)
    return _combine(off, end, slot_t, y, x1)


def _rope_tables(seq):
    inv_freq = ROPE_THETA ** (-jnp.arange(ROPE_HALF, dtype=F32) * 2.0 / (2 * ROPE_HALF))
    ang = jnp.arange(seq, dtype=F32)[:, None] * inv_freq[None, :]
    cos, sin = jnp.cos(ang), jnp.sin(ang)
    z = lambda w: jnp.zeros((seq, w), F32)
    rest = HEAD_DIM - 2 * ROPE_HALF
    rc = jnp.concatenate([cos, cos, jnp.ones((seq, rest), F32)], axis=1)
    r1 = jnp.concatenate([-sin, z(ROPE_HALF + rest)], axis=1)
    r2 = jnp.concatenate([z(ROPE_HALF), sin, z(rest)], axis=1)
    return tuple(jnp.tile(t, (1, LANE // HEAD_DIM)) for t in (rc, r1, r2))


def _router_split(w):
    wp = jnp.pad(w, ((0, 0), (0, LANE - N_EXPERTS)))
    hi = wp.astype(BF16)
    return hi, (wp - hi.astype(F32)).astype(BF16)


def kernel(x, attn_norm, w_in_even, q_norm, k_norm, ssd_conv_w, ssd_conv_b, ssd_a_log_fwd, ssd_a_log_bwd,
           ssd_dt_bias_fwd, ssd_dt_bias_bwd, ssd_d, ssd_out_norm, w_out_even, conv_norm, conv_w_in, conv_w,
           conv_w_out, ffn_norm, router_w, expert_w_gate, expert_w_up, expert_w_down):
    b, seq, _ = x.shape
    n = b * seq
    depth = ffn_norm.shape[0]
    rc, r1, r2 = _rope_tables(seq)
    blk = np.arange(256) // HEAD_DIM
    bd = jnp.asarray((blk[:, None] == blk[None, :]) / HEAD_DIM, BF16)
    row = lambda v: v.reshape(1, -1).astype(F32)

    xf = x.reshape(n, D_MODEL)
    for layer in range(depth):
        i = layer // 2
        wr_hi, wr_lo = _router_split(router_w[layer])
        fg = row(ffn_norm[layer])
        if layer % 2 == 0:
            w = w_in_even[i].astype(BF16)
            o = np.cumsum([0, D_MODEL, D_MODEL, D_MODEL, D_MODEL, SSD_XBC, N_HEADS, N_HEADS])
            wq, wk, wv, wz, wx = (w[:, o[j]:o[j + 1]] for j in range(5))
            wd = jnp.pad(w[:, o[5]:o[7]], ((0, 0), (0, LANE - 2 * N_HEADS)))
            tile_heads = lambda g: row(jnp.tile(g, N_HEADS))
            q, k, v, z, xbc, dt = _in_even(xf, row(attn_norm[i]), wq, wk, wv, wz, wx, wd,
                                           tile_heads(q_norm[i]), tile_heads(k_norm[i]), bd, rc, r1, r2, seq)
            as3 = lambda t: t.reshape(b, seq, -1)
            attn = _attention(as3(q), as3(k), as3(v))
            cw = jnp.pad(ssd_conv_w[i], ((0, 8 - SSD_CONV), (0, 0)))
            act = _ssd_conv(as3(xbc), cw, row(ssd_conv_b[i]))
            pc = jnp.pad(jnp.stack([ssd_dt_bias_fwd[i], ssd_dt_bias_bwd[i], -jnp.exp(ssd_a_log_fwd[i]),
                                    -jnp.exp(ssd_a_log_bwd[i])], axis=1).astype(F32), ((0, 0), (0, LANE - 4)))
            y = _ssd(act, as3(dt), as3(z), pc, row(jnp.repeat(ssd_d[i], HEAD_DIM)), row(ssd_out_norm[i]))
            wo = w_out_even[i].astype(BF16)
            x1, hn, aff = _out_even(xf, attn.reshape(n, D_MODEL), y.reshape(n, D_MODEL), wo[:D_MODEL], wo[D_MODEL:],
                                    fg, wr_hi, wr_lo, b, seq)
        else:
            w = conv_w_in[i].astype(BF16)
            gb, cu = _in_odd(xf, row(conv_norm[i]), w[:, :D_MODEL], w[:, D_MODEL:2 * D_MODEL], w[:, 2 * D_MODEL:])
            cw = jnp.pad(conv_w[i], ((0, 8 - SHORT_CONV), (0, 0)))
            x1, hn, aff = _out_odd(xf, gb, cu, cw, conv_w_out[i].astype(BF16), fg, wr_hi, wr_lo, b, seq)
        xf = _moe(x1.reshape(b, seq, D_MODEL), hn.reshape(b, seq, D_MODEL), aff, expert_w_gate, expert_w_up,
                  expert_w_down, layer).reshape(n, D_MODEL)
    return xf.reshape(b, seq, D_MODEL)
```

```python
import functools
import math

import jax
import jax.numpy as jnp
import numpy as np
from jax import lax
from jax.experimental import pallas as pl
from jax.experimental.pallas import tpu as pltpu

F32, BF16, I32 = jnp.float32, jnp.bfloat16, jnp.int32

D_MODEL = 1024
N_HEADS = 16
HEAD_DIM = 64
ROPE_HALF = 8
ROPE_THETA = 500000.0
PATTERNS = ((128, 1), (512, 4), (2048, 16))
HALF_STEPS = 64
SSD_GROUPS = 2
SSD_STATE = 128
SSD_XBC = 1536
SSD_CONV = 5
CHUNK = 128
SSD_CPS = 4
N_EXPERTS = 16
CAPACITY_FACTOR = 2
D_FF = 2048
SHORT_CONV = 3
EPS = 1e-6

LANE = 128
VMEM_LIMIT = 56 * 1024 * 1024

ROW_TILE = 512
OUT_ROW_TILE = 256
ATTN_SUPER = 2048
ATTN_HALO = 64
ATTN_TQ = 128
ATTN_TK = ATTN_TQ + 2 * HALF_STEPS
ATTN_UNROLL = 16
TOK_TILE = 512
SUB = TOK_TILE // LANE
SLOT_ALIGN = 16
WINDOW = 96
FFN_ROWS = 256
CAST_ROWS = 64


def _cparams(n_axes):
    return pltpu.CompilerParams(dimension_semantics=("arbitrary",) * n_axes, vmem_limit_bytes=VMEM_LIMIT)


def _const_spec(shape):
    nd = len(shape)
    return pl.BlockSpec(shape, lambda *_: (0,) * nd, pipeline_mode=pl.Buffered(1))


def _dot(a, b):
    return jnp.dot(a, b, preferred_element_type=F32)


def _dot_nt(a, b):
    return lax.dot_general(a, b, (((1,), (1,)), ((), ())), preferred_element_type=F32)


def _dot_tn(a, b):
    return lax.dot_general(a, b, (((0,), (0,)), ((), ())), preferred_element_type=F32)


def _split2(x):
    hi = x.astype(BF16)
    lo = (x - hi.astype(F32)).astype(BF16)
    return hi, lo


def _split3(x):
    hi = x.astype(BF16)
    r = x - hi.astype(F32)
    mid = r.astype(BF16)
    lo = (r - mid.astype(F32)).astype(BF16)
    return hi, mid, lo


def _dot3(x, m_bf16):
    hi, mid, lo = _split3(x)
    return _dot(hi, m_bf16) + _dot(mid, m_bf16) + _dot(lo, m_bf16)


def _rms(x, g):
    return x * lax.rsqrt(jnp.mean(x * x, axis=-1, keepdims=True) + EPS) * g


def _silu(x):
    return x * jax.nn.sigmoid(x)


def _in_even_kernel(x_ref, g_ref, wq_ref, wk_ref, wv_ref, wz_ref, wx_ref, wd_ref, qg_ref, kg_ref, bd_ref,
                    rc_ref, r1_ref, r2_ref, q_ref, k_ref, v_ref, z_ref, xbc_ref, dt_ref):
    hb = _rms(x_ref[...], g_ref[...]).astype(BF16)
    bd = bd_ref[...]
    rc, r1, r2 = rc_ref[...], r1_ref[...], r2_ref[...]

    def head_norm_rope(w_ref, gain_ref, out_ref):
        t = _dot(hb, w_ref[...])
        for c in range(D_MODEL // 256):
            tc = t[:, c * 256:(c + 1) * 256]
            sq_hi, sq_lo = _split2(tc * tc)
            ms = _dot(sq_hi, bd) + _dot(sq_lo, bd)
            tn = tc * lax.rsqrt(ms + EPS) * gain_ref[:, c * 256:(c + 1) * 256]
            for hh in range(2):
                u = tn[:, hh * LANE:(hh + 1) * LANE]
                r = u * rc + pltpu.roll(u, LANE - ROPE_HALF, 1) * r1 + pltpu.roll(u, ROPE_HALF, 1) * r2
                out_ref[:, c * 256 + hh * LANE:c * 256 + (hh + 1) * LANE] = r

    head_norm_rope(wq_ref, qg_ref, q_ref)
    head_norm_rope(wk_ref, kg_ref, k_ref)
    v_ref[...] = _dot(hb, wv_ref[...])
    z_ref[...] = _dot(hb, wz_ref[...])
    xbc_ref[...] = _dot(hb, wx_ref[...])
    dt_ref[...] = _dot(hb, wd_ref[...])


def _in_even(x, g, wq, wk, wv, wz, wx, wd, qg, kg, bd, rc, r1, r2, seq):
    n = x.shape[0]
    tm = ROW_TILE
    row = lambda w: pl.BlockSpec((tm, w), lambda i: (i, 0))
    tab = pl.BlockSpec((tm, LANE), lambda i: (i % (seq // tm), 0))
    outs = [jax.ShapeDtypeStruct((n, D_MODEL), F32)] * 4 + [jax.ShapeDtypeStruct((n, SSD_XBC), F32),
                                                           jax.ShapeDtypeStruct((n, LANE), F32)]
    return pl.pallas_call(
        _in_even_kernel, grid=(n // tm,),
        in_specs=[row(D_MODEL), _const_spec((1, D_MODEL)), _const_spec(wq.shape), _const_spec(wk.shape),
                  _const_spec(wv.shape), _const_spec(wz.shape), _const_spec(wx.shape), _const_spec(wd.shape),
                  _const_spec((1, D_MODEL)), _const_spec((1, D_MODEL)), _const_spec((256, 256)), tab, tab, tab],
        out_specs=[row(D_MODEL)] * 4 + [row(SSD_XBC), row(LANE)], out_shape=outs,
        compiler_params=_cparams(1), name="in_even",
    )(x, g, wq, wk, wv, wz, wx, wd, qg, kg, bd, rc, r1, r2)


def _attn_kernel(q_ref, k0, k1, k2, v0, v1, v2, qw_hbm, kw_hbm, vw_hbm, o_ref, qf, kf, vf, q16, k16, v16,
                 acc, mst, lst, a16, m16, l16, an3, mn3, ln3, sem, rsem, *, seq):
    dmax = PATTERNS[-1][1]
    bi, hp, j = pl.program_id(0), pl.program_id(1), pl.program_id(2)
    n_hp, n_j = pl.num_programs(1), pl.num_programs(2)
    p0 = j * ATTN_SUPER
    halo = HALF_STEPS
    step = (bi * n_hp + hp) * n_j + j
    cur = step % 2

    def residue_copies(at, buf, r, where):
        b_, hp_, j_ = at
        lanes = pl.ds(pl.multiple_of(hp_ * LANE, LANE), LANE)
        q0 = pl.multiple_of(j_ * ATTN_TQ, ATTN_TQ)
        n_kv = ATTN_TK - halo if where else ATTN_TK
        src0 = 0 if where < 0 else q0 - halo
        dst0 = halo if where < 0 else 0
        cps = [pltpu.make_async_copy(qw_hbm.at[b_, pl.ds(q0, ATTN_TQ), r, lanes], q16.at[buf, r], rsem.at[buf, 0, r])]
        for n, (src, dst) in enumerate(((kw_hbm, k16), (vw_hbm, v16))):
            cps.append(pltpu.make_async_copy(src.at[b_, pl.ds(src0, n_kv), r, lanes],
                                             dst.at[buf, r, pl.ds(dst0, n_kv), :], rsem.at[buf, 1 + n, r]))
        return cps

    def all_residues(at, buf, action):
        def run(where):
            for r in range(dmax):
                for cp in residue_copies(at, buf, r, where):
                    getattr(cp, action)()
        j_ = at[2]
        pl.when(j_ == 0)(functools.partial(run, -1))
        pl.when((j_ > 0) & (j_ < n_j - 1))(functools.partial(run, 0))
        pl.when(j_ == n_j - 1)(functools.partial(run, 1))

    def state_copies(r):
        return [pltpu.make_async_copy(src.at[r], dst.at[:, r, :], sem.at[n, r])
                for n, (src, dst) in enumerate(((a16, an3), (m16, mn3), (l16, ln3)))]

    @pl.when(step == 0)
    def _():
        k16[...] = jnp.zeros_like(k16)
        v16[...] = jnp.zeros_like(v16)
        all_residues((bi, hp, j), cur, "start")

    @pl.when(step + 1 < pl.num_programs(0) * n_hp * n_j)
    def _():
        nxt = step + 1
        all_residues((nxt // (n_hp * n_j), (nxt // n_j) % n_hp, nxt % n_j), 1 - cur, "start")

    head_a = lax.broadcasted_iota(I32, (ATTN_TQ, LANE), 1) < HEAD_DIM
    row_i = lax.broadcasted_iota(I32, (ATTN_TQ, ATTN_TK), 0)
    col_i = lax.broadcasted_iota(I32, (ATTN_TQ, ATTN_TK), 1)
    band = (col_i - row_i >= 0) & (col_i - row_i <= 2 * HALF_STEPS)
    colpos = lax.broadcasted_iota(I32, (1, ATTN_TK), 1)

    def local_softmax(q, kt, vt, mask, pos):
        ok = mask & (pos >= 0) & (pos < seq)
        valid = jnp.concatenate([ok, ok], axis=0)
        q2 = jnp.concatenate([jnp.where(head_a, q, 0.0), jnp.where(head_a, 0.0, q)], axis=0).astype(BF16)
        s = jnp.where(valid, _dot_nt(q2, kt.astype(BF16)), -jnp.inf)
        m = jnp.max(s, axis=1, keepdims=True)
        p = jnp.exp(s - m)
        l = jnp.sum(p, axis=1, keepdims=True)
        n = _dot(p.astype(BF16), vt.astype(BF16))
        return (jnp.where(head_a, m[:ATTN_TQ], m[ATTN_TQ:]), jnp.where(head_a, l[:ATTN_TQ], l[ATTN_TQ:]),
                jnp.where(head_a, n[:ATTN_TQ], n[ATTN_TQ:]))

    def visit_dense(i):
        qs = pl.multiple_of(i * ATTN_TQ, ATTN_TQ)
        rows = pl.ds(qs, ATTN_TQ)
        keys = pl.ds(qs, ATTN_TK)
        mst[rows, :], lst[rows, :], acc[rows, :] = local_softmax(
            qf[rows, :], kf[keys, :], vf[keys, :], band, p0 + qs - HALF_STEPS + colpos)

    def visit_dmax(r):
        q = q16[cur, r] * (HEAD_DIM ** -0.5)
        m16[r], l16[r], a16[r] = local_softmax(q, k16[cur, r], v16[cur, r], band,
                                               p0 - dmax * HALF_STEPS + r + dmax * colpos)

    dmid = PATTERNS[1][1]
    per = dmax // dmid
    seg_q, seg_k = ATTN_TQ // per, ATTN_TK // per
    q_step = per * (row_i % seg_q) + row_i // seg_q
    k_step = per * (col_i % seg_k - HALF_STEPS // per) + col_i // seg_k
    mask_mid = jnp.abs(k_step - q_step) <= HALF_STEPS
    k_off_mid = dmax * (colpos % seg_k - HALF_STEPS // per) + dmid * (colpos // seg_k)

    def visit_mid(i):
        r = i & (dmid - 1)
        t = i >> (dmid.bit_length() - 1)
        n0 = pl.multiple_of(t * seg_q, seg_q)
        q_rows = pl.ds(n0, seg_q)
        k_rows = pl.ds(pl.multiple_of(n0 + HALF_STEPS - HALF_STEPS // per, 8), seg_k)
        classes = [r + dmid * c for c in range(per)]
        cat = lambda ref, rows, buf=None: jnp.concatenate(
            [ref[c, rows, :] if buf is None else ref[buf, c, rows, :] for c in classes], axis=0)
        m_loc, l_loc, n_loc = local_softmax(cat(q16, q_rows, cur) * (HEAD_DIM ** -0.5), cat(k16, k_rows, cur),
                                            cat(v16, k_rows, cur), mask_mid, p0 + dmax * n0 + r + k_off_mid)
        m_old = cat(m16, q_rows)
        m_new = jnp.maximum(m_old, m_loc)
        w_old = jnp.exp(m_old - m_new)
        w_loc = jnp.exp(m_loc - m_new)
        l_new = cat(l16, q_rows) * w_old + l_loc * w_loc
        a_new = cat(a16, q_rows) * w_old + n_loc * w_loc
        for n, c in enumerate(classes):
            seg = slice(n * seg_q, (n + 1) * seg_q)
            m16[c, q_rows, :], l16[c, q_rows, :], a16[c, q_rows, :] = m_new[seg], l_new[seg], a_new[seg]

    def loop(n, fn):
        def body(i, c):
            for u in range(ATTN_UNROLL):
                fn(i * ATTN_UNROLL + u)
            return c
        lax.fori_loop(0, n // ATTN_UNROLL, body, 0)

    n_visits = ATTN_SUPER // ATTN_TQ
    all_residues((bi, hp, j), cur, "wait")
    loop(dmax, visit_dmax)
    loop(n_visits, visit_mid)
    for r in range(dmax):
        for cp in state_copies(r):
            cp.start()

    qf[...] = q_ref[0] * (HEAD_DIM ** -0.5)
    row0 = 0
    for kr, vr in ((k0, v0), (k1, v1), (k2, v2)):
        n_rows = kr.shape[1]
        kf[row0:row0 + n_rows, :] = kr[0]
        vf[row0:row0 + n_rows, :] = vr[0]
        row0 += n_rows
    loop(n_visits, visit_dense)
    for r in range(dmax):
        for cp in state_copies(r):
            cp.wait()

    def finish(i, c):
        rows = pl.ds(pl.multiple_of(i * ATTN_TQ, ATTN_TQ), ATTN_TQ)
        slabs = pl.ds(pl.multiple_of(i * (ATTN_TQ // dmax), ATTN_TQ // dmax), ATTN_TQ // dmax)
        a_a, m_a, l_a = (t[slabs].reshape(ATTN_TQ, LANE) for t in (an3, mn3, ln3))
        m_b = mst[rows, :]
        m = jnp.maximum(m_a, m_b)
        w_a, w_b = jnp.exp(m_a - m), jnp.exp(m_b - m)
        o = (a_a * w_a + acc[rows, :] * w_b) / (l_a * w_a + lst[rows, :] * w_b)
        o_ref[0, rows, :] = o.astype(BF16)
        return c

    lax.fori_loop(0, n_visits, finish, 0)


def _attention(q, k, v):
    b, seq, _ = q.shape
    nblk = seq // ATTN_HALO
    ratio = ATTN_SUPER // ATTN_HALO
    dmax = PATTERNS[-1][1]
    assert ATTN_SUPER // dmax == ATTN_TQ
    assert len(PATTERNS) == 3 and ATTN_HALO == PATTERNS[0][1] * HALF_STEPS and seq // ATTN_SUPER >= 2

    def halo(after):
        return pl.BlockSpec((1, ATTN_HALO, LANE),
                            lambda bi, hp, j: (bi, jnp.clip(ratio * (j + after) - 1 + after, 0, nblk - 1), hp))

    main = pl.BlockSpec((1, ATTN_SUPER, LANE), lambda bi, hp, j: (bi, j, hp))
    kv_specs = [halo(0), main, halo(1)]
    hbm = pl.BlockSpec(memory_space=pl.ANY)
    by_residue = lambda t: t.reshape(b, seq // dmax, dmax, D_MODEL)
    return pl.pallas_call(
        functools.partial(_attn_kernel, seq=seq),
        grid=(b, D_MODEL // LANE, seq // ATTN_SUPER),
        in_specs=[main] + kv_specs * 2 + [hbm] * 3,
        out_specs=main,
        out_shape=jax.ShapeDtypeStruct((b, seq, D_MODEL), BF16),
        scratch_shapes=[pltpu.VMEM((ATTN_SUPER, LANE), F32), pltpu.VMEM((ATTN_SUPER + 2 * ATTN_HALO, LANE), F32),
                        pltpu.VMEM((ATTN_SUPER + 2 * ATTN_HALO, LANE), F32), pltpu.VMEM((2, dmax, ATTN_TQ, LANE), F32),
                        pltpu.VMEM((2, dmax, ATTN_TK, LANE), F32), pltpu.VMEM((2, dmax, ATTN_TK, LANE), F32)]
                       + [pltpu.VMEM((ATTN_SUPER, LANE), F32)] * 3
                       + [pltpu.VMEM((dmax, ATTN_TQ, LANE), F32)] * 3
                       + [pltpu.VMEM((ATTN_TQ, dmax, LANE), F32)] * 3
                       + [pltpu.SemaphoreType.DMA((3, dmax)), pltpu.SemaphoreType.DMA((2, 3, dmax))],
        compiler_params=_cparams(3), name="dilated_attn",
    )(q, k, k, k, v, v, v, by_residue(q), by_residue(k), by_residue(v))


def _shifted(cur, prev8, next8, s, first, last):
    n = cur.shape[0]
    if s == 0:
        return cur
    rows = lax.broadcasted_iota(I32, cur.shape, 0)
    out = pltpu.roll(cur, (-s) % n, 0)
    if s < 0:
        for j in range(-s):
            src = jnp.where(first, 0.0, prev8[8 + s + j:8 + s + j + 1, :])
            out = jnp.where(rows == j, src, out)
    else:
        for j in range(s):
            src = jnp.where(last, 0.0, next8[j:j + 1, :])
            out = jnp.where(rows == n - s + j, src, out)
    return out


def _ssd_conv_kernel(c_ref, p_ref, n_ref, w_ref, b_ref, o_ref):
    first = pl.program_id(1) == 0
    last = pl.program_id(1) == pl.num_programs(1) - 1
    for c in range(SSD_XBC // 256):
        sl = slice(c * 256, (c + 1) * 256)
        cur, prev8, next8 = c_ref[0, :, sl], p_ref[0, :, sl], n_ref[0, :, sl]
        y = b_ref[:, sl]
        for j in range(SSD_CONV):
            y = y + _shifted(cur, prev8, next8, j - SSD_CONV // 2, first, last) * w_ref[j:j + 1, sl]
        o_ref[0, :, sl] = _silu(y)


def _halo_specs(tm, width, seq):
    cur = pl.BlockSpec((1, tm, width), lambda b, i: (b, i, 0))
    prev = pl.BlockSpec((1, 8, width), lambda b, i: (b, jnp.maximum(i * (tm // 8) - 1, 0), 0))
    nxt = pl.BlockSpec((1, 8, width), lambda b, i: (b, jnp.minimum((i + 1) * (tm // 8), seq // 8 - 1), 0))
    return cur, prev, nxt


def _ssd_conv(xbc, w, bias):
    b, seq, width = xbc.shape
    tm = 512
    cur, prev, nxt = _halo_specs(tm, width, seq)
    return pl.pallas_call(
        _ssd_conv_kernel, grid=(b, seq // tm),
        in_specs=[cur, prev, nxt, pl.BlockSpec((8, width), lambda b, i: (0, 0)),
                  pl.BlockSpec((1, width), lambda b, i: (0, 0))],
        out_specs=cur, out_shape=jax.ShapeDtypeStruct(xbc.shape, F32),
        compiler_params=_cparams(2), name="ssd_conv",
    )(xbc, xbc, xbc, w, bias)


def _softplus(x):
    return jnp.maximum(x, 0.0) + jnp.log1p(jnp.exp(-jnp.abs(x)))


def _tri(kind):
    s = lax.broadcasted_iota(I32, (CHUNK, CHUNK), 0)
    l = lax.broadcasted_iota(I32, (CHUNK, CHUNK), 1)
    return {"le": s <= l, "ge": s >= l, "lt": s < l}[kind]


def _expand(cols, e2_ref):
    hi, lo = _split2(cols)
    return _dot(jnp.concatenate([hi, lo], axis=1), e2_ref[...])


def _ssd_bwd_kernel(xs_ref, b_ref, dt_ref, pc_ref, e2_ref, sb_ref, st):
    @pl.when(pl.program_id(1) == 0)
    def _():
        st[...] = jnp.zeros_like(st)

    for c in reversed(range(SSD_CPS)):
        _ssd_bwd_chunk(slice(c * CHUNK, (c + 1) * CHUNK), c, xs_ref, b_ref, dt_ref, pc_ref, e2_ref, sb_ref, st)


def _ssd_bwd_chunk(rs, c, xs_ref, b_ref, dt_ref, pc_ref, e2_ref, sb_ref, st):
    sb_ref[0, c] = st[...].astype(BF16)
    dt_t = dt_ref[0, rs, :].T
    dtb = _softplus(dt_t[N_HEADS:2 * N_HEADS, :] + pc_ref[:, 1:2])
    a = dtb * pc_ref[:, 3:4]
    ex = _dot3(a, _tri("lt").astype(BF16))
    tot = ex[:, CHUNK - 1:CHUNK] + a[:, CHUNK - 1:CHUNK]
    rowform = jnp.concatenate([dtb * jnp.exp(ex), jnp.broadcast_to(jnp.exp(tot), (N_HEADS, CHUNK)),
                               jnp.zeros((CHUNK - 2 * N_HEADS, CHUNK), F32)], axis=0)
    ex2 = _expand(rowform.T, e2_ref)
    xw = (xs_ref[0, rs, :] * ex2[:, :D_MODEL]).astype(BF16)
    half = D_MODEL // SSD_GROUPS
    upd = [_dot(b_ref[0, rs, g * SSD_STATE:(g + 1) * SSD_STATE].T.astype(BF16), xw[:, g * half:(g + 1) * half])
           for g in range(SSD_GROUPS)]
    st[...] = st[...] * ex2[0:1, D_MODEL:] + jnp.concatenate(upd, axis=1)


def _ssd_fwd_kernel(xs_ref, b_ref, c_ref, dt_ref, z_ref, sb_ref, pc_ref, e3_ref, dexp_ref, on_ref, o_ref, st):
    @pl.when(pl.program_id(1) == 0)
    def _():
        st[...] = jnp.zeros_like(st)

    for c in range(SSD_CPS):
        _ssd_fwd_chunk(slice(c * CHUNK, (c + 1) * CHUNK), c, xs_ref, b_ref, c_ref, dt_ref, z_ref, sb_ref, pc_ref,
                       e3_ref, dexp_ref, on_ref, o_ref, st)


def _ssd_fwd_chunk(rs, c, xs_ref, b_ref, c_ref, dt_ref, z_ref, sb_ref, pc_ref, e3_ref, dexp_ref, on_ref, o_ref, st):
    xs = xs_ref[0, rs, :]
    dt_t = dt_ref[0, rs, :].T
    dtf = _softplus(dt_t[0:N_HEADS, :] + pc_ref[:, 0:1])
    dtb = _softplus(dt_t[N_HEADS:2 * N_HEADS, :] + pc_ref[:, 1:2])
    af = dtf * pc_ref[:, 2:3]
    ab = dtb * pc_ref[:, 3:4]
    csf = _dot3(af, _tri("le").astype(BF16))
    rcs = _dot3(ab, _tri("ge").astype(BF16))
    totf = csf[:, CHUNK - 1:CHUNK]
    rowform = jnp.concatenate([dtf * jnp.exp(totf - csf), jnp.exp(csf), jnp.exp(rcs), csf, rcs,
                               jnp.zeros((CHUNK - 5 * N_HEADS, CHUNK), F32)], axis=0)
    cols = rowform.T
    ex3 = _expand(cols, e3_ref)
    w_state, e_f, e_b = ex3[:, :D_MODEL], ex3[:, D_MODEL:2 * D_MODEL], ex3[:, 2 * D_MODEL:]

    xb = xs.astype(BF16)
    lower, upper = _tri("ge"), _tri("le")
    head_a = lax.broadcasted_iota(I32, (CHUNK, LANE), 1) < HEAD_DIM
    half = D_MODEL // SSD_GROUPS
    hpg = N_HEADS // SSD_GROUPS
    st_all = st[...]
    sb_all = sb_ref[0, c]
    ys = []
    b_t = []
    for g in range(SSD_GROUPS):
        bg = b_ref[0, rs, g * SSD_STATE:(g + 1) * SSD_STATE]
        cg = c_ref[0, rs, g * SSD_STATE:(g + 1) * SSD_STATE].astype(BF16)
        b_t.append(bg.T.astype(BF16))
        gm = _dot_nt(cg, bg.astype(BF16))
        states = jnp.concatenate([st_all[:, g * half:(g + 1) * half].astype(BF16),
                                  sb_all[:, g * half:(g + 1) * half]], axis=1)
        off = _dot(cg, states)
        y_off = (off[:, :half] * e_f[:, g * half:(g + 1) * half]
                 + off[:, half:] * e_b[:, g * half:(g + 1) * half])
        for pair in range(hpg // 2):
            ms = []
            for h in (g * hpg + 2 * pair, g * hpg + 2 * pair + 1):
                dec_f = jnp.where(lower, jnp.exp(cols[:, 3 * N_HEADS + h:3 * N_HEADS + h + 1] - csf[h:h + 1, :]), 0.0)
                dec_b = jnp.where(upper, jnp.exp(cols[:, 4 * N_HEADS + h:4 * N_HEADS + h + 1] - rcs[h:h + 1, :]), 0.0)
                ms.append((gm * (dec_f * dtf[h:h + 1, :] + dec_b * dtb[h:h + 1, :])).astype(BF16))
            lo = g * half + pair * LANE
            xp = xb[:, lo:lo + LANE]
            ys.append(jnp.where(head_a, _dot(ms[0], xp), _dot(ms[1], xp))
                      + y_off[:, pair * LANE:(pair + 1) * LANE])
    y = jnp.concatenate(ys, axis=1) + xs * dexp_ref[...]
    yz = y * _silu(z_ref[0, rs, :])
    o_ref[0, rs, :] = _rms(yz, on_ref[...]).astype(BF16)

    xw = (xs * w_state).astype(BF16)
    upd = [_dot(b_t[g], xw[:, g * half:(g + 1) * half]) for g in range(SSD_GROUPS)]
    st[...] = st_all * e_f[CHUNK - 1:CHUNK, :] + jnp.concatenate(upd, axis=1)


def _ssd(xbc_act, dt, z, pc, dexp, out_norm):
    b, seq, _ = xbc_act.shape
    nc = seq // CHUNK
    sel = np.zeros((2 * CHUNK, 3 * D_MODEL), np.float32)
    for part in range(3):
        for h in range(N_HEADS):
            for rep in range(2):
                sel[rep * CHUNK + part * N_HEADS + h, part * D_MODEL + h * HEAD_DIM:part * D_MODEL + (h + 1) * HEAD_DIM] = 1.0
    e3 = jnp.asarray(sel, BF16)
    e2 = jnp.asarray(sel[:, :2 * D_MODEL], BF16)

    rows = SSD_CPS * CHUNK
    nblk = seq // rows
    bcol = D_MODEL // (2 * SSD_STATE)
    rev = lambda bi, c: (bi, nblk - 1 - c, 0)
    sb = pl.pallas_call(
        _ssd_bwd_kernel, grid=(b, nblk),
        in_specs=[pl.BlockSpec((1, rows, D_MODEL), rev),
                  pl.BlockSpec((1, rows, 2 * SSD_STATE), lambda bi, c: (bi, nblk - 1 - c, bcol)),
                  pl.BlockSpec((1, rows, LANE), rev), _const_spec(pc.shape), _const_spec(e2.shape)],
        out_specs=pl.BlockSpec((1, SSD_CPS, SSD_STATE, D_MODEL), lambda bi, c: (bi, nblk - 1 - c, 0, 0)),
        out_shape=jax.ShapeDtypeStruct((b, nc, SSD_STATE, D_MODEL), BF16),
        scratch_shapes=[pltpu.VMEM((SSD_STATE, D_MODEL), F32)],
        compiler_params=_cparams(2), name="ssd_bwd_state",
    )(xbc_act, xbc_act, dt, pc, e2)

    fwd = lambda bi, c: (bi, c, 0)
    return pl.pallas_call(
        _ssd_fwd_kernel, grid=(b, nblk),
        in_specs=[pl.BlockSpec((1, rows, D_MODEL), fwd),
                  pl.BlockSpec((1, rows, 2 * SSD_STATE), lambda bi, c: (bi, c, bcol)),
                  pl.BlockSpec((1, rows, 2 * SSD_STATE), lambda bi, c: (bi, c, bcol + 1)),
                  pl.BlockSpec((1, rows, LANE), fwd), pl.BlockSpec((1, rows, D_MODEL), fwd),
                  pl.BlockSpec((1, SSD_CPS, SSD_STATE, D_MODEL), lambda bi, c: (bi, c, 0, 0)),
                  _const_spec(pc.shape), _const_spec(e3.shape), _const_spec((1, D_MODEL)), _const_spec((1, D_MODEL))],
        out_specs=pl.BlockSpec((1, rows, D_MODEL), fwd),
        out_shape=jax.ShapeDtypeStruct((b, seq, D_MODEL), BF16),
        scratch_shapes=[pltpu.VMEM((SSD_STATE, D_MODEL), F32)],
        compiler_params=_cparams(2), name="ssd_fwd",
    )(xbc_act, xbc_act, xbc_act, dt, z, sb, pc, e3, dexp, out_norm)


def _norm_and_route(x1, fg_ref, wr_hi_ref, wr_lo_ref, hn_ref, aff_ref):
    hn = _rms(x1, fg_ref[...])
    hi, lo = _split2(hn)
    hn_ref[...] = hi
    logits = _dot(hi, wr_hi_ref[...]) + _dot(lo, wr_hi_ref[...]) + _dot(hi, wr_lo_ref[...])
    lt = logits.T[0:N_EXPERTS, :]
    e = jnp.exp(lt - jnp.max(lt, axis=0, keepdims=True))
    aff_ref[0] = e / jnp.sum(e, axis=0, keepdims=True)


def _out_even_kernel(x_ref, a_ref, y_ref, wa_ref, wy_ref, fg_ref, wr_hi_ref, wr_lo_ref, x1_ref, hn_ref, aff_ref):
    x1 = x_ref[...] + _dot(a_ref[...], wa_ref[...]) + _dot(y_ref[...], wy_ref[...])
    x1_ref[...] = x1
    _norm_and_route(x1, fg_ref, wr_hi_ref, wr_lo_ref, hn_ref, aff_ref)


def _out_odd_kernel(x_ref, gb_ref, cu_ref, cp_ref, cn_ref, cw_ref, wo_ref, fg_ref, wr_hi_ref, wr_lo_ref,
                    x1_ref, hn_ref, aff_ref, *, tiles_per_seq):
    i = pl.program_id(0) % tiles_per_seq
    first, last = i == 0, i == tiles_per_seq - 1
    cur, prev8, next8 = cu_ref[...], cp_ref[...], cn_ref[...]
    conv = sum(_shifted(cur, prev8, next8, j - SHORT_CONV // 2, first, last) * cw_ref[j:j + 1, :]
               for j in range(SHORT_CONV))
    x1 = x_ref[...] + _dot((gb_ref[...] * conv).astype(BF16), wo_ref[...])
    x1_ref[...] = x1
    _norm_and_route(x1, fg_ref, wr_hi_ref, wr_lo_ref, hn_ref, aff_ref)


def _route_outs(n, b, seq, tm):
    row = pl.BlockSpec((tm, D_MODEL), lambda i: (i, 0))
    aff = pl.BlockSpec((1, N_EXPERTS, tm), lambda i: (i // (seq // tm), 0, i % (seq // tm)))
    shapes = [jax.ShapeDtypeStruct((n, D_MODEL), F32), jax.ShapeDtypeStruct((n, D_MODEL), BF16),
              jax.ShapeDtypeStruct((b, N_EXPERTS, seq), F32)]
    return [row, row, aff], shapes


def _out_even(x, attn, y, wa, wy, fg, wr_hi, wr_lo, b, seq):
    n = x.shape[0]
    tm = OUT_ROW_TILE
    row = pl.BlockSpec((tm, D_MODEL), lambda i: (i, 0))
    out_specs, shapes = _route_outs(n, b, seq, tm)
    return pl.pallas_call(
        _out_even_kernel, grid=(n // tm,),
        in_specs=[row, row, row, _const_spec(wa.shape), _const_spec(wy.shape), _const_spec((1, D_MODEL)),
                  _const_spec(wr_hi.shape), _const_spec(wr_lo.shape)],
        out_specs=out_specs, out_shape=shapes, compiler_params=_cparams(1), name="out_even",
    )(x, attn, y, wa, wy, fg, wr_hi, wr_lo)


def _out_odd(x, gb, cu, cw, wo, fg, wr_hi, wr_lo, b, seq):
    n = x.shape[0]
    tm = OUT_ROW_TILE
    row = pl.BlockSpec((tm, D_MODEL), lambda i: (i, 0))
    prev = pl.BlockSpec((8, D_MODEL), lambda i: (jnp.maximum(i * (tm // 8) - 1, 0), 0))
    nxt = pl.BlockSpec((8, D_MODEL), lambda i: (jnp.minimum((i + 1) * (tm // 8), n // 8 - 1), 0))
    out_specs, shapes = _route_outs(n, b, seq, tm)
    return pl.pallas_call(
        functools.partial(_out_odd_kernel, tiles_per_seq=seq // tm), grid=(n // tm,),
        in_specs=[row, row, row, prev, nxt, pl.BlockSpec((8, D_MODEL), lambda i: (0, 0)), _const_spec(wo.shape),
                  _const_spec((1, D_MODEL)), _const_spec(wr_hi.shape), _const_spec(wr_lo.shape)],
        out_specs=out_specs, out_shape=shapes, compiler_params=_cparams(1), name="out_odd",
    )(x, gb, cu, cu, cu, cw, wo, fg, wr_hi, wr_lo)


def _in_odd_kernel(x_ref, g_ref, wb_ref, wc_ref, wu_ref, gb_ref, cu_ref):
    hb = _rms(x_ref[...], g_ref[...]).astype(BF16)
    gb_ref[...] = _dot(hb, wb_ref[...])
    cu_ref[...] = _dot(hb, wc_ref[...]) * _dot(hb, wu_ref[...])


def _in_odd(x, g, wb, wc, wu):
    n = x.shape[0]
    tm = ROW_TILE
    row = pl.BlockSpec((tm, D_MODEL), lambda i: (i, 0))
    return pl.pallas_call(
        _in_odd_kernel, grid=(n // tm,),
        in_specs=[row, _const_spec((1, D_MODEL)), _const_spec(wb.shape), _const_spec(wc.shape), _const_spec(wu.shape)],
        out_specs=[row, row], out_shape=[jax.ShapeDtypeStruct((n, D_MODEL), F32)] * 2,
        compiler_params=_cparams(1), name="in_odd",
    )(x, g, wb, wc, wu)


def _count(mask):
    return jnp.sum(jnp.sum(mask.astype(F32), axis=0, keepdims=True), axis=1, keepdims=True)


def _route_kernel(aff_ref, incl_ref, ones_ref, strict_ref, local_ref, group_ref, first_ref, slot_ref, off_ref, end_ref,
                  *, cap):
    def step(i, thrs):
        bit = jnp.int32(1) << (30 - i)
        out = []
        for e in range(N_EXPERTS):
            cand = thrs[e] | bit
            out.append(jnp.where(_count(pltpu.bitcast(aff_ref[0, e], I32) >= cand) >= cap, cand, thrs[e]))
        return tuple(out)

    thrs = lax.fori_loop(0, 31, step, tuple(jnp.zeros((1, 1), I32) for _ in range(N_EXPERTS)))
    for e in range(N_EXPERTS):
        bits = pltpu.bitcast(aff_ref[0, e], I32)
        thr = thrs[e]
        gt = bits > thr
        eq = (bits == thr).astype(BF16)
        eq_rank = _dot(eq, incl_ref[...]) + _dot(strict_ref[...], _dot(eq, ones_ref[...]).astype(BF16))
        sel = (gt | ((bits == thr) & (eq_rank <= cap - _count(gt)))).astype(BF16)
        within = _dot(sel, incl_ref[...])
        totals = _dot(sel, ones_ref[...]).astype(BF16)
        local = _dot(local_ref[...], totals)
        cnt = _dot(group_ref[...], totals)
        padded = jnp.floor((cnt + (SLOT_ALIGN - 1)) * (1.0 / SLOT_ALIGN)) * SLOT_ALIGN
        start = _dot(first_ref[...], padded.astype(BF16))
        slot_ref[0, e] = jnp.where(sel > 0, (start + local + within).astype(I32) - 1, -1)
        off_ref[0, e:e + 1, :] = start.T[0:1, :].astype(I32)
        end_ref[0, e:e + 1, :] = (start + padded).T[0:1, :].astype(I32)


def _route(aff, cap):
    b, _, seq = aff.shape
    nt = seq // LANE
    tri = np.arange(LANE)
    tt = np.arange(nt)
    grp = tt // SUB
    as_bf16 = lambda m: jnp.asarray(m, BF16)
    incl = as_bf16(tri[:, None] <= tri[None, :])
    strict = as_bf16(tt[None, :] < tt[:, None])
    local = as_bf16((tt[None, :] < tt[:, None]) & (grp[None, :] == grp[:, None]))
    group = as_bf16(grp[None, :] == grp[:, None])
    first = as_bf16((grp[None, :] < grp[:, None]) & (tt[None, :] % SUB == 0))
    ones = jnp.ones((LANE, LANE), BF16)
    tiles = pl.BlockSpec((1, N_EXPERTS, nt, LANE), lambda i: (i, 0, 0, 0))
    rows = pl.BlockSpec((1, N_EXPERTS, nt), lambda i: (i, 0, 0))
    return pl.pallas_call(
        functools.partial(_route_kernel, cap=cap), grid=(b,),
        in_specs=[tiles, _const_spec((LANE, LANE)), _const_spec((LANE, LANE))] + [_const_spec((nt, nt))] * 4,
        out_specs=[tiles, rows, rows],
        out_shape=[jax.ShapeDtypeStruct((b, N_EXPERTS, nt, LANE), I32)] + [jax.ShapeDtypeStruct((b, N_EXPERTS, nt), I32)] * 2,
        compiler_params=_cparams(1), name="route",
    )(aff.reshape(b, N_EXPERTS, nt, LANE), incl, ones, strict, local, group, first)


def _one_hot(slots, base):
    return (slots == lax.broadcasted_iota(I32, (WINDOW, slots.shape[1]), 0) + base).astype(BF16)


def _gather_kernel(off_ref, end_ref, slot_ref, aff_ref, hn_ref, xe_hbm, stage, extra, sem, xsem):
    bi, j = pl.program_id(0), pl.program_id(1)
    n_j = pl.num_programs(1)
    step = bi * n_j + j
    cur = step % 2

    def window_copy(bb, jj, e, buf):
        start = pl.multiple_of(off_ref[bb, e, jj * SUB], SLOT_ALIGN)
        return pltpu.make_async_copy(stage.at[buf, e], xe_hbm.at[bb, e, pl.ds(start, WINDOW), :], sem.at[buf, e])

    tokens = hn_ref[0]
    parts = [p.astype(F32) for p in _split3(aff_ref[0, 0])]
    gates = jnp.concatenate(parts + [jnp.zeros((LANE - 3 * N_EXPERTS, TOK_TILE), F32)], axis=0).astype(BF16)

    def rows_of(p):
        return jnp.concatenate([_dot(p, tokens), _dot_nt(p, gates)], axis=1).astype(BF16)

    p_all = jnp.concatenate([_one_hot(slot_ref[0, 0, e:e + 1, :], off_ref[bi, e, j * SUB])
                             for e in range(N_EXPERTS)], axis=0)
    stage[cur] = rows_of(p_all).reshape(N_EXPERTS, WINDOW, D_MODEL + LANE)

    @pl.when(j > 0)
    def _():
        for e in range(N_EXPERTS):
            window_copy(bi, j - 1, e, 1 - cur).wait()

    for e in range(N_EXPERTS):
        window_copy(bi, j, e, cur).start()

    def overflow(e, c):
        first = off_ref[bi, e, j * SUB]
        n_win = (end_ref[bi, e, j * SUB] - first + WINDOW - 1) // WINDOW
        slots = slot_ref[0, 0, pl.ds(e, 1), :]

        def one(w, c2):
            base = pl.multiple_of(first + w * WINDOW, SLOT_ALIGN)
            extra[...] = rows_of(_one_hot(slots, base))
            cp = pltpu.make_async_copy(extra, xe_hbm.at[bi, e, pl.ds(base, WINDOW), :], xsem.at[0])
            cp.start()
            cp.wait()
            return c2

        return lax.fori_loop(1, n_win, one, c)

    lax.fori_loop(0, N_EXPERTS, overflow, 0)

    @pl.when(j == n_j - 1)
    def _():
        for e in range(N_EXPERTS):
            window_copy(bi, j, e, cur).wait()
        extra[...] = jnp.zeros_like(extra)
        cap_pad = xe_hbm.shape[2]

        def fill(e, c):
            used = end_ref[bi, e, end_ref.shape[2] - 1]
            n_big = (cap_pad - used) // WINDOW
            small0 = used + n_big * WINDOW
            n_small = (cap_pad - small0) // SLOT_ALIGN
            big = lambda i: pltpu.make_async_copy(
                extra, xe_hbm.at[bi, e, pl.ds(pl.multiple_of(used + i * WINDOW, SLOT_ALIGN), WINDOW), :], xsem.at[0])
            small = lambda i: pltpu.make_async_copy(
                extra.at[0:SLOT_ALIGN],
                xe_hbm.at[bi, e, pl.ds(pl.multiple_of(small0 + i * SLOT_ALIGN, SLOT_ALIGN), SLOT_ALIGN), :], xsem.at[0])
            for n, mk in ((n_big, big), (n_small, small)):
                lax.fori_loop(0, n, lambda i, c2, mk=mk: (mk(i).start(), c2)[1], 0)
            for n, mk in ((n_big, big), (n_small, small)):
                lax.fori_loop(0, n, lambda i, c2, mk=mk: (mk(i).wait(), c2)[1], 0)
            return c

        lax.fori_loop(0, N_EXPERTS, fill, 0)


def _gather(off, end, slot_t, aff_t, hn):
    b, seq, _ = hn.shape
    cap_pad = _cap_pad(seq)
    width = D_MODEL + LANE
    per_tile = pl.BlockSpec((1, 1, N_EXPERTS, TOK_TILE), lambda bi, j, *_: (bi, j, 0, 0))
    return pl.pallas_call(
        _gather_kernel,
        grid_spec=pltpu.PrefetchScalarGridSpec(
            num_scalar_prefetch=2, grid=(b, seq // TOK_TILE),
            in_specs=[per_tile, per_tile, pl.BlockSpec((1, TOK_TILE, D_MODEL), lambda bi, j, *_: (bi, j, 0))],
            out_specs=pl.BlockSpec(memory_space=pl.ANY),
            scratch_shapes=[pltpu.VMEM((2, N_EXPERTS, WINDOW, width), BF16), pltpu.VMEM((WINDOW, width), BF16),
                            pltpu.SemaphoreType.DMA((2, N_EXPERTS)), pltpu.SemaphoreType.DMA((1,))]),
        out_shape=jax.ShapeDtypeStruct((b, N_EXPERTS, cap_pad, width), BF16),
        compiler_params=_cparams(2), name="moe_gather",
    )(off, end, slot_t, aff_t, hn)


def _ffn_kernel(end_ref, xe_ref, wg_hbm, wu_hbm, wd_hbm, y_ref, stage_g, stage_u, stage_d, wg, wu, wd, sem, *, layer):
    e, bi, r = pl.program_id(0), pl.program_id(1), pl.program_id(2)
    used = end_ref[bi, e, end_ref.shape[2] - 1]
    pairs = ((wg_hbm, stage_g, wg), (wu_hbm, stage_u, wu), (wd_hbm, stage_d, wd))

    def weight_copies(ee):
        return [pltpu.make_async_copy(src.at[layer, ee], stg, sem.at[k]) for k, (src, stg, _) in enumerate(pairs)]

    @pl.when((bi == 0) & (r == 0))
    def _():
        @pl.when(e == 0)
        def _():
            for cp in weight_copies(e):
                cp.start()

        for cp, (_, stg, dst) in zip(weight_copies(e), pairs):
            cp.wait()
            n_rows = stg.shape[0]

            def cast(i, c, stg=stg, dst=dst):
                rows = pl.ds(pl.multiple_of(i * CAST_ROWS, CAST_ROWS), CAST_ROWS)
                dst[rows, :] = stg[rows, :].astype(BF16)
                return c

            lax.fori_loop(0, n_rows // CAST_ROWS, cast, 0)

        @pl.when(e + 1 < pl.num_programs(0))
        def _():
            for cp in weight_copies(e + 1):
                cp.start()

    @pl.when(r * FFN_ROWS < used)
    def _():
        xe = xe_ref[0, 0, :, 0:D_MODEL]
        hid = (_silu(_dot(xe, wg[...])) * _dot(xe, wu[...])).astype(BF16)
        g = xe_ref[0, 0, :, D_MODEL:].astype(F32)
        lane = lax.broadcasted_iota(I32, g.shape, 1)
        mine = (lane % N_EXPERTS == e) & (lane < 3 * N_EXPERTS)
        gate = jnp.sum(jnp.where(mine, g, 0.0), axis=1, keepdims=True)
        y_ref[0, 0] = (_dot(hid, wd[...]) * gate).astype(BF16)

    @pl.when(r * FFN_ROWS >= used)
    def _():
        y_ref[...] = jnp.zeros_like(y_ref)


def _ffn(end, xe, wg, wu, wd, layer):
    b, ne, cap_pad, width = xe.shape
    rows = lambda w: pl.BlockSpec((1, 1, FFN_ROWS, w), lambda e, bi, r, *_: (bi, e, r, 0))
    hbm = pl.BlockSpec(memory_space=pl.ANY)
    return pl.pallas_call(
        functools.partial(_ffn_kernel, layer=layer),
        grid_spec=pltpu.PrefetchScalarGridSpec(
            num_scalar_prefetch=1, grid=(ne, b, cap_pad // FFN_ROWS),
            in_specs=[rows(width), hbm, hbm, hbm],
            out_specs=rows(D_MODEL),
            scratch_shapes=[pltpu.VMEM((D_MODEL, D_FF), F32), pltpu.VMEM((D_MODEL, D_FF), F32),
                            pltpu.VMEM((D_FF, D_MODEL), F32), pltpu.VMEM((D_MODEL, D_FF), BF16),
                            pltpu.VMEM((D_MODEL, D_FF), BF16), pltpu.VMEM((D_FF, D_MODEL), BF16),
                            pltpu.SemaphoreType.DMA((3,))]),
        out_shape=jax.ShapeDtypeStruct((b, ne, cap_pad, D_MODEL), BF16),
        compiler_params=_cparams(3), name="moe_ffn",
    )(end, xe, wg, wu, wd)


def _combine_kernel(off_ref, end_ref, slot_ref, y_hbm, x1_ref, o_ref, win, extra, sem, xsem):
    bi, j = pl.program_id(0), pl.program_id(1)
    n_j = pl.num_programs(1)
    step = bi * n_j + j
    cur = step % 2

    def window_copy(bb, jj, e, buf):
        start = pl.multiple_of(off_ref[bb, e, jj * SUB], SLOT_ALIGN)
        return pltpu.make_async_copy(y_hbm.at[bb, e, pl.ds(start, WINDOW), :], win.at[buf, e], sem.at[buf, e])

    @pl.when(step == 0)
    def _():
        for e in range(N_EXPERTS):
            window_copy(bi, j, e, cur).start()

    @pl.when(step + 1 < pl.num_programs(0) * n_j)
    def _():
        nxt = step + 1
        for e in range(N_EXPERTS):
            window_copy(nxt // n_j, nxt % n_j, e, 1 - cur).start()

    ps = []
    for e in range(N_EXPERTS):
        window_copy(bi, j, e, cur).wait()
        ps.append(_one_hot(slot_ref[0, 0, e:e + 1, :], off_ref[bi, e, j * SUB]))
    p_all = jnp.concatenate(ps, axis=0)
    y_all = win[cur].reshape(N_EXPERTS * WINDOW, D_MODEL)
    o_ref[0] = x1_ref[0] + _dot_tn(p_all, y_all)

    def overflow(e, c):
        first = off_ref[bi, e, j * SUB]
        n_win = (end_ref[bi, e, j * SUB] - first + WINDOW - 1) // WINDOW
        slots = slot_ref[0, 0, pl.ds(e, 1), :]

        def one(w, c2):
            base = pl.multiple_of(first + w * WINDOW, SLOT_ALIGN)
            cp = pltpu.make_async_copy(y_hbm.at[bi, e, pl.ds(base, WINDOW), :], extra, xsem.at[0])
            cp.start()
            cp.wait()
            o_ref[0] += _dot_tn(_one_hot(slots, base), extra[...])
            return c2

        return lax.fori_loop(1, n_win, one, c)

    lax.fori_loop(0, N_EXPERTS, overflow, 0)


def _combine(off, end, slot_t, y, x1):
    b, seq, _ = x1.shape
    tile = pl.BlockSpec((1, TOK_TILE, D_MODEL), lambda bi, j, *_: (bi, j, 0))
    return pl.pallas_call(
        _combine_kernel,
        grid_spec=pltpu.PrefetchScalarGridSpec(
            num_scalar_prefetch=2, grid=(b, seq // TOK_TILE),
            in_specs=[pl.BlockSpec((1, 1, N_EXPERTS, TOK_TILE), lambda bi, j, *_: (bi, j, 0, 0)),
                      pl.BlockSpec(memory_space=pl.ANY), tile],
            out_specs=tile,
            scratch_shapes=[pltpu.VMEM((2, N_EXPERTS, WINDOW, D_MODEL), BF16), pltpu.VMEM((WINDOW, D_MODEL), BF16),
                            pltpu.SemaphoreType.DMA((2, N_EXPERTS)), pltpu.SemaphoreType.DMA((1,))]),
        out_shape=jax.ShapeDtypeStruct((b, seq, D_MODEL), F32),
        compiler_params=_cparams(2), name="moe_combine",
    )(off, end, slot_t, y, x1)


def _cap_pad(seq):
    cap = CAPACITY_FACTOR * seq // N_EXPERTS
    worst = cap + (seq // TOK_TILE) * (SLOT_ALIGN - 1) + WINDOW
    return -(-worst // FFN_ROWS) * FFN_ROWS


def _moe(x1, hn, aff, wg, wu, wd, layer):
    b, seq, _ = x1.shape
    cap = CAPACITY_FACTOR * seq // N_EXPERTS
    slot, off, end = _route(aff, cap)
    per_tile = lambda a: jnp.swapaxes(a.reshape(b, N_EXPERTS, seq // TOK_TILE, TOK_TILE), 1, 2)
    slot_t = per_tile(slot)
    xe = _gather(off, end, slot_t, per_tile(aff), hn)
    y = _ffn(end, xe, wg, wu, wd, layer)
    return _combine(off, end, slot_t, y, x1)


def _rope_tables(seq):
    inv_freq = ROPE_THETA ** (-jnp.arange(ROPE_HALF, dtype=F32) * 2.0 / (2 * ROPE_HALF))
    ang = jnp.arange(seq, dtype=F32)[:, None] * inv_freq[None, :]
    cos, sin = jnp.cos(ang), jnp.sin(ang)
    z = lambda w: jnp.zeros((seq, w), F32)
    rest = HEAD_DIM - 2 * ROPE_HALF
    rc = jnp.concatenate([cos, cos, jnp.ones((seq, rest), F32)], axis=1)
    r1 = jnp.concatenate([-sin, z(ROPE_HALF + rest)], axis=1)
    r2 = jnp.concatenate([z(ROPE_HALF), sin, z(rest)], axis=1)
    return tuple(jnp.tile(t, (1, LANE // HEAD_DIM)) for t in (rc, r1, r2))


def _router_split(w):
    wp = jnp.pad(w, ((0, 0), (0, LANE - N_EXPERTS)))
    hi = wp.astype(BF16)
    return hi, (wp - hi.astype(F32)).astype(BF16)


def kernel(x, attn_norm, w_in_even, q_norm, k_norm, ssd_conv_w, ssd_conv_b, ssd_a_log_fwd, ssd_a_log_bwd,
           ssd_dt_bias_fwd, ssd_dt_bias_bwd, ssd_d, ssd_out_norm, w_out_even, conv_norm, conv_w_in, conv_w,
           conv_w_out, ffn_norm, router_w, expert_w_gate, expert_w_up, expert_w_down):
    b, seq, _ = x.shape
    n = b * seq
    depth = ffn_norm.shape[0]
    rc, r1, r2 = _rope_tables(seq)
    blk = np.arange(256) // HEAD_DIM
    bd = jnp.asarray((blk[:, None] == blk[None, :]) / HEAD_DIM, BF16)
    row = lambda v: v.reshape(1, -1).astype(F32)

    xf = x.reshape(n, D_MODEL)
    for layer in range(depth):
        i = layer // 2
        wr_hi, wr_lo = _router_split(router_w[layer])
        fg = row(ffn_norm[layer])
        if layer % 2 == 0:
            w = w_in_even[i].astype(BF16)
            o = np.cumsum([0, D_MODEL, D_MODEL, D_MODEL, D_MODEL, SSD_XBC, N_HEADS, N_HEADS])
            wq, wk, wv, wz, wx = (w[:, o[j]:o[j + 1]] for j in range(5))
            wd = jnp.pad(w[:, o[5]:o[7]], ((0, 0), (0, LANE - 2 * N_HEADS)))
            tile_heads = lambda g: row(jnp.tile(g, N_HEADS))
            q, k, v, z, xbc, dt = _in_even(xf, row(attn_norm[i]), wq, wk, wv, wz, wx, wd,
                                           tile_heads(q_norm[i]), tile_heads(k_norm[i]), bd, rc, r1, r2, seq)
            as3 = lambda t: t.reshape(b, seq, -1)
            attn = _attention(as3(q), as3(k), as3(v))
            cw = jnp.pad(ssd_conv_w[i], ((0, 8 - SSD_CONV), (0, 0)))
            act = _ssd_conv(as3(xbc), cw, row(ssd_conv_b[i]))
            pc = jnp.pad(jnp.stack([ssd_dt_bias_fwd[i], ssd_dt_bias_bwd[i], -jnp.exp(ssd_a_log_fwd[i]),
                                    -jnp.exp(ssd_a_log_bwd[i])], axis=1).astype(F32), ((0, 0), (0, LANE - 4)))
            y = _ssd(act, as3(dt), as3(z), pc, row(jnp.repeat(ssd_d[i], HEAD_DIM)), row(ssd_out_norm[i]))
            wo = w_out_even[i].astype(BF16)
            x1, hn, aff = _out_even(xf, attn.reshape(n, D_MODEL), y.reshape(n, D_MODEL), wo[:D_MODEL], wo[D_MODEL:],
                                    fg, wr_hi, wr_lo, b, seq)
        else:
            w = conv_w_in[i].astype(BF16)
            gb, cu = _in_odd(xf, row(conv_norm[i]), w[:, :D_MODEL], w[:, D_MODEL:2 * D_MODEL], w[:, 2 * D_MODEL:])
            cw = jnp.pad(conv_w[i], ((0, 8 - SHORT_CONV), (0, 0)))
            x1, hn, aff = _out_odd(xf, gb, cu, cw, conv_w_out[i].astype(BF16), fg, wr_hi, wr_lo, b, seq)
        xf = _moe(x1.reshape(b, seq, D_MODEL), hn.reshape(b, seq, D_MODEL), aff, expert_w_gate, expert_w_up,
                  expert_w_down, layer).reshape(n, D_MODEL)
    return xf.reshape(b, seq, D_MODEL)
```

```python
import functools
import math

import jax
import jax.numpy as jnp
import numpy as np
from jax import lax
from jax.experimental import pallas as pl
from jax.experimental.pallas import tpu as pltpu

F32, BF16, I32 = jnp.float32, jnp.bfloat16, jnp.int32

D_MODEL = 1024
N_HEADS = 16
HEAD_DIM = 64
ROPE_HALF = 8
ROPE_THETA = 500000.0
PATTERNS = ((128, 1), (512, 4), (2048, 16))
HALF_STEPS = 64
SSD_GROUPS = 2
SSD_STATE = 128
SSD_XBC = 1536
SSD_CONV = 5
CHUNK = 128
SSD_CPS = 8
N_EXPERTS = 16
CAPACITY_FACTOR = 2
D_FF = 2048
SHORT_CONV = 3
EPS = 1e-6

LANE = 128
VMEM_LIMIT = 56 * 1024 * 1024

ROW_TILE = 512
OUT_ROW_TILE = 256
ATTN_SUPER = 2048
ATTN_HALO = 64
ATTN_TQ = 128
ATTN_TK = ATTN_TQ + 2 * HALF_STEPS
ATTN_UNROLL = 16
TOK_TILE = 512
SUB = TOK_TILE // LANE
SLOT_ALIGN = 16
WINDOW = 96
FFN_ROWS = 256
CAST_ROWS = 64


def _cparams(n_axes):
    return pltpu.CompilerParams(dimension_semantics=("arbitrary",) * n_axes, vmem_limit_bytes=VMEM_LIMIT)


def _const_spec(shape):
    nd = len(shape)
    return pl.BlockSpec(shape, lambda *_: (0,) * nd, pipeline_mode=pl.Buffered(1))


def _dot(a, b):
    return jnp.dot(a, b, preferred_element_type=F32)


def _dot_nt(a, b):
    return lax.dot_general(a, b, (((1,), (1,)), ((), ())), preferred_element_type=F32)


def _dot_tn(a, b):
    return lax.dot_general(a, b, (((0,), (0,)), ((), ())), preferred_element_type=F32)


def _split2(x):
    hi = x.astype(BF16)
    lo = (x - hi.astype(F32)).astype(BF16)
    return hi, lo


def _split3(x):
    hi = x.astype(BF16)
    r = x - hi.astype(F32)
    mid = r.astype(BF16)
    lo = (r - mid.astype(F32)).astype(BF16)
    return hi, mid, lo


def _dot3(x, m_bf16):
    hi, mid, lo = _split3(x)
    return _dot(hi, m_bf16) + _dot(mid, m_bf16) + _dot(lo, m_bf16)


def _rms(x, g):
    return x * lax.rsqrt(jnp.mean(x * x, axis=-1, keepdims=True) + EPS) * g


def _silu(x):
    return x * jax.nn.sigmoid(x)


def _in_even_kernel(x_ref, g_ref, wq_ref, wk_ref, wv_ref, wz_ref, wx_ref, wd_ref, qg_ref, kg_ref, bd_ref,
                    rc_ref, r1_ref, r2_ref, q_ref, k_ref, v_ref, z_ref, xbc_ref, dt_ref):
    hb = _rms(x_ref[...], g_ref[...]).astype(BF16)
    bd = bd_ref[...]
    rc, r1, r2 = rc_ref[...], r1_ref[...], r2_ref[...]

    def head_norm_rope(w_ref, gain_ref, out_ref):
        t = _dot(hb, w_ref[...])
        for c in range(D_MODEL // 256):
            tc = t[:, c * 256:(c + 1) * 256]
            sq_hi, sq_lo = _split2(tc * tc)
            ms = _dot(sq_hi, bd) + _dot(sq_lo, bd)
            tn = tc * lax.rsqrt(ms + EPS) * gain_ref[:, c * 256:(c + 1) * 256]
            for hh in range(2):
                u = tn[:, hh * LANE:(hh + 1) * LANE]
                r = u * rc + pltpu.roll(u, LANE - ROPE_HALF, 1) * r1 + pltpu.roll(u, ROPE_HALF, 1) * r2
                out_ref[:, c * 256 + hh * LANE:c * 256 + (hh + 1) * LANE] = r

    head_norm_rope(wq_ref, qg_ref, q_ref)
    head_norm_rope(wk_ref, kg_ref, k_ref)
    v_ref[...] = _dot(hb, wv_ref[...])
    z_ref[...] = _dot(hb, wz_ref[...])
    xbc_ref[...] = _dot(hb, wx_ref[...])
    dt_ref[...] = _dot(hb, wd_ref[...])


def _in_even(x, g, wq, wk, wv, wz, wx, wd, qg, kg, bd, rc, r1, r2, seq):
    n = x.shape[0]
    tm = ROW_TILE
    row = lambda w: pl.BlockSpec((tm, w), lambda i: (i, 0))
    tab = pl.BlockSpec((tm, LANE), lambda i: (i % (seq // tm), 0))
    outs = [jax.ShapeDtypeStruct((n, D_MODEL), F32)] * 4 + [jax.ShapeDtypeStruct((n, SSD_XBC), F32),
                                                           jax.ShapeDtypeStruct((n, LANE), F32)]
    return pl.pallas_call(
        _in_even_kernel, grid=(n // tm,),
        in_specs=[row(D_MODEL), _const_spec((1, D_MODEL)), _const_spec(wq.shape), _const_spec(wk.shape),
                  _const_spec(wv.shape), _const_spec(wz.shape), _const_spec(wx.shape), _const_spec(wd.shape),
                  _const_spec((1, D_MODEL)), _const_spec((1, D_MODEL)), _const_spec((256, 256)), tab, tab, tab],
        out_specs=[row(D_MODEL)] * 4 + [row(SSD_XBC), row(LANE)], out_shape=outs,
        compiler_params=_cparams(1), name="in_even",
    )(x, g, wq, wk, wv, wz, wx, wd, qg, kg, bd, rc, r1, r2)


def _attn_kernel(q_ref, k0, k1, k2, v0, v1, v2, qw_hbm, kw_hbm, vw_hbm, o_ref, qf, kf, vf, q16, k16, v16,
                 acc, mst, lst, a16, m16, l16, an3, mn3, ln3, sem, rsem, *, seq):
    dmax = PATTERNS[-1][1]
    bi, hp, j = pl.program_id(0), pl.program_id(1), pl.program_id(2)
    n_hp, n_j = pl.num_programs(1), pl.num_programs(2)
    p0 = j * ATTN_SUPER
    halo = HALF_STEPS
    step = (bi * n_hp + hp) * n_j + j
    cur = step % 2

    def residue_copies(at, buf, r, where):
        b_, hp_, j_ = at
        lanes = pl.ds(pl.multiple_of(hp_ * LANE, LANE), LANE)
        q0 = pl.multiple_of(j_ * ATTN_TQ, ATTN_TQ)
        n_kv = ATTN_TK - halo if where else ATTN_TK
        src0 = 0 if where < 0 else q0 - halo
        dst0 = halo if where < 0 else 0
        cps = [pltpu.make_async_copy(qw_hbm.at[b_, pl.ds(q0, ATTN_TQ), r, lanes], q16.at[buf, r], rsem.at[buf, 0, r])]
        for n, (src, dst) in enumerate(((kw_hbm, k16), (vw_hbm, v16))):
            cps.append(pltpu.make_async_copy(src.at[b_, pl.ds(src0, n_kv), r, lanes],
                                             dst.at[buf, r, pl.ds(dst0, n_kv), :], rsem.at[buf, 1 + n, r]))
        return cps

    def all_residues(at, buf, action):
        def run(where):
            for r in range(dmax):
                for cp in residue_copies(at, buf, r, where):
                    getattr(cp, action)()
        j_ = at[2]
        pl.when(j_ == 0)(functools.partial(run, -1))
        pl.when((j_ > 0) & (j_ < n_j - 1))(functools.partial(run, 0))
        pl.when(j_ == n_j - 1)(functools.partial(run, 1))

    def state_copies(r):
        return [pltpu.make_async_copy(src.at[r], dst.at[:, r, :], sem.at[n, r])
                for n, (src, dst) in enumerate(((a16, an3), (m16, mn3), (l16, ln3)))]

    @pl.when(step == 0)
    def _():
        k16[...] = jnp.zeros_like(k16)
        v16[...] = jnp.zeros_like(v16)
        all_residues((bi, hp, j), cur, "start")

    @pl.when(step + 1 < pl.num_programs(0) * n_hp * n_j)
    def _():
        nxt = step + 1
        all_residues((nxt // (n_hp * n_j), (nxt // n_j) % n_hp, nxt % n_j), 1 - cur, "start")

    head_a = lax.broadcasted_iota(I32, (ATTN_TQ, LANE), 1) < HEAD_DIM
    row_i = lax.broadcasted_iota(I32, (ATTN_TQ, ATTN_TK), 0)
    col_i = lax.broadcasted_iota(I32, (ATTN_TQ, ATTN_TK), 1)
    band = (col_i - row_i >= 0) & (col_i - row_i <= 2 * HALF_STEPS)
    colpos = lax.broadcasted_iota(I32, (1, ATTN_TK), 1)

    def local_softmax(q, kt, vt, mask, pos):
        ok = mask & (pos >= 0) & (pos < seq)
        valid = jnp.concatenate([ok, ok], axis=0)
        q2 = jnp.concatenate([jnp.where(head_a, q, 0.0), jnp.where(head_a, 0.0, q)], axis=0).astype(BF16)
        s = jnp.where(valid, _dot_nt(q2, kt.astype(BF16)), -jnp.inf)
        m = jnp.max(s, axis=1, keepdims=True)
        p = jnp.exp(s - m)
        l = jnp.sum(p, axis=1, keepdims=True)
        n = _dot(p.astype(BF16), vt.astype(BF16))
        return (jnp.where(head_a, m[:ATTN_TQ], m[ATTN_TQ:]), jnp.where(head_a, l[:ATTN_TQ], l[ATTN_TQ:]),
                jnp.where(head_a, n[:ATTN_TQ], n[ATTN_TQ:]))

    def visit_dense(i):
        qs = pl.multiple_of(i * ATTN_TQ, ATTN_TQ)
        rows = pl.ds(qs, ATTN_TQ)
        keys = pl.ds(qs, ATTN_TK)
        mst[rows, :], lst[rows, :], acc[rows, :] = local_softmax(
            qf[rows, :], kf[keys, :], vf[keys, :], band, p0 + qs - HALF_STEPS + colpos)

    def visit_dmax(r):
        q = q16[cur, r] * (HEAD_DIM ** -0.5)
        m16[r], l16[r], a16[r] = local_softmax(q, k16[cur, r], v16[cur, r], band,
                                               p0 - dmax * HALF_STEPS + r + dmax * colpos)

    dmid = PATTERNS[1][1]
    per = dmax // dmid
    seg_q, seg_k = ATTN_TQ // per, ATTN_TK // per
    q_step = per * (row_i % seg_q) + row_i // seg_q
    k_step = per * (col_i % seg_k - HALF_STEPS // per) + col_i // seg_k
    mask_mid = jnp.abs(k_step - q_step) <= HALF_STEPS
    k_off_mid = dmax * (colpos % seg_k - HALF_STEPS // per) + dmid * (colpos // seg_k)

    def visit_mid(i):
        r = i & (dmid - 1)
        t = i >> (dmid.bit_length() - 1)
        n0 = pl.multiple_of(t * seg_q, seg_q)
        q_rows = pl.ds(n0, seg_q)
        k_rows = pl.ds(pl.multiple_of(n0 + HALF_STEPS - HALF_STEPS // per, 8), seg_k)
        classes = [r + dmid * c for c in range(per)]
        cat = lambda ref, rows, buf=None: jnp.concatenate(
            [ref[c, rows, :] if buf is None else ref[buf, c, rows, :] for c in classes], axis=0)
        m_loc, l_loc, n_loc = local_softmax(cat(q16, q_rows, cur) * (HEAD_DIM ** -0.5), cat(k16, k_rows, cur),
                                            cat(v16, k_rows, cur), mask_mid, p0 + dmax * n0 + r + k_off_mid)
        m_old = cat(m16, q_rows)
        m_new = jnp.maximum(m_old, m_loc)
        w_old = jnp.exp(m_old - m_new)
        w_loc = jnp.exp(m_loc - m_new)
        l_new = cat(l16, q_rows) * w_old + l_loc * w_loc
        a_new = cat(a16, q_rows) * w_old + n_loc * w_loc
        for n, c in enumerate(classes):
            seg = slice(n * seg_q, (n + 1) * seg_q)
            m16[c, q_rows, :], l16[c, q_rows, :], a16[c, q_rows, :] = m_new[seg], l_new[seg], a_new[seg]

    def loop(n, fn):
        def body(i, c):
            for u in range(ATTN_UNROLL):
                fn(i * ATTN_UNROLL + u)
            return c
        lax.fori_loop(0, n // ATTN_UNROLL, body, 0)

    n_visits = ATTN_SUPER // ATTN_TQ
    all_residues((bi, hp, j), cur, "wait")
    loop(dmax, visit_dmax)
    loop(n_visits, visit_mid)
    for r in range(dmax):
        for cp in state_copies(r):
            cp.start()

    qf[...] = q_ref[0] * (HEAD_DIM ** -0.5)
    row0 = 0
    for kr, vr in ((k0, v0), (k1, v1), (k2, v2)):
        n_rows = kr.shape[1]
        kf[row0:row0 + n_rows, :] = kr[0]
        vf[row0:row0 + n_rows, :] = vr[0]
        row0 += n_rows
    loop(n_visits, visit_dense)
    for r in range(dmax):
        for cp in state_copies(r):
            cp.wait()

    def finish(i, c):
        rows = pl.ds(pl.multiple_of(i * ATTN_TQ, ATTN_TQ), ATTN_TQ)
        slabs = pl.ds(pl.multiple_of(i * (ATTN_TQ // dmax), ATTN_TQ // dmax), ATTN_TQ // dmax)
        a_a, m_a, l_a = (t[slabs].reshape(ATTN_TQ, LANE) for t in (an3, mn3, ln3))
        m_b = mst[rows, :]
        m = jnp.maximum(m_a, m_b)
        w_a, w_b = jnp.exp(m_a - m), jnp.exp(m_b - m)
        o = (a_a * w_a + acc[rows, :] * w_b) / (l_a * w_a + lst[rows, :] * w_b)
        o_ref[0, rows, :] = o.astype(BF16)
        return c

    lax.fori_loop(0, n_visits, finish, 0)


def _attention(q, k, v):
    b, seq, _ = q.shape
    nblk = seq // ATTN_HALO
    ratio = ATTN_SUPER // ATTN_HALO
    dmax = PATTERNS[-1][1]
    assert ATTN_SUPER // dmax == ATTN_TQ
    assert len(PATTERNS) == 3 and ATTN_HALO == PATTERNS[0][1] * HALF_STEPS and seq // ATTN_SUPER >= 2

    def halo(after):
        return pl.BlockSpec((1, ATTN_HALO, LANE),
                            lambda bi, hp, j: (bi, jnp.clip(ratio * (j + after) - 1 + after, 0, nblk - 1), hp))

    main = pl.BlockSpec((1, ATTN_SUPER, LANE), lambda bi, hp, j: (bi, j, hp))
    kv_specs = [halo(0), main, halo(1)]
    hbm = pl.BlockSpec(memory_space=pl.ANY)
    by_residue = lambda t: t.reshape(b, seq // dmax, dmax, D_MODEL)
    return pl.pallas_call(
        functools.partial(_attn_kernel, seq=seq),
        grid=(b, D_MODEL // LANE, seq // ATTN_SUPER),
        in_specs=[main] + kv_specs * 2 + [hbm] * 3,
        out_specs=main,
        out_shape=jax.ShapeDtypeStruct((b, seq, D_MODEL), BF16),
        scratch_shapes=[pltpu.VMEM((ATTN_SUPER, LANE), F32), pltpu.VMEM((ATTN_SUPER + 2 * ATTN_HALO, LANE), F32),
                        pltpu.VMEM((ATTN_SUPER + 2 * ATTN_HALO, LANE), F32), pltpu.VMEM((2, dmax, ATTN_TQ, LANE), F32),
                        pltpu.VMEM((2, dmax, ATTN_TK, LANE), F32), pltpu.VMEM((2, dmax, ATTN_TK, LANE), F32)]
                       + [pltpu.VMEM((ATTN_SUPER, LANE), F32)] * 3
                       + [pltpu.VMEM((dmax, ATTN_TQ, LANE), F32)] * 3
                       + [pltpu.VMEM((ATTN_TQ, dmax, LANE), F32)] * 3
                       + [pltpu.SemaphoreType.DMA((3, dmax)), pltpu.SemaphoreType.DMA((2, 3, dmax))],
        compiler_params=_cparams(3), name="dilated_attn",
    )(q, k, k, k, v, v, v, by_residue(q), by_residue(k), by_residue(v))


def _shifted(cur, prev8, next8, s, first, last):
    n = cur.shape[0]
    if s == 0:
        return cur
    rows = lax.broadcasted_iota(I32, cur.shape, 0)
    out = pltpu.roll(cur, (-s) % n, 0)
    if s < 0:
        for j in range(-s):
            src = jnp.where(first, 0.0, prev8[8 + s + j:8 + s + j + 1, :])
            out = jnp.where(rows == j, src, out)
    else:
        for j in range(s):
            src = jnp.where(last, 0.0, next8[j:j + 1, :])
            out = jnp.where(rows == n - s + j, src, out)
    return out


def _ssd_conv_kernel(c_ref, p_ref, n_ref, w_ref, b_ref, o_ref):
    first = pl.program_id(1) == 0
    last = pl.program_id(1) == pl.num_programs(1) - 1
    for c in range(SSD_XBC // 256):
        sl = slice(c * 256, (c + 1) * 256)
        cur, prev8, next8 = c_ref[0, :, sl], p_ref[0, :, sl], n_ref[0, :, sl]
        y = b_ref[:, sl]
        for j in range(SSD_CONV):
            y = y + _shifted(cur, prev8, next8, j - SSD_CONV // 2, first, last) * w_ref[j:j + 1, sl]
        o_ref[0, :, sl] = _silu(y)


def _halo_specs(tm, width, seq):
    cur = pl.BlockSpec((1, tm, width), lambda b, i: (b, i, 0))
    prev = pl.BlockSpec((1, 8, width), lambda b, i: (b, jnp.maximum(i * (tm // 8) - 1, 0), 0))
    nxt = pl.BlockSpec((1, 8, width), lambda b, i: (b, jnp.minimum((i + 1) * (tm // 8), seq // 8 - 1), 0))
    return cur, prev, nxt


def _ssd_conv(xbc, w, bias):
    b, seq, width = xbc.shape
    tm = 512
    cur, prev, nxt = _halo_specs(tm, width, seq)
    return pl.pallas_call(
        _ssd_conv_kernel, grid=(b, seq // tm),
        in_specs=[cur, prev, nxt, pl.BlockSpec((8, width), lambda b, i: (0, 0)),
                  pl.BlockSpec((1, width), lambda b, i: (0, 0))],
        out_specs=cur, out_shape=jax.ShapeDtypeStruct(xbc.shape, F32),
        compiler_params=_cparams(2), name="ssd_conv",
    )(xbc, xbc, xbc, w, bias)


def _softplus(x):
    return jnp.maximum(x, 0.0) + jnp.log1p(jnp.exp(-jnp.abs(x)))


def _tri(kind):
    s = lax.broadcasted_iota(I32, (CHUNK, CHUNK), 0)
    l = lax.broadcasted_iota(I32, (CHUNK, CHUNK), 1)
    return {"le": s <= l, "ge": s >= l, "lt": s < l}[kind]


def _expand(cols, e2_ref):
    hi, lo = _split2(cols)
    return _dot(jnp.concatenate([hi, lo], axis=1), e2_ref[...])


def _ssd_bwd_kernel(xs_ref, b_ref, dt_ref, pc_ref, e2_ref, sb_ref, st):
    @pl.when(pl.program_id(1) == 0)
    def _():
        st[...] = jnp.zeros_like(st)

    for c in reversed(range(SSD_CPS)):
        _ssd_bwd_chunk(slice(c * CHUNK, (c + 1) * CHUNK), c, xs_ref, b_ref, dt_ref, pc_ref, e2_ref, sb_ref, st)


def _ssd_bwd_chunk(rs, c, xs_ref, b_ref, dt_ref, pc_ref, e2_ref, sb_ref, st):
    sb_ref[0, c] = st[...].astype(BF16)
    dt_t = dt_ref[0, rs, :].T
    dtb = _softplus(dt_t[N_HEADS:2 * N_HEADS, :] + pc_ref[:, 1:2])
    a = dtb * pc_ref[:, 3:4]
    ex = _dot3(a, _tri("lt").astype(BF16))
    tot = ex[:, CHUNK - 1:CHUNK] + a[:, CHUNK - 1:CHUNK]
    rowform = jnp.concatenate([dtb * jnp.exp(ex), jnp.broadcast_to(jnp.exp(tot), (N_HEADS, CHUNK)),
                               jnp.zeros((CHUNK - 2 * N_HEADS, CHUNK), F32)], axis=0)
    ex2 = _expand(rowform.T, e2_ref)
    xw = (xs_ref[0, rs, :] * ex2[:, :D_MODEL]).astype(BF16)
    half = D_MODEL // SSD_GROUPS
    upd = [_dot(b_ref[0, rs, g * SSD_STATE:(g + 1) * SSD_STATE].T.astype(BF16), xw[:, g * half:(g + 1) * half])
           for g in range(SSD_GROUPS)]
    st[...] = st[...] * ex2[0:1, D_MODEL:] + jnp.concatenate(upd, axis=1)


def _ssd_fwd_kernel(xs_ref, b_ref, c_ref, dt_ref, z_ref, sb_ref, pc_ref, e3_ref, dexp_ref, on_ref, o_ref, st):
    @pl.when(pl.program_id(1) == 0)
    def _():
        st[...] = jnp.zeros_like(st)

    for c in range(SSD_CPS):
        _ssd_fwd_chunk(slice(c * CHUNK, (c + 1) * CHUNK), c, xs_ref, b_ref, c_ref, dt_ref, z_ref, sb_ref, pc_ref,
                       e3_ref, dexp_ref, on_ref, o_ref, st)


def _ssd_fwd_chunk(rs, c, xs_ref, b_ref, c_ref, dt_ref, z_ref, sb_ref, pc_ref, e3_ref, dexp_ref, on_ref, o_ref, st):
    xs = xs_ref[0, rs, :]
    dt_t = dt_ref[0, rs, :].T
    dtf = _softplus(dt_t[0:N_HEADS, :] + pc_ref[:, 0:1])
    dtb = _softplus(dt_t[N_HEADS:2 * N_HEADS, :] + pc_ref[:, 1:2])
    af = dtf * pc_ref[:, 2:3]
    ab = dtb * pc_ref[:, 3:4]
    csf = _dot3(af, _tri("le").astype(BF16))
    rcs = _dot3(ab, _tri("ge").astype(BF16))
    totf = csf[:, CHUNK - 1:CHUNK]
    rowform = jnp.concatenate([dtf * jnp.exp(totf - csf), jnp.exp(csf), jnp.exp(rcs), csf, rcs,
                               jnp.zeros((CHUNK - 5 * N_HEADS, CHUNK), F32)], axis=0)
    cols = rowform.T
    ex3 = _expand(cols, e3_ref)
    w_state, e_f, e_b = ex3[:, :D_MODEL], ex3[:, D_MODEL:2 * D_MODEL], ex3[:, 2 * D_MODEL:]

    xb = xs.astype(BF16)
    lower, upper = _tri("ge"), _tri("le")
    head_a = lax.broadcasted_iota(I32, (CHUNK, LANE), 1) < HEAD_DIM
    half = D_MODEL // SSD_GROUPS
    hpg = N_HEADS // SSD_GROUPS
    st_all = st[...]
    sb_all = sb_ref[0, c]
    ys = []
    b_t = []
    for g in range(SSD_GROUPS):
        bg = b_ref[0, rs, g * SSD_STATE:(g + 1) * SSD_STATE]
        cg = c_ref[0, rs, g * SSD_STATE:(g + 1) * SSD_STATE].astype(BF16)
        b_t.append(bg.T.astype(BF16))
        gm = _dot_nt(cg, bg.astype(BF16))
        states = jnp.concatenate([st_all[:, g * half:(g + 1) * half].astype(BF16),
                                  sb_all[:, g * half:(g + 1) * half]], axis=1)
        off = _dot(cg, states)
        y_off = (off[:, :half] * e_f[:, g * half:(g + 1) * half]
                 + off[:, half:] * e_b[:, g * half:(g + 1) * half])
        for pair in range(hpg // 2):
            ms = []
            for h in (g * hpg + 2 * pair, g * hpg + 2 * pair + 1):
                dec_f = jnp.where(lower, jnp.exp(cols[:, 3 * N_HEADS + h:3 * N_HEADS + h + 1] - csf[h:h + 1, :]), 0.0)
                dec_b = jnp.where(upper, jnp.exp(cols[:, 4 * N_HEADS + h:4 * N_HEADS + h + 1] - rcs[h:h + 1, :]), 0.0)
                ms.append((gm * (dec_f * dtf[h:h + 1, :] + dec_b * dtb[h:h + 1, :])).astype(BF16))
            lo = g * half + pair * LANE
            xp = xb[:, lo:lo + LANE]
            ys.append(jnp.where(head_a, _dot(ms[0], xp), _dot(ms[1], xp))
                      + y_off[:, pair * LANE:(pair + 1) * LANE])
    y = jnp.concatenate(ys, axis=1) + xs * dexp_ref[...]
    yz = y * _silu(z_ref[0, rs, :])
    o_ref[0, rs, :] = _rms(yz, on_ref[...]).astype(BF16)

    xw = (xs * w_state).astype(BF16)
    upd = [_dot(b_t[g], xw[:, g * half:(g + 1) * half]) for g in range(SSD_GROUPS)]
    st[...] = st_all * e_f[CHUNK - 1:CHUNK, :] + jnp.concatenate(upd, axis=1)


def _ssd(xbc_act, dt, z, pc, dexp, out_norm):
    b, seq, _ = xbc_act.shape
    nc = seq // CHUNK
    sel = np.zeros((2 * CHUNK, 3 * D_MODEL), np.float32)
    for part in range(3):
        for h in range(N_HEADS):
            for rep in range(2):
                sel[rep * CHUNK + part * N_HEADS + h, part * D_MODEL + h * HEAD_DIM:part * D_MODEL + (h + 1) * HEAD_DIM] = 1.0
    e3 = jnp.asarray(sel, BF16)
    e2 = jnp.asarray(sel[:, :2 * D_MODEL], BF16)

    rows = SSD_CPS * CHUNK
    nblk = seq // rows
    bcol = D_MODEL // (2 * SSD_STATE)
    rev = lambda bi, c: (bi, nblk - 1 - c, 0)
    sb = pl.pallas_call(
        _ssd_bwd_kernel, grid=(b, nblk),
        in_specs=[pl.BlockSpec((1, rows, D_MODEL), rev),
                  pl.BlockSpec((1, rows, 2 * SSD_STATE), lambda bi, c: (bi, nblk - 1 - c, bcol)),
                  pl.BlockSpec((1, rows, LANE), rev), _const_spec(pc.shape), _const_spec(e2.shape)],
        out_specs=pl.BlockSpec((1, SSD_CPS, SSD_STATE, D_MODEL), lambda bi, c: (bi, nblk - 1 - c, 0, 0)),
        out_shape=jax.ShapeDtypeStruct((b, nc, SSD_STATE, D_MODEL), BF16),
        scratch_shapes=[pltpu.VMEM((SSD_STATE, D_MODEL), F32)],
        compiler_params=_cparams(2), name="ssd_bwd_state",
    )(xbc_act, xbc_act, dt, pc, e2)

    fwd = lambda bi, c: (bi, c, 0)
    return pl.pallas_call(
        _ssd_fwd_kernel, grid=(b, nblk),
        in_specs=[pl.BlockSpec((1, rows, D_MODEL), fwd),
                  pl.BlockSpec((1, rows, 2 * SSD_STATE), lambda bi, c: (bi, c, bcol)),
                  pl.BlockSpec((1, rows, 2 * SSD_STATE), lambda bi, c: (bi, c, bcol + 1)),
                  pl.BlockSpec((1, rows, LANE), fwd), pl.BlockSpec((1, rows, D_MODEL), fwd),
                  pl.BlockSpec((1, SSD_CPS, SSD_STATE, D_MODEL), lambda bi, c: (bi, c, 0, 0)),
                  _const_spec(pc.shape), _const_spec(e3.shape), _const_spec((1, D_MODEL)), _const_spec((1, D_MODEL))],
        out_specs=pl.BlockSpec((1, rows, D_MODEL), fwd),
        out_shape=jax.ShapeDtypeStruct((b, seq, D_MODEL), BF16),
        scratch_shapes=[pltpu.VMEM((SSD_STATE, D_MODEL), F32)],
        compiler_params=_cparams(2), name="ssd_fwd",
    )(xbc_act, xbc_act, xbc_act, dt, z, sb, pc, e3, dexp, out_norm)


def _norm_and_route(x1, fg_ref, wr_hi_ref, wr_lo_ref, hn_ref, aff_ref):
    hn = _rms(x1, fg_ref[...])
    hi, lo = _split2(hn)
    hn_ref[...] = hi
    logits = _dot(hi, wr_hi_ref[...]) + _dot(lo, wr_hi_ref[...]) + _dot(hi, wr_lo_ref[...])
    lt = logits.T[0:N_EXPERTS, :]
    e = jnp.exp(lt - jnp.max(lt, axis=0, keepdims=True))
    aff_ref[0] = e / jnp.sum(e, axis=0, keepdims=True)


def _out_even_kernel(x_ref, a_ref, y_ref, wa_ref, wy_ref, fg_ref, wr_hi_ref, wr_lo_ref, x1_ref, hn_ref, aff_ref):
    x1 = x_ref[...] + _dot(a_ref[...], wa_ref[...]) + _dot(y_ref[...], wy_ref[...])
    x1_ref[...] = x1
    _norm_and_route(x1, fg_ref, wr_hi_ref, wr_lo_ref, hn_ref, aff_ref)


def _out_odd_kernel(x_ref, gb_ref, cu_ref, cp_ref, cn_ref, cw_ref, wo_ref, fg_ref, wr_hi_ref, wr_lo_ref,
                    x1_ref, hn_ref, aff_ref, *, tiles_per_seq):
    i = pl.program_id(0) % tiles_per_seq
    first, last = i == 0, i == tiles_per_seq - 1
    cur, prev8, next8 = cu_ref[...], cp_ref[...], cn_ref[...]
    conv = sum(_shifted(cur, prev8, next8, j - SHORT_CONV // 2, first, last) * cw_ref[j:j + 1, :]
               for j in range(SHORT_CONV))
    x1 = x_ref[...] + _dot((gb_ref[...] * conv).astype(BF16), wo_ref[...])
    x1_ref[...] = x1
    _norm_and_route(x1, fg_ref, wr_hi_ref, wr_lo_ref, hn_ref, aff_ref)


def _route_outs(n, b, seq, tm):
    row = pl.BlockSpec((tm, D_MODEL), lambda i: (i, 0))
    aff = pl.BlockSpec((1, N_EXPERTS, tm), lambda i: (i // (seq // tm), 0, i % (seq // tm)))
    shapes = [jax.ShapeDtypeStruct((n, D_MODEL), F32), jax.ShapeDtypeStruct((n, D_MODEL), BF16),
              jax.ShapeDtypeStruct((b, N_EXPERTS, seq), F32)]
    return [row, row, aff], shapes


def _out_even(x, attn, y, wa, wy, fg, wr_hi, wr_lo, b, seq):
    n = x.shape[0]
    tm = OUT_ROW_TILE
    row = pl.BlockSpec((tm, D_MODEL), lambda i: (i, 0))
    out_specs, shapes = _route_outs(n, b, seq, tm)
    return pl.pallas_call(
        _out_even_kernel, grid=(n // tm,),
        in_specs=[row, row, row, _const_spec(wa.shape), _const_spec(wy.shape), _const_spec((1, D_MODEL)),
                  _const_spec(wr_hi.shape), _const_spec(wr_lo.shape)],
        out_specs=out_specs, out_shape=shapes, compiler_params=_cparams(1), name="out_even",
    )(x, attn, y, wa, wy, fg, wr_hi, wr_lo)


def _out_odd(x, gb, cu, cw, wo, fg, wr_hi, wr_lo, b, seq):
    n = x.shape[0]
    tm = OUT_ROW_TILE
    row = pl.BlockSpec((tm, D_MODEL), lambda i: (i, 0))
    prev = pl.BlockSpec((8, D_MODEL), lambda i: (jnp.maximum(i * (tm // 8) - 1, 0), 0))
    nxt = pl.BlockSpec((8, D_MODEL), lambda i: (jnp.minimum((i + 1) * (tm // 8), n // 8 - 1), 0))
    out_specs, shapes = _route_outs(n, b, seq, tm)
    return pl.pallas_call(
        functools.partial(_out_odd_kernel, tiles_per_seq=seq // tm), grid=(n // tm,),
        in_specs=[row, row, row, prev, nxt, pl.BlockSpec((8, D_MODEL), lambda i: (0, 0)), _const_spec(wo.shape),
                  _const_spec((1, D_MODEL)), _const_spec(wr_hi.shape), _const_spec(wr_lo.shape)],
        out_specs=out_specs, out_shape=shapes, compiler_params=_cparams(1), name="out_odd",
    )(x, gb, cu, cu, cu, cw, wo, fg, wr_hi, wr_lo)


def _in_odd_kernel(x_ref, g_ref, wb_ref, wc_ref, wu_ref, gb_ref, cu_ref):
    hb = _rms(x_ref[...], g_ref[...]).astype(BF16)
    gb_ref[...] = _dot(hb, wb_ref[...])
    cu_ref[...] = _dot(hb, wc_ref[...]) * _dot(hb, wu_ref[...])


def _in_odd(x, g, wb, wc, wu):
    n = x.shape[0]
    tm = ROW_TILE
    row = pl.BlockSpec((tm, D_MODEL), lambda i: (i, 0))
    return pl.pallas_call(
        _in_odd_kernel, grid=(n // tm,),
        in_specs=[row, _const_spec((1, D_MODEL)), _const_spec(wb.shape), _const_spec(wc.shape), _const_spec(wu.shape)],
        out_specs=[row, row], out_shape=[jax.ShapeDtypeStruct((n, D_MODEL), F32)] * 2,
        compiler_params=_cparams(1), name="in_odd",
    )(x, g, wb, wc, wu)


def _count(mask):
    return jnp.sum(jnp.sum(mask.astype(F32), axis=0, keepdims=True), axis=1, keepdims=True)


def _route_kernel(aff_ref, incl_ref, ones_ref, strict_ref, local_ref, group_ref, first_ref, slot_ref, off_ref, end_ref,
                  *, cap):
    def step(i, thrs):
        bit = jnp.int32(1) << (30 - i)
        out = []
        for e in range(N_EXPERTS):
            cand = thrs[e] | bit
            out.append(jnp.where(_count(pltpu.bitcast(aff_ref[0, e], I32) >= cand) >= cap, cand, thrs[e]))
        return tuple(out)

    thrs = lax.fori_loop(0, 31, step, tuple(jnp.zeros((1, 1), I32) for _ in range(N_EXPERTS)))
    for e in range(N_EXPERTS):
        bits = pltpu.bitcast(aff_ref[0, e], I32)
        thr = thrs[e]
        gt = bits > thr
        eq = (bits == thr).astype(BF16)
        eq_rank = _dot(eq, incl_ref[...]) + _dot(strict_ref[...], _dot(eq, ones_ref[...]).astype(BF16))
        sel = (gt | ((bits == thr) & (eq_rank <= cap - _count(gt)))).astype(BF16)
        within = _dot(sel, incl_ref[...])
        totals = _dot(sel, ones_ref[...]).astype(BF16)
        local = _dot(local_ref[...], totals)
        cnt = _dot(group_ref[...], totals)
        padded = jnp.floor((cnt + (SLOT_ALIGN - 1)) * (1.0 / SLOT_ALIGN)) * SLOT_ALIGN
        start = _dot(first_ref[...], padded.astype(BF16))
        slot_ref[0, e] = jnp.where(sel > 0, (start + local + within).astype(I32) - 1, -1)
        off_ref[0, e:e + 1, :] = start.T[0:1, :].astype(I32)
        end_ref[0, e:e + 1, :] = (start + padded).T[0:1, :].astype(I32)


def _route(aff, cap):
    b, _, seq = aff.shape
    nt = seq // LANE
    tri = np.arange(LANE)
    tt = np.arange(nt)
    grp = tt // SUB
    as_bf16 = lambda m: jnp.asarray(m, BF16)
    incl = as_bf16(tri[:, None] <= tri[None, :])
    strict = as_bf16(tt[None, :] < tt[:, None])
    local = as_bf16((tt[None, :] < tt[:, None]) & (grp[None, :] == grp[:, None]))
    group = as_bf16(grp[None, :] == grp[:, None])
    first = as_bf16((grp[None, :] < grp[:, None]) & (tt[None, :] % SUB == 0))
    ones = jnp.ones((LANE, LANE), BF16)
    tiles = pl.BlockSpec((1, N_EXPERTS, nt, LANE), lambda i: (i, 0, 0, 0))
    rows = pl.BlockSpec((1, N_EXPERTS, nt), lambda i: (i, 0, 0))
    return pl.pallas_call(
        functools.partial(_route_kernel, cap=cap), grid=(b,),
        in_specs=[tiles, _const_spec((LANE, LANE)), _const_spec((LANE, LANE))] + [_const_spec((nt, nt))] * 4,
        out_specs=[tiles, rows, rows],
        out_shape=[jax.ShapeDtypeStruct((b, N_EXPERTS, nt, LANE), I32)] + [jax.ShapeDtypeStruct((b, N_EXPERTS, nt), I32)] * 2,
        compiler_params=_cparams(1), name="route",
    )(aff.reshape(b, N_EXPERTS, nt, LANE), incl, ones, strict, local, group, first)


def _one_hot(slots, base):
    return (slots == lax.broadcasted_iota(I32, (WINDOW, slots.shape[1]), 0) + base).astype(BF16)


def _gather_kernel(off_ref, end_ref, slot_ref, aff_ref, hn_ref, xe_hbm, stage, extra, sem, xsem):
    bi, j = pl.program_id(0), pl.program_id(1)
    n_j = pl.num_programs(1)
    step = bi * n_j + j
    cur = step % 2

    def window_copy(bb, jj, e, buf):
        start = pl.multiple_of(off_ref[bb, e, jj * SUB], SLOT_ALIGN)
        return pltpu.make_async_copy(stage.at[buf, e], xe_hbm.at[bb, e, pl.ds(start, WINDOW), :], sem.at[buf, e])

    tokens = hn_ref[0]
    parts = [p.astype(F32) for p in _split3(aff_ref[0, 0])]
    gates = jnp.concatenate(parts + [jnp.zeros((LANE - 3 * N_EXPERTS, TOK_TILE), F32)], axis=0).astype(BF16)

    def rows_of(p):
        return jnp.concatenate([_dot(p, tokens), _dot_nt(p, gates)], axis=1).astype(BF16)

    p_all = jnp.concatenate([_one_hot(slot_ref[0, 0, e:e + 1, :], off_ref[bi, e, j * SUB])
                             for e in range(N_EXPERTS)], axis=0)
    stage[cur] = rows_of(p_all).reshape(N_EXPERTS, WINDOW, D_MODEL + LANE)

    @pl.when(j > 0)
    def _():
        for e in range(N_EXPERTS):
            window_copy(bi, j - 1, e, 1 - cur).wait()

    for e in range(N_EXPERTS):
        window_copy(bi, j, e, cur).start()

    def overflow(e, c):
        first = off_ref[bi, e, j * SUB]
        n_win = (end_ref[bi, e, j * SUB] - first + WINDOW - 1) // WINDOW
        slots = slot_ref[0, 0, pl.ds(e, 1), :]

        def one(w, c2):
            base = pl.multiple_of(first + w * WINDOW, SLOT_ALIGN)
            extra[...] = rows_of(_one_hot(slots, base))
            cp = pltpu.make_async_copy(extra, xe_hbm.at[bi, e, pl.ds(base, WINDOW), :], xsem.at[0])
            cp.start()
            cp.wait()
            return c2

        return lax.fori_loop(1, n_win, one, c)

    lax.fori_loop(0, N_EXPERTS, overflow, 0)

    @pl.when(j == n_j - 1)
    def _():
        for e in range(N_EXPERTS):
            window_copy(bi, j, e, cur).wait()
        extra[...] = jnp.zeros_like(extra)
        cap_pad = xe_hbm.shape[2]

        def fill(e, c):
            used = end_ref[bi, e, end_ref.shape[2] - 1]
            n_big = (cap_pad - used) // WINDOW
            small0 = used + n_big * WINDOW
            n_small = (cap_pad - small0) // SLOT_ALIGN
            big = lambda i: pltpu.make_async_copy(
                extra, xe_hbm.at[bi, e, pl.ds(pl.multiple_of(used + i * WINDOW, SLOT_ALIGN), WINDOW), :], xsem.at[0])
            small = lambda i: pltpu.make_async_copy(
                extra.at[0:SLOT_ALIGN],
                xe_hbm.at[bi, e, pl.ds(pl.multiple_of(small0 + i * SLOT_ALIGN, SLOT_ALIGN), SLOT_ALIGN), :], xsem.at[0])
            for n, mk in ((n_big, big), (n_small, small)):
                lax.fori_loop(0, n, lambda i, c2, mk=mk: (mk(i).start(), c2)[1], 0)
            for n, mk in ((n_big, big), (n_small, small)):
                lax.fori_loop(0, n, lambda i, c2, mk=mk: (mk(i).wait(), c2)[1], 0)
            return c

        lax.fori_loop(0, N_EXPERTS, fill, 0)


def _gather(off, end, slot_t, aff_t, hn):
    b, seq, _ = hn.shape
    cap_pad = _cap_pad(seq)
    width = D_MODEL + LANE
    per_tile = pl.BlockSpec((1, 1, N_EXPERTS, TOK_TILE), lambda bi, j, *_: (bi, j, 0, 0))
    return pl.pallas_call(
        _gather_kernel,
        grid_spec=pltpu.PrefetchScalarGridSpec(
            num_scalar_prefetch=2, grid=(b, seq // TOK_TILE),
            in_specs=[per_tile, per_tile, pl.BlockSpec((1, TOK_TILE, D_MODEL), lambda bi, j, *_: (bi, j, 0))],
            out_specs=pl.BlockSpec(memory_space=pl.ANY),
            scratch_shapes=[pltpu.VMEM((2, N_EXPERTS, WINDOW, width), BF16), pltpu.VMEM((WINDOW, width), BF16),
                            pltpu.SemaphoreType.DMA((2, N_EXPERTS)), pltpu.SemaphoreType.DMA((1,))]),
        out_shape=jax.ShapeDtypeStruct((b, N_EXPERTS, cap_pad, width), BF16),
        compiler_params=_cparams(2), name="moe_gather",
    )(off, end, slot_t, aff_t, hn)


def _ffn_kernel(end_ref, xe_ref, wg_hbm, wu_hbm, wd_hbm, y_ref, stage_g, stage_u, stage_d, wg, wu, wd, sem, *, layer):
    e, bi, r = pl.program_id(0), pl.program_id(1), pl.program_id(2)
    used = end_ref[bi, e, end_ref.shape[2] - 1]
    pairs = ((wg_hbm, stage_g, wg), (wu_hbm, stage_u, wu), (wd_hbm, stage_d, wd))

    def weight_copies(ee):
        return [pltpu.make_async_copy(src.at[layer, ee], stg, sem.at[k]) for k, (src, stg, _) in enumerate(pairs)]

    @pl.when((bi == 0) & (r == 0))
    def _():
        @pl.when(e == 0)
        def _():
            for cp in weight_copies(e):
                cp.start()

        for cp, (_, stg, dst) in zip(weight_copies(e), pairs):
            cp.wait()
            n_rows = stg.shape[0]

            def cast(i, c, stg=stg, dst=dst):
                rows = pl.ds(pl.multiple_of(i * CAST_ROWS, CAST_ROWS), CAST_ROWS)
                dst[rows, :] = stg[rows, :].astype(BF16)
                return c

            lax.fori_loop(0, n_rows // CAST_ROWS, cast, 0)

        @pl.when(e + 1 < pl.num_programs(0))
        def _():
            for cp in weight_copies(e + 1):
                cp.start()

    @pl.when(r * FFN_ROWS < used)
    def _():
        xe = xe_ref[0, 0, :, 0:D_MODEL]
        hid = (_silu(_dot(xe, wg[...])) * _dot(xe, wu[...])).astype(BF16)
        g = xe_ref[0, 0, :, D_MODEL:].astype(F32)
        lane = lax.broadcasted_iota(I32, g.shape, 1)
        mine = (lane % N_EXPERTS == e) & (lane < 3 * N_EXPERTS)
        gate = jnp.sum(jnp.where(mine, g, 0.0), axis=1, keepdims=True)
        y_ref[0, 0] = (_dot(hid, wd[...]) * gate).astype(BF16)

    @pl.when(r * FFN_ROWS >= used)
    def _():
        y_ref[...] = jnp.zeros_like(y_ref)


def _ffn(end, xe, wg, wu, wd, layer):
    b, ne, cap_pad, width = xe.shape
    rows = lambda w: pl.BlockSpec((1, 1, FFN_ROWS, w), lambda e, bi, r, *_: (bi, e, r, 0))
    hbm = pl.BlockSpec(memory_space=pl.ANY)
    return pl.pallas_call(
        functools.partial(_ffn_kernel, layer=layer),
        grid_spec=pltpu.PrefetchScalarGridSpec(
            num_scalar_prefetch=1, grid=(ne, b, cap_pad // FFN_ROWS),
            in_specs=[rows(width), hbm, hbm, hbm],
            out_specs=rows(D_MODEL),
            scratch_shapes=[pltpu.VMEM((D_MODEL, D_FF), F32), pltpu.VMEM((D_MODEL, D_FF), F32),
                            pltpu.VMEM((D_FF, D_MODEL), F32), pltpu.VMEM((D_MODEL, D_FF), BF16),
                            pltpu.VMEM((D_MODEL, D_FF), BF16), pltpu.VMEM((D_FF, D_MODEL), BF16),
                            pltpu.SemaphoreType.DMA((3,))]),
        out_shape=jax.ShapeDtypeStruct((b, ne, cap_pad, D_MODEL), BF16),
        compiler_params=_cparams(3), name="moe_ffn",
    )(end, xe, wg, wu, wd)


def _combine_kernel(off_ref, end_ref, slot_ref, y_hbm, x1_ref, o_ref, win, extra, sem, xsem):
    bi, j = pl.program_id(0), pl.program_id(1)
    n_j = pl.num_programs(1)
    step = bi * n_j + j
    cur = step % 2

    def window_copy(bb, jj, e, buf):
        start = pl.multiple_of(off_ref[bb, e, jj * SUB], SLOT_ALIGN)
        return pltpu.make_async_copy(y_hbm.at[bb, e, pl.ds(start, WINDOW), :], win.at[buf, e], sem.at[buf, e])

    @pl.when(step == 0)
    def _():
        for e in range(N_EXPERTS):
            window_copy(bi, j, e, cur).start()

    @pl.when(step + 1 < pl.num_programs(0) * n_j)
    def _():
        nxt = step + 1
        for e in range(N_EXPERTS):
            window_copy(nxt // n_j, nxt % n_j, e, 1 - cur).start()

    ps = []
    for e in range(N_EXPERTS):
        window_copy(bi, j, e, cur).wait()
        ps.append(_one_hot(slot_ref[0, 0, e:e + 1, :], off_ref[bi, e, j * SUB]))
    p_all = jnp.concatenate(ps, axis=0)
    y_all = win[cur].reshape(N_EXPERTS * WINDOW, D_MODEL)
    o_ref[0] = x1_ref[0] + _dot_tn(p_all, y_all)

    def overflow(e, c):
        first = off_ref[bi, e, j * SUB]
        n_win = (end_ref[bi, e, j * SUB] - first + WINDOW - 1) // WINDOW
        slots = slot_ref[0, 0, pl.ds(e, 1), :]

        def one(w, c2):
            base = pl.multiple_of(first + w * WINDOW, SLOT_ALIGN)
            cp = pltpu.make_async_copy(y_hbm.at[bi, e, pl.ds(base, WINDOW), :], extra, xsem.at[0])
            cp.start()
            cp.wait()
            o_ref[0] += _dot_tn(_one_hot(slots, base), extra[...])
            return c2

        return lax.fori_loop(1, n_win, one, c)

    lax.fori_loop(0, N_EXPERTS, overflow, 0)


def _combine(off, end, slot_t, y, x1):
    b, seq, _ = x1.shape
    tile = pl.BlockSpec((1, TOK_TILE, D_MODEL), lambda bi, j, *_: (bi, j, 0))
    return pl.pallas_call(
        _combine_kernel,
        grid_spec=pltpu.PrefetchScalarGridSpec(
            num_scalar_prefetch=2, grid=(b, seq // TOK_TILE),
            in_specs=[pl.BlockSpec((1, 1, N_EXPERTS, TOK_TILE), lambda bi, j, *_: (bi, j, 0, 0)),
                      pl.BlockSpec(memory_space=pl.ANY), tile],
            out_specs=tile,
            scratch_shapes=[pltpu.VMEM((2, N_EXPERTS, WINDOW, D_MODEL), BF16), pltpu.VMEM((WINDOW, D_MODEL), BF16),
                            pltpu.SemaphoreType.DMA((2, N_EXPERTS)), pltpu.SemaphoreType.DMA((1,))]),
        out_shape=jax.ShapeDtypeStruct((b, seq, D_MODEL), F32),
        compiler_params=_cparams(2), name="moe_combine",
    )(off, end, slot_t, y, x1)


def _cap_pad(seq):
    cap = CAPACITY_FACTOR * seq // N_EXPERTS
    worst = cap + (seq // TOK_TILE) * (SLOT_ALIGN - 1) + WINDOW
    return -(-worst // FFN_ROWS) * FFN_ROWS


def _moe(x1, hn, aff, wg, wu, wd, layer):
    b, seq, _ = x1.shape
    cap = CAPACITY_FACTOR * seq // N_EXPERTS
    slot, off, end = _route(aff, cap)
    per_tile = lambda a: jnp.swapaxes(a.reshape(b, N_EXPERTS, seq // TOK_TILE, TOK_TILE), 1, 2)
    slot_t = per_tile(slot)
    xe = _gather(off, end, slot_t, per_tile(aff), hn)
    y = _ffn(end, xe, wg, wu, wd, layer)
    return _combine(off, end, slot_t, y, x1)


def _rope_tables(seq):
    inv_freq = ROPE_THETA ** (-jnp.arange(ROPE_HALF, dtype=F32) * 2.0 / (2 * ROPE_HALF))
    ang = jnp.arange(seq, dtype=F32)[:, None] * inv_freq[None, :]
    cos, sin = jnp.cos(ang), jnp.sin(ang)
    z = lambda w: jnp.zeros((seq, w), F32)
    rest = HEAD_DIM - 2 * ROPE_HALF
    rc = jnp.concatenate([cos, cos, jnp.ones((seq, rest), F32)], axis=1)
    r1 = jnp.concatenate([-sin, z(ROPE_HALF + rest)], axis=1)
    r2 = jnp.concatenate([z(ROPE_HALF), sin, z(rest)], axis=1)
    return tuple(jnp.tile(t, (1, LANE // HEAD_DIM)) for t in (rc, r1, r2))


def _router_split(w):
    wp = jnp.pad(w, ((0, 0), (0, LANE - N_EXPERTS)))
    hi = wp.astype(BF16)
    return hi, (wp - hi.astype(F32)).astype(BF16)


def kernel(x, attn_norm, w_in_even, q_norm, k_norm, ssd_conv_w, ssd_conv_b, ssd_a_log_fwd, ssd_a_log_bwd,
           ssd_dt_bias_fwd, ssd_dt_bias_bwd, ssd_d, ssd_out_norm, w_out_even, conv_norm, conv_w_in, conv_w,
           conv_w_out, ffn_norm, router_w, expert_w_gate, expert_w_up, expert_w_down):
    b, seq, _ = x.shape
    n = b * seq
    depth = ffn_norm.shape[0]
    rc, r1, r2 = _rope_tables(seq)
    blk = np.arange(256) // HEAD_DIM
    bd = jnp.asarray((blk[:, None] == blk[None, :]) / HEAD_DIM, BF16)
    row = lambda v: v.reshape(1, -1).astype(F32)

    xf = x.reshape(n, D_MODEL)
    for layer in range(depth):
        i = layer // 2
        wr_hi, wr_lo = _router_split(router_w[layer])
        fg = row(ffn_norm[layer])
        if layer % 2 == 0:
            w = w_in_even[i].astype(BF16)
            o = np.cumsum([0, D_MODEL, D_MODEL, D_MODEL, D_MODEL, SSD_XBC, N_HEADS, N_HEADS])
            wq, wk, wv, wz, wx = (w[:, o[j]:o[j + 1]] for j in range(5))
            wd = jnp.pad(w[:, o[5]:o[7]], ((0, 0), (0, LANE - 2 * N_HEADS)))
            tile_heads = lambda g: row(jnp.tile(g, N_HEADS))
            q, k, v, z, xbc, dt = _in_even(xf, row(attn_norm[i]), wq, wk, wv, wz, wx, wd,
                                           tile_heads(q_norm[i]), tile_heads(k_norm[i]), bd, rc, r1, r2, seq)
            as3 = lambda t: t.reshape(b, seq, -1)
            attn = _attention(as3(q), as3(k), as3(v))
            cw = jnp.pad(ssd_conv_w[i], ((0, 8 - SSD_CONV), (0, 0)))
            act = _ssd_conv(as3(xbc), cw, row(ssd_conv_b[i]))
            pc = jnp.pad(jnp.stack([ssd_dt_bias_fwd[i], ssd_dt_bias_bwd[i], -jnp.exp(ssd_a_log_fwd[i]),
                                    -jnp.exp(ssd_a_log_bwd[i])], axis=1).astype(F32), ((0, 0), (0, LANE - 4)))
            y = _ssd(act, as3(dt), as3(z), pc, row(jnp.repeat(ssd_d[i], HEAD_DIM)), row(ssd_out_norm[i]))
            wo = w_out_even[i].astype(BF16)
            x1, hn, aff = _out_even(xf, attn.reshape(n, D_MODEL), y.reshape(n, D_MODEL), wo[:D_MODEL], wo[D_MODEL:],
                                    fg, wr_hi, wr_lo, b, seq)
        else:
            w = conv_w_in[i].astype(BF16)
            gb, cu = _in_odd(xf, row(conv_norm[i]), w[:, :D_MODEL], w[:, D_MODEL:2 * D_MODEL], w[:, 2 * D_MODEL:])
            cw = jnp.pad(conv_w[i], ((0, 8 - SHORT_CONV), (0, 0)))
            x1, hn, aff = _out_odd(xf, gb, cu, cw, conv_w_out[i].astype(BF16), fg, wr_hi, wr_lo, b, seq)
        xf = _moe(x1.reshape(b, seq, D_MODEL), hn.reshape(b, seq, D_MODEL), aff, expert_w_gate, expert_w_up,
                  expert_w_down, layer).reshape(n, D_MODEL)
    return xf.reshape(b, seq, D_MODEL)
```

```python
import functools
import math

import jax
import jax.numpy as jnp
import numpy as np
from jax import lax
from jax.experimental import pallas as pl
from jax.experimental.pallas import tpu as pltpu

F32, BF16, I32 = jnp.float32, jnp.bfloat16, jnp.int32

D_MODEL = 1024
N_HEADS = 16
HEAD_DIM = 64
ROPE_HALF = 8
ROPE_THETA = 500000.0
PATTERNS = ((128, 1), (512, 4), (2048, 16))
HALF_STEPS = 64
SSD_GROUPS = 2
SSD_STATE = 128
SSD_XBC = 1536
SSD_CONV = 5
CHUNK = 128
SSD_CPS = 8
N_EXPERTS = 16
CAPACITY_FACTOR = 2
D_FF = 2048
SHORT_CONV = 3
EPS = 1e-6

LANE = 128
VMEM_LIMIT = 56 * 1024 * 1024

ROW_TILE = 512
OUT_ROW_TILE = 256
ATTN_SUPER = 2048
ATTN_HALO = 64
ATTN_TQ = 128
ATTN_TK = ATTN_TQ + 2 * HALF_STEPS
ATTN_UNROLL = 16
TOK_TILE = 1024
SUB = TOK_TILE // LANE
SLOT_ALIGN = 16
WINDOW = 192
FFN_ROWS = 256
CAST_ROWS = 64


def _cparams(n_axes):
    return pltpu.CompilerParams(dimension_semantics=("arbitrary",) * n_axes, vmem_limit_bytes=VMEM_LIMIT)


def _const_spec(shape):
    nd = len(shape)
    return pl.BlockSpec(shape, lambda *_: (0,) * nd, pipeline_mode=pl.Buffered(1))


def _dot(a, b):
    return jnp.dot(a, b, preferred_element_type=F32)


def _dot_nt(a, b):
    return lax.dot_general(a, b, (((1,), (1,)), ((), ())), preferred_element_type=F32)


def _dot_tn(a, b):
    return lax.dot_general(a, b, (((0,), (0,)), ((), ())), preferred_element_type=F32)


def _split2(x):
    hi = x.astype(BF16)
    lo = (x - hi.astype(F32)).astype(BF16)
    return hi, lo


def _split3(x):
    hi = x.astype(BF16)
    r = x - hi.astype(F32)
    mid = r.astype(BF16)
    lo = (r - mid.astype(F32)).astype(BF16)
    return hi, mid, lo


def _dot3(x, m_bf16):
    hi, mid, lo = _split3(x)
    return _dot(hi, m_bf16) + _dot(mid, m_bf16) + _dot(lo, m_bf16)


def _rms(x, g):
    return x * lax.rsqrt(jnp.mean(x * x, axis=-1, keepdims=True) + EPS) * g


def _silu(x):
    return x * jax.nn.sigmoid(x)


def _in_even_kernel(x_ref, g_ref, wq_ref, wk_ref, wv_ref, wz_ref, wx_ref, wd_ref, qg_ref, kg_ref, bd_ref,
                    rc_ref, r1_ref, r2_ref, q_ref, k_ref, v_ref, z_ref, xbc_ref, dt_ref):
    hb = _rms(x_ref[...], g_ref[...]).astype(BF16)
    bd = bd_ref[...]
    rc, r1, r2 = rc_ref[...], r1_ref[...], r2_ref[...]

    def head_norm_rope(w_ref, gain_ref, out_ref):
        t = _dot(hb, w_ref[...])
        for c in range(D_MODEL // 256):
            tc = t[:, c * 256:(c + 1) * 256]
            sq_hi, sq_lo = _split2(tc * tc)
            ms = _dot(sq_hi, bd) + _dot(sq_lo, bd)
            tn = tc * lax.rsqrt(ms + EPS) * gain_ref[:, c * 256:(c + 1) * 256]
            for hh in range(2):
                u = tn[:, hh * LANE:(hh + 1) * LANE]
                r = u * rc + pltpu.roll(u, LANE - ROPE_HALF, 1) * r1 + pltpu.roll(u, ROPE_HALF, 1) * r2
                out_ref[:, c * 256 + hh * LANE:c * 256 + (hh + 1) * LANE] = r

    head_norm_rope(wq_ref, qg_ref, q_ref)
    head_norm_rope(wk_ref, kg_ref, k_ref)
    v_ref[...] = _dot(hb, wv_ref[...])
    z_ref[...] = _dot(hb, wz_ref[...])
    xbc_ref[...] = _dot(hb, wx_ref[...])
    dt_ref[...] = _dot(hb, wd_ref[...])


def _in_even(x, g, wq, wk, wv, wz, wx, wd, qg, kg, bd, rc, r1, r2, seq):
    n = x.shape[0]
    tm = ROW_TILE
    row = lambda w: pl.BlockSpec((tm, w), lambda i: (i, 0))
    tab = pl.BlockSpec((tm, LANE), lambda i: (i % (seq // tm), 0))
    outs = [jax.ShapeDtypeStruct((n, D_MODEL), F32)] * 4 + [jax.ShapeDtypeStruct((n, SSD_XBC), F32),
                                                           jax.ShapeDtypeStruct((n, LANE), F32)]
    return pl.pallas_call(
        _in_even_kernel, grid=(n // tm,),
        in_specs=[row(D_MODEL), _const_spec((1, D_MODEL)), _const_spec(wq.shape), _const_spec(wk.shape),
                  _const_spec(wv.shape), _const_spec(wz.shape), _const_spec(wx.shape), _const_spec(wd.shape),
                  _const_spec((1, D_MODEL)), _const_spec((1, D_MODEL)), _const_spec((256, 256)), tab, tab, tab],
        out_specs=[row(D_MODEL)] * 4 + [row(SSD_XBC), row(LANE)], out_shape=outs,
        compiler_params=_cparams(1), name="in_even",
    )(x, g, wq, wk, wv, wz, wx, wd, qg, kg, bd, rc, r1, r2)


def _attn_kernel(q_ref, k0, k1, k2, v0, v1, v2, qw_hbm, kw_hbm, vw_hbm, o_ref, qf, kf, vf, q16, k16, v16,
                 acc, mst, lst, a16, m16, l16, an3, mn3, ln3, sem, rsem, *, seq):
    dmax = PATTERNS[-1][1]
    bi, hp, j = pl.program_id(0), pl.program_id(1), pl.program_id(2)
    n_hp, n_j = pl.num_programs(1), pl.num_programs(2)
    p0 = j * ATTN_SUPER
    halo = HALF_STEPS
    step = (bi * n_hp + hp) * n_j + j
    cur = step % 2

    def residue_copies(at, buf, r, where):
        b_, hp_, j_ = at
        lanes = pl.ds(pl.multiple_of(hp_ * LANE, LANE), LANE)
        q0 = pl.multiple_of(j_ * ATTN_TQ, ATTN_TQ)
        n_kv = ATTN_TK - halo if where else ATTN_TK
        src0 = 0 if where < 0 else q0 - halo
        dst0 = halo if where < 0 else 0
        cps = [pltpu.make_async_copy(qw_hbm.at[b_, pl.ds(q0, ATTN_TQ), r, lanes], q16.at[buf, r], rsem.at[buf, 0, r])]
        for n, (src, dst) in enumerate(((kw_hbm, k16), (vw_hbm, v16))):
            cps.append(pltpu.make_async_copy(src.at[b_, pl.ds(src0, n_kv), r, lanes],
                                             dst.at[buf, r, pl.ds(dst0, n_kv), :], rsem.at[buf, 1 + n, r]))
        return cps

    def all_residues(at, buf, action):
        def run(where):
            for r in range(dmax):
                for cp in residue_copies(at, buf, r, where):
                    getattr(cp, action)()
        j_ = at[2]
        pl.when(j_ == 0)(functools.partial(run, -1))
        pl.when((j_ > 0) & (j_ < n_j - 1))(functools.partial(run, 0))
        pl.when(j_ == n_j - 1)(functools.partial(run, 1))

    def state_copies(r):
        return [pltpu.make_async_copy(src.at[r], dst.at[:, r, :], sem.at[n, r])
                for n, (src, dst) in enumerate(((a16, an3), (m16, mn3), (l16, ln3)))]

    @pl.when(step == 0)
    def _():
        k16[...] = jnp.zeros_like(k16)
        v16[...] = jnp.zeros_like(v16)
        all_residues((bi, hp, j), cur, "start")

    @pl.when(step + 1 < pl.num_programs(0) * n_hp * n_j)
    def _():
        nxt = step + 1
        all_residues((nxt // (n_hp * n_j), (nxt // n_j) % n_hp, nxt % n_j), 1 - cur, "start")

    head_a = lax.broadcasted_iota(I32, (ATTN_TQ, LANE), 1) < HEAD_DIM
    row_i = lax.broadcasted_iota(I32, (ATTN_TQ, ATTN_TK), 0)
    col_i = lax.broadcasted_iota(I32, (ATTN_TQ, ATTN_TK), 1)
    band = (col_i - row_i >= 0) & (col_i - row_i <= 2 * HALF_STEPS)
    colpos = lax.broadcasted_iota(I32, (1, ATTN_TK), 1)

    def local_softmax(q, kt, vt, mask, pos):
        ok = mask & (pos >= 0) & (pos < seq)
        valid = jnp.concatenate([ok, ok], axis=0)
        q2 = jnp.concatenate([jnp.where(head_a, q, 0.0), jnp.where(head_a, 0.0, q)], axis=0).astype(BF16)
        s = jnp.where(valid, _dot_nt(q2, kt.astype(BF16)), -jnp.inf)
        m = jnp.max(s, axis=1, keepdims=True)
        p = jnp.exp(s - m)
        l = jnp.sum(p, axis=1, keepdims=True)
        n = _dot(p.astype(BF16), vt.astype(BF16))
        return (jnp.where(head_a, m[:ATTN_TQ], m[ATTN_TQ:]), jnp.where(head_a, l[:ATTN_TQ], l[ATTN_TQ:]),
                jnp.where(head_a, n[:ATTN_TQ], n[ATTN_TQ:]))

    def visit_dense(i):
        qs = pl.multiple_of(i * ATTN_TQ, ATTN_TQ)
        rows = pl.ds(qs, ATTN_TQ)
        keys = pl.ds(qs, ATTN_TK)
        mst[rows, :], lst[rows, :], acc[rows, :] = local_softmax(
            qf[rows, :], kf[keys, :], vf[keys, :], band, p0 + qs - HALF_STEPS + colpos)

    def visit_dmax(r):
        q = q16[cur, r] * (HEAD_DIM ** -0.5)
        m16[r], l16[r], a16[r] = local_softmax(q, k16[cur, r], v16[cur, r], band,
                                               p0 - dmax * HALF_STEPS + r + dmax * colpos)

    dmid = PATTERNS[1][1]
    per = dmax // dmid
    seg_q, seg_k = ATTN_TQ // per, ATTN_TK // per
    q_step = per * (row_i % seg_q) + row_i // seg_q
    k_step = per * (col_i % seg_k - HALF_STEPS // per) + col_i // seg_k
    mask_mid = jnp.abs(k_step - q_step) <= HALF_STEPS
    k_off_mid = dmax * (colpos % seg_k - HALF_STEPS // per) + dmid * (colpos // seg_k)

    def visit_mid(i):
        r = i & (dmid - 1)
        t = i >> (dmid.bit_length() - 1)
        n0 = pl.multiple_of(t * seg_q, seg_q)
        q_rows = pl.ds(n0, seg_q)
        k_rows = pl.ds(pl.multiple_of(n0 + HALF_STEPS - HALF_STEPS // per, 8), seg_k)
        classes = [r + dmid * c for c in range(per)]
        cat = lambda ref, rows, buf=None: jnp.concatenate(
            [ref[c, rows, :] if buf is None else ref[buf, c, rows, :] for c in classes], axis=0)
        m_loc, l_loc, n_loc = local_softmax(cat(q16, q_rows, cur) * (HEAD_DIM ** -0.5), cat(k16, k_rows, cur),
                                            cat(v16, k_rows, cur), mask_mid, p0 + dmax * n0 + r + k_off_mid)
        m_old = cat(m16, q_rows)
        m_new = jnp.maximum(m_old, m_loc)
        w_old = jnp.exp(m_old - m_new)
        w_loc = jnp.exp(m_loc - m_new)
        l_new = cat(l16, q_rows) * w_old + l_loc * w_loc
        a_new = cat(a16, q_rows) * w_old + n_loc * w_loc
        for n, c in enumerate(classes):
            seg = slice(n * seg_q, (n + 1) * seg_q)
            m16[c, q_rows, :], l16[c, q_rows, :], a16[c, q_rows, :] = m_new[seg], l_new[seg], a_new[seg]

    def loop(n, fn):
        def body(i, c):
            for u in range(ATTN_UNROLL):
                fn(i * ATTN_UNROLL + u)
            return c
        lax.fori_loop(0, n // ATTN_UNROLL, body, 0)

    n_visits = ATTN_SUPER // ATTN_TQ
    all_residues((bi, hp, j), cur, "wait")
    loop(dmax, visit_dmax)
    loop(n_visits, visit_mid)
    for r in range(dmax):
        for cp in state_copies(r):
            cp.start()

    qf[...] = q_ref[0] * (HEAD_DIM ** -0.5)
    row0 = 0
    for kr, vr in ((k0, v0), (k1, v1), (k2, v2)):
        n_rows = kr.shape[1]
        kf[row0:row0 + n_rows, :] = kr[0]
        vf[row0:row0 + n_rows, :] = vr[0]
        row0 += n_rows
    loop(n_visits, visit_dense)
    for r in range(dmax):
        for cp in state_copies(r):
            cp.wait()

    def finish(i, c):
        rows = pl.ds(pl.multiple_of(i * ATTN_TQ, ATTN_TQ), ATTN_TQ)
        slabs = pl.ds(pl.multiple_of(i * (ATTN_TQ // dmax), ATTN_TQ // dmax), ATTN_TQ // dmax)
        a_a, m_a, l_a = (t[slabs].reshape(ATTN_TQ, LANE) for t in (an3, mn3, ln3))
        m_b = mst[rows, :]
        m = jnp.maximum(m_a, m_b)
        w_a, w_b = jnp.exp(m_a - m), jnp.exp(m_b - m)
        o = (a_a * w_a + acc[rows, :] * w_b) / (l_a * w_a + lst[rows, :] * w_b)
        o_ref[0, rows, :] = o.astype(BF16)
        return c

    lax.fori_loop(0, n_visits, finish, 0)


def _attention(q, k, v):
    b, seq, _ = q.shape
    nblk = seq // ATTN_HALO
    ratio = ATTN_SUPER // ATTN_HALO
    dmax = PATTERNS[-1][1]
    assert ATTN_SUPER // dmax == ATTN_TQ
    assert len(PATTERNS) == 3 and ATTN_HALO == PATTERNS[0][1] * HALF_STEPS and seq // ATTN_SUPER >= 2

    def halo(after):
        return pl.BlockSpec((1, ATTN_HALO, LANE),
                            lambda bi, hp, j: (bi, jnp.clip(ratio * (j + after) - 1 + after, 0, nblk - 1), hp))

    main = pl.BlockSpec((1, ATTN_SUPER, LANE), lambda bi, hp, j: (bi, j, hp))
    kv_specs = [halo(0), main, halo(1)]
    hbm = pl.BlockSpec(memory_space=pl.ANY)
    by_residue = lambda t: t.reshape(b, seq // dmax, dmax, D_MODEL)
    return pl.pallas_call(
        functools.partial(_attn_kernel, seq=seq),
        grid=(b, D_MODEL // LANE, seq // ATTN_SUPER),
        in_specs=[main] + kv_specs * 2 + [hbm] * 3,
        out_specs=main,
        out_shape=jax.ShapeDtypeStruct((b, seq, D_MODEL), BF16),
        scratch_shapes=[pltpu.VMEM((ATTN_SUPER, LANE), F32), pltpu.VMEM((ATTN_SUPER + 2 * ATTN_HALO, LANE), F32),
                        pltpu.VMEM((ATTN_SUPER + 2 * ATTN_HALO, LANE), F32), pltpu.VMEM((2, dmax, ATTN_TQ, LANE), F32),
                        pltpu.VMEM((2, dmax, ATTN_TK, LANE), F32), pltpu.VMEM((2, dmax, ATTN_TK, LANE), F32)]
                       + [pltpu.VMEM((ATTN_SUPER, LANE), F32)] * 3
                       + [pltpu.VMEM((dmax, ATTN_TQ, LANE), F32)] * 3
                       + [pltpu.VMEM((ATTN_TQ, dmax, LANE), F32)] * 3
                       + [pltpu.SemaphoreType.DMA((3, dmax)), pltpu.SemaphoreType.DMA((2, 3, dmax))],
        compiler_params=_cparams(3), name="dilated_attn",
    )(q, k, k, k, v, v, v, by_residue(q), by_residue(k), by_residue(v))


def _shifted(cur, prev8, next8, s, first, last):
    n = cur.shape[0]
    if s == 0:
        return cur
    rows = lax.broadcasted_iota(I32, cur.shape, 0)
    out = pltpu.roll(cur, (-s) % n, 0)
    if s < 0:
        for j in range(-s):
            src = jnp.where(first, 0.0, prev8[8 + s + j:8 + s + j + 1, :])
            out = jnp.where(rows == j, src, out)
    else:
        for j in range(s):
            src = jnp.where(last, 0.0, next8[j:j + 1, :])
            out = jnp.where(rows == n - s + j, src, out)
    return out


def _ssd_conv_kernel(c_ref, p_ref, n_ref, w_ref, b_ref, o_ref):
    first = pl.program_id(1) == 0
    last = pl.program_id(1) == pl.num_programs(1) - 1
    for c in range(SSD_XBC // 256):
        sl = slice(c * 256, (c + 1) * 256)
        cur, prev8, next8 = c_ref[0, :, sl], p_ref[0, :, sl], n_ref[0, :, sl]
        y = b_ref[:, sl]
        for j in range(SSD_CONV):
            y = y + _shifted(cur, prev8, next8, j - SSD_CONV // 2, first, last) * w_ref[j:j + 1, sl]
        o_ref[0, :, sl] = _silu(y)


def _halo_specs(tm, width, seq):
    cur = pl.BlockSpec((1, tm, width), lambda b, i: (b, i, 0))
    prev = pl.BlockSpec((1, 8, width), lambda b, i: (b, jnp.maximum(i * (tm // 8) - 1, 0), 0))
    nxt = pl.BlockSpec((1, 8, width), lambda b, i: (b, jnp.minimum((i + 1) * (tm // 8), seq // 8 - 1), 0))
    return cur, prev, nxt


def _ssd_conv(xbc, w, bias):
    b, seq, width = xbc.shape
    tm = 512
    cur, prev, nxt = _halo_specs(tm, width, seq)
    return pl.pallas_call(
        _ssd_conv_kernel, grid=(b, seq // tm),
        in_specs=[cur, prev, nxt, pl.BlockSpec((8, width), lambda b, i: (0, 0)),
                  pl.BlockSpec((1, width), lambda b, i: (0, 0))],
        out_specs=cur, out_shape=jax.ShapeDtypeStruct(xbc.shape, F32),
        compiler_params=_cparams(2), name="ssd_conv",
    )(xbc, xbc, xbc, w, bias)


def _softplus(x):
    return jnp.maximum(x, 0.0) + jnp.log1p(jnp.exp(-jnp.abs(x)))


def _tri(kind):
    s = lax.broadcasted_iota(I32, (CHUNK, CHUNK), 0)
    l = lax.broadcasted_iota(I32, (CHUNK, CHUNK), 1)
    return {"le": s <= l, "ge": s >= l, "lt": s < l}[kind]


def _expand(cols, e2_ref):
    hi, lo = _split2(cols)
    return _dot(jnp.concatenate([hi, lo], axis=1), e2_ref[...])


def _ssd_bwd_kernel(xs_ref, b_ref, dt_ref, pc_ref, e2_ref, sb_ref, st):
    @pl.when(pl.program_id(1) == 0)
    def _():
        st[...] = jnp.zeros_like(st)

    for c in reversed(range(SSD_CPS)):
        _ssd_bwd_chunk(slice(c * CHUNK, (c + 1) * CHUNK), c, xs_ref, b_ref, dt_ref, pc_ref, e2_ref, sb_ref, st)


def _ssd_bwd_chunk(rs, c, xs_ref, b_ref, dt_ref, pc_ref, e2_ref, sb_ref, st):
    sb_ref[0, c] = st[...].astype(BF16)
    dt_t = dt_ref[0, rs, :].T
    dtb = _softplus(dt_t[N_HEADS:2 * N_HEADS, :] + pc_ref[:, 1:2])
    a = dtb * pc_ref[:, 3:4]
    ex = _dot3(a, _tri("lt").astype(BF16))
    tot = ex[:, CHUNK - 1:CHUNK] + a[:, CHUNK - 1:CHUNK]
    rowform = jnp.concatenate([dtb * jnp.exp(ex), jnp.broadcast_to(jnp.exp(tot), (N_HEADS, CHUNK)),
                               jnp.zeros((CHUNK - 2 * N_HEADS, CHUNK), F32)], axis=0)
    ex2 = _expand(rowform.T, e2_ref)
    xw = (xs_ref[0, rs, :] * ex2[:, :D_MODEL]).astype(BF16)
    half = D_MODEL // SSD_GROUPS
    upd = [_dot(b_ref[0, rs, g * SSD_STATE:(g + 1) * SSD_STATE].T.astype(BF16), xw[:, g * half:(g + 1) * half])
           for g in range(SSD_GROUPS)]
    st[...] = st[...] * ex2[0:1, D_MODEL:] + jnp.concatenate(upd, axis=1)


def _ssd_fwd_kernel(xs_ref, b_ref, c_ref, dt_ref, z_ref, sb_ref, pc_ref, e3_ref, dexp_ref, on_ref, o_ref, st):
    @pl.when(pl.program_id(1) == 0)
    def _():
        st[...] = jnp.zeros_like(st)

    for c in range(SSD_CPS):
        _ssd_fwd_chunk(slice(c * CHUNK, (c + 1) * CHUNK), c, xs_ref, b_ref, c_ref, dt_ref, z_ref, sb_ref, pc_ref,
                       e3_ref, dexp_ref, on_ref, o_ref, st)


def _ssd_fwd_chunk(rs, c, xs_ref, b_ref, c_ref, dt_ref, z_ref, sb_ref, pc_ref, e3_ref, dexp_ref, on_ref, o_ref, st):
    xs = xs_ref[0, rs, :]
    dt_t = dt_ref[0, rs, :].T
    dtf = _softplus(dt_t[0:N_HEADS, :] + pc_ref[:, 0:1])
    dtb = _softplus(dt_t[N_HEADS:2 * N_HEADS, :] + pc_ref[:, 1:2])
    af = dtf * pc_ref[:, 2:3]
    ab = dtb * pc_ref[:, 3:4]
    csf = _dot3(af, _tri("le").astype(BF16))
    rcs = _dot3(ab, _tri("ge").astype(BF16))
    totf = csf[:, CHUNK - 1:CHUNK]
    rowform = jnp.concatenate([dtf * jnp.exp(totf - csf), jnp.exp(csf), jnp.exp(rcs), csf, rcs,
                               jnp.zeros((CHUNK - 5 * N_HEADS, CHUNK), F32)], axis=0)
    cols = rowform.T
    ex3 = _expand(cols, e3_ref)
    w_state, e_f, e_b = ex3[:, :D_MODEL], ex3[:, D_MODEL:2 * D_MODEL], ex3[:, 2 * D_MODEL:]

    xb = xs.astype(BF16)
    lower, upper = _tri("ge"), _tri("le")
    head_a = lax.broadcasted_iota(I32, (CHUNK, LANE), 1) < HEAD_DIM
    half = D_MODEL // SSD_GROUPS
    hpg = N_HEADS // SSD_GROUPS
    st_all = st[...]
    sb_all = sb_ref[0, c]
    ys = []
    b_t = []
    for g in range(SSD_GROUPS):
        bg = b_ref[0, rs, g * SSD_STATE:(g + 1) * SSD_STATE]
        cg = c_ref[0, rs, g * SSD_STATE:(g + 1) * SSD_STATE].astype(BF16)
        b_t.append(bg.T.astype(BF16))
        gm = _dot_nt(cg, bg.astype(BF16))
        states = jnp.concatenate([st_all[:, g * half:(g + 1) * half].astype(BF16),
                                  sb_all[:, g * half:(g + 1) * half]], axis=1)
        off = _dot(cg, states)
        y_off = (off[:, :half] * e_f[:, g * half:(g + 1) * half]
                 + off[:, half:] * e_b[:, g * half:(g + 1) * half])
        for pair in range(hpg // 2):
            ms = []
            for h in (g * hpg + 2 * pair, g * hpg + 2 * pair + 1):
                dec_f = jnp.where(lower, jnp.exp(cols[:, 3 * N_HEADS + h:3 * N_HEADS + h + 1] - csf[h:h + 1, :]), 0.0)
                dec_b = jnp.where(upper, jnp.exp(cols[:, 4 * N_HEADS + h:4 * N_HEADS + h + 1] - rcs[h:h + 1, :]), 0.0)
                ms.append((gm * (dec_f * dtf[h:h + 1, :] + dec_b * dtb[h:h + 1, :])).astype(BF16))
            lo = g * half + pair * LANE
            xp = xb[:, lo:lo + LANE]
            ys.append(jnp.where(head_a, _dot(ms[0], xp), _dot(ms[1], xp))
                      + y_off[:, pair * LANE:(pair + 1) * LANE])
    y = jnp.concatenate(ys, axis=1) + xs * dexp_ref[...]
    yz = y * _silu(z_ref[0, rs, :])
    o_ref[0, rs, :] = _rms(yz, on_ref[...]).astype(BF16)

    xw = (xs * w_state).astype(BF16)
    upd = [_dot(b_t[g], xw[:, g * half:(g + 1) * half]) for g in range(SSD_GROUPS)]
    st[...] = st_all * e_f[CHUNK - 1:CHUNK, :] + jnp.concatenate(upd, axis=1)


def _ssd(xbc_act, dt, z, pc, dexp, out_norm):
    b, seq, _ = xbc_act.shape
    nc = seq // CHUNK
    sel = np.zeros((2 * CHUNK, 3 * D_MODEL), np.float32)
    for part in range(3):
        for h in range(N_HEADS):
            for rep in range(2):
                sel[rep * CHUNK + part * N_HEADS + h, part * D_MODEL + h * HEAD_DIM:part * D_MODEL + (h + 1) * HEAD_DIM] = 1.0
    e3 = jnp.asarray(sel, BF16)
    e2 = jnp.asarray(sel[:, :2 * D_MODEL], BF16)

    rows = SSD_CPS * CHUNK
    nblk = seq // rows
    bcol = D_MODEL // (2 * SSD_STATE)
    rev = lambda bi, c: (bi, nblk - 1 - c, 0)
    sb = pl.pallas_call(
        _ssd_bwd_kernel, grid=(b, nblk),
        in_specs=[pl.BlockSpec((1, rows, D_MODEL), rev),
                  pl.BlockSpec((1, rows, 2 * SSD_STATE), lambda bi, c: (bi, nblk - 1 - c, bcol)),
                  pl.BlockSpec((1, rows, LANE), rev), _const_spec(pc.shape), _const_spec(e2.shape)],
        out_specs=pl.BlockSpec((1, SSD_CPS, SSD_STATE, D_MODEL), lambda bi, c: (bi, nblk - 1 - c, 0, 0)),
        out_shape=jax.ShapeDtypeStruct((b, nc, SSD_STATE, D_MODEL), BF16),
        scratch_shapes=[pltpu.VMEM((SSD_STATE, D_MODEL), F32)],
        compiler_params=_cparams(2), name="ssd_bwd_state",
    )(xbc_act, xbc_act, dt, pc, e2)

    fwd = lambda bi, c: (bi, c, 0)
    return pl.pallas_call(
        _ssd_fwd_kernel, grid=(b, nblk),
        in_specs=[pl.BlockSpec((1, rows, D_MODEL), fwd),
                  pl.BlockSpec((1, rows, 2 * SSD_STATE), lambda bi, c: (bi, c, bcol)),
                  pl.BlockSpec((1, rows, 2 * SSD_STATE), lambda bi, c: (bi, c, bcol + 1)),
                  pl.BlockSpec((1, rows, LANE), fwd), pl.BlockSpec((1, rows, D_MODEL), fwd),
                  pl.BlockSpec((1, SSD_CPS, SSD_STATE, D_MODEL), lambda bi, c: (bi, c, 0, 0)),
                  _const_spec(pc.shape), _const_spec(e3.shape), _const_spec((1, D_MODEL)), _const_spec((1, D_MODEL))],
        out_specs=pl.BlockSpec((1, rows, D_MODEL), fwd),
        out_shape=jax.ShapeDtypeStruct((b, seq, D_MODEL), BF16),
        scratch_shapes=[pltpu.VMEM((SSD_STATE, D_MODEL), F32)],
        compiler_params=_cparams(2), name="ssd_fwd",
    )(xbc_act, xbc_act, xbc_act, dt, z, sb, pc, e3, dexp, out_norm)


def _norm_and_route(x1, fg_ref, wr_hi_ref, wr_lo_ref, hn_ref, aff_ref):
    hn = _rms(x1, fg_ref[...])
    hi, lo = _split2(hn)
    hn_ref[...] = hi
    logits = _dot(hi, wr_hi_ref[...]) + _dot(lo, wr_hi_ref[...]) + _dot(hi, wr_lo_ref[...])
    lt = logits.T[0:N_EXPERTS, :]
    e = jnp.exp(lt - jnp.max(lt, axis=0, keepdims=True))
    aff_ref[0] = e / jnp.sum(e, axis=0, keepdims=True)


def _out_even_kernel(x_ref, a_ref, y_ref, wa_ref, wy_ref, fg_ref, wr_hi_ref, wr_lo_ref, x1_ref, hn_ref, aff_ref):
    x1 = x_ref[...] + _dot(a_ref[...], wa_ref[...]) + _dot(y_ref[...], wy_ref[...])
    x1_ref[...] = x1
    _norm_and_route(x1, fg_ref, wr_hi_ref, wr_lo_ref, hn_ref, aff_ref)


def _out_odd_kernel(x_ref, gb_ref, cu_ref, cp_ref, cn_ref, cw_ref, wo_ref, fg_ref, wr_hi_ref, wr_lo_ref,
                    x1_ref, hn_ref, aff_ref, *, tiles_per_seq):
    i = pl.program_id(0) % tiles_per_seq
    first, last = i == 0, i == tiles_per_seq - 1
    cur, prev8, next8 = cu_ref[...], cp_ref[...], cn_ref[...]
    conv = sum(_shifted(cur, prev8, next8, j - SHORT_CONV // 2, first, last) * cw_ref[j:j + 1, :]
               for j in range(SHORT_CONV))
    x1 = x_ref[...] + _dot((gb_ref[...] * conv).astype(BF16), wo_ref[...])
    x1_ref[...] = x1
    _norm_and_route(x1, fg_ref, wr_hi_ref, wr_lo_ref, hn_ref, aff_ref)


def _route_outs(n, b, seq, tm):
    row = pl.BlockSpec((tm, D_MODEL), lambda i: (i, 0))
    aff = pl.BlockSpec((1, N_EXPERTS, tm), lambda i: (i // (seq // tm), 0, i % (seq // tm)))
    shapes = [jax.ShapeDtypeStruct((n, D_MODEL), F32), jax.ShapeDtypeStruct((n, D_MODEL), BF16),
              jax.ShapeDtypeStruct((b, N_EXPERTS, seq), F32)]
    return [row, row, aff], shapes


def _out_even(x, attn, y, wa, wy, fg, wr_hi, wr_lo, b, seq):
    n = x.shape[0]
    tm = OUT_ROW_TILE
    row = pl.BlockSpec((tm, D_MODEL), lambda i: (i, 0))
    out_specs, shapes = _route_outs(n, b, seq, tm)
    return pl.pallas_call(
        _out_even_kernel, grid=(n // tm,),
        in_specs=[row, row, row, _const_spec(wa.shape), _const_spec(wy.shape), _const_spec((1, D_MODEL)),
                  _const_spec(wr_hi.shape), _const_spec(wr_lo.shape)],
        out_specs=out_specs, out_shape=shapes, compiler_params=_cparams(1), name="out_even",
    )(x, attn, y, wa, wy, fg, wr_hi, wr_lo)


def _out_odd(x, gb, cu, cw, wo, fg, wr_hi, wr_lo, b, seq):
    n = x.shape[0]
    tm = OUT_ROW_TILE
    row = pl.BlockSpec((tm, D_MODEL), lambda i: (i, 0))
    prev = pl.BlockSpec((8, D_MODEL), lambda i: (jnp.maximum(i * (tm // 8) - 1, 0), 0))
    nxt = pl.BlockSpec((8, D_MODEL), lambda i: (jnp.minimum((i + 1) * (tm // 8), n // 8 - 1), 0))
    out_specs, shapes = _route_outs(n, b, seq, tm)
    return pl.pallas_call(
        functools.partial(_out_odd_kernel, tiles_per_seq=seq // tm), grid=(n // tm,),
        in_specs=[row, row, row, prev, nxt, pl.BlockSpec((8, D_MODEL), lambda i: (0, 0)), _const_spec(wo.shape),
                  _const_spec((1, D_MODEL)), _const_spec(wr_hi.shape), _const_spec(wr_lo.shape)],
        out_specs=out_specs, out_shape=shapes, compiler_params=_cparams(1), name="out_odd",
    )(x, gb, cu, cu, cu, cw, wo, fg, wr_hi, wr_lo)


def _in_odd_kernel(x_ref, g_ref, wb_ref, wc_ref, wu_ref, gb_ref, cu_ref):
    hb = _rms(x_ref[...], g_ref[...]).astype(BF16)
    gb_ref[...] = _dot(hb, wb_ref[...])
    cu_ref[...] = _dot(hb, wc_ref[...]) * _dot(hb, wu_ref[...])


def _in_odd(x, g, wb, wc, wu):
    n = x.shape[0]
    tm = ROW_TILE
    row = pl.BlockSpec((tm, D_MODEL), lambda i: (i, 0))
    return pl.pallas_call(
        _in_odd_kernel, grid=(n // tm,),
        in_specs=[row, _const_spec((1, D_MODEL)), _const_spec(wb.shape), _const_spec(wc.shape), _const_spec(wu.shape)],
        out_specs=[row, row], out_shape=[jax.ShapeDtypeStruct((n, D_MODEL), F32)] * 2,
        compiler_params=_cparams(1), name="in_odd",
    )(x, g, wb, wc, wu)


def _count(mask):
    return jnp.sum(jnp.sum(mask.astype(F32), axis=0, keepdims=True), axis=1, keepdims=True)


def _route_kernel(aff_ref, incl_ref, ones_ref, strict_ref, local_ref, group_ref, first_ref, slot_ref, off_ref, end_ref,
                  *, cap):
    def step(i, thrs):
        bit = jnp.int32(1) << (30 - i)
        out = []
        for e in range(N_EXPERTS):
            cand = thrs[e] | bit
            out.append(jnp.where(_count(pltpu.bitcast(aff_ref[0, e], I32) >= cand) >= cap, cand, thrs[e]))
        return tuple(out)

    thrs = lax.fori_loop(0, 31, step, tuple(jnp.zeros((1, 1), I32) for _ in range(N_EXPERTS)))
    for e in range(N_EXPERTS):
        bits = pltpu.bitcast(aff_ref[0, e], I32)
        thr = thrs[e]
        gt = bits > thr
        eq = (bits == thr).astype(BF16)
        eq_rank = _dot(eq, incl_ref[...]) + _dot(strict_ref[...], _dot(eq, ones_ref[...]).astype(BF16))
        sel = (gt | ((bits == thr) & (eq_rank <= cap - _count(gt)))).astype(BF16)
        within = _dot(sel, incl_ref[...])
        totals = _dot(sel, ones_ref[...]).astype(BF16)
        local = _dot(local_ref[...], totals)
        cnt = _dot(group_ref[...], totals)
        padded = jnp.floor((cnt + (SLOT_ALIGN - 1)) * (1.0 / SLOT_ALIGN)) * SLOT_ALIGN
        start = _dot(first_ref[...], padded.astype(BF16))
        slot_ref[0, e] = jnp.where(sel > 0, (start + local + within).astype(I32) - 1, -1)
        off_ref[0, e:e + 1, :] = start.T[0:1, :].astype(I32)
        end_ref[0, e:e + 1, :] = (start + padded).T[0:1, :].astype(I32)


def _route(aff, cap):
    b, _, seq = aff.shape
    nt = seq // LANE
    tri = np.arange(LANE)
    tt = np.arange(nt)
    grp = tt // SUB
    as_bf16 = lambda m: jnp.asarray(m, BF16)
    incl = as_bf16(tri[:, None] <= tri[None, :])
    strict = as_bf16(tt[None, :] < tt[:, None])
    local = as_bf16((tt[None, :] < tt[:, None]) & (grp[None, :] == grp[:, None]))
    group = as_bf16(grp[None, :] == grp[:, None])
    first = as_bf16((grp[None, :] < grp[:, None]) & (tt[None, :] % SUB == 0))
    ones = jnp.ones((LANE, LANE), BF16)
    tiles = pl.BlockSpec((1, N_EXPERTS, nt, LANE), lambda i: (i, 0, 0, 0))
    rows = pl.BlockSpec((1, N_EXPERTS, nt), lambda i: (i, 0, 0))
    return pl.pallas_call(
        functools.partial(_route_kernel, cap=cap), grid=(b,),
        in_specs=[tiles, _const_spec((LANE, LANE)), _const_spec((LANE, LANE))] + [_const_spec((nt, nt))] * 4,
        out_specs=[tiles, rows, rows],
        out_shape=[jax.ShapeDtypeStruct((b, N_EXPERTS, nt, LANE), I32)] + [jax.ShapeDtypeStruct((b, N_EXPERTS, nt), I32)] * 2,
        compiler_params=_cparams(1), name="route",
    )(aff.reshape(b, N_EXPERTS, nt, LANE), incl, ones, strict, local, group, first)


def _one_hot(slots, base):
    return (slots == lax.broadcasted_iota(I32, (WINDOW, slots.shape[1]), 0) + base).astype(BF16)


def _gather_kernel(off_ref, end_ref, slot_ref, aff_ref, hn_ref, xe_hbm, stage, extra, sem, xsem):
    bi, j = pl.program_id(0), pl.program_id(1)
    n_j = pl.num_programs(1)
    step = bi * n_j + j
    cur = step % 2

    def window_copy(bb, jj, e, buf):
        start = pl.multiple_of(off_ref[bb, e, jj * SUB], SLOT_ALIGN)
        return pltpu.make_async_copy(stage.at[buf, e], xe_hbm.at[bb, e, pl.ds(start, WINDOW), :], sem.at[buf, e])

    tokens = hn_ref[0]
    parts = [p.astype(F32) for p in _split3(aff_ref[0, 0])]
    gates = jnp.concatenate(parts + [jnp.zeros((LANE - 3 * N_EXPERTS, TOK_TILE), F32)], axis=0).astype(BF16)

    def rows_of(p):
        return jnp.concatenate([_dot(p, tokens), _dot_nt(p, gates)], axis=1).astype(BF16)

    p_all = jnp.concatenate([_one_hot(slot_ref[0, 0, e:e + 1, :], off_ref[bi, e, j * SUB])
                             for e in range(N_EXPERTS)], axis=0)
    stage[cur] = rows_of(p_all).reshape(N_EXPERTS, WINDOW, D_MODEL + LANE)

    @pl.when(j > 0)
    def _():
        for e in range(N_EXPERTS):
            window_copy(bi, j - 1, e, 1 - cur).wait()

    for e in range(N_EXPERTS):
        window_copy(bi, j, e, cur).start()

    def overflow(e, c):
        first = off_ref[bi, e, j * SUB]
        n_win = (end_ref[bi, e, j * SUB] - first + WINDOW - 1) // WINDOW
        slots = slot_ref[0, 0, pl.ds(e, 1), :]

        def one(w, c2):
            base = pl.multiple_of(first + w * WINDOW, SLOT_ALIGN)
            extra[...] = rows_of(_one_hot(slots, base))
            cp = pltpu.make_async_copy(extra, xe_hbm.at[bi, e, pl.ds(base, WINDOW), :], xsem.at[0])
            cp.start()
            cp.wait()
            return c2

        return lax.fori_loop(1, n_win, one, c)

    lax.fori_loop(0, N_EXPERTS, overflow, 0)

    @pl.when(j == n_j - 1)
    def _():
        for e in range(N_EXPERTS):
            window_copy(bi, j, e, cur).wait()
        extra[...] = jnp.zeros_like(extra)
        cap_pad = xe_hbm.shape[2]

        def fill(e, c):
            used = end_ref[bi, e, end_ref.shape[2] - 1]
            n_big = (cap_pad - used) // WINDOW
            small0 = used + n_big * WINDOW
            n_small = (cap_pad - small0) // SLOT_ALIGN
            big = lambda i: pltpu.make_async_copy(
                extra, xe_hbm.at[bi, e, pl.ds(pl.multiple_of(used + i * WINDOW, SLOT_ALIGN), WINDOW), :], xsem.at[0])
            small = lambda i: pltpu.make_async_copy(
                extra.at[0:SLOT_ALIGN],
                xe_hbm.at[bi, e, pl.ds(pl.multiple_of(small0 + i * SLOT_ALIGN, SLOT_ALIGN), SLOT_ALIGN), :], xsem.at[0])
            for n, mk in ((n_big, big), (n_small, small)):
                lax.fori_loop(0, n, lambda i, c2, mk=mk: (mk(i).start(), c2)[1], 0)
            for n, mk in ((n_big, big), (n_small, small)):
                lax.fori_loop(0, n, lambda i, c2, mk=mk: (mk(i).wait(), c2)[1], 0)
            return c

        lax.fori_loop(0, N_EXPERTS, fill, 0)


def _gather(off, end, slot_t, aff_t, hn):
    b, seq, _ = hn.shape
    cap_pad = _cap_pad(seq)
    width = D_MODEL + LANE
    per_tile = pl.BlockSpec((1, 1, N_EXPERTS, TOK_TILE), lambda bi, j, *_: (bi, j, 0, 0))
    return pl.pallas_call(
        _gather_kernel,
        grid_spec=pltpu.PrefetchScalarGridSpec(
            num_scalar_prefetch=2, grid=(b, seq // TOK_TILE),
            in_specs=[per_tile, per_tile, pl.BlockSpec((1, TOK_TILE, D_MODEL), lambda bi, j, *_: (bi, j, 0))],
            out_specs=pl.BlockSpec(memory_space=pl.ANY),
            scratch_shapes=[pltpu.VMEM((2, N_EXPERTS, WINDOW, width), BF16), pltpu.VMEM((WINDOW, width), BF16),
                            pltpu.SemaphoreType.DMA((2, N_EXPERTS)), pltpu.SemaphoreType.DMA((1,))]),
        out_shape=jax.ShapeDtypeStruct((b, N_EXPERTS, cap_pad, width), BF16),
        compiler_params=_cparams(2), name="moe_gather",
    )(off, end, slot_t, aff_t, hn)


def _ffn_kernel(end_ref, xe_ref, wg_hbm, wu_hbm, wd_hbm, y_ref, stage_g, stage_u, stage_d, wg, wu, wd, sem, *, layer):
    e, bi, r = pl.program_id(0), pl.program_id(1), pl.program_id(2)
    used = end_ref[bi, e, end_ref.shape[2] - 1]
    pairs = ((wg_hbm, stage_g, wg), (wu_hbm, stage_u, wu), (wd_hbm, stage_d, wd))

    def weight_copies(ee):
        return [pltpu.make_async_copy(src.at[layer, ee], stg, sem.at[k]) for k, (src, stg, _) in enumerate(pairs)]

    @pl.when((bi == 0) & (r == 0))
    def _():
        @pl.when(e == 0)
        def _():
            for cp in weight_copies(e):
                cp.start()

        for cp, (_, stg, dst) in zip(weight_copies(e), pairs):
            cp.wait()
            n_rows = stg.shape[0]

            def cast(i, c, stg=stg, dst=dst):
                rows = pl.ds(pl.multiple_of(i * CAST_ROWS, CAST_ROWS), CAST_ROWS)
                dst[rows, :] = stg[rows, :].astype(BF16)
                return c

            lax.fori_loop(0, n_rows // CAST_ROWS, cast, 0)

        @pl.when(e + 1 < pl.num_programs(0))
        def _():
            for cp in weight_copies(e + 1):
                cp.start()

    @pl.when(r * FFN_ROWS < used)
    def _():
        xe = xe_ref[0, 0, :, 0:D_MODEL]
        hid = (_silu(_dot(xe, wg[...])) * _dot(xe, wu[...])).astype(BF16)
        g = xe_ref[0, 0, :, D_MODEL:].astype(F32)
        lane = lax.broadcasted_iota(I32, g.shape, 1)
        mine = (lane % N_EXPERTS == e) & (lane < 3 * N_EXPERTS)
        gate = jnp.sum(jnp.where(mine, g, 0.0), axis=1, keepdims=True)
        y_ref[0, 0] = (_dot(hid, wd[...]) * gate).astype(BF16)

    @pl.when(r * FFN_ROWS >= used)
    def _():
        y_ref[...] = jnp.zeros_like(y_ref)


def _ffn(end, xe, wg, wu, wd, layer):
    b, ne, cap_pad, width = xe.shape
    rows = lambda w: pl.BlockSpec((1, 1, FFN_ROWS, w), lambda e, bi, r, *_: (bi, e, r, 0))
    hbm = pl.BlockSpec(memory_space=pl.ANY)
    return pl.pallas_call(
        functools.partial(_ffn_kernel, layer=layer),
        grid_spec=pltpu.PrefetchScalarGridSpec(
            num_scalar_prefetch=1, grid=(ne, b, cap_pad // FFN_ROWS),
            in_specs=[rows(width), hbm, hbm, hbm],
            out_specs=rows(D_MODEL),
            scratch_shapes=[pltpu.VMEM((D_MODEL, D_FF), F32), pltpu.VMEM((D_MODEL, D_FF), F32),
                            pltpu.VMEM((D_FF, D_MODEL), F32), pltpu.VMEM((D_MODEL, D_FF), BF16),
                            pltpu.VMEM((D_MODEL, D_FF), BF16), pltpu.VMEM((D_FF, D_MODEL), BF16),
                            pltpu.SemaphoreType.DMA((3,))]),
        out_shape=jax.ShapeDtypeStruct((b, ne, cap_pad, D_MODEL), BF16),
        compiler_params=_cparams(3), name="moe_ffn",
    )(end, xe, wg, wu, wd)


def _combine_kernel(off_ref, end_ref, slot_ref, y_hbm, x1_ref, o_ref, win, extra, sem, xsem):
    bi, j = pl.program_id(0), pl.program_id(1)
    n_j = pl.num_programs(1)
    step = bi * n_j + j
    cur = step % 2

    def window_copy(bb, jj, e, buf):
        start = pl.multiple_of(off_ref[bb, e, jj * SUB], SLOT_ALIGN)
        return pltpu.make_async_copy(y_hbm.at[bb, e, pl.ds(start, WINDOW), :], win.at[buf, e], sem.at[buf, e])

    @pl.when(step == 0)
    def _():
        for e in range(N_EXPERTS):
            window_copy(bi, j, e, cur).start()

    @pl.when(step + 1 < pl.num_programs(0) * n_j)
    def _():
        nxt = step + 1
        for e in range(N_EXPERTS):
            window_copy(nxt // n_j, nxt % n_j, e, 1 - cur).start()

    ps = []
    for e in range(N_EXPERTS):
        window_copy(bi, j, e, cur).wait()
        ps.append(_one_hot(slot_ref[0, 0, e:e + 1, :], off_ref[bi, e, j * SUB]))
    p_all = jnp.concatenate(ps, axis=0)
    y_all = win[cur].reshape(N_EXPERTS * WINDOW, D_MODEL)
    o_ref[0] = x1_ref[0] + _dot_tn(p_all, y_all)

    def overflow(e, c):
        first = off_ref[bi, e, j * SUB]
        n_win = (end_ref[bi, e, j * SUB] - first + WINDOW - 1) // WINDOW
        slots = slot_ref[0, 0, pl.ds(e, 1), :]

        def one(w, c2):
            base = pl.multiple_of(first + w * WINDOW, SLOT_ALIGN)
            cp = pltpu.make_async_copy(y_hbm.at[bi, e, pl.ds(base, WINDOW), :], extra, xsem.at[0])
            cp.start()
            cp.wait()
            o_ref[0] += _dot_tn(_one_hot(slots, base), extra[...])
            return c2

        return lax.fori_loop(1, n_win, one, c)

    lax.fori_loop(0, N_EXPERTS, overflow, 0)


def _combine(off, end, slot_t, y, x1):
    b, seq, _ = x1.shape
    tile = pl.BlockSpec((1, TOK_TILE, D_MODEL), lambda bi, j, *_: (bi, j, 0))
    return pl.pallas_call(
        _combine_kernel,
        grid_spec=pltpu.PrefetchScalarGridSpec(
            num_scalar_prefetch=2, grid=(b, seq // TOK_TILE),
            in_specs=[pl.BlockSpec((1, 1, N_EXPERTS, TOK_TILE), lambda bi, j, *_: (bi, j, 0, 0)),
                      pl.BlockSpec(memory_space=pl.ANY), tile],
            out_specs=tile,
            scratch_shapes=[pltpu.VMEM((2, N_EXPERTS, WINDOW, D_MODEL), BF16), pltpu.VMEM((WINDOW, D_MODEL), BF16),
                            pltpu.SemaphoreType.DMA((2, N_EXPERTS)), pltpu.SemaphoreType.DMA((1,))]),
        out_shape=jax.ShapeDtypeStruct((b, seq, D_MODEL), F32),
        compiler_params=_cparams(2), name="moe_combine",
    )(off, end, slot_t, y, x1)


def _cap_pad(seq):
    cap = CAPACITY_FACTOR * seq // N_EXPERTS
    worst = cap + (seq // TOK_TILE) * (SLOT_ALIGN - 1) + WINDOW
    return -(-worst // FFN_ROWS) * FFN_ROWS


def _moe(x1, hn, aff, wg, wu, wd, layer):
    b, seq, _ = x1.shape
    cap = CAPACITY_FACTOR * seq // N_EXPERTS
    slot, off, end = _route(aff, cap)
    per_tile = lambda a: jnp.swapaxes(a.reshape(b, N_EXPERTS, seq // TOK_TILE, TOK_TILE), 1, 2)
    slot_t = per_tile(slot)
    xe = _gather(off, end, slot_t, per_tile(aff), hn)
    y = _ffn(end, xe, wg, wu, wd, layer)
    return _combine(off, end, slot_t, y, x1)


def _rope_tables(seq):
    inv_freq = ROPE_THETA ** (-jnp.arange(ROPE_HALF, dtype=F32) * 2.0 / (2 * ROPE_HALF))
    ang = jnp.arange(seq, dtype=F32)[:, None] * inv_freq[None, :]
    cos, sin = jnp.cos(ang), jnp.sin(ang)
    z = lambda w: jnp.zeros((seq, w), F32)
    rest = HEAD_DIM - 2 * ROPE_HALF
    rc = jnp.concatenate([cos, cos, jnp.ones((seq, rest), F32)], axis=1)
    r1 = jnp.concatenate([-sin, z(ROPE_HALF + rest)], axis=1)
    r2 = jnp.concatenate([z(ROPE_HALF), sin, z(rest)], axis=1)
    return tuple(jnp.tile(t, (1, LANE // HEAD_DIM)) for t in (rc, r1, r2))


def _router_split(w):
    wp = jnp.pad(w, ((0, 0), (0, LANE - N_EXPERTS)))
    hi = wp.astype(BF16)
    return hi, (wp - hi.astype(F32)).astype(BF16)


def kernel(x, attn_norm, w_in_even, q_norm, k_norm, ssd_conv_w, ssd_conv_b, ssd_a_log_fwd, ssd_a_log_bwd,
           ssd_dt_bias_fwd, ssd_dt_bias_bwd, ssd_d, ssd_out_norm, w_out_even, conv_norm, conv_w_in, conv_w,
           conv_w_out, ffn_norm, router_w, expert_w_gate, expert_w_up, expert_w_down):
    b, seq, _ = x.shape
    n = b * seq
    depth = ffn_norm.shape[0]
    rc, r1, r2 = _rope_tables(seq)
    blk = np.arange(256) // HEAD_DIM
    bd = jnp.asarray((blk[:, None] == blk[None, :]) / HEAD_DIM, BF16)
    row = lambda v: v.reshape(1, -1).astype(F32)

    xf = x.reshape(n, D_MODEL)
    for layer in range(depth):
        i = layer // 2
        wr_hi, wr_lo = _router_split(router_w[layer])
        fg = row(ffn_norm[layer])
        if layer % 2 == 0:
            w = w_in_even[i].astype(BF16)
            o = np.cumsum([0, D_MODEL, D_MODEL, D_MODEL, D_MODEL, SSD_XBC, N_HEADS, N_HEADS])
            wq, wk, wv, wz, wx = (w[:, o[j]:o[j + 1]] for j in range(5))
            wd = jnp.pad(w[:, o[5]:o[7]], ((0, 0), (0, LANE - 2 * N_HEADS)))
            tile_heads = lambda g: row(jnp.tile(g, N_HEADS))
            q, k, v, z, xbc, dt = _in_even(xf, row(attn_norm[i]), wq, wk, wv, wz, wx, wd,
                                           tile_heads(q_norm[i]), tile_heads(k_norm[i]), bd, rc, r1, r2, seq)
            as3 = lambda t: t.reshape(b, seq, -1)
            attn = _attention(as3(q), as3(k), as3(v))
            cw = jnp.pad(ssd_conv_w[i], ((0, 8 - SSD_CONV), (0, 0)))
            act = _ssd_conv(as3(xbc), cw, row(ssd_conv_b[i]))
            pc = jnp.pad(jnp.stack([ssd_dt_bias_fwd[i], ssd_dt_bias_bwd[i], -jnp.exp(ssd_a_log_fwd[i]),
                                    -jnp.exp(ssd_a_log_bwd[i])], axis=1).astype(F32), ((0, 0), (0, LANE - 4)))
            y = _ssd(act, as3(dt), as3(z), pc, row(jnp.repeat(ssd_d[i], HEAD_DIM)), row(ssd_out_norm[i]))
            wo = w_out_even[i].astype(BF16)
            x1, hn, aff = _out_even(xf, attn.reshape(n, D_MODEL), y.reshape(n, D_MODEL), wo[:D_MODEL], wo[D_MODEL:],
                                    fg, wr_hi, wr_lo, b, seq)
        else:
            w = conv_w_in[i].astype(BF16)
            gb, cu = _in_odd(xf, row(conv_norm[i]), w[:, :D_MODEL], w[:, D_MODEL:2 * D_MODEL], w[:, 2 * D_MODEL:])
            cw = jnp.pad(conv_w[i], ((0, 8 - SHORT_CONV), (0, 0)))
            x1, hn, aff = _out_odd(xf, gb, cu, cw, conv_w_out[i].astype(BF16), fg, wr_hi, wr_lo, b, seq)
        xf = _moe(x1.reshape(b, seq, D_MODEL), hn.reshape(b, seq, D_MODEL), aff, expert_w_gate, expert_w_up,
                  expert_w_down, layer).reshape(n, D_MODEL)
    return xf.reshape(b, seq, D_MODEL)
```

```python
import functools
import math

import jax
import jax.numpy as jnp
import numpy as np
from jax import lax
from jax.experimental import pallas as pl
from jax.experimental.pallas import tpu as pltpu

F32, BF16, I32 = jnp.float32, jnp.bfloat16, jnp.int32

D_MODEL = 1024
N_HEADS = 16
HEAD_DIM = 64
ROPE_HALF = 8
ROPE_THETA = 500000.0
PATTERNS = ((128, 1), (512, 4), (2048, 16))
HALF_STEPS = 64
SSD_GROUPS = 2
SSD_STATE = 128
SSD_XBC = 1536
SSD_CONV = 5
CHUNK = 128
SSD_CPS = 8
N_EXPERTS = 16
CAPACITY_FACTOR = 2
D_FF = 2048
SHORT_CONV = 3
EPS = 1e-6

LANE = 128
VMEM_LIMIT = 56 * 1024 * 1024

ROW_TILE = 512
OUT_ROW_TILE = 256
ATTN_SUPER = 2048
ATTN_HALO = 64
ATTN_TQ = 128
ATTN_TK = ATTN_TQ + 2 * HALF_STEPS
ATTN_UNROLL = 16
TOK_TILE = 512
SUB = TOK_TILE // LANE
SLOT_ALIGN = 16
WINDOW = 96
FFN_ROWS = 256
CAST_ROWS = 64


def _cparams(n_axes):
    return pltpu.CompilerParams(dimension_semantics=("arbitrary",) * n_axes, vmem_limit_bytes=VMEM_LIMIT)


def _const_spec(shape):
    nd = len(shape)
    return pl.BlockSpec(shape, lambda *_: (0,) * nd, pipeline_mode=pl.Buffered(1))


def _dot(a, b):
    return jnp.dot(a, b, preferred_element_type=F32)


def _dot_nt(a, b):
    return lax.dot_general(a, b, (((1,), (1,)), ((), ())), preferred_element_type=F32)


def _dot_tn(a, b):
    return lax.dot_general(a, b, (((0,), (0,)), ((), ())), preferred_element_type=F32)


def _split2(x):
    hi = x.astype(BF16)
    lo = (x - hi.astype(F32)).astype(BF16)
    return hi, lo


def _split3(x):
    hi = x.astype(BF16)
    r = x - hi.astype(F32)
    mid = r.astype(BF16)
    lo = (r - mid.astype(F32)).astype(BF16)
    return hi, mid, lo


def _dot3(x, m_bf16):
    hi, mid, lo = _split3(x)
    return _dot(hi, m_bf16) + _dot(mid, m_bf16) + _dot(lo, m_bf16)


def _rms(x, g):
    return x * lax.rsqrt(jnp.mean(x * x, axis=-1, keepdims=True) + EPS) * g


def _silu(x):
    return x * jax.nn.sigmoid(x)


def _in_even_kernel(x_ref, g_ref, wq_ref, wk_ref, wv_ref, wz_ref, wx_ref, wd_ref, qg_ref, kg_ref, bd_ref,
                    rc_ref, r1_ref, r2_ref, q_ref, k_ref, v_ref, z_ref, xbc_ref, dt_ref):
    hb = _rms(x_ref[...], g_ref[...]).astype(BF16)
    bd = bd_ref[...]
    rc, r1, r2 = rc_ref[...], r1_ref[...], r2_ref[...]

    def head_norm_rope(w_ref, gain_ref, out_ref):
        t = _dot(hb, w_ref[...])
        for c in range(D_MODEL // 256):
            tc = t[:, c * 256:(c + 1) * 256]
            sq_hi, sq_lo = _split2(tc * tc)
            ms = _dot(sq_hi, bd) + _dot(sq_lo, bd)
            tn = tc * lax.rsqrt(ms + EPS) * gain_ref[:, c * 256:(c + 1) * 256]
            for hh in range(2):
                u = tn[:, hh * LANE:(hh + 1) * LANE]
                r = u * rc + pltpu.roll(u, LANE - ROPE_HALF, 1) * r1 + pltpu.roll(u, ROPE_HALF, 1) * r2
                out_ref[:, c * 256 + hh * LANE:c * 256 + (hh + 1) * LANE] = r

    head_norm_rope(wq_ref, qg_ref, q_ref)
    head_norm_rope(wk_ref, kg_ref, k_ref)
    v_ref[...] = _dot(hb, wv_ref[...])
    z_ref[...] = _dot(hb, wz_ref[...])
    xbc_ref[...] = _dot(hb, wx_ref[...])
    dt_ref[...] = _dot(hb, wd_ref[...])


def _in_even(x, g, wq, wk, wv, wz, wx, wd, qg, kg, bd, rc, r1, r2, seq):
    n = x.shape[0]
    tm = ROW_TILE
    row = lambda w: pl.BlockSpec((tm, w), lambda i: (i, 0))
    tab = pl.BlockSpec((tm, LANE), lambda i: (i % (seq // tm), 0))
    outs = [jax.ShapeDtypeStruct((n, D_MODEL), F32)] * 4 + [jax.ShapeDtypeStruct((n, SSD_XBC), F32),
                                                           jax.ShapeDtypeStruct((n, LANE), F32)]
    return pl.pallas_call(
        _in_even_kernel, grid=(n // tm,),
        in_specs=[row(D_MODEL), _const_spec((1, D_MODEL)), _const_spec(wq.shape), _const_spec(wk.shape),
                  _const_spec(wv.shape), _const_spec(wz.shape), _const_spec(wx.shape), _const_spec(wd.shape),
                  _const_spec((1, D_MODEL)), _const_spec((1, D_MODEL)), _const_spec((256, 256)), tab, tab, tab],
        out_specs=[row(D_MODEL)] * 4 + [row(SSD_XBC), row(LANE)], out_shape=outs,
        compiler_params=_cparams(1), name="in_even",
    )(x, g, wq, wk, wv, wz, wx, wd, qg, kg, bd, rc, r1, r2)


def _attn_kernel(q_ref, k0, k1, k2, v0, v1, v2, qw_hbm, kw_hbm, vw_hbm, o_ref, qf, kf, vf, q16, k16, v16,
                 acc, mst, lst, a16, m16, l16, an3, mn3, ln3, sem, rsem, *, seq):
    dmax = PATTERNS[-1][1]
    bi, hp, j = pl.program_id(0), pl.program_id(1), pl.program_id(2)
    n_hp, n_j = pl.num_programs(1), pl.num_programs(2)
    p0 = j * ATTN_SUPER
    halo = HALF_STEPS
    step = (bi * n_hp + hp) * n_j + j
    cur = step % 2

    def residue_copies(at, buf, r, where):
        b_, hp_, j_ = at
        lanes = pl.ds(pl.multiple_of(hp_ * LANE, LANE), LANE)
        q0 = pl.multiple_of(j_ * ATTN_TQ, ATTN_TQ)
        n_kv = ATTN_TK - halo if where else ATTN_TK
        src0 = 0 if where < 0 else q0 - halo
        dst0 = halo if where < 0 else 0
        cps = [pltpu.make_async_copy(qw_hbm.at[b_, pl.ds(q0, ATTN_TQ), r, lanes], q16.at[buf, r], rsem.at[buf, 0, r])]
        for n, (src, dst) in enumerate(((kw_hbm, k16), (vw_hbm, v16))):
            cps.append(pltpu.make_async_copy(src.at[b_, pl.ds(src0, n_kv), r, lanes],
                                             dst.at[buf, r, pl.ds(dst0, n_kv), :], rsem.at[buf, 1 + n, r]))
        return cps

    def all_residues(at, buf, action):
        def run(where):
            for r in range(dmax):
                for cp in residue_copies(at, buf, r, where):
                    getattr(cp, action)()
        j_ = at[2]
        pl.when(j_ == 0)(functools.partial(run, -1))
        pl.when((j_ > 0) & (j_ < n_j - 1))(functools.partial(run, 0))
        pl.when(j_ == n_j - 1)(functools.partial(run, 1))

    def state_copies(r):
        return [pltpu.make_async_copy(src.at[r], dst.at[:, r, :], sem.at[n, r])
                for n, (src, dst) in enumerate(((a16, an3), (m16, mn3), (l16, ln3)))]

    @pl.when(step == 0)
    def _():
        k16[...] = jnp.zeros_like(k16)
        v16[...] = jnp.zeros_like(v16)
        all_residues((bi, hp, j), cur, "start")

    @pl.when(step + 1 < pl.num_programs(0) * n_hp * n_j)
    def _():
        nxt = step + 1
        all_residues((nxt // (n_hp * n_j), (nxt // n_j) % n_hp, nxt % n_j), 1 - cur, "start")

    head_a = lax.broadcasted_iota(I32, (ATTN_TQ, LANE), 1) < HEAD_DIM
    row_i = lax.broadcasted_iota(I32, (ATTN_TQ, ATTN_TK), 0)
    col_i = lax.broadcasted_iota(I32, (ATTN_TQ, ATTN_TK), 1)
    band = (col_i - row_i >= 0) & (col_i - row_i <= 2 * HALF_STEPS)
    colpos = lax.broadcasted_iota(I32, (1, ATTN_TK), 1)

    def local_softmax(q, kt, vt, mask, pos):
        ok = mask & (pos >= 0) & (pos < seq)
        valid = jnp.concatenate([ok, ok], axis=0)
        q2 = jnp.concatenate([jnp.where(head_a, q, 0.0), jnp.where(head_a, 0.0, q)], axis=0).astype(BF16)
        s = jnp.where(valid, _dot_nt(q2, kt.astype(BF16)), -jnp.inf)
        m = jnp.max(s, axis=1, keepdims=True)
        p = jnp.exp(s - m)
        l = jnp.sum(p, axis=1, keepdims=True)
        n = _dot(p.astype(BF16), vt.astype(BF16))
        return (jnp.where(head_a, m[:ATTN_TQ], m[ATTN_TQ:]), jnp.where(head_a, l[:ATTN_TQ], l[ATTN_TQ:]),
                jnp.where(head_a, n[:ATTN_TQ], n[ATTN_TQ:]))

    def visit_dense(i):
        qs = pl.multiple_of(i * ATTN_TQ, ATTN_TQ)
        rows = pl.ds(qs, ATTN_TQ)
        keys = pl.ds(qs, ATTN_TK)
        mst[rows, :], lst[rows, :], acc[rows, :] = local_softmax(
            qf[rows, :], kf[keys, :], vf[keys, :], band, p0 + qs - HALF_STEPS + colpos)

    def visit_dmax(r):
        q = q16[cur, r] * (HEAD_DIM ** -0.5)
        m16[r], l16[r], a16[r] = local_softmax(q, k16[cur, r], v16[cur, r], band,
                                               p0 - dmax * HALF_STEPS + r + dmax * colpos)

    dmid = PATTERNS[1][1]
    per = dmax // dmid
    seg_q, seg_k = ATTN_TQ // per, ATTN_TK // per
    q_step = per * (row_i % seg_q) + row_i // seg_q
    k_step = per * (col_i % seg_k - HALF_STEPS // per) + col_i // seg_k
    mask_mid = jnp.abs(k_step - q_step) <= HALF_STEPS
    k_off_mid = dmax * (colpos % seg_k - HALF_STEPS // per) + dmid * (colpos // seg_k)

    def visit_mid(i):
        r = i & (dmid - 1)
        t = i >> (dmid.bit_length() - 1)
        n0 = pl.multiple_of(t * seg_q, seg_q)
        q_rows = pl.ds(n0, seg_q)
        k_rows = pl.ds(pl.multiple_of(n0 + HALF_STEPS - HALF_STEPS // per, 8), seg_k)
        classes = [r + dmid * c for c in range(per)]
        cat = lambda ref, rows, buf=None: jnp.concatenate(
            [ref[c, rows, :] if buf is None else ref[buf, c, rows, :] for c in classes], axis=0)
        m_loc, l_loc, n_loc = local_softmax(cat(q16, q_rows, cur) * (HEAD_DIM ** -0.5), cat(k16, k_rows, cur),
                                            cat(v16, k_rows, cur), mask_mid, p0 + dmax * n0 + r + k_off_mid)
        m_old = cat(m16, q_rows)
        m_new = jnp.maximum(m_old, m_loc)
        w_old = jnp.exp(m_old - m_new)
        w_loc = jnp.exp(m_loc - m_new)
        l_new = cat(l16, q_rows) * w_old + l_loc * w_loc
        a_new = cat(a16, q_rows) * w_old + n_loc * w_loc
        for n, c in enumerate(classes):
            seg = slice(n * seg_q, (n + 1) * seg_q)
            m16[c, q_rows, :], l16[c, q_rows, :], a16[c, q_rows, :] = m_new[seg], l_new[seg], a_new[seg]

    def loop(n, fn):
        def body(i, c):
            for u in range(ATTN_UNROLL):
                fn(i * ATTN_UNROLL + u)
            return c
        lax.fori_loop(0, n // ATTN_UNROLL, body, 0)

    n_visits = ATTN_SUPER // ATTN_TQ
    all_residues((bi, hp, j), cur, "wait")
    loop(dmax, visit_dmax)
    loop(n_visits, visit_mid)
    for r in range(dmax):
        for cp in state_copies(r):
            cp.start()

    qf[...] = q_ref[0] * (HEAD_DIM ** -0.5)
    row0 = 0
    for kr, vr in ((k0, v0), (k1, v1), (k2, v2)):
        n_rows = kr.shape[1]
        kf[row0:row0 + n_rows, :] = kr[0]
        vf[row0:row0 + n_rows, :] = vr[0]
        row0 += n_rows
    loop(n_visits, visit_dense)
    for r in range(dmax):
        for cp in state_copies(r):
            cp.wait()

    def finish(i, c):
        rows = pl.ds(pl.multiple_of(i * ATTN_TQ, ATTN_TQ), ATTN_TQ)
        slabs = pl.ds(pl.multiple_of(i * (ATTN_TQ // dmax), ATTN_TQ // dmax), ATTN_TQ // dmax)
        a_a, m_a, l_a = (t[slabs].reshape(ATTN_TQ, LANE) for t in (an3, mn3, ln3))
        m_b = mst[rows, :]
        m = jnp.maximum(m_a, m_b)
        w_a, w_b = jnp.exp(m_a - m), jnp.exp(m_b - m)
        o = (a_a * w_a + acc[rows, :] * w_b) / (l_a * w_a + lst[rows, :] * w_b)
        o_ref[0, rows, :] = o.astype(BF16)
        return c

    lax.fori_loop(0, n_visits, finish, 0)


def _attention(q, k, v):
    b, seq, _ = q.shape
    nblk = seq // ATTN_HALO
    ratio = ATTN_SUPER // ATTN_HALO
    dmax = PATTERNS[-1][1]
    assert ATTN_SUPER // dmax == ATTN_TQ
    assert len(PATTERNS) == 3 and ATTN_HALO == PATTERNS[0][1] * HALF_STEPS and seq // ATTN_SUPER >= 2

    def halo(after):
        return pl.BlockSpec((1, ATTN_HALO, LANE),
                            lambda bi, hp, j: (bi, jnp.clip(ratio * (j + after) - 1 + after, 0, nblk - 1), hp))

    main = pl.BlockSpec((1, ATTN_SUPER, LANE), lambda bi, hp, j: (bi, j, hp))
    kv_specs = [halo(0), main, halo(1)]
    hbm = pl.BlockSpec(memory_space=pl.ANY)
    by_residue = lambda t: t.reshape(b, seq // dmax, dmax, D_MODEL)
    return pl.pallas_call(
        functools.partial(_attn_kernel, seq=seq),
        grid=(b, D_MODEL // LANE, seq // ATTN_SUPER),
        in_specs=[main] + kv_specs * 2 + [hbm] * 3,
        out_specs=main,
        out_shape=jax.ShapeDtypeStruct((b, seq, D_MODEL), BF16),
        scratch_shapes=[pltpu.VMEM((ATTN_SUPER, LANE), F32), pltpu.VMEM((ATTN_SUPER + 2 * ATTN_HALO, LANE), F32),
                        pltpu.VMEM((ATTN_SUPER + 2 * ATTN_HALO, LANE), F32), pltpu.VMEM((2, dmax, ATTN_TQ, LANE), F32),
                        pltpu.VMEM((2, dmax, ATTN_TK, LANE), F32), pltpu.VMEM((2, dmax, ATTN_TK, LANE), F32)]
                       + [pltpu.VMEM((ATTN_SUPER, LANE), F32)] * 3
                       + [pltpu.VMEM((dmax, ATTN_TQ, LANE), F32)] * 3
                       + [pltpu.VMEM((ATTN_TQ, dmax, LANE), F32)] * 3
                       + [pltpu.SemaphoreType.DMA((3, dmax)), pltpu.SemaphoreType.DMA((2, 3, dmax))],
        compiler_params=_cparams(3), name="dilated_attn",
    )(q, k, k, k, v, v, v, by_residue(q), by_residue(k), by_residue(v))


def _shifted(cur, prev8, next8, s, first, last):
    n = cur.shape[0]
    if s == 0:
        return cur
    rolled = pltpu.roll(cur, (-s) % n, 0)
    if s < 0:
        top = jnp.concatenate([jnp.where(first, 0.0, prev8), cur[0:8]], axis=0)[8 + s:16 + s]
        return jnp.concatenate([top, rolled[8:]], axis=0)
    bottom = jnp.concatenate([cur[n - 8:n], jnp.where(last, 0.0, next8)], axis=0)[s:8 + s]
    return jnp.concatenate([rolled[:n - 8], bottom], axis=0)


def _ssd_conv_kernel(c_ref, p_ref, n_ref, w_ref, b_ref, o_ref):
    first = pl.program_id(1) == 0
    last = pl.program_id(1) == pl.num_programs(1) - 1
    for c in range(SSD_XBC // 256):
        sl = slice(c * 256, (c + 1) * 256)
        cur, prev8, next8 = c_ref[0, :, sl], p_ref[0, :, sl], n_ref[0, :, sl]
        y = b_ref[:, sl]
        for j in range(SSD_CONV):
            y = y + _shifted(cur, prev8, next8, j - SSD_CONV // 2, first, last) * w_ref[j:j + 1, sl]
        o_ref[0, :, sl] = _silu(y)


def _halo_specs(tm, width, seq):
    cur = pl.BlockSpec((1, tm, width), lambda b, i: (b, i, 0))
    prev = pl.BlockSpec((1, 8, width), lambda b, i: (b, jnp.maximum(i * (tm // 8) - 1, 0), 0))
    nxt = pl.BlockSpec((1, 8, width), lambda b, i: (b, jnp.minimum((i + 1) * (tm // 8), seq // 8 - 1), 0))
    return cur, prev, nxt


def _ssd_conv(xbc, w, bias):
    b, seq, width = xbc.shape
    tm = 512
    cur, prev, nxt = _halo_specs(tm, width, seq)
    return pl.pallas_call(
        _ssd_conv_kernel, grid=(b, seq // tm),
        in_specs=[cur, prev, nxt, pl.BlockSpec((8, width), lambda b, i: (0, 0)),
                  pl.BlockSpec((1, width), lambda b, i: (0, 0))],
        out_specs=cur, out_shape=jax.ShapeDtypeStruct(xbc.shape, F32),
        compiler_params=_cparams(2), name="ssd_conv",
    )(xbc, xbc, xbc, w, bias)


def _softplus(x):
    return jnp.maximum(x, 0.0) + jnp.log1p(jnp.exp(-jnp.abs(x)))


def _tri(kind):
    s = lax.broadcasted_iota(I32, (CHUNK, CHUNK), 0)
    l = lax.broadcasted_iota(I32, (CHUNK, CHUNK), 1)
    return {"le": s <= l, "ge": s >= l, "lt": s < l}[kind]


def _expand(cols, e2_ref):
    hi, lo = _split2(cols)
    return _dot(jnp.concatenate([hi, lo], axis=1), e2_ref[...])


def _ssd_bwd_kernel(xs_ref, b_ref, dt_ref, pc_ref, e2_ref, sb_ref, st):
    @pl.when(pl.program_id(1) == 0)
    def _():
        st[...] = jnp.zeros_like(st)

    for c in reversed(range(SSD_CPS)):
        _ssd_bwd_chunk(slice(c * CHUNK, (c + 1) * CHUNK), c, xs_ref, b_ref, dt_ref, pc_ref, e2_ref, sb_ref, st)


def _ssd_bwd_chunk(rs, c, xs_ref, b_ref, dt_ref, pc_ref, e2_ref, sb_ref, st):
    sb_ref[0, c] = st[...].astype(BF16)
    dt_t = dt_ref[0, rs, :].T
    dtb = _softplus(dt_t[N_HEADS:2 * N_HEADS, :] + pc_ref[:, 1:2])
    a = dtb * pc_ref[:, 3:4]
    ex = _dot3(a, _tri("lt").astype(BF16))
    tot = ex[:, CHUNK - 1:CHUNK] + a[:, CHUNK - 1:CHUNK]
    rowform = jnp.concatenate([dtb * jnp.exp(ex), jnp.broadcast_to(jnp.exp(tot), (N_HEADS, CHUNK)),
                               jnp.zeros((CHUNK - 2 * N_HEADS, CHUNK), F32)], axis=0)
    ex2 = _expand(rowform.T, e2_ref)
    xw = (xs_ref[0, rs, :] * ex2[:, :D_MODEL]).astype(BF16)
    half = D_MODEL // SSD_GROUPS
    upd = [_dot(b_ref[0, rs, g * SSD_STATE:(g + 1) * SSD_STATE].T.astype(BF16), xw[:, g * half:(g + 1) * half])
           for g in range(SSD_GROUPS)]
    st[...] = st[...] * ex2[0:1, D_MODEL:] + jnp.concatenate(upd, axis=1)


def _ssd_fwd_kernel(xs_ref, b_ref, c_ref, dt_ref, z_ref, sb_ref, pc_ref, e3_ref, dexp_ref, on_ref, o_ref, st):
    @pl.when(pl.program_id(1) == 0)
    def _():
        st[...] = jnp.zeros_like(st)

    for c in range(SSD_CPS):
        _ssd_fwd_chunk(slice(c * CHUNK, (c + 1) * CHUNK), c, xs_ref, b_ref, c_ref, dt_ref, z_ref, sb_ref, pc_ref,
                       e3_ref, dexp_ref, on_ref, o_ref, st)


def _ssd_fwd_chunk(rs, c, xs_ref, b_ref, c_ref, dt_ref, z_ref, sb_ref, pc_ref, e3_ref, dexp_ref, on_ref, o_ref, st):
    xs = xs_ref[0, rs, :]
    dt_t = dt_ref[0, rs, :].T
    dtf = _softplus(dt_t[0:N_HEADS, :] + pc_ref[:, 0:1])
    dtb = _softplus(dt_t[N_HEADS:2 * N_HEADS, :] + pc_ref[:, 1:2])
    af = dtf * pc_ref[:, 2:3]
    ab = dtb * pc_ref[:, 3:4]
    csf = _dot3(af, _tri("le").astype(BF16))
    rcs = _dot3(ab, _tri("ge").astype(BF16))
    totf = csf[:, CHUNK - 1:CHUNK]
    rowform = jnp.concatenate([dtf * jnp.exp(totf - csf), jnp.exp(csf), jnp.exp(rcs), csf, rcs,
                               jnp.zeros((CHUNK - 5 * N_HEADS, CHUNK), F32)], axis=0)
    cols = rowform.T
    ex3 = _expand(cols, e3_ref)
    w_state, e_f, e_b = ex3[:, :D_MODEL], ex3[:, D_MODEL:2 * D_MODEL], ex3[:, 2 * D_MODEL:]

    xb = xs.astype(BF16)
    lower, upper = _tri("ge"), _tri("le")
    head_a = lax.broadcasted_iota(I32, (CHUNK, LANE), 1) < HEAD_DIM
    half = D_MODEL // SSD_GROUPS
    hpg = N_HEADS // SSD_GROUPS
    st_all = st[...]
    sb_all = sb_ref[0, c]
    ys = []
    b_t = []
    for g in range(SSD_GROUPS):
        bg = b_ref[0, rs, g * SSD_STATE:(g + 1) * SSD_STATE]
        cg = c_ref[0, rs, g * SSD_STATE:(g + 1) * SSD_STATE].astype(BF16)
        b_t.append(bg.T.astype(BF16))
        gm = _dot_nt(cg, bg.astype(BF16))
        states = jnp.concatenate([st_all[:, g * half:(g + 1) * half].astype(BF16),
                                  sb_all[:, g * half:(g + 1) * half]], axis=1)
        off = _dot(cg, states)
        y_off = (off[:, :half] * e_f[:, g * half:(g + 1) * half]
                 + off[:, half:] * e_b[:, g * half:(g + 1) * half])
        for pair in range(hpg // 2):
            ms = []
            for h in (g * hpg + 2 * pair, g * hpg + 2 * pair + 1):
                dec_f = jnp.where(lower, jnp.exp(cols[:, 3 * N_HEADS + h:3 * N_HEADS + h + 1] - csf[h:h + 1, :]), 0.0)
                dec_b = jnp.where(upper, jnp.exp(cols[:, 4 * N_HEADS + h:4 * N_HEADS + h + 1] - rcs[h:h + 1, :]), 0.0)
                ms.append((gm * (dec_f * dtf[h:h + 1, :] + dec_b * dtb[h:h + 1, :])).astype(BF16))
            lo = g * half + pair * LANE
            xp = xb[:, lo:lo + LANE]
            ys.append(jnp.where(head_a, _dot(ms[0], xp), _dot(ms[1], xp))
                      + y_off[:, pair * LANE:(pair + 1) * LANE])
    y = jnp.concatenate(ys, axis=1) + xs * dexp_ref[...]
    yz = y * _silu(z_ref[0, rs, :])
    o_ref[0, rs, :] = _rms(yz, on_ref[...]).astype(BF16)

    xw = (xs * w_state).astype(BF16)
    upd = [_dot(b_t[g], xw[:, g * half:(g + 1) * half]) for g in range(SSD_GROUPS)]
    st[...] = st_all * e_f[CHUNK - 1:CHUNK, :] + jnp.concatenate(upd, axis=1)


def _ssd(xbc_act, dt, z, pc, dexp, out_norm):
    b, seq, _ = xbc_act.shape
    nc = seq // CHUNK
    sel = np.zeros((2 * CHUNK, 3 * D_MODEL), np.float32)
    for part in range(3):
        for h in range(N_HEADS):
            for rep in range(2):
                sel[rep * CHUNK + part * N_HEADS + h, part * D_MODEL + h * HEAD_DIM:part * D_MODEL + (h + 1) * HEAD_DIM] = 1.0
    e3 = jnp.asarray(sel, BF16)
    e2 = jnp.asarray(sel[:, :2 * D_MODEL], BF16)

    rows = SSD_CPS * CHUNK
    nblk = seq // rows
    bcol = D_MODEL // (2 * SSD_STATE)
    rev = lambda bi, c: (bi, nblk - 1 - c, 0)
    sb = pl.pallas_call(
        _ssd_bwd_kernel, grid=(b, nblk),
        in_specs=[pl.BlockSpec((1, rows, D_MODEL), rev),
                  pl.BlockSpec((1, rows, 2 * SSD_STATE), lambda bi, c: (bi, nblk - 1 - c, bcol)),
                  pl.BlockSpec((1, rows, LANE), rev), _const_spec(pc.shape), _const_spec(e2.shape)],
        out_specs=pl.BlockSpec((1, SSD_CPS, SSD_STATE, D_MODEL), lambda bi, c: (bi, nblk - 1 - c, 0, 0)),
        out_shape=jax.ShapeDtypeStruct((b, nc, SSD_STATE, D_MODEL), BF16),
        scratch_shapes=[pltpu.VMEM((SSD_STATE, D_MODEL), F32)],
        compiler_params=_cparams(2), name="ssd_bwd_state",
    )(xbc_act, xbc_act, dt, pc, e2)

    fwd = lambda bi, c: (bi, c, 0)
    return pl.pallas_call(
        _ssd_fwd_kernel, grid=(b, nblk),
        in_specs=[pl.BlockSpec((1, rows, D_MODEL), fwd),
                  pl.BlockSpec((1, rows, 2 * SSD_STATE), lambda bi, c: (bi, c, bcol)),
                  pl.BlockSpec((1, rows, 2 * SSD_STATE), lambda bi, c: (bi, c, bcol + 1)),
                  pl.BlockSpec((1, rows, LANE), fwd), pl.BlockSpec((1, rows, D_MODEL), fwd),
                  pl.BlockSpec((1, SSD_CPS, SSD_STATE, D_MODEL), lambda bi, c: (bi, c, 0, 0)),
                  _const_spec(pc.shape), _const_spec(e3.shape), _const_spec((1, D_MODEL)), _const_spec((1, D_MODEL))],
        out_specs=pl.BlockSpec((1, rows, D_MODEL), fwd),
        out_shape=jax.ShapeDtypeStruct((b, seq, D_MODEL), BF16),
        scratch_shapes=[pltpu.VMEM((SSD_STATE, D_MODEL), F32)],
        compiler_params=_cparams(2), name="ssd_fwd",
    )(xbc_act, xbc_act, xbc_act, dt, z, sb, pc, e3, dexp, out_norm)


def _norm_and_route(x1, fg_ref, wr_hi_ref, wr_lo_ref, hn_ref, aff_ref):
    hn = _rms(x1, fg_ref[...])
    hi, lo = _split2(hn)
    hn_ref[...] = hi
    logits = _dot(hi, wr_hi_ref[...]) + _dot(lo, wr_hi_ref[...]) + _dot(hi, wr_lo_ref[...])
    lt = logits.T[0:N_EXPERTS, :]
    e = jnp.exp(lt - jnp.max(lt, axis=0, keepdims=True))
    aff_ref[0] = e / jnp.sum(e, axis=0, keepdims=True)


def _out_even_kernel(x_ref, a_ref, y_ref, wa_ref, wy_ref, fg_ref, wr_hi_ref, wr_lo_ref, x1_ref, hn_ref, aff_ref):
    x1 = x_ref[...] + _dot(a_ref[...], wa_ref[...]) + _dot(y_ref[...], wy_ref[...])
    x1_ref[...] = x1
    _norm_and_route(x1, fg_ref, wr_hi_ref, wr_lo_ref, hn_ref, aff_ref)


def _out_odd_kernel(x_ref, gb_ref, cu_ref, cp_ref, cn_ref, cw_ref, wo_ref, fg_ref, wr_hi_ref, wr_lo_ref,
                    x1_ref, hn_ref, aff_ref, *, tiles_per_seq):
    i = pl.program_id(0) % tiles_per_seq
    first, last = i == 0, i == tiles_per_seq - 1
    cur, prev8, next8 = cu_ref[...], cp_ref[...], cn_ref[...]
    conv = sum(_shifted(cur, prev8, next8, j - SHORT_CONV // 2, first, last) * cw_ref[j:j + 1, :]
               for j in range(SHORT_CONV))
    x1 = x_ref[...] + _dot((gb_ref[...] * conv).astype(BF16), wo_ref[...])
    x1_ref[...] = x1
    _norm_and_route(x1, fg_ref, wr_hi_ref, wr_lo_ref, hn_ref, aff_ref)


def _route_outs(n, b, seq, tm):
    row = pl.BlockSpec((tm, D_MODEL), lambda i: (i, 0))
    aff = pl.BlockSpec((1, N_EXPERTS, tm), lambda i: (i // (seq // tm), 0, i % (seq // tm)))
    shapes = [jax.ShapeDtypeStruct((n, D_MODEL), F32), jax.ShapeDtypeStruct((n, D_MODEL), BF16),
              jax.ShapeDtypeStruct((b, N_EXPERTS, seq), F32)]
    return [row, row, aff], shapes


def _out_even(x, attn, y, wa, wy, fg, wr_hi, wr_lo, b, seq):
    n = x.shape[0]
    tm = OUT_ROW_TILE
    row = pl.BlockSpec((tm, D_MODEL), lambda i: (i, 0))
    out_specs, shapes = _route_outs(n, b, seq, tm)
    return pl.pallas_call(
        _out_even_kernel, grid=(n // tm,),
        in_specs=[row, row, row, _const_spec(wa.shape), _const_spec(wy.shape), _const_spec((1, D_MODEL)),
                  _const_spec(wr_hi.shape), _const_spec(wr_lo.shape)],
        out_specs=out_specs, out_shape=shapes, compiler_params=_cparams(1), name="out_even",
    )(x, attn, y, wa, wy, fg, wr_hi, wr_lo)


def _out_odd(x, gb, cu, cw, wo, fg, wr_hi, wr_lo, b, seq):
    n = x.shape[0]
    tm = OUT_ROW_TILE
    row = pl.BlockSpec((tm, D_MODEL), lambda i: (i, 0))
    prev = pl.BlockSpec((8, D_MODEL), lambda i: (jnp.maximum(i * (tm // 8) - 1, 0), 0))
    nxt = pl.BlockSpec((8, D_MODEL), lambda i: (jnp.minimum((i + 1) * (tm // 8), n // 8 - 1), 0))
    out_specs, shapes = _route_outs(n, b, seq, tm)
    return pl.pallas_call(
        functools.partial(_out_odd_kernel, tiles_per_seq=seq // tm), grid=(n // tm,),
        in_specs=[row, row, row, prev, nxt, pl.BlockSpec((8, D_MODEL), lambda i: (0, 0)), _const_spec(wo.shape),
                  _const_spec((1, D_MODEL)), _const_spec(wr_hi.shape), _const_spec(wr_lo.shape)],
        out_specs=out_specs, out_shape=shapes, compiler_params=_cparams(1), name="out_odd",
    )(x, gb, cu, cu, cu, cw, wo, fg, wr_hi, wr_lo)


def _in_odd_kernel(x_ref, g_ref, wb_ref, wc_ref, wu_ref, gb_ref, cu_ref):
    hb = _rms(x_ref[...], g_ref[...]).astype(BF16)
    gb_ref[...] = _dot(hb, wb_ref[...])
    cu_ref[...] = _dot(hb, wc_ref[...]) * _dot(hb, wu_ref[...])


def _in_odd(x, g, wb, wc, wu):
    n = x.shape[0]
    tm = ROW_TILE
    row = pl.BlockSpec((tm, D_MODEL), lambda i: (i, 0))
    return pl.pallas_call(
        _in_odd_kernel, grid=(n // tm,),
        in_specs=[row, _const_spec((1, D_MODEL)), _const_spec(wb.shape), _const_spec(wc.shape), _const_spec(wu.shape)],
        out_specs=[row, row], out_shape=[jax.ShapeDtypeStruct((n, D_MODEL), F32)] * 2,
        compiler_params=_cparams(1), name="in_odd",
    )(x, g, wb, wc, wu)


def _count(mask):
    return jnp.sum(jnp.sum(mask.astype(F32), axis=0, keepdims=True), axis=1, keepdims=True)


def _route_kernel(aff_ref, incl_ref, ones_ref, strict_ref, local_ref, group_ref, first_ref, slot_ref, off_ref, end_ref,
                  *, cap):
    def step(i, thrs):
        bit = jnp.int32(1) << (30 - i)
        out = []
        for e in range(N_EXPERTS):
            cand = thrs[e] | bit
            out.append(jnp.where(_count(pltpu.bitcast(aff_ref[0, e], I32) >= cand) >= cap, cand, thrs[e]))
        return tuple(out)

    thrs = lax.fori_loop(0, 31, step, tuple(jnp.zeros((1, 1), I32) for _ in range(N_EXPERTS)))
    for e in range(N_EXPERTS):
        bits = pltpu.bitcast(aff_ref[0, e], I32)
        thr = thrs[e]
        gt = bits > thr
        eq = (bits == thr).astype(BF16)
        eq_rank = _dot(eq, incl_ref[...]) + _dot(strict_ref[...], _dot(eq, ones_ref[...]).astype(BF16))
        sel = (gt | ((bits == thr) & (eq_rank <= cap - _count(gt)))).astype(BF16)
        within = _dot(sel, incl_ref[...])
        totals = _dot(sel, ones_ref[...]).astype(BF16)
        local = _dot(local_ref[...], totals)
        cnt = _dot(group_ref[...], totals)
        padded = jnp.floor((cnt + (SLOT_ALIGN - 1)) * (1.0 / SLOT_ALIGN)) * SLOT_ALIGN
        start = _dot(first_ref[...], padded.astype(BF16))
        slot_ref[0, e] = jnp.where(sel > 0, (start + local + within).astype(I32) - 1, -1)
        off_ref[0, e:e + 1, :] = start.T[0:1, :].astype(I32)
        end_ref[0, e:e + 1, :] = (start + padded).T[0:1, :].astype(I32)


def _route(aff, cap):
    b, _, seq = aff.shape
    nt = seq // LANE
    tri = np.arange(LANE)
    tt = np.arange(nt)
    grp = tt // SUB
    as_bf16 = lambda m: jnp.asarray(m, BF16)
    incl = as_bf16(tri[:, None] <= tri[None, :])
    strict = as_bf16(tt[None, :] < tt[:, None])
    local = as_bf16((tt[None, :] < tt[:, None]) & (grp[None, :] == grp[:, None]))
    group = as_bf16(grp[None, :] == grp[:, None])
    first = as_bf16((grp[None, :] < grp[:, None]) & (tt[None, :] % SUB == 0))
    ones = jnp.ones((LANE, LANE), BF16)
    tiles = pl.BlockSpec((1, N_EXPERTS, nt, LANE), lambda i: (i, 0, 0, 0))
    rows = pl.BlockSpec((1, N_EXPERTS, nt), lambda i: (i, 0, 0))
    return pl.pallas_call(
        functools.partial(_route_kernel, cap=cap), grid=(b,),
        in_specs=[tiles, _const_spec((LANE, LANE)), _const_spec((LANE, LANE))] + [_const_spec((nt, nt))] * 4,
        out_specs=[tiles, rows, rows],
        out_shape=[jax.ShapeDtypeStruct((b, N_EXPERTS, nt, LANE), I32)] + [jax.ShapeDtypeStruct((b, N_EXPERTS, nt), I32)] * 2,
        compiler_params=_cparams(1), name="route",
    )(aff.reshape(b, N_EXPERTS, nt, LANE), incl, ones, strict, local, group, first)


def _one_hot(slots, base):
    return (slots == lax.broadcasted_iota(I32, (WINDOW, slots.shape[1]), 0) + base).astype(BF16)


def _gather_kernel(off_ref, end_ref, slot_ref, aff_ref, hn_ref, xe_hbm, stage, extra, sem, xsem):
    bi, j = pl.program_id(0), pl.program_id(1)
    n_j = pl.num_programs(1)
    step = bi * n_j + j
    cur = step % 2

    def window_copy(bb, jj, e, buf):
        start = pl.multiple_of(off_ref[bb, e, jj * SUB], SLOT_ALIGN)
        return pltpu.make_async_copy(stage.at[buf, e], xe_hbm.at[bb, e, pl.ds(start, WINDOW), :], sem.at[buf, e])

    tokens = hn_ref[0]
    parts = [p.astype(F32) for p in _split3(aff_ref[0, 0])]
    gates = jnp.concatenate(parts + [jnp.zeros((LANE - 3 * N_EXPERTS, TOK_TILE), F32)], axis=0).astype(BF16)

    def rows_of(p):
        return jnp.concatenate([_dot(p, tokens), _dot_nt(p, gates)], axis=1).astype(BF16)

    p_all = jnp.concatenate([_one_hot(slot_ref[0, 0, e:e + 1, :], off_ref[bi, e, j * SUB])
                             for e in range(N_EXPERTS)], axis=0)
    stage[cur] = rows_of(p_all).reshape(N_EXPERTS, WINDOW, D_MODEL + LANE)

    @pl.when(j > 0)
    def _():
        for e in range(N_EXPERTS):
            window_copy(bi, j - 1, e, 1 - cur).wait()

    for e in range(N_EXPERTS):
        window_copy(bi, j, e, cur).start()

    def overflow(e, c):
        first = off_ref[bi, e, j * SUB]
        n_win = (end_ref[bi, e, j * SUB] - first + WINDOW - 1) // WINDOW
        slots = slot_ref[0, 0, pl.ds(e, 1), :]

        def one(w, c2):
            base = pl.multiple_of(first + w * WINDOW, SLOT_ALIGN)
            extra[...] = rows_of(_one_hot(slots, base))
            cp = pltpu.make_async_copy(extra, xe_hbm.at[bi, e, pl.ds(base, WINDOW), :], xsem.at[0])
            cp.start()
            cp.wait()
            return c2

        return lax.fori_loop(1, n_win, one, c)

    lax.fori_loop(0, N_EXPERTS, overflow, 0)

    @pl.when(j == n_j - 1)
    def _():
        for e in range(N_EXPERTS):
            window_copy(bi, j, e, cur).wait()
        extra[...] = jnp.zeros_like(extra)
        cap_pad = xe_hbm.shape[2]

        def fill(e, c):
            used = end_ref[bi, e, end_ref.shape[2] - 1]
            n_big = (cap_pad - used) // WINDOW
            small0 = used + n_big * WINDOW
            n_small = (cap_pad - small0) // SLOT_ALIGN
            big = lambda i: pltpu.make_async_copy(
                extra, xe_hbm.at[bi, e, pl.ds(pl.multiple_of(used + i * WINDOW, SLOT_ALIGN), WINDOW), :], xsem.at[0])
            small = lambda i: pltpu.make_async_copy(
                extra.at[0:SLOT_ALIGN],
                xe_hbm.at[bi, e, pl.ds(pl.multiple_of(small0 + i * SLOT_ALIGN, SLOT_ALIGN), SLOT_ALIGN), :], xsem.at[0])
            for n, mk in ((n_big, big), (n_small, small)):
                lax.fori_loop(0, n, lambda i, c2, mk=mk: (mk(i).start(), c2)[1], 0)
            for n, mk in ((n_big, big), (n_small, small)):
                lax.fori_loop(0, n, lambda i, c2, mk=mk: (mk(i).wait(), c2)[1], 0)
            return c

        lax.fori_loop(0, N_EXPERTS, fill, 0)


def _gather(off, end, slot_t, aff_t, hn):
    b, seq, _ = hn.shape
    cap_pad = _cap_pad(seq)
    width = D_MODEL + LANE
    per_tile = pl.BlockSpec((1, 1, N_EXPERTS, TOK_TILE), lambda bi, j, *_: (bi, j, 0, 0))
    return pl.pallas_call(
        _gather_kernel,
        grid_spec=pltpu.PrefetchScalarGridSpec(
            num_scalar_prefetch=2, grid=(b, seq // TOK_TILE),
            in_specs=[per_tile, per_tile, pl.BlockSpec((1, TOK_TILE, D_MODEL), lambda bi, j, *_: (bi, j, 0))],
            out_specs=pl.BlockSpec(memory_space=pl.ANY),
            scratch_shapes=[pltpu.VMEM((2, N_EXPERTS, WINDOW, width), BF16), pltpu.VMEM((WINDOW, width), BF16),
                            pltpu.SemaphoreType.DMA((2, N_EXPERTS)), pltpu.SemaphoreType.DMA((1,))]),
        out_shape=jax.ShapeDtypeStruct((b, N_EXPERTS, cap_pad, width), BF16),
        compiler_params=_cparams(2), name="moe_gather",
    )(off, end, slot_t, aff_t, hn)


def _ffn_kernel(end_ref, xe_ref, wg_hbm, wu_hbm, wd_hbm, y_ref, stage_g, stage_u, stage_d, wg, wu, wd, sem, *, layer):
    e, bi, r = pl.program_id(0), pl.program_id(1), pl.program_id(2)
    used = end_ref[bi, e, end_ref.shape[2] - 1]
    pairs = ((wg_hbm, stage_g, wg), (wu_hbm, stage_u, wu), (wd_hbm, stage_d, wd))

    def weight_copies(ee):
        return [pltpu.make_async_copy(src.at[layer, ee], stg, sem.at[k]) for k, (src, stg, _) in enumerate(pairs)]

    @pl.when((bi == 0) & (r == 0))
    def _():
        @pl.when(e == 0)
        def _():
            for cp in weight_copies(e):
                cp.start()

        for cp, (_, stg, dst) in zip(weight_copies(e), pairs):
            cp.wait()
            n_rows = stg.shape[0]

            def cast(i, c, stg=stg, dst=dst):
                rows = pl.ds(pl.multiple_of(i * CAST_ROWS, CAST_ROWS), CAST_ROWS)
                dst[rows, :] = stg[rows, :].astype(BF16)
                return c

            lax.fori_loop(0, n_rows // CAST_ROWS, cast, 0)

        @pl.when(e + 1 < pl.num_programs(0))
        def _():
            for cp in weight_copies(e + 1):
                cp.start()

    @pl.when(r * FFN_ROWS < used)
    def _():
        xe = xe_ref[0, 0, :, 0:D_MODEL]
        hid = (_silu(_dot(xe, wg[...])) * _dot(xe, wu[...])).astype(BF16)
        g = xe_ref[0, 0, :, D_MODEL:].astype(F32)
        lane = lax.broadcasted_iota(I32, g.shape, 1)
        mine = (lane % N_EXPERTS == e) & (lane < 3 * N_EXPERTS)
        gate = jnp.sum(jnp.where(mine, g, 0.0), axis=1, keepdims=True)
        y_ref[0, 0] = (_dot(hid, wd[...]) * gate).astype(BF16)

    @pl.when(r * FFN_ROWS >= used)
    def _():
        y_ref[...] = jnp.zeros_like(y_ref)


def _ffn(end, xe, wg, wu, wd, layer):
    b, ne, cap_pad, width = xe.shape
    rows = lambda w: pl.BlockSpec((1, 1, FFN_ROWS, w), lambda e, bi, r, *_: (bi, e, r, 0))
    hbm = pl.BlockSpec(memory_space=pl.ANY)
    return pl.pallas_call(
        functools.partial(_ffn_kernel, layer=layer),
        grid_spec=pltpu.PrefetchScalarGridSpec(
            num_scalar_prefetch=1, grid=(ne, b, cap_pad // FFN_ROWS),
            in_specs=[rows(width), hbm, hbm, hbm],
            out_specs=rows(D_MODEL),
            scratch_shapes=[pltpu.VMEM((D_MODEL, D_FF), F32), pltpu.VMEM((D_MODEL, D_FF), F32),
                            pltpu.VMEM((D_FF, D_MODEL), F32), pltpu.VMEM((D_MODEL, D_FF), BF16),
                            pltpu.VMEM((D_MODEL, D_FF), BF16), pltpu.VMEM((D_FF, D_MODEL), BF16),
                            pltpu.SemaphoreType.DMA((3,))]),
        out_shape=jax.ShapeDtypeStruct((b, ne, cap_pad, D_MODEL), BF16),
        compiler_params=_cparams(3), name="moe_ffn",
    )(end, xe, wg, wu, wd)


def _combine_kernel(off_ref, end_ref, slot_ref, y_hbm, x1_ref, o_ref, win, extra, sem, xsem):
    bi, j = pl.program_id(0), pl.program_id(1)
    n_j = pl.num_programs(1)
    step = bi * n_j + j
    cur = step % 2

    def window_copy(bb, jj, e, buf):
        start = pl.multiple_of(off_ref[bb, e, jj * SUB], SLOT_ALIGN)
        return pltpu.make_async_copy(y_hbm.at[bb, e, pl.ds(start, WINDOW), :], win.at[buf, e], sem.at[buf, e])

    @pl.when(step == 0)
    def _():
        for e in range(N_EXPERTS):
            window_copy(bi, j, e, cur).start()

    @pl.when(step + 1 < pl.num_programs(0) * n_j)
    def _():
        nxt = step + 1
        for e in range(N_EXPERTS):
            window_copy(nxt // n_j, nxt % n_j, e, 1 - cur).start()

    ps = []
    for e in range(N_EXPERTS):
        window_copy(bi, j, e, cur).wait()
        ps.append(_one_hot(slot_ref[0, 0, e:e + 1, :], off_ref[bi, e, j * SUB]))
    p_all = jnp.concatenate(ps, axis=0)
    y_all = win[cur].reshape(N_EXPERTS * WINDOW, D_MODEL)
    o_ref[0] = x1_ref[0] + _dot_tn(p_all, y_all)

    def overflow(e, c):
        first = off_ref[bi, e, j * SUB]
        n_win = (end_ref[bi, e, j * SUB] - first + WINDOW - 1) // WINDOW
        slots = slot_ref[0, 0, pl.ds(e, 1), :]

        def one(w, c2):
            base = pl.multiple_of(first + w * WINDOW, SLOT_ALIGN)
            cp = pltpu.make_async_copy(y_hbm.at[bi, e, pl.ds(base, WINDOW), :], extra, xsem.at[0])
            cp.start()
            cp.wait()
            o_ref[0] += _dot_tn(_one_hot(slots, base), extra[...])
            return c2

        return lax.fori_loop(1, n_win, one, c)

    lax.fori_loop(0, N_EXPERTS, overflow, 0)


def _combine(off, end, slot_t, y, x1):
    b, seq, _ = x1.shape
    tile = pl.BlockSpec((1, TOK_TILE, D_MODEL), lambda bi, j, *_: (bi, j, 0))
    return pl.pallas_call(
        _combine_kernel,
        grid_spec=pltpu.PrefetchScalarGridSpec(
            num_scalar_prefetch=2, grid=(b, seq // TOK_TILE),
            in_specs=[pl.BlockSpec((1, 1, N_EXPERTS, TOK_TILE), lambda bi, j, *_: (bi, j, 0, 0)),
                      pl.BlockSpec(memory_space=pl.ANY), tile],
            out_specs=tile,
            scratch_shapes=[pltpu.VMEM((2, N_EXPERTS, WINDOW, D_MODEL), BF16), pltpu.VMEM((WINDOW, D_MODEL), BF16),
                            pltpu.SemaphoreType.DMA((2, N_EXPERTS)), pltpu.SemaphoreType.DMA((1,))]),
        out_shape=jax.ShapeDtypeStruct((b, seq, D_MODEL), F32),
        compiler_params=_cparams(2), name="moe_combine",
    )(off, end, slot_t, y, x1)


def _cap_pad(seq):
    cap = CAPACITY_FACTOR * seq // N_EXPERTS
    worst = cap + (seq // TOK_TILE) * (SLOT_ALIGN - 1) + WINDOW
    return -(-worst // FFN_ROWS) * FFN_ROWS


def _moe(x1, hn, aff, wg, wu, wd, layer):
    b, seq, _ = x1.shape
    cap = CAPACITY_FACTOR * seq // N_EXPERTS
    slot, off, end = _route(aff, cap)
    per_tile = lambda a: jnp.swapaxes(a.reshape(b, N_EXPERTS, seq // TOK_TILE, TOK_TILE), 1, 2)
    slot_t = per_tile(slot)
    xe = _gather(off, end, slot_t, per_tile(aff), hn)
    y = _ffn(end, xe, wg, wu, wd, layer)
    return _combine(off, end, slot_t, y, x1)


def _rope_tables(seq):
    inv_freq = ROPE_THETA ** (-jnp.arange(ROPE_HALF, dtype=F32) * 2.0 / (2 * ROPE_HALF))
    ang = jnp.arange(seq, dtype=F32)[:, None] * inv_freq[None, :]
    cos, sin = jnp.cos(ang), jnp.sin(ang)
    z = lambda w: jnp.zeros((seq, w), F32)
    rest = HEAD_DIM - 2 * ROPE_HALF
    rc = jnp.concatenate([cos, cos, jnp.ones((seq, rest), F32)], axis=1)
    r1 = jnp.concatenate([-sin, z(ROPE_HALF + rest)], axis=1)
    r2 = jnp.concatenate([z(ROPE_HALF), sin, z(rest)], axis=1)
    return tuple(jnp.tile(t, (1, LANE // HEAD_DIM)) for t in (rc, r1, r2))


def _router_split(w):
    wp = jnp.pad(w, ((0, 0), (0, LANE - N_EXPERTS)))
    hi = wp.astype(BF16)
    return hi, (wp - hi.astype(F32)).astype(BF16)


def kernel(x, attn_norm, w_in_even, q_norm, k_norm, ssd_conv_w, ssd_conv_b, ssd_a_log_fwd, ssd_a_log_bwd,
           ssd_dt_bias_fwd, ssd_dt_bias_bwd, ssd_d, ssd_out_norm, w_out_even, conv_norm, conv_w_in, conv_w,
           conv_w_out, ffn_norm, router_w, expert_w_gate, expert_w_up, expert_w_down):
    b, seq, _ = x.shape
    n = b * seq
    depth = ffn_norm.shape[0]
    rc, r1, r2 = _rope_tables(seq)
    blk = np.arange(256) // HEAD_DIM
    bd = jnp.asarray((blk[:, None] == blk[None, :]) / HEAD_DIM, BF16)
    row = lambda v: v.reshape(1, -1).astype(F32)

    xf = x.reshape(n, D_MODEL)
    for layer in range(depth):
        i = layer // 2
        wr_hi, wr_lo = _router_split(router_w[layer])
        fg = row(ffn_norm[layer])
        if layer % 2 == 0:
            w = w_in_even[i].astype(BF16)
            o = np.cumsum([0, D_MODEL, D_MODEL, D_MODEL, D_MODEL, SSD_XBC, N_HEADS, N_HEADS])
            wq, wk, wv, wz, wx = (w[:, o[j]:o[j + 1]] for j in range(5))
            wd = jnp.pad(w[:, o[5]:o[7]], ((0, 0), (0, LANE - 2 * N_HEADS)))
            tile_heads = lambda g: row(jnp.tile(g, N_HEADS))
            q, k, v, z, xbc, dt = _in_even(xf, row(attn_norm[i]), wq, wk, wv, wz, wx, wd,
                                           tile_heads(q_norm[i]), tile_heads(k_norm[i]), bd, rc, r1, r2, seq)
            as3 = lambda t: t.reshape(b, seq, -1)
            attn = _attention(as3(q), as3(k), as3(v))
            cw = jnp.pad(ssd_conv_w[i], ((0, 8 - SSD_CONV), (0, 0)))
            act = _ssd_conv(as3(xbc), cw, row(ssd_conv_b[i]))
            pc = jnp.pad(jnp.stack([ssd_dt_bias_fwd[i], ssd_dt_bias_bwd[i], -jnp.exp(ssd_a_log_fwd[i]),
                                    -jnp.exp(ssd_a_log_bwd[i])], axis=1).astype(F32), ((0, 0), (0, LANE - 4)))
            y = _ssd(act, as3(dt), as3(z), pc, row(jnp.repeat(ssd_d[i], HEAD_DIM)), row(ssd_out_norm[i]))
            wo = w_out_even[i].astype(BF16)
            x1, hn, aff = _out_even(xf, attn.reshape(n, D_MODEL), y.reshape(n, D_MODEL), wo[:D_MODEL], wo[D_MODEL:],
                                    fg, wr_hi, wr_lo, b, seq)
        else:
            w = conv_w_in[i].astype(BF16)
            gb, cu = _in_odd(xf, row(conv_norm[i]), w[:, :D_MODEL], w[:, D_MODEL:2 * D_MODEL], w[:, 2 * D_MODEL:])
            cw = jnp.pad(conv_w[i], ((0, 8 - SHORT_CONV), (0, 0)))
            x1, hn, aff = _out_odd(xf, gb, cu, cw, conv_w_out[i].astype(BF16), fg, wr_hi, wr_lo, b, seq)
        xf = _moe(x1.reshape(b, seq, D_MODEL), hn.reshape(b, seq, D_MODEL), aff, expert_w_gate, expert_w_up,
                  expert_w_down, layer).reshape(n, D_MODEL)
    return xf.reshape(b, seq, D_MODEL)
```

```python
import functools
import math

import jax
import jax.numpy as jnp
import numpy as np
from jax import lax
from jax.experimental import pallas as pl
from jax.experimental.pallas import tpu as pltpu

F32, BF16, I32 = jnp.float32, jnp.bfloat16, jnp.int32

D_MODEL = 1024
N_HEADS = 16
HEAD_DIM = 64
ROPE_HALF = 8
ROPE_THETA = 500000.0
PATTERNS = ((128, 1), (512, 4), (2048, 16))
HALF_STEPS = 64
SSD_GROUPS = 2
SSD_STATE = 128
SSD_XBC = 1536
SSD_CONV = 5
CHUNK = 128
SSD_CPS = 8
N_EXPERTS = 16
CAPACITY_FACTOR = 2
D_FF = 2048
SHORT_CONV = 3
EPS = 1e-6

LANE = 128
VMEM_LIMIT = 56 * 1024 * 1024

ROW_TILE = 512
OUT_ROW_TILE = 256
ATTN_SUPER = 2048
ATTN_HALO = 64
ATTN_TQ = 128
ATTN_TK = ATTN_TQ + 2 * HALF_STEPS
ATTN_UNROLL = 16
TOK_TILE = 512
SUB = TOK_TILE // LANE
SLOT_ALIGN = 16
WINDOW = 96
FFN_ROWS = 384
CAST_ROWS = 64


def _cparams(n_axes):
    return pltpu.CompilerParams(dimension_semantics=("arbitrary",) * n_axes, vmem_limit_bytes=VMEM_LIMIT)


def _const_spec(shape):
    nd = len(shape)
    return pl.BlockSpec(shape, lambda *_: (0,) * nd, pipeline_mode=pl.Buffered(1))


def _dot(a, b):
    return jnp.dot(a, b, preferred_element_type=F32)


def _dot_nt(a, b):
    return lax.dot_general(a, b, (((1,), (1,)), ((), ())), preferred_element_type=F32)


def _dot_tn(a, b):
    return lax.dot_general(a, b, (((0,), (0,)), ((), ())), preferred_element_type=F32)


def _split2(x):
    hi = x.astype(BF16)
    lo = (x - hi.astype(F32)).astype(BF16)
    return hi, lo


def _split3(x):
    hi = x.astype(BF16)
    r = x - hi.astype(F32)
    mid = r.astype(BF16)
    lo = (r - mid.astype(F32)).astype(BF16)
    return hi, mid, lo


def _dot3(x, m_bf16):
    hi, mid, lo = _split3(x)
    return _dot(hi, m_bf16) + _dot(mid, m_bf16) + _dot(lo, m_bf16)


def _rms(x, g):
    return x * lax.rsqrt(jnp.mean(x * x, axis=-1, keepdims=True) + EPS) * g


def _silu(x):
    return x * jax.nn.sigmoid(x)


def _in_even_kernel(x_ref, g_ref, wq_ref, wk_ref, wv_ref, wz_ref, wx_ref, wd_ref, qg_ref, kg_ref, bd_ref,
                    rc_ref, r1_ref, r2_ref, q_ref, k_ref, v_ref, z_ref, xbc_ref, dt_ref):
    hb = _rms(x_ref[...], g_ref[...]).astype(BF16)
    bd = bd_ref[...]
    rc, r1, r2 = rc_ref[...], r1_ref[...], r2_ref[...]

    def head_norm_rope(w_ref, gain_ref, out_ref):
        t = _dot(hb, w_ref[...])
        for c in range(D_MODEL // 256):
            tc = t[:, c * 256:(c + 1) * 256]
            sq_hi, sq_lo = _split2(tc * tc)
            ms = _dot(sq_hi, bd) + _dot(sq_lo, bd)
            tn = tc * lax.rsqrt(ms + EPS) * gain_ref[:, c * 256:(c + 1) * 256]
            for hh in range(2):
                u = tn[:, hh * LANE:(hh + 1) * LANE]
                r = u * rc + pltpu.roll(u, LANE - ROPE_HALF, 1) * r1 + pltpu.roll(u, ROPE_HALF, 1) * r2
                out_ref[:, c * 256 + hh * LANE:c * 256 + (hh + 1) * LANE] = r

    head_norm_rope(wq_ref, qg_ref, q_ref)
    head_norm_rope(wk_ref, kg_ref, k_ref)
    v_ref[...] = _dot(hb, wv_ref[...])
    z_ref[...] = _dot(hb, wz_ref[...])
    xbc_ref[...] = _dot(hb, wx_ref[...])
    dt_ref[...] = _dot(hb, wd_ref[...])


def _in_even(x, g, wq, wk, wv, wz, wx, wd, qg, kg, bd, rc, r1, r2, seq):
    n = x.shape[0]
    tm = ROW_TILE
    row = lambda w: pl.BlockSpec((tm, w), lambda i: (i, 0))
    tab = pl.BlockSpec((tm, LANE), lambda i: (i % (seq // tm), 0))
    outs = [jax.ShapeDtypeStruct((n, D_MODEL), F32)] * 4 + [jax.ShapeDtypeStruct((n, SSD_XBC), F32),
                                                           jax.ShapeDtypeStruct((n, LANE), F32)]
    return pl.pallas_call(
        _in_even_kernel, grid=(n // tm,),
        in_specs=[row(D_MODEL), _const_spec((1, D_MODEL)), _const_spec(wq.shape), _const_spec(wk.shape),
                  _const_spec(wv.shape), _const_spec(wz.shape), _const_spec(wx.shape), _const_spec(wd.shape),
                  _const_spec((1, D_MODEL)), _const_spec((1, D_MODEL)), _const_spec((256, 256)), tab, tab, tab],
        out_specs=[row(D_MODEL)] * 4 + [row(SSD_XBC), row(LANE)], out_shape=outs,
        compiler_params=_cparams(1), name="in_even",
    )(x, g, wq, wk, wv, wz, wx, wd, qg, kg, bd, rc, r1, r2)


def _attn_kernel(q_ref, k0, k1, k2, v0, v1, v2, qw_hbm, kw_hbm, vw_hbm, o_ref, qf, kf, vf, q16, k16, v16,
                 acc, mst, lst, a16, m16, l16, an3, mn3, ln3, sem, rsem, *, seq):
    dmax = PATTERNS[-1][1]
    bi, hp, j = pl.program_id(0), pl.program_id(1), pl.program_id(2)
    n_hp, n_j = pl.num_programs(1), pl.num_programs(2)
    p0 = j * ATTN_SUPER
    halo = HALF_STEPS
    step = (bi * n_hp + hp) * n_j + j
    cur = step % 2

    def residue_copies(at, buf, r, where):
        b_, hp_, j_ = at
        lanes = pl.ds(pl.multiple_of(hp_ * LANE, LANE), LANE)
        q0 = pl.multiple_of(j_ * ATTN_TQ, ATTN_TQ)
        n_kv = ATTN_TK - halo if where else ATTN_TK
        src0 = 0 if where < 0 else q0 - halo
        dst0 = halo if where < 0 else 0
        cps = [pltpu.make_async_copy(qw_hbm.at[b_, pl.ds(q0, ATTN_TQ), r, lanes], q16.at[buf, r], rsem.at[buf, 0, r])]
        for n, (src, dst) in enumerate(((kw_hbm, k16), (vw_hbm, v16))):
            cps.append(pltpu.make_async_copy(src.at[b_, pl.ds(src0, n_kv), r, lanes],
                                             dst.at[buf, r, pl.ds(dst0, n_kv), :], rsem.at[buf, 1 + n, r]))
        return cps

    def all_residues(at, buf, action):
        def run(where):
            for r in range(dmax):
                for cp in residue_copies(at, buf, r, where):
                    getattr(cp, action)()
        j_ = at[2]
        pl.when(j_ == 0)(functools.partial(run, -1))
        pl.when((j_ > 0) & (j_ < n_j - 1))(functools.partial(run, 0))
        pl.when(j_ == n_j - 1)(functools.partial(run, 1))

    def state_copies(r):
        return [pltpu.make_async_copy(src.at[r], dst.at[:, r, :], sem.at[n, r])
                for n, (src, dst) in enumerate(((a16, an3), (m16, mn3), (l16, ln3)))]

    @pl.when(step == 0)
    def _():
        k16[...] = jnp.zeros_like(k16)
        v16[...] = jnp.zeros_like(v16)
        all_residues((bi, hp, j), cur, "start")

    @pl.when(step + 1 < pl.num_programs(0) * n_hp * n_j)
    def _():
        nxt = step + 1
        all_residues((nxt // (n_hp * n_j), (nxt // n_j) % n_hp, nxt % n_j), 1 - cur, "start")

    head_a = lax.broadcasted_iota(I32, (ATTN_TQ, LANE), 1) < HEAD_DIM
    row_i = lax.broadcasted_iota(I32, (ATTN_TQ, ATTN_TK), 0)
    col_i = lax.broadcasted_iota(I32, (ATTN_TQ, ATTN_TK), 1)
    band = (col_i - row_i >= 0) & (col_i - row_i <= 2 * HALF_STEPS)
    colpos = lax.broadcasted_iota(I32, (1, ATTN_TK), 1)

    def local_softmax(q, kt, vt, mask, pos):
        ok = mask & (pos >= 0) & (pos < seq)
        valid = jnp.concatenate([ok, ok], axis=0)
        q2 = jnp.concatenate([jnp.where(head_a, q, 0.0), jnp.where(head_a, 0.0, q)], axis=0).astype(BF16)
        s = jnp.where(valid, _dot_nt(q2, kt.astype(BF16)), -jnp.inf)
        m = jnp.max(s, axis=1, keepdims=True)
        p = jnp.exp(s - m)
        l = jnp.sum(p, axis=1, keepdims=True)
        n = _dot(p.astype(BF16), vt.astype(BF16))
        return (jnp.where(head_a, m[:ATTN_TQ], m[ATTN_TQ:]), jnp.where(head_a, l[:ATTN_TQ], l[ATTN_TQ:]),
                jnp.where(head_a, n[:ATTN_TQ], n[ATTN_TQ:]))

    def visit_dense(i):
        qs = pl.multiple_of(i * ATTN_TQ, ATTN_TQ)
        rows = pl.ds(qs, ATTN_TQ)
        keys = pl.ds(qs, ATTN_TK)
        mst[rows, :], lst[rows, :], acc[rows, :] = local_softmax(
            qf[rows, :], kf[keys, :], vf[keys, :], band, p0 + qs - HALF_STEPS + colpos)

    def visit_dmax(r):
        q = q16[cur, r] * (HEAD_DIM ** -0.5)
        m16[r], l16[r], a16[r] = local_softmax(q, k16[cur, r], v16[cur, r], band,
                                               p0 - dmax * HALF_STEPS + r + dmax * colpos)

    dmid = PATTERNS[1][1]
    per = dmax // dmid
    seg_q, seg_k = ATTN_TQ // per, ATTN_TK // per
    q_step = per * (row_i % seg_q) + row_i // seg_q
    k_step = per * (col_i % seg_k - HALF_STEPS // per) + col_i // seg_k
    mask_mid = jnp.abs(k_step - q_step) <= HALF_STEPS
    k_off_mid = dmax * (colpos % seg_k - HALF_STEPS // per) + dmid * (colpos // seg_k)

    def visit_mid(i):
        r = i & (dmid - 1)
        t = i >> (dmid.bit_length() - 1)
        n0 = pl.multiple_of(t * seg_q, seg_q)
        q_rows = pl.ds(n0, seg_q)
        k_rows = pl.ds(pl.multiple_of(n0 + HALF_STEPS - HALF_STEPS // per, 8), seg_k)
        classes = [r + dmid * c for c in range(per)]
        cat = lambda ref, rows, buf=None: jnp.concatenate(
            [ref[c, rows, :] if buf is None else ref[buf, c, rows, :] for c in classes], axis=0)
        m_loc, l_loc, n_loc = local_softmax(cat(q16, q_rows, cur) * (HEAD_DIM ** -0.5), cat(k16, k_rows, cur),
                                            cat(v16, k_rows, cur), mask_mid, p0 + dmax * n0 + r + k_off_mid)
        m_old = cat(m16, q_rows)
        m_new = jnp.maximum(m_old, m_loc)
        w_old = jnp.exp(m_old - m_new)
        w_loc = jnp.exp(m_loc - m_new)
        l_new = cat(l16, q_rows) * w_old + l_loc * w_loc
        a_new = cat(a16, q_rows) * w_old + n_loc * w_loc
        for n, c in enumerate(classes):
            seg = slice(n * seg_q, (n + 1) * seg_q)
            m16[c, q_rows, :], l16[c, q_rows, :], a16[c, q_rows, :] = m_new[seg], l_new[seg], a_new[seg]

    def loop(n, fn):
        def body(i, c):
            for u in range(ATTN_UNROLL):
                fn(i * ATTN_UNROLL + u)
            return c
        lax.fori_loop(0, n // ATTN_UNROLL, body, 0)

    n_visits = ATTN_SUPER // ATTN_TQ
    all_residues((bi, hp, j), cur, "wait")
    loop(dmax, visit_dmax)
    loop(n_visits, visit_mid)
    for r in range(dmax):
        for cp in state_copies(r):
            cp.start()

    qf[...] = q_ref[0] * (HEAD_DIM ** -0.5)
    row0 = 0
    for kr, vr in ((k0, v0), (k1, v1), (k2, v2)):
        n_rows = kr.shape[1]
        kf[row0:row0 + n_rows, :] = kr[0]
        vf[row0:row0 + n_rows, :] = vr[0]
        row0 += n_rows
    loop(n_visits, visit_dense)
    for r in range(dmax):
        for cp in state_copies(r):
            cp.wait()

    def finish(i, c):
        rows = pl.ds(pl.multiple_of(i * ATTN_TQ, ATTN_TQ), ATTN_TQ)
        slabs = pl.ds(pl.multiple_of(i * (ATTN_TQ // dmax), ATTN_TQ // dmax), ATTN_TQ // dmax)
        a_a, m_a, l_a = (t[slabs].reshape(ATTN_TQ, LANE) for t in (an3, mn3, ln3))
        m_b = mst[rows, :]
        m = jnp.maximum(m_a, m_b)
        w_a, w_b = jnp.exp(m_a - m), jnp.exp(m_b - m)
        o = (a_a * w_a + acc[rows, :] * w_b) / (l_a * w_a + lst[rows, :] * w_b)
        o_ref[0, rows, :] = o.astype(BF16)
        return c

    lax.fori_loop(0, n_visits, finish, 0)


def _attention(q, k, v):
    b, seq, _ = q.shape
    nblk = seq // ATTN_HALO
    ratio = ATTN_SUPER // ATTN_HALO
    dmax = PATTERNS[-1][1]
    assert ATTN_SUPER // dmax == ATTN_TQ
    assert len(PATTERNS) == 3 and ATTN_HALO == PATTERNS[0][1] * HALF_STEPS and seq // ATTN_SUPER >= 2

    def halo(after):
        return pl.BlockSpec((1, ATTN_HALO, LANE),
                            lambda bi, hp, j: (bi, jnp.clip(ratio * (j + after) - 1 + after, 0, nblk - 1), hp))

    main = pl.BlockSpec((1, ATTN_SUPER, LANE), lambda bi, hp, j: (bi, j, hp))
    kv_specs = [halo(0), main, halo(1)]
    hbm = pl.BlockSpec(memory_space=pl.ANY)
    by_residue = lambda t: t.reshape(b, seq // dmax, dmax, D_MODEL)
    return pl.pallas_call(
        functools.partial(_attn_kernel, seq=seq),
        grid=(b, D_MODEL // LANE, seq // ATTN_SUPER),
        in_specs=[main] + kv_specs * 2 + [hbm] * 3,
        out_specs=main,
        out_shape=jax.ShapeDtypeStruct((b, seq, D_MODEL), BF16),
        scratch_shapes=[pltpu.VMEM((ATTN_SUPER, LANE), F32), pltpu.VMEM((ATTN_SUPER + 2 * ATTN_HALO, LANE), F32),
                        pltpu.VMEM((ATTN_SUPER + 2 * ATTN_HALO, LANE), F32), pltpu.VMEM((2, dmax, ATTN_TQ, LANE), F32),
                        pltpu.VMEM((2, dmax, ATTN_TK, LANE), F32), pltpu.VMEM((2, dmax, ATTN_TK, LANE), F32)]
                       + [pltpu.VMEM((ATTN_SUPER, LANE), F32)] * 3
                       + [pltpu.VMEM((dmax, ATTN_TQ, LANE), F32)] * 3
                       + [pltpu.VMEM((ATTN_TQ, dmax, LANE), F32)] * 3
                       + [pltpu.SemaphoreType.DMA((3, dmax)), pltpu.SemaphoreType.DMA((2, 3, dmax))],
        compiler_params=_cparams(3), name="dilated_attn",
    )(q, k, k, k, v, v, v, by_residue(q), by_residue(k), by_residue(v))


def _shifted(cur, prev8, next8, s, first, last):
    n = cur.shape[0]
    if s == 0:
        return cur
    rolled = pltpu.roll(cur, (-s) % n, 0)
    if s < 0:
        top = jnp.concatenate([jnp.where(first, 0.0, prev8), cur[0:8]], axis=0)[8 + s:16 + s]
        return jnp.concatenate([top, rolled[8:]], axis=0)
    bottom = jnp.concatenate([cur[n - 8:n], jnp.where(last, 0.0, next8)], axis=0)[s:8 + s]
    return jnp.concatenate([rolled[:n - 8], bottom], axis=0)


def _ssd_conv_kernel(c_ref, p_ref, n_ref, w_ref, b_ref, o_ref):
    first = pl.program_id(1) == 0
    last = pl.program_id(1) == pl.num_programs(1) - 1
    for c in range(SSD_XBC // 256):
        sl = slice(c * 256, (c + 1) * 256)
        cur, prev8, next8 = c_ref[0, :, sl], p_ref[0, :, sl], n_ref[0, :, sl]
        y = b_ref[:, sl]
        for j in range(SSD_CONV):
            y = y + _shifted(cur, prev8, next8, j - SSD_CONV // 2, first, last) * w_ref[j:j + 1, sl]
        o_ref[0, :, sl] = _silu(y)


def _halo_specs(tm, width, seq):
    cur = pl.BlockSpec((1, tm, width), lambda b, i: (b, i, 0))
    prev = pl.BlockSpec((1, 8, width), lambda b, i: (b, jnp.maximum(i * (tm // 8) - 1, 0), 0))
    nxt = pl.BlockSpec((1, 8, width), lambda b, i: (b, jnp.minimum((i + 1) * (tm // 8), seq // 8 - 1), 0))
    return cur, prev, nxt


def _ssd_conv(xbc, w, bias):
    b, seq, width = xbc.shape
    tm = 512
    cur, prev, nxt = _halo_specs(tm, width, seq)
    return pl.pallas_call(
        _ssd_conv_kernel, grid=(b, seq // tm),
        in_specs=[cur, prev, nxt, pl.BlockSpec((8, width), lambda b, i: (0, 0)),
                  pl.BlockSpec((1, width), lambda b, i: (0, 0))],
        out_specs=cur, out_shape=jax.ShapeDtypeStruct(xbc.shape, F32),
        compiler_params=_cparams(2), name="ssd_conv",
    )(xbc, xbc, xbc, w, bias)


def _softplus(x):
    return jnp.maximum(x, 0.0) + jnp.log1p(jnp.exp(-jnp.abs(x)))


def _tri(kind):
    s = lax.broadcasted_iota(I32, (CHUNK, CHUNK), 0)
    l = lax.broadcasted_iota(I32, (CHUNK, CHUNK), 1)
    return {"le": s <= l, "ge": s >= l, "lt": s < l}[kind]


def _expand(cols, e2_ref):
    hi, lo = _split2(cols)
    return _dot(jnp.concatenate([hi, lo], axis=1), e2_ref[...])


def _ssd_bwd_kernel(xs_ref, b_ref, dt_ref, pc_ref, e2_ref, sb_ref, st):
    @pl.when(pl.program_id(1) == 0)
    def _():
        st[...] = jnp.zeros_like(st)

    for c in reversed(range(SSD_CPS)):
        _ssd_bwd_chunk(slice(c * CHUNK, (c + 1) * CHUNK), c, xs_ref, b_ref, dt_ref, pc_ref, e2_ref, sb_ref, st)


def _ssd_bwd_chunk(rs, c, xs_ref, b_ref, dt_ref, pc_ref, e2_ref, sb_ref, st):
    sb_ref[0, c] = st[...].astype(BF16)
    dt_t = dt_ref[0, rs, :].T
    dtb = _softplus(dt_t[N_HEADS:2 * N_HEADS, :] + pc_ref[:, 1:2])
    a = dtb * pc_ref[:, 3:4]
    ex = _dot3(a, _tri("lt").astype(BF16))
    tot = ex[:, CHUNK - 1:CHUNK] + a[:, CHUNK - 1:CHUNK]
    rowform = jnp.concatenate([dtb * jnp.exp(ex), jnp.broadcast_to(jnp.exp(tot), (N_HEADS, CHUNK)),
                               jnp.zeros((CHUNK - 2 * N_HEADS, CHUNK), F32)], axis=0)
    ex2 = _expand(rowform.T, e2_ref)
    xw = (xs_ref[0, rs, :] * ex2[:, :D_MODEL]).astype(BF16)
    half = D_MODEL // SSD_GROUPS
    upd = [_dot(b_ref[0, rs, g * SSD_STATE:(g + 1) * SSD_STATE].T.astype(BF16), xw[:, g * half:(g + 1) * half])
           for g in range(SSD_GROUPS)]
    st[...] = st[...] * ex2[0:1, D_MODEL:] + jnp.concatenate(upd, axis=1)


def _ssd_fwd_kernel(xs_ref, b_ref, c_ref, dt_ref, z_ref, sb_ref, pc_ref, e3_ref, dexp_ref, on_ref, o_ref, st):
    @pl.when(pl.program_id(1) == 0)
    def _():
        st[...] = jnp.zeros_like(st)

    for c in range(SSD_CPS):
        _ssd_fwd_chunk(slice(c * CHUNK, (c + 1) * CHUNK), c, xs_ref, b_ref, c_ref, dt_ref, z_ref, sb_ref, pc_ref,
                       e3_ref, dexp_ref, on_ref, o_ref, st)


def _ssd_fwd_chunk(rs, c, xs_ref, b_ref, c_ref, dt_ref, z_ref, sb_ref, pc_ref, e3_ref, dexp_ref, on_ref, o_ref, st):
    xs = xs_ref[0, rs, :]
    dt_t = dt_ref[0, rs, :].T
    dtf = _softplus(dt_t[0:N_HEADS, :] + pc_ref[:, 0:1])
    dtb = _softplus(dt_t[N_HEADS:2 * N_HEADS, :] + pc_ref[:, 1:2])
    af = dtf * pc_ref[:, 2:3]
    ab = dtb * pc_ref[:, 3:4]
    csf = _dot3(af, _tri("le").astype(BF16))
    rcs = _dot3(ab, _tri("ge").astype(BF16))
    totf = csf[:, CHUNK - 1:CHUNK]
    rowform = jnp.concatenate([dtf * jnp.exp(totf - csf), jnp.exp(csf), jnp.exp(rcs), csf, rcs,
                               jnp.zeros((CHUNK - 5 * N_HEADS, CHUNK), F32)], axis=0)
    cols = rowform.T
    ex3 = _expand(cols, e3_ref)
    w_state, e_f, e_b = ex3[:, :D_MODEL], ex3[:, D_MODEL:2 * D_MODEL], ex3[:, 2 * D_MODEL:]

    xb = xs.astype(BF16)
    lower, upper = _tri("ge"), _tri("le")
    head_a = lax.broadcasted_iota(I32, (CHUNK, LANE), 1) < HEAD_DIM
    half = D_MODEL // SSD_GROUPS
    hpg = N_HEADS // SSD_GROUPS
    st_all = st[...]
    sb_all = sb_ref[0, c]
    ys = []
    b_t = []
    for g in range(SSD_GROUPS):
        bg = b_ref[0, rs, g * SSD_STATE:(g + 1) * SSD_STATE]
        cg = c_ref[0, rs, g * SSD_STATE:(g + 1) * SSD_STATE].astype(BF16)
        b_t.append(bg.T.astype(BF16))
        gm = _dot_nt(cg, bg.astype(BF16))
        states = jnp.concatenate([st_all[:, g * half:(g + 1) * half].astype(BF16),
                                  sb_all[:, g * half:(g + 1) * half]], axis=1)
        off = _dot(cg, states)
        y_off = (off[:, :half] * e_f[:, g * half:(g + 1) * half]
                 + off[:, half:] * e_b[:, g * half:(g + 1) * half])
        for pair in range(hpg // 2):
            ms = []
            for h in (g * hpg + 2 * pair, g * hpg + 2 * pair + 1):
                dec_f = jnp.where(lower, jnp.exp(cols[:, 3 * N_HEADS + h:3 * N_HEADS + h + 1] - csf[h:h + 1, :]), 0.0)
                dec_b = jnp.where(upper, jnp.exp(cols[:, 4 * N_HEADS + h:4 * N_HEADS + h + 1] - rcs[h:h + 1, :]), 0.0)
                ms.append((gm * (dec_f * dtf[h:h + 1, :] + dec_b * dtb[h:h + 1, :])).astype(BF16))
            lo = g * half + pair * LANE
            xp = xb[:, lo:lo + LANE]
            ys.append(jnp.where(head_a, _dot(ms[0], xp), _dot(ms[1], xp))
                      + y_off[:, pair * LANE:(pair + 1) * LANE])
    y = jnp.concatenate(ys, axis=1) + xs * dexp_ref[...]
    yz = y * _silu(z_ref[0, rs, :])
    o_ref[0, rs, :] = _rms(yz, on_ref[...]).astype(BF16)

    xw = (xs * w_state).astype(BF16)
    upd = [_dot(b_t[g], xw[:, g * half:(g + 1) * half]) for g in range(SSD_GROUPS)]
    st[...] = st_all * e_f[CHUNK - 1:CHUNK, :] + jnp.concatenate(upd, axis=1)


def _ssd(xbc_act, dt, z, pc, dexp, out_norm):
    b, seq, _ = xbc_act.shape
    nc = seq // CHUNK
    sel = np.zeros((2 * CHUNK, 3 * D_MODEL), np.float32)
    for part in range(3):
        for h in range(N_HEADS):
            for rep in range(2):
                sel[rep * CHUNK + part * N_HEADS + h, part * D_MODEL + h * HEAD_DIM:part * D_MODEL + (h + 1) * HEAD_DIM] = 1.0
    e3 = jnp.asarray(sel, BF16)
    e2 = jnp.asarray(sel[:, :2 * D_MODEL], BF16)

    rows = SSD_CPS * CHUNK
    nblk = seq // rows
    bcol = D_MODEL // (2 * SSD_STATE)
    rev = lambda bi, c: (bi, nblk - 1 - c, 0)
    sb = pl.pallas_call(
        _ssd_bwd_kernel, grid=(b, nblk),
        in_specs=[pl.BlockSpec((1, rows, D_MODEL), rev),
                  pl.BlockSpec((1, rows, 2 * SSD_STATE), lambda bi, c: (bi, nblk - 1 - c, bcol)),
                  pl.BlockSpec((1, rows, LANE), rev), _const_spec(pc.shape), _const_spec(e2.shape)],
        out_specs=pl.BlockSpec((1, SSD_CPS, SSD_STATE, D_MODEL), lambda bi, c: (bi, nblk - 1 - c, 0, 0)),
        out_shape=jax.ShapeDtypeStruct((b, nc, SSD_STATE, D_MODEL), BF16),
        scratch_shapes=[pltpu.VMEM((SSD_STATE, D_MODEL), F32)],
        compiler_params=_cparams(2), name="ssd_bwd_state",
    )(xbc_act, xbc_act, dt, pc, e2)

    fwd = lambda bi, c: (bi, c, 0)
    return pl.pallas_call(
        _ssd_fwd_kernel, grid=(b, nblk),
        in_specs=[pl.BlockSpec((1, rows, D_MODEL), fwd),
                  pl.BlockSpec((1, rows, 2 * SSD_STATE), lambda bi, c: (bi, c, bcol)),
                  pl.BlockSpec((1, rows, 2 * SSD_STATE), lambda bi, c: (bi, c, bcol + 1)),
                  pl.BlockSpec((1, rows, LANE), fwd), pl.BlockSpec((1, rows, D_MODEL), fwd),
                  pl.BlockSpec((1, SSD_CPS, SSD_STATE, D_MODEL), lambda bi, c: (bi, c, 0, 0)),
                  _const_spec(pc.shape), _const_spec(e3.shape), _const_spec((1, D_MODEL)), _const_spec((1, D_MODEL))],
        out_specs=pl.BlockSpec((1, rows, D_MODEL), fwd),
        out_shape=jax.ShapeDtypeStruct((b, seq, D_MODEL), BF16),
        scratch_shapes=[pltpu.VMEM((SSD_STATE, D_MODEL), F32)],
        compiler_params=_cparams(2), name="ssd_fwd",
    )(xbc_act, xbc_act, xbc_act, dt, z, sb, pc, e3, dexp, out_norm)


def _norm_and_route(x1, fg_ref, wr_hi_ref, wr_lo_ref, hn_ref, aff_ref):
    hn = _rms(x1, fg_ref[...])
    hi, lo = _split2(hn)
    hn_ref[...] = hi
    logits = _dot(hi, wr_hi_ref[...]) + _dot(lo, wr_hi_ref[...]) + _dot(hi, wr_lo_ref[...])
    lt = logits.T[0:N_EXPERTS, :]
    e = jnp.exp(lt - jnp.max(lt, axis=0, keepdims=True))
    aff_ref[0] = e / jnp.sum(e, axis=0, keepdims=True)


def _out_even_kernel(x_ref, a_ref, y_ref, wa_ref, wy_ref, fg_ref, wr_hi_ref, wr_lo_ref, x1_ref, hn_ref, aff_ref):
    x1 = x_ref[...] + _dot(a_ref[...], wa_ref[...]) + _dot(y_ref[...], wy_ref[...])
    x1_ref[...] = x1
    _norm_and_route(x1, fg_ref, wr_hi_ref, wr_lo_ref, hn_ref, aff_ref)


def _out_odd_kernel(x_ref, gb_ref, cu_ref, cp_ref, cn_ref, cw_ref, wo_ref, fg_ref, wr_hi_ref, wr_lo_ref,
                    x1_ref, hn_ref, aff_ref, *, tiles_per_seq):
    i = pl.program_id(0) % tiles_per_seq
    first, last = i == 0, i == tiles_per_seq - 1
    cur, prev8, next8 = cu_ref[...], cp_ref[...], cn_ref[...]
    conv = sum(_shifted(cur, prev8, next8, j - SHORT_CONV // 2, first, last) * cw_ref[j:j + 1, :]
               for j in range(SHORT_CONV))
    x1 = x_ref[...] + _dot((gb_ref[...] * conv).astype(BF16), wo_ref[...])
    x1_ref[...] = x1
    _norm_and_route(x1, fg_ref, wr_hi_ref, wr_lo_ref, hn_ref, aff_ref)


def _route_outs(n, b, seq, tm):
    row = pl.BlockSpec((tm, D_MODEL), lambda i: (i, 0))
    aff = pl.BlockSpec((1, N_EXPERTS, tm), lambda i: (i // (seq // tm), 0, i % (seq // tm)))
    shapes = [jax.ShapeDtypeStruct((n, D_MODEL), F32), jax.ShapeDtypeStruct((n, D_MODEL), BF16),
              jax.ShapeDtypeStruct((b, N_EXPERTS, seq), F32)]
    return [row, row, aff], shapes


def _out_even(x, attn, y, wa, wy, fg, wr_hi, wr_lo, b, seq):
    n = x.shape[0]
    tm = OUT_ROW_TILE
    row = pl.BlockSpec((tm, D_MODEL), lambda i: (i, 0))
    out_specs, shapes = _route_outs(n, b, seq, tm)
    return pl.pallas_call(
        _out_even_kernel, grid=(n // tm,),
        in_specs=[row, row, row, _const_spec(wa.shape), _const_spec(wy.shape), _const_spec((1, D_MODEL)),
                  _const_spec(wr_hi.shape), _const_spec(wr_lo.shape)],
        out_specs=out_specs, out_shape=shapes, compiler_params=_cparams(1), name="out_even",
    )(x, attn, y, wa, wy, fg, wr_hi, wr_lo)


def _out_odd(x, gb, cu, cw, wo, fg, wr_hi, wr_lo, b, seq):
    n = x.shape[0]
    tm = OUT_ROW_TILE
    row = pl.BlockSpec((tm, D_MODEL), lambda i: (i, 0))
    prev = pl.BlockSpec((8, D_MODEL), lambda i: (jnp.maximum(i * (tm // 8) - 1, 0), 0))
    nxt = pl.BlockSpec((8, D_MODEL), lambda i: (jnp.minimum((i + 1) * (tm // 8), n // 8 - 1), 0))
    out_specs, shapes = _route_outs(n, b, seq, tm)
    return pl.pallas_call(
        functools.partial(_out_odd_kernel, tiles_per_seq=seq // tm), grid=(n // tm,),
        in_specs=[row, row, row, prev, nxt, pl.BlockSpec((8, D_MODEL), lambda i: (0, 0)), _const_spec(wo.shape),
                  _const_spec((1, D_MODEL)), _const_spec(wr_hi.shape), _const_spec(wr_lo.shape)],
        out_specs=out_specs, out_shape=shapes, compiler_params=_cparams(1), name="out_odd",
    )(x, gb, cu, cu, cu, cw, wo, fg, wr_hi, wr_lo)


def _in_odd_kernel(x_ref, g_ref, wb_ref, wc_ref, wu_ref, gb_ref, cu_ref):
    hb = _rms(x_ref[...], g_ref[...]).astype(BF16)
    gb_ref[...] = _dot(hb, wb_ref[...])
    cu_ref[...] = _dot(hb, wc_ref[...]) * _dot(hb, wu_ref[...])


def _in_odd(x, g, wb, wc, wu):
    n = x.shape[0]
    tm = ROW_TILE
    row = pl.BlockSpec((tm, D_MODEL), lambda i: (i, 0))
    return pl.pallas_call(
        _in_odd_kernel, grid=(n // tm,),
        in_specs=[row, _const_spec((1, D_MODEL)), _const_spec(wb.shape), _const_spec(wc.shape), _const_spec(wu.shape)],
        out_specs=[row, row], out_shape=[jax.ShapeDtypeStruct((n, D_MODEL), F32)] * 2,
        compiler_params=_cparams(1), name="in_odd",
    )(x, g, wb, wc, wu)


def _count(mask):
    return jnp.sum(jnp.sum(mask.astype(F32), axis=0, keepdims=True), axis=1, keepdims=True)


def _route_kernel(aff_ref, incl_ref, ones_ref, strict_ref, local_ref, group_ref, first_ref, slot_ref, off_ref, end_ref,
                  *, cap):
    def step(i, thrs):
        bit = jnp.int32(1) << (30 - i)
        out = []
        for e in range(N_EXPERTS):
            cand = thrs[e] | bit
            out.append(jnp.where(_count(pltpu.bitcast(aff_ref[0, e], I32) >= cand) >= cap, cand, thrs[e]))
        return tuple(out)

    thrs = lax.fori_loop(0, 31, step, tuple(jnp.zeros((1, 1), I32) for _ in range(N_EXPERTS)))
    for e in range(N_EXPERTS):
        bits = pltpu.bitcast(aff_ref[0, e], I32)
        thr = thrs[e]
        gt = bits > thr
        eq = (bits == thr).astype(BF16)
        eq_rank = _dot(eq, incl_ref[...]) + _dot(strict_ref[...], _dot(eq, ones_ref[...]).astype(BF16))
        sel = (gt | ((bits == thr) & (eq_rank <= cap - _count(gt)))).astype(BF16)
        within = _dot(sel, incl_ref[...])
        totals = _dot(sel, ones_ref[...]).astype(BF16)
        local = _dot(local_ref[...], totals)
        cnt = _dot(group_ref[...], totals)
        padded = jnp.floor((cnt + (SLOT_ALIGN - 1)) * (1.0 / SLOT_ALIGN)) * SLOT_ALIGN
        start = _dot(first_ref[...], padded.astype(BF16))
        slot_ref[0, e] = jnp.where(sel > 0, (start + local + within).astype(I32) - 1, -1)
        off_ref[0, e:e + 1, :] = start.T[0:1, :].astype(I32)
        end_ref[0, e:e + 1, :] = (start + padded).T[0:1, :].astype(I32)


def _route(aff, cap):
    b, _, seq = aff.shape
    nt = seq // LANE
    tri = np.arange(LANE)
    tt = np.arange(nt)
    grp = tt // SUB
    as_bf16 = lambda m: jnp.asarray(m, BF16)
    incl = as_bf16(tri[:, None] <= tri[None, :])
    strict = as_bf16(tt[None, :] < tt[:, None])
    local = as_bf16((tt[None, :] < tt[:, None]) & (grp[None, :] == grp[:, None]))
    group = as_bf16(grp[None, :] == grp[:, None])
    first = as_bf16((grp[None, :] < grp[:, None]) & (tt[None, :] % SUB == 0))
    ones = jnp.ones((LANE, LANE), BF16)
    tiles = pl.BlockSpec((1, N_EXPERTS, nt, LANE), lambda i: (i, 0, 0, 0))
    rows = pl.BlockSpec((1, N_EXPERTS, nt), lambda i: (i, 0, 0))
    return pl.pallas_call(
        functools.partial(_route_kernel, cap=cap), grid=(b,),
        in_specs=[tiles, _const_spec((LANE, LANE)), _const_spec((LANE, LANE))] + [_const_spec((nt, nt))] * 4,
        out_specs=[tiles, rows, rows],
        out_shape=[jax.ShapeDtypeStruct((b, N_EXPERTS, nt, LANE), I32)] + [jax.ShapeDtypeStruct((b, N_EXPERTS, nt), I32)] * 2,
        compiler_params=_cparams(1), name="route",
    )(aff.reshape(b, N_EXPERTS, nt, LANE), incl, ones, strict, local, group, first)


def _one_hot(slots, base):
    return (slots == lax.broadcasted_iota(I32, (WINDOW, slots.shape[1]), 0) + base).astype(BF16)


def _gather_kernel(off_ref, end_ref, slot_ref, aff_ref, hn_ref, xe_hbm, stage, extra, sem, xsem):
    bi, j = pl.program_id(0), pl.program_id(1)
    n_j = pl.num_programs(1)
    step = bi * n_j + j
    cur = step % 2

    def window_copy(bb, jj, e, buf):
        start = pl.multiple_of(off_ref[bb, e, jj * SUB], SLOT_ALIGN)
        return pltpu.make_async_copy(stage.at[buf, e], xe_hbm.at[bb, e, pl.ds(start, WINDOW), :], sem.at[buf, e])

    tokens = hn_ref[0]
    parts = [p.astype(F32) for p in _split3(aff_ref[0, 0])]
    gates = jnp.concatenate(parts + [jnp.zeros((LANE - 3 * N_EXPERTS, TOK_TILE), F32)], axis=0).astype(BF16)

    def rows_of(p):
        return jnp.concatenate([_dot(p, tokens), _dot_nt(p, gates)], axis=1).astype(BF16)

    p_all = jnp.concatenate([_one_hot(slot_ref[0, 0, e:e + 1, :], off_ref[bi, e, j * SUB])
                             for e in range(N_EXPERTS)], axis=0)
    stage[cur] = rows_of(p_all).reshape(N_EXPERTS, WINDOW, D_MODEL + LANE)

    @pl.when(j > 0)
    def _():
        for e in range(N_EXPERTS):
            window_copy(bi, j - 1, e, 1 - cur).wait()

    for e in range(N_EXPERTS):
        window_copy(bi, j, e, cur).start()

    def overflow(e, c):
        first = off_ref[bi, e, j * SUB]
        n_win = (end_ref[bi, e, j * SUB] - first + WINDOW - 1) // WINDOW
        slots = slot_ref[0, 0, pl.ds(e, 1), :]

        def one(w, c2):
            base = pl.multiple_of(first + w * WINDOW, SLOT_ALIGN)
            extra[...] = rows_of(_one_hot(slots, base))
            cp = pltpu.make_async_copy(extra, xe_hbm.at[bi, e, pl.ds(base, WINDOW), :], xsem.at[0])
            cp.start()
            cp.wait()
            return c2

        return lax.fori_loop(1, n_win, one, c)

    lax.fori_loop(0, N_EXPERTS, overflow, 0)

    @pl.when(j == n_j - 1)
    def _():
        for e in range(N_EXPERTS):
            window_copy(bi, j, e, cur).wait()
        extra[...] = jnp.zeros_like(extra)
        cap_pad = xe_hbm.shape[2]

        def fill(e, c):
            used = end_ref[bi, e, end_ref.shape[2] - 1]
            n_big = (cap_pad - used) // WINDOW
            small0 = used + n_big * WINDOW
            n_small = (cap_pad - small0) // SLOT_ALIGN
            big = lambda i: pltpu.make_async_copy(
                extra, xe_hbm.at[bi, e, pl.ds(pl.multiple_of(used + i * WINDOW, SLOT_ALIGN), WINDOW), :], xsem.at[0])
            small = lambda i: pltpu.make_async_copy(
                extra.at[0:SLOT_ALIGN],
                xe_hbm.at[bi, e, pl.ds(pl.multiple_of(small0 + i * SLOT_ALIGN, SLOT_ALIGN), SLOT_ALIGN), :], xsem.at[0])
            for n, mk in ((n_big, big), (n_small, small)):
                lax.fori_loop(0, n, lambda i, c2, mk=mk: (mk(i).start(), c2)[1], 0)
            for n, mk in ((n_big, big), (n_small, small)):
                lax.fori_loop(0, n, lambda i, c2, mk=mk: (mk(i).wait(), c2)[1], 0)
            return c

        lax.fori_loop(0, N_EXPERTS, fill, 0)


def _gather(off, end, slot_t, aff_t, hn):
    b, seq, _ = hn.shape
    cap_pad = _cap_pad(seq)
    width = D_MODEL + LANE
    per_tile = pl.BlockSpec((1, 1, N_EXPERTS, TOK_TILE), lambda bi, j, *_: (bi, j, 0, 0))
    return pl.pallas_call(
        _gather_kernel,
        grid_spec=pltpu.PrefetchScalarGridSpec(
            num_scalar_prefetch=2, grid=(b, seq // TOK_TILE),
            in_specs=[per_tile, per_tile, pl.BlockSpec((1, TOK_TILE, D_MODEL), lambda bi, j, *_: (bi, j, 0))],
            out_specs=pl.BlockSpec(memory_space=pl.ANY),
            scratch_shapes=[pltpu.VMEM((2, N_EXPERTS, WINDOW, width), BF16), pltpu.VMEM((WINDOW, width), BF16),
                            pltpu.SemaphoreType.DMA((2, N_EXPERTS)), pltpu.SemaphoreType.DMA((1,))]),
        out_shape=jax.ShapeDtypeStruct((b, N_EXPERTS, cap_pad, width), BF16),
        compiler_params=_cparams(2), name="moe_gather",
    )(off, end, slot_t, aff_t, hn)


def _ffn_kernel(end_ref, xe_ref, wg_hbm, wu_hbm, wd_hbm, y_ref, stage_g, stage_u, stage_d, wg, wu, wd, sem, *, layer):
    e, bi, r = pl.program_id(0), pl.program_id(1), pl.program_id(2)
    used = end_ref[bi, e, end_ref.shape[2] - 1]
    pairs = ((wg_hbm, stage_g, wg), (wu_hbm, stage_u, wu), (wd_hbm, stage_d, wd))

    def weight_copies(ee):
        return [pltpu.make_async_copy(src.at[layer, ee], stg, sem.at[k]) for k, (src, stg, _) in enumerate(pairs)]

    @pl.when((bi == 0) & (r == 0))
    def _():
        @pl.when(e == 0)
        def _():
            for cp in weight_copies(e):
                cp.start()

        for cp, (_, stg, dst) in zip(weight_copies(e), pairs):
            cp.wait()
            n_rows = stg.shape[0]

            def cast(i, c, stg=stg, dst=dst):
                rows = pl.ds(pl.multiple_of(i * CAST_ROWS, CAST_ROWS), CAST_ROWS)
                dst[rows, :] = stg[rows, :].astype(BF16)
                return c

            lax.fori_loop(0, n_rows // CAST_ROWS, cast, 0)

        @pl.when(e + 1 < pl.num_programs(0))
        def _():
            for cp in weight_copies(e + 1):
                cp.start()

    @pl.when(r * FFN_ROWS < used)
    def _():
        xe = xe_ref[0, 0, :, 0:D_MODEL]
        hid = (_silu(_dot(xe, wg[...])) * _dot(xe, wu[...])).astype(BF16)
        g = xe_ref[0, 0, :, D_MODEL:].astype(F32)
        lane = lax.broadcasted_iota(I32, g.shape, 1)
        mine = (lane % N_EXPERTS == e) & (lane < 3 * N_EXPERTS)
        gate = jnp.sum(jnp.where(mine, g, 0.0), axis=1, keepdims=True)
        y_ref[0, 0] = (_dot(hid, wd[...]) * gate).astype(BF16)

    @pl.when(r * FFN_ROWS >= used)
    def _():
        y_ref[...] = jnp.zeros_like(y_ref)


def _ffn(end, xe, wg, wu, wd, layer):
    b, ne, cap_pad, width = xe.shape
    rows = lambda w: pl.BlockSpec((1, 1, FFN_ROWS, w), lambda e, bi, r, *_: (bi, e, r, 0))
    hbm = pl.BlockSpec(memory_space=pl.ANY)
    return pl.pallas_call(
        functools.partial(_ffn_kernel, layer=layer),
        grid_spec=pltpu.PrefetchScalarGridSpec(
            num_scalar_prefetch=1, grid=(ne, b, cap_pad // FFN_ROWS),
            in_specs=[rows(width), hbm, hbm, hbm],
            out_specs=rows(D_MODEL),
            scratch_shapes=[pltpu.VMEM((D_MODEL, D_FF), F32), pltpu.VMEM((D_MODEL, D_FF), F32),
                            pltpu.VMEM((D_FF, D_MODEL), F32), pltpu.VMEM((D_MODEL, D_FF), BF16),
                            pltpu.VMEM((D_MODEL, D_FF), BF16), pltpu.VMEM((D_FF, D_MODEL), BF16),
                            pltpu.SemaphoreType.DMA((3,))]),
        out_shape=jax.ShapeDtypeStruct((b, ne, cap_pad, D_MODEL), BF16),
        compiler_params=_cparams(3), name="moe_ffn",
    )(end, xe, wg, wu, wd)


def _combine_kernel(off_ref, end_ref, slot_ref, y_hbm, x1_ref, o_ref, win, extra, sem, xsem):
    bi, j = pl.program_id(0), pl.program_id(1)
    n_j = pl.num_programs(1)
    step = bi * n_j + j
    cur = step % 2

    def window_copy(bb, jj, e, buf):
        start = pl.multiple_of(off_ref[bb, e, jj * SUB], SLOT_ALIGN)
        return pltpu.make_async_copy(y_hbm.at[bb, e, pl.ds(start, WINDOW), :], win.at[buf, e], sem.at[buf, e])

    @pl.when(step == 0)
    def _():
        for e in range(N_EXPERTS):
            window_copy(bi, j, e, cur).start()

    @pl.when(step + 1 < pl.num_programs(0) * n_j)
    def _():
        nxt = step + 1
        for e in range(N_EXPERTS):
            window_copy(nxt // n_j, nxt % n_j, e, 1 - cur).start()

    ps = []
    for e in range(N_EXPERTS):
        window_copy(bi, j, e, cur).wait()
        ps.append(_one_hot(slot_ref[0, 0, e:e + 1, :], off_ref[bi, e, j * SUB]))
    p_all = jnp.concatenate(ps, axis=0)
    y_all = win[cur].reshape(N_EXPERTS * WINDOW, D_MODEL)
    o_ref[0] = x1_ref[0] + _dot_tn(p_all, y_all)

    def overflow(e, c):
        first = off_ref[bi, e, j * SUB]
        n_win = (end_ref[bi, e, j * SUB] - first + WINDOW - 1) // WINDOW
        slots = slot_ref[0, 0, pl.ds(e, 1), :]

        def one(w, c2):
            base = pl.multiple_of(first + w * WINDOW, SLOT_ALIGN)
            cp = pltpu.make_async_copy(y_hbm.at[bi, e, pl.ds(base, WINDOW), :], extra, xsem.at[0])
            cp.start()
            cp.wait()
            o_ref[0] += _dot_tn(_one_hot(slots, base), extra[...])
            return c2

        return lax.fori_loop(1, n_win, one, c)

    lax.fori_loop(0, N_EXPERTS, overflow, 0)


def _combine(off, end, slot_t, y, x1):
    b, seq, _ = x1.shape
    tile = pl.BlockSpec((1, TOK_TILE, D_MODEL), lambda bi, j, *_: (bi, j, 0))
    return pl.pallas_call(
        _combine_kernel,
        grid_spec=pltpu.PrefetchScalarGridSpec(
            num_scalar_prefetch=2, grid=(b, seq // TOK_TILE),
            in_specs=[pl.BlockSpec((1, 1, N_EXPERTS, TOK_TILE), lambda bi, j, *_: (bi, j, 0, 0)),
                      pl.BlockSpec(memory_space=pl.ANY), tile],
            out_specs=tile,
            scratch_shapes=[pltpu.VMEM((2, N_EXPERTS, WINDOW, D_MODEL), BF16), pltpu.VMEM((WINDOW, D_MODEL), BF16),
                            pltpu.SemaphoreType.DMA((2, N_EXPERTS)), pltpu.SemaphoreType.DMA((1,))]),
        out_shape=jax.ShapeDtypeStruct((b, seq, D_MODEL), F32),
        compiler_params=_cparams(2), name="moe_combine",
    )(off, end, slot_t, y, x1)


def _cap_pad(seq):
    cap = CAPACITY_FACTOR * seq // N_EXPERTS
    worst = cap + (seq // TOK_TILE) * (SLOT_ALIGN - 1) + WINDOW
    return -(-worst // FFN_ROWS) * FFN_ROWS


def _moe(x1, hn, aff, wg, wu, wd, layer):
    b, seq, _ = x1.shape
    cap = CAPACITY_FACTOR * seq // N_EXPERTS
    slot, off, end = _route(aff, cap)
    per_tile = lambda a: jnp.swapaxes(a.reshape(b, N_EXPERTS, seq // TOK_TILE, TOK_TILE), 1, 2)
    slot_t = per_tile(slot)
    xe = _gather(off, end, slot_t, per_tile(aff), hn)
    y = _ffn(end, xe, wg, wu, wd, layer)
    return _combine(off, end, slot_t, y, x1)


def _rope_tables(seq):
    inv_freq = ROPE_THETA ** (-jnp.arange(ROPE_HALF, dtype=F32) * 2.0 / (2 * ROPE_HALF))
    ang = jnp.arange(seq, dtype=F32)[:, None] * inv_freq[None, :]
    cos, sin = jnp.cos(ang), jnp.sin(ang)
    z = lambda w: jnp.zeros((seq, w), F32)
    rest = HEAD_DIM - 2 * ROPE_HALF
    rc = jnp.concatenate([cos, cos, jnp.ones((seq, rest), F32)], axis=1)
    r1 = jnp.concatenate([-sin, z(ROPE_HALF + rest)], axis=1)
    r2 = jnp.concatenate([z(ROPE_HALF), sin, z(rest)], axis=1)
    return tuple(jnp.tile(t, (1, LANE // HEAD_DIM)) for t in (rc, r1, r2))


def _router_split(w):
    wp = jnp.pad(w, ((0, 0), (0, LANE - N_EXPERTS)))
    hi = wp.astype(BF16)
    return hi, (wp - hi.astype(F32)).astype(BF16)


def kernel(x, attn_norm, w_in_even, q_norm, k_norm, ssd_conv_w, ssd_conv_b, ssd_a_log_fwd, ssd_a_log_bwd,
           ssd_dt_bias_fwd, ssd_dt_bias_bwd, ssd_d, ssd_out_norm, w_out_even, conv_norm, conv_w_in, conv_w,
           conv_w_out, ffn_norm, router_w, expert_w_gate, expert_w_up, expert_w_down):
    b, seq, _ = x.shape
    n = b * seq
    depth = ffn_norm.shape[0]
    rc, r1, r2 = _rope_tables(seq)
    blk = np.arange(256) // HEAD_DIM
    bd = jnp.asarray((blk[:, None] == blk[None, :]) / HEAD_DIM, BF16)
    row = lambda v: v.reshape(1, -1).astype(F32)

    xf = x.reshape(n, D_MODEL)
    for layer in range(depth):
        i = layer // 2
        wr_hi, wr_lo = _router_split(router_w[layer])
        fg = row(ffn_norm[layer])
        if layer % 2 == 0:
            w = w_in_even[i].astype(BF16)
            o = np.cumsum([0, D_MODEL, D_MODEL, D_MODEL, D_MODEL, SSD_XBC, N_HEADS, N_HEADS])
            wq, wk, wv, wz, wx = (w[:, o[j]:o[j + 1]] for j in range(5))
            wd = jnp.pad(w[:, o[5]:o[7]], ((0, 0), (0, LANE - 2 * N_HEADS)))
            tile_heads = lambda g: row(jnp.tile(g, N_HEADS))
            q, k, v, z, xbc, dt = _in_even(xf, row(attn_norm[i]), wq, wk, wv, wz, wx, wd,
                                           tile_heads(q_norm[i]), tile_heads(k_norm[i]), bd, rc, r1, r2, seq)
            as3 = lambda t: t.reshape(b, seq, -1)
            attn = _attention(as3(q), as3(k), as3(v))
            cw = jnp.pad(ssd_conv_w[i], ((0, 8 - SSD_CONV), (0, 0)))
            act = _ssd_conv(as3(xbc), cw, row(ssd_conv_b[i]))
            pc = jnp.pad(jnp.stack([ssd_dt_bias_fwd[i], ssd_dt_bias_bwd[i], -jnp.exp(ssd_a_log_fwd[i]),
                                    -jnp.exp(ssd_a_log_bwd[i])], axis=1).astype(F32), ((0, 0), (0, LANE - 4)))
            y = _ssd(act, as3(dt), as3(z), pc, row(jnp.repeat(ssd_d[i], HEAD_DIM)), row(ssd_out_norm[i]))
            wo = w_out_even[i].astype(BF16)
            x1, hn, aff = _out_even(xf, attn.reshape(n, D_MODEL), y.reshape(n, D_MODEL), wo[:D_MODEL], wo[D_MODEL:],
                                    fg, wr_hi, wr_lo, b, seq)
        else:
            w = conv_w_in[i].astype(BF16)
            gb, cu = _in_odd(xf, row(conv_norm[i]), w[:, :D_MODEL], w[:, D_MODEL:2 * D_MODEL], w[:, 2 * D_MODEL:])
            cw = jnp.pad(conv_w[i], ((0, 8 - SHORT_CONV), (0, 0)))
            x1, hn, aff = _out_odd(xf, gb, cu, cw, conv_w_out[i].astype(BF16), fg, wr_hi, wr_lo, b, seq)
        xf = _moe(x1.reshape(b, seq, D_MODEL), hn.reshape(b, seq, D_MODEL), aff, expert_w_gate, expert_w_up,
                  expert_w_down, layer).reshape(n, D_MODEL)
    return xf.reshape(b, seq, D_MODEL)
```

```python
import functools
import math

import jax
import jax.numpy as jnp
import numpy as np
from jax import lax
from jax.experimental import pallas as pl
from jax.experimental.pallas import tpu as pltpu

F32, BF16, I32 = jnp.float32, jnp.bfloat16, jnp.int32

D_MODEL = 1024
N_HEADS = 16
HEAD_DIM = 64
ROPE_HALF = 8
ROPE_THETA = 500000.0
PATTERNS = ((128, 1), (512, 4), (2048, 16))
HALF_STEPS = 64
SSD_GROUPS = 2
SSD_STATE = 128
SSD_XBC = 1536
SSD_CONV = 5
CHUNK = 128
SSD_CPS = 8
N_EXPERTS = 16
CAPACITY_FACTOR = 2
D_FF = 2048
SHORT_CONV = 3
EPS = 1e-6

LANE = 128
VMEM_LIMIT = 56 * 1024 * 1024

ROW_TILE = 512
OUT_ROW_TILE = 256
ATTN_SUPER = 2048
ATTN_HALO = 64
ATTN_TQ = 128
ATTN_TK = ATTN_TQ + 2 * HALF_STEPS
ATTN_UNROLL = 16
TOK_TILE = 512
SUB = TOK_TILE // LANE
SLOT_ALIGN = 16
WINDOW = 96
FFN_ROWS = 384
CAST_ROWS = 64


def _cparams(n_axes):
    return pltpu.CompilerParams(dimension_semantics=("arbitrary",) * n_axes, vmem_limit_bytes=VMEM_LIMIT)


def _const_spec(shape):
    nd = len(shape)
    return pl.BlockSpec(shape, lambda *_: (0,) * nd, pipeline_mode=pl.Buffered(1))


def _dot(a, b):
    return jnp.dot(a, b, preferred_element_type=F32)


def _dot_nt(a, b):
    return lax.dot_general(a, b, (((1,), (1,)), ((), ())), preferred_element_type=F32)


def _dot_tn(a, b):
    return lax.dot_general(a, b, (((0,), (0,)), ((), ())), preferred_element_type=F32)


def _split2(x):
    hi = x.astype(BF16)
    lo = (x - hi.astype(F32)).astype(BF16)
    return hi, lo


def _split3(x):
    hi = x.astype(BF16)
    r = x - hi.astype(F32)
    mid = r.astype(BF16)
    lo = (r - mid.astype(F32)).astype(BF16)
    return hi, mid, lo


def _dot3(x, m_bf16):
    hi, mid, lo = _split3(x)
    return _dot(hi, m_bf16) + _dot(mid, m_bf16) + _dot(lo, m_bf16)


def _rms(x, g):
    return x * lax.rsqrt(jnp.mean(x * x, axis=-1, keepdims=True) + EPS) * g


def _silu(x):
    return x * jax.nn.sigmoid(x)


def _in_even_kernel(x_ref, g_ref, wq_ref, wk_ref, wv_ref, wz_ref, wx_ref, wd_ref, qg_ref, kg_ref, bd_ref,
                    rc_ref, r1_ref, r2_ref, q_ref, k_ref, v_ref, z_ref, xbc_ref, dt_ref):
    hb = _rms(x_ref[...], g_ref[...]).astype(BF16)
    bd = bd_ref[...]
    rc, r1, r2 = rc_ref[...], r1_ref[...], r2_ref[...]

    def head_norm_rope(w_ref, gain_ref, out_ref):
        t = _dot(hb, w_ref[...])
        for c in range(D_MODEL // 256):
            tc = t[:, c * 256:(c + 1) * 256]
            sq_hi, sq_lo = _split2(tc * tc)
            ms = _dot(sq_hi, bd) + _dot(sq_lo, bd)
            tn = tc * lax.rsqrt(ms + EPS) * gain_ref[:, c * 256:(c + 1) * 256]
            for hh in range(2):
                u = tn[:, hh * LANE:(hh + 1) * LANE]
                r = u * rc + pltpu.roll(u, LANE - ROPE_HALF, 1) * r1 + pltpu.roll(u, ROPE_HALF, 1) * r2
                out_ref[:, c * 256 + hh * LANE:c * 256 + (hh + 1) * LANE] = r

    head_norm_rope(wq_ref, qg_ref, q_ref)
    head_norm_rope(wk_ref, kg_ref, k_ref)
    v_ref[...] = _dot(hb, wv_ref[...])
    z_ref[...] = _dot(hb, wz_ref[...])
    xbc_ref[...] = _dot(hb, wx_ref[...])
    dt_ref[...] = _dot(hb, wd_ref[...])


def _in_even(x, g, wq, wk, wv, wz, wx, wd, qg, kg, bd, rc, r1, r2, seq):
    n = x.shape[0]
    tm = ROW_TILE
    row = lambda w: pl.BlockSpec((tm, w), lambda i: (i, 0))
    tab = pl.BlockSpec((tm, LANE), lambda i: (i % (seq // tm), 0))
    outs = [jax.ShapeDtypeStruct((n, D_MODEL), F32)] * 4 + [jax.ShapeDtypeStruct((n, SSD_XBC), F32),
                                                           jax.ShapeDtypeStruct((n, LANE), F32)]
    return pl.pallas_call(
        _in_even_kernel, grid=(n // tm,),
        in_specs=[row(D_MODEL), _const_spec((1, D_MODEL)), _const_spec(wq.shape), _const_spec(wk.shape),
                  _const_spec(wv.shape), _const_spec(wz.shape), _const_spec(wx.shape), _const_spec(wd.shape),
                  _const_spec((1, D_MODEL)), _const_spec((1, D_MODEL)), _const_spec((256, 256)), tab, tab, tab],
        out_specs=[row(D_MODEL)] * 4 + [row(SSD_XBC), row(LANE)], out_shape=outs,
        compiler_params=_cparams(1), name="in_even",
    )(x, g, wq, wk, wv, wz, wx, wd, qg, kg, bd, rc, r1, r2)


def _attn_kernel(q_ref, k0, k1, k2, v0, v1, v2, qw_hbm, kw_hbm, vw_hbm, o_ref, qf, kf, vf, q16, k16, v16,
                 acc, mst, lst, a16, m16, l16, an3, mn3, ln3, sem, rsem, *, seq):
    dmax = PATTERNS[-1][1]
    bi, hp, j = pl.program_id(0), pl.program_id(1), pl.program_id(2)
    n_hp, n_j = pl.num_programs(1), pl.num_programs(2)
    p0 = j * ATTN_SUPER
    halo = HALF_STEPS
    step = (bi * n_hp + hp) * n_j + j
    cur = step % 2

    def residue_copies(at, buf, r, where):
        b_, hp_, j_ = at
        lanes = pl.ds(pl.multiple_of(hp_ * LANE, LANE), LANE)
        q0 = pl.multiple_of(j_ * ATTN_TQ, ATTN_TQ)
        n_kv = ATTN_TK - halo if where else ATTN_TK
        src0 = 0 if where < 0 else q0 - halo
        dst0 = halo if where < 0 else 0
        cps = [pltpu.make_async_copy(qw_hbm.at[b_, pl.ds(q0, ATTN_TQ), r, lanes], q16.at[buf, r], rsem.at[buf, 0, r])]
        for n, (src, dst) in enumerate(((kw_hbm, k16), (vw_hbm, v16))):
            cps.append(pltpu.make_async_copy(src.at[b_, pl.ds(src0, n_kv), r, lanes],
                                             dst.at[buf, r, pl.ds(dst0, n_kv), :], rsem.at[buf, 1 + n, r]))
        return cps

    def all_residues(at, buf, action):
        def run(where):
            for r in range(dmax):
                for cp in residue_copies(at, buf, r, where):
                    getattr(cp, action)()
        j_ = at[2]
        pl.when(j_ == 0)(functools.partial(run, -1))
        pl.when((j_ > 0) & (j_ < n_j - 1))(functools.partial(run, 0))
        pl.when(j_ == n_j - 1)(functools.partial(run, 1))

    def state_copies(r):
        return [pltpu.make_async_copy(src.at[r], dst.at[:, r, :], sem.at[n, r])
                for n, (src, dst) in enumerate(((a16, an3), (m16, mn3), (l16, ln3)))]

    @pl.when(step == 0)
    def _():
        k16[...] = jnp.zeros_like(k16)
        v16[...] = jnp.zeros_like(v16)
        all_residues((bi, hp, j), cur, "start")

    @pl.when(step + 1 < pl.num_programs(0) * n_hp * n_j)
    def _():
        nxt = step + 1
        all_residues((nxt // (n_hp * n_j), (nxt // n_j) % n_hp, nxt % n_j), 1 - cur, "start")

    head_a = lax.broadcasted_iota(I32, (ATTN_TQ, LANE), 1) < HEAD_DIM
    row_i = lax.broadcasted_iota(I32, (ATTN_TQ, ATTN_TK), 0)
    col_i = lax.broadcasted_iota(I32, (ATTN_TQ, ATTN_TK), 1)
    band = (col_i - row_i >= 0) & (col_i - row_i <= 2 * HALF_STEPS)
    colpos = lax.broadcasted_iota(I32, (1, ATTN_TK), 1)

    def local_softmax(q, kt, vt, mask, pos):
        ok = mask & (pos >= 0) & (pos < seq)
        valid = jnp.concatenate([ok, ok], axis=0)
        q2 = jnp.concatenate([jnp.where(head_a, q, 0.0), jnp.where(head_a, 0.0, q)], axis=0).astype(BF16)
        s = jnp.where(valid, _dot_nt(q2, kt.astype(BF16)), -jnp.inf)
        m = jnp.max(s, axis=1, keepdims=True)
        p = jnp.exp(s - m)
        l = jnp.sum(p, axis=1, keepdims=True)
        n = _dot(p.astype(BF16), vt.astype(BF16))
        return (jnp.where(head_a, m[:ATTN_TQ], m[ATTN_TQ:]), jnp.where(head_a, l[:ATTN_TQ], l[ATTN_TQ:]),
                jnp.where(head_a, n[:ATTN_TQ], n[ATTN_TQ:]))

    def visit_dense(i):
        qs = pl.multiple_of(i * ATTN_TQ, ATTN_TQ)
        rows = pl.ds(qs, ATTN_TQ)
        keys = pl.ds(qs, ATTN_TK)
        mst[rows, :], lst[rows, :], acc[rows, :] = local_softmax(
            qf[rows, :], kf[keys, :], vf[keys, :], band, p0 + qs - HALF_STEPS + colpos)

    def visit_dmax(r):
        q = q16[cur, r] * (HEAD_DIM ** -0.5)
        m16[r], l16[r], a16[r] = local_softmax(q, k16[cur, r], v16[cur, r], band,
                                               p0 - dmax * HALF_STEPS + r + dmax * colpos)

    dmid = PATTERNS[1][1]
    per = dmax // dmid
    seg_q, seg_k = ATTN_TQ // per, ATTN_TK // per
    q_step = per * (row_i % seg_q) + row_i // seg_q
    k_step = per * (col_i % seg_k - HALF_STEPS // per) + col_i // seg_k
    mask_mid = jnp.abs(k_step - q_step) <= HALF_STEPS
    k_off_mid = dmax * (colpos % seg_k - HALF_STEPS // per) + dmid * (colpos // seg_k)

    def visit_mid(i):
        r = i & (dmid - 1)
        t = i >> (dmid.bit_length() - 1)
        n0 = pl.multiple_of(t * seg_q, seg_q)
        q_rows = pl.ds(n0, seg_q)
        k_rows = pl.ds(pl.multiple_of(n0 + HALF_STEPS - HALF_STEPS // per, 8), seg_k)
        classes = [r + dmid * c for c in range(per)]
        cat = lambda ref, rows, buf=None: jnp.concatenate(
            [ref[c, rows, :] if buf is None else ref[buf, c, rows, :] for c in classes], axis=0)
        m_loc, l_loc, n_loc = local_softmax(cat(q16, q_rows, cur) * (HEAD_DIM ** -0.5), cat(k16, k_rows, cur),
                                            cat(v16, k_rows, cur), mask_mid, p0 + dmax * n0 + r + k_off_mid)
        m_old = cat(m16, q_rows)
        m_new = jnp.maximum(m_old, m_loc)
        w_old = jnp.exp(m_old - m_new)
        w_loc = jnp.exp(m_loc - m_new)
        l_new = cat(l16, q_rows) * w_old + l_loc * w_loc
        a_new = cat(a16, q_rows) * w_old + n_loc * w_loc
        for n, c in enumerate(classes):
            seg = slice(n * seg_q, (n + 1) * seg_q)
            m16[c, q_rows, :], l16[c, q_rows, :], a16[c, q_rows, :] = m_new[seg], l_new[seg], a_new[seg]

    def loop(n, fn):
        def body(i, c):
            for u in range(ATTN_UNROLL):
                fn(i * ATTN_UNROLL + u)
            return c
        lax.fori_loop(0, n // ATTN_UNROLL, body, 0)

    n_visits = ATTN_SUPER // ATTN_TQ
    all_residues((bi, hp, j), cur, "wait")
    loop(dmax, visit_dmax)
    loop(n_visits, visit_mid)
    for r in range(dmax):
        for cp in state_copies(r):
            cp.start()

    qf[...] = q_ref[0] * (HEAD_DIM ** -0.5)
    row0 = 0
    for kr, vr in ((k0, v0), (k1, v1), (k2, v2)):
        n_rows = kr.shape[1]
        kf[row0:row0 + n_rows, :] = kr[0]
        vf[row0:row0 + n_rows, :] = vr[0]
        row0 += n_rows
    loop(n_visits, visit_dense)
    for r in range(dmax):
        for cp in state_copies(r):
            cp.wait()

    def finish(i, c):
        rows = pl.ds(pl.multiple_of(i * ATTN_TQ, ATTN_TQ), ATTN_TQ)
        slabs = pl.ds(pl.multiple_of(i * (ATTN_TQ // dmax), ATTN_TQ // dmax), ATTN_TQ // dmax)
        a_a, m_a, l_a = (t[slabs].reshape(ATTN_TQ, LANE) for t in (an3, mn3, ln3))
        m_b = mst[rows, :]
        m = jnp.maximum(m_a, m_b)
        w_a, w_b = jnp.exp(m_a - m), jnp.exp(m_b - m)
        o = (a_a * w_a + acc[rows, :] * w_b) / (l_a * w_a + lst[rows, :] * w_b)
        o_ref[0, rows, :] = o.astype(BF16)
        return c

    lax.fori_loop(0, n_visits, finish, 0)


def _attention(q, k, v):
    b, seq, _ = q.shape
    nblk = seq // ATTN_HALO
    ratio = ATTN_SUPER // ATTN_HALO
    dmax = PATTERNS[-1][1]
    assert ATTN_SUPER // dmax == ATTN_TQ
    assert len(PATTERNS) == 3 and ATTN_HALO == PATTERNS[0][1] * HALF_STEPS and seq // ATTN_SUPER >= 2

    def halo(after):
        return pl.BlockSpec((1, ATTN_HALO, LANE),
                            lambda bi, hp, j: (bi, jnp.clip(ratio * (j + after) - 1 + after, 0, nblk - 1), hp))

    main = pl.BlockSpec((1, ATTN_SUPER, LANE), lambda bi, hp, j: (bi, j, hp))
    kv_specs = [halo(0), main, halo(1)]
    hbm = pl.BlockSpec(memory_space=pl.ANY)
    by_residue = lambda t: t.reshape(b, seq // dmax, dmax, D_MODEL)
    return pl.pallas_call(
        functools.partial(_attn_kernel, seq=seq),
        grid=(b, D_MODEL // LANE, seq // ATTN_SUPER),
        in_specs=[main] + kv_specs * 2 + [hbm] * 3,
        out_specs=main,
        out_shape=jax.ShapeDtypeStruct((b, seq, D_MODEL), BF16),
        scratch_shapes=[pltpu.VMEM((ATTN_SUPER, LANE), F32), pltpu.VMEM((ATTN_SUPER + 2 * ATTN_HALO, LANE), F32),
                        pltpu.VMEM((ATTN_SUPER + 2 * ATTN_HALO, LANE), F32), pltpu.VMEM((2, dmax, ATTN_TQ, LANE), F32),
                        pltpu.VMEM((2, dmax, ATTN_TK, LANE), F32), pltpu.VMEM((2, dmax, ATTN_TK, LANE), F32)]
                       + [pltpu.VMEM((ATTN_SUPER, LANE), F32)] * 3
                       + [pltpu.VMEM((dmax, ATTN_TQ, LANE), F32)] * 3
                       + [pltpu.VMEM((ATTN_TQ, dmax, LANE), F32)] * 3
                       + [pltpu.SemaphoreType.DMA((3, dmax)), pltpu.SemaphoreType.DMA((2, 3, dmax))],
        compiler_params=_cparams(3), name="dilated_attn",
    )(q, k, k, k, v, v, v, by_residue(q), by_residue(k), by_residue(v))


def _shifted(cur, prev8, next8, s, first, last):
    n = cur.shape[0]
    if s == 0:
        return cur
    rolled = pltpu.roll(cur, (-s) % n, 0)
    if s < 0:
        top = jnp.concatenate([jnp.where(first, 0.0, prev8), cur[0:8]], axis=0)[8 + s:16 + s]
        return jnp.concatenate([top, rolled[8:]], axis=0)
    bottom = jnp.concatenate([cur[n - 8:n], jnp.where(last, 0.0, next8)], axis=0)[s:8 + s]
    return jnp.concatenate([rolled[:n - 8], bottom], axis=0)


def _ssd_conv_kernel(c_ref, p_ref, n_ref, w_ref, b_ref, o_ref):
    first = pl.program_id(1) == 0
    last = pl.program_id(1) == pl.num_programs(1) - 1
    for c in range(SSD_XBC // 256):
        sl = slice(c * 256, (c + 1) * 256)
        cur, prev8, next8 = c_ref[0, :, sl], p_ref[0, :, sl], n_ref[0, :, sl]
        y = b_ref[:, sl]
        for j in range(SSD_CONV):
            y = y + _shifted(cur, prev8, next8, j - SSD_CONV // 2, first, last) * w_ref[j:j + 1, sl]
        o_ref[0, :, sl] = _silu(y)


def _halo_specs(tm, width, seq):
    cur = pl.BlockSpec((1, tm, width), lambda b, i: (b, i, 0))
    prev = pl.BlockSpec((1, 8, width), lambda b, i: (b, jnp.maximum(i * (tm // 8) - 1, 0), 0))
    nxt = pl.BlockSpec((1, 8, width), lambda b, i: (b, jnp.minimum((i + 1) * (tm // 8), seq // 8 - 1), 0))
    return cur, prev, nxt


def _ssd_conv(xbc, w, bias):
    b, seq, width = xbc.shape
    tm = 512
    cur, prev, nxt = _halo_specs(tm, width, seq)
    return pl.pallas_call(
        _ssd_conv_kernel, grid=(b, seq // tm),
        in_specs=[cur, prev, nxt, pl.BlockSpec((8, width), lambda b, i: (0, 0)),
                  pl.BlockSpec((1, width), lambda b, i: (0, 0))],
        out_specs=cur, out_shape=jax.ShapeDtypeStruct(xbc.shape, F32),
        compiler_params=_cparams(2), name="ssd_conv",
    )(xbc, xbc, xbc, w, bias)


def _softplus(x):
    return jnp.maximum(x, 0.0) + jnp.log1p(jnp.exp(-jnp.abs(x)))


def _tri(kind):
    s = lax.broadcasted_iota(I32, (CHUNK, CHUNK), 0)
    l = lax.broadcasted_iota(I32, (CHUNK, CHUNK), 1)
    return {"le": s <= l, "ge": s >= l, "lt": s < l}[kind]


def _expand(cols, e2_ref):
    hi, lo = _split2(cols)
    return _dot(jnp.concatenate([hi, lo], axis=1), e2_ref[...])


def _ssd_bwd_kernel(xs_ref, b_ref, dt_ref, pc_ref, e2_ref, sb_ref, st):
    @pl.when(pl.program_id(1) == 0)
    def _():
        st[...] = jnp.zeros_like(st)

    for c in reversed(range(SSD_CPS)):
        _ssd_bwd_chunk(slice(c * CHUNK, (c + 1) * CHUNK), c, xs_ref, b_ref, dt_ref, pc_ref, e2_ref, sb_ref, st)


def _ssd_bwd_chunk(rs, c, xs_ref, b_ref, dt_ref, pc_ref, e2_ref, sb_ref, st):
    sb_ref[0, c] = st[...].astype(BF16)
    dt_t = dt_ref[0, rs, :].T
    dtb = _softplus(dt_t[N_HEADS:2 * N_HEADS, :] + pc_ref[:, 1:2])
    a = dtb * pc_ref[:, 3:4]
    ex = _dot3(a, _tri("lt").astype(BF16))
    tot = ex[:, CHUNK - 1:CHUNK] + a[:, CHUNK - 1:CHUNK]
    rowform = jnp.concatenate([dtb * jnp.exp(ex), jnp.broadcast_to(jnp.exp(tot), (N_HEADS, CHUNK)),
                               jnp.zeros((CHUNK - 2 * N_HEADS, CHUNK), F32)], axis=0)
    ex2 = _expand(rowform.T, e2_ref)
    xw = (xs_ref[0, rs, :] * ex2[:, :D_MODEL]).astype(BF16)
    half = D_MODEL // SSD_GROUPS
    upd = [_dot(b_ref[0, rs, g * SSD_STATE:(g + 1) * SSD_STATE].T.astype(BF16), xw[:, g * half:(g + 1) * half])
           for g in range(SSD_GROUPS)]
    st[...] = st[...] * ex2[0:1, D_MODEL:] + jnp.concatenate(upd, axis=1)


def _ssd_fwd_kernel(xs_ref, b_ref, c_ref, dt_ref, z_ref, sb_ref, pc_ref, e3_ref, dexp_ref, on_ref, o_ref, st):
    @pl.when(pl.program_id(1) == 0)
    def _():
        st[...] = jnp.zeros_like(st)

    for c in range(SSD_CPS):
        _ssd_fwd_chunk(slice(c * CHUNK, (c + 1) * CHUNK), c, xs_ref, b_ref, c_ref, dt_ref, z_ref, sb_ref, pc_ref,
                       e3_ref, dexp_ref, on_ref, o_ref, st)


def _ssd_fwd_chunk(rs, c, xs_ref, b_ref, c_ref, dt_ref, z_ref, sb_ref, pc_ref, e3_ref, dexp_ref, on_ref, o_ref, st):
    xs = xs_ref[0, rs, :]
    dt_t = dt_ref[0, rs, :].T
    dtf = _softplus(dt_t[0:N_HEADS, :] + pc_ref[:, 0:1])
    dtb = _softplus(dt_t[N_HEADS:2 * N_HEADS, :] + pc_ref[:, 1:2])
    af = dtf * pc_ref[:, 2:3]
    ab = dtb * pc_ref[:, 3:4]
    csf = _dot3(af, _tri("le").astype(BF16))
    rcs = _dot3(ab, _tri("ge").astype(BF16))
    totf = csf[:, CHUNK - 1:CHUNK]
    rowform = jnp.concatenate([dtf * jnp.exp(totf - csf), jnp.exp(csf), jnp.exp(rcs), csf, rcs,
                               jnp.zeros((CHUNK - 5 * N_HEADS, CHUNK), F32)], axis=0)
    cols = rowform.T
    ex3 = _expand(cols, e3_ref)
    w_state, e_f, e_b = ex3[:, :D_MODEL], ex3[:, D_MODEL:2 * D_MODEL], ex3[:, 2 * D_MODEL:]

    xb = xs.astype(BF16)
    lower, upper = _tri("ge"), _tri("le")
    head_a = lax.broadcasted_iota(I32, (CHUNK, LANE), 1) < HEAD_DIM
    half = D_MODEL // SSD_GROUPS
    hpg = N_HEADS // SSD_GROUPS
    st_all = st[...]
    sb_all = sb_ref[0, c]
    ys = []
    b_t = []
    for g in range(SSD_GROUPS):
        bg = b_ref[0, rs, g * SSD_STATE:(g + 1) * SSD_STATE]
        cg = c_ref[0, rs, g * SSD_STATE:(g + 1) * SSD_STATE].astype(BF16)
        b_t.append(bg.T.astype(BF16))
        gm = _dot_nt(cg, bg.astype(BF16))
        states = jnp.concatenate([st_all[:, g * half:(g + 1) * half].astype(BF16),
                                  sb_all[:, g * half:(g + 1) * half]], axis=1)
        off = _dot(cg, states)
        y_off = (off[:, :half] * e_f[:, g * half:(g + 1) * half]
                 + off[:, half:] * e_b[:, g * half:(g + 1) * half])
        for pair in range(hpg // 2):
            ms = []
            for h in (g * hpg + 2 * pair, g * hpg + 2 * pair + 1):
                dec_f = jnp.where(lower, jnp.exp(cols[:, 3 * N_HEADS + h:3 * N_HEADS + h + 1] - csf[h:h + 1, :]), 0.0)
                dec_b = jnp.where(upper, jnp.exp(cols[:, 4 * N_HEADS + h:4 * N_HEADS + h + 1] - rcs[h:h + 1, :]), 0.0)
                ms.append((gm * (dec_f * dtf[h:h + 1, :] + dec_b * dtb[h:h + 1, :])).astype(BF16))
            lo = g * half + pair * LANE
            xp = xb[:, lo:lo + LANE]
            ys.append(jnp.where(head_a, _dot(ms[0], xp), _dot(ms[1], xp))
                      + y_off[:, pair * LANE:(pair + 1) * LANE])
    y = jnp.concatenate(ys, axis=1) + xs * dexp_ref[...]
    yz = y * _silu(z_ref[0, rs, :])
    o_ref[0, rs, :] = _rms(yz, on_ref[...]).astype(BF16)

    xw = (xs * w_state).astype(BF16)
    upd = [_dot(b_t[g], xw[:, g * half:(g + 1) * half]) for g in range(SSD_GROUPS)]
    st[...] = st_all * e_f[CHUNK - 1:CHUNK, :] + jnp.concatenate(upd, axis=1)


def _ssd(xbc_act, dt, z, pc, dexp, out_norm):
    b, seq, _ = xbc_act.shape
    nc = seq // CHUNK
    sel = np.zeros((2 * CHUNK, 3 * D_MODEL), np.float32)
    for part in range(3):
        for h in range(N_HEADS):
            for rep in range(2):
                sel[rep * CHUNK + part * N_HEADS + h, part * D_MODEL + h * HEAD_DIM:part * D_MODEL + (h + 1) * HEAD_DIM] = 1.0
    e3 = jnp.asarray(sel, BF16)
    e2 = jnp.asarray(sel[:, :2 * D_MODEL], BF16)

    rows = SSD_CPS * CHUNK
    nblk = seq // rows
    bcol = D_MODEL // (2 * SSD_STATE)
    rev = lambda bi, c: (bi, nblk - 1 - c, 0)
    sb = pl.pallas_call(
        _ssd_bwd_kernel, grid=(b, nblk),
        in_specs=[pl.BlockSpec((1, rows, D_MODEL), rev),
                  pl.BlockSpec((1, rows, 2 * SSD_STATE), lambda bi, c: (bi, nblk - 1 - c, bcol)),
                  pl.BlockSpec((1, rows, LANE), rev), _const_spec(pc.shape), _const_spec(e2.shape)],
        out_specs=pl.BlockSpec((1, SSD_CPS, SSD_STATE, D_MODEL), lambda bi, c: (bi, nblk - 1 - c, 0, 0)),
        out_shape=jax.ShapeDtypeStruct((b, nc, SSD_STATE, D_MODEL), BF16),
        scratch_shapes=[pltpu.VMEM((SSD_STATE, D_MODEL), F32)],
        compiler_params=_cparams(2), name="ssd_bwd_state",
    )(xbc_act, xbc_act, dt, pc, e2)

    fwd = lambda bi, c: (bi, c, 0)
    return pl.pallas_call(
        _ssd_fwd_kernel, grid=(b, nblk),
        in_specs=[pl.BlockSpec((1, rows, D_MODEL), fwd),
                  pl.BlockSpec((1, rows, 2 * SSD_STATE), lambda bi, c: (bi, c, bcol)),
                  pl.BlockSpec((1, rows, 2 * SSD_STATE), lambda bi, c: (bi, c, bcol + 1)),
                  pl.BlockSpec((1, rows, LANE), fwd), pl.BlockSpec((1, rows, D_MODEL), fwd),
                  pl.BlockSpec((1, SSD_CPS, SSD_STATE, D_MODEL), lambda bi, c: (bi, c, 0, 0)),
                  _const_spec(pc.shape), _const_spec(e3.shape), _const_spec((1, D_MODEL)), _const_spec((1, D_MODEL))],
        out_specs=pl.BlockSpec((1, rows, D_MODEL), fwd),
        out_shape=jax.ShapeDtypeStruct((b, seq, D_MODEL), BF16),
        scratch_shapes=[pltpu.VMEM((SSD_STATE, D_MODEL), F32)],
        compiler_params=_cparams(2), name="ssd_fwd",
    )(xbc_act, xbc_act, xbc_act, dt, z, sb, pc, e3, dexp, out_norm)


def _norm_and_route(x1, fg_ref, wr_hi_ref, wr_lo_ref, hn_ref, aff_ref):
    hn = _rms(x1, fg_ref[...])
    hi, lo = _split2(hn)
    hn_ref[...] = hi
    logits = _dot(hi, wr_hi_ref[...]) + _dot(lo, wr_hi_ref[...]) + _dot(hi, wr_lo_ref[...])
    lt = logits.T[0:N_EXPERTS, :]
    e = jnp.exp(lt - jnp.max(lt, axis=0, keepdims=True))
    aff_ref[0] = e / jnp.sum(e, axis=0, keepdims=True)


def _out_even_kernel(x_ref, a_ref, y_ref, wa_ref, wy_ref, fg_ref, wr_hi_ref, wr_lo_ref, x1_ref, hn_ref, aff_ref):
    x1 = x_ref[...] + _dot(a_ref[...], wa_ref[...]) + _dot(y_ref[...], wy_ref[...])
    x1_ref[...] = x1
    _norm_and_route(x1, fg_ref, wr_hi_ref, wr_lo_ref, hn_ref, aff_ref)


def _odd_kernel(x_ref, xp_ref, xn_ref, g_ref, wb_ref, wc_ref, wu_ref, cw_ref, wo_ref, fg_ref, wr_hi_ref, wr_lo_ref,
                x1_ref, hn_ref, aff_ref, *, tiles_per_seq):
    i = pl.program_id(0) % tiles_per_seq
    first, last = i == 0, i == tiles_per_seq - 1
    x = x_ref[...]
    tm = x.shape[0]
    n = tm + 16
    hb = _rms(jnp.concatenate([xp_ref[...], x, xn_ref[...]], axis=0), g_ref[...]).astype(BF16)
    cu = _dot(hb, wc_ref[...]) * _dot(hb, wu_ref[...])
    rows = lax.broadcasted_iota(I32, (n, 1), 0)
    cu = jnp.where((first & (rows < 8)) | (last & (rows >= tm + 8)), 0.0, cu)
    conv = sum(pltpu.roll(cu, (SHORT_CONV // 2 - j) % n, 0)[8:8 + tm] * cw_ref[j:j + 1, :] for j in range(SHORT_CONV))
    gb = _dot(hb[8:8 + tm], wb_ref[...])
    x1 = x + _dot((gb * conv).astype(BF16), wo_ref[...])
    x1_ref[...] = x1
    _norm_and_route(x1, fg_ref, wr_hi_ref, wr_lo_ref, hn_ref, aff_ref)


def _odd(x, g, wb, wc, wu, cw, wo, fg, wr_hi, wr_lo, b, seq):
    n = x.shape[0]
    tm = OUT_ROW_TILE
    row = pl.BlockSpec((tm, D_MODEL), lambda i: (i, 0))
    prev = pl.BlockSpec((8, D_MODEL), lambda i: (jnp.maximum(i * (tm // 8) - 1, 0), 0))
    nxt = pl.BlockSpec((8, D_MODEL), lambda i: (jnp.minimum((i + 1) * (tm // 8), n // 8 - 1), 0))
    out_specs, shapes = _route_outs(n, b, seq, tm)
    return pl.pallas_call(
        functools.partial(_odd_kernel, tiles_per_seq=seq // tm), grid=(n // tm,),
        in_specs=[row, prev, nxt, _const_spec((1, D_MODEL)), _const_spec(wb.shape), _const_spec(wc.shape),
                  _const_spec(wu.shape), pl.BlockSpec((8, D_MODEL), lambda i: (0, 0)), _const_spec(wo.shape),
                  _const_spec((1, D_MODEL)), _const_spec(wr_hi.shape), _const_spec(wr_lo.shape)],
        out_specs=out_specs, out_shape=shapes, compiler_params=_cparams(1), name="odd_mixer",
    )(x, x, x, g, wb, wc, wu, cw, wo, fg, wr_hi, wr_lo)


def _route_outs(n, b, seq, tm):
    row = pl.BlockSpec((tm, D_MODEL), lambda i: (i, 0))
    aff = pl.BlockSpec((1, N_EXPERTS, tm), lambda i: (i // (seq // tm), 0, i % (seq // tm)))
    shapes = [jax.ShapeDtypeStruct((n, D_MODEL), F32), jax.ShapeDtypeStruct((n, D_MODEL), BF16),
              jax.ShapeDtypeStruct((b, N_EXPERTS, seq), F32)]
    return [row, row, aff], shapes


def _out_even(x, attn, y, wa, wy, fg, wr_hi, wr_lo, b, seq):
    n = x.shape[0]
    tm = OUT_ROW_TILE
    row = pl.BlockSpec((tm, D_MODEL), lambda i: (i, 0))
    out_specs, shapes = _route_outs(n, b, seq, tm)
    return pl.pallas_call(
        _out_even_kernel, grid=(n // tm,),
        in_specs=[row, row, row, _const_spec(wa.shape), _const_spec(wy.shape), _const_spec((1, D_MODEL)),
                  _const_spec(wr_hi.shape), _const_spec(wr_lo.shape)],
        out_specs=out_specs, out_shape=shapes, compiler_params=_cparams(1), name="out_even",
    )(x, attn, y, wa, wy, fg, wr_hi, wr_lo)


def _count(mask):
    return jnp.sum(jnp.sum(mask.astype(F32), axis=0, keepdims=True), axis=1, keepdims=True)


def _route_kernel(aff_ref, incl_ref, ones_ref, strict_ref, local_ref, group_ref, first_ref, slot_ref, off_ref, end_ref,
                  *, cap):
    def step(i, thrs):
        bit = jnp.int32(1) << (30 - i)
        out = []
        for e in range(N_EXPERTS):
            cand = thrs[e] | bit
            out.append(jnp.where(_count(pltpu.bitcast(aff_ref[0, e], I32) >= cand) >= cap, cand, thrs[e]))
        return tuple(out)

    thrs = lax.fori_loop(0, 31, step, tuple(jnp.zeros((1, 1), I32) for _ in range(N_EXPERTS)))
    for e in range(N_EXPERTS):
        bits = pltpu.bitcast(aff_ref[0, e], I32)
        thr = thrs[e]
        gt = bits > thr
        eq = (bits == thr).astype(BF16)
        eq_rank = _dot(eq, incl_ref[...]) + _dot(strict_ref[...], _dot(eq, ones_ref[...]).astype(BF16))
        sel = (gt | ((bits == thr) & (eq_rank <= cap - _count(gt)))).astype(BF16)
        within = _dot(sel, incl_ref[...])
        totals = _dot(sel, ones_ref[...]).astype(BF16)
        local = _dot(local_ref[...], totals)
        cnt = _dot(group_ref[...], totals)
        padded = jnp.floor((cnt + (SLOT_ALIGN - 1)) * (1.0 / SLOT_ALIGN)) * SLOT_ALIGN
        start = _dot(first_ref[...], padded.astype(BF16))
        slot_ref[0, e] = jnp.where(sel > 0, (start + local + within).astype(I32) - 1, -1)
        off_ref[0, e:e + 1, :] = start.T[0:1, :].astype(I32)
        end_ref[0, e:e + 1, :] = (start + padded).T[0:1, :].astype(I32)


def _route(aff, cap):
    b, _, seq = aff.shape
    nt = seq // LANE
    tri = np.arange(LANE)
    tt = np.arange(nt)
    grp = tt // SUB
    as_bf16 = lambda m: jnp.asarray(m, BF16)
    incl = as_bf16(tri[:, None] <= tri[None, :])
    strict = as_bf16(tt[None, :] < tt[:, None])
    local = as_bf16((tt[None, :] < tt[:, None]) & (grp[None, :] == grp[:, None]))
    group = as_bf16(grp[None, :] == grp[:, None])
    first = as_bf16((grp[None, :] < grp[:, None]) & (tt[None, :] % SUB == 0))
    ones = jnp.ones((LANE, LANE), BF16)
    tiles = pl.BlockSpec((1, N_EXPERTS, nt, LANE), lambda i: (i, 0, 0, 0))
    rows = pl.BlockSpec((1, N_EXPERTS, nt), lambda i: (i, 0, 0))
    return pl.pallas_call(
        functools.partial(_route_kernel, cap=cap), grid=(b,),
        in_specs=[tiles, _const_spec((LANE, LANE)), _const_spec((LANE, LANE))] + [_const_spec((nt, nt))] * 4,
        out_specs=[tiles, rows, rows],
        out_shape=[jax.ShapeDtypeStruct((b, N_EXPERTS, nt, LANE), I32)] + [jax.ShapeDtypeStruct((b, N_EXPERTS, nt), I32)] * 2,
        compiler_params=_cparams(1), name="route",
    )(aff.reshape(b, N_EXPERTS, nt, LANE), incl, ones, strict, local, group, first)


def _one_hot(slots, base):
    return (slots == lax.broadcasted_iota(I32, (WINDOW, slots.shape[1]), 0) + base).astype(BF16)


def _gather_kernel(off_ref, end_ref, slot_ref, aff_ref, hn_ref, xe_hbm, stage, extra, sem, xsem):
    bi, j = pl.program_id(0), pl.program_id(1)
    n_j = pl.num_programs(1)
    step = bi * n_j + j
    cur = step % 2

    def window_copy(bb, jj, e, buf):
        start = pl.multiple_of(off_ref[bb, e, jj * SUB], SLOT_ALIGN)
        return pltpu.make_async_copy(stage.at[buf, e], xe_hbm.at[bb, e, pl.ds(start, WINDOW), :], sem.at[buf, e])

    tokens = hn_ref[0]
    parts = [p.astype(F32) for p in _split3(aff_ref[0, 0])]
    gates = jnp.concatenate(parts + [jnp.zeros((LANE - 3 * N_EXPERTS, TOK_TILE), F32)], axis=0).astype(BF16)

    def rows_of(p):
        return jnp.concatenate([_dot(p, tokens), _dot_nt(p, gates)], axis=1).astype(BF16)

    p_all = jnp.concatenate([_one_hot(slot_ref[0, 0, e:e + 1, :], off_ref[bi, e, j * SUB])
                             for e in range(N_EXPERTS)], axis=0)
    stage[cur] = rows_of(p_all).reshape(N_EXPERTS, WINDOW, D_MODEL + LANE)

    @pl.when(j > 0)
    def _():
        for e in range(N_EXPERTS):
            window_copy(bi, j - 1, e, 1 - cur).wait()

    for e in range(N_EXPERTS):
        window_copy(bi, j, e, cur).start()

    def overflow(e, c):
        first = off_ref[bi, e, j * SUB]
        n_win = (end_ref[bi, e, j * SUB] - first + WINDOW - 1) // WINDOW
        slots = slot_ref[0, 0, pl.ds(e, 1), :]

        def one(w, c2):
            base = pl.multiple_of(first + w * WINDOW, SLOT_ALIGN)
            extra[...] = rows_of(_one_hot(slots, base))
            cp = pltpu.make_async_copy(extra, xe_hbm.at[bi, e, pl.ds(base, WINDOW), :], xsem.at[0])
            cp.start()
            cp.wait()
            return c2

        return lax.fori_loop(1, n_win, one, c)

    lax.fori_loop(0, N_EXPERTS, overflow, 0)

    @pl.when(j == n_j - 1)
    def _():
        for e in range(N_EXPERTS):
            window_copy(bi, j, e, cur).wait()
        extra[...] = jnp.zeros_like(extra)
        cap_pad = xe_hbm.shape[2]

        def fill(e, c):
            used = end_ref[bi, e, end_ref.shape[2] - 1]
            n_big = (cap_pad - used) // WINDOW
            small0 = used + n_big * WINDOW
            n_small = (cap_pad - small0) // SLOT_ALIGN
            big = lambda i: pltpu.make_async_copy(
                extra, xe_hbm.at[bi, e, pl.ds(pl.multiple_of(used + i * WINDOW, SLOT_ALIGN), WINDOW), :], xsem.at[0])
            small = lambda i: pltpu.make_async_copy(
                extra.at[0:SLOT_ALIGN],
                xe_hbm.at[bi, e, pl.ds(pl.multiple_of(small0 + i * SLOT_ALIGN, SLOT_ALIGN), SLOT_ALIGN), :], xsem.at[0])
            for n, mk in ((n_big, big), (n_small, small)):
                lax.fori_loop(0, n, lambda i, c2, mk=mk: (mk(i).start(), c2)[1], 0)
            for n, mk in ((n_big, big), (n_small, small)):
                lax.fori_loop(0, n, lambda i, c2, mk=mk: (mk(i).wait(), c2)[1], 0)
            return c

        lax.fori_loop(0, N_EXPERTS, fill, 0)


def _gather(off, end, slot_t, aff_t, hn):
    b, seq, _ = hn.shape
    cap_pad = _cap_pad(seq)
    width = D_MODEL + LANE
    per_tile = pl.BlockSpec((1, 1, N_EXPERTS, TOK_TILE), lambda bi, j, *_: (bi, j, 0, 0))
    return pl.pallas_call(
        _gather_kernel,
        grid_spec=pltpu.PrefetchScalarGridSpec(
            num_scalar_prefetch=2, grid=(b, seq // TOK_TILE),
            in_specs=[per_tile, per_tile, pl.BlockSpec((1, TOK_TILE, D_MODEL), lambda bi, j, *_: (bi, j, 0))],
            out_specs=pl.BlockSpec(memory_space=pl.ANY),
            scratch_shapes=[pltpu.VMEM((2, N_EXPERTS, WINDOW, width), BF16), pltpu.VMEM((WINDOW, width), BF16),
                            pltpu.SemaphoreType.DMA((2, N_EXPERTS)), pltpu.SemaphoreType.DMA((1,))]),
        out_shape=jax.ShapeDtypeStruct((b, N_EXPERTS, cap_pad, width), BF16),
        compiler_params=_cparams(2), name="moe_gather",
    )(off, end, slot_t, aff_t, hn)


def _ffn_kernel(end_ref, xe_ref, wg_hbm, wu_hbm, wd_hbm, y_ref, stage_g, stage_u, stage_d, wg, wu, wd, sem, *, layer):
    e, bi, r = pl.program_id(0), pl.program_id(1), pl.program_id(2)
    used = end_ref[bi, e, end_ref.shape[2] - 1]
    pairs = ((wg_hbm, stage_g, wg), (wu_hbm, stage_u, wu), (wd_hbm, stage_d, wd))

    def weight_copies(ee):
        return [pltpu.make_async_copy(src.at[layer, ee], stg, sem.at[k]) for k, (src, stg, _) in enumerate(pairs)]

    @pl.when((bi == 0) & (r == 0))
    def _():
        @pl.when(e == 0)
        def _():
            for cp in weight_copies(e):
                cp.start()

        for cp, (_, stg, dst) in zip(weight_copies(e), pairs):
            cp.wait()
            n_rows = stg.shape[0]

            def cast(i, c, stg=stg, dst=dst):
                rows = pl.ds(pl.multiple_of(i * CAST_ROWS, CAST_ROWS), CAST_ROWS)
                dst[rows, :] = stg[rows, :].astype(BF16)
                return c

            lax.fori_loop(0, n_rows // CAST_ROWS, cast, 0)

        @pl.when(e + 1 < pl.num_programs(0))
        def _():
            for cp in weight_copies(e + 1):
                cp.start()

    @pl.when(r * FFN_ROWS < used)
    def _():
        xe = xe_ref[0, 0, :, 0:D_MODEL]
        hid = (_silu(_dot(xe, wg[...])) * _dot(xe, wu[...])).astype(BF16)
        g = xe_ref[0, 0, :, D_MODEL:].astype(F32)
        lane = lax.broadcasted_iota(I32, g.shape, 1)
        mine = (lane % N_EXPERTS == e) & (lane < 3 * N_EXPERTS)
        gate = jnp.sum(jnp.where(mine, g, 0.0), axis=1, keepdims=True)
        y_ref[0, 0] = (_dot(hid, wd[...]) * gate).astype(BF16)

    @pl.when(r * FFN_ROWS >= used)
    def _():
        y_ref[...] = jnp.zeros_like(y_ref)


def _ffn(end, xe, wg, wu, wd, layer):
    b, ne, cap_pad, width = xe.shape
    rows = lambda w: pl.BlockSpec((1, 1, FFN_ROWS, w), lambda e, bi, r, *_: (bi, e, r, 0))
    hbm = pl.BlockSpec(memory_space=pl.ANY)
    return pl.pallas_call(
        functools.partial(_ffn_kernel, layer=layer),
        grid_spec=pltpu.PrefetchScalarGridSpec(
            num_scalar_prefetch=1, grid=(ne, b, cap_pad // FFN_ROWS),
            in_specs=[rows(width), hbm, hbm, hbm],
            out_specs=rows(D_MODEL),
            scratch_shapes=[pltpu.VMEM((D_MODEL, D_FF), F32), pltpu.VMEM((D_MODEL, D_FF), F32),
                            pltpu.VMEM((D_FF, D_MODEL), F32), pltpu.VMEM((D_MODEL, D_FF), BF16),
                            pltpu.VMEM((D_MODEL, D_FF), BF16), pltpu.VMEM((D_FF, D_MODEL), BF16),
                            pltpu.SemaphoreType.DMA((3,))]),
        out_shape=jax.ShapeDtypeStruct((b, ne, cap_pad, D_MODEL), BF16),
        compiler_params=_cparams(3), name="moe_ffn",
    )(end, xe, wg, wu, wd)


def _combine_kernel(off_ref, end_ref, slot_ref, y_hbm, x1_ref, o_ref, win, extra, sem, xsem):
    bi, j = pl.program_id(0), pl.program_id(1)
    n_j = pl.num_programs(1)
    step = bi * n_j + j
    cur = step % 2

    def window_copy(bb, jj, e, buf):
        start = pl.multiple_of(off_ref[bb, e, jj * SUB], SLOT_ALIGN)
        return pltpu.make_async_copy(y_hbm.at[bb, e, pl.ds(start, WINDOW), :], win.at[buf, e], sem.at[buf, e])

    @pl.when(step == 0)
    def _():
        for e in range(N_EXPERTS):
            window_copy(bi, j, e, cur).start()

    @pl.when(step + 1 < pl.num_programs(0) * n_j)
    def _():
        nxt = step + 1
        for e in range(N_EXPERTS):
            window_copy(nxt // n_j, nxt % n_j, e, 1 - cur).start()

    ps = []
    for e in range(N_EXPERTS):
        window_copy(bi, j, e, cur).wait()
        ps.append(_one_hot(slot_ref[0, 0, e:e + 1, :], off_ref[bi, e, j * SUB]))
    p_all = jnp.concatenate(ps, axis=0)
    y_all = win[cur].reshape(N_EXPERTS * WINDOW, D_MODEL)
    o_ref[0] = x1_ref[0] + _dot_tn(p_all, y_all)

    def overflow(e, c):
        first = off_ref[bi, e, j * SUB]
        n_win = (end_ref[bi, e, j * SUB] - first + WINDOW - 1) // WINDOW
        slots = slot_ref[0, 0, pl.ds(e, 1), :]

        def one(w, c2):
            base = pl.multiple_of(first + w * WINDOW, SLOT_ALIGN)
            cp = pltpu.make_async_copy(y_hbm.at[bi, e, pl.ds(base, WINDOW), :], extra, xsem.at[0])
            cp.start()
            cp.wait()
            o_ref[0] += _dot_tn(_one_hot(slots, base), extra[...])
            return c2

        return lax.fori_loop(1, n_win, one, c)

    lax.fori_loop(0, N_EXPERTS, overflow, 0)


def _combine(off, end, slot_t, y, x1):
    b, seq, _ = x1.shape
    tile = pl.BlockSpec((1, TOK_TILE, D_MODEL), lambda bi, j, *_: (bi, j, 0))
    return pl.pallas_call(
        _combine_kernel,
        grid_spec=pltpu.PrefetchScalarGridSpec(
            num_scalar_prefetch=2, grid=(b, seq // TOK_TILE),
            in_specs=[pl.BlockSpec((1, 1, N_EXPERTS, TOK_TILE), lambda bi, j, *_: (bi, j, 0, 0)),
                      pl.BlockSpec(memory_space=pl.ANY), tile],
            out_specs=tile,
            scratch_shapes=[pltpu.VMEM((2, N_EXPERTS, WINDOW, D_MODEL), BF16), pltpu.VMEM((WINDOW, D_MODEL), BF16),
                            pltpu.SemaphoreType.DMA((2, N_EXPERTS)), pltpu.SemaphoreType.DMA((1,))]),
        out_shape=jax.ShapeDtypeStruct((b, seq, D_MODEL), F32),
        compiler_params=_cparams(2), name="moe_combine",
    )(off, end, slot_t, y, x1)


def _cap_pad(seq):
    cap = CAPACITY_FACTOR * seq // N_EXPERTS
    worst = cap + (seq // TOK_TILE) * (SLOT_ALIGN - 1) + WINDOW
    return -(-worst // FFN_ROWS) * FFN_ROWS


def _moe(x1, hn, aff, wg, wu, wd, layer):
    b, seq, _ = x1.shape
    cap = CAPACITY_FACTOR * seq // N_EXPERTS
    slot, off, end = _route(aff, cap)
    per_tile = lambda a: jnp.swapaxes(a.reshape(b, N_EXPERTS, seq // TOK_TILE, TOK_TILE), 1, 2)
    slot_t = per_tile(slot)
    xe = _gather(off, end, slot_t, per_tile(aff), hn)
    y = _ffn(end, xe, wg, wu, wd, layer)
    return _combine(off, end, slot_t, y, x1)


def _rope_tables(seq):
    inv_freq = ROPE_THETA ** (-jnp.arange(ROPE_HALF, dtype=F32) * 2.0 / (2 * ROPE_HALF))
    ang = jnp.arange(seq, dtype=F32)[:, None] * inv_freq[None, :]
    cos, sin = jnp.cos(ang), jnp.sin(ang)
    z = lambda w: jnp.zeros((seq, w), F32)
    rest = HEAD_DIM - 2 * ROPE_HALF
    rc = jnp.concatenate([cos, cos, jnp.ones((seq, rest), F32)], axis=1)
    r1 = jnp.concatenate([-sin, z(ROPE_HALF + rest)], axis=1)
    r2 = jnp.concatenate([z(ROPE_HALF), sin, z(rest)], axis=1)
    return tuple(jnp.tile(t, (1, LANE // HEAD_DIM)) for t in (rc, r1, r2))


def _router_split(w):
    wp = jnp.pad(w, ((0, 0), (0, LANE - N_EXPERTS)))
    hi = wp.astype(BF16)
    return hi, (wp - hi.astype(F32)).astype(BF16)


def kernel(x, attn_norm, w_in_even, q_norm, k_norm, ssd_conv_w, ssd_conv_b, ssd_a_log_fwd, ssd_a_log_bwd,
           ssd_dt_bias_fwd, ssd_dt_bias_bwd, ssd_d, ssd_out_norm, w_out_even, conv_norm, conv_w_in, conv_w,
           conv_w_out, ffn_norm, router_w, expert_w_gate, expert_w_up, expert_w_down):
    b, seq, _ = x.shape
    n = b * seq
    depth = ffn_norm.shape[0]
    rc, r1, r2 = _rope_tables(seq)
    blk = np.arange(256) // HEAD_DIM
    bd = jnp.asarray((blk[:, None] == blk[None, :]) / HEAD_DIM, BF16)
    row = lambda v: v.reshape(1, -1).astype(F32)

    xf = x.reshape(n, D_MODEL)
    for layer in range(depth):
        i = layer // 2
        wr_hi, wr_lo = _router_split(router_w[layer])
        fg = row(ffn_norm[layer])
        if layer % 2 == 0:
            w = w_in_even[i].astype(BF16)
            o = np.cumsum([0, D_MODEL, D_MODEL, D_MODEL, D_MODEL, SSD_XBC, N_HEADS, N_HEADS])
            wq, wk, wv, wz, wx = (w[:, o[j]:o[j + 1]] for j in range(5))
            wd = jnp.pad(w[:, o[5]:o[7]], ((0, 0), (0, LANE - 2 * N_HEADS)))
            tile_heads = lambda g: row(jnp.tile(g, N_HEADS))
            q, k, v, z, xbc, dt = _in_even(xf, row(attn_norm[i]), wq, wk, wv, wz, wx, wd,
                                           tile_heads(q_norm[i]), tile_heads(k_norm[i]), bd, rc, r1, r2, seq)
            as3 = lambda t: t.reshape(b, seq, -1)
            attn = _attention(as3(q), as3(k), as3(v))
            cw = jnp.pad(ssd_conv_w[i], ((0, 8 - SSD_CONV), (0, 0)))
            act = _ssd_conv(as3(xbc), cw, row(ssd_conv_b[i]))
            pc = jnp.pad(jnp.stack([ssd_dt_bias_fwd[i], ssd_dt_bias_bwd[i], -jnp.exp(ssd_a_log_fwd[i]),
                                    -jnp.exp(ssd_a_log_bwd[i])], axis=1).astype(F32), ((0, 0), (0, LANE - 4)))
            y = _ssd(act, as3(dt), as3(z), pc, row(jnp.repeat(ssd_d[i], HEAD_DIM)), row(ssd_out_norm[i]))
            wo = w_out_even[i].astype(BF16)
            x1, hn, aff = _out_even(xf, attn.reshape(n, D_MODEL), y.reshape(n, D_MODEL), wo[:D_MODEL], wo[D_MODEL:],
                                    fg, wr_hi, wr_lo, b, seq)
        else:
            w = conv_w_in[i].astype(BF16)
            cw = jnp.pad(conv_w[i], ((0, 8 - SHORT_CONV), (0, 0)))
            x1, hn, aff = _odd(xf, row(conv_norm[i]), w[:, :D_MODEL], w[:, D_MODEL:2 * D_MODEL], w[:, 2 * D_MODEL:],
                               cw, conv_w_out[i].astype(BF16), fg, wr_hi, wr_lo, b, seq)
        xf = _moe(x1.reshape(b, seq, D_MODEL), hn.reshape(b, seq, D_MODEL), aff, expert_w_gate, expert_w_up,
                  expert_w_down, layer).reshape(n, D_MODEL)
    return xf.reshape(b, seq, D_MODEL)
```
